```python
import jax, jax.numpy as jnp
from jax import lax
import numpy as np

D_MODEL = 1024
BATCH = 8
SEQ = 4096
DEPTH = 1

CHUNK = 64

CONV_GROUPS = 8
CONV_GROUP_DIM = 64
CONV_WIDTH = CONV_GROUPS * CONV_GROUP_DIM
CONV_K = 3
ATTN_HEADS = 8
HEAD_DIM = 64
ATTN_WIDTH = ATTN_HEADS * HEAD_DIM
MIX_WIDTH = CONV_WIDTH + ATTN_WIDTH
IN_COLS = 3 * CONV_WIDTH + 3 * ATTN_WIDTH
D_FF = 4 * D_MODEL
Q_BLOCK = 128
ALPHA = float((2 * DEPTH) ** 0.25)
BETA = float((8 * DEPTH) ** -0.25)
LN_EPS = 1e-5
RMS_EPS = 1e-6

kernel_name = "hybrid_shortconv_stickbreaking_deepnorm_block"


def layer_norm(x, g, b):
    xf = x.astype(jnp.float32)
    mu = jnp.mean(xf, axis=-1, keepdims=True)
    var = jnp.mean(jnp.square(xf - mu), axis=-1, keepdims=True)
    y = (xf - mu) * lax.rsqrt(var + LN_EPS)
    return (y * g.astype(jnp.float32) + b.astype(jnp.float32)).astype(x.dtype)


def group_rmsnorm(y, g, n_groups):
    bsz, seq, w = y.shape
    yf = y.astype(jnp.float32).reshape(bsz, seq, n_groups, w // n_groups)
    yf = yf * lax.rsqrt(jnp.mean(jnp.square(yf), axis=-1, keepdims=True) + RMS_EPS)
    return (yf.reshape(bsz, seq, w) * g.astype(jnp.float32)).astype(y.dtype)


def short_gated_conv(b_gate, c_gate, h, w_conv):
    seq = h.shape[1]
    u = c_gate * h
    u_pad = jnp.pad(u, ((0, 0), (CONV_K - 1, 0), (0, 0)))
    y = sum(w_conv[j] * u_pad[:, j:j + seq, :] for j in range(CONV_K))
    return b_gate * y


def stick_breaking_attention(q, k, v):
    bsz, seq, nh, dh = q.shape
    scale = dh ** -0.5
    qh = jnp.transpose(q, (0, 2, 1, 3))
    kh = jnp.transpose(k, (0, 2, 1, 3))
    vh = jnp.transpose(v, (0, 2, 1, 3))
    outs = []
    for i in range(seq // Q_BLOCK):
        q0 = i * Q_BLOCK
        kend = q0 + Q_BLOCK
        qb = qh[:, :, q0:kend]
        kb = kh[:, :, :kend]
        vb = vh[:, :, :kend]
        z = jnp.einsum('bhqd,bhkd->bhqk', qb, kb).astype(jnp.float32) * scale
        t_idx = q0 + jnp.arange(Q_BLOCK)[:, None]
        s_idx = jnp.arange(kend)[None, :]
        mask = s_idx < t_idx
        log_keep = jnp.where(mask, jax.nn.log_sigmoid(-z), 0.0)
        suffix = lax.cumsum(log_keep, axis=3, reverse=True) - log_keep
        a = jnp.where(mask, jnp.exp(jax.nn.log_sigmoid(z) + suffix), 0.0)
        outs.append(jnp.einsum('bhqk,bhkd->bhqd', a.astype(vb.dtype), vb))
    o = jnp.concatenate(outs, axis=2)
    return jnp.transpose(o, (0, 2, 1, 3)).reshape(bsz, seq, nh * dh)


def _fwd_setup_inputs(seed: int = 0) -> dict:
    key = jax.random.key(seed)
    ks = jax.random.split(key, 13)
    f32 = jnp.float32
    x = jax.random.normal(ks[0], (BATCH, SEQ, D_MODEL), f32)
    w_in = jax.random.normal(ks[1], (DEPTH, D_MODEL, IN_COLS), f32) * D_MODEL ** -0.5
    conv_w = jax.random.normal(ks[2], (DEPTH, CONV_K, CONV_WIDTH), f32) * CONV_K ** -0.5
    g_conv = 1.0 + 0.02 * jax.random.normal(ks[3], (DEPTH, CONV_WIDTH), f32)
    g_attn = 1.0 + 0.02 * jax.random.normal(ks[4], (DEPTH, ATTN_WIDTH), f32)
    w_out = jax.random.normal(ks[5], (DEPTH, MIX_WIDTH, D_MODEL), f32) * (MIX_WIDTH ** -0.5) * BETA
    ln1_g = 1.0 + 0.02 * jax.random.normal(ks[6], (DEPTH, D_MODEL), f32)
    ln1_b = 0.02 * jax.random.normal(ks[7], (DEPTH, D_MODEL), f32)
    w_up = jax.random.normal(ks[8], (DEPTH, D_MODEL, D_FF), f32) * D_MODEL ** -0.5
    w_down = jax.random.normal(ks[9], (DEPTH, D_FF, D_MODEL), f32) * (D_FF ** -0.5) * BETA
    ln2_g = 1.0 + 0.02 * jax.random.normal(ks[10], (DEPTH, D_MODEL), f32)
    ln2_b = 0.02 * jax.random.normal(ks[11], (DEPTH, D_MODEL), f32)
    return {"x": x, "w_in": w_in, "conv_w": conv_w, "g_conv": g_conv,
            "g_attn": g_attn, "w_out": w_out, "ln1_g": ln1_g, "ln1_b": ln1_b,
            "w_up": w_up, "w_down": w_down, "ln2_g": ln2_g, "ln2_b": ln2_b}


def _fwd_reference(x, w_in, conv_w, g_conv, g_attn, w_out, ln1_g, ln1_b,
              w_up, w_down, ln2_g, ln2_b):
    bsz, seq, _ = x.shape
    split_at = [CONV_WIDTH, 2 * CONV_WIDTH, 3 * CONV_WIDTH,
                3 * CONV_WIDTH + ATTN_WIDTH, 3 * CONV_WIDTH + 2 * ATTN_WIDTH]
    for l in range(DEPTH):
        proj = jnp.einsum('bsd,dc->bsc', x, w_in[l])
        b_gate, c_gate, h_conv, q, k, v = jnp.split(proj, split_at, axis=-1)
        y_conv = short_gated_conv(b_gate, c_gate, h_conv, conv_w[l])
        qh = q.reshape(bsz, seq, ATTN_HEADS, HEAD_DIM)
        kh = k.reshape(bsz, seq, ATTN_HEADS, HEAD_DIM)
        vh = v.reshape(bsz, seq, ATTN_HEADS, HEAD_DIM)
        y_attn = stick_breaking_attention(qh, kh, vh)
        y_mix = jnp.concatenate([group_rmsnorm(y_conv, g_conv[l], CONV_GROUPS),
                                 group_rmsnorm(y_attn, g_attn[l], ATTN_HEADS)], axis=-1)
        mix_out = jnp.einsum('bsc,cd->bsd', y_mix, w_out[l])
        x = layer_norm(ALPHA * x + mix_out, ln1_g[l], ln1_b[l])
        hid = jnp.square(jax.nn.relu(jnp.einsum('bsd,df->bsf', x, w_up[l])))
        ffn_out = jnp.einsum('bsf,fd->bsd', hid, w_down[l])
        x = layer_norm(ALPHA * x + ffn_out, ln2_g[l], ln2_b[l])
    return x


import jax as _jax
import jax.numpy as _jnp

TWIN_FORMAT = 'train_step'
FWD_PARAMS = ['x', 'w_in', 'conv_w', 'g_conv', 'g_attn', 'w_out', 'ln1_g', 'ln1_b', 'w_up', 'w_down', 'ln2_g', 'ln2_b']
TWIN_WEIGHTS = ['w_in', 'conv_w', 'g_conv', 'g_attn', 'w_out', 'ln1_g', 'ln1_b', 'w_up', 'w_down', 'ln2_g', 'ln2_b']
TWIN_DIFF_INPUT = 'x'
TWIN_INPUTS = ['x', 'w_in', 'conv_w', 'g_conv', 'g_attn', 'w_out', 'ln1_g', 'ln1_b', 'w_up', 'w_down', 'ln2_g', 'ln2_b', 'loss_target', 'm_w_in', 'm_conv_w', 'm_g_conv', 'm_g_attn', 'm_w_out', 'm_ln1_g', 'm_ln1_b', 'm_w_up', 'm_w_down', 'm_ln2_g', 'm_ln2_b', 'v_w_in', 'v_conv_w', 'v_g_conv', 'v_g_attn', 'v_w_out', 'v_ln1_g', 'v_ln1_b', 'v_w_up', 'v_w_down', 'v_ln2_g', 'v_ln2_b']
TWIN_OUTPUTS = ['loss', 'grad_x', 'grad_w_in', 'grad_conv_w', 'grad_g_conv', 'grad_g_attn', 'grad_w_out', 'grad_ln1_g', 'grad_ln1_b', 'grad_w_up', 'grad_w_down', 'grad_ln2_g', 'grad_ln2_b', 'delta_w_in', 'delta_conv_w', 'delta_g_conv', 'delta_g_attn', 'delta_w_out', 'delta_ln1_g', 'delta_ln1_b', 'delta_w_up', 'delta_w_down', 'delta_ln2_g', 'delta_ln2_b', 'new_m_w_in', 'new_m_conv_w', 'new_m_g_conv', 'new_m_g_attn', 'new_m_w_out', 'new_m_ln1_g', 'new_m_ln1_b', 'new_m_w_up', 'new_m_w_down', 'new_m_ln2_g', 'new_m_ln2_b', 'new_v_w_in', 'new_v_conv_w', 'new_v_g_conv', 'new_v_g_attn', 'new_v_w_out', 'new_v_ln1_g', 'new_v_ln1_b', 'new_v_w_up', 'new_v_w_down', 'new_v_ln2_g', 'new_v_ln2_b']
TWIN_LEAF_KINDS = {'loss': 'loss', 'grad_x': 'grad_x', 'grad_w_in': 'grad_w', 'grad_conv_w': 'grad_w', 'grad_g_conv': 'grad_w', 'grad_g_attn': 'grad_w', 'grad_w_out': 'grad_w', 'grad_ln1_g': 'grad_w', 'grad_ln1_b': 'grad_w', 'grad_w_up': 'grad_w', 'grad_w_down': 'grad_w', 'grad_ln2_g': 'grad_w', 'grad_ln2_b': 'grad_w', 'delta_w_in': 'delta_w', 'delta_conv_w': 'delta_w', 'delta_g_conv': 'delta_w', 'delta_g_attn': 'delta_w', 'delta_w_out': 'delta_w', 'delta_ln1_g': 'delta_w', 'delta_ln1_b': 'delta_w', 'delta_w_up': 'delta_w', 'delta_w_down': 'delta_w', 'delta_ln2_g': 'delta_w', 'delta_ln2_b': 'delta_w', 'new_m_w_in': 'new_m', 'new_m_conv_w': 'new_m', 'new_m_g_conv': 'new_m', 'new_m_g_attn': 'new_m', 'new_m_w_out': 'new_m', 'new_m_ln1_g': 'new_m', 'new_m_ln1_b': 'new_m', 'new_m_w_up': 'new_m', 'new_m_w_down': 'new_m', 'new_m_ln2_g': 'new_m', 'new_m_ln2_b': 'new_m', 'new_v_w_in': 'new_v', 'new_v_conv_w': 'new_v', 'new_v_g_conv': 'new_v', 'new_v_g_attn': 'new_v', 'new_v_w_out': 'new_v', 'new_v_ln1_g': 'new_v', 'new_v_ln1_b': 'new_v', 'new_v_w_up': 'new_v', 'new_v_w_down': 'new_v', 'new_v_ln2_g': 'new_v', 'new_v_ln2_b': 'new_v'}


def _forward(args):
    return _fwd_reference(*[args[k] for k in FWD_PARAMS])


def _output_shape():
    out = _jax.eval_shape(lambda: _forward(_fwd_setup_inputs(0)))
    return out.shape, out.dtype

N_MICROBATCH = 1
ADAM_LR = 0.001
ADAM_B1 = 0.9
ADAM_B2 = 0.999
ADAM_EPS = 1e-08
ADAM_WD = 0.01
ADAM_STEP = 10
PER_EXAMPLE_BATCH_AXIS = {'x': 0, 'loss_target': 0}
SHARED_INPUTS = []
_WEIGHT_DTYPES = {'w_in': _jnp.float32, 'conv_w': _jnp.float32, 'g_conv': _jnp.float32, 'g_attn': _jnp.float32, 'w_out': _jnp.float32, 'ln1_g': _jnp.float32, 'ln1_b': _jnp.float32, 'w_up': _jnp.float32, 'w_down': _jnp.float32, 'ln2_g': _jnp.float32, 'ln2_b': _jnp.float32}
MOMENT_SCALE = {'w_in': 7.394712e-02, 'conv_w': 9.054593e-02, 'g_conv': 9.095872e-02, 'g_attn': 8.853932e-02, 'w_out': 1.390734e-01, 'ln1_g': 7.211231e-01, 'ln1_b': 4.473229e-01, 'w_up': 5.349985e-02, 'w_down': 2.026268e-01, 'ln2_g': 3.211700e+01, 'ln2_b': 6.642086e+00}


def _to_microbatches(a, axis):
    t = _jnp.moveaxis(a, axis, 0)
    t = t.reshape((N_MICROBATCH, t.shape[0] // N_MICROBATCH) + t.shape[1:])
    return _jnp.moveaxis(t, 1, axis + 1)


def setup_inputs(seed: int = 0) -> dict:
    inp = _fwd_setup_inputs(seed)
    key = _jax.random.fold_in(_jax.random.key(seed), 7919)
    shape, _ = _output_shape()
    out = dict(inp)
    out["loss_target"] = _jax.random.normal(_jax.random.fold_in(key, 0), shape, _jnp.float32)
    for i, name in enumerate(TWIN_WEIGHTS):
        w = inp[name].astype(_jnp.float32)
        if MOMENT_SCALE is None:
            s = _jnp.sqrt(_jnp.mean(_jnp.square(w)) + 1e-30)
        else:
            s = MOMENT_SCALE[name]
        km, kv = _jax.random.split(_jax.random.fold_in(key, i + 1))
        out[name] = w
        out["m_" + name] = s * _jax.random.normal(km, w.shape, _jnp.float32)
        out["v_" + name] = (s * s) * _jax.random.uniform(kv, w.shape, _jnp.float32, 0.5, 1.5)
    if N_MICROBATCH > 1:
        for name, axis in PER_EXAMPLE_BATCH_AXIS.items():
            out[name] = _to_microbatches(out[name], axis)
    return {'x': out['x'], 'w_in': out['w_in'], 'conv_w': out['conv_w'], 'g_conv': out['g_conv'], 'g_attn': out['g_attn'], 'w_out': out['w_out'], 'ln1_g': out['ln1_g'], 'ln1_b': out['ln1_b'], 'w_up': out['w_up'], 'w_down': out['w_down'], 'ln2_g': out['ln2_g'], 'ln2_b': out['ln2_b'], 'loss_target': out['loss_target'], 'm_w_in': out['m_w_in'], 'm_conv_w': out['m_conv_w'], 'm_g_conv': out['m_g_conv'], 'm_g_attn': out['m_g_attn'], 'm_w_out': out['m_w_out'], 'm_ln1_g': out['m_ln1_g'], 'm_ln1_b': out['m_ln1_b'], 'm_w_up': out['m_w_up'], 'm_w_down': out['m_w_down'], 'm_ln2_g': out['m_ln2_g'], 'm_ln2_b': out['m_ln2_b'], 'v_w_in': out['v_w_in'], 'v_conv_w': out['v_conv_w'], 'v_g_conv': out['v_g_conv'], 'v_g_attn': out['v_g_attn'], 'v_w_out': out['v_w_out'], 'v_ln1_g': out['v_ln1_g'], 'v_ln1_b': out['v_ln1_b'], 'v_w_up': out['v_w_up'], 'v_w_down': out['v_w_down'], 'v_ln2_g': out['v_ln2_g'], 'v_ln2_b': out['v_ln2_b']}


def _loss(weights, diff, rest, loss_target):
    with _jax.named_scope("forward"):
        args = {**rest, TWIN_DIFF_INPUT: diff, **{k: w.astype(_WEIGHT_DTYPES[k]) for k, w in weights.items()}}
        y = _forward(args)
    with _jax.named_scope("loss_head"):
        err = _jnp.square(y.astype(_jnp.float32) - loss_target)
        return 0.5 * _jnp.sum(_jnp.mean(err, axis=-1)) if err.ndim else 0.5 * err


def _adamw(w, g, m, v):
    m = ADAM_B1 * m + (1.0 - ADAM_B1) * g
    v = ADAM_B2 * v + (1.0 - ADAM_B2) * _jnp.square(g)
    m_hat = m / (1.0 - ADAM_B1 ** ADAM_STEP)
    v_hat = v / (1.0 - ADAM_B2 ** ADAM_STEP)
    delta = -ADAM_LR * (m_hat / (_jnp.sqrt(v_hat) + ADAM_EPS) + ADAM_WD * w)
    return delta, m, v


def reference(x, w_in, conv_w, g_conv, g_attn, w_out, ln1_g, ln1_b, w_up, w_down, ln2_g, ln2_b, loss_target, m_w_in, m_conv_w, m_g_conv, m_g_attn, m_w_out, m_ln1_g, m_ln1_b, m_w_up, m_w_down, m_ln2_g, m_ln2_b, v_w_in, v_conv_w, v_g_conv, v_g_attn, v_w_out, v_ln1_g, v_ln1_b, v_w_up, v_w_down, v_ln2_g, v_ln2_b):
    given = dict(x=x, w_in=w_in, conv_w=conv_w, g_conv=g_conv, g_attn=g_attn, w_out=w_out, ln1_g=ln1_g, ln1_b=ln1_b, w_up=w_up, w_down=w_down, ln2_g=ln2_g, ln2_b=ln2_b, loss_target=loss_target, m_w_in=m_w_in, m_conv_w=m_conv_w, m_g_conv=m_g_conv, m_g_attn=m_g_attn, m_w_out=m_w_out, m_ln1_g=m_ln1_g, m_ln1_b=m_ln1_b, m_w_up=m_w_up, m_w_down=m_w_down, m_ln2_g=m_ln2_g, m_ln2_b=m_ln2_b, v_w_in=v_w_in, v_conv_w=v_conv_w, v_g_conv=v_g_conv, v_g_attn=v_g_attn, v_w_out=v_w_out, v_ln1_g=v_ln1_g, v_ln1_b=v_ln1_b, v_w_up=v_w_up, v_w_down=v_w_down, v_ln2_g=v_ln2_g, v_ln2_b=v_ln2_b)
    weights = {n: given[n] for n in TWIN_WEIGHTS}
    shared = {n: given[n] for n in SHARED_INPUTS}
    per_example = {n: given[n] for n in ['x']}
    grad_fn = _jax.value_and_grad(_loss, argnums=(0, 1))

    def one_microbatch(ex, loss_target):
        ex = dict(ex)
        diff = ex.pop(TWIN_DIFF_INPUT)
        return grad_fn(weights, diff, {**shared, **ex}, loss_target)

    if N_MICROBATCH == 1:
        loss, (grad_w, grad_x) = one_microbatch(per_example, given["loss_target"])
    else:
        def body(carry, xs):
            loss_sum, grad_sum = carry
            l_k, (gw_k, gx_k) = one_microbatch(xs[0], xs[1])
            with _jax.named_scope("update"):
                return (loss_sum + l_k, _jax.tree.map(_jnp.add, grad_sum, gw_k)), gx_k

        init = (_jnp.zeros((), _jnp.float32), _jax.tree.map(_jnp.zeros_like, weights))
        (loss, grad_w), grad_x = _jax.lax.scan(body, init, (per_example, given["loss_target"]))
    with _jax.named_scope("update"):
        delta_w, new_m, new_v = {}, {}, {}
        for n in TWIN_WEIGHTS:
            delta_w[n], new_m[n], new_v[n] = _adamw(weights[n], grad_w[n], given["m_" + n], given["v_" + n])
    return (loss, grad_x, *[grad_w[n] for n in TWIN_WEIGHTS], *[delta_w[n] for n in TWIN_WEIGHTS],
            *[new_m[n] for n in TWIN_WEIGHTS], *[new_v[n] for n in TWIN_WEIGHTS])
```

```python
import functools

import jax
import jax.numpy as jnp
from jax import lax
from jax.experimental import pallas as pl
from jax.experimental.pallas import tpu as pltpu

F32 = jnp.float32
BF16 = jnp.bfloat16
SDS = jax.ShapeDtypeStruct

D_MODEL = 1024
CONV_WIDTH = 512
ATTN_WIDTH = 512
GROUP = 64
GATE_COLS = 3 * CONV_WIDTH
QKV_COLS = 3 * ATTN_WIDTH
IN_COLS = GATE_COLS + QKV_COLS
D_FF = 4 * D_MODEL
N_CHIPS = 4
IN_SHARD = IN_COLS // N_CHIPS
FF_SHARD = D_FF // N_CHIPS
ALPHA = float(2.0 ** 0.25)
LN_EPS = 1e-5
RMS_EPS = 1e-6
ATTN_SCALE = GROUP ** -0.5
ADAM_LR = 0.001
ADAM_B1 = 0.9
ADAM_B2 = 0.999
ADAM_EPS = 1e-08
ADAM_WD = 0.01
ADAM_STEP = 10

LANES = 128
SUBLANES = 8
KEY_BLOCK = 128
VMEM_LIMIT = 56 * 1024 * 1024

MESH = pl.DeviceIdType.MESH
CHIP_FLIPS = ((1, 0), (0, 1), (1, 1))
DEVICE_FLIPS = tuple((fx, fy, fc) for fx in (0, 1) for fy in (0, 1) for fc in (0, 1))[1:]
NT_DIMS = (((1,), (1,)), ((), ()))
TN_DIMS = (((0,), (0,)), ((), ()))

ROW_LOSS = 0
ROW_GCONV = 8
ROW_GATTN = 12
ROW_LN1G = 16
ROW_LN1B = 24
ROW_LN2G = 32
ROW_LN2B = 40
ROW_CONVW = 48
SMALL_ROWS = 64
PARAM_ROWS = 48


def _params(sem=None):
    return pltpu.CompilerParams(dimension_semantics=sem, vmem_limit_bytes=VMEM_LIMIT)


def _flip(v, f):
    return 1 - v if f else v


def _position():
    return lax.axis_index("x"), lax.axis_index("y"), lax.axis_index("c")


def _hilo(v):
    hi = v.astype(BF16)
    lo = (v - hi.astype(F32)).astype(BF16)
    return jnp.concatenate([hi, lo], axis=1)


def _hilo_dot(v, mat):
    return jnp.dot(_hilo(v), mat, preferred_element_type=F32)


def _group_sum(v, gmat):
    parts = [_hilo_dot(v[:, LANES * j:LANES * (j + 1)], gmat) for j in range(v.shape[1] // LANES)]
    return parts[0] if len(parts) == 1 else jnp.concatenate(parts, axis=1)


def _softplus_terms(z):
    sp = jnp.log1p(jnp.exp(-jnp.abs(z)))
    log_beta = jnp.minimum(z, 0.0) - sp
    return log_beta, log_beta - z


def _layer_norm_fwd(pre, g, b):
    mu = jnp.mean(pre, axis=-1, keepdims=True)
    d = pre - mu
    var = jnp.mean(d * d, axis=-1, keepdims=True)
    rstd = lax.rsqrt(var + LN_EPS)
    xhat = d * rstd
    return xhat * g + b, xhat, rstd


def _layer_norm_bwd(dy, xhat, rstd, g):
    dxh = dy * g
    m1 = jnp.mean(dxh, axis=-1, keepdims=True)
    m2 = jnp.mean(dxh * xhat, axis=-1, keepdims=True)
    return rstd * (dxh - m1 - xhat * m2)


def _row_tile(s, want):
    return min(s, want)


def _cast_bf16(w, name):
    r, c = w.shape
    tr = _row_tile(r, 256)

    def body(w_ref, o_ref):
        o_ref[...] = w_ref[...].astype(BF16)

    return pl.pallas_call(
        body, name=name, grid=(r // tr,),
        in_specs=[pl.BlockSpec((tr, c), lambda i: (i, 0))],
        out_specs=pl.BlockSpec((tr, c), lambda i: (i, 0)),
        out_shape=SDS((r, c), BF16), compiler_params=_params(("parallel",)))(w)


def _proj(x, w_in):
    s = x.shape[0]
    tm = _row_tile(s, 512)

    def body(x_ref, w_ref, gates_ref, qkv_ref):
        xb = x_ref[...].astype(BF16)
        for k in range(N_CHIPS):
            acc = jnp.dot(xb, w_ref[k], preferred_element_type=F32)
            if k < 2:
                gates_ref[:, IN_SHARD * k:IN_SHARD * (k + 1)] = acc
            else:
                qkv_ref[:, IN_SHARD * (k - 2):IN_SHARD * (k - 1)] = acc.astype(BF16)

    return pl.pallas_call(
        body, name="proj", grid=(s // tm,),
        in_specs=[pl.BlockSpec((tm, D_MODEL), lambda i: (i, 0)),
                  pl.BlockSpec((N_CHIPS, D_MODEL, IN_SHARD), lambda i: (0, 0, 0))],
        out_specs=[pl.BlockSpec((tm, GATE_COLS), lambda i: (i, 0)),
                   pl.BlockSpec((tm, QKV_COLS), lambda i: (i, 0))],
        out_shape=[SDS((s, GATE_COLS), F32), SDS((s, QKV_COLS), BF16)],
        compiler_params=_params(("parallel",)))(x, w_in)


def _conv_forward_values(g_ref, halo_ref, taps_ref, first_block):
    gates = g_ref[...]
    tr = gates.shape[0]
    bg = gates[:, :CONV_WIDTH]
    cg = gates[:, CONV_WIDTH:2 * CONV_WIDTH]
    h = gates[:, 2 * CONV_WIDTH:]
    u = cg * h

    def prev(r):
        v = halo_ref[r:r + 1, CONV_WIDTH:2 * CONV_WIDTH] * halo_ref[r:r + 1, 2 * CONV_WIDTH:GATE_COLS]
        return jnp.where(first_block, 0.0, v)

    row = lax.broadcasted_iota(jnp.int32, (tr, CONV_WIDTH), 0)
    u1 = jnp.where(row == 0, prev(7), pltpu.roll(u, 1, 0))
    u2 = jnp.where(row == 0, prev(6), jnp.where(row == 1, prev(7), pltpu.roll(u, 2, 0)))
    y = taps_ref[0:1, :] * u2 + taps_ref[1:2, :] * u1 + taps_ref[2:3, :] * u
    return bg, cg, h, u, u1, u2, y


def _conv_fwd(gates, taps, g_conv, gmat):
    s = gates.shape[0]
    tr = _row_tile(s, 512)
    hb = tr // SUBLANES

    def body(g_ref, halo_ref, taps_ref, gain_ref, gmat_ref, out_ref):
        i = pl.program_id(0)
        bg, _, _, _, _, _, y = _conv_forward_values(g_ref, halo_ref, taps_ref, i == 0)
        yc = bg * y
        ms = _group_sum(yc * yc, gmat_ref[...]) * (1.0 / GROUP)
        out_ref[...] = (yc * lax.rsqrt(ms + RMS_EPS) * gain_ref[...]).astype(BF16)

    return pl.pallas_call(
        body, name="conv_fwd", grid=(s // tr,),
        in_specs=[pl.BlockSpec((tr, GATE_COLS), lambda i: (i, 0)),
                  pl.BlockSpec((SUBLANES, GATE_COLS), lambda i: (jnp.maximum(i * hb - 1, 0), 0)),
                  pl.BlockSpec((SUBLANES, CONV_WIDTH), lambda i: (0, 0)),
                  pl.BlockSpec((1, CONV_WIDTH), lambda i: (0, 0)),
                  pl.BlockSpec((2 * LANES, LANES), lambda i: (0, 0))],
        out_specs=pl.BlockSpec((tr, CONV_WIDTH), lambda i: (i, 0)),
        out_shape=SDS((s, CONV_WIDTH), BF16),
        compiler_params=_params(("parallel",)))(gates, gates, taps, g_conv, gmat)


def _conv_bwd_gate(gates, dycn, taps, g_conv, gmat):
    s = gates.shape[0]
    tr = _row_tile(s, 512)
    hb = tr // SUBLANES

    def body(g_ref, halo_ref, dn_ref, taps_ref, gain_ref, gmat_ref, dbg_ref, dy_ref, sums_ref):
        i = pl.program_id(0)
        bg, _, _, u, u1, u2, y = _conv_forward_values(g_ref, halo_ref, taps_ref, i == 0)
        gmat_v = gmat_ref[...]
        yc = bg * y
        rstd = lax.rsqrt(_group_sum(yc * yc, gmat_v) * (1.0 / GROUP) + RMS_EPS)
        n = yc * rstd
        dout = dn_ref[...]
        dn = dout * gain_ref[...]
        dyc = rstd * (dn - n * (_group_sum(dn * n, gmat_v) * (1.0 / GROUP)))
        dbg_ref[...] = (dyc * y).astype(BF16)
        dy = dyc * bg
        dy_ref[...] = dy

        @pl.when(i == 0)
        def _():
            sums_ref[...] = jnp.zeros_like(sums_ref)

        sums_ref[0:1, :] += jnp.sum(dy * u2, axis=0, keepdims=True)
        sums_ref[1:2, :] += jnp.sum(dy * u1, axis=0, keepdims=True)
        sums_ref[2:3, :] += jnp.sum(dy * u, axis=0, keepdims=True)
        sums_ref[3:4, :] += jnp.sum(dout * n, axis=0, keepdims=True)

    return pl.pallas_call(
        body, name="conv_bwd_gate", grid=(s // tr,),
        in_specs=[pl.BlockSpec((tr, GATE_COLS), lambda i: (i, 0)),
                  pl.BlockSpec((SUBLANES, GATE_COLS), lambda i: (jnp.maximum(i * hb - 1, 0), 0)),
                  pl.BlockSpec((tr, CONV_WIDTH), lambda i: (i, 0)),
                  pl.BlockSpec((SUBLANES, CONV_WIDTH), lambda i: (0, 0)),
                  pl.BlockSpec((1, CONV_WIDTH), lambda i: (0, 0)),
                  pl.BlockSpec((2 * LANES, LANES), lambda i: (0, 0))],
        out_specs=[pl.BlockSpec((tr, CONV_WIDTH), lambda i: (i, 0)),
                   pl.BlockSpec((tr, CONV_WIDTH), lambda i: (i, 0)),
                   pl.BlockSpec((SUBLANES, CONV_WIDTH), lambda i: (0, 0))],
        out_shape=[SDS((s, CONV_WIDTH), BF16), SDS((s, CONV_WIDTH), F32), SDS((SUBLANES, CONV_WIDTH), F32)],
        compiler_params=_params(("arbitrary",)))(gates, gates, dycn, taps, g_conv, gmat)


def _dproj_assemble(gates, dy, dbg, dq, dk, dv, taps):
    s = gates.shape[0]
    tr = _row_tile(s, 512)
    hb = tr // SUBLANES
    last = s // SUBLANES - 1
    n_blocks = s // tr

    def body(g_ref, dy_ref, halo_ref, dbg_ref, dq_ref, dk_ref, dv_ref, taps_ref, out_ref):
        i = pl.program_id(0)
        gates_v = g_ref[...]
        cg = gates_v[:, CONV_WIDTH:2 * CONV_WIDTH]
        h = gates_v[:, 2 * CONV_WIDTH:]
        dy_v = dy_ref[...]
        last_block = i == n_blocks - 1
        nxt = lambda r: jnp.where(last_block, 0.0, halo_ref[r:r + 1, :])
        row = lax.broadcasted_iota(jnp.int32, (tr, CONV_WIDTH), 0)
        d1 = jnp.where(row == tr - 1, nxt(0), pltpu.roll(dy_v, tr - 1, 0))
        d2 = jnp.where(row == tr - 1, nxt(1), jnp.where(row == tr - 2, nxt(0), pltpu.roll(dy_v, tr - 2, 0)))
        du = taps_ref[2:3, :] * dy_v + taps_ref[1:2, :] * d1 + taps_ref[0:1, :] * d2
        out_ref[:, 0:CONV_WIDTH] = dbg_ref[...]
        out_ref[:, CONV_WIDTH:2 * CONV_WIDTH] = (du * h).astype(BF16)
        out_ref[:, 2 * CONV_WIDTH:GATE_COLS] = (du * cg).astype(BF16)
        out_ref[:, GATE_COLS:GATE_COLS + ATTN_WIDTH] = dq_ref[...]
        out_ref[:, GATE_COLS + ATTN_WIDTH:GATE_COLS + 2 * ATTN_WIDTH] = dk_ref[...].astype(BF16)
        out_ref[:, GATE_COLS + 2 * ATTN_WIDTH:] = dv_ref[...].astype(BF16)

    row_spec = lambda w: pl.BlockSpec((tr, w), lambda i: (i, 0))
    return pl.pallas_call(
        body, name="dproj_assemble", grid=(s // tr,),
        in_specs=[row_spec(GATE_COLS), row_spec(CONV_WIDTH),
                  pl.BlockSpec((SUBLANES, CONV_WIDTH), lambda i: (jnp.minimum((i + 1) * hb, last), 0)),
                  row_spec(CONV_WIDTH), row_spec(ATTN_WIDTH), row_spec(ATTN_WIDTH), row_spec(ATTN_WIDTH),
                  pl.BlockSpec((SUBLANES, CONV_WIDTH), lambda i: (0, 0))],
        out_specs=row_spec(IN_COLS),
        out_shape=SDS((s, IN_COLS), BF16),
        compiler_params=_params(("parallel",)))(gates, dy, dy, dbg, dq, dk, dv, taps)


def _head_masks(v):
    lane = lax.broadcasted_iota(jnp.int32, (1, LANES), 1)
    zero = jnp.zeros_like(v)
    return jnp.where(lane < GROUP, v, zero), jnp.where(lane < GROUP, zero, v)


def _attn_fwd(qkv, g_attn, tri, gmat):
    s = qkv.shape[0]
    tq = _row_tile(s, 256)
    tk = KEY_BLOCK
    ratio = tq // tk
    pairs = ATTN_WIDTH // LANES

    def body(q_ref, k_ref, v_ref, gain_ref, tri_ref, gmat_ref, o_ref, yn_ref, tot_ref):
        i = pl.program_id(1)
        qs = _head_masks(q_ref[...])
        tri_v = tri_ref[...]
        row_t = lax.broadcasted_iota(jnp.int32, (tq, tk), 0)
        col_s = lax.broadcasted_iota(jnp.int32, (tq, tk), 1)

        def block(j, carry, diag):
            run = [carry[0], carry[1]]
            oacc = carry[2]
            s0 = pl.multiple_of(j * tk, tk)
            ks = k_ref[pl.ds(s0, tk), :]
            vsel = _head_masks(v_ref[pl.ds(s0, tk), :])
            for h in range(2):
                z = lax.dot_general(qs[h], ks, NT_DIMS, preferred_element_type=F32) * ATTN_SCALE
                log_beta, log_keep = _softplus_terms(z)
                if diag is not None:
                    valid = col_s + diag * tk < row_t
                    log_keep = jnp.where(valid, log_keep, 0.0)
                ct = _hilo_dot(log_keep, tri_v)
                a = jnp.exp(log_beta + ct[:, :tk] + run[h])
                run[h] = run[h] + ct[:, tk:]
                if diag is not None:
                    a = jnp.where(valid, a, 0.0)
                oacc = oacc + jnp.dot(a.astype(BF16), vsel[h], preferred_element_type=F32)
            return run[0], run[1], oacc

        carry = (jnp.zeros((tq, tk), F32), jnp.zeros((tq, tk), F32), jnp.zeros((tq, LANES), F32))
        for m in reversed(range(ratio)):
            carry = block(i * ratio + m, carry, m)
        n_full = i * ratio
        carry = lax.fori_loop(0, n_full, lambda it, c: block(n_full - 1 - it, c, None), carry)
        run_a, run_b, oacc = carry
        lane = lax.broadcasted_iota(jnp.int32, (1, LANES), 1)
        o_ref[...] = oacc
        tot_ref[...] = jnp.where(lane < GROUP, run_a, run_b)
        ms = _group_sum(oacc * oacc, gmat_ref[...]) * (1.0 / GROUP)
        yn_ref[...] = (oacc * lax.rsqrt(ms + RMS_EPS) * gain_ref[...]).astype(BF16)

    blk = lambda: pl.BlockSpec((tq, LANES), lambda p, i: (i, p))
    return pl.pallas_call(
        body, name="attn_fwd", grid=(pairs, s // tq),
        in_specs=[pl.BlockSpec((tq, LANES), lambda p, i: (i, p)),
                  pl.BlockSpec((s, LANES), lambda p, i: (0, pairs + p)),
                  pl.BlockSpec((s, LANES), lambda p, i: (0, 2 * pairs + p)),
                  pl.BlockSpec((1, LANES), lambda p, i: (0, p)),
                  pl.BlockSpec((2 * tk, 2 * tk), lambda p, i: (0, 0)),
                  pl.BlockSpec((2 * LANES, LANES), lambda p, i: (0, 0))],
        out_specs=[blk(), blk(), blk()],
        out_shape=[SDS((s, ATTN_WIDTH), F32), SDS((s, ATTN_WIDTH), BF16), SDS((s, ATTN_WIDTH), F32)],
        compiler_params=_params(("parallel", "parallel")))(qkv, qkv, qkv, g_attn, tri, gmat)


def _attn_bwd(qkv, o, tot, dyn, g_attn, tri, gmat):
    s = qkv.shape[0]
    tq = _row_tile(s, 256)
    tk = KEY_BLOCK
    ratio = tq // tk
    pairs = ATTN_WIDTH // LANES

    def body(q_ref, k_ref, v_ref, o_ref, tot_ref, dyn_ref, gain_ref, tri_ref, gmat_ref,
             dq_ref, dk_ref, dv_ref, dg_ref):
        i = pl.program_id(1)

        @pl.when(i == 0)
        def _():
            dk_ref[...] = jnp.zeros_like(dk_ref)
            dv_ref[...] = jnp.zeros_like(dv_ref)
            dg_ref[...] = jnp.zeros_like(dg_ref)

        gmat_v = gmat_ref[...]
        o_v = o_ref[...]
        rstd = lax.rsqrt(_group_sum(o_v * o_v, gmat_v) * (1.0 / GROUP) + RMS_EPS)
        n = o_v * rstd
        dout = dyn_ref[...]
        dg_ref[0:1, :] += jnp.sum(dout * n, axis=0, keepdims=True)
        dn = dout * gain_ref[...]
        do = rstd * (dn - n * (_group_sum(dn * n, gmat_v) * (1.0 / GROUP)))
        dos = _head_masks(do.astype(BF16))
        qs = _head_masks(q_ref[...])
        tot_v = tot_ref[...]
        tots = (jnp.broadcast_to(tot_v[:, 0:1], (tq, tk)), jnp.broadcast_to(tot_v[:, GROUP:GROUP + 1], (tq, tk)))
        tri_v = tri_ref[...]
        row_t = lax.broadcasted_iota(jnp.int32, (tq, tk), 0)
        col_s = lax.broadcasted_iota(jnp.int32, (tq, tk), 1)

        def block(j, carry, diag):
            pref_l = [carry[0], carry[1]]
            pref_g = [carry[2], carry[3]]
            dq = carry[4]
            s0 = pl.multiple_of(j * tk, tk)
            k2 = k_ref[pl.ds(s0, tk), :]
            v2 = v_ref[pl.ds(s0, tk), :]
            ksel = _head_masks(k2)
            dk_blk = jnp.zeros((tk, LANES), F32)
            dv_blk = jnp.zeros((tk, LANES), F32)
            for h in range(2):
                z = lax.dot_general(qs[h], k2, NT_DIMS, preferred_element_type=F32) * ATTN_SCALE
                log_beta, log_keep = _softplus_terms(z)
                if diag is not None:
                    valid = col_s + diag * tk < row_t
                    log_keep = jnp.where(valid, log_keep, 0.0)
                ct = _hilo_dot(log_keep, tri_v)
                suffix = tots[h] - pref_l[h] - ct[:, :tk] - log_keep
                pref_l[h] = pref_l[h] + ct[:, tk:]
                a = jnp.exp(log_beta + suffix)
                if diag is not None:
                    a = jnp.where(valid, a, 0.0)
                da = lax.dot_general(dos[h], v2, NT_DIMS, preferred_element_type=F32)
                g = a * da
                gt = _hilo_dot(g, tri_v)
                prefix = pref_g[h] + gt[:, :tk]
                pref_g[h] = pref_g[h] + gt[:, tk:]
                beta = jnp.exp(log_beta)
                dz = g * (1.0 - beta) - prefix * beta
                if diag is not None:
                    dz = jnp.where(valid, dz, 0.0)
                dz = dz * ATTN_SCALE
                dq = dq + jnp.dot(dz.astype(BF16), ksel[h], preferred_element_type=F32)
                dk_blk = dk_blk + jnp.dot(dz.T.astype(BF16), qs[h], preferred_element_type=F32)
                dv_blk = dv_blk + jnp.dot(a.T.astype(BF16), dos[h], preferred_element_type=F32)
            dk_ref[pl.ds(s0, tk), :] += dk_blk
            dv_ref[pl.ds(s0, tk), :] += dv_blk
            return pref_l[0], pref_l[1], pref_g[0], pref_g[1], dq

        zeros_qk = jnp.zeros((tq, tk), F32)
        carry = (zeros_qk, zeros_qk, zeros_qk, zeros_qk, jnp.zeros((tq, LANES), F32))
        carry = lax.fori_loop(0, i * ratio, lambda j, c: block(j, c, None), carry)
        for m in range(ratio):
            carry = block(i * ratio + m, carry, m)
        dq_ref[...] = carry[4].astype(BF16)

    blk = lambda: pl.BlockSpec((tq, LANES), lambda p, i: (i, p))
    col = lambda: pl.BlockSpec((s, LANES), lambda p, i: (0, p))
    return pl.pallas_call(
        body, name="attn_bwd", grid=(pairs, s // tq),
        in_specs=[pl.BlockSpec((tq, LANES), lambda p, i: (i, p)),
                  pl.BlockSpec((s, LANES), lambda p, i: (0, pairs + p)),
                  pl.BlockSpec((s, LANES), lambda p, i: (0, 2 * pairs + p)),
                  blk(), blk(), blk(),
                  pl.BlockSpec((1, LANES), lambda p, i: (0, p)),
                  pl.BlockSpec((2 * tk, 2 * tk), lambda p, i: (0, 0)),
                  pl.BlockSpec((2 * LANES, LANES), lambda p, i: (0, 0))],
        out_specs=[blk(), col(), col(), pl.BlockSpec((SUBLANES, LANES), lambda p, i: (0, p))],
        out_shape=[SDS((s, ATTN_WIDTH), BF16), SDS((s, ATTN_WIDTH), F32), SDS((s, ATTN_WIDTH), F32),
                   SDS((SUBLANES, ATTN_WIDTH), F32)],
        compiler_params=_params(("parallel", "arbitrary")))(qkv, qkv, qkv, o, tot, dyn, g_attn, tri, gmat)


def _mix_ln1(ycn, yan, w_out, x, g, b):
    s = x.shape[0]
    tm = _row_tile(s, 512)

    def body(yc_ref, ya_ref, w_ref, x_ref, g_ref, b_ref, x1_ref, xhat_ref, rstd_ref):
        mix = jnp.dot(yc_ref[...], w_ref[0:CONV_WIDTH, :], preferred_element_type=F32)
        mix = mix + jnp.dot(ya_ref[...], w_ref[CONV_WIDTH:, :], preferred_element_type=F32)
        x1, xhat, rstd = _layer_norm_fwd(ALPHA * x_ref[...] + mix, g_ref[...], b_ref[...])
        x1_ref[...] = x1
        xhat_ref[...] = xhat
        rstd_ref[...] = rstd

    row = lambda w: pl.BlockSpec((tm, w), lambda i: (i, 0))
    vec = lambda: pl.BlockSpec((1, D_MODEL), lambda i: (0, 0))
    return pl.pallas_call(
        body, name="mix_ln1", grid=(s // tm,),
        in_specs=[row(CONV_WIDTH), row(ATTN_WIDTH), pl.BlockSpec((D_MODEL, D_MODEL), lambda i: (0, 0)),
                  row(D_MODEL), vec(), vec()],
        out_specs=[row(D_MODEL), row(D_MODEL), row(1)],
        out_shape=[SDS((s, D_MODEL), F32), SDS((s, D_MODEL), F32), SDS((s, 1), F32)],
        compiler_params=_params(("parallel",)))(ycn, yan, w_out, x, g, b)


def _mlp_fwd_loss(x1, w_up, w_down, target, g, b):
    s = x1.shape[0]
    tm = _row_tile(s, 256)

    def body(x1_ref, wu_ref, wd_ref, t_ref, g_ref, b_ref, dpre_ref, sums_ref, loss_ref):
        i = pl.program_id(0)
        x1_v = x1_ref[...]
        xb = x1_v.astype(BF16)
        ffn = jnp.zeros((tm, D_MODEL), F32)
        for k in range(N_CHIPS):
            r = jnp.maximum(jnp.dot(xb, wu_ref[k], preferred_element_type=F32), 0.0)
            ffn = ffn + jnp.dot((r * r).astype(BF16), wd_ref[k], preferred_element_type=F32)
        g_v = g_ref[...]
        x2, xhat, rstd = _layer_norm_fwd(ALPHA * x1_v + ffn, g_v, b_ref[...])
        err = x2 - t_ref[...]
        dx2 = err * (1.0 / D_MODEL)
        dpre_ref[...] = _layer_norm_bwd(dx2, xhat, rstd, g_v)

        @pl.when(i == 0)
        def _():
            sums_ref[...] = jnp.zeros_like(sums_ref)
            loss_ref[...] = jnp.zeros_like(loss_ref)

        sums_ref[0:1, :] += jnp.sum(dx2 * xhat, axis=0, keepdims=True)
        sums_ref[1:2, :] += jnp.sum(dx2, axis=0, keepdims=True)
        loss_ref[...] += jnp.sum(jnp.sum(err * err, axis=1, keepdims=True), axis=0, keepdims=True) * (0.5 / D_MODEL)

    row = lambda: pl.BlockSpec((tm, D_MODEL), lambda i: (i, 0))
    vec = lambda: pl.BlockSpec((1, D_MODEL), lambda i: (0, 0))
    wspec = lambda: pl.BlockSpec((N_CHIPS, D_MODEL, FF_SHARD), lambda i: (0, 0, 0))
    return pl.pallas_call(
        body, name="mlp_fwd_loss", grid=(s // tm,),
        in_specs=[row(), wspec(), wspec(), row(), vec(), vec()],
        out_specs=[row(), pl.BlockSpec((SUBLANES, D_MODEL), lambda i: (0, 0)),
                   pl.BlockSpec((SUBLANES, LANES), lambda i: (0, 0))],
        out_shape=[SDS((s, D_MODEL), F32), SDS((SUBLANES, D_MODEL), F32), SDS((SUBLANES, LANES), F32)],
        compiler_params=_params(("arbitrary",)))(x1, w_up, w_down, target, g, b)


def _mlp_bwd_ln1(x1, dpre2, w_up, w_down, xhat1, rstd1, g1):
    s = x1.shape[0]
    tm = _row_tile(s, 256)

    def body(x1_ref, d2_ref, wu_ref, wd_ref, xh_ref, rs_ref, g_ref, hid_ref, dup_ref, dpre_ref, sums_ref):
        i = pl.program_id(0)
        xb = x1_ref[...].astype(BF16)
        d2 = d2_ref[...]
        d2b = d2.astype(BF16)
        dx1 = ALPHA * d2
        for k in range(N_CHIPS):
            r = jnp.maximum(jnp.dot(xb, wu_ref[k], preferred_element_type=F32), 0.0)
            hid_ref[:, FF_SHARD * k:FF_SHARD * (k + 1)] = (r * r).astype(BF16)
            dhid = lax.dot_general(d2b, wd_ref[k], NT_DIMS, preferred_element_type=F32)
            dupb = (dhid * (2.0 * r)).astype(BF16)
            dup_ref[:, FF_SHARD * k:FF_SHARD * (k + 1)] = dupb
            dx1 = dx1 + lax.dot_general(dupb, wu_ref[k], NT_DIMS, preferred_element_type=F32)
        xhat = xh_ref[...]
        dpre_ref[...] = _layer_norm_bwd(dx1, xhat, rs_ref[...], g_ref[...])

        @pl.when(i == 0)
        def _():
            sums_ref[...] = jnp.zeros_like(sums_ref)

        sums_ref[0:1, :] += jnp.sum(dx1 * xhat, axis=0, keepdims=True)
        sums_ref[1:2, :] += jnp.sum(dx1, axis=0, keepdims=True)

    row = lambda w: pl.BlockSpec((tm, w), lambda i: (i, 0))
    wspec = lambda: pl.BlockSpec((N_CHIPS, D_MODEL, FF_SHARD), lambda i: (0, 0, 0))
    return pl.pallas_call(
        body, name="mlp_bwd_ln1", grid=(s // tm,),
        in_specs=[row(D_MODEL), row(D_MODEL), wspec(), wspec(), row(D_MODEL), row(1),
                  pl.BlockSpec((1, D_MODEL), lambda i: (0, 0))],
        out_specs=[row(D_FF), row(D_FF), row(D_MODEL), pl.BlockSpec((SUBLANES, D_MODEL), lambda i: (0, 0))],
        out_shape=[SDS((s, D_FF), BF16), SDS((s, D_FF), BF16), SDS((s, D_MODEL), F32),
                   SDS((SUBLANES, D_MODEL), F32)],
        compiler_params=_params(("arbitrary",)))(x1, dpre2, w_up, w_down, xhat1, rstd1, g1)


def _grad_tn(a, b, name, out_cols, stacked):
    s, ka = a.shape
    n = b.shape[1]
    ts = _row_tile(s, 512)
    if stacked:
        tka, tn = ka, out_cols
        grid = (1, n // tn, s // ts)
        out_shape = SDS((n // tn, ka, tn), F32)
        out_spec = pl.BlockSpec((None, tka, tn), lambda r, c, t: (c, 0, 0))
    else:
        tka, tn = min(ka, 1024), n
        grid = (ka // tka, 1, s // ts)
        out_shape = SDS((ka, n), F32)
        out_spec = pl.BlockSpec((tka, tn), lambda r, c, t: (r, 0))

    def body(a_ref, b_ref, o_ref):
        @pl.when(pl.program_id(2) == 0)
        def _():
            o_ref[...] = jnp.zeros_like(o_ref)

        o_ref[...] += lax.dot_general(a_ref[...].astype(BF16), b_ref[...].astype(BF16), TN_DIMS,
                                      preferred_element_type=F32)

    return pl.pallas_call(
        body, name=name, grid=grid,
        in_specs=[pl.BlockSpec((ts, tka), lambda r, c, t: (t, r)),
                  pl.BlockSpec((ts, tn), lambda r, c, t: (t, c))],
        out_specs=out_spec, out_shape=out_shape,
        compiler_params=_params(("parallel", "parallel", "arbitrary")))(a, b)


def _dmix(dpre1, w_out):
    s = dpre1.shape[0]
    tm = _row_tile(s, 512)

    def body(d_ref, w_ref, dc_ref, da_ref):
        db = d_ref[...].astype(BF16)
        dc_ref[...] = lax.dot_general(db, w_ref[0:CONV_WIDTH, :], NT_DIMS, preferred_element_type=F32)
        da_ref[...] = lax.dot_general(db, w_ref[CONV_WIDTH:, :], NT_DIMS, preferred_element_type=F32)

    return pl.pallas_call(
        body, name="dmix", grid=(s // tm,),
        in_specs=[pl.BlockSpec((tm, D_MODEL), lambda i: (i, 0)),
                  pl.BlockSpec((D_MODEL, D_MODEL), lambda i: (0, 0))],
        out_specs=[pl.BlockSpec((tm, CONV_WIDTH), lambda i: (i, 0)),
                   pl.BlockSpec((tm, ATTN_WIDTH), lambda i: (i, 0))],
        out_shape=[SDS((s, CONV_WIDTH), F32), SDS((s, ATTN_WIDTH), F32)],
        compiler_params=_params(("parallel",)))(dpre1, w_out)


def _grad_x(dproj, w_in, dpre1):
    s = dproj.shape[0]
    tm = _row_tile(s, 512)

    def body(dp_ref, w_ref, d1_ref, o_ref):
        acc = ALPHA * d1_ref[...]
        for k in range(N_CHIPS):
            acc = acc + lax.dot_general(dp_ref[:, IN_SHARD * k:IN_SHARD * (k + 1)], w_ref[k], NT_DIMS,
                                        preferred_element_type=F32)
        o_ref[...] = acc

    return pl.pallas_call(
        body, name="grad_x", grid=(s // tm,),
        in_specs=[pl.BlockSpec((tm, IN_COLS), lambda i: (i, 0)),
                  pl.BlockSpec((N_CHIPS, D_MODEL, IN_SHARD), lambda i: (0, 0, 0)),
                  pl.BlockSpec((tm, D_MODEL), lambda i: (i, 0))],
        out_specs=pl.BlockSpec((tm, D_MODEL), lambda i: (i, 0)),
        out_shape=SDS((s, D_MODEL), F32),
        compiler_params=_params(("parallel",)))(dproj, w_in, dpre1)


def _adamw(w, g, m, v, name):
    r, c = w.shape
    tr = _row_tile(r, 256)

    def body(w_ref, g_ref, m_ref, v_ref, d_ref, nm_ref, nv_ref):
        g_v = g_ref[...]
        nm = ADAM_B1 * m_ref[...] + (1.0 - ADAM_B1) * g_v
        nv = ADAM_B2 * v_ref[...] + (1.0 - ADAM_B2) * (g_v * g_v)
        m_hat = nm / (1.0 - ADAM_B1 ** ADAM_STEP)
        v_hat = nv / (1.0 - ADAM_B2 ** ADAM_STEP)
        d_ref[...] = -ADAM_LR * (m_hat / (jnp.sqrt(v_hat) + ADAM_EPS) + ADAM_WD * w_ref[...])
        nm_ref[...] = nm
        nv_ref[...] = nv

    spec = lambda: pl.BlockSpec((tr, c), lambda i: (i, 0))
    return pl.pallas_call(
        body, name=name, grid=(r // tr,),
        in_specs=[spec(), spec(), spec(), spec()], out_specs=[spec(), spec(), spec()],
        out_shape=[SDS((r, c), F32)] * 3, compiler_params=_params(("parallel",)))(w, g, m, v)


def _add_halves(c_idx, grad, recv, name):
    _, _, h, cols = grad.shape
    th = _row_tile(h, 256)

    def body(c_ref, g_ref, r_ref, o_ref):
        o_ref[...] = g_ref[...] + r_ref[...]

    grid_spec = pltpu.PrefetchScalarGridSpec(
        num_scalar_prefetch=1, grid=(N_CHIPS, h // th),
        in_specs=[pl.BlockSpec((None, None, th, cols), lambda k, t, c_ref: (k, c_ref[0], t, 0)),
                  pl.BlockSpec((None, th, cols), lambda k, t, c_ref: (k, t, 0))],
        out_specs=pl.BlockSpec((None, th, cols), lambda k, t, c_ref: (k, t, 0)))
    return pl.pallas_call(
        body, name=name, grid_spec=grid_spec, out_shape=SDS((N_CHIPS, h, cols), F32),
        compiler_params=_params(("parallel", "parallel")))(c_idx, grad, recv)


def _add_chips(k_idx, halves, recv, name):
    _, h, cols = halves.shape
    th = _row_tile(h, 256)

    def body(k_ref, own_ref, r_ref, o_ref):
        o_ref[...] = own_ref[...] + r_ref[0] + r_ref[1] + r_ref[2]

    grid_spec = pltpu.PrefetchScalarGridSpec(
        num_scalar_prefetch=1, grid=(h // th,),
        in_specs=[pl.BlockSpec((None, th, cols), lambda t, k_ref: (k_ref[0], t, 0)),
                  pl.BlockSpec((3, th, cols), lambda t, k_ref: (0, t, 0))],
        out_specs=pl.BlockSpec((th, cols), lambda t, k_ref: (t, 0)))
    return pl.pallas_call(
        body, name=name, grid_spec=grid_spec, out_shape=SDS((h, cols), F32),
        compiler_params=_params(("parallel",)))(k_idx, halves, recv)


ANY = pl.BlockSpec(memory_space=pl.ANY)


def _gather_weights(shards, conv_shard):
    n = len(shards)

    def body(*refs):
        ins, conv_in = refs[:n], refs[n]
        outs, conv_out = refs[n + 1:2 * n + 1], refs[2 * n + 1]
        send_sems, recv_sems, local_sems = refs[2 * n + 2:]
        x, y, c = _position()
        k = 2 * x + y
        sibling = (x, y, 1 - c)
        chips = [(_flip(x, fx), _flip(y, fy)) for fx, fy in CHIP_FLIPS]

        def half(a, rows_of_core):
            h = shards[a].shape[0] // 2
            return pl.ds(pl.multiple_of(rows_of_core * h, h), h)

        def remote(src, dst, idx, target):
            return pltpu.make_async_remote_copy(src_ref=src, dst_ref=dst, send_sem=send_sems.at[idx],
                                                recv_sem=recv_sems.at[idx], device_id=target,
                                                device_id_type=MESH)

        local = [pltpu.make_async_copy(ins[a], outs[a].at[k], local_sems.at[a]) for a in range(n)]
        local.append(pltpu.make_async_copy(conv_in, conv_out.at[k], local_sems.at[n]))
        for cp in local:
            cp.start()
        started = []
        for a in range(n):
            for j, (tx, ty) in enumerate(chips):
                cp = remote(ins[a].at[half(a, c)], outs[a].at[k, half(a, c)], 6 * a + j, (tx, ty, c))
                cp.start()
                started.append(cp)
        for j, (tx, ty) in enumerate(chips):
            cp = remote(conv_in, conv_out.at[k], 6 * n + j, (tx, ty, c))
            cp.start()
            started.append(cp)
        for a in range(n):
            for j, (tx, ty) in enumerate(chips):
                kj = 2 * tx + ty
                landed = outs[a].at[kj, half(a, c)]
                remote(landed, landed, 6 * a + j, sibling).wait_recv()
                cp = remote(landed, landed, 6 * a + 3 + j, sibling)
                cp.start()
                started.append(cp)
        for a in range(n):
            for j, (tx, ty) in enumerate(chips):
                kj = 2 * tx + ty
                other = outs[a].at[kj, half(a, 1 - c)]
                remote(other, other, 6 * a + 3 + j, sibling).wait_recv()
        for j, (tx, ty) in enumerate(chips):
            kj = 2 * tx + ty
            remote(conv_out.at[kj], conv_out.at[kj], 6 * n + j, sibling).wait_recv()
        for cp in started:
            cp.wait_send()
        for cp in local:
            cp.wait()

    out_shape = [SDS((N_CHIPS,) + w.shape, w.dtype) for w in shards]
    out_shape.append(SDS((N_CHIPS,) + conv_shard.shape, conv_shard.dtype))
    n_sems = 6 * n + 3
    return pl.pallas_call(
        body, name="gather_weights", in_specs=[ANY] * (n + 1), out_specs=[ANY] * (n + 1),
        out_shape=out_shape,
        scratch_shapes=[pltpu.SemaphoreType.DMA((n_sems,)), pltpu.SemaphoreType.DMA((n_sems,)),
                        pltpu.SemaphoreType.DMA((n + 1,))])(*shards, conv_shard)


def _exchange_sibling_halves(grads):
    n = len(grads)

    def body(*refs):
        ins, outs = refs[:n], refs[n:2 * n]
        send_sems, recv_sems = refs[2 * n:]
        x, y, c = _position()
        sibling = (x, y, 1 - c)
        started = []
        for a in range(n):
            h = grads[a].shape[1] // 2
            theirs = pl.ds(pl.multiple_of((1 - c) * h, h), h)
            for k in range(N_CHIPS):
                cp = pltpu.make_async_remote_copy(
                    src_ref=ins[a].at[k, theirs], dst_ref=outs[a].at[k], send_sem=send_sems.at[4 * a + k],
                    recv_sem=recv_sems.at[4 * a + k], device_id=sibling, device_id_type=MESH)
                cp.start()
                started.append(cp)
        for cp in started:
            cp.wait_recv()
        for cp in started:
            cp.wait_send()

    out_shape = [SDS((N_CHIPS, g.shape[1] // 2, g.shape[2]), g.dtype) for g in grads]
    return pl.pallas_call(
        body, name="exchange_sibling_halves", in_specs=[ANY] * n, out_specs=[ANY] * n, out_shape=out_shape,
        scratch_shapes=[pltpu.SemaphoreType.DMA((4 * n,)), pltpu.SemaphoreType.DMA((4 * n,))])(*grads)


def _exchange_chip_pieces(halves):
    n = len(halves)

    def body(*refs):
        ins, outs = refs[:n], refs[n:2 * n]
        send_sems, recv_sems = refs[2 * n:]
        x, y, c = _position()
        started = []
        for a in range(n):
            for j, (fx, fy) in enumerate(CHIP_FLIPS):
                tx, ty = _flip(x, fx), _flip(y, fy)
                cp = pltpu.make_async_remote_copy(
                    src_ref=ins[a].at[2 * tx + ty], dst_ref=outs[a].at[j], send_sem=send_sems.at[3 * a + j],
                    recv_sem=recv_sems.at[3 * a + j], device_id=(tx, ty, c), device_id_type=MESH)
                cp.start()
                started.append(cp)
        for cp in started:
            cp.wait_recv()
        for cp in started:
            cp.wait_send()

    out_shape = [SDS((3,) + h.shape[1:], h.dtype) for h in halves]
    return pl.pallas_call(
        body, name="exchange_chip_pieces", in_specs=[ANY] * n, out_specs=[ANY] * n, out_shape=out_shape,
        scratch_shapes=[pltpu.SemaphoreType.DMA((3 * n,)), pltpu.SemaphoreType.DMA((3 * n,))])(*halves)


def _share_with_sibling(pieces):
    n = len(pieces)

    def body(*refs):
        ins, outs = refs[:n], refs[n:2 * n]
        send_sems, recv_sems, local_sems = refs[2 * n:]
        x, y, c = _position()
        sibling = (x, y, 1 - c)
        started, local = [], []
        for a in range(n):
            h = pieces[a].shape[0]
            mine = pl.ds(pl.multiple_of(c * h, h), h)
            cp = pltpu.make_async_copy(ins[a], outs[a].at[mine], local_sems.at[a])
            cp.start()
            local.append(cp)
            cp = pltpu.make_async_remote_copy(
                src_ref=ins[a], dst_ref=outs[a].at[mine], send_sem=send_sems.at[a], recv_sem=recv_sems.at[a],
                device_id=sibling, device_id_type=MESH)
            cp.start()
            started.append(cp)
        for a in range(n):
            h = pieces[a].shape[0]
            theirs = outs[a].at[pl.ds(pl.multiple_of((1 - c) * h, h), h)]
            pltpu.make_async_remote_copy(
                src_ref=theirs, dst_ref=theirs, send_sem=send_sems.at[a], recv_sem=recv_sems.at[a],
                device_id=sibling, device_id_type=MESH).wait_recv()
        for cp in started:
            cp.wait_send()
        for cp in local:
            cp.wait()

    out_shape = [SDS((2 * p.shape[0], p.shape[1]), p.dtype) for p in pieces]
    return pl.pallas_call(
        body, name="share_with_sibling", in_specs=[ANY] * n, out_specs=[ANY] * n, out_shape=out_shape,
        scratch_shapes=[pltpu.SemaphoreType.DMA((n,)), pltpu.SemaphoreType.DMA((n,)),
                        pltpu.SemaphoreType.DMA((n,))])(*pieces)


def _all_reduce_small(vec):
    n_dev = 2 * N_CHIPS

    def body(v_ref, o_ref, buf, send_sems, recv_sems):
        x, y, c = _position()
        me = 4 * x + 2 * y + c
        buf[me] = v_ref[...]
        started = []
        for idx, (fx, fy, fc) in enumerate(DEVICE_FLIPS):
            cp = pltpu.make_async_remote_copy(
                src_ref=v_ref, dst_ref=buf.at[me], send_sem=send_sems.at[idx], recv_sem=recv_sems.at[idx],
                device_id=(_flip(x, fx), _flip(y, fy), _flip(c, fc)), device_id_type=MESH)
            cp.start()
            started.append(cp)
        for idx, (fx, fy, fc) in enumerate(DEVICE_FLIPS):
            src = 4 * _flip(x, fx) + 2 * _flip(y, fy) + _flip(c, fc)
            pltpu.make_async_remote_copy(
                src_ref=v_ref, dst_ref=buf.at[src], send_sem=send_sems.at[idx], recv_sem=recv_sems.at[idx],
                device_id=(x, y, c), device_id_type=MESH).wait_recv()
        for cp in started:
            cp.wait_send()
        acc = buf[0]
        for d in range(1, n_dev):
            acc = acc + buf[d]
        o_ref[...] = acc

    vmem = pl.BlockSpec(memory_space=pltpu.VMEM)
    return pl.pallas_call(
        body, name="all_reduce_small", in_specs=[vmem], out_specs=vmem, out_shape=SDS(vec.shape, vec.dtype),
        scratch_shapes=[pltpu.VMEM((n_dev,) + vec.shape, vec.dtype),
                        pltpu.SemaphoreType.DMA((n_dev - 1,)), pltpu.SemaphoreType.DMA((n_dev - 1,))])(vec)


def _constants():
    r = jnp.arange(2 * KEY_BLOCK)[:, None] % KEY_BLOCK
    c = jnp.arange(2 * KEY_BLOCK)[None, :]
    later = jnp.where(c < KEY_BLOCK, r > c, True).astype(BF16)
    earlier = jnp.where(c < KEY_BLOCK, r < c, True).astype(BF16)
    gr = (jnp.arange(2 * LANES)[:, None] % LANES) // GROUP
    gc = jnp.arange(LANES)[None, :] // GROUP
    gmat = (gr == gc).astype(BF16)
    return later, earlier, gmat


def _rows(v):
    return v.reshape(-1, LANES)


def kernel(x, w_in, conv_w, g_conv, g_attn, w_out, ln1_g, ln1_b, w_up, w_down, ln2_g, ln2_b, loss_target, m_w_in, m_conv_w, m_g_conv, m_g_attn, m_w_out, m_ln1_g, m_ln1_b, m_w_up, m_w_down, m_ln2_g, m_ln2_b, v_w_in, v_conv_w, v_g_conv, v_g_attn, v_w_out, v_ln1_g, v_ln1_b, v_w_up, v_w_down, v_ln2_g, v_ln2_b):
    xs, target = x[0], loss_target[0]
    mesh_x, mesh_y, mesh_c = _position()
    c_idx = jnp.reshape(mesh_c, (1,)).astype(jnp.int32)
    k_idx = jnp.reshape(2 * mesh_x + mesh_y, (1,)).astype(jnp.int32)
    tri_later, tri_earlier, gmat = _constants()

    pad_rows = lambda a: jnp.pad(a, ((0, SUBLANES - a.shape[0]), (0, 0)))
    big = [_cast_bf16(w[0], "cast_" + nm) for w, nm in ((w_in, "w_in"), (w_out, "w_out"), (w_up, "w_up"), (w_down, "w_down"))]
    w_in_f, w_out_f, w_up_f, w_down_f, conv_f = _gather_weights(big, pad_rows(conv_w[0]))
    w_out_f = w_out_f.reshape(D_MODEL, D_MODEL)
    taps = jnp.transpose(conv_f, (1, 0, 2)).reshape(SUBLANES, CONV_WIDTH)

    gates, qkv = _proj(xs, w_in_f)
    ycn = _conv_fwd(gates, taps, g_conv, gmat)
    o, yan, tot = _attn_fwd(qkv, g_attn, tri_later, gmat)
    x1, xhat1, rstd1 = _mix_ln1(ycn, yan, w_out_f, xs, ln1_g, ln1_b)
    dpre2, ln2_sums, loss_sum = _mlp_fwd_loss(x1, w_up_f, w_down_f, target, ln2_g, ln2_b)

    hid, dup, dpre1, ln1_sums = _mlp_bwd_ln1(x1, dpre2, w_up_f, w_down_f, xhat1, rstd1, ln1_g)
    gw_up = _grad_tn(x1, dup, "grad_w_up", FF_SHARD, True)
    gw_down = _grad_tn(hid, dpre2, "grad_w_down", D_MODEL, False).reshape(N_CHIPS, FF_SHARD, D_MODEL)
    gw_out = jnp.concatenate([_grad_tn(ycn, dpre1, "grad_w_out_conv", D_MODEL, False),
                              _grad_tn(yan, dpre1, "grad_w_out_attn", D_MODEL, False)], axis=0)
    gw_out = gw_out.reshape(N_CHIPS, D_MODEL // N_CHIPS, D_MODEL)
    dycn, dyan = _dmix(dpre1, w_out_f)
    dq, dk, dv, gattn_sums = _attn_bwd(qkv, o, tot, dyan, g_attn, tri_earlier, gmat)
    dbg, dy, conv_sums = _conv_bwd_gate(gates, dycn, taps, g_conv, gmat)
    dproj = _dproj_assemble(gates, dy, dbg, dq, dk, dv, taps)
    gw_in = _grad_tn(xs, dproj, "grad_w_in", IN_SHARD, True)
    grad_x = _grad_x(dproj, w_in_f, dpre1)

    grads = [gw_in, gw_out, gw_up, gw_down]
    names = ["w_in", "w_out", "w_up", "w_down"]
    from_sibling = _exchange_sibling_halves(grads)
    halves = [_add_halves(c_idx, g.reshape(N_CHIPS, 2, g.shape[1] // 2, g.shape[2]), r, "add_halves_" + nm)
              for g, r, nm in zip(grads, from_sibling, names)]
    from_chips = _exchange_chip_pieces(halves)
    pieces = [_add_chips(k_idx, h, r, "add_chips_" + nm) for h, r, nm in zip(halves, from_chips, names)]
    g_w_in, g_w_out, g_w_up, g_w_down = _share_with_sibling(pieces)

    conv_rows = jnp.transpose(conv_sums[0:3].reshape(3, N_CHIPS, LANES), (1, 0, 2)).reshape(3 * N_CHIPS, LANES)
    small = jnp.concatenate([
        loss_sum, _rows(conv_sums[3]), _rows(gattn_sums[0]), _rows(ln1_sums[0]), _rows(ln1_sums[1]),
        _rows(ln2_sums[0]), _rows(ln2_sums[1]), conv_rows,
        jnp.zeros((SMALL_ROWS - ROW_CONVW - 3 * N_CHIPS, LANES), F32)], axis=0)
    total = _all_reduce_small(small)
    loss = total[ROW_LOSS, 0]
    g_conv_w = lax.dynamic_slice(total, (ROW_CONVW + 3 * k_idx[0], 0), (3, LANES))

    def pack(gc_, ga_, l1g, l1b, l2g, l2b, cw):
        return jnp.concatenate([_rows(gc_), _rows(ga_), _rows(l1g), _rows(l1b), _rows(l2g), _rows(l2b), cw[0],
                                jnp.zeros((PARAM_ROWS + SUBLANES - ROW_CONVW - 3, LANES), F32)], axis=0)

    small_w = pack(g_conv, g_attn, ln1_g, ln1_b, ln2_g, ln2_b, conv_w)
    small_m = pack(m_g_conv, m_g_attn, m_ln1_g, m_ln1_b, m_ln2_g, m_ln2_b, m_conv_w)
    small_v = pack(v_g_conv, v_g_attn, v_ln1_g, v_ln1_b, v_ln2_g, v_ln2_b, v_conv_w)
    small_g = jnp.concatenate([total[ROW_GCONV:ROW_CONVW], g_conv_w,
                               jnp.zeros((PARAM_ROWS + SUBLANES - ROW_CONVW - 3, LANES), F32)], axis=0)
    small_out = _adamw(small_w, small_g, small_m, small_v, "adamw_small")

    def unpack(p):
        off = ROW_GCONV
        vec = lambda a, b: p[a - off:b - off].reshape(1, -1)
        return {"g_conv": vec(ROW_GCONV, ROW_GATTN), "g_attn": vec(ROW_GATTN, ROW_LN1G),
                "ln1_g": vec(ROW_LN1G, ROW_LN1B), "ln1_b": vec(ROW_LN1B, ROW_LN2G),
                "ln2_g": vec(ROW_LN2G, ROW_LN2B), "ln2_b": vec(ROW_LN2B, ROW_CONVW),
                "conv_w": p[ROW_CONVW - off:ROW_CONVW - off + 3][None]}

    big_out = {
        "w_in": _adamw(w_in[0], g_w_in, m_w_in[0], v_w_in[0], "adamw_w_in"),
        "w_out": _adamw(w_out[0], g_w_out, m_w_out[0], v_w_out[0], "adamw_w_out"),
        "w_up": _adamw(w_up[0], g_w_up, m_w_up[0], v_w_up[0], "adamw_w_up"),
        "w_down": _adamw(w_down[0], g_w_down, m_w_down[0], v_w_down[0], "adamw_w_down"),
    }
    big_grads = {"w_in": g_w_in, "w_out": g_w_out, "w_up": g_w_up, "w_down": g_w_down}
    order = ["w_in", "conv_w", "g_conv", "g_attn", "w_out", "ln1_g", "ln1_b", "w_up", "w_down", "ln2_g", "ln2_b"]
    small_parts = [unpack(small_g)] + [unpack(p) for p in small_out]

    def leaf(kind, name):
        if name in big_out:
            return (big_grads[name] if kind == 0 else big_out[name][kind - 1])[None]
        return small_parts[kind][name]

    outs = [loss, grad_x[None]]
    for kind in range(4):
        outs.extend(leaf(kind, name) for name in order)
    return tuple(outs)
```

```python
import functools

import jax
import jax.numpy as jnp
from jax import lax
from jax.experimental import pallas as pl
from jax.experimental.pallas import tpu as pltpu

F32 = jnp.float32
BF16 = jnp.bfloat16
SDS = jax.ShapeDtypeStruct

D_MODEL = 1024
CONV_WIDTH = 512
ATTN_WIDTH = 512
GROUP = 64
GATE_COLS = 3 * CONV_WIDTH
QKV_COLS = 3 * ATTN_WIDTH
IN_COLS = GATE_COLS + QKV_COLS
D_FF = 4 * D_MODEL
N_CHIPS = 4
IN_SHARD = IN_COLS // N_CHIPS
FF_SHARD = D_FF // N_CHIPS
ALPHA = float(2.0 ** 0.25)
LN_EPS = 1e-5
RMS_EPS = 1e-6
ATTN_SCALE = GROUP ** -0.5
ADAM_LR = 0.001
ADAM_B1 = 0.9
ADAM_B2 = 0.999
ADAM_EPS = 1e-08
ADAM_WD = 0.01
ADAM_STEP = 10

LANES = 128
SUBLANES = 8
KEY_BLOCK = 128
ATTN_Q_TILE = 512
ATTN_KEY_BLOCKS = 2
VMEM_LIMIT = 56 * 1024 * 1024

MESH = pl.DeviceIdType.MESH
CHIP_FLIPS = ((1, 0), (0, 1), (1, 1))
DEVICE_FLIPS = tuple((fx, fy, fc) for fx in (0, 1) for fy in (0, 1) for fc in (0, 1))[1:]
NT_DIMS = (((1,), (1,)), ((), ()))
TN_DIMS = (((0,), (0,)), ((), ()))

ROW_LOSS = 0
ROW_GCONV = 8
ROW_GATTN = 12
ROW_LN1G = 16
ROW_LN1B = 24
ROW_LN2G = 32
ROW_LN2B = 40
ROW_CONVW = 48
SMALL_ROWS = 64
PARAM_ROWS = 48


def _params(sem=None):
    return pltpu.CompilerParams(dimension_semantics=sem, vmem_limit_bytes=VMEM_LIMIT)


def _flip(v, f):
    return 1 - v if f else v


def _position():
    return lax.axis_index("x"), lax.axis_index("y"), lax.axis_index("c")


def _hilo(v):
    hi = v.astype(BF16)
    lo = (v - hi.astype(F32)).astype(BF16)
    return jnp.concatenate([hi, lo], axis=1)


def _hilo_dot(v, mat):
    return jnp.dot(_hilo(v), mat, preferred_element_type=F32)


def _group_sum(v, gmat):
    parts = [_hilo_dot(v[:, LANES * j:LANES * (j + 1)], gmat) for j in range(v.shape[1] // LANES)]
    return parts[0] if len(parts) == 1 else jnp.concatenate(parts, axis=1)


def _softplus_terms(z):
    sp = jnp.log1p(jnp.exp(-jnp.abs(z)))
    log_beta = jnp.minimum(z, 0.0) - sp
    return log_beta, log_beta - z


def _layer_norm_fwd(pre, g, b):
    mu = jnp.mean(pre, axis=-1, keepdims=True)
    d = pre - mu
    var = jnp.mean(d * d, axis=-1, keepdims=True)
    rstd = lax.rsqrt(var + LN_EPS)
    xhat = d * rstd
    return xhat * g + b, xhat, rstd


def _layer_norm_bwd(dy, xhat, rstd, g):
    dxh = dy * g
    m1 = jnp.mean(dxh, axis=-1, keepdims=True)
    m2 = jnp.mean(dxh * xhat, axis=-1, keepdims=True)
    return rstd * (dxh - m1 - xhat * m2)


def _row_tile(s, want):
    return min(s, want)


def _cast_bf16(w, name):
    r, c = w.shape
    tr = _row_tile(r, 256)

    def body(w_ref, o_ref):
        o_ref[...] = w_ref[...].astype(BF16)

    return pl.pallas_call(
        body, name=name, grid=(r // tr,),
        in_specs=[pl.BlockSpec((tr, c), lambda i: (i, 0))],
        out_specs=pl.BlockSpec((tr, c), lambda i: (i, 0)),
        out_shape=SDS((r, c), BF16), compiler_params=_params(("parallel",)))(w)


def _proj(x, w_in):
    s = x.shape[0]
    tm = _row_tile(s, 512)

    def body(x_ref, w_ref, gates_ref, qkv_ref):
        xb = x_ref[...].astype(BF16)
        for k in range(N_CHIPS):
            acc = jnp.dot(xb, w_ref[k], preferred_element_type=F32)
            if k < 2:
                gates_ref[:, IN_SHARD * k:IN_SHARD * (k + 1)] = acc
            else:
                qkv_ref[:, IN_SHARD * (k - 2):IN_SHARD * (k - 1)] = acc.astype(BF16)

    return pl.pallas_call(
        body, name="proj", grid=(s // tm,),
        in_specs=[pl.BlockSpec((tm, D_MODEL), lambda i: (i, 0)),
                  pl.BlockSpec((N_CHIPS, D_MODEL, IN_SHARD), lambda i: (0, 0, 0))],
        out_specs=[pl.BlockSpec((tm, GATE_COLS), lambda i: (i, 0)),
                   pl.BlockSpec((tm, QKV_COLS), lambda i: (i, 0))],
        out_shape=[SDS((s, GATE_COLS), F32), SDS((s, QKV_COLS), BF16)],
        compiler_params=_params(("parallel",)))(x, w_in)


def _conv_forward_values(g_ref, halo_ref, taps_ref, first_block):
    gates = g_ref[...]
    tr = gates.shape[0]
    bg = gates[:, :CONV_WIDTH]
    cg = gates[:, CONV_WIDTH:2 * CONV_WIDTH]
    h = gates[:, 2 * CONV_WIDTH:]
    u = cg * h

    def prev(r):
        v = halo_ref[r:r + 1, CONV_WIDTH:2 * CONV_WIDTH] * halo_ref[r:r + 1, 2 * CONV_WIDTH:GATE_COLS]
        return jnp.where(first_block, 0.0, v)

    row = lax.broadcasted_iota(jnp.int32, (tr, CONV_WIDTH), 0)
    u1 = jnp.where(row == 0, prev(7), pltpu.roll(u, 1, 0))
    u2 = jnp.where(row == 0, prev(6), jnp.where(row == 1, prev(7), pltpu.roll(u, 2, 0)))
    y = taps_ref[0:1, :] * u2 + taps_ref[1:2, :] * u1 + taps_ref[2:3, :] * u
    return bg, cg, h, u, u1, u2, y


def _conv_fwd(gates, taps, g_conv, gmat):
    s = gates.shape[0]
    tr = _row_tile(s, 512)
    hb = tr // SUBLANES

    def body(g_ref, halo_ref, taps_ref, gain_ref, gmat_ref, out_ref):
        i = pl.program_id(0)
        bg, _, _, _, _, _, y = _conv_forward_values(g_ref, halo_ref, taps_ref, i == 0)
        yc = bg * y
        ms = _group_sum(yc * yc, gmat_ref[...]) * (1.0 / GROUP)
        out_ref[...] = (yc * lax.rsqrt(ms + RMS_EPS) * gain_ref[...]).astype(BF16)

    return pl.pallas_call(
        body, name="conv_fwd", grid=(s // tr,),
        in_specs=[pl.BlockSpec((tr, GATE_COLS), lambda i: (i, 0)),
                  pl.BlockSpec((SUBLANES, GATE_COLS), lambda i: (jnp.maximum(i * hb - 1, 0), 0)),
                  pl.BlockSpec((SUBLANES, CONV_WIDTH), lambda i: (0, 0)),
                  pl.BlockSpec((1, CONV_WIDTH), lambda i: (0, 0)),
                  pl.BlockSpec((2 * LANES, LANES), lambda i: (0, 0))],
        out_specs=pl.BlockSpec((tr, CONV_WIDTH), lambda i: (i, 0)),
        out_shape=SDS((s, CONV_WIDTH), BF16),
        compiler_params=_params(("parallel",)))(gates, gates, taps, g_conv, gmat)


def _conv_bwd_gate(gates, dycn, taps, g_conv, gmat):
    s = gates.shape[0]
    tr = _row_tile(s, 512)
    hb = tr // SUBLANES

    def body(g_ref, halo_ref, dn_ref, taps_ref, gain_ref, gmat_ref, dbg_ref, dy_ref, sums_ref):
        i = pl.program_id(0)
        bg, _, _, u, u1, u2, y = _conv_forward_values(g_ref, halo_ref, taps_ref, i == 0)
        gmat_v = gmat_ref[...]
        yc = bg * y
        rstd = lax.rsqrt(_group_sum(yc * yc, gmat_v) * (1.0 / GROUP) + RMS_EPS)
        n = yc * rstd
        dout = dn_ref[...]
        dn = dout * gain_ref[...]
        dyc = rstd * (dn - n * (_group_sum(dn * n, gmat_v) * (1.0 / GROUP)))
        dbg_ref[...] = (dyc * y).astype(BF16)
        dy = dyc * bg
        dy_ref[...] = dy

        @pl.when(i == 0)
        def _():
            sums_ref[...] = jnp.zeros_like(sums_ref)

        sums_ref[0:1, :] += jnp.sum(dy * u2, axis=0, keepdims=True)
        sums_ref[1:2, :] += jnp.sum(dy * u1, axis=0, keepdims=True)
        sums_ref[2:3, :] += jnp.sum(dy * u, axis=0, keepdims=True)
        sums_ref[3:4, :] += jnp.sum(dout * n, axis=0, keepdims=True)

    return pl.pallas_call(
        body, name="conv_bwd_gate", grid=(s // tr,),
        in_specs=[pl.BlockSpec((tr, GATE_COLS), lambda i: (i, 0)),
                  pl.BlockSpec((SUBLANES, GATE_COLS), lambda i: (jnp.maximum(i * hb - 1, 0), 0)),
                  pl.BlockSpec((tr, CONV_WIDTH), lambda i: (i, 0)),
                  pl.BlockSpec((SUBLANES, CONV_WIDTH), lambda i: (0, 0)),
                  pl.BlockSpec((1, CONV_WIDTH), lambda i: (0, 0)),
                  pl.BlockSpec((2 * LANES, LANES), lambda i: (0, 0))],
        out_specs=[pl.BlockSpec((tr, CONV_WIDTH), lambda i: (i, 0)),
                   pl.BlockSpec((tr, CONV_WIDTH), lambda i: (i, 0)),
                   pl.BlockSpec((SUBLANES, CONV_WIDTH), lambda i: (0, 0))],
        out_shape=[SDS((s, CONV_WIDTH), BF16), SDS((s, CONV_WIDTH), F32), SDS((SUBLANES, CONV_WIDTH), F32)],
        compiler_params=_params(("arbitrary",)))(gates, gates, dycn, taps, g_conv, gmat)


def _dproj_assemble(gates, dy, dbg, dq, dk, dv, taps):
    s = gates.shape[0]
    tr = _row_tile(s, 512)
    hb = tr // SUBLANES
    last = s // SUBLANES - 1
    n_blocks = s // tr

    def body(g_ref, dy_ref, halo_ref, dbg_ref, dq_ref, dk_ref, dv_ref, taps_ref, out_ref):
        i = pl.program_id(0)
        gates_v = g_ref[...]
        cg = gates_v[:, CONV_WIDTH:2 * CONV_WIDTH]
        h = gates_v[:, 2 * CONV_WIDTH:]
        dy_v = dy_ref[...]
        last_block = i == n_blocks - 1
        nxt = lambda r: jnp.where(last_block, 0.0, halo_ref[r:r + 1, :])
        row = lax.broadcasted_iota(jnp.int32, (tr, CONV_WIDTH), 0)
        d1 = jnp.where(row == tr - 1, nxt(0), pltpu.roll(dy_v, tr - 1, 0))
        d2 = jnp.where(row == tr - 1, nxt(1), jnp.where(row == tr - 2, nxt(0), pltpu.roll(dy_v, tr - 2, 0)))
        du = taps_ref[2:3, :] * dy_v + taps_ref[1:2, :] * d1 + taps_ref[0:1, :] * d2
        out_ref[:, 0:CONV_WIDTH] = dbg_ref[...]
        out_ref[:, CONV_WIDTH:2 * CONV_WIDTH] = (du * h).astype(BF16)
        out_ref[:, 2 * CONV_WIDTH:GATE_COLS] = (du * cg).astype(BF16)
        out_ref[:, GATE_COLS:GATE_COLS + ATTN_WIDTH] = dq_ref[...]
        out_ref[:, GATE_COLS + ATTN_WIDTH:GATE_COLS + 2 * ATTN_WIDTH] = dk_ref[...].astype(BF16)
        out_ref[:, GATE_COLS + 2 * ATTN_WIDTH:] = dv_ref[...].astype(BF16)

    row_spec = lambda w: pl.BlockSpec((tr, w), lambda i: (i, 0))
    return pl.pallas_call(
        body, name="dproj_assemble", grid=(s // tr,),
        in_specs=[row_spec(GATE_COLS), row_spec(CONV_WIDTH),
                  pl.BlockSpec((SUBLANES, CONV_WIDTH), lambda i: (jnp.minimum((i + 1) * hb, last), 0)),
                  row_spec(CONV_WIDTH), row_spec(ATTN_WIDTH), row_spec(ATTN_WIDTH), row_spec(ATTN_WIDTH),
                  pl.BlockSpec((SUBLANES, CONV_WIDTH), lambda i: (0, 0))],
        out_specs=row_spec(IN_COLS),
        out_shape=SDS((s, IN_COLS), BF16),
        compiler_params=_params(("parallel",)))(gates, dy, dy, dbg, dq, dk, dv, taps)


def _stack_heads(rows, nb):
    lane = lax.broadcasted_iota(jnp.int32, (1, LANES), 1)
    zero = jnp.zeros((KEY_BLOCK, LANES), rows.dtype)
    parts = []
    for blk in range(nb):
        r = rows[blk * KEY_BLOCK:(blk + 1) * KEY_BLOCK]
        parts.append(jnp.where(lane < GROUP, r, zero))
        parts.append(jnp.where(lane < GROUP, zero, r))
    return jnp.concatenate(parts, axis=0)


def _stack_hilo(v, n_cols):
    return jnp.concatenate([_hilo(v[:, c * KEY_BLOCK:(c + 1) * KEY_BLOCK]) for c in range(n_cols)], axis=0)


def _causal_mask(tq, nb, diag_base):
    shape = (tq, 2 * nb * KEY_BLOCK)
    row = lax.broadcasted_iota(jnp.int32, shape, 0)
    col = lax.broadcasted_iota(jnp.int32, shape, 1)
    key = diag_base + (col // (2 * KEY_BLOCK)) * KEY_BLOCK + col % KEY_BLOCK
    return key < row


def _attn_fwd(qkv, g_attn, tri, gmat):
    s = qkv.shape[0]
    tq = _row_tile(s, ATTN_Q_TILE)
    tk = KEY_BLOCK
    nb = ATTN_KEY_BLOCKS
    width = nb * tk
    diag_trips = tq // width
    pairs = ATTN_WIDTH // LANES

    def body(q_ref, k_ref, v_ref, gain_ref, tri_ref, gmat_ref, o_ref, yn_ref, tot_ref):
        i = pl.program_id(1)
        q2 = q_ref[...]
        tri_v = tri_ref[...]

        def trip(t, carry, diag_base):
            run = [carry[0], carry[1]]
            oacc = carry[2]
            s0 = pl.multiple_of(t * width, width)
            ksel = _stack_heads(k_ref[pl.ds(s0, width), :], nb)
            vsel = _stack_heads(v_ref[pl.ds(s0, width), :], nb)
            z = lax.dot_general(q2, ksel, NT_DIMS, preferred_element_type=F32) * ATTN_SCALE
            log_beta, log_keep = _softplus_terms(z)
            if diag_base is not None:
                valid = _causal_mask(tq, nb, diag_base)
                log_keep = jnp.where(valid, log_keep, 0.0)
            ct = jnp.dot(_stack_hilo(log_keep, 2 * nb), tri_v, preferred_element_type=F32)
            a_parts = [None] * (2 * nb)
            for c in reversed(range(2 * nb)):
                h = c % 2
                ct_c = ct[c * tq:(c + 1) * tq]
                a_parts[c] = jnp.exp(log_beta[:, c * tk:(c + 1) * tk] + ct_c[:, :tk] + run[h])
                run[h] = run[h] + ct_c[:, tk:]
            a = jnp.concatenate(a_parts, axis=1)
            if diag_base is not None:
                a = jnp.where(valid, a, 0.0)
            oacc = oacc + jnp.dot(a.astype(BF16), vsel, preferred_element_type=F32)
            return run[0], run[1], oacc

        carry = (jnp.zeros((tq, tk), F32), jnp.zeros((tq, tk), F32), jnp.zeros((tq, LANES), F32))
        for d in reversed(range(diag_trips)):
            carry = trip(i * diag_trips + d, carry, d * width)
        n_full = i * diag_trips
        carry = lax.fori_loop(0, n_full, lambda it, c: trip(n_full - 1 - it, c, None), carry)
        run_a, run_b, oacc = carry
        lane = lax.broadcasted_iota(jnp.int32, (1, LANES), 1)
        o_ref[...] = oacc
        tot_ref[...] = jnp.where(lane < GROUP, run_a, run_b)
        ms = _group_sum(oacc * oacc, gmat_ref[...]) * (1.0 / GROUP)
        yn_ref[...] = (oacc * lax.rsqrt(ms + RMS_EPS) * gain_ref[...]).astype(BF16)

    blk = lambda: pl.BlockSpec((tq, LANES), lambda p, i: (i, p))
    return pl.pallas_call(
        body, name="attn_fwd", grid=(pairs, s // tq),
        in_specs=[pl.BlockSpec((tq, LANES), lambda p, i: (i, p)),
                  pl.BlockSpec((s, LANES), lambda p, i: (0, pairs + p)),
                  pl.BlockSpec((s, LANES), lambda p, i: (0, 2 * pairs + p)),
                  pl.BlockSpec((1, LANES), lambda p, i: (0, p)),
                  pl.BlockSpec((2 * tk, 2 * tk), lambda p, i: (0, 0)),
                  pl.BlockSpec((2 * LANES, LANES), lambda p, i: (0, 0))],
        out_specs=[blk(), blk(), blk()],
        out_shape=[SDS((s, ATTN_WIDTH), F32), SDS((s, ATTN_WIDTH), BF16), SDS((s, ATTN_WIDTH), F32)],
        compiler_params=_params(("parallel", "parallel")))(qkv, qkv, qkv, g_attn, tri, gmat)


def _attn_bwd(qkv, o, tot, dyn, g_attn, tri, gmat):
    s = qkv.shape[0]
    tq = _row_tile(s, ATTN_Q_TILE)
    tk = KEY_BLOCK
    nb = ATTN_KEY_BLOCKS
    width = nb * tk
    diag_trips = tq // width
    pairs = ATTN_WIDTH // LANES

    def body(q_ref, k_ref, v_ref, o_ref, tot_ref, dyn_ref, gain_ref, tri_ref, gmat_ref,
             dq_ref, dk_ref, dv_ref, dg_ref):
        i = pl.program_id(1)

        @pl.when(i == 0)
        def _():
            dk_ref[...] = jnp.zeros_like(dk_ref)
            dv_ref[...] = jnp.zeros_like(dv_ref)
            dg_ref[...] = jnp.zeros_like(dg_ref)

        gmat_v = gmat_ref[...]
        o_v = o_ref[...]
        rstd = lax.rsqrt(_group_sum(o_v * o_v, gmat_v) * (1.0 / GROUP) + RMS_EPS)
        n = o_v * rstd
        dout = dyn_ref[...]
        dg_ref[0:1, :] += jnp.sum(dout * n, axis=0, keepdims=True)
        dn = dout * gain_ref[...]
        do2 = (rstd * (dn - n * (_group_sum(dn * n, gmat_v) * (1.0 / GROUP)))).astype(BF16)
        q2 = q_ref[...]
        tot_v = tot_ref[...]
        tots = (jnp.broadcast_to(tot_v[:, 0:1], (tq, tk)), jnp.broadcast_to(tot_v[:, GROUP:GROUP + 1], (tq, tk)))
        tri_v = tri_ref[...]
        lane = lax.broadcasted_iota(jnp.int32, (1, LANES), 1)

        def trip(t, carry, diag_base):
            pref_l = [carry[0], carry[1]]
            pref_g = [carry[2], carry[3]]
            dq = carry[4]
            s0 = pl.multiple_of(t * width, width)
            ksel = _stack_heads(k_ref[pl.ds(s0, width), :], nb)
            vsel = _stack_heads(v_ref[pl.ds(s0, width), :], nb)
            z = lax.dot_general(q2, ksel, NT_DIMS, preferred_element_type=F32) * ATTN_SCALE
            log_beta, log_keep = _softplus_terms(z)
            if diag_base is not None:
                valid = _causal_mask(tq, nb, diag_base)
                log_keep = jnp.where(valid, log_keep, 0.0)
            ctl = jnp.dot(_stack_hilo(log_keep, 2 * nb), tri_v, preferred_element_type=F32)
            da = lax.dot_general(do2, vsel, NT_DIMS, preferred_element_type=F32)
            a_parts = []
            for c in range(2 * nb):
                h = c % 2
                ct_c = ctl[c * tq:(c + 1) * tq]
                cols = slice(c * tk, (c + 1) * tk)
                suffix = tots[h] - pref_l[h] - ct_c[:, :tk] - log_keep[:, cols]
                pref_l[h] = pref_l[h] + ct_c[:, tk:]
                a_parts.append(jnp.exp(log_beta[:, cols] + suffix))
            a = jnp.concatenate(a_parts, axis=1)
            if diag_base is not None:
                a = jnp.where(valid, a, 0.0)
            g = a * da
            ctg = jnp.dot(_stack_hilo(g, 2 * nb), tri_v, preferred_element_type=F32)
            dz_parts = []
            for c in range(2 * nb):
                h = c % 2
                ct_c = ctg[c * tq:(c + 1) * tq]
                cols = slice(c * tk, (c + 1) * tk)
                prefix = pref_g[h] + ct_c[:, :tk]
                pref_g[h] = pref_g[h] + ct_c[:, tk:]
                beta = jnp.exp(log_beta[:, cols])
                dz_parts.append(g[:, cols] * (1.0 - beta) - prefix * beta)
            dz = jnp.concatenate(dz_parts, axis=1) * ATTN_SCALE
            if diag_base is not None:
                dz = jnp.where(valid, dz, 0.0)
            dzb = dz.astype(BF16)
            dq = dq + jnp.dot(dzb, ksel, preferred_element_type=F32)
            dkt = lax.dot_general(dzb, q2, TN_DIMS, preferred_element_type=F32)
            dvt = lax.dot_general(a.astype(BF16), do2, TN_DIMS, preferred_element_type=F32)
            for blk in range(nb):
                ra, rb = slice(2 * blk * tk, (2 * blk + 1) * tk), slice((2 * blk + 1) * tk, (2 * blk + 2) * tk)
                rows = pl.ds(pl.multiple_of(s0 + blk * tk, tk), tk)
                dk_ref[rows, :] += jnp.where(lane < GROUP, dkt[ra], dkt[rb])
                dv_ref[rows, :] += jnp.where(lane < GROUP, dvt[ra], dvt[rb])
            return pref_l[0], pref_l[1], pref_g[0], pref_g[1], dq

        zeros_qk = jnp.zeros((tq, tk), F32)
        carry = (zeros_qk, zeros_qk, zeros_qk, zeros_qk, jnp.zeros((tq, LANES), F32))
        carry = lax.fori_loop(0, i * diag_trips, lambda t, c: trip(t, c, None), carry)
        for d in range(diag_trips):
            carry = trip(i * diag_trips + d, carry, d * width)
        dq_ref[...] = carry[4].astype(BF16)

    blk = lambda: pl.BlockSpec((tq, LANES), lambda p, i: (i, p))
    col = lambda: pl.BlockSpec((s, LANES), lambda p, i: (0, p))
    return pl.pallas_call(
        body, name="attn_bwd", grid=(pairs, s // tq),
        in_specs=[pl.BlockSpec((tq, LANES), lambda p, i: (i, p)),
                  pl.BlockSpec((s, LANES), lambda p, i: (0, pairs + p)),
                  pl.BlockSpec((s, LANES), lambda p, i: (0, 2 * pairs + p)),
                  blk(), blk(), blk(),
                  pl.BlockSpec((1, LANES), lambda p, i: (0, p)),
                  pl.BlockSpec((2 * tk, 2 * tk), lambda p, i: (0, 0)),
                  pl.BlockSpec((2 * LANES, LANES), lambda p, i: (0, 0))],
        out_specs=[blk(), col(), col(), pl.BlockSpec((SUBLANES, LANES), lambda p, i: (0, p))],
        out_shape=[SDS((s, ATTN_WIDTH), BF16), SDS((s, ATTN_WIDTH), F32), SDS((s, ATTN_WIDTH), F32),
                   SDS((SUBLANES, ATTN_WIDTH), F32)],
        compiler_params=_params(("parallel", "arbitrary")))(qkv, qkv, qkv, o, tot, dyn, g_attn, tri, gmat)


def _mix_ln1(ycn, yan, w_out, x, g, b):
    s = x.shape[0]
    tm = _row_tile(s, 512)

    def body(yc_ref, ya_ref, w_ref, x_ref, g_ref, b_ref, x1_ref, xhat_ref, rstd_ref):
        mix = jnp.dot(yc_ref[...], w_ref[0:CONV_WIDTH, :], preferred_element_type=F32)
        mix = mix + jnp.dot(ya_ref[...], w_ref[CONV_WIDTH:, :], preferred_element_type=F32)
        x1, xhat, rstd = _layer_norm_fwd(ALPHA * x_ref[...] + mix, g_ref[...], b_ref[...])
        x1_ref[...] = x1
        xhat_ref[...] = xhat
        rstd_ref[...] = rstd

    row = lambda w: pl.BlockSpec((tm, w), lambda i: (i, 0))
    vec = lambda: pl.BlockSpec((1, D_MODEL), lambda i: (0, 0))
    return pl.pallas_call(
        body, name="mix_ln1", grid=(s // tm,),
        in_specs=[row(CONV_WIDTH), row(ATTN_WIDTH), pl.BlockSpec((D_MODEL, D_MODEL), lambda i: (0, 0)),
                  row(D_MODEL), vec(), vec()],
        out_specs=[row(D_MODEL), row(D_MODEL), row(1)],
        out_shape=[SDS((s, D_MODEL), F32), SDS((s, D_MODEL), F32), SDS((s, 1), F32)],
        compiler_params=_params(("parallel",)))(ycn, yan, w_out, x, g, b)


def _mlp_fwd_loss(x1, w_up, w_down, target, g, b):
    s = x1.shape[0]
    tm = _row_tile(s, 256)

    def body(x1_ref, wu_ref, wd_ref, t_ref, g_ref, b_ref, dpre_ref, sums_ref, loss_ref):
        i = pl.program_id(0)
        x1_v = x1_ref[...]
        xb = x1_v.astype(BF16)
        ffn = jnp.zeros((tm, D_MODEL), F32)
        for k in range(N_CHIPS):
            r = jnp.maximum(jnp.dot(xb, wu_ref[k], preferred_element_type=F32), 0.0)
            ffn = ffn + jnp.dot((r * r).astype(BF16), wd_ref[k], preferred_element_type=F32)
        g_v = g_ref[...]
        x2, xhat, rstd = _layer_norm_fwd(ALPHA * x1_v + ffn, g_v, b_ref[...])
        err = x2 - t_ref[...]
        dx2 = err * (1.0 / D_MODEL)
        dpre_ref[...] = _layer_norm_bwd(dx2, xhat, rstd, g_v)

        @pl.when(i == 0)
        def _():
            sums_ref[...] = jnp.zeros_like(sums_ref)
            loss_ref[...] = jnp.zeros_like(loss_ref)

        sums_ref[0:1, :] += jnp.sum(dx2 * xhat, axis=0, keepdims=True)
        sums_ref[1:2, :] += jnp.sum(dx2, axis=0, keepdims=True)
        loss_ref[...] += jnp.sum(jnp.sum(err * err, axis=1, keepdims=True), axis=0, keepdims=True) * (0.5 / D_MODEL)

    row = lambda: pl.BlockSpec((tm, D_MODEL), lambda i: (i, 0))
    vec = lambda: pl.BlockSpec((1, D_MODEL), lambda i: (0, 0))
    wspec = lambda: pl.BlockSpec((N_CHIPS, D_MODEL, FF_SHARD), lambda i: (0, 0, 0))
    return pl.pallas_call(
        body, name="mlp_fwd_loss", grid=(s // tm,),
        in_specs=[row(), wspec(), wspec(), row(), vec(), vec()],
        out_specs=[row(), pl.BlockSpec((SUBLANES, D_MODEL), lambda i: (0, 0)),
                   pl.BlockSpec((SUBLANES, LANES), lambda i: (0, 0))],
        out_shape=[SDS((s, D_MODEL), F32), SDS((SUBLANES, D_MODEL), F32), SDS((SUBLANES, LANES), F32)],
        compiler_params=_params(("arbitrary",)))(x1, w_up, w_down, target, g, b)


def _mlp_bwd_ln1(x1, dpre2, w_up, w_down, xhat1, rstd1, g1):
    s = x1.shape[0]
    tm = _row_tile(s, 256)

    def body(x1_ref, d2_ref, wu_ref, wd_ref, xh_ref, rs_ref, g_ref, hid_ref, dup_ref, dpre_ref, sums_ref):
        i = pl.program_id(0)
        xb = x1_ref[...].astype(BF16)
        d2 = d2_ref[...]
        d2b = d2.astype(BF16)
        dx1 = ALPHA * d2
        for k in range(N_CHIPS):
            r = jnp.maximum(jnp.dot(xb, wu_ref[k], preferred_element_type=F32), 0.0)
            hid_ref[:, FF_SHARD * k:FF_SHARD * (k + 1)] = (r * r).astype(BF16)
            dhid = lax.dot_general(d2b, wd_ref[k], NT_DIMS, preferred_element_type=F32)
            dupb = (dhid * (2.0 * r)).astype(BF16)
            dup_ref[:, FF_SHARD * k:FF_SHARD * (k + 1)] = dupb
            dx1 = dx1 + lax.dot_general(dupb, wu_ref[k], NT_DIMS, preferred_element_type=F32)
        xhat = xh_ref[...]
        dpre_ref[...] = _layer_norm_bwd(dx1, xhat, rs_ref[...], g_ref[...])

        @pl.when(i == 0)
        def _():
            sums_ref[...] = jnp.zeros_like(sums_ref)

        sums_ref[0:1, :] += jnp.sum(dx1 * xhat, axis=0, keepdims=True)
        sums_ref[1:2, :] += jnp.sum(dx1, axis=0, keepdims=True)

    row = lambda w: pl.BlockSpec((tm, w), lambda i: (i, 0))
    wspec = lambda: pl.BlockSpec((N_CHIPS, D_MODEL, FF_SHARD), lambda i: (0, 0, 0))
    return pl.pallas_call(
        body, name="mlp_bwd_ln1", grid=(s // tm,),
        in_specs=[row(D_MODEL), row(D_MODEL), wspec(), wspec(), row(D_MODEL), row(1),
                  pl.BlockSpec((1, D_MODEL), lambda i: (0, 0))],
        out_specs=[row(D_FF), row(D_FF), row(D_MODEL), pl.BlockSpec((SUBLANES, D_MODEL), lambda i: (0, 0))],
        out_shape=[SDS((s, D_FF), BF16), SDS((s, D_FF), BF16), SDS((s, D_MODEL), F32),
                   SDS((SUBLANES, D_MODEL), F32)],
        compiler_params=_params(("arbitrary",)))(x1, dpre2, w_up, w_down, xhat1, rstd1, g1)


def _grad_tn(a, b, name, out_cols, stacked):
    s, ka = a.shape
    n = b.shape[1]
    ts = _row_tile(s, 512)
    if stacked:
        tka, tn = ka, out_cols
        grid = (1, n // tn, s // ts)
        out_shape = SDS((n // tn, ka, tn), F32)
        out_spec = pl.BlockSpec((None, tka, tn), lambda r, c, t: (c, 0, 0))
    else:
        tka, tn = min(ka, 1024), n
        grid = (ka // tka, 1, s // ts)
        out_shape = SDS((ka, n), F32)
        out_spec = pl.BlockSpec((tka, tn), lambda r, c, t: (r, 0))

    def body(a_ref, b_ref, o_ref):
        @pl.when(pl.program_id(2) == 0)
        def _():
            o_ref[...] = jnp.zeros_like(o_ref)

        o_ref[...] += lax.dot_general(a_ref[...].astype(BF16), b_ref[...].astype(BF16), TN_DIMS,
                                      preferred_element_type=F32)

    return pl.pallas_call(
        body, name=name, grid=grid,
        in_specs=[pl.BlockSpec((ts, tka), lambda r, c, t: (t, r)),
                  pl.BlockSpec((ts, tn), lambda r, c, t: (t, c))],
        out_specs=out_spec, out_shape=out_shape,
        compiler_params=_params(("parallel", "parallel", "arbitrary")))(a, b)


def _dmix(dpre1, w_out):
    s = dpre1.shape[0]
    tm = _row_tile(s, 512)

    def body(d_ref, w_ref, dc_ref, da_ref):
        db = d_ref[...].astype(BF16)
        dc_ref[...] = lax.dot_general(db, w_ref[0:CONV_WIDTH, :], NT_DIMS, preferred_element_type=F32)
        da_ref[...] = lax.dot_general(db, w_ref[CONV_WIDTH:, :], NT_DIMS, preferred_element_type=F32)

    return pl.pallas_call(
        body, name="dmix", grid=(s // tm,),
        in_specs=[pl.BlockSpec((tm, D_MODEL), lambda i: (i, 0)),
                  pl.BlockSpec((D_MODEL, D_MODEL), lambda i: (0, 0))],
        out_specs=[pl.BlockSpec((tm, CONV_WIDTH), lambda i: (i, 0)),
                   pl.BlockSpec((tm, ATTN_WIDTH), lambda i: (i, 0))],
        out_shape=[SDS((s, CONV_WIDTH), F32), SDS((s, ATTN_WIDTH), F32)],
        compiler_params=_params(("parallel",)))(dpre1, w_out)


def _grad_x(dproj, w_in, dpre1):
    s = dproj.shape[0]
    tm = _row_tile(s, 512)

    def body(dp_ref, w_ref, d1_ref, o_ref):
        acc = ALPHA * d1_ref[...]
        for k in range(N_CHIPS):
            acc = acc + lax.dot_general(dp_ref[:, IN_SHARD * k:IN_SHARD * (k + 1)], w_ref[k], NT_DIMS,
                                        preferred_element_type=F32)
        o_ref[...] = acc

    return pl.pallas_call(
        body, name="grad_x", grid=(s // tm,),
        in_specs=[pl.BlockSpec((tm, IN_COLS), lambda i: (i, 0)),
                  pl.BlockSpec((N_CHIPS, D_MODEL, IN_SHARD), lambda i: (0, 0, 0)),
                  pl.BlockSpec((tm, D_MODEL), lambda i: (i, 0))],
        out_specs=pl.BlockSpec((tm, D_MODEL), lambda i: (i, 0)),
        out_shape=SDS((s, D_MODEL), F32),
        compiler_params=_params(("parallel",)))(dproj, w_in, dpre1)


def _adamw(w, g, m, v, name):
    r, c = w.shape
    tr = _row_tile(r, 256)

    def body(w_ref, g_ref, m_ref, v_ref, d_ref, nm_ref, nv_ref):
        g_v = g_ref[...]
        nm = ADAM_B1 * m_ref[...] + (1.0 - ADAM_B1) * g_v
        nv = ADAM_B2 * v_ref[...] + (1.0 - ADAM_B2) * (g_v * g_v)
        m_hat = nm / (1.0 - ADAM_B1 ** ADAM_STEP)
        v_hat = nv / (1.0 - ADAM_B2 ** ADAM_STEP)
        d_ref[...] = -ADAM_LR * (m_hat / (jnp.sqrt(v_hat) + ADAM_EPS) + ADAM_WD * w_ref[...])
        nm_ref[...] = nm
        nv_ref[...] = nv

    spec = lambda: pl.BlockSpec((tr, c), lambda i: (i, 0))
    return pl.pallas_call(
        body, name=name, grid=(r // tr,),
        in_specs=[spec(), spec(), spec(), spec()], out_specs=[spec(), spec(), spec()],
        out_shape=[SDS((r, c), F32)] * 3, compiler_params=_params(("parallel",)))(w, g, m, v)


def _add_halves(c_idx, grad, recv, name):
    _, _, h, cols = grad.shape
    th = _row_tile(h, 256)

    def body(c_ref, g_ref, r_ref, o_ref):
        o_ref[...] = g_ref[...] + r_ref[...]

    grid_spec = pltpu.PrefetchScalarGridSpec(
        num_scalar_prefetch=1, grid=(N_CHIPS, h // th),
        in_specs=[pl.BlockSpec((None, None, th, cols), lambda k, t, c_ref: (k, c_ref[0], t, 0)),
                  pl.BlockSpec((None, th, cols), lambda k, t, c_ref: (k, t, 0))],
        out_specs=pl.BlockSpec((None, th, cols), lambda k, t, c_ref: (k, t, 0)))
    return pl.pallas_call(
        body, name=name, grid_spec=grid_spec, out_shape=SDS((N_CHIPS, h, cols), F32),
        compiler_params=_params(("parallel", "parallel")))(c_idx, grad, recv)


def _add_chips(k_idx, halves, recv, name):
    _, h, cols = halves.shape
    th = _row_tile(h, 256)

    def body(k_ref, own_ref, r_ref, o_ref):
        o_ref[...] = own_ref[...] + r_ref[0] + r_ref[1] + r_ref[2]

    grid_spec = pltpu.PrefetchScalarGridSpec(
        num_scalar_prefetch=1, grid=(h // th,),
        in_specs=[pl.BlockSpec((None, th, cols), lambda t, k_ref: (k_ref[0], t, 0)),
                  pl.BlockSpec((3, th, cols), lambda t, k_ref: (0, t, 0))],
        out_specs=pl.BlockSpec((th, cols), lambda t, k_ref: (t, 0)))
    return pl.pallas_call(
        body, name=name, grid_spec=grid_spec, out_shape=SDS((h, cols), F32),
        compiler_params=_params(("parallel",)))(k_idx, halves, recv)


ANY = pl.BlockSpec(memory_space=pl.ANY)


def _gather_weights(shards, conv_shard):
    n = len(shards)

    def body(*refs):
        ins, conv_in = refs[:n], refs[n]
        outs, conv_out = refs[n + 1:2 * n + 1], refs[2 * n + 1]
        send_sems, recv_sems, local_sems = refs[2 * n + 2:]
        x, y, c = _position()
        k = 2 * x + y
        sibling = (x, y, 1 - c)
        chips = [(_flip(x, fx), _flip(y, fy)) for fx, fy in CHIP_FLIPS]

        def half(a, rows_of_core):
            h = shards[a].shape[0] // 2
            return pl.ds(pl.multiple_of(rows_of_core * h, h), h)

        def remote(src, dst, idx, target):
            return pltpu.make_async_remote_copy(src_ref=src, dst_ref=dst, send_sem=send_sems.at[idx],
                                                recv_sem=recv_sems.at[idx], device_id=target,
                                                device_id_type=MESH)

        local = [pltpu.make_async_copy(ins[a], outs[a].at[k], local_sems.at[a]) for a in range(n)]
        local.append(pltpu.make_async_copy(conv_in, conv_out.at[k], local_sems.at[n]))
        for cp in local:
            cp.start()
        started = []
        for a in range(n):
            for j, (tx, ty) in enumerate(chips):
                cp = remote(ins[a].at[half(a, c)], outs[a].at[k, half(a, c)], 6 * a + j, (tx, ty, c))
                cp.start()
                started.append(cp)
        for j, (tx, ty) in enumerate(chips):
            cp = remote(conv_in, conv_out.at[k], 6 * n + j, (tx, ty, c))
            cp.start()
            started.append(cp)
        for a in range(n):
            for j, (tx, ty) in enumerate(chips):
                kj = 2 * tx + ty
                landed = outs[a].at[kj, half(a, c)]
                remote(landed, landed, 6 * a + j, sibling).wait_recv()
                cp = remote(landed, landed, 6 * a + 3 + j, sibling)
                cp.start()
                started.append(cp)
        for a in range(n):
            for j, (tx, ty) in enumerate(chips):
                kj = 2 * tx + ty
                other = outs[a].at[kj, half(a, 1 - c)]
                remote(other, other, 6 * a + 3 + j, sibling).wait_recv()
        for j, (tx, ty) in enumerate(chips):
            kj = 2 * tx + ty
            remote(conv_out.at[kj], conv_out.at[kj], 6 * n + j, sibling).wait_recv()
        for cp in started:
            cp.wait_send()
        for cp in local:
            cp.wait()

    out_shape = [SDS((N_CHIPS,) + w.shape, w.dtype) for w in shards]
    out_shape.append(SDS((N_CHIPS,) + conv_shard.shape, conv_shard.dtype))
    n_sems = 6 * n + 3
    return pl.pallas_call(
        body, name="gather_weights", in_specs=[ANY] * (n + 1), out_specs=[ANY] * (n + 1),
        out_shape=out_shape,
        scratch_shapes=[pltpu.SemaphoreType.DMA((n_sems,)), pltpu.SemaphoreType.DMA((n_sems,)),
                        pltpu.SemaphoreType.DMA((n + 1,))])(*shards, conv_shard)


def _exchange_sibling_halves(grads):
    n = len(grads)

    def body(*refs):
        ins, outs = refs[:n], refs[n:2 * n]
        send_sems, recv_sems = refs[2 * n:]
        x, y, c = _position()
        sibling = (x, y, 1 - c)
        started = []
        for a in range(n):
            h = grads[a].shape[1] // 2
            theirs = pl.ds(pl.multiple_of((1 - c) * h, h), h)
            for k in range(N_CHIPS):
                cp = pltpu.make_async_remote_copy(
                    src_ref=ins[a].at[k, theirs], dst_ref=outs[a].at[k], send_sem=send_sems.at[4 * a + k],
                    recv_sem=recv_sems.at[4 * a + k], device_id=sibling, device_id_type=MESH)
                cp.start()
                started.append(cp)
        for cp in started:
            cp.wait_recv()
        for cp in started:
            cp.wait_send()

    out_shape = [SDS((N_CHIPS, g.shape[1] // 2, g.shape[2]), g.dtype) for g in grads]
    return pl.pallas_call(
        body, name="exchange_sibling_halves", in_specs=[ANY] * n, out_specs=[ANY] * n, out_shape=out_shape,
        scratch_shapes=[pltpu.SemaphoreType.DMA((4 * n,)), pltpu.SemaphoreType.DMA((4 * n,))])(*grads)


def _exchange_chip_pieces(halves):
    n = len(halves)

    def body(*refs):
        ins, outs = refs[:n], refs[n:2 * n]
        send_sems, recv_sems = refs[2 * n:]
        x, y, c = _position()
        started = []
        for a in range(n):
            for j, (fx, fy) in enumerate(CHIP_FLIPS):
                tx, ty = _flip(x, fx), _flip(y, fy)
                cp = pltpu.make_async_remote_copy(
                    src_ref=ins[a].at[2 * tx + ty], dst_ref=outs[a].at[j], send_sem=send_sems.at[3 * a + j],
                    recv_sem=recv_sems.at[3 * a + j], device_id=(tx, ty, c), device_id_type=MESH)
                cp.start()
                started.append(cp)
        for cp in started:
            cp.wait_recv()
        for cp in started:
            cp.wait_send()

    out_shape = [SDS((3,) + h.shape[1:], h.dtype) for h in halves]
    return pl.pallas_call(
        body, name="exchange_chip_pieces", in_specs=[ANY] * n, out_specs=[ANY] * n, out_shape=out_shape,
        scratch_shapes=[pltpu.SemaphoreType.DMA((3 * n,)), pltpu.SemaphoreType.DMA((3 * n,))])(*halves)


def _share_with_sibling(pieces):
    n = len(pieces)

    def body(*refs):
        ins, outs = refs[:n], refs[n:2 * n]
        send_sems, recv_sems, local_sems = refs[2 * n:]
        x, y, c = _position()
        sibling = (x, y, 1 - c)
        started, local = [], []
        for a in range(n):
            h = pieces[a].shape[0]
            mine = pl.ds(pl.multiple_of(c * h, h), h)
            cp = pltpu.make_async_copy(ins[a], outs[a].at[mine], local_sems.at[a])
            cp.start()
            local.append(cp)
            cp = pltpu.make_async_remote_copy(
                src_ref=ins[a], dst_ref=outs[a].at[mine], send_sem=send_sems.at[a], recv_sem=recv_sems.at[a],
                device_id=sibling, device_id_type=MESH)
            cp.start()
            started.append(cp)
        for a in range(n):
            h = pieces[a].shape[0]
            theirs = outs[a].at[pl.ds(pl.multiple_of((1 - c) * h, h), h)]
            pltpu.make_async_remote_copy(
                src_ref=theirs, dst_ref=theirs, send_sem=send_sems.at[a], recv_sem=recv_sems.at[a],
                device_id=sibling, device_id_type=MESH).wait_recv()
        for cp in started:
            cp.wait_send()
        for cp in local:
            cp.wait()

    out_shape = [SDS((2 * p.shape[0], p.shape[1]), p.dtype) for p in pieces]
    return pl.pallas_call(
        body, name="share_with_sibling", in_specs=[ANY] * n, out_specs=[ANY] * n, out_shape=out_shape,
        scratch_shapes=[pltpu.SemaphoreType.DMA((n,)), pltpu.SemaphoreType.DMA((n,)),
                        pltpu.SemaphoreType.DMA((n,))])(*pieces)


def _all_reduce_small(vec):
    n_dev = 2 * N_CHIPS

    def body(v_ref, o_ref, buf, send_sems, recv_sems):
        x, y, c = _position()
        me = 4 * x + 2 * y + c
        buf[me] = v_ref[...]
        started = []
        for idx, (fx, fy, fc) in enumerate(DEVICE_FLIPS):
            cp = pltpu.make_async_remote_copy(
                src_ref=v_ref, dst_ref=buf.at[me], send_sem=send_sems.at[idx], recv_sem=recv_sems.at[idx],
                device_id=(_flip(x, fx), _flip(y, fy), _flip(c, fc)), device_id_type=MESH)
            cp.start()
            started.append(cp)
        for idx, (fx, fy, fc) in enumerate(DEVICE_FLIPS):
            src = 4 * _flip(x, fx) + 2 * _flip(y, fy) + _flip(c, fc)
            pltpu.make_async_remote_copy(
                src_ref=v_ref, dst_ref=buf.at[src], send_sem=send_sems.at[idx], recv_sem=recv_sems.at[idx],
                device_id=(x, y, c), device_id_type=MESH).wait_recv()
        for cp in started:
            cp.wait_send()
        acc = buf[0]
        for d in range(1, n_dev):
            acc = acc + buf[d]
        o_ref[...] = acc

    vmem = pl.BlockSpec(memory_space=pltpu.VMEM)
    return pl.pallas_call(
        body, name="all_reduce_small", in_specs=[vmem], out_specs=vmem, out_shape=SDS(vec.shape, vec.dtype),
        scratch_shapes=[pltpu.VMEM((n_dev,) + vec.shape, vec.dtype),
                        pltpu.SemaphoreType.DMA((n_dev - 1,)), pltpu.SemaphoreType.DMA((n_dev - 1,))])(vec)


def _constants():
    r = jnp.arange(2 * KEY_BLOCK)[:, None] % KEY_BLOCK
    c = jnp.arange(2 * KEY_BLOCK)[None, :]
    later = jnp.where(c < KEY_BLOCK, r > c, True).astype(BF16)
    earlier = jnp.where(c < KEY_BLOCK, r < c, True).astype(BF16)
    gr = (jnp.arange(2 * LANES)[:, None] % LANES) // GROUP
    gc = jnp.arange(LANES)[None, :] // GROUP
    gmat = (gr == gc).astype(BF16)
    return later, earlier, gmat


def _rows(v):
    return v.reshape(-1, LANES)


def kernel(x, w_in, conv_w, g_conv, g_attn, w_out, ln1_g, ln1_b, w_up, w_down, ln2_g, ln2_b, loss_target, m_w_in, m_conv_w, m_g_conv, m_g_attn, m_w_out, m_ln1_g, m_ln1_b, m_w_up, m_w_down, m_ln2_g, m_ln2_b, v_w_in, v_conv_w, v_g_conv, v_g_attn, v_w_out, v_ln1_g, v_ln1_b, v_w_up, v_w_down, v_ln2_g, v_ln2_b):
    xs, target = x[0], loss_target[0]
    mesh_x, mesh_y, mesh_c = _position()
    c_idx = jnp.reshape(mesh_c, (1,)).astype(jnp.int32)
    k_idx = jnp.reshape(2 * mesh_x + mesh_y, (1,)).astype(jnp.int32)
    tri_later, tri_earlier, gmat = _constants()

    pad_rows = lambda a: jnp.pad(a, ((0, SUBLANES - a.shape[0]), (0, 0)))
    big = [_cast_bf16(w[0], "cast_" + nm) for w, nm in ((w_in, "w_in"), (w_out, "w_out"), (w_up, "w_up"), (w_down, "w_down"))]
    w_in_f, w_out_f, w_up_f, w_down_f, conv_f = _gather_weights(big, pad_rows(conv_w[0]))
    w_out_f = w_out_f.reshape(D_MODEL, D_MODEL)
    taps = jnp.transpose(conv_f, (1, 0, 2)).reshape(SUBLANES, CONV_WIDTH)

    gates, qkv = _proj(xs, w_in_f)
    ycn = _conv_fwd(gates, taps, g_conv, gmat)
    o, yan, tot = _attn_fwd(qkv, g_attn, tri_later, gmat)
    x1, xhat1, rstd1 = _mix_ln1(ycn, yan, w_out_f, xs, ln1_g, ln1_b)
    dpre2, ln2_sums, loss_sum = _mlp_fwd_loss(x1, w_up_f, w_down_f, target, ln2_g, ln2_b)

    hid, dup, dpre1, ln1_sums = _mlp_bwd_ln1(x1, dpre2, w_up_f, w_down_f, xhat1, rstd1, ln1_g)
    gw_up = _grad_tn(x1, dup, "grad_w_up", FF_SHARD, True)
    gw_down = _grad_tn(hid, dpre2, "grad_w_down", D_MODEL, False).reshape(N_CHIPS, FF_SHARD, D_MODEL)
    gw_out = jnp.concatenate([_grad_tn(ycn, dpre1, "grad_w_out_conv", D_MODEL, False),
                              _grad_tn(yan, dpre1, "grad_w_out_attn", D_MODEL, False)], axis=0)
    gw_out = gw_out.reshape(N_CHIPS, D_MODEL // N_CHIPS, D_MODEL)
    dycn, dyan = _dmix(dpre1, w_out_f)
    dq, dk, dv, gattn_sums = _attn_bwd(qkv, o, tot, dyan, g_attn, tri_earlier, gmat)
    dbg, dy, conv_sums = _conv_bwd_gate(gates, dycn, taps, g_conv, gmat)
    dproj = _dproj_assemble(gates, dy, dbg, dq, dk, dv, taps)
    gw_in = _grad_tn(xs, dproj, "grad_w_in", IN_SHARD, True)
    grad_x = _grad_x(dproj, w_in_f, dpre1)

    grads = [gw_in, gw_out, gw_up, gw_down]
    names = ["w_in", "w_out", "w_up", "w_down"]
    from_sibling = _exchange_sibling_halves(grads)
    halves = [_add_halves(c_idx, g.reshape(N_CHIPS, 2, g.shape[1] // 2, g.shape[2]), r, "add_halves_" + nm)
              for g, r, nm in zip(grads, from_sibling, names)]
    from_chips = _exchange_chip_pieces(halves)
    pieces = [_add_chips(k_idx, h, r, "add_chips_" + nm) for h, r, nm in zip(halves, from_chips, names)]
    g_w_in, g_w_out, g_w_up, g_w_down = _share_with_sibling(pieces)

    conv_rows = jnp.transpose(conv_sums[0:3].reshape(3, N_CHIPS, LANES), (1, 0, 2)).reshape(3 * N_CHIPS, LANES)
    small = jnp.concatenate([
        loss_sum, _rows(conv_sums[3]), _rows(gattn_sums[0]), _rows(ln1_sums[0]), _rows(ln1_sums[1]),
        _rows(ln2_sums[0]), _rows(ln2_sums[1]), conv_rows,
        jnp.zeros((SMALL_ROWS - ROW_CONVW - 3 * N_CHIPS, LANES), F32)], axis=0)
    total = _all_reduce_small(small)
    loss = total[ROW_LOSS, 0]
    g_conv_w = lax.dynamic_slice(total, (ROW_CONVW + 3 * k_idx[0], 0), (3, LANES))

    def pack(gc_, ga_, l1g, l1b, l2g, l2b, cw):
        return jnp.concatenate([_rows(gc_), _rows(ga_), _rows(l1g), _rows(l1b), _rows(l2g), _rows(l2b), cw[0],
                                jnp.zeros((PARAM_ROWS + SUBLANES - ROW_CONVW - 3, LANES), F32)], axis=0)

    small_w = pack(g_conv, g_attn, ln1_g, ln1_b, ln2_g, ln2_b, conv_w)
    small_m = pack(m_g_conv, m_g_attn, m_ln1_g, m_ln1_b, m_ln2_g, m_ln2_b, m_conv_w)
    small_v = pack(v_g_conv, v_g_attn, v_ln1_g, v_ln1_b, v_ln2_g, v_ln2_b, v_conv_w)
    small_g = jnp.concatenate([total[ROW_GCONV:ROW_CONVW], g_conv_w,
                               jnp.zeros((PARAM_ROWS + SUBLANES - ROW_CONVW - 3, LANES), F32)], axis=0)
    small_out = _adamw(small_w, small_g, small_m, small_v, "adamw_small")

    def unpack(p):
        off = ROW_GCONV
        vec = lambda a, b: p[a - off:b - off].reshape(1, -1)
        return {"g_conv": vec(ROW_GCONV, ROW_GATTN), "g_attn": vec(ROW_GATTN, ROW_LN1G),
                "ln1_g": vec(ROW_LN1G, ROW_LN1B), "ln1_b": vec(ROW_LN1B, ROW_LN2G),
                "ln2_g": vec(ROW_LN2G, ROW_LN2B), "ln2_b": vec(ROW_LN2B, ROW_CONVW),
                "conv_w": p[ROW_CONVW - off:ROW_CONVW - off + 3][None]}

    big_out = {
        "w_in": _adamw(w_in[0], g_w_in, m_w_in[0], v_w_in[0], "adamw_w_in"),
        "w_out": _adamw(w_out[0], g_w_out, m_w_out[0], v_w_out[0], "adamw_w_out"),
        "w_up": _adamw(w_up[0], g_w_up, m_w_up[0], v_w_up[0], "adamw_w_up"),
        "w_down": _adamw(w_down[0], g_w_down, m_w_down[0], v_w_down[0], "adamw_w_down"),
    }
    big_grads = {"w_in": g_w_in, "w_out": g_w_out, "w_up": g_w_up, "w_down": g_w_down}
    order = ["w_in", "conv_w", "g_conv", "g_attn", "w_out", "ln1_g", "ln1_b", "w_up", "w_down", "ln2_g", "ln2_b"]
    small_parts = [unpack(small_g)] + [unpack(p) for p in small_out]

    def leaf(kind, name):
        if name in big_out:
            return (big_grads[name] if kind == 0 else big_out[name][kind - 1])[None]
        return small_parts[kind][name]

    outs = [loss, grad_x[None]]
    for kind in range(4):
        outs.extend(leaf(kind, name) for name in order)
    return tuple(outs)
```

```python
import functools

import jax
import jax.numpy as jnp
from jax import lax
from jax.experimental import pallas as pl
from jax.experimental.pallas import tpu as pltpu

F32 = jnp.float32
BF16 = jnp.bfloat16
SDS = jax.ShapeDtypeStruct

D_MODEL = 1024
CONV_WIDTH = 512
ATTN_WIDTH = 512
GROUP = 64
GATE_COLS = 3 * CONV_WIDTH
QKV_COLS = 3 * ATTN_WIDTH
IN_COLS = GATE_COLS + QKV_COLS
D_FF = 4 * D_MODEL
N_CHIPS = 4
IN_SHARD = IN_COLS // N_CHIPS
FF_SHARD = D_FF // N_CHIPS
ALPHA = float(2.0 ** 0.25)
LN_EPS = 1e-5
RMS_EPS = 1e-6
ATTN_SCALE = GROUP ** -0.5
ADAM_LR = 0.001
ADAM_B1 = 0.9
ADAM_B2 = 0.999
ADAM_EPS = 1e-08
ADAM_WD = 0.01
ADAM_STEP = 10

LANES = 128
SUBLANES = 8
KEY_BLOCK = 128
ATTN_Q_TILE = 512
ATTN_KEY_BLOCKS = 2
VMEM_LIMIT = 56 * 1024 * 1024

MESH = pl.DeviceIdType.MESH
CHIP_FLIPS = ((1, 0), (0, 1), (1, 1))
DEVICE_FLIPS = tuple((fx, fy, fc) for fx in (0, 1) for fy in (0, 1) for fc in (0, 1))[1:]
NT_DIMS = (((1,), (1,)), ((), ()))
TN_DIMS = (((0,), (0,)), ((), ()))

ROW_LOSS = 0
ROW_GCONV = 8
ROW_GATTN = 12
ROW_LN1G = 16
ROW_LN1B = 24
ROW_LN2G = 32
ROW_LN2B = 40
ROW_CONVW = 48
SMALL_ROWS = 64
PARAM_ROWS = 48


def _params(sem=None):
    return pltpu.CompilerParams(dimension_semantics=sem, vmem_limit_bytes=VMEM_LIMIT)


def _flip(v, f):
    return 1 - v if f else v


def _position():
    return lax.axis_index("x"), lax.axis_index("y"), lax.axis_index("c")


def _hilo(v):
    hi = v.astype(BF16)
    lo = (v - hi.astype(F32)).astype(BF16)
    return jnp.concatenate([hi, lo], axis=1)


def _hilo_dot(v, mat):
    return jnp.dot(_hilo(v), mat, preferred_element_type=F32)


def _group_sum(v, gmat):
    parts = [_hilo_dot(v[:, LANES * j:LANES * (j + 1)], gmat) for j in range(v.shape[1] // LANES)]
    return parts[0] if len(parts) == 1 else jnp.concatenate(parts, axis=1)


def _softplus_terms(z):
    sp = jnp.log1p(jnp.exp(-jnp.abs(z)))
    log_beta = jnp.minimum(z, 0.0) - sp
    return log_beta, log_beta - z


def _layer_norm_fwd(pre, g, b):
    mu = jnp.mean(pre, axis=-1, keepdims=True)
    d = pre - mu
    var = jnp.mean(d * d, axis=-1, keepdims=True)
    rstd = lax.rsqrt(var + LN_EPS)
    xhat = d * rstd
    return xhat * g + b, xhat, rstd


def _layer_norm_bwd(dy, xhat, rstd, g):
    dxh = dy * g
    m1 = jnp.mean(dxh, axis=-1, keepdims=True)
    m2 = jnp.mean(dxh * xhat, axis=-1, keepdims=True)
    return rstd * (dxh - m1 - xhat * m2)


def _row_tile(s, want):
    return min(s, want)


def _cast_bf16(w, name):
    r, c = w.shape
    tr = _row_tile(r, 256)

    def body(w_ref, o_ref):
        o_ref[...] = w_ref[...].astype(BF16)

    return pl.pallas_call(
        body, name=name, grid=(r // tr,),
        in_specs=[pl.BlockSpec((tr, c), lambda i: (i, 0))],
        out_specs=pl.BlockSpec((tr, c), lambda i: (i, 0)),
        out_shape=SDS((r, c), BF16), compiler_params=_params(("parallel",)))(w)


def _proj(x, w_in):
    s = x.shape[0]
    tm = _row_tile(s, 512)

    def body(x_ref, w_ref, gates_ref, qkv_ref):
        xb = x_ref[...].astype(BF16)
        for k in range(N_CHIPS):
            acc = jnp.dot(xb, w_ref[k], preferred_element_type=F32)
            if k < 2:
                gates_ref[:, IN_SHARD * k:IN_SHARD * (k + 1)] = acc
            else:
                qkv_ref[:, IN_SHARD * (k - 2):IN_SHARD * (k - 1)] = acc.astype(BF16)

    return pl.pallas_call(
        body, name="proj", grid=(s // tm,),
        in_specs=[pl.BlockSpec((tm, D_MODEL), lambda i: (i, 0)),
                  pl.BlockSpec((N_CHIPS, D_MODEL, IN_SHARD), lambda i: (0, 0, 0))],
        out_specs=[pl.BlockSpec((tm, GATE_COLS), lambda i: (i, 0)),
                   pl.BlockSpec((tm, QKV_COLS), lambda i: (i, 0))],
        out_shape=[SDS((s, GATE_COLS), F32), SDS((s, QKV_COLS), BF16)],
        compiler_params=_params(("parallel",)))(x, w_in)


def _conv_forward_values(g_ref, halo_ref, taps_ref, first_block):
    gates = g_ref[...]
    tr = gates.shape[0]
    bg = gates[:, :CONV_WIDTH]
    cg = gates[:, CONV_WIDTH:2 * CONV_WIDTH]
    h = gates[:, 2 * CONV_WIDTH:]
    u = cg * h

    def prev(r):
        v = halo_ref[r:r + 1, CONV_WIDTH:2 * CONV_WIDTH] * halo_ref[r:r + 1, 2 * CONV_WIDTH:GATE_COLS]
        return jnp.where(first_block, 0.0, v)

    row = lax.broadcasted_iota(jnp.int32, (tr, CONV_WIDTH), 0)
    u1 = jnp.where(row == 0, prev(7), pltpu.roll(u, 1, 0))
    u2 = jnp.where(row == 0, prev(6), jnp.where(row == 1, prev(7), pltpu.roll(u, 2, 0)))
    y = taps_ref[0:1, :] * u2 + taps_ref[1:2, :] * u1 + taps_ref[2:3, :] * u
    return bg, cg, h, u, u1, u2, y


def _conv_fwd(gates, taps, g_conv, gmat):
    s = gates.shape[0]
    tr = _row_tile(s, 512)
    hb = tr // SUBLANES

    def body(g_ref, halo_ref, taps_ref, gain_ref, gmat_ref, out_ref):
        i = pl.program_id(0)
        bg, _, _, _, _, _, y = _conv_forward_values(g_ref, halo_ref, taps_ref, i == 0)
        yc = bg * y
        ms = _group_sum(yc * yc, gmat_ref[...]) * (1.0 / GROUP)
        out_ref[...] = (yc * lax.rsqrt(ms + RMS_EPS) * gain_ref[...]).astype(BF16)

    return pl.pallas_call(
        body, name="conv_fwd", grid=(s // tr,),
        in_specs=[pl.BlockSpec((tr, GATE_COLS), lambda i: (i, 0)),
                  pl.BlockSpec((SUBLANES, GATE_COLS), lambda i: (jnp.maximum(i * hb - 1, 0), 0)),
                  pl.BlockSpec((SUBLANES, CONV_WIDTH), lambda i: (0, 0)),
                  pl.BlockSpec((1, CONV_WIDTH), lambda i: (0, 0)),
                  pl.BlockSpec((2 * LANES, LANES), lambda i: (0, 0))],
        out_specs=pl.BlockSpec((tr, CONV_WIDTH), lambda i: (i, 0)),
        out_shape=SDS((s, CONV_WIDTH), BF16),
        compiler_params=_params(("parallel",)))(gates, gates, taps, g_conv, gmat)


def _conv_bwd_gate(gates, dycn, taps, g_conv, gmat):
    s = gates.shape[0]
    tr = _row_tile(s, 512)
    hb = tr // SUBLANES

    def body(g_ref, halo_ref, dn_ref, taps_ref, gain_ref, gmat_ref, dbg_ref, dy_ref, sums_ref):
        i = pl.program_id(0)
        bg, _, _, u, u1, u2, y = _conv_forward_values(g_ref, halo_ref, taps_ref, i == 0)
        gmat_v = gmat_ref[...]
        yc = bg * y
        rstd = lax.rsqrt(_group_sum(yc * yc, gmat_v) * (1.0 / GROUP) + RMS_EPS)
        n = yc * rstd
        dout = dn_ref[...]
        dn = dout * gain_ref[...]
        dyc = rstd * (dn - n * (_group_sum(dn * n, gmat_v) * (1.0 / GROUP)))
        dbg_ref[...] = (dyc * y).astype(BF16)
        dy = dyc * bg
        dy_ref[...] = dy

        @pl.when(i == 0)
        def _():
            sums_ref[...] = jnp.zeros_like(sums_ref)

        sums_ref[0:1, :] += jnp.sum(dy * u2, axis=0, keepdims=True)
        sums_ref[1:2, :] += jnp.sum(dy * u1, axis=0, keepdims=True)
        sums_ref[2:3, :] += jnp.sum(dy * u, axis=0, keepdims=True)
        sums_ref[3:4, :] += jnp.sum(dout * n, axis=0, keepdims=True)

    return pl.pallas_call(
        body, name="conv_bwd_gate", grid=(s // tr,),
        in_specs=[pl.BlockSpec((tr, GATE_COLS), lambda i: (i, 0)),
                  pl.BlockSpec((SUBLANES, GATE_COLS), lambda i: (jnp.maximum(i * hb - 1, 0), 0)),
                  pl.BlockSpec((tr, CONV_WIDTH), lambda i: (i, 0)),
                  pl.BlockSpec((SUBLANES, CONV_WIDTH), lambda i: (0, 0)),
                  pl.BlockSpec((1, CONV_WIDTH), lambda i: (0, 0)),
                  pl.BlockSpec((2 * LANES, LANES), lambda i: (0, 0))],
        out_specs=[pl.BlockSpec((tr, CONV_WIDTH), lambda i: (i, 0)),
                   pl.BlockSpec((tr, CONV_WIDTH), lambda i: (i, 0)),
                   pl.BlockSpec((SUBLANES, CONV_WIDTH), lambda i: (0, 0))],
        out_shape=[SDS((s, CONV_WIDTH), BF16), SDS((s, CONV_WIDTH), F32), SDS((SUBLANES, CONV_WIDTH), F32)],
        compiler_params=_params(("arbitrary",)))(gates, gates, dycn, taps, g_conv, gmat)


def _dproj_assemble(gates, dy, dbg, dq, dk, dv, taps):
    s = gates.shape[0]
    tr = _row_tile(s, 512)
    hb = tr // SUBLANES
    last = s // SUBLANES - 1
    n_blocks = s // tr

    def body(g_ref, dy_ref, halo_ref, dbg_ref, dq_ref, dk_ref, dv_ref, taps_ref, out_ref):
        i = pl.program_id(0)
        gates_v = g_ref[...]
        cg = gates_v[:, CONV_WIDTH:2 * CONV_WIDTH]
        h = gates_v[:, 2 * CONV_WIDTH:]
        dy_v = dy_ref[...]
        last_block = i == n_blocks - 1
        nxt = lambda r: jnp.where(last_block, 0.0, halo_ref[r:r + 1, :])
        row = lax.broadcasted_iota(jnp.int32, (tr, CONV_WIDTH), 0)
        d1 = jnp.where(row == tr - 1, nxt(0), pltpu.roll(dy_v, tr - 1, 0))
        d2 = jnp.where(row == tr - 1, nxt(1), jnp.where(row == tr - 2, nxt(0), pltpu.roll(dy_v, tr - 2, 0)))
        du = taps_ref[2:3, :] * dy_v + taps_ref[1:2, :] * d1 + taps_ref[0:1, :] * d2
        out_ref[:, 0:CONV_WIDTH] = dbg_ref[...]
        out_ref[:, CONV_WIDTH:2 * CONV_WIDTH] = (du * h).astype(BF16)
        out_ref[:, 2 * CONV_WIDTH:GATE_COLS] = (du * cg).astype(BF16)
        out_ref[:, GATE_COLS:GATE_COLS + ATTN_WIDTH] = dq_ref[...]
        out_ref[:, GATE_COLS + ATTN_WIDTH:GATE_COLS + 2 * ATTN_WIDTH] = dk_ref[...].astype(BF16)
        out_ref[:, GATE_COLS + 2 * ATTN_WIDTH:] = dv_ref[...].astype(BF16)

    row_spec = lambda w: pl.BlockSpec((tr, w), lambda i: (i, 0))
    return pl.pallas_call(
        body, name="dproj_assemble", grid=(s // tr,),
        in_specs=[row_spec(GATE_COLS), row_spec(CONV_WIDTH),
                  pl.BlockSpec((SUBLANES, CONV_WIDTH), lambda i: (jnp.minimum((i + 1) * hb, last), 0)),
                  row_spec(CONV_WIDTH), row_spec(ATTN_WIDTH), row_spec(ATTN_WIDTH), row_spec(ATTN_WIDTH),
                  pl.BlockSpec((SUBLANES, CONV_WIDTH), lambda i: (0, 0))],
        out_specs=row_spec(IN_COLS),
        out_shape=SDS((s, IN_COLS), BF16),
        compiler_params=_params(("parallel",)))(gates, dy, dy, dbg, dq, dk, dv, taps)


def _stack_heads(rows, nb):
    lane = lax.broadcasted_iota(jnp.int32, (1, LANES), 1)
    zero = jnp.zeros((KEY_BLOCK, LANES), rows.dtype)
    parts = []
    for blk in range(nb):
        r = rows[blk * KEY_BLOCK:(blk + 1) * KEY_BLOCK]
        parts.append(jnp.where(lane < GROUP, r, zero))
        parts.append(jnp.where(lane < GROUP, zero, r))
    return jnp.concatenate(parts, axis=0)


def _stack_hilo(v, n_cols):
    return jnp.concatenate([_hilo(v[:, c * KEY_BLOCK:(c + 1) * KEY_BLOCK]) for c in range(n_cols)], axis=0)


def _causal_mask(tq, nb, diag_base):
    shape = (tq, 2 * nb * KEY_BLOCK)
    row = lax.broadcasted_iota(jnp.int32, shape, 0)
    col = lax.broadcasted_iota(jnp.int32, shape, 1)
    key = diag_base + (col // (2 * KEY_BLOCK)) * KEY_BLOCK + col % KEY_BLOCK
    return key < row


ANY = pl.BlockSpec(memory_space=pl.ANY)


def _remote_copy(src, dst, sems, idx, target):
    return pltpu.make_async_remote_copy(src_ref=src, dst_ref=dst, send_sem=sems[0].at[idx], recv_sem=sems[1].at[idx],
                                        device_id=target, device_id_type=MESH)


def _gather_copies(ins, outs, sems, local_sems):
    x, y, c = _position()
    k = 2 * x + y
    local = [pltpu.make_async_copy(ins[a], outs[a].at[k], local_sems.at[a]) for a in range(len(ins))]
    sends, arrivals = [], []
    for a in range(len(ins)):
        for j, (fx, fy) in enumerate(CHIP_FLIPS):
            tx, ty = _flip(x, fx), _flip(y, fy)
            there = outs[a].at[2 * tx + ty]
            sends.append(_remote_copy(ins[a], outs[a].at[k], sems, 3 * a + j, (tx, ty, c)))
            arrivals.append(_remote_copy(there, there, sems, 3 * a + j, (tx, ty, c)))
    return local, sends, arrivals


def _reduce_copies(ins, outs, sems):
    x, y, c = _position()
    sends, arrivals = [], []
    for a in range(len(ins)):
        h = ins[a].shape[1] // 2
        for f, (fx, fy, fc) in enumerate(DEVICE_FLIPS):
            tx, ty, tc = _flip(x, fx), _flip(y, fy), _flip(c, fc)
            src = ins[a].at[2 * tx + ty, pl.ds(pl.multiple_of(tc * h, h), h)]
            sends.append(_remote_copy(src, outs[a].at[f], sems, 7 * a + f, (tx, ty, tc)))
            arrivals.append(_remote_copy(outs[a].at[f], outs[a].at[f], sems, 7 * a + f, (tx, ty, tc)))
    return [], sends, arrivals


def _start_copies(make):
    local, sends, _ = make()
    for cp in local + sends:
        cp.start()


def _finish_copies(make):
    local, sends, arrivals = make()
    for cp in arrivals:
        cp.wait_recv()
    for cp in sends:
        cp.wait_send()
    for cp in local:
        cp.wait()


def _attn_fwd(qkv, g_attn, tri, gmat, shards):
    n_w = len(shards)
    s = qkv.shape[0]
    tq = _row_tile(s, ATTN_Q_TILE)
    tk = KEY_BLOCK
    nb = ATTN_KEY_BLOCKS
    width = nb * tk
    diag_trips = tq // width
    pairs = ATTN_WIDTH // LANES

    def body(q_ref, k_ref, v_ref, gain_ref, tri_ref, gmat_ref, *rest):
        w_ins, (o_ref, yn_ref, tot_ref) = rest[:n_w], rest[n_w:n_w + 3]
        w_outs, (send_sems, recv_sems, local_sems) = rest[n_w + 3:2 * n_w + 3], rest[2 * n_w + 3:]
        copies = functools.partial(_gather_copies, w_ins, w_outs, (send_sems, recv_sems), local_sems)
        p, i = pl.program_id(0), pl.program_id(1)
        pl.when((p == 0) & (i == 0))(functools.partial(_start_copies, copies))
        q2 = q_ref[...]
        tri_v = tri_ref[...]

        def trip(t, carry, diag_base):
            run = [carry[0], carry[1]]
            oacc = carry[2]
            s0 = pl.multiple_of(t * width, width)
            ksel = _stack_heads(k_ref[pl.ds(s0, width), :], nb)
            vsel = _stack_heads(v_ref[pl.ds(s0, width), :], nb)
            z = lax.dot_general(q2, ksel, NT_DIMS, preferred_element_type=F32) * ATTN_SCALE
            log_beta, log_keep = _softplus_terms(z)
            if diag_base is not None:
                valid = _causal_mask(tq, nb, diag_base)
                log_keep = jnp.where(valid, log_keep, 0.0)
            ct = jnp.dot(_stack_hilo(log_keep, 2 * nb), tri_v, preferred_element_type=F32)
            a_parts = [None] * (2 * nb)
            for c in reversed(range(2 * nb)):
                h = c % 2
                ct_c = ct[c * tq:(c + 1) * tq]
                a_parts[c] = jnp.exp(log_beta[:, c * tk:(c + 1) * tk] + ct_c[:, :tk] + run[h])
                run[h] = run[h] + ct_c[:, tk:]
            a = jnp.concatenate(a_parts, axis=1)
            if diag_base is not None:
                a = jnp.where(valid, a, 0.0)
            oacc = oacc + jnp.dot(a.astype(BF16), vsel, preferred_element_type=F32)
            return run[0], run[1], oacc

        carry = (jnp.zeros((tq, tk), F32), jnp.zeros((tq, tk), F32), jnp.zeros((tq, LANES), F32))
        for d in reversed(range(diag_trips)):
            carry = trip(i * diag_trips + d, carry, d * width)
        n_full = i * diag_trips
        carry = lax.fori_loop(0, n_full, lambda it, c: trip(n_full - 1 - it, c, None), carry)
        run_a, run_b, oacc = carry
        lane = lax.broadcasted_iota(jnp.int32, (1, LANES), 1)
        o_ref[...] = oacc
        tot_ref[...] = jnp.where(lane < GROUP, run_a, run_b)
        ms = _group_sum(oacc * oacc, gmat_ref[...]) * (1.0 / GROUP)
        yn_ref[...] = (oacc * lax.rsqrt(ms + RMS_EPS) * gain_ref[...]).astype(BF16)
        pl.when((p == pairs - 1) & (i == pl.num_programs(1) - 1))(functools.partial(_finish_copies, copies))

    blk = lambda: pl.BlockSpec((tq, LANES), lambda p, i: (i, p))
    return pl.pallas_call(
        body, name="attn_fwd", grid=(pairs, s // tq),
        in_specs=[pl.BlockSpec((tq, LANES), lambda p, i: (i, p)),
                  pl.BlockSpec((s, LANES), lambda p, i: (0, pairs + p)),
                  pl.BlockSpec((s, LANES), lambda p, i: (0, 2 * pairs + p)),
                  pl.BlockSpec((1, LANES), lambda p, i: (0, p)),
                  pl.BlockSpec((2 * tk, 2 * tk), lambda p, i: (0, 0)),
                  pl.BlockSpec((2 * LANES, LANES), lambda p, i: (0, 0))] + [ANY] * n_w,
        out_specs=[blk(), blk(), blk()] + [ANY] * n_w,
        out_shape=[SDS((s, ATTN_WIDTH), F32), SDS((s, ATTN_WIDTH), BF16), SDS((s, ATTN_WIDTH), F32)]
        + [SDS((N_CHIPS,) + w.shape, w.dtype) for w in shards],
        scratch_shapes=[pltpu.SemaphoreType.DMA((3 * n_w,)), pltpu.SemaphoreType.DMA((3 * n_w,)),
                        pltpu.SemaphoreType.DMA((n_w,))],
        compiler_params=_params(("arbitrary", "arbitrary")))(qkv, qkv, qkv, g_attn, tri, gmat, *shards)


def _attn_bwd(qkv, o, tot, dyn, g_attn, tri, gmat, partials):
    n_g = len(partials)
    s = qkv.shape[0]
    tq = _row_tile(s, ATTN_Q_TILE)
    tk = KEY_BLOCK
    nb = ATTN_KEY_BLOCKS
    width = nb * tk
    diag_trips = tq // width
    pairs = ATTN_WIDTH // LANES

    def body(q_ref, k_ref, v_ref, o_ref, tot_ref, dyn_ref, gain_ref, tri_ref, gmat_ref, *rest):
        g_ins, (dq_ref, dk_ref, dv_ref, dg_ref) = rest[:n_g], rest[n_g:n_g + 4]
        g_outs, sems = rest[n_g + 4:2 * n_g + 4], rest[2 * n_g + 4:]
        copies = functools.partial(_reduce_copies, g_ins, g_outs, sems)
        p, i = pl.program_id(0), pl.program_id(1)
        pl.when((p == 0) & (i == 0))(functools.partial(_start_copies, copies))

        @pl.when(i == 0)
        def _():
            dk_ref[...] = jnp.zeros_like(dk_ref)
            dv_ref[...] = jnp.zeros_like(dv_ref)
            dg_ref[...] = jnp.zeros_like(dg_ref)

        gmat_v = gmat_ref[...]
        o_v = o_ref[...]
        rstd = lax.rsqrt(_group_sum(o_v * o_v, gmat_v) * (1.0 / GROUP) + RMS_EPS)
        n = o_v * rstd
        dout = dyn_ref[...]
        dg_ref[0:1, :] += jnp.sum(dout * n, axis=0, keepdims=True)
        dn = dout * gain_ref[...]
        do2 = (rstd * (dn - n * (_group_sum(dn * n, gmat_v) * (1.0 / GROUP)))).astype(BF16)
        q2 = q_ref[...]
        tot_v = tot_ref[...]
        tots = (jnp.broadcast_to(tot_v[:, 0:1], (tq, tk)), jnp.broadcast_to(tot_v[:, GROUP:GROUP + 1], (tq, tk)))
        tri_v = tri_ref[...]
        lane = lax.broadcasted_iota(jnp.int32, (1, LANES), 1)

        def trip(t, carry, diag_base):
            pref_l = [carry[0], carry[1]]
            pref_g = [carry[2], carry[3]]
            dq = carry[4]
            s0 = pl.multiple_of(t * width, width)
            ksel = _stack_heads(k_ref[pl.ds(s0, width), :], nb)
            vsel = _stack_heads(v_ref[pl.ds(s0, width), :], nb)
            z = lax.dot_general(q2, ksel, NT_DIMS, preferred_element_type=F32) * ATTN_SCALE
            log_beta, log_keep = _softplus_terms(z)
            if diag_base is not None:
                valid = _causal_mask(tq, nb, diag_base)
                log_keep = jnp.where(valid, log_keep, 0.0)
            ctl = jnp.dot(_stack_hilo(log_keep, 2 * nb), tri_v, preferred_element_type=F32)
            da = lax.dot_general(do2, vsel, NT_DIMS, preferred_element_type=F32)
            a_parts = []
            for c in range(2 * nb):
                h = c % 2
                ct_c = ctl[c * tq:(c + 1) * tq]
                cols = slice(c * tk, (c + 1) * tk)
                suffix = tots[h] - pref_l[h] - ct_c[:, :tk] - log_keep[:, cols]
                pref_l[h] = pref_l[h] + ct_c[:, tk:]
                a_parts.append(jnp.exp(log_beta[:, cols] + suffix))
            a = jnp.concatenate(a_parts, axis=1)
            if diag_base is not None:
                a = jnp.where(valid, a, 0.0)
            g = a * da
            ctg = jnp.dot(_stack_hilo(g, 2 * nb), tri_v, preferred_element_type=F32)
            dz_parts = []
            for c in range(2 * nb):
                h = c % 2
                ct_c = ctg[c * tq:(c + 1) * tq]
                cols = slice(c * tk, (c + 1) * tk)
                prefix = pref_g[h] + ct_c[:, :tk]
                pref_g[h] = pref_g[h] + ct_c[:, tk:]
                beta = jnp.exp(log_beta[:, cols])
                dz_parts.append(g[:, cols] * (1.0 - beta) - prefix * beta)
            dz = jnp.concatenate(dz_parts, axis=1) * ATTN_SCALE
            if diag_base is not None:
                dz = jnp.where(valid, dz, 0.0)
            dzb = dz.astype(BF16)
            dq = dq + jnp.dot(dzb, ksel, preferred_element_type=F32)
            dkt = lax.dot_general(dzb, q2, TN_DIMS, preferred_element_type=F32)
            dvt = lax.dot_general(a.astype(BF16), do2, TN_DIMS, preferred_element_type=F32)
            for blk in range(nb):
                ra, rb = slice(2 * blk * tk, (2 * blk + 1) * tk), slice((2 * blk + 1) * tk, (2 * blk + 2) * tk)
                rows = pl.ds(pl.multiple_of(s0 + blk * tk, tk), tk)
                dk_ref[rows, :] += jnp.where(lane < GROUP, dkt[ra], dkt[rb])
                dv_ref[rows, :] += jnp.where(lane < GROUP, dvt[ra], dvt[rb])
            return pref_l[0], pref_l[1], pref_g[0], pref_g[1], dq

        zeros_qk = jnp.zeros((tq, tk), F32)
        carry = (zeros_qk, zeros_qk, zeros_qk, zeros_qk, jnp.zeros((tq, LANES), F32))
        carry = lax.fori_loop(0, i * diag_trips, lambda t, c: trip(t, c, None), carry)
        for d in range(diag_trips):
            carry = trip(i * diag_trips + d, carry, d * width)
        dq_ref[...] = carry[4].astype(BF16)
        pl.when((p == pairs - 1) & (i == pl.num_programs(1) - 1))(functools.partial(_finish_copies, copies))

    blk = lambda: pl.BlockSpec((tq, LANES), lambda p, i: (i, p))
    col = lambda: pl.BlockSpec((s, LANES), lambda p, i: (0, p))
    n_peers = len(DEVICE_FLIPS)
    return pl.pallas_call(
        body, name="attn_bwd", grid=(pairs, s // tq),
        in_specs=[pl.BlockSpec((tq, LANES), lambda p, i: (i, p)),
                  pl.BlockSpec((s, LANES), lambda p, i: (0, pairs + p)),
                  pl.BlockSpec((s, LANES), lambda p, i: (0, 2 * pairs + p)),
                  blk(), blk(), blk(),
                  pl.BlockSpec((1, LANES), lambda p, i: (0, p)),
                  pl.BlockSpec((2 * tk, 2 * tk), lambda p, i: (0, 0)),
                  pl.BlockSpec((2 * LANES, LANES), lambda p, i: (0, 0))] + [ANY] * n_g,
        out_specs=[blk(), col(), col(), pl.BlockSpec((SUBLANES, LANES), lambda p, i: (0, p))] + [ANY] * n_g,
        out_shape=[SDS((s, ATTN_WIDTH), BF16), SDS((s, ATTN_WIDTH), F32), SDS((s, ATTN_WIDTH), F32),
                   SDS((SUBLANES, ATTN_WIDTH), F32)]
        + [SDS((n_peers, g.shape[1] // 2, g.shape[2]), g.dtype) for g in partials],
        scratch_shapes=[pltpu.SemaphoreType.DMA((n_peers * n_g,)), pltpu.SemaphoreType.DMA((n_peers * n_g,))],
        compiler_params=_params(("arbitrary", "arbitrary")))(qkv, qkv, qkv, o, tot, dyn, g_attn, tri, gmat, *partials)


def _mix_ln1(ycn, yan, w_out, x, g, b):
    s = x.shape[0]
    tm = _row_tile(s, 512)

    def body(yc_ref, ya_ref, w_ref, x_ref, g_ref, b_ref, x1_ref, xhat_ref, rstd_ref):
        mix = jnp.dot(yc_ref[...], w_ref[0:CONV_WIDTH, :], preferred_element_type=F32)
        mix = mix + jnp.dot(ya_ref[...], w_ref[CONV_WIDTH:, :], preferred_element_type=F32)
        x1, xhat, rstd = _layer_norm_fwd(ALPHA * x_ref[...] + mix, g_ref[...], b_ref[...])
        x1_ref[...] = x1
        xhat_ref[...] = xhat
        rstd_ref[...] = rstd

    row = lambda w: pl.BlockSpec((tm, w), lambda i: (i, 0))
    vec = lambda: pl.BlockSpec((1, D_MODEL), lambda i: (0, 0))
    return pl.pallas_call(
        body, name="mix_ln1", grid=(s // tm,),
        in_specs=[row(CONV_WIDTH), row(ATTN_WIDTH), pl.BlockSpec((D_MODEL, D_MODEL), lambda i: (0, 0)),
                  row(D_MODEL), vec(), vec()],
        out_specs=[row(D_MODEL), row(D_MODEL), row(1)],
        out_shape=[SDS((s, D_MODEL), F32), SDS((s, D_MODEL), F32), SDS((s, 1), F32)],
        compiler_params=_params(("parallel",)))(ycn, yan, w_out, x, g, b)


def _mlp_fwd_loss(x1, w_up, w_down, target, g, b):
    s = x1.shape[0]
    tm = _row_tile(s, 256)

    def body(x1_ref, wu_ref, wd_ref, t_ref, g_ref, b_ref, dpre_ref, sums_ref, loss_ref):
        i = pl.program_id(0)
        x1_v = x1_ref[...]
        xb = x1_v.astype(BF16)
        ffn = jnp.zeros((tm, D_MODEL), F32)
        for k in range(N_CHIPS):
            r = jnp.maximum(jnp.dot(xb, wu_ref[k], preferred_element_type=F32), 0.0)
            ffn = ffn + jnp.dot((r * r).astype(BF16), wd_ref[k], preferred_element_type=F32)
        g_v = g_ref[...]
        x2, xhat, rstd = _layer_norm_fwd(ALPHA * x1_v + ffn, g_v, b_ref[...])
        err = x2 - t_ref[...]
        dx2 = err * (1.0 / D_MODEL)
        dpre_ref[...] = _layer_norm_bwd(dx2, xhat, rstd, g_v)

        @pl.when(i == 0)
        def _():
            sums_ref[...] = jnp.zeros_like(sums_ref)
            loss_ref[...] = jnp.zeros_like(loss_ref)

        sums_ref[0:1, :] += jnp.sum(dx2 * xhat, axis=0, keepdims=True)
        sums_ref[1:2, :] += jnp.sum(dx2, axis=0, keepdims=True)
        loss_ref[...] += jnp.sum(jnp.sum(err * err, axis=1, keepdims=True), axis=0, keepdims=True) * (0.5 / D_MODEL)

    row = lambda: pl.BlockSpec((tm, D_MODEL), lambda i: (i, 0))
    vec = lambda: pl.BlockSpec((1, D_MODEL), lambda i: (0, 0))
    wspec = lambda: pl.BlockSpec((N_CHIPS, D_MODEL, FF_SHARD), lambda i: (0, 0, 0))
    return pl.pallas_call(
        body, name="mlp_fwd_loss", grid=(s // tm,),
        in_specs=[row(), wspec(), wspec(), row(), vec(), vec()],
        out_specs=[row(), pl.BlockSpec((SUBLANES, D_MODEL), lambda i: (0, 0)),
                   pl.BlockSpec((SUBLANES, LANES), lambda i: (0, 0))],
        out_shape=[SDS((s, D_MODEL), F32), SDS((SUBLANES, D_MODEL), F32), SDS((SUBLANES, LANES), F32)],
        compiler_params=_params(("arbitrary",)))(x1, w_up, w_down, target, g, b)


def _mlp_bwd_ln1(x1, dpre2, w_up, w_down, xhat1, rstd1, g1):
    s = x1.shape[0]
    tm = _row_tile(s, 256)

    def body(x1_ref, d2_ref, wu_ref, wd_ref, xh_ref, rs_ref, g_ref, hid_ref, dup_ref, dpre_ref, sums_ref):
        i = pl.program_id(0)
        xb = x1_ref[...].astype(BF16)
        d2 = d2_ref[...]
        d2b = d2.astype(BF16)
        dx1 = ALPHA * d2
        for k in range(N_CHIPS):
            r = jnp.maximum(jnp.dot(xb, wu_ref[k], preferred_element_type=F32), 0.0)
            hid_ref[:, FF_SHARD * k:FF_SHARD * (k + 1)] = (r * r).astype(BF16)
            dhid = lax.dot_general(d2b, wd_ref[k], NT_DIMS, preferred_element_type=F32)
            dupb = (dhid * (2.0 * r)).astype(BF16)
            dup_ref[:, FF_SHARD * k:FF_SHARD * (k + 1)] = dupb
            dx1 = dx1 + lax.dot_general(dupb, wu_ref[k], NT_DIMS, preferred_element_type=F32)
        xhat = xh_ref[...]
        dpre_ref[...] = _layer_norm_bwd(dx1, xhat, rs_ref[...], g_ref[...])

        @pl.when(i == 0)
        def _():
            sums_ref[...] = jnp.zeros_like(sums_ref)

        sums_ref[0:1, :] += jnp.sum(dx1 * xhat, axis=0, keepdims=True)
        sums_ref[1:2, :] += jnp.sum(dx1, axis=0, keepdims=True)

    row = lambda w: pl.BlockSpec((tm, w), lambda i: (i, 0))
    wspec = lambda: pl.BlockSpec((N_CHIPS, D_MODEL, FF_SHARD), lambda i: (0, 0, 0))
    return pl.pallas_call(
        body, name="mlp_bwd_ln1", grid=(s // tm,),
        in_specs=[row(D_MODEL), row(D_MODEL), wspec(), wspec(), row(D_MODEL), row(1),
                  pl.BlockSpec((1, D_MODEL), lambda i: (0, 0))],
        out_specs=[row(D_FF), row(D_FF), row(D_MODEL), pl.BlockSpec((SUBLANES, D_MODEL), lambda i: (0, 0))],
        out_shape=[SDS((s, D_FF), BF16), SDS((s, D_FF), BF16), SDS((s, D_MODEL), F32),
                   SDS((SUBLANES, D_MODEL), F32)],
        compiler_params=_params(("arbitrary",)))(x1, dpre2, w_up, w_down, xhat1, rstd1, g1)


def _grad_tn(a, b, name, out_cols, stacked):
    s, ka = a.shape
    n = b.shape[1]
    ts = _row_tile(s, 512)
    n_steps = s // ts
    if stacked:
        tka, tn = ka, out_cols
        grid = (1, n // tn, n_steps)
        shape = (n // tn, ka, tn)
        out_spec = lambda: pl.BlockSpec((None, tka, tn), lambda r, c, t: (c, 0, 0))
    else:
        tka, tn = min(ka, 1024), n
        grid = (ka // tka, 1, n_steps)
        shape = (ka, n)
        out_spec = lambda: pl.BlockSpec((tka, tn), lambda r, c, t: (r, 0))

    def body(a_ref, b_ref, o_ref, ob_ref):
        t = pl.program_id(2)

        @pl.when(t == 0)
        def _():
            o_ref[...] = jnp.zeros_like(o_ref)

        o_ref[...] += lax.dot_general(a_ref[...].astype(BF16), b_ref[...].astype(BF16), TN_DIMS,
                                      preferred_element_type=F32)

        @pl.when(t == n_steps - 1)
        def _():
            ob_ref[...] = o_ref[...].astype(BF16)

    return pl.pallas_call(
        body, name=name, grid=grid,
        in_specs=[pl.BlockSpec((ts, tka), lambda r, c, t: (t, r)),
                  pl.BlockSpec((ts, tn), lambda r, c, t: (t, c))],
        out_specs=[out_spec(), out_spec()], out_shape=[SDS(shape, F32), SDS(shape, BF16)],
        compiler_params=_params(("parallel", "parallel", "arbitrary")))(a, b)


def _dmix(dpre1, w_out):
    s = dpre1.shape[0]
    tm = _row_tile(s, 512)

    def body(d_ref, w_ref, dc_ref, da_ref):
        db = d_ref[...].astype(BF16)
        dc_ref[...] = lax.dot_general(db, w_ref[0:CONV_WIDTH, :], NT_DIMS, preferred_element_type=F32)
        da_ref[...] = lax.dot_general(db, w_ref[CONV_WIDTH:, :], NT_DIMS, preferred_element_type=F32)

    return pl.pallas_call(
        body, name="dmix", grid=(s // tm,),
        in_specs=[pl.BlockSpec((tm, D_MODEL), lambda i: (i, 0)),
                  pl.BlockSpec((D_MODEL, D_MODEL), lambda i: (0, 0))],
        out_specs=[pl.BlockSpec((tm, CONV_WIDTH), lambda i: (i, 0)),
                   pl.BlockSpec((tm, ATTN_WIDTH), lambda i: (i, 0))],
        out_shape=[SDS((s, CONV_WIDTH), F32), SDS((s, ATTN_WIDTH), F32)],
        compiler_params=_params(("parallel",)))(dpre1, w_out)


def _grad_x(dproj, w_in, dpre1, partial):
    s = dproj.shape[0]
    tm = _row_tile(s, 512)
    n_peers = len(DEVICE_FLIPS)

    def body(dp_ref, w_ref, d1_ref, g_in, o_ref, g_out, send_sems, recv_sems):
        copies = functools.partial(_reduce_copies, [g_in], [g_out], (send_sems, recv_sems))
        i = pl.program_id(0)
        pl.when(i == 0)(functools.partial(_start_copies, copies))
        acc = ALPHA * d1_ref[...]
        for k in range(N_CHIPS):
            acc = acc + lax.dot_general(dp_ref[:, IN_SHARD * k:IN_SHARD * (k + 1)], w_ref[k], NT_DIMS,
                                        preferred_element_type=F32)
        o_ref[...] = acc
        pl.when(i == pl.num_programs(0) - 1)(functools.partial(_finish_copies, copies))

    return pl.pallas_call(
        body, name="grad_x", grid=(s // tm,),
        in_specs=[pl.BlockSpec((tm, IN_COLS), lambda i: (i, 0)),
                  pl.BlockSpec((N_CHIPS, D_MODEL, IN_SHARD), lambda i: (0, 0, 0)),
                  pl.BlockSpec((tm, D_MODEL), lambda i: (i, 0)), ANY],
        out_specs=[pl.BlockSpec((tm, D_MODEL), lambda i: (i, 0)), ANY],
        out_shape=[SDS((s, D_MODEL), F32), SDS((n_peers, partial.shape[1] // 2, partial.shape[2]), partial.dtype)],
        scratch_shapes=[pltpu.SemaphoreType.DMA((n_peers,)), pltpu.SemaphoreType.DMA((n_peers,))],
        compiler_params=_params(("arbitrary",)))(dproj, w_in, dpre1, partial)


def _adamw(w, g, m, v, name):
    r, c = w.shape
    tr = _row_tile(r, 256)

    def body(w_ref, g_ref, m_ref, v_ref, d_ref, nm_ref, nv_ref):
        g_v = g_ref[...]
        nm = ADAM_B1 * m_ref[...] + (1.0 - ADAM_B1) * g_v
        nv = ADAM_B2 * v_ref[...] + (1.0 - ADAM_B2) * (g_v * g_v)
        m_hat = nm / (1.0 - ADAM_B1 ** ADAM_STEP)
        v_hat = nv / (1.0 - ADAM_B2 ** ADAM_STEP)
        d_ref[...] = -ADAM_LR * (m_hat / (jnp.sqrt(v_hat) + ADAM_EPS) + ADAM_WD * w_ref[...])
        nm_ref[...] = nm
        nv_ref[...] = nv

    spec = lambda: pl.BlockSpec((tr, c), lambda i: (i, 0))
    return pl.pallas_call(
        body, name=name, grid=(r // tr,),
        in_specs=[spec(), spec(), spec(), spec()], out_specs=[spec(), spec(), spec()],
        out_shape=[SDS((r, c), F32)] * 3, compiler_params=_params(("parallel",)))(w, g, m, v)


def _sum_partials(kc_idx, grad, recv, name):
    _, _, h, cols = grad.shape
    th = _row_tile(h, 128)
    n_peers = recv.shape[0]

    def body(kc_ref, own_ref, r_ref, o_ref):
        acc = own_ref[...]
        for f in range(n_peers):
            acc = acc + r_ref[f].astype(F32)
        o_ref[...] = acc

    grid_spec = pltpu.PrefetchScalarGridSpec(
        num_scalar_prefetch=1, grid=(h // th,),
        in_specs=[pl.BlockSpec((None, None, th, cols), lambda t, kc: (kc[0], kc[1], t, 0)),
                  pl.BlockSpec((n_peers, th, cols), lambda t, kc: (0, t, 0))],
        out_specs=pl.BlockSpec((th, cols), lambda t, kc: (t, 0)))
    return pl.pallas_call(
        body, name=name, grid_spec=grid_spec, out_shape=SDS((h, cols), F32),
        compiler_params=_params(("parallel",)))(kc_idx, grad, recv)


def _gather_weights(shards, conv_shard):
    n = len(shards)

    def body(*refs):
        ins, conv_in = refs[:n], refs[n]
        outs, conv_out = refs[n + 1:2 * n + 1], refs[2 * n + 1]
        send_sems, recv_sems, local_sems = refs[2 * n + 2:]
        x, y, c = _position()
        k = 2 * x + y
        sibling = (x, y, 1 - c)
        chips = [(_flip(x, fx), _flip(y, fy)) for fx, fy in CHIP_FLIPS]

        def half(a, rows_of_core):
            h = shards[a].shape[0] // 2
            return pl.ds(pl.multiple_of(rows_of_core * h, h), h)

        def remote(src, dst, idx, target):
            return pltpu.make_async_remote_copy(src_ref=src, dst_ref=dst, send_sem=send_sems.at[idx],
                                                recv_sem=recv_sems.at[idx], device_id=target,
                                                device_id_type=MESH)

        local = [pltpu.make_async_copy(ins[a], outs[a].at[k], local_sems.at[a]) for a in range(n)]
        local.append(pltpu.make_async_copy(conv_in, conv_out.at[k], local_sems.at[n]))
        for cp in local:
            cp.start()
        started = []
        for a in range(n):
            for j, (tx, ty) in enumerate(chips):
                cp = remote(ins[a].at[half(a, c)], outs[a].at[k, half(a, c)], 6 * a + j, (tx, ty, c))
                cp.start()
                started.append(cp)
        for j, (tx, ty) in enumerate(chips):
            cp = remote(conv_in, conv_out.at[k], 6 * n + j, (tx, ty, c))
            cp.start()
            started.append(cp)
        for a in range(n):
            for j, (tx, ty) in enumerate(chips):
                kj = 2 * tx + ty
                landed = outs[a].at[kj, half(a, c)]
                remote(landed, landed, 6 * a + j, sibling).wait_recv()
                cp = remote(landed, landed, 6 * a + 3 + j, sibling)
                cp.start()
                started.append(cp)
        for a in range(n):
            for j, (tx, ty) in enumerate(chips):
                kj = 2 * tx + ty
                other = outs[a].at[kj, half(a, 1 - c)]
                remote(other, other, 6 * a + 3 + j, sibling).wait_recv()
        for j, (tx, ty) in enumerate(chips):
            kj = 2 * tx + ty
            remote(conv_out.at[kj], conv_out.at[kj], 6 * n + j, sibling).wait_recv()
        for cp in started:
            cp.wait_send()
        for cp in local:
            cp.wait()

    out_shape = [SDS((N_CHIPS,) + w.shape, w.dtype) for w in shards]
    out_shape.append(SDS((N_CHIPS,) + conv_shard.shape, conv_shard.dtype))
    n_sems = 6 * n + 3
    return pl.pallas_call(
        body, name="gather_weights", in_specs=[ANY] * (n + 1), out_specs=[ANY] * (n + 1),
        out_shape=out_shape,
        scratch_shapes=[pltpu.SemaphoreType.DMA((n_sems,)), pltpu.SemaphoreType.DMA((n_sems,)),
                        pltpu.SemaphoreType.DMA((n + 1,))])(*shards, conv_shard)


def _finish_exchange(pieces, vec):
    n = len(pieces)
    n_dev = 2 * N_CHIPS

    def body(*refs):
        ins, v_ref = refs[:n], refs[n]
        outs, o_ref = refs[n + 1:2 * n + 1], refs[2 * n + 1]
        buf, send_sems, recv_sems, local_sems = refs[2 * n + 2:]
        x, y, c = _position()
        sibling = (x, y, 1 - c)
        me = 4 * x + 2 * y + c
        buf[me] = v_ref[...]
        started, local = [], []
        for f, (fx, fy, fc) in enumerate(DEVICE_FLIPS):
            cp = pltpu.make_async_remote_copy(
                src_ref=v_ref, dst_ref=buf.at[me], send_sem=send_sems.at[n + f], recv_sem=recv_sems.at[n + f],
                device_id=(_flip(x, fx), _flip(y, fy), _flip(c, fc)), device_id_type=MESH)
            cp.start()
            started.append(cp)
        for a in range(n):
            h = pieces[a].shape[0]
            mine = pl.ds(pl.multiple_of(c * h, h), h)
            cp = pltpu.make_async_copy(ins[a], outs[a].at[mine], local_sems.at[a])
            cp.start()
            local.append(cp)
            cp = pltpu.make_async_remote_copy(
                src_ref=ins[a], dst_ref=outs[a].at[mine], send_sem=send_sems.at[a], recv_sem=recv_sems.at[a],
                device_id=sibling, device_id_type=MESH)
            cp.start()
            started.append(cp)
        for a in range(n):
            h = pieces[a].shape[0]
            theirs = outs[a].at[pl.ds(pl.multiple_of((1 - c) * h, h), h)]
            pltpu.make_async_remote_copy(
                src_ref=theirs, dst_ref=theirs, send_sem=send_sems.at[a], recv_sem=recv_sems.at[a],
                device_id=sibling, device_id_type=MESH).wait_recv()
        for f, (fx, fy, fc) in enumerate(DEVICE_FLIPS):
            src = 4 * _flip(x, fx) + 2 * _flip(y, fy) + _flip(c, fc)
            pltpu.make_async_remote_copy(
                src_ref=v_ref, dst_ref=buf.at[src], send_sem=send_sems.at[n + f], recv_sem=recv_sems.at[n + f],
                device_id=(x, y, c), device_id_type=MESH).wait_recv()
        for cp in started:
            cp.wait_send()
        for cp in local:
            cp.wait()
        acc = buf[0]
        for d in range(1, n_dev):
            acc = acc + buf[d]
        o_ref[...] = acc

    vmem = pl.BlockSpec(memory_space=pltpu.VMEM)
    out_shape = [SDS((2 * p.shape[0], p.shape[1]), p.dtype) for p in pieces] + [SDS(vec.shape, vec.dtype)]
    n_sems = n + n_dev - 1
    return pl.pallas_call(
        body, name="finish_exchange", in_specs=[ANY] * n + [vmem], out_specs=[ANY] * n + [vmem],
        out_shape=out_shape,
        scratch_shapes=[pltpu.VMEM((n_dev,) + vec.shape, vec.dtype), pltpu.SemaphoreType.DMA((n_sems,)),
                        pltpu.SemaphoreType.DMA((n_sems,)), pltpu.SemaphoreType.DMA((n,))])(*pieces, vec)


def _constants():
    r = jnp.arange(2 * KEY_BLOCK)[:, None] % KEY_BLOCK
    c = jnp.arange(2 * KEY_BLOCK)[None, :]
    later = jnp.where(c < KEY_BLOCK, r > c, True).astype(BF16)
    earlier = jnp.where(c < KEY_BLOCK, r < c, True).astype(BF16)
    gr = (jnp.arange(2 * LANES)[:, None] % LANES) // GROUP
    gc = jnp.arange(LANES)[None, :] // GROUP
    gmat = (gr == gc).astype(BF16)
    return later, earlier, gmat


def _rows(v):
    return v.reshape(-1, LANES)


def kernel(x, w_in, conv_w, g_conv, g_attn, w_out, ln1_g, ln1_b, w_up, w_down, ln2_g, ln2_b, loss_target, m_w_in, m_conv_w, m_g_conv, m_g_attn, m_w_out, m_ln1_g, m_ln1_b, m_w_up, m_w_down, m_ln2_g, m_ln2_b, v_w_in, v_conv_w, v_g_conv, v_g_attn, v_w_out, v_ln1_g, v_ln1_b, v_w_up, v_w_down, v_ln2_g, v_ln2_b):
    xs, target = x[0], loss_target[0]
    mesh_x, mesh_y, mesh_c = _position()
    k_idx = 2 * mesh_x + mesh_y
    kc_idx = jnp.stack([k_idx, mesh_c]).astype(jnp.int32)
    tri_later, tri_earlier, gmat = _constants()

    pad_rows = lambda a: jnp.pad(a, ((0, SUBLANES - a.shape[0]), (0, 0)))
    w_in_b, w_out_b, w_up_b, w_down_b = [
        _cast_bf16(w[0], "cast_" + nm) for w, nm in ((w_in, "w_in"), (w_out, "w_out"), (w_up, "w_up"), (w_down, "w_down"))]
    w_in_f, conv_f = _gather_weights([w_in_b], pad_rows(conv_w[0]))
    taps = jnp.transpose(conv_f, (1, 0, 2)).reshape(SUBLANES, CONV_WIDTH)

    gates, qkv = _proj(xs, w_in_f)
    ycn = _conv_fwd(gates, taps, g_conv, gmat)
    o, yan, tot, w_out_f, w_up_f, w_down_f = _attn_fwd(qkv, g_attn, tri_later, gmat, [w_out_b, w_up_b, w_down_b])
    w_out_f = w_out_f.reshape(D_MODEL, D_MODEL)
    x1, xhat1, rstd1 = _mix_ln1(ycn, yan, w_out_f, xs, ln1_g, ln1_b)
    dpre2, ln2_sums, loss_sum = _mlp_fwd_loss(x1, w_up_f, w_down_f, target, ln2_g, ln2_b)

    hid, dup, dpre1, ln1_sums = _mlp_bwd_ln1(x1, dpre2, w_up_f, w_down_f, xhat1, rstd1, ln1_g)
    gw_up = _grad_tn(x1, dup, "grad_w_up", FF_SHARD, True)
    gw_down = [g.reshape(N_CHIPS, FF_SHARD, D_MODEL) for g in _grad_tn(hid, dpre2, "grad_w_down", D_MODEL, False)]
    gw_out_conv = _grad_tn(ycn, dpre1, "grad_w_out_conv", D_MODEL, False)
    gw_out_attn = _grad_tn(yan, dpre1, "grad_w_out_attn", D_MODEL, False)
    gw_out = [jnp.concatenate([gc_, ga_], axis=0).reshape(N_CHIPS, D_MODEL // N_CHIPS, D_MODEL)
              for gc_, ga_ in zip(gw_out_conv, gw_out_attn)]
    dycn, dyan = _dmix(dpre1, w_out_f)
    dq, dk, dv, gattn_sums, recv_out, recv_up, recv_down = _attn_bwd(
        qkv, o, tot, dyan, g_attn, tri_earlier, gmat, [gw_out[1], gw_up[1], gw_down[1]])
    dbg, dy, conv_sums = _conv_bwd_gate(gates, dycn, taps, g_conv, gmat)
    dproj = _dproj_assemble(gates, dy, dbg, dq, dk, dv, taps)
    gw_in = _grad_tn(xs, dproj, "grad_w_in", IN_SHARD, True)
    grad_x, recv_in = _grad_x(dproj, w_in_f, dpre1, gw_in[1])

    halves = lambda g: g.reshape(N_CHIPS, 2, g.shape[1] // 2, g.shape[2])
    pieces = [_sum_partials(kc_idx, halves(g[0]), r, "sum_partials_" + nm)
              for g, r, nm in ((gw_in, recv_in, "w_in"), (gw_out, recv_out, "w_out"), (gw_up, recv_up, "w_up"),
                               (gw_down, recv_down, "w_down"))]
    conv_rows = jnp.transpose(conv_sums[0:3].reshape(3, N_CHIPS, LANES), (1, 0, 2)).reshape(3 * N_CHIPS, LANES)
    small = jnp.concatenate([
        loss_sum, _rows(conv_sums[3]), _rows(gattn_sums[0]), _rows(ln1_sums[0]), _rows(ln1_sums[1]),
        _rows(ln2_sums[0]), _rows(ln2_sums[1]), conv_rows,
        jnp.zeros((SMALL_ROWS - ROW_CONVW - 3 * N_CHIPS, LANES), F32)], axis=0)
    g_w_in, g_w_out, g_w_up, g_w_down, total = _finish_exchange(pieces, small)
    loss = total[ROW_LOSS, 0]
    g_conv_w = lax.dynamic_slice(total, (ROW_CONVW + 3 * k_idx, 0), (3, LANES))

    def pack(gc_, ga_, l1g, l1b, l2g, l2b, cw):
        return jnp.concatenate([_rows(gc_), _rows(ga_), _rows(l1g), _rows(l1b), _rows(l2g), _rows(l2b), cw[0],
                                jnp.zeros((PARAM_ROWS + SUBLANES - ROW_CONVW - 3, LANES), F32)], axis=0)

    small_w = pack(g_conv, g_attn, ln1_g, ln1_b, ln2_g, ln2_b, conv_w)
    small_m = pack(m_g_conv, m_g_attn, m_ln1_g, m_ln1_b, m_ln2_g, m_ln2_b, m_conv_w)
    small_v = pack(v_g_conv, v_g_attn, v_ln1_g, v_ln1_b, v_ln2_g, v_ln2_b, v_conv_w)
    small_g = jnp.concatenate([total[ROW_GCONV:ROW_CONVW], g_conv_w,
                               jnp.zeros((PARAM_ROWS + SUBLANES - ROW_CONVW - 3, LANES), F32)], axis=0)
    small_out = _adamw(small_w, small_g, small_m, small_v, "adamw_small")

    def unpack(p):
        off = ROW_GCONV
        vec = lambda a, b: p[a - off:b - off].reshape(1, -1)
        return {"g_conv": vec(ROW_GCONV, ROW_GATTN), "g_attn": vec(ROW_GATTN, ROW_LN1G),
                "ln1_g": vec(ROW_LN1G, ROW_LN1B), "ln1_b": vec(ROW_LN1B, ROW_LN2G),
                "ln2_g": vec(ROW_LN2G, ROW_LN2B), "ln2_b": vec(ROW_LN2B, ROW_CONVW),
                "conv_w": p[ROW_CONVW - off:ROW_CONVW - off + 3][None]}

    big_out = {
        "w_in": _adamw(w_in[0], g_w_in, m_w_in[0], v_w_in[0], "adamw_w_in"),
        "w_out": _adamw(w_out[0], g_w_out, m_w_out[0], v_w_out[0], "adamw_w_out"),
        "w_up": _adamw(w_up[0], g_w_up, m_w_up[0], v_w_up[0], "adamw_w_up"),
        "w_down": _adamw(w_down[0], g_w_down, m_w_down[0], v_w_down[0], "adamw_w_down"),
    }
    big_grads = {"w_in": g_w_in, "w_out": g_w_out, "w_up": g_w_up, "w_down": g_w_down}
    order = ["w_in", "conv_w", "g_conv", "g_attn", "w_out", "ln1_g", "ln1_b", "w_up", "w_down", "ln2_g", "ln2_b"]
    small_parts = [unpack(small_g)] + [unpack(p) for p in small_out]

    def leaf(kind, name):
        if name in big_out:
            return (big_grads[name] if kind == 0 else big_out[name][kind - 1])[None]
        return small_parts[kind][name]

    outs = [loss, grad_x[None]]
    for kind in range(4):
        outs.extend(leaf(kind, name) for name in order)
    return tuple(outs)
```

```python
import functools

import jax
import jax.numpy as jnp
from jax import lax
from jax.experimental import pallas as pl
from jax.experimental.pallas import tpu as pltpu

F32 = jnp.float32
BF16 = jnp.bfloat16
SDS = jax.ShapeDtypeStruct

D_MODEL = 1024
CONV_WIDTH = 512
ATTN_WIDTH = 512
GROUP = 64
GATE_COLS = 3 * CONV_WIDTH
QKV_COLS = 3 * ATTN_WIDTH
IN_COLS = GATE_COLS + QKV_COLS
D_FF = 4 * D_MODEL
N_CHIPS = 4
IN_SHARD = IN_COLS // N_CHIPS
FF_SHARD = D_FF // N_CHIPS
ALPHA = float(2.0 ** 0.25)
LN_EPS = 1e-5
RMS_EPS = 1e-6
ATTN_SCALE = GROUP ** -0.5
ADAM_LR = 0.001
ADAM_B1 = 0.9
ADAM_B2 = 0.999
ADAM_EPS = 1e-08
ADAM_WD = 0.01
ADAM_STEP = 10

LANES = 128
SUBLANES = 8
KEY_BLOCK = 128
ATTN_Q_TILE = 512
ATTN_KEY_BLOCKS = 2
VMEM_LIMIT = 56 * 1024 * 1024

MESH = pl.DeviceIdType.MESH
CHIP_FLIPS = ((1, 0), (0, 1), (1, 1))
DEVICE_FLIPS = tuple((fx, fy, fc) for fx in (0, 1) for fy in (0, 1) for fc in (0, 1))[1:]
NT_DIMS = (((1,), (1,)), ((), ()))
TN_DIMS = (((0,), (0,)), ((), ()))

ROW_LOSS = 0
ROW_GCONV = 8
ROW_GATTN = 12
ROW_LN1G = 16
ROW_LN1B = 24
ROW_LN2G = 32
ROW_LN2B = 40
ROW_CONVW = 48
SMALL_ROWS = 64
PARAM_ROWS = 48


def _params(sem=None):
    return pltpu.CompilerParams(dimension_semantics=sem, vmem_limit_bytes=VMEM_LIMIT)


def _flip(v, f):
    return 1 - v if f else v


def _position():
    return lax.axis_index("x"), lax.axis_index("y"), lax.axis_index("c")


def _hilo(v):
    hi = v.astype(BF16)
    lo = (v - hi.astype(F32)).astype(BF16)
    return jnp.concatenate([hi, lo], axis=1)


def _hilo_dot(v, mat):
    return jnp.dot(_hilo(v), mat, preferred_element_type=F32)


def _group_sum(v, gmat):
    parts = [_hilo_dot(v[:, LANES * j:LANES * (j + 1)], gmat) for j in range(v.shape[1] // LANES)]
    return parts[0] if len(parts) == 1 else jnp.concatenate(parts, axis=1)


def _softplus_terms(z):
    sp = jnp.log1p(jnp.exp(-jnp.abs(z)))
    log_beta = jnp.minimum(z, 0.0) - sp
    return log_beta, log_beta - z


def _layer_norm_fwd(pre, g, b):
    mu = jnp.mean(pre, axis=-1, keepdims=True)
    d = pre - mu
    var = jnp.mean(d * d, axis=-1, keepdims=True)
    rstd = lax.rsqrt(var + LN_EPS)
    xhat = d * rstd
    return xhat * g + b, xhat, rstd


def _layer_norm_bwd(dy, xhat, rstd, g):
    dxh = dy * g
    m1 = jnp.mean(dxh, axis=-1, keepdims=True)
    m2 = jnp.mean(dxh * xhat, axis=-1, keepdims=True)
    return rstd * (dxh - m1 - xhat * m2)


def _row_tile(s, want):
    return min(s, want)


def _cast_into_slot(kc_idx, w, name):
    r, c = w.shape
    tr = _row_tile(r, 256)

    def body(kc_ref, w_ref, o_ref):
        o_ref[...] = w_ref[...].astype(BF16)

    grid_spec = pltpu.PrefetchScalarGridSpec(
        num_scalar_prefetch=1, grid=(r // tr,),
        in_specs=[pl.BlockSpec((tr, c), lambda i, kc: (i, 0))],
        out_specs=pl.BlockSpec((None, tr, c), lambda i, kc: (kc[0], i, 0)))
    return pl.pallas_call(
        body, name=name, grid_spec=grid_spec, out_shape=SDS((N_CHIPS, r, c), BF16),
        compiler_params=_params(("parallel",)))(kc_idx, w)


def _proj(x, w_in):
    s = x.shape[0]
    tm = _row_tile(s, 512)

    def body(x_ref, w_ref, gates_ref, qkv_ref):
        xb = x_ref[...].astype(BF16)
        for k in range(N_CHIPS):
            acc = jnp.dot(xb, w_ref[k], preferred_element_type=F32)
            if k < 2:
                gates_ref[:, IN_SHARD * k:IN_SHARD * (k + 1)] = acc
            else:
                qkv_ref[:, IN_SHARD * (k - 2):IN_SHARD * (k - 1)] = acc.astype(BF16)

    return pl.pallas_call(
        body, name="proj", grid=(s // tm,),
        in_specs=[pl.BlockSpec((tm, D_MODEL), lambda i: (i, 0)),
                  pl.BlockSpec((N_CHIPS, D_MODEL, IN_SHARD), lambda i: (0, 0, 0))],
        out_specs=[pl.BlockSpec((tm, GATE_COLS), lambda i: (i, 0)),
                   pl.BlockSpec((tm, QKV_COLS), lambda i: (i, 0))],
        out_shape=[SDS((s, GATE_COLS), F32), SDS((s, QKV_COLS), BF16)],
        compiler_params=_params(("parallel",)))(x, w_in)


def _conv_forward_values(g_ref, halo_ref, taps_ref, first_block):
    gates = g_ref[...]
    tr = gates.shape[0]
    bg = gates[:, :CONV_WIDTH]
    cg = gates[:, CONV_WIDTH:2 * CONV_WIDTH]
    h = gates[:, 2 * CONV_WIDTH:]
    u = cg * h

    def prev(r):
        v = halo_ref[r:r + 1, CONV_WIDTH:2 * CONV_WIDTH] * halo_ref[r:r + 1, 2 * CONV_WIDTH:GATE_COLS]
        return jnp.where(first_block, 0.0, v)

    row = lax.broadcasted_iota(jnp.int32, (tr, CONV_WIDTH), 0)
    u1 = jnp.where(row == 0, prev(7), pltpu.roll(u, 1, 0))
    u2 = jnp.where(row == 0, prev(6), jnp.where(row == 1, prev(7), pltpu.roll(u, 2, 0)))
    y = taps_ref[0:1, :] * u2 + taps_ref[1:2, :] * u1 + taps_ref[2:3, :] * u
    return bg, cg, h, u, u1, u2, y


def _conv_fwd(gates, taps, g_conv, gmat):
    s = gates.shape[0]
    tr = _row_tile(s, 512)
    hb = tr // SUBLANES

    def body(g_ref, halo_ref, taps_ref, gain_ref, gmat_ref, out_ref):
        i = pl.program_id(0)
        bg, _, _, _, _, _, y = _conv_forward_values(g_ref, halo_ref, taps_ref, i == 0)
        yc = bg * y
        ms = _group_sum(yc * yc, gmat_ref[...]) * (1.0 / GROUP)
        out_ref[...] = (yc * lax.rsqrt(ms + RMS_EPS) * gain_ref[...]).astype(BF16)

    return pl.pallas_call(
        body, name="conv_fwd", grid=(s // tr,),
        in_specs=[pl.BlockSpec((tr, GATE_COLS), lambda i: (i, 0)),
                  pl.BlockSpec((SUBLANES, GATE_COLS), lambda i: (jnp.maximum(i * hb - 1, 0), 0)),
                  pl.BlockSpec((SUBLANES, CONV_WIDTH), lambda i: (0, 0)),
                  pl.BlockSpec((1, CONV_WIDTH), lambda i: (0, 0)),
                  pl.BlockSpec((2 * LANES, LANES), lambda i: (0, 0))],
        out_specs=pl.BlockSpec((tr, CONV_WIDTH), lambda i: (i, 0)),
        out_shape=SDS((s, CONV_WIDTH), BF16),
        compiler_params=_params(("parallel",)))(gates, gates, taps, g_conv, gmat)


def _conv_bwd_gate(gates, dycn, taps, g_conv, gmat):
    s = gates.shape[0]
    tr = _row_tile(s, 512)
    hb = tr // SUBLANES

    def body(g_ref, halo_ref, dn_ref, taps_ref, gain_ref, gmat_ref, dbg_ref, dy_ref, sums_ref):
        i = pl.program_id(0)
        bg, _, _, u, u1, u2, y = _conv_forward_values(g_ref, halo_ref, taps_ref, i == 0)
        gmat_v = gmat_ref[...]
        yc = bg * y
        rstd = lax.rsqrt(_group_sum(yc * yc, gmat_v) * (1.0 / GROUP) + RMS_EPS)
        n = yc * rstd
        dout = dn_ref[...]
        dn = dout * gain_ref[...]
        dyc = rstd * (dn - n * (_group_sum(dn * n, gmat_v) * (1.0 / GROUP)))
        dbg_ref[...] = (dyc * y).astype(BF16)
        dy = dyc * bg
        dy_ref[...] = dy

        @pl.when(i == 0)
        def _():
            sums_ref[...] = jnp.zeros_like(sums_ref)

        sums_ref[0:1, :] += jnp.sum(dy * u2, axis=0, keepdims=True)
        sums_ref[1:2, :] += jnp.sum(dy * u1, axis=0, keepdims=True)
        sums_ref[2:3, :] += jnp.sum(dy * u, axis=0, keepdims=True)
        sums_ref[3:4, :] += jnp.sum(dout * n, axis=0, keepdims=True)

    return pl.pallas_call(
        body, name="conv_bwd_gate", grid=(s // tr,),
        in_specs=[pl.BlockSpec((tr, GATE_COLS), lambda i: (i, 0)),
                  pl.BlockSpec((SUBLANES, GATE_COLS), lambda i: (jnp.maximum(i * hb - 1, 0), 0)),
                  pl.BlockSpec((tr, CONV_WIDTH), lambda i: (i, 0)),
                  pl.BlockSpec((SUBLANES, CONV_WIDTH), lambda i: (0, 0)),
                  pl.BlockSpec((1, CONV_WIDTH), lambda i: (0, 0)),
                  pl.BlockSpec((2 * LANES, LANES), lambda i: (0, 0))],
        out_specs=[pl.BlockSpec((tr, CONV_WIDTH), lambda i: (i, 0)),
                   pl.BlockSpec((tr, CONV_WIDTH), lambda i: (i, 0)),
                   pl.BlockSpec((SUBLANES, CONV_WIDTH), lambda i: (0, 0))],
        out_shape=[SDS((s, CONV_WIDTH), BF16), SDS((s, CONV_WIDTH), F32), SDS((SUBLANES, CONV_WIDTH), F32)],
        compiler_params=_params(("arbitrary",)))(gates, gates, dycn, taps, g_conv, gmat)


def _dproj_assemble(gates, dy, dbg, dq, dk, dv, taps):
    s = gates.shape[0]
    tr = _row_tile(s, 512)
    hb = tr // SUBLANES
    last = s // SUBLANES - 1
    n_blocks = s // tr

    def body(g_ref, dy_ref, halo_ref, dbg_ref, dq_ref, dk_ref, dv_ref, taps_ref, out_ref):
        i = pl.program_id(0)
        gates_v = g_ref[...]
        cg = gates_v[:, CONV_WIDTH:2 * CONV_WIDTH]
        h = gates_v[:, 2 * CONV_WIDTH:]
        dy_v = dy_ref[...]
        last_block = i == n_blocks - 1
        nxt = lambda r: jnp.where(last_block, 0.0, halo_ref[r:r + 1, :])
        row = lax.broadcasted_iota(jnp.int32, (tr, CONV_WIDTH), 0)
        d1 = jnp.where(row == tr - 1, nxt(0), pltpu.roll(dy_v, tr - 1, 0))
        d2 = jnp.where(row == tr - 1, nxt(1), jnp.where(row == tr - 2, nxt(0), pltpu.roll(dy_v, tr - 2, 0)))
        du = taps_ref[2:3, :] * dy_v + taps_ref[1:2, :] * d1 + taps_ref[0:1, :] * d2
        out_ref[:, 0:CONV_WIDTH] = dbg_ref[...]
        out_ref[:, CONV_WIDTH:2 * CONV_WIDTH] = (du * h).astype(BF16)
        out_ref[:, 2 * CONV_WIDTH:GATE_COLS] = (du * cg).astype(BF16)
        out_ref[:, GATE_COLS:GATE_COLS + ATTN_WIDTH] = dq_ref[...]
        out_ref[:, GATE_COLS + ATTN_WIDTH:GATE_COLS + 2 * ATTN_WIDTH] = dk_ref[...].astype(BF16)
        out_ref[:, GATE_COLS + 2 * ATTN_WIDTH:] = dv_ref[...].astype(BF16)

    row_spec = lambda w: pl.BlockSpec((tr, w), lambda i: (i, 0))
    return pl.pallas_call(
        body, name="dproj_assemble", grid=(s // tr,),
        in_specs=[row_spec(GATE_COLS), row_spec(CONV_WIDTH),
                  pl.BlockSpec((SUBLANES, CONV_WIDTH), lambda i: (jnp.minimum((i + 1) * hb, last), 0)),
                  row_spec(CONV_WIDTH), row_spec(ATTN_WIDTH), row_spec(ATTN_WIDTH), row_spec(ATTN_WIDTH),
                  pl.BlockSpec((SUBLANES, CONV_WIDTH), lambda i: (0, 0))],
        out_specs=row_spec(IN_COLS),
        out_shape=SDS((s, IN_COLS), BF16),
        compiler_params=_params(("parallel",)))(gates, dy, dy, dbg, dq, dk, dv, taps)


def _stack_heads(rows, nb):
    lane = lax.broadcasted_iota(jnp.int32, (1, LANES), 1)
    zero = jnp.zeros((KEY_BLOCK, LANES), rows.dtype)
    parts = []
    for blk in range(nb):
        r = rows[blk * KEY_BLOCK:(blk + 1) * KEY_BLOCK]
        parts.append(jnp.where(lane < GROUP, r, zero))
        parts.append(jnp.where(lane < GROUP, zero, r))
    return jnp.concatenate(parts, axis=0)


def _stack_hilo(v, n_cols):
    return jnp.concatenate([_hilo(v[:, c * KEY_BLOCK:(c + 1) * KEY_BLOCK]) for c in range(n_cols)], axis=0)


def _causal_mask(tq, nb, diag_base):
    shape = (tq, 2 * nb * KEY_BLOCK)
    row = lax.broadcasted_iota(jnp.int32, shape, 0)
    col = lax.broadcasted_iota(jnp.int32, shape, 1)
    key = diag_base + (col // (2 * KEY_BLOCK)) * KEY_BLOCK + col % KEY_BLOCK
    return key < row


ANY = pl.BlockSpec(memory_space=pl.ANY)


def _remote_copy(src, dst, sems, idx, target):
    return pltpu.make_async_remote_copy(src_ref=src, dst_ref=dst, send_sem=sems[0].at[idx], recv_sem=sems[1].at[idx],
                                        device_id=target, device_id_type=MESH)


def _gather_copies(bufs, sems):
    x, y, c = _position()
    sends, arrivals = [], []
    for a in range(len(bufs)):
        mine = bufs[a].at[2 * x + y]
        for j, (fx, fy) in enumerate(CHIP_FLIPS):
            tx, ty = _flip(x, fx), _flip(y, fy)
            there = bufs[a].at[2 * tx + ty]
            sends.append(_remote_copy(mine, mine, sems, 3 * a + j, (tx, ty, c)))
            arrivals.append(_remote_copy(there, there, sems, 3 * a + j, (tx, ty, c)))
    return sends, arrivals


def _reduce_copies(ins, outs, sems):
    x, y, c = _position()
    sends, arrivals = [], []
    for a in range(len(ins)):
        h = ins[a].shape[1] // 2
        for f, (fx, fy, fc) in enumerate(DEVICE_FLIPS):
            tx, ty, tc = _flip(x, fx), _flip(y, fy), _flip(c, fc)
            src = ins[a].at[2 * tx + ty, pl.ds(pl.multiple_of(tc * h, h), h)]
            sends.append(_remote_copy(src, outs[a].at[f], sems, 7 * a + f, (tx, ty, tc)))
            arrivals.append(_remote_copy(outs[a].at[f], outs[a].at[f], sems, 7 * a + f, (tx, ty, tc)))
    return sends, arrivals


def _start_copies(make):
    sends, _ = make()
    for cp in sends:
        cp.start()


def _finish_copies(make):
    sends, arrivals = make()
    for cp in arrivals:
        cp.wait_recv()
    for cp in sends:
        cp.wait_send()


def _attn_fwd(qkv, g_attn, tri, gmat, shards):
    n_w = len(shards)
    s = qkv.shape[0]
    tq = _row_tile(s, ATTN_Q_TILE)
    tk = KEY_BLOCK
    nb = ATTN_KEY_BLOCKS
    width = nb * tk
    diag_trips = tq // width
    pairs = ATTN_WIDTH // LANES

    def body(q_ref, k_ref, v_ref, gain_ref, tri_ref, gmat_ref, *rest):
        o_ref, yn_ref, tot_ref = rest[n_w:n_w + 3]
        w_bufs, sems = rest[n_w + 3:2 * n_w + 3], rest[2 * n_w + 3:]
        copies = functools.partial(_gather_copies, w_bufs, sems)
        p, i = pl.program_id(0), pl.program_id(1)
        pl.when((p == 0) & (i == 0))(functools.partial(_start_copies, copies))
        q2 = q_ref[...]
        tri_v = tri_ref[...]

        def trip(t, carry, diag_base):
            run = [carry[0], carry[1]]
            oacc = carry[2]
            s0 = pl.multiple_of(t * width, width)
            ksel = _stack_heads(k_ref[pl.ds(s0, width), :], nb)
            vsel = _stack_heads(v_ref[pl.ds(s0, width), :], nb)
            z = lax.dot_general(q2, ksel, NT_DIMS, preferred_element_type=F32) * ATTN_SCALE
            log_beta, log_keep = _softplus_terms(z)
            if diag_base is not None:
                valid = _causal_mask(tq, nb, diag_base)
                log_keep = jnp.where(valid, log_keep, 0.0)
            ct = jnp.dot(_stack_hilo(log_keep, 2 * nb), tri_v, preferred_element_type=F32)
            a_parts = [None] * (2 * nb)
            for c in reversed(range(2 * nb)):
                h = c % 2
                ct_c = ct[c * tq:(c + 1) * tq]
                a_parts[c] = jnp.exp(log_beta[:, c * tk:(c + 1) * tk] + ct_c[:, :tk] + run[h])
                run[h] = run[h] + ct_c[:, tk:]
            a = jnp.concatenate(a_parts, axis=1)
            if diag_base is not None:
                a = jnp.where(valid, a, 0.0)
            oacc = oacc + jnp.dot(a.astype(BF16), vsel, preferred_element_type=F32)
            return run[0], run[1], oacc

        carry = (jnp.zeros((tq, tk), F32), jnp.zeros((tq, tk), F32), jnp.zeros((tq, LANES), F32))
        for d in reversed(range(diag_trips)):
            carry = trip(i * diag_trips + d, carry, d * width)
        n_full = i * diag_trips
        carry = lax.fori_loop(0, n_full, lambda it, c: trip(n_full - 1 - it, c, None), carry)
        run_a, run_b, oacc = carry
        lane = lax.broadcasted_iota(jnp.int32, (1, LANES), 1)
        o_ref[...] = oacc
        tot_ref[...] = jnp.where(lane < GROUP, run_a, run_b)
        ms = _group_sum(oacc * oacc, gmat_ref[...]) * (1.0 / GROUP)
        yn_ref[...] = (oacc * lax.rsqrt(ms + RMS_EPS) * gain_ref[...]).astype(BF16)
        pl.when((p == pairs - 1) & (i == pl.num_programs(1) - 1))(functools.partial(_finish_copies, copies))

    blk = lambda: pl.BlockSpec((tq, LANES), lambda p, i: (i, p))
    return pl.pallas_call(
        body, name="attn_fwd", grid=(pairs, s // tq),
        in_specs=[pl.BlockSpec((tq, LANES), lambda p, i: (i, p)),
                  pl.BlockSpec((s, LANES), lambda p, i: (0, pairs + p)),
                  pl.BlockSpec((s, LANES), lambda p, i: (0, 2 * pairs + p)),
                  pl.BlockSpec((1, LANES), lambda p, i: (0, p)),
                  pl.BlockSpec((2 * tk, 2 * tk), lambda p, i: (0, 0)),
                  pl.BlockSpec((2 * LANES, LANES), lambda p, i: (0, 0))] + [ANY] * n_w,
        out_specs=[blk(), blk(), blk()] + [ANY] * n_w,
        out_shape=[SDS((s, ATTN_WIDTH), F32), SDS((s, ATTN_WIDTH), BF16), SDS((s, ATTN_WIDTH), F32)]
        + [SDS(w.shape, w.dtype) for w in shards],
        input_output_aliases={6 + a: 3 + a for a in range(n_w)},
        scratch_shapes=[pltpu.SemaphoreType.DMA((3 * n_w,)), pltpu.SemaphoreType.DMA((3 * n_w,))],
        compiler_params=_params(("arbitrary", "arbitrary")))(qkv, qkv, qkv, g_attn, tri, gmat, *shards)


def _attn_bwd(qkv, o, tot, dyn, g_attn, tri, gmat, partials):
    n_g = len(partials)
    s = qkv.shape[0]
    tq = _row_tile(s, ATTN_Q_TILE)
    tk = KEY_BLOCK
    nb = ATTN_KEY_BLOCKS
    width = nb * tk
    diag_trips = tq // width
    pairs = ATTN_WIDTH // LANES

    def body(q_ref, k_ref, v_ref, o_ref, tot_ref, dyn_ref, gain_ref, tri_ref, gmat_ref, *rest):
        g_ins, (dq_ref, dk_ref, dv_ref, dg_ref) = rest[:n_g], rest[n_g:n_g + 4]
        g_outs, sems = rest[n_g + 4:2 * n_g + 4], rest[2 * n_g + 4:]
        copies = functools.partial(_reduce_copies, g_ins, g_outs, sems)
        p, i = pl.program_id(0), pl.program_id(1)
        pl.when((p == 0) & (i == 0))(functools.partial(_start_copies, copies))

        @pl.when(i == 0)
        def _():
            dk_ref[...] = jnp.zeros_like(dk_ref)
            dv_ref[...] = jnp.zeros_like(dv_ref)
            dg_ref[...] = jnp.zeros_like(dg_ref)

        gmat_v = gmat_ref[...]
        o_v = o_ref[...]
        rstd = lax.rsqrt(_group_sum(o_v * o_v, gmat_v) * (1.0 / GROUP) + RMS_EPS)
        n = o_v * rstd
        dout = dyn_ref[...]
        dg_ref[0:1, :] += jnp.sum(dout * n, axis=0, keepdims=True)
        dn = dout * gain_ref[...]
        do2 = (rstd * (dn - n * (_group_sum(dn * n, gmat_v) * (1.0 / GROUP)))).astype(BF16)
        q2 = q_ref[...]
        tot_v = tot_ref[...]
        tots = (jnp.broadcast_to(tot_v[:, 0:1], (tq, tk)), jnp.broadcast_to(tot_v[:, GROUP:GROUP + 1], (tq, tk)))
        tri_v = tri_ref[...]
        lane = lax.broadcasted_iota(jnp.int32, (1, LANES), 1)

        def trip(t, carry, diag_base):
            pref_l = [carry[0], carry[1]]
            pref_g = [carry[2], carry[3]]
            dq = carry[4]
            s0 = pl.multiple_of(t * width, width)
            ksel = _stack_heads(k_ref[pl.ds(s0, width), :], nb)
            vsel = _stack_heads(v_ref[pl.ds(s0, width), :], nb)
            z = lax.dot_general(q2, ksel, NT_DIMS, preferred_element_type=F32) * ATTN_SCALE
            log_beta, log_keep = _softplus_terms(z)
            if diag_base is not None:
                valid = _causal_mask(tq, nb, diag_base)
                log_keep = jnp.where(valid, log_keep, 0.0)
            ctl = jnp.dot(_stack_hilo(log_keep, 2 * nb), tri_v, preferred_element_type=F32)
            da = lax.dot_general(do2, vsel, NT_DIMS, preferred_element_type=F32)
            a_parts = []
            for c in range(2 * nb):
                h = c % 2
                ct_c = ctl[c * tq:(c + 1) * tq]
                cols = slice(c * tk, (c + 1) * tk)
                suffix = tots[h] - pref_l[h] - ct_c[:, :tk] - log_keep[:, cols]
                pref_l[h] = pref_l[h] + ct_c[:, tk:]
                a_parts.append(jnp.exp(log_beta[:, cols] + suffix))
            a = jnp.concatenate(a_parts, axis=1)
            if diag_base is not None:
                a = jnp.where(valid, a, 0.0)
            g = a * da
            ctg = jnp.dot(_stack_hilo(g, 2 * nb), tri_v, preferred_element_type=F32)
            dz_parts = []
            for c in range(2 * nb):
                h = c % 2
                ct_c = ctg[c * tq:(c + 1) * tq]
                cols = slice(c * tk, (c + 1) * tk)
                prefix = pref_g[h] + ct_c[:, :tk]
                pref_g[h] = pref_g[h] + ct_c[:, tk:]
                beta = jnp.exp(log_beta[:, cols])
                dz_parts.append(g[:, cols] * (1.0 - beta) - prefix * beta)
            dz = jnp.concatenate(dz_parts, axis=1) * ATTN_SCALE
            if diag_base is not None:
                dz = jnp.where(valid, dz, 0.0)
            dzb = dz.astype(BF16)
            dq = dq + jnp.dot(dzb, ksel, preferred_element_type=F32)
            dkt = lax.dot_general(dzb, q2, TN_DIMS, preferred_element_type=F32)
            dvt = lax.dot_general(a.astype(BF16), do2, TN_DIMS, preferred_element_type=F32)
            for blk in range(nb):
                ra, rb = slice(2 * blk * tk, (2 * blk + 1) * tk), slice((2 * blk + 1) * tk, (2 * blk + 2) * tk)
                rows = pl.ds(pl.multiple_of(s0 + blk * tk, tk), tk)
                dk_ref[rows, :] += jnp.where(lane < GROUP, dkt[ra], dkt[rb])
                dv_ref[rows, :] += jnp.where(lane < GROUP, dvt[ra], dvt[rb])
            return pref_l[0], pref_l[1], pref_g[0], pref_g[1], dq

        zeros_qk = jnp.zeros((tq, tk), F32)
        carry = (zeros_qk, zeros_qk, zeros_qk, zeros_qk, jnp.zeros((tq, LANES), F32))
        carry = lax.fori_loop(0, i * diag_trips, lambda t, c: trip(t, c, None), carry)
        for d in range(diag_trips):
            carry = trip(i * diag_trips + d, carry, d * width)
        dq_ref[...] = carry[4].astype(BF16)
        pl.when((p == pairs - 1) & (i == pl.num_programs(1) - 1))(functools.partial(_finish_copies, copies))

    blk = lambda: pl.BlockSpec((tq, LANES), lambda p, i: (i, p))
    col = lambda: pl.BlockSpec((s, LANES), lambda p, i: (0, p))
    n_peers = len(DEVICE_FLIPS)
    return pl.pallas_call(
        body, name="attn_bwd", grid=(pairs, s // tq),
        in_specs=[pl.BlockSpec((tq, LANES), lambda p, i: (i, p)),
                  pl.BlockSpec((s, LANES), lambda p, i: (0, pairs + p)),
                  pl.BlockSpec((s, LANES), lambda p, i: (0, 2 * pairs + p)),
                  blk(), blk(), blk(),
                  pl.BlockSpec((1, LANES), lambda p, i: (0, p)),
                  pl.BlockSpec((2 * tk, 2 * tk), lambda p, i: (0, 0)),
                  pl.BlockSpec((2 * LANES, LANES), lambda p, i: (0, 0))] + [ANY] * n_g,
        out_specs=[blk(), col(), col(), pl.BlockSpec((SUBLANES, LANES), lambda p, i: (0, p))] + [ANY] * n_g,
        out_shape=[SDS((s, ATTN_WIDTH), BF16), SDS((s, ATTN_WIDTH), F32), SDS((s, ATTN_WIDTH), F32),
                   SDS((SUBLANES, ATTN_WIDTH), F32)]
        + [SDS((n_peers, g.shape[1] // 2, g.shape[2]), g.dtype) for g in partials],
        scratch_shapes=[pltpu.SemaphoreType.DMA((n_peers * n_g,)), pltpu.SemaphoreType.DMA((n_peers * n_g,))],
        compiler_params=_params(("arbitrary", "arbitrary")))(qkv, qkv, qkv, o, tot, dyn, g_attn, tri, gmat, *partials)


def _mix_ln1(ycn, yan, w_out, x, g, b):
    s = x.shape[0]
    tm = _row_tile(s, 512)

    def body(yc_ref, ya_ref, w_ref, x_ref, g_ref, b_ref, x1_ref, xhat_ref, rstd_ref):
        mix = jnp.dot(yc_ref[...], w_ref[0:CONV_WIDTH, :], preferred_element_type=F32)
        mix = mix + jnp.dot(ya_ref[...], w_ref[CONV_WIDTH:, :], preferred_element_type=F32)
        x1, xhat, rstd = _layer_norm_fwd(ALPHA * x_ref[...] + mix, g_ref[...], b_ref[...])
        x1_ref[...] = x1
        xhat_ref[...] = xhat
        rstd_ref[...] = rstd

    row = lambda w: pl.BlockSpec((tm, w), lambda i: (i, 0))
    vec = lambda: pl.BlockSpec((1, D_MODEL), lambda i: (0, 0))
    return pl.pallas_call(
        body, name="mix_ln1", grid=(s // tm,),
        in_specs=[row(CONV_WIDTH), row(ATTN_WIDTH), pl.BlockSpec((D_MODEL, D_MODEL), lambda i: (0, 0)),
                  row(D_MODEL), vec(), vec()],
        out_specs=[row(D_MODEL), row(D_MODEL), row(1)],
        out_shape=[SDS((s, D_MODEL), F32), SDS((s, D_MODEL), F32), SDS((s, 1), F32)],
        compiler_params=_params(("parallel",)))(ycn, yan, w_out, x, g, b)


def _mlp_fwd_loss(x1, w_up, w_down, target, g, b):
    s = x1.shape[0]
    tm = _row_tile(s, 256)

    def body(x1_ref, wu_ref, wd_ref, t_ref, g_ref, b_ref, dpre_ref, sums_ref, loss_ref):
        i = pl.program_id(0)
        x1_v = x1_ref[...]
        xb = x1_v.astype(BF16)
        ffn = jnp.zeros((tm, D_MODEL), F32)
        for k in range(N_CHIPS):
            r = jnp.maximum(jnp.dot(xb, wu_ref[k], preferred_element_type=F32), 0.0)
            ffn = ffn + jnp.dot((r * r).astype(BF16), wd_ref[k], preferred_element_type=F32)
        g_v = g_ref[...]
        x2, xhat, rstd = _layer_norm_fwd(ALPHA * x1_v + ffn, g_v, b_ref[...])
        err = x2 - t_ref[...]
        dx2 = err * (1.0 / D_MODEL)
        dpre_ref[...] = _layer_norm_bwd(dx2, xhat, rstd, g_v)

        @pl.when(i == 0)
        def _():
            sums_ref[...] = jnp.zeros_like(sums_ref)
            loss_ref[...] = jnp.zeros_like(loss_ref)

        sums_ref[0:1, :] += jnp.sum(dx2 * xhat, axis=0, keepdims=True)
        sums_ref[1:2, :] += jnp.sum(dx2, axis=0, keepdims=True)
        loss_ref[...] += jnp.sum(jnp.sum(err * err, axis=1, keepdims=True), axis=0, keepdims=True) * (0.5 / D_MODEL)

    row = lambda: pl.BlockSpec((tm, D_MODEL), lambda i: (i, 0))
    vec = lambda: pl.BlockSpec((1, D_MODEL), lambda i: (0, 0))
    wspec = lambda: pl.BlockSpec((N_CHIPS, D_MODEL, FF_SHARD), lambda i: (0, 0, 0))
    return pl.pallas_call(
        body, name="mlp_fwd_loss", grid=(s // tm,),
        in_specs=[row(), wspec(), wspec(), row(), vec(), vec()],
        out_specs=[row(), pl.BlockSpec((SUBLANES, D_MODEL), lambda i: (0, 0)),
                   pl.BlockSpec((SUBLANES, LANES), lambda i: (0, 0))],
        out_shape=[SDS((s, D_MODEL), F32), SDS((SUBLANES, D_MODEL), F32), SDS((SUBLANES, LANES), F32)],
        compiler_params=_params(("arbitrary",)))(x1, w_up, w_down, target, g, b)


def _mlp_bwd_ln1(x1, dpre2, w_up, w_down, xhat1, rstd1, g1):
    s = x1.shape[0]
    tm = _row_tile(s, 256)

    def body(x1_ref, d2_ref, wu_ref, wd_ref, xh_ref, rs_ref, g_ref, hid_ref, dup_ref, dpre_ref, sums_ref):
        i = pl.program_id(0)
        xb = x1_ref[...].astype(BF16)
        d2 = d2_ref[...]
        d2b = d2.astype(BF16)
        dx1 = ALPHA * d2
        for k in range(N_CHIPS):
            r = jnp.maximum(jnp.dot(xb, wu_ref[k], preferred_element_type=F32), 0.0)
            hid_ref[:, FF_SHARD * k:FF_SHARD * (k + 1)] = (r * r).astype(BF16)
            dhid = lax.dot_general(d2b, wd_ref[k], NT_DIMS, preferred_element_type=F32)
            dupb = (dhid * (2.0 * r)).astype(BF16)
            dup_ref[:, FF_SHARD * k:FF_SHARD * (k + 1)] = dupb
            dx1 = dx1 + lax.dot_general(dupb, wu_ref[k], NT_DIMS, preferred_element_type=F32)
        xhat = xh_ref[...]
        dpre_ref[...] = _layer_norm_bwd(dx1, xhat, rs_ref[...], g_ref[...])

        @pl.when(i == 0)
        def _():
            sums_ref[...] = jnp.zeros_like(sums_ref)

        sums_ref[0:1, :] += jnp.sum(dx1 * xhat, axis=0, keepdims=True)
        sums_ref[1:2, :] += jnp.sum(dx1, axis=0, keepdims=True)

    row = lambda w: pl.BlockSpec((tm, w), lambda i: (i, 0))
    wspec = lambda: pl.BlockSpec((N_CHIPS, D_MODEL, FF_SHARD), lambda i: (0, 0, 0))
    return pl.pallas_call(
        body, name="mlp_bwd_ln1", grid=(s // tm,),
        in_specs=[row(D_MODEL), row(D_MODEL), wspec(), wspec(), row(D_MODEL), row(1),
                  pl.BlockSpec((1, D_MODEL), lambda i: (0, 0))],
        out_specs=[row(D_FF), row(D_FF), row(D_MODEL), pl.BlockSpec((SUBLANES, D_MODEL), lambda i: (0, 0))],
        out_shape=[SDS((s, D_FF), BF16), SDS((s, D_FF), BF16), SDS((s, D_MODEL), F32),
                   SDS((SUBLANES, D_MODEL), F32)],
        compiler_params=_params(("arbitrary",)))(x1, dpre2, w_up, w_down, xhat1, rstd1, g1)


def _grad_tn(a, b, name, out_cols, stacked):
    s, ka = a.shape
    n = b.shape[1]
    ts = _row_tile(s, 512)
    n_steps = s // ts
    if stacked:
        tka, tn = ka, out_cols
        grid = (1, n // tn, n_steps)
        shape = (n // tn, ka, tn)
        out_spec = lambda: pl.BlockSpec((None, tka, tn), lambda r, c, t: (c, 0, 0))
    else:
        tka, tn = min(ka, 1024), n
        grid = (ka // tka, 1, n_steps)
        shape = (ka, n)
        out_spec = lambda: pl.BlockSpec((tka, tn), lambda r, c, t: (r, 0))

    def body(a_ref, b_ref, o_ref, ob_ref):
        t = pl.program_id(2)

        @pl.when(t == 0)
        def _():
            o_ref[...] = jnp.zeros_like(o_ref)

        o_ref[...] += lax.dot_general(a_ref[...].astype(BF16), b_ref[...].astype(BF16), TN_DIMS,
                                      preferred_element_type=F32)

        @pl.when(t == n_steps - 1)
        def _():
            ob_ref[...] = o_ref[...].astype(BF16)

    return pl.pallas_call(
        body, name=name, grid=grid,
        in_specs=[pl.BlockSpec((ts, tka), lambda r, c, t: (t, r)),
                  pl.BlockSpec((ts, tn), lambda r, c, t: (t, c))],
        out_specs=[out_spec(), out_spec()], out_shape=[SDS(shape, F32), SDS(shape, BF16)],
        compiler_params=_params(("parallel", "parallel", "arbitrary")))(a, b)


def _dmix(dpre1, w_out):
    s = dpre1.shape[0]
    tm = _row_tile(s, 512)

    def body(d_ref, w_ref, dc_ref, da_ref):
        db = d_ref[...].astype(BF16)
        dc_ref[...] = lax.dot_general(db, w_ref[0:CONV_WIDTH, :], NT_DIMS, preferred_element_type=F32)
        da_ref[...] = lax.dot_general(db, w_ref[CONV_WIDTH:, :], NT_DIMS, preferred_element_type=F32)

    return pl.pallas_call(
        body, name="dmix", grid=(s // tm,),
        in_specs=[pl.BlockSpec((tm, D_MODEL), lambda i: (i, 0)),
                  pl.BlockSpec((D_MODEL, D_MODEL), lambda i: (0, 0))],
        out_specs=[pl.BlockSpec((tm, CONV_WIDTH), lambda i: (i, 0)),
                   pl.BlockSpec((tm, ATTN_WIDTH), lambda i: (i, 0))],
        out_shape=[SDS((s, CONV_WIDTH), F32), SDS((s, ATTN_WIDTH), F32)],
        compiler_params=_params(("parallel",)))(dpre1, w_out)


def _grad_x(dproj, w_in, dpre1, partial):
    s = dproj.shape[0]
    tm = _row_tile(s, 512)
    n_peers = len(DEVICE_FLIPS)

    def body(dp_ref, w_ref, d1_ref, g_in, o_ref, g_out, send_sems, recv_sems):
        copies = functools.partial(_reduce_copies, [g_in], [g_out], (send_sems, recv_sems))
        i = pl.program_id(0)
        pl.when(i == 0)(functools.partial(_start_copies, copies))
        acc = ALPHA * d1_ref[...]
        for k in range(N_CHIPS):
            acc = acc + lax.dot_general(dp_ref[:, IN_SHARD * k:IN_SHARD * (k + 1)], w_ref[k], NT_DIMS,
                                        preferred_element_type=F32)
        o_ref[...] = acc
        pl.when(i == pl.num_programs(0) - 1)(functools.partial(_finish_copies, copies))

    return pl.pallas_call(
        body, name="grad_x", grid=(s // tm,),
        in_specs=[pl.BlockSpec((tm, IN_COLS), lambda i: (i, 0)),
                  pl.BlockSpec((N_CHIPS, D_MODEL, IN_SHARD), lambda i: (0, 0, 0)),
                  pl.BlockSpec((tm, D_MODEL), lambda i: (i, 0)), ANY],
        out_specs=[pl.BlockSpec((tm, D_MODEL), lambda i: (i, 0)), ANY],
        out_shape=[SDS((s, D_MODEL), F32), SDS((n_peers, partial.shape[1] // 2, partial.shape[2]), partial.dtype)],
        scratch_shapes=[pltpu.SemaphoreType.DMA((n_peers,)), pltpu.SemaphoreType.DMA((n_peers,))],
        compiler_params=_params(("arbitrary",)))(dproj, w_in, dpre1, partial)


def _adamw(w, g, m, v, name):
    r, c = w.shape
    tr = _row_tile(r, 256)

    def body(w_ref, g_ref, m_ref, v_ref, d_ref, nm_ref, nv_ref):
        g_v = g_ref[...]
        nm = ADAM_B1 * m_ref[...] + (1.0 - ADAM_B1) * g_v
        nv = ADAM_B2 * v_ref[...] + (1.0 - ADAM_B2) * (g_v * g_v)
        m_hat = nm / (1.0 - ADAM_B1 ** ADAM_STEP)
        v_hat = nv / (1.0 - ADAM_B2 ** ADAM_STEP)
        d_ref[...] = -ADAM_LR * (m_hat / (jnp.sqrt(v_hat) + ADAM_EPS) + ADAM_WD * w_ref[...])
        nm_ref[...] = nm
        nv_ref[...] = nv

    spec = lambda: pl.BlockSpec((tr, c), lambda i: (i, 0))
    return pl.pallas_call(
        body, name=name, grid=(r // tr,),
        in_specs=[spec(), spec(), spec(), spec()], out_specs=[spec(), spec(), spec()],
        out_shape=[SDS((r, c), F32)] * 3, compiler_params=_params(("parallel",)))(w, g, m, v)


def _sum_partials(kc_idx, grad, recv, name):
    _, _, h, cols = grad.shape
    th = _row_tile(h, 128)
    n_peers = recv.shape[0]

    def body(kc_ref, own_ref, r_ref, o_ref):
        acc = own_ref[...]
        for f in range(n_peers):
            acc = acc + r_ref[f].astype(F32)
        o_ref[...] = acc

    grid_spec = pltpu.PrefetchScalarGridSpec(
        num_scalar_prefetch=1, grid=(h // th,),
        in_specs=[pl.BlockSpec((None, None, th, cols), lambda t, kc: (kc[0], kc[1], t, 0)),
                  pl.BlockSpec((n_peers, th, cols), lambda t, kc: (0, t, 0))],
        out_specs=pl.BlockSpec((th, cols), lambda t, kc: (kc[1] * (h // th) + t, 0)))
    return pl.pallas_call(
        body, name=name, grid_spec=grid_spec, out_shape=SDS((2 * h, cols), F32),
        compiler_params=_params(("parallel",)))(kc_idx, grad, recv)


def _gather_weights(shards, conv_shard):
    n = len(shards)

    def body(*refs):
        outs, conv_out = refs[n + 1:2 * n + 1], refs[2 * n + 1]
        send_sems, recv_sems = refs[2 * n + 2:]
        x, y, c = _position()
        k = 2 * x + y
        sibling = (x, y, 1 - c)
        chips = [(_flip(x, fx), _flip(y, fy)) for fx, fy in CHIP_FLIPS]

        def half(a, rows_of_core):
            h = shards[a].shape[1] // 2
            return pl.ds(pl.multiple_of(rows_of_core * h, h), h)

        def remote(src, dst, idx, target):
            return pltpu.make_async_remote_copy(src_ref=src, dst_ref=dst, send_sem=send_sems.at[idx],
                                                recv_sem=recv_sems.at[idx], device_id=target,
                                                device_id_type=MESH)

        started = []
        for a in range(n):
            mine = outs[a].at[k, half(a, c)]
            for j, (tx, ty) in enumerate(chips):
                cp = remote(mine, mine, 6 * a + j, (tx, ty, c))
                cp.start()
                started.append(cp)
        for j, (tx, ty) in enumerate(chips):
            cp = remote(conv_out.at[k], conv_out.at[k], 6 * n + j, (tx, ty, c))
            cp.start()
            started.append(cp)
        for a in range(n):
            for j, (tx, ty) in enumerate(chips):
                kj = 2 * tx + ty
                landed = outs[a].at[kj, half(a, c)]
                remote(landed, landed, 6 * a + j, sibling).wait_recv()
                cp = remote(landed, landed, 6 * a + 3 + j, sibling)
                cp.start()
                started.append(cp)
        for a in range(n):
            for j, (tx, ty) in enumerate(chips):
                kj = 2 * tx + ty
                other = outs[a].at[kj, half(a, 1 - c)]
                remote(other, other, 6 * a + 3 + j, sibling).wait_recv()
        for j, (tx, ty) in enumerate(chips):
            kj = 2 * tx + ty
            remote(conv_out.at[kj], conv_out.at[kj], 6 * n + j, sibling).wait_recv()
        for cp in started:
            cp.wait_send()

    out_shape = [SDS(w.shape, w.dtype) for w in shards] + [SDS(conv_shard.shape, conv_shard.dtype)]
    n_sems = 6 * n + 3
    return pl.pallas_call(
        body, name="gather_weights", in_specs=[ANY] * (n + 1), out_specs=[ANY] * (n + 1),
        out_shape=out_shape, input_output_aliases={a: a for a in range(n + 1)},
        scratch_shapes=[pltpu.SemaphoreType.DMA((n_sems,)), pltpu.SemaphoreType.DMA((n_sems,))])(
            *shards, conv_shard)


def _finish_exchange(pieces, vec):
    n = len(pieces)
    n_dev = 2 * N_CHIPS

    def body(*refs):
        v_ref = refs[n]
        outs, o_ref = refs[n + 1:2 * n + 1], refs[2 * n + 1]
        buf, send_sems, recv_sems = refs[2 * n + 2:]
        x, y, c = _position()
        sibling = (x, y, 1 - c)
        me = 4 * x + 2 * y + c
        buf[me] = v_ref[...]
        started = []
        for f, (fx, fy, fc) in enumerate(DEVICE_FLIPS):
            cp = pltpu.make_async_remote_copy(
                src_ref=v_ref, dst_ref=buf.at[me], send_sem=send_sems.at[n + f], recv_sem=recv_sems.at[n + f],
                device_id=(_flip(x, fx), _flip(y, fy), _flip(c, fc)), device_id_type=MESH)
            cp.start()
            started.append(cp)
        for a in range(n):
            h = pieces[a].shape[0] // 2
            mine = outs[a].at[pl.ds(pl.multiple_of(c * h, h), h)]
            cp = pltpu.make_async_remote_copy(
                src_ref=mine, dst_ref=mine, send_sem=send_sems.at[a], recv_sem=recv_sems.at[a],
                device_id=sibling, device_id_type=MESH)
            cp.start()
            started.append(cp)
        for a in range(n):
            h = pieces[a].shape[0] // 2
            theirs = outs[a].at[pl.ds(pl.multiple_of((1 - c) * h, h), h)]
            pltpu.make_async_remote_copy(
                src_ref=theirs, dst_ref=theirs, send_sem=send_sems.at[a], recv_sem=recv_sems.at[a],
                device_id=sibling, device_id_type=MESH).wait_recv()
        for f, (fx, fy, fc) in enumerate(DEVICE_FLIPS):
            src = 4 * _flip(x, fx) + 2 * _flip(y, fy) + _flip(c, fc)
            pltpu.make_async_remote_copy(
                src_ref=v_ref, dst_ref=buf.at[src], send_sem=send_sems.at[n + f], recv_sem=recv_sems.at[n + f],
                device_id=(x, y, c), device_id_type=MESH).wait_recv()
        for cp in started:
            cp.wait_send()
        acc = buf[0]
        for d in range(1, n_dev):
            acc = acc + buf[d]
        o_ref[...] = acc

    vmem = pl.BlockSpec(memory_space=pltpu.VMEM)
    out_shape = [SDS(p.shape, p.dtype) for p in pieces] + [SDS(vec.shape, vec.dtype)]
    n_sems = n + n_dev - 1
    return pl.pallas_call(
        body, name="finish_exchange", in_specs=[ANY] * n + [vmem], out_specs=[ANY] * n + [vmem],
        out_shape=out_shape, input_output_aliases={a: a for a in range(n)},
        scratch_shapes=[pltpu.VMEM((n_dev,) + vec.shape, vec.dtype), pltpu.SemaphoreType.DMA((n_sems,)),
                        pltpu.SemaphoreType.DMA((n_sems,))])(*pieces, vec)


def _constants():
    r = jnp.arange(2 * KEY_BLOCK)[:, None] % KEY_BLOCK
    c = jnp.arange(2 * KEY_BLOCK)[None, :]
    later = jnp.where(c < KEY_BLOCK, r > c, True).astype(BF16)
    earlier = jnp.where(c < KEY_BLOCK, r < c, True).astype(BF16)
    gr = (jnp.arange(2 * LANES)[:, None] % LANES) // GROUP
    gc = jnp.arange(LANES)[None, :] // GROUP
    gmat = (gr == gc).astype(BF16)
    return later, earlier, gmat


def _rows(v):
    return v.reshape(-1, LANES)


def kernel(x, w_in, conv_w, g_conv, g_attn, w_out, ln1_g, ln1_b, w_up, w_down, ln2_g, ln2_b, loss_target, m_w_in, m_conv_w, m_g_conv, m_g_attn, m_w_out, m_ln1_g, m_ln1_b, m_w_up, m_w_down, m_ln2_g, m_ln2_b, v_w_in, v_conv_w, v_g_conv, v_g_attn, v_w_out, v_ln1_g, v_ln1_b, v_w_up, v_w_down, v_ln2_g, v_ln2_b):
    xs, target = x[0], loss_target[0]
    mesh_x, mesh_y, mesh_c = _position()
    k_idx = 2 * mesh_x + mesh_y
    kc_idx = jnp.stack([k_idx, mesh_c]).astype(jnp.int32)
    tri_later, tri_earlier, gmat = _constants()

    w_in_b, w_out_b, w_up_b, w_down_b = [
        _cast_into_slot(kc_idx, w[0], "cast_" + nm)
        for w, nm in ((w_in, "w_in"), (w_out, "w_out"), (w_up, "w_up"), (w_down, "w_down"))]
    conv_slot = jnp.pad(conv_w, ((0, 0), (0, SUBLANES - conv_w.shape[1]), (0, 0)))
    conv_b = lax.dynamic_update_slice(jnp.zeros((N_CHIPS, SUBLANES, LANES), F32), conv_slot, (k_idx, 0, 0))
    w_in_f, conv_f = _gather_weights([w_in_b], conv_b)
    taps = jnp.transpose(conv_f, (1, 0, 2)).reshape(SUBLANES, CONV_WIDTH)

    gates, qkv = _proj(xs, w_in_f)
    ycn = _conv_fwd(gates, taps, g_conv, gmat)
    o, yan, tot, w_out_f, w_up_f, w_down_f = _attn_fwd(qkv, g_attn, tri_later, gmat, [w_out_b, w_up_b, w_down_b])
    w_out_f = w_out_f.reshape(D_MODEL, D_MODEL)
    x1, xhat1, rstd1 = _mix_ln1(ycn, yan, w_out_f, xs, ln1_g, ln1_b)
    dpre2, ln2_sums, loss_sum = _mlp_fwd_loss(x1, w_up_f, w_down_f, target, ln2_g, ln2_b)

    hid, dup, dpre1, ln1_sums = _mlp_bwd_ln1(x1, dpre2, w_up_f, w_down_f, xhat1, rstd1, ln1_g)
    gw_up = _grad_tn(x1, dup, "grad_w_up", FF_SHARD, True)
    gw_down = [g.reshape(N_CHIPS, FF_SHARD, D_MODEL) for g in _grad_tn(hid, dpre2, "grad_w_down", D_MODEL, False)]
    gw_out_conv = _grad_tn(ycn, dpre1, "grad_w_out_conv", D_MODEL, False)
    gw_out_attn = _grad_tn(yan, dpre1, "grad_w_out_attn", D_MODEL, False)
    gw_out = [jnp.concatenate([gc_, ga_], axis=0).reshape(N_CHIPS, D_MODEL // N_CHIPS, D_MODEL)
              for gc_, ga_ in zip(gw_out_conv, gw_out_attn)]
    dycn, dyan = _dmix(dpre1, w_out_f)
    dq, dk, dv, gattn_sums, recv_out, recv_up, recv_down = _attn_bwd(
        qkv, o, tot, dyan, g_attn, tri_earlier, gmat, [gw_out[1], gw_up[1], gw_down[1]])
    dbg, dy, conv_sums = _conv_bwd_gate(gates, dycn, taps, g_conv, gmat)
    dproj = _dproj_assemble(gates, dy, dbg, dq, dk, dv, taps)
    gw_in = _grad_tn(xs, dproj, "grad_w_in", IN_SHARD, True)
    grad_x, recv_in = _grad_x(dproj, w_in_f, dpre1, gw_in[1])

    halves = lambda g: g.reshape(N_CHIPS, 2, g.shape[1] // 2, g.shape[2])
    pieces = [_sum_partials(kc_idx, halves(g[0]), r, "sum_partials_" + nm)
              for g, r, nm in ((gw_in, recv_in, "w_in"), (gw_out, recv_out, "w_out"), (gw_up, recv_up, "w_up"),
                               (gw_down, recv_down, "w_down"))]
    conv_rows = jnp.transpose(conv_sums[0:3].reshape(3, N_CHIPS, LANES), (1, 0, 2)).reshape(3 * N_CHIPS, LANES)
    small = jnp.concatenate([
        loss_sum, _rows(conv_sums[3]), _rows(gattn_sums[0]), _rows(ln1_sums[0]), _rows(ln1_sums[1]),
        _rows(ln2_sums[0]), _rows(ln2_sums[1]), conv_rows,
        jnp.zeros((SMALL_ROWS - ROW_CONVW - 3 * N_CHIPS, LANES), F32)], axis=0)
    g_w_in, g_w_out, g_w_up, g_w_down, total = _finish_exchange(pieces, small)
    loss = total[ROW_LOSS, 0]
    g_conv_w = lax.dynamic_slice(total, (ROW_CONVW + 3 * k_idx, 0), (3, LANES))

    def pack(gc_, ga_, l1g, l1b, l2g, l2b, cw):
        return jnp.concatenate([_rows(gc_), _rows(ga_), _rows(l1g), _rows(l1b), _rows(l2g), _rows(l2b), cw[0],
                                jnp.zeros((PARAM_ROWS + SUBLANES - ROW_CONVW - 3, LANES), F32)], axis=0)

    small_w = pack(g_conv, g_attn, ln1_g, ln1_b, ln2_g, ln2_b, conv_w)
    small_m = pack(m_g_conv, m_g_attn, m_ln1_g, m_ln1_b, m_ln2_g, m_ln2_b, m_conv_w)
    small_v = pack(v_g_conv, v_g_attn, v_ln1_g, v_ln1_b, v_ln2_g, v_ln2_b, v_conv_w)
    small_g = jnp.concatenate([total[ROW_GCONV:ROW_CONVW], g_conv_w,
                               jnp.zeros((PARAM_ROWS + SUBLANES - ROW_CONVW - 3, LANES), F32)], axis=0)
    small_out = _adamw(small_w, small_g, small_m, small_v, "adamw_small")

    def unpack(p):
        off = ROW_GCONV
        vec = lambda a, b: p[a - off:b - off].reshape(1, -1)
        return {"g_conv": vec(ROW_GCONV, ROW_GATTN), "g_attn": vec(ROW_GATTN, ROW_LN1G),
                "ln1_g": vec(ROW_LN1G, ROW_LN1B), "ln1_b": vec(ROW_LN1B, ROW_LN2G),
                "ln2_g": vec(ROW_LN2G, ROW_LN2B), "ln2_b": vec(ROW_LN2B, ROW_CONVW),
                "conv_w": p[ROW_CONVW - off:ROW_CONVW - off + 3][None]}

    big_out = {
        "w_in": _adamw(w_in[0], g_w_in, m_w_in[0], v_w_in[0], "adamw_w_in"),
        "w_out": _adamw(w_out[0], g_w_out, m_w_out[0], v_w_out[0], "adamw_w_out"),
        "w_up": _adamw(w_up[0], g_w_up, m_w_up[0], v_w_up[0], "adamw_w_up"),
        "w_down": _adamw(w_down[0], g_w_down, m_w_down[0], v_w_down[0], "adamw_w_down"),
    }
    big_grads = {"w_in": g_w_in, "w_out": g_w_out, "w_up": g_w_up, "w_down": g_w_down}
    order = ["w_in", "conv_w", "g_conv", "g_attn", "w_out", "ln1_g", "ln1_b", "w_up", "w_down", "ln2_g", "ln2_b"]
    small_parts = [unpack(small_g)] + [unpack(p) for p in small_out]

    def leaf(kind, name):
        if name in big_out:
            return (big_grads[name] if kind == 0 else big_out[name][kind - 1])[None]
        return small_parts[kind][name]

    outs = [loss, grad_x[None]]
    for kind in range(4):
        outs.extend(leaf(kind, name) for name in order)
    return tuple(outs)
```

```python
import functools

import jax
import jax.numpy as jnp
from jax import lax
from jax.experimental import pallas as pl
from jax.experimental.pallas import tpu as pltpu

F32 = jnp.float32
BF16 = jnp.bfloat16
SDS = jax.ShapeDtypeStruct

D_MODEL = 1024
CONV_WIDTH = 512
ATTN_WIDTH = 512
GROUP = 64
GATE_COLS = 3 * CONV_WIDTH
QKV_COLS = 3 * ATTN_WIDTH
IN_COLS = GATE_COLS + QKV_COLS
D_FF = 4 * D_MODEL
N_CHIPS = 4
IN_SHARD = IN_COLS // N_CHIPS
FF_SHARD = D_FF // N_CHIPS
ALPHA = float(2.0 ** 0.25)
LN_EPS = 1e-5
RMS_EPS = 1e-6
ATTN_SCALE = GROUP ** -0.5
LOG2_E = 1.4426950408889634
ADAM_LR = 0.001
ADAM_B1 = 0.9
ADAM_B2 = 0.999
ADAM_EPS = 1e-08
ADAM_WD = 0.01
ADAM_STEP = 10

LANES = 128
SUBLANES = 8
KEY_BLOCK = 128
ATTN_Q_TILE = 512
ATTN_KEY_BLOCKS = 4
VMEM_LIMIT = 56 * 1024 * 1024

MESH = pl.DeviceIdType.MESH
CHIP_FLIPS = ((1, 0), (0, 1), (1, 1))
DEVICE_FLIPS = tuple((fx, fy, fc) for fx in (0, 1) for fy in (0, 1) for fc in (0, 1))[1:]
NT_DIMS = (((1,), (1,)), ((), ()))
TN_DIMS = (((0,), (0,)), ((), ()))

ROW_LOSS = 0
ROW_GCONV = 8
ROW_GATTN = 12
ROW_LN1G = 16
ROW_LN1B = 24
ROW_LN2G = 32
ROW_LN2B = 40
ROW_CONVW = 48
SMALL_ROWS = 64
PARAM_ROWS = 48


def _params(sem=None):
    return pltpu.CompilerParams(dimension_semantics=sem, vmem_limit_bytes=VMEM_LIMIT)


def _flip(v, f):
    return 1 - v if f else v


def _position():
    return lax.axis_index("x"), lax.axis_index("y"), lax.axis_index("c")


def _hilo(v):
    hi = v.astype(BF16)
    lo = (v - hi.astype(F32)).astype(BF16)
    return jnp.concatenate([hi, lo], axis=1)


def _hilo_dot(v, mat):
    return jnp.dot(_hilo(v), mat, preferred_element_type=F32)


def _group_sum(v, gmat):
    parts = [_hilo_dot(v[:, LANES * j:LANES * (j + 1)], gmat) for j in range(v.shape[1] // LANES)]
    return parts[0] if len(parts) == 1 else jnp.concatenate(parts, axis=1)


def _softplus_terms(z):
    sp = jnp.log2(1.0 + jnp.exp2(-jnp.abs(z)))
    log_beta = jnp.minimum(z, 0.0) - sp
    return log_beta, log_beta - z


def _layer_norm_fwd(pre, g, b):
    mu = jnp.mean(pre, axis=-1, keepdims=True)
    d = pre - mu
    var = jnp.mean(d * d, axis=-1, keepdims=True)
    rstd = lax.rsqrt(var + LN_EPS)
    xhat = d * rstd
    return xhat * g + b, xhat, rstd


def _layer_norm_bwd(dy, xhat, rstd, g):
    dxh = dy * g
    m1 = jnp.mean(dxh, axis=-1, keepdims=True)
    m2 = jnp.mean(dxh * xhat, axis=-1, keepdims=True)
    return rstd * (dxh - m1 - xhat * m2)


def _row_tile(s, want):
    return min(s, want)


def _cast_into_slot(kc_idx, w, name):
    r, c = w.shape
    tr = _row_tile(r, 256)

    def body(kc_ref, w_ref, o_ref):
        o_ref[...] = w_ref[...].astype(BF16)

    grid_spec = pltpu.PrefetchScalarGridSpec(
        num_scalar_prefetch=1, grid=(r // tr,),
        in_specs=[pl.BlockSpec((tr, c), lambda i, kc: (i, 0))],
        out_specs=pl.BlockSpec((None, tr, c), lambda i, kc: (kc[0], i, 0)))
    return pl.pallas_call(
        body, name=name, grid_spec=grid_spec, out_shape=SDS((N_CHIPS, r, c), BF16),
        compiler_params=_params(("parallel",)))(kc_idx, w)


def _proj(x, w_in):
    s = x.shape[0]
    tm = _row_tile(s, 512)

    def body(x_ref, w_ref, gates_ref, qkv_ref):
        xb = x_ref[...].astype(BF16)
        for k in range(N_CHIPS):
            acc = jnp.dot(xb, w_ref[k], preferred_element_type=F32)
            if k < 2:
                gates_ref[:, IN_SHARD * k:IN_SHARD * (k + 1)] = acc
            else:
                qkv_ref[:, IN_SHARD * (k - 2):IN_SHARD * (k - 1)] = acc.astype(BF16)

    return pl.pallas_call(
        body, name="proj", grid=(s // tm,),
        in_specs=[pl.BlockSpec((tm, D_MODEL), lambda i: (i, 0)),
                  pl.BlockSpec((N_CHIPS, D_MODEL, IN_SHARD), lambda i: (0, 0, 0))],
        out_specs=[pl.BlockSpec((tm, GATE_COLS), lambda i: (i, 0)),
                   pl.BlockSpec((tm, QKV_COLS), lambda i: (i, 0))],
        out_shape=[SDS((s, GATE_COLS), F32), SDS((s, QKV_COLS), BF16)],
        compiler_params=_params(("parallel",)))(x, w_in)


def _conv_forward_values(g_ref, halo_ref, taps_ref, first_block):
    gates = g_ref[...]
    tr = gates.shape[0]
    bg = gates[:, :CONV_WIDTH]
    cg = gates[:, CONV_WIDTH:2 * CONV_WIDTH]
    h = gates[:, 2 * CONV_WIDTH:]
    u = cg * h

    def prev(r):
        v = halo_ref[r:r + 1, CONV_WIDTH:2 * CONV_WIDTH] * halo_ref[r:r + 1, 2 * CONV_WIDTH:GATE_COLS]
        return jnp.where(first_block, 0.0, v)

    row = lax.broadcasted_iota(jnp.int32, (tr, CONV_WIDTH), 0)
    u1 = jnp.where(row == 0, prev(7), pltpu.roll(u, 1, 0))
    u2 = jnp.where(row == 0, prev(6), jnp.where(row == 1, prev(7), pltpu.roll(u, 2, 0)))
    y = taps_ref[0:1, :] * u2 + taps_ref[1:2, :] * u1 + taps_ref[2:3, :] * u
    return bg, cg, h, u, u1, u2, y


def _conv_fwd(gates, taps, g_conv, gmat):
    s = gates.shape[0]
    tr = _row_tile(s, 512)
    hb = tr // SUBLANES

    def body(g_ref, halo_ref, taps_ref, gain_ref, gmat_ref, out_ref):
        i = pl.program_id(0)
        bg, _, _, _, _, _, y = _conv_forward_values(g_ref, halo_ref, taps_ref, i == 0)
        yc = bg * y
        ms = _group_sum(yc * yc, gmat_ref[...]) * (1.0 / GROUP)
        out_ref[...] = (yc * lax.rsqrt(ms + RMS_EPS) * gain_ref[...]).astype(BF16)

    return pl.pallas_call(
        body, name="conv_fwd", grid=(s // tr,),
        in_specs=[pl.BlockSpec((tr, GATE_COLS), lambda i: (i, 0)),
                  pl.BlockSpec((SUBLANES, GATE_COLS), lambda i: (jnp.maximum(i * hb - 1, 0), 0)),
                  pl.BlockSpec((SUBLANES, CONV_WIDTH), lambda i: (0, 0)),
                  pl.BlockSpec((1, CONV_WIDTH), lambda i: (0, 0)),
                  pl.BlockSpec((2 * LANES, LANES), lambda i: (0, 0))],
        out_specs=pl.BlockSpec((tr, CONV_WIDTH), lambda i: (i, 0)),
        out_shape=SDS((s, CONV_WIDTH), BF16),
        compiler_params=_params(("parallel",)))(gates, gates, taps, g_conv, gmat)


def _conv_bwd_gate(gates, dycn, taps, g_conv, gmat):
    s = gates.shape[0]
    tr = _row_tile(s, 512)
    hb = tr // SUBLANES

    def body(g_ref, halo_ref, dn_ref, taps_ref, gain_ref, gmat_ref, dbg_ref, dy_ref, sums_ref):
        i = pl.program_id(0)
        bg, _, _, u, u1, u2, y = _conv_forward_values(g_ref, halo_ref, taps_ref, i == 0)
        gmat_v = gmat_ref[...]
        yc = bg * y
        rstd = lax.rsqrt(_group_sum(yc * yc, gmat_v) * (1.0 / GROUP) + RMS_EPS)
        n = yc * rstd
        dout = dn_ref[...]
        dn = dout * gain_ref[...]
        dyc = rstd * (dn - n * (_group_sum(dn * n, gmat_v) * (1.0 / GROUP)))
        dbg_ref[...] = (dyc * y).astype(BF16)
        dy = dyc * bg
        dy_ref[...] = dy

        @pl.when(i == 0)
        def _():
            sums_ref[...] = jnp.zeros_like(sums_ref)

        sums_ref[0:1, :] += jnp.sum(dy * u2, axis=0, keepdims=True)
        sums_ref[1:2, :] += jnp.sum(dy * u1, axis=0, keepdims=True)
        sums_ref[2:3, :] += jnp.sum(dy * u, axis=0, keepdims=True)
        sums_ref[3:4, :] += jnp.sum(dout * n, axis=0, keepdims=True)

    return pl.pallas_call(
        body, name="conv_bwd_gate", grid=(s // tr,),
        in_specs=[pl.BlockSpec((tr, GATE_COLS), lambda i: (i, 0)),
                  pl.BlockSpec((SUBLANES, GATE_COLS), lambda i: (jnp.maximum(i * hb - 1, 0), 0)),
                  pl.BlockSpec((tr, CONV_WIDTH), lambda i: (i, 0)),
                  pl.BlockSpec((SUBLANES, CONV_WIDTH), lambda i: (0, 0)),
                  pl.BlockSpec((1, CONV_WIDTH), lambda i: (0, 0)),
                  pl.BlockSpec((2 * LANES, LANES), lambda i: (0, 0))],
        out_specs=[pl.BlockSpec((tr, CONV_WIDTH), lambda i: (i, 0)),
                   pl.BlockSpec((tr, CONV_WIDTH), lambda i: (i, 0)),
                   pl.BlockSpec((SUBLANES, CONV_WIDTH), lambda i: (0, 0))],
        out_shape=[SDS((s, CONV_WIDTH), BF16), SDS((s, CONV_WIDTH), F32), SDS((SUBLANES, CONV_WIDTH), F32)],
        compiler_params=_params(("arbitrary",)))(gates, gates, dycn, taps, g_conv, gmat)


def _dproj_assemble(gates, dy, dbg, dq, dk, dv, taps):
    s = gates.shape[0]
    tr = _row_tile(s, 512)
    hb = tr // SUBLANES
    last = s // SUBLANES - 1
    n_blocks = s // tr

    def body(g_ref, dy_ref, halo_ref, dbg_ref, dq_ref, dk_ref, dv_ref, taps_ref, out_ref):
        i = pl.program_id(0)
        gates_v = g_ref[...]
        cg = gates_v[:, CONV_WIDTH:2 * CONV_WIDTH]
        h = gates_v[:, 2 * CONV_WIDTH:]
        dy_v = dy_ref[...]
        last_block = i == n_blocks - 1
        nxt = lambda r: jnp.where(last_block, 0.0, halo_ref[r:r + 1, :])
        row = lax.broadcasted_iota(jnp.int32, (tr, CONV_WIDTH), 0)
        d1 = jnp.where(row == tr - 1, nxt(0), pltpu.roll(dy_v, tr - 1, 0))
        d2 = jnp.where(row == tr - 1, nxt(1), jnp.where(row == tr - 2, nxt(0), pltpu.roll(dy_v, tr - 2, 0)))
        du = taps_ref[2:3, :] * dy_v + taps_ref[1:2, :] * d1 + taps_ref[0:1, :] * d2
        out_ref[:, 0:CONV_WIDTH] = dbg_ref[...]
        out_ref[:, CONV_WIDTH:2 * CONV_WIDTH] = (du * h).astype(BF16)
        out_ref[:, 2 * CONV_WIDTH:GATE_COLS] = (du * cg).astype(BF16)
        out_ref[:, GATE_COLS:GATE_COLS + ATTN_WIDTH] = dq_ref[...]
        out_ref[:, GATE_COLS + ATTN_WIDTH:GATE_COLS + 2 * ATTN_WIDTH] = dk_ref[...].astype(BF16)
        out_ref[:, GATE_COLS + 2 * ATTN_WIDTH:] = dv_ref[...].astype(BF16)

    row_spec = lambda w: pl.BlockSpec((tr, w), lambda i: (i, 0))
    return pl.pallas_call(
        body, name="dproj_assemble", grid=(s // tr,),
        in_specs=[row_spec(GATE_COLS), row_spec(CONV_WIDTH),
                  pl.BlockSpec((SUBLANES, CONV_WIDTH), lambda i: (jnp.minimum((i + 1) * hb, last), 0)),
                  row_spec(CONV_WIDTH), row_spec(ATTN_WIDTH), row_spec(ATTN_WIDTH), row_spec(ATTN_WIDTH),
                  pl.BlockSpec((SUBLANES, CONV_WIDTH), lambda i: (0, 0))],
        out_specs=row_spec(IN_COLS),
        out_shape=SDS((s, IN_COLS), BF16),
        compiler_params=_params(("parallel",)))(gates, dy, dy, dbg, dq, dk, dv, taps)


def _stack_heads(rows, nb):
    lane = lax.broadcasted_iota(jnp.int32, (1, LANES), 1)
    zero = jnp.zeros((KEY_BLOCK, LANES), rows.dtype)
    parts = []
    for blk in range(nb):
        r = rows[blk * KEY_BLOCK:(blk + 1) * KEY_BLOCK]
        parts.append(jnp.where(lane < GROUP, r, zero))
        parts.append(jnp.where(lane < GROUP, zero, r))
    return jnp.concatenate(parts, axis=0)


def _stack_hilo(v, n_cols):
    return jnp.concatenate([_hilo(v[:, c * KEY_BLOCK:(c + 1) * KEY_BLOCK]) for c in range(n_cols)], axis=0)


def _causal_mask(tq, nb, diag_base):
    shape = (tq, 2 * nb * KEY_BLOCK)
    row = lax.broadcasted_iota(jnp.int32, shape, 0)
    col = lax.broadcasted_iota(jnp.int32, shape, 1)
    key = diag_base + (col // (2 * KEY_BLOCK)) * KEY_BLOCK + col % KEY_BLOCK
    return key < row


ANY = pl.BlockSpec(memory_space=pl.ANY)


def _remote_copy(src, dst, sems, idx, target):
    return pltpu.make_async_remote_copy(src_ref=src, dst_ref=dst, send_sem=sems[0].at[idx], recv_sem=sems[1].at[idx],
                                        device_id=target, device_id_type=MESH)


def _gather_copies(bufs, sems):
    x, y, c = _position()
    sends, arrivals = [], []
    for a in range(len(bufs)):
        mine = bufs[a].at[2 * x + y]
        for j, (fx, fy) in enumerate(CHIP_FLIPS):
            tx, ty = _flip(x, fx), _flip(y, fy)
            there = bufs[a].at[2 * tx + ty]
            sends.append(_remote_copy(mine, mine, sems, 3 * a + j, (tx, ty, c)))
            arrivals.append(_remote_copy(there, there, sems, 3 * a + j, (tx, ty, c)))
    return sends, arrivals


def _reduce_copies(ins, outs, sems):
    x, y, c = _position()
    sends, arrivals = [], []
    for a in range(len(ins)):
        h = ins[a].shape[1] // 2
        for f, (fx, fy, fc) in enumerate(DEVICE_FLIPS):
            tx, ty, tc = _flip(x, fx), _flip(y, fy), _flip(c, fc)
            src = ins[a].at[2 * tx + ty, pl.ds(pl.multiple_of(tc * h, h), h)]
            sends.append(_remote_copy(src, outs[a].at[f], sems, 7 * a + f, (tx, ty, tc)))
            arrivals.append(_remote_copy(outs[a].at[f], outs[a].at[f], sems, 7 * a + f, (tx, ty, tc)))
    return sends, arrivals


def _start_copies(make):
    sends, _ = make()
    for cp in sends:
        cp.start()


def _finish_copies(make):
    sends, arrivals = make()
    for cp in arrivals:
        cp.wait_recv()
    for cp in sends:
        cp.wait_send()


def _attn_fwd(qkv, g_attn, tri, gmat, shards):
    n_w = len(shards)
    s = qkv.shape[0]
    tq = _row_tile(s, ATTN_Q_TILE)
    tk = KEY_BLOCK
    nb = ATTN_KEY_BLOCKS
    width = nb * tk
    diag_trips = tq // width
    pairs = ATTN_WIDTH // LANES

    def body(q_ref, k_ref, v_ref, gain_ref, tri_ref, gmat_ref, *rest):
        o_ref, yn_ref, tot_ref = rest[n_w:n_w + 3]
        w_bufs, sems = rest[n_w + 3:2 * n_w + 3], rest[2 * n_w + 3:]
        copies = functools.partial(_gather_copies, w_bufs, sems)
        p, i = pl.program_id(0), pl.program_id(1)
        pl.when((p == 0) & (i == 0))(functools.partial(_start_copies, copies))
        q2 = q_ref[...]
        tri_v = tri_ref[...]

        def trip(t, carry, diag_base):
            run = [carry[0], carry[1]]
            oacc = carry[2]
            s0 = pl.multiple_of(t * width, width)
            ksel = _stack_heads(k_ref[pl.ds(s0, width), :], nb)
            vsel = _stack_heads(v_ref[pl.ds(s0, width), :], nb)
            z = lax.dot_general(q2, ksel, NT_DIMS, preferred_element_type=F32) * (ATTN_SCALE * LOG2_E)
            log_beta, log_keep = _softplus_terms(z)
            if diag_base is not None:
                valid = _causal_mask(tq, nb, diag_base)
                log_keep = jnp.where(valid, log_keep, 0.0)
            ct = jnp.dot(_stack_hilo(log_keep, 2 * nb), tri_v, preferred_element_type=F32)
            a_parts = [None] * (2 * nb)
            for c in reversed(range(2 * nb)):
                h = c % 2
                ct_c = ct[c * tq:(c + 1) * tq]
                a_parts[c] = jnp.exp2(log_beta[:, c * tk:(c + 1) * tk] + ct_c[:, :tk] + run[h])
                run[h] = run[h] + ct_c[:, tk:]
            a = jnp.concatenate(a_parts, axis=1)
            if diag_base is not None:
                a = jnp.where(valid, a, 0.0)
            oacc = oacc + jnp.dot(a.astype(BF16), vsel, preferred_element_type=F32)
            return run[0], run[1], oacc

        carry = (jnp.zeros((tq, tk), F32), jnp.zeros((tq, tk), F32), jnp.zeros((tq, LANES), F32))
        for d in reversed(range(diag_trips)):
            carry = trip(i * diag_trips + d, carry, d * width)
        n_full = i * diag_trips
        carry = lax.fori_loop(0, n_full, lambda it, c: trip(n_full - 1 - it, c, None), carry)
        run_a, run_b, oacc = carry
        lane = lax.broadcasted_iota(jnp.int32, (1, LANES), 1)
        o_ref[...] = oacc
        tot_ref[...] = jnp.where(lane < GROUP, run_a, run_b)
        ms = _group_sum(oacc * oacc, gmat_ref[...]) * (1.0 / GROUP)
        yn_ref[...] = (oacc * lax.rsqrt(ms + RMS_EPS) * gain_ref[...]).astype(BF16)
        pl.when((p == pairs - 1) & (i == pl.num_programs(1) - 1))(functools.partial(_finish_copies, copies))

    blk = lambda: pl.BlockSpec((tq, LANES), lambda p, i: (i, p))
    return pl.pallas_call(
        body, name="attn_fwd", grid=(pairs, s // tq),
        in_specs=[pl.BlockSpec((tq, LANES), lambda p, i: (i, p)),
                  pl.BlockSpec((s, LANES), lambda p, i: (0, pairs + p)),
                  pl.BlockSpec((s, LANES), lambda p, i: (0, 2 * pairs + p)),
                  pl.BlockSpec((1, LANES), lambda p, i: (0, p)),
                  pl.BlockSpec((2 * tk, 2 * tk), lambda p, i: (0, 0)),
                  pl.BlockSpec((2 * LANES, LANES), lambda p, i: (0, 0))] + [ANY] * n_w,
        out_specs=[blk(), blk(), blk()] + [ANY] * n_w,
        out_shape=[SDS((s, ATTN_WIDTH), F32), SDS((s, ATTN_WIDTH), BF16), SDS((s, ATTN_WIDTH), F32)]
        + [SDS(w.shape, w.dtype) for w in shards],
        input_output_aliases={6 + a: 3 + a for a in range(n_w)},
        scratch_shapes=[pltpu.SemaphoreType.DMA((3 * n_w,)), pltpu.SemaphoreType.DMA((3 * n_w,))],
        compiler_params=_params(("arbitrary", "arbitrary")))(qkv, qkv, qkv, g_attn, tri, gmat, *shards)


def _attn_bwd(qkv, o, tot, dyn, g_attn, tri, gmat, partials):
    n_g = len(partials)
    s = qkv.shape[0]
    tq = _row_tile(s, ATTN_Q_TILE)
    tk = KEY_BLOCK
    nb = ATTN_KEY_BLOCKS
    width = nb * tk
    diag_trips = tq // width
    pairs = ATTN_WIDTH // LANES

    def body(q_ref, k_ref, v_ref, o_ref, tot_ref, dyn_ref, gain_ref, tri_ref, gmat_ref, *rest):
        g_ins, (dq_ref, dk_ref, dv_ref, dg_ref) = rest[:n_g], rest[n_g:n_g + 4]
        g_outs, sems = rest[n_g + 4:2 * n_g + 4], rest[2 * n_g + 4:]
        copies = functools.partial(_reduce_copies, g_ins, g_outs, sems)
        p, i = pl.program_id(0), pl.program_id(1)
        pl.when((p == 0) & (i == 0))(functools.partial(_start_copies, copies))

        @pl.when(i == 0)
        def _():
            dk_ref[...] = jnp.zeros_like(dk_ref)
            dv_ref[...] = jnp.zeros_like(dv_ref)
            dg_ref[...] = jnp.zeros_like(dg_ref)

        gmat_v = gmat_ref[...]
        o_v = o_ref[...]
        rstd = lax.rsqrt(_group_sum(o_v * o_v, gmat_v) * (1.0 / GROUP) + RMS_EPS)
        n = o_v * rstd
        dout = dyn_ref[...]
        dg_ref[0:1, :] += jnp.sum(dout * n, axis=0, keepdims=True)
        dn = dout * gain_ref[...]
        do2 = (rstd * (dn - n * (_group_sum(dn * n, gmat_v) * (1.0 / GROUP)))).astype(BF16)
        q2 = q_ref[...]
        tot_v = tot_ref[...]
        tots = (jnp.broadcast_to(tot_v[:, 0:1], (tq, tk)), jnp.broadcast_to(tot_v[:, GROUP:GROUP + 1], (tq, tk)))
        tri_v, tri_incl_v = tri_ref[0], tri_ref[1]
        lane = lax.broadcasted_iota(jnp.int32, (1, LANES), 1)

        def trip(t, carry, diag_base):
            rest_l = [carry[0], carry[1]]
            pref_g = [carry[2], carry[3]]
            dq = carry[4]
            s0 = pl.multiple_of(t * width, width)
            ksel = _stack_heads(k_ref[pl.ds(s0, width), :], nb)
            vsel = _stack_heads(v_ref[pl.ds(s0, width), :], nb)
            z = lax.dot_general(q2, ksel, NT_DIMS, preferred_element_type=F32) * (ATTN_SCALE * LOG2_E)
            log_beta, log_keep = _softplus_terms(z)
            if diag_base is not None:
                valid = _causal_mask(tq, nb, diag_base)
                log_keep = jnp.where(valid, log_keep, 0.0)
            ctl = jnp.dot(_stack_hilo(log_keep, 2 * nb), tri_incl_v, preferred_element_type=F32)
            da = lax.dot_general(do2, vsel, NT_DIMS, preferred_element_type=F32)
            a_parts = []
            for c in range(2 * nb):
                h = c % 2
                ct_c = ctl[c * tq:(c + 1) * tq]
                cols = slice(c * tk, (c + 1) * tk)
                a_parts.append(jnp.exp2(log_beta[:, cols] + (rest_l[h] - ct_c[:, :tk])))
                rest_l[h] = rest_l[h] - ct_c[:, tk:]
            a = jnp.concatenate(a_parts, axis=1)
            if diag_base is not None:
                a = jnp.where(valid, a, 0.0)
            g = a * da
            ctg = jnp.dot(_stack_hilo(g, 2 * nb), tri_v, preferred_element_type=F32)
            dz_parts = []
            for c in range(2 * nb):
                h = c % 2
                ct_c = ctg[c * tq:(c + 1) * tq]
                cols = slice(c * tk, (c + 1) * tk)
                prefix = pref_g[h] + ct_c[:, :tk]
                pref_g[h] = pref_g[h] + ct_c[:, tk:]
                g_c = g[:, cols]
                dz_parts.append(g_c - jnp.exp2(log_beta[:, cols]) * (g_c + prefix))
            dz = jnp.concatenate(dz_parts, axis=1) * ATTN_SCALE
            if diag_base is not None:
                dz = jnp.where(valid, dz, 0.0)
            dzb = dz.astype(BF16)
            dq = dq + jnp.dot(dzb, ksel, preferred_element_type=F32)
            dkt = lax.dot_general(dzb, q2, TN_DIMS, preferred_element_type=F32)
            dvt = lax.dot_general(a.astype(BF16), do2, TN_DIMS, preferred_element_type=F32)
            for blk in range(nb):
                ra, rb = slice(2 * blk * tk, (2 * blk + 1) * tk), slice((2 * blk + 1) * tk, (2 * blk + 2) * tk)
                rows = pl.ds(pl.multiple_of(s0 + blk * tk, tk), tk)
                dk_ref[rows, :] += jnp.where(lane < GROUP, dkt[ra], dkt[rb])
                dv_ref[rows, :] += jnp.where(lane < GROUP, dvt[ra], dvt[rb])
            return rest_l[0], rest_l[1], pref_g[0], pref_g[1], dq

        zeros_qk = jnp.zeros((tq, tk), F32)
        carry = (tots[0], tots[1], zeros_qk, zeros_qk, jnp.zeros((tq, LANES), F32))
        carry = lax.fori_loop(0, i * diag_trips, lambda t, c: trip(t, c, None), carry)
        for d in range(diag_trips):
            carry = trip(i * diag_trips + d, carry, d * width)
        dq_ref[...] = carry[4].astype(BF16)
        pl.when((p == pairs - 1) & (i == pl.num_programs(1) - 1))(functools.partial(_finish_copies, copies))

    blk = lambda: pl.BlockSpec((tq, LANES), lambda p, i: (i, p))
    col = lambda: pl.BlockSpec((s, LANES), lambda p, i: (0, p))
    n_peers = len(DEVICE_FLIPS)
    return pl.pallas_call(
        body, name="attn_bwd", grid=(pairs, s // tq),
        in_specs=[pl.BlockSpec((tq, LANES), lambda p, i: (i, p)),
                  pl.BlockSpec((s, LANES), lambda p, i: (0, pairs + p)),
                  pl.BlockSpec((s, LANES), lambda p, i: (0, 2 * pairs + p)),
                  blk(), blk(), blk(),
                  pl.BlockSpec((1, LANES), lambda p, i: (0, p)),
                  pl.BlockSpec((2, 2 * tk, 2 * tk), lambda p, i: (0, 0, 0)),
                  pl.BlockSpec((2 * LANES, LANES), lambda p, i: (0, 0))] + [ANY] * n_g,
        out_specs=[blk(), col(), col(), pl.BlockSpec((SUBLANES, LANES), lambda p, i: (0, p))] + [ANY] * n_g,
        out_shape=[SDS((s, ATTN_WIDTH), BF16), SDS((s, ATTN_WIDTH), F32), SDS((s, ATTN_WIDTH), F32),
                   SDS((SUBLANES, ATTN_WIDTH), F32)]
        + [SDS((n_peers, g.shape[1] // 2, g.shape[2]), g.dtype) for g in partials],
        scratch_shapes=[pltpu.SemaphoreType.DMA((n_peers * n_g,)), pltpu.SemaphoreType.DMA((n_peers * n_g,))],
        compiler_params=_params(("arbitrary", "arbitrary")))(qkv, qkv, qkv, o, tot, dyn, g_attn, tri, gmat, *partials)


def _mix_ln1(ycn, yan, w_out, x, g, b):
    s = x.shape[0]
    tm = _row_tile(s, 512)

    def body(yc_ref, ya_ref, w_ref, x_ref, g_ref, b_ref, x1_ref, xhat_ref, rstd_ref):
        mix = jnp.dot(yc_ref[...], w_ref[0:CONV_WIDTH, :], preferred_element_type=F32)
        mix = mix + jnp.dot(ya_ref[...], w_ref[CONV_WIDTH:, :], preferred_element_type=F32)
        x1, xhat, rstd = _layer_norm_fwd(ALPHA * x_ref[...] + mix, g_ref[...], b_ref[...])
        x1_ref[...] = x1
        xhat_ref[...] = xhat
        rstd_ref[...] = rstd

    row = lambda w: pl.BlockSpec((tm, w), lambda i: (i, 0))
    vec = lambda: pl.BlockSpec((1, D_MODEL), lambda i: (0, 0))
    return pl.pallas_call(
        body, name="mix_ln1", grid=(s // tm,),
        in_specs=[row(CONV_WIDTH), row(ATTN_WIDTH), pl.BlockSpec((D_MODEL, D_MODEL), lambda i: (0, 0)),
                  row(D_MODEL), vec(), vec()],
        out_specs=[row(D_MODEL), row(D_MODEL), row(1)],
        out_shape=[SDS((s, D_MODEL), F32), SDS((s, D_MODEL), F32), SDS((s, 1), F32)],
        compiler_params=_params(("parallel",)))(ycn, yan, w_out, x, g, b)


def _mlp_fwd_loss(x1, w_up, w_down, target, g, b):
    s = x1.shape[0]
    tm = _row_tile(s, 256)

    def body(x1_ref, wu_ref, wd_ref, t_ref, g_ref, b_ref, dpre_ref, sums_ref, loss_ref):
        i = pl.program_id(0)
        x1_v = x1_ref[...]
        xb = x1_v.astype(BF16)
        ffn = jnp.zeros((tm, D_MODEL), F32)
        for k in range(N_CHIPS):
            r = jnp.maximum(jnp.dot(xb, wu_ref[k], preferred_element_type=F32), 0.0)
            ffn = ffn + jnp.dot((r * r).astype(BF16), wd_ref[k], preferred_element_type=F32)
        g_v = g_ref[...]
        x2, xhat, rstd = _layer_norm_fwd(ALPHA * x1_v + ffn, g_v, b_ref[...])
        err = x2 - t_ref[...]
        dx2 = err * (1.0 / D_MODEL)
        dpre_ref[...] = _layer_norm_bwd(dx2, xhat, rstd, g_v)

        @pl.when(i == 0)
        def _():
            sums_ref[...] = jnp.zeros_like(sums_ref)
            loss_ref[...] = jnp.zeros_like(loss_ref)

        sums_ref[0:1, :] += jnp.sum(dx2 * xhat, axis=0, keepdims=True)
        sums_ref[1:2, :] += jnp.sum(dx2, axis=0, keepdims=True)
        loss_ref[...] += jnp.sum(jnp.sum(err * err, axis=1, keepdims=True), axis=0, keepdims=True) * (0.5 / D_MODEL)

    row = lambda: pl.BlockSpec((tm, D_MODEL), lambda i: (i, 0))
    vec = lambda: pl.BlockSpec((1, D_MODEL), lambda i: (0, 0))
    wspec = lambda: pl.BlockSpec((N_CHIPS, D_MODEL, FF_SHARD), lambda i: (0, 0, 0))
    return pl.pallas_call(
        body, name="mlp_fwd_loss", grid=(s // tm,),
        in_specs=[row(), wspec(), wspec(), row(), vec(), vec()],
        out_specs=[row(), pl.BlockSpec((SUBLANES, D_MODEL), lambda i: (0, 0)),
                   pl.BlockSpec((SUBLANES, LANES), lambda i: (0, 0))],
        out_shape=[SDS((s, D_MODEL), F32), SDS((SUBLANES, D_MODEL), F32), SDS((SUBLANES, LANES), F32)],
        compiler_params=_params(("arbitrary",)))(x1, w_up, w_down, target, g, b)


def _mlp_bwd_ln1(x1, dpre2, w_up, w_down, xhat1, rstd1, g1):
    s = x1.shape[0]
    tm = _row_tile(s, 256)

    def body(x1_ref, d2_ref, wu_ref, wd_ref, xh_ref, rs_ref, g_ref, hid_ref, dup_ref, dpre_ref, sums_ref):
        i = pl.program_id(0)
        xb = x1_ref[...].astype(BF16)
        d2 = d2_ref[...]
        d2b = d2.astype(BF16)
        dx1 = ALPHA * d2
        for k in range(N_CHIPS):
            r = jnp.maximum(jnp.dot(xb, wu_ref[k], preferred_element_type=F32), 0.0)
            hid_ref[:, FF_SHARD * k:FF_SHARD * (k + 1)] = (r * r).astype(BF16)
            dhid = lax.dot_general(d2b, wd_ref[k], NT_DIMS, preferred_element_type=F32)
            dupb = (dhid * (2.0 * r)).astype(BF16)
            dup_ref[:, FF_SHARD * k:FF_SHARD * (k + 1)] = dupb
            dx1 = dx1 + lax.dot_general(dupb, wu_ref[k], NT_DIMS, preferred_element_type=F32)
        xhat = xh_ref[...]
        dpre_ref[...] = _layer_norm_bwd(dx1, xhat, rs_ref[...], g_ref[...])

        @pl.when(i == 0)
        def _():
            sums_ref[...] = jnp.zeros_like(sums_ref)

        sums_ref[0:1, :] += jnp.sum(dx1 * xhat, axis=0, keepdims=True)
        sums_ref[1:2, :] += jnp.sum(dx1, axis=0, keepdims=True)

    row = lambda w: pl.BlockSpec((tm, w), lambda i: (i, 0))
    wspec = lambda: pl.BlockSpec((N_CHIPS, D_MODEL, FF_SHARD), lambda i: (0, 0, 0))
    return pl.pallas_call(
        body, name="mlp_bwd_ln1", grid=(s // tm,),
        in_specs=[row(D_MODEL), row(D_MODEL), wspec(), wspec(), row(D_MODEL), row(1),
                  pl.BlockSpec((1, D_MODEL), lambda i: (0, 0))],
        out_specs=[row(D_FF), row(D_FF), row(D_MODEL), pl.BlockSpec((SUBLANES, D_MODEL), lambda i: (0, 0))],
        out_shape=[SDS((s, D_FF), BF16), SDS((s, D_FF), BF16), SDS((s, D_MODEL), F32),
                   SDS((SUBLANES, D_MODEL), F32)],
        compiler_params=_params(("arbitrary",)))(x1, dpre2, w_up, w_down, xhat1, rstd1, g1)


def _grad_tn(a, b, name, out_cols, stacked):
    s, ka = a.shape
    n = b.shape[1]
    ts = _row_tile(s, 2048)
    n_steps = s // ts
    if stacked:
        tka, tn = ka, out_cols
        grid = (1, n // tn, n_steps)
        shape = (n // tn, ka, tn)
        out_spec = lambda: pl.BlockSpec((None, tka, tn), lambda r, c, t: (c, 0, 0))
    else:
        tka, tn = min(ka, 1024), n
        grid = (ka // tka, 1, n_steps)
        shape = (ka, n)
        out_spec = lambda: pl.BlockSpec((tka, tn), lambda r, c, t: (r, 0))

    def body(a_ref, b_ref, o_ref, ob_ref):
        t = pl.program_id(2)

        @pl.when(t == 0)
        def _():
            o_ref[...] = jnp.zeros_like(o_ref)

        o_ref[...] += lax.dot_general(a_ref[...].astype(BF16), b_ref[...].astype(BF16), TN_DIMS,
                                      preferred_element_type=F32)

        @pl.when(t == n_steps - 1)
        def _():
            ob_ref[...] = o_ref[...].astype(BF16)

    return pl.pallas_call(
        body, name=name, grid=grid,
        in_specs=[pl.BlockSpec((ts, tka), lambda r, c, t: (t, r)),
                  pl.BlockSpec((ts, tn), lambda r, c, t: (t, c))],
        out_specs=[out_spec(), out_spec()], out_shape=[SDS(shape, F32), SDS(shape, BF16)],
        compiler_params=_params(("parallel", "parallel", "arbitrary")))(a, b)


def _dmix(dpre1, w_out):
    s = dpre1.shape[0]
    tm = _row_tile(s, 512)

    def body(d_ref, w_ref, dc_ref, da_ref):
        db = d_ref[...].astype(BF16)
        dc_ref[...] = lax.dot_general(db, w_ref[0:CONV_WIDTH, :], NT_DIMS, preferred_element_type=F32)
        da_ref[...] = lax.dot_general(db, w_ref[CONV_WIDTH:, :], NT_DIMS, preferred_element_type=F32)

    return pl.pallas_call(
        body, name="dmix", grid=(s // tm,),
        in_specs=[pl.BlockSpec((tm, D_MODEL), lambda i: (i, 0)),
                  pl.BlockSpec((D_MODEL, D_MODEL), lambda i: (0, 0))],
        out_specs=[pl.BlockSpec((tm, CONV_WIDTH), lambda i: (i, 0)),
                   pl.BlockSpec((tm, ATTN_WIDTH), lambda i: (i, 0))],
        out_shape=[SDS((s, CONV_WIDTH), F32), SDS((s, ATTN_WIDTH), F32)],
        compiler_params=_params(("parallel",)))(dpre1, w_out)


def _grad_x(dproj, w_in, dpre1, partial):
    s = dproj.shape[0]
    tm = _row_tile(s, 512)
    n_peers = len(DEVICE_FLIPS)

    def body(dp_ref, w_ref, d1_ref, g_in, o_ref, g_out, send_sems, recv_sems):
        copies = functools.partial(_reduce_copies, [g_in], [g_out], (send_sems, recv_sems))
        i = pl.program_id(0)
        pl.when(i == 0)(functools.partial(_start_copies, copies))
        acc = ALPHA * d1_ref[...]
        for k in range(N_CHIPS):
            acc = acc + lax.dot_general(dp_ref[:, IN_SHARD * k:IN_SHARD * (k + 1)], w_ref[k], NT_DIMS,
                                        preferred_element_type=F32)
        o_ref[...] = acc
        pl.when(i == pl.num_programs(0) - 1)(functools.partial(_finish_copies, copies))

    return pl.pallas_call(
        body, name="grad_x", grid=(s // tm,),
        in_specs=[pl.BlockSpec((tm, IN_COLS), lambda i: (i, 0)),
                  pl.BlockSpec((N_CHIPS, D_MODEL, IN_SHARD), lambda i: (0, 0, 0)),
                  pl.BlockSpec((tm, D_MODEL), lambda i: (i, 0)), ANY],
        out_specs=[pl.BlockSpec((tm, D_MODEL), lambda i: (i, 0)), ANY],
        out_shape=[SDS((s, D_MODEL), F32), SDS((n_peers, partial.shape[1] // 2, partial.shape[2]), partial.dtype)],
        scratch_shapes=[pltpu.SemaphoreType.DMA((n_peers,)), pltpu.SemaphoreType.DMA((n_peers,))],
        compiler_params=_params(("arbitrary",)))(dproj, w_in, dpre1, partial)


def _adamw(w, g, m, v, name):
    r, c = w.shape
    tr = _row_tile(r, 256)

    def body(w_ref, g_ref, m_ref, v_ref, d_ref, nm_ref, nv_ref):
        g_v = g_ref[...]
        nm = ADAM_B1 * m_ref[...] + (1.0 - ADAM_B1) * g_v
        nv = ADAM_B2 * v_ref[...] + (1.0 - ADAM_B2) * (g_v * g_v)
        m_hat = nm / (1.0 - ADAM_B1 ** ADAM_STEP)
        v_hat = nv / (1.0 - ADAM_B2 ** ADAM_STEP)
        d_ref[...] = -ADAM_LR * (m_hat / (jnp.sqrt(v_hat) + ADAM_EPS) + ADAM_WD * w_ref[...])
        nm_ref[...] = nm
        nv_ref[...] = nv

    spec = lambda: pl.BlockSpec((tr, c), lambda i: (i, 0))
    return pl.pallas_call(
        body, name=name, grid=(r // tr,),
        in_specs=[spec(), spec(), spec(), spec()], out_specs=[spec(), spec(), spec()],
        out_shape=[SDS((r, c), F32)] * 3, compiler_params=_params(("parallel",)))(w, g, m, v)


def _sum_partials(kc_idx, grad, recv, name):
    _, _, h, cols = grad.shape
    th = _row_tile(h, 128)
    n_peers = recv.shape[0]

    def body(kc_ref, own_ref, r_ref, o_ref):
        acc = own_ref[...]
        for f in range(n_peers):
            acc = acc + r_ref[f].astype(F32)
        o_ref[...] = acc

    grid_spec = pltpu.PrefetchScalarGridSpec(
        num_scalar_prefetch=1, grid=(h // th,),
        in_specs=[pl.BlockSpec((None, None, th, cols), lambda t, kc: (kc[0], kc[1], t, 0)),
                  pl.BlockSpec((n_peers, th, cols), lambda t, kc: (0, t, 0))],
        out_specs=pl.BlockSpec((th, cols), lambda t, kc: (kc[1] * (h // th) + t, 0)))
    return pl.pallas_call(
        body, name=name, grid_spec=grid_spec, out_shape=SDS((2 * h, cols), F32),
        compiler_params=_params(("parallel",)))(kc_idx, grad, recv)


def _gather_weights(shards, conv_shard):
    n = len(shards)

    def body(*refs):
        outs, conv_out = refs[n + 1:2 * n + 1], refs[2 * n + 1]
        send_sems, recv_sems = refs[2 * n + 2:]
        x, y, c = _position()
        k = 2 * x + y
        sibling = (x, y, 1 - c)
        chips = [(_flip(x, fx), _flip(y, fy)) for fx, fy in CHIP_FLIPS]

        def half(a, rows_of_core):
            h = shards[a].shape[1] // 2
            return pl.ds(pl.multiple_of(rows_of_core * h, h), h)

        def remote(src, dst, idx, target):
            return pltpu.make_async_remote_copy(src_ref=src, dst_ref=dst, send_sem=send_sems.at[idx],
                                                recv_sem=recv_sems.at[idx], device_id=target,
                                                device_id_type=MESH)

        started = []
        for a in range(n):
            mine = outs[a].at[k, half(a, c)]
            for j, (tx, ty) in enumerate(chips):
                cp = remote(mine, mine, 6 * a + j, (tx, ty, c))
                cp.start()
                started.append(cp)
        for j, (tx, ty) in enumerate(chips):
            cp = remote(conv_out.at[k], conv_out.at[k], 6 * n + j, (tx, ty, c))
            cp.start()
            started.append(cp)
        for a in range(n):
            for j, (tx, ty) in enumerate(chips):
                kj = 2 * tx + ty
                landed = outs[a].at[kj, half(a, c)]
                remote(landed, landed, 6 * a + j, sibling).wait_recv()
                cp = remote(landed, landed, 6 * a + 3 + j, sibling)
                cp.start()
                started.append(cp)
        for a in range(n):
            for j, (tx, ty) in enumerate(chips):
                kj = 2 * tx + ty
                other = outs[a].at[kj, half(a, 1 - c)]
                remote(other, other, 6 * a + 3 + j, sibling).wait_recv()
        for j, (tx, ty) in enumerate(chips):
            kj = 2 * tx + ty
            remote(conv_out.at[kj], conv_out.at[kj], 6 * n + j, sibling).wait_recv()
        for cp in started:
            cp.wait_send()

    out_shape = [SDS(w.shape, w.dtype) for w in shards] + [SDS(conv_shard.shape, conv_shard.dtype)]
    n_sems = 6 * n + 3
    return pl.pallas_call(
        body, name="gather_weights", in_specs=[ANY] * (n + 1), out_specs=[ANY] * (n + 1),
        out_shape=out_shape, input_output_aliases={a: a for a in range(n + 1)},
        scratch_shapes=[pltpu.SemaphoreType.DMA((n_sems,)), pltpu.SemaphoreType.DMA((n_sems,))])(
            *shards, conv_shard)


def _finish_exchange(pieces, vec):
    n = len(pieces)
    n_dev = 2 * N_CHIPS

    def body(*refs):
        v_ref = refs[n]
        outs, o_ref = refs[n + 1:2 * n + 1], refs[2 * n + 1]
        buf, send_sems, recv_sems = refs[2 * n + 2:]
        x, y, c = _position()
        sibling = (x, y, 1 - c)
        me = 4 * x + 2 * y + c
        buf[me] = v_ref[...]
        started = []
        for f, (fx, fy, fc) in enumerate(DEVICE_FLIPS):
            cp = pltpu.make_async_remote_copy(
                src_ref=v_ref, dst_ref=buf.at[me], send_sem=send_sems.at[n + f], recv_sem=recv_sems.at[n + f],
                device_id=(_flip(x, fx), _flip(y, fy), _flip(c, fc)), device_id_type=MESH)
            cp.start()
            started.append(cp)
        for a in range(n):
            h = pieces[a].shape[0] // 2
            mine = outs[a].at[pl.ds(pl.multiple_of(c * h, h), h)]
            cp = pltpu.make_async_remote_copy(
                src_ref=mine, dst_ref=mine, send_sem=send_sems.at[a], recv_sem=recv_sems.at[a],
                device_id=sibling, device_id_type=MESH)
            cp.start()
            started.append(cp)
        for a in range(n):
            h = pieces[a].shape[0] // 2
            theirs = outs[a].at[pl.ds(pl.multiple_of((1 - c) * h, h), h)]
            pltpu.make_async_remote_copy(
                src_ref=theirs, dst_ref=theirs, send_sem=send_sems.at[a], recv_sem=recv_sems.at[a],
                device_id=sibling, device_id_type=MESH).wait_recv()
        for f, (fx, fy, fc) in enumerate(DEVICE_FLIPS):
            src = 4 * _flip(x, fx) + 2 * _flip(y, fy) + _flip(c, fc)
            pltpu.make_async_remote_copy(
                src_ref=v_ref, dst_ref=buf.at[src], send_sem=send_sems.at[n + f], recv_sem=recv_sems.at[n + f],
                device_id=(x, y, c), device_id_type=MESH).wait_recv()
        for cp in started:
            cp.wait_send()
        acc = buf[0]
        for d in range(1, n_dev):
            acc = acc + buf[d]
        o_ref[...] = acc

    vmem = pl.BlockSpec(memory_space=pltpu.VMEM)
    out_shape = [SDS(p.shape, p.dtype) for p in pieces] + [SDS(vec.shape, vec.dtype)]
    n_sems = n + n_dev - 1
    return pl.pallas_call(
        body, name="finish_exchange", in_specs=[ANY] * n + [vmem], out_specs=[ANY] * n + [vmem],
        out_shape=out_shape, input_output_aliases={a: a for a in range(n)},
        scratch_shapes=[pltpu.VMEM((n_dev,) + vec.shape, vec.dtype), pltpu.SemaphoreType.DMA((n_sems,)),
                        pltpu.SemaphoreType.DMA((n_sems,))])(*pieces, vec)


def _constants():
    r = jnp.arange(2 * KEY_BLOCK)[:, None] % KEY_BLOCK
    c = jnp.arange(2 * KEY_BLOCK)[None, :]
    later = jnp.where(c < KEY_BLOCK, r > c, True).astype(BF16)
    earlier = jnp.where(c < KEY_BLOCK, r < c, True).astype(BF16)
    upto = jnp.where(c < KEY_BLOCK, r <= c, True).astype(BF16)
    gr = (jnp.arange(2 * LANES)[:, None] % LANES) // GROUP
    gc = jnp.arange(LANES)[None, :] // GROUP
    gmat = (gr == gc).astype(BF16)
    return later, jnp.stack([earlier, upto]), gmat


def _rows(v):
    return v.reshape(-1, LANES)


def kernel(x, w_in, conv_w, g_conv, g_attn, w_out, ln1_g, ln1_b, w_up, w_down, ln2_g, ln2_b, loss_target, m_w_in, m_conv_w, m_g_conv, m_g_attn, m_w_out, m_ln1_g, m_ln1_b, m_w_up, m_w_down, m_ln2_g, m_ln2_b, v_w_in, v_conv_w, v_g_conv, v_g_attn, v_w_out, v_ln1_g, v_ln1_b, v_w_up, v_w_down, v_ln2_g, v_ln2_b):
    xs, target = x[0], loss_target[0]
    mesh_x, mesh_y, mesh_c = _position()
    k_idx = 2 * mesh_x + mesh_y
    kc_idx = jnp.stack([k_idx, mesh_c]).astype(jnp.int32)
    tri_later, tri_earlier, gmat = _constants()

    w_in_b, w_out_b, w_up_b, w_down_b = [
        _cast_into_slot(kc_idx, w[0], "cast_" + nm)
        for w, nm in ((w_in, "w_in"), (w_out, "w_out"), (w_up, "w_up"), (w_down, "w_down"))]
    conv_slot = jnp.pad(conv_w, ((0, 0), (0, SUBLANES - conv_w.shape[1]), (0, 0)))
    conv_b = lax.dynamic_update_slice(jnp.zeros((N_CHIPS, SUBLANES, LANES), F32), conv_slot, (k_idx, 0, 0))
    w_in_f, conv_f = _gather_weights([w_in_b], conv_b)
    taps = jnp.transpose(conv_f, (1, 0, 2)).reshape(SUBLANES, CONV_WIDTH)

    gates, qkv = _proj(xs, w_in_f)
    ycn = _conv_fwd(gates, taps, g_conv, gmat)
    o, yan, tot, w_out_f, w_up_f, w_down_f = _attn_fwd(qkv, g_attn, tri_later, gmat, [w_out_b, w_up_b, w_down_b])
    w_out_f = w_out_f.reshape(D_MODEL, D_MODEL)
    x1, xhat1, rstd1 = _mix_ln1(ycn, yan, w_out_f, xs, ln1_g, ln1_b)
    dpre2, ln2_sums, loss_sum = _mlp_fwd_loss(x1, w_up_f, w_down_f, target, ln2_g, ln2_b)

    hid, dup, dpre1, ln1_sums = _mlp_bwd_ln1(x1, dpre2, w_up_f, w_down_f, xhat1, rstd1, ln1_g)
    gw_up = _grad_tn(x1, dup, "grad_w_up", FF_SHARD, True)
    gw_down = [g.reshape(N_CHIPS, FF_SHARD, D_MODEL) for g in _grad_tn(hid, dpre2, "grad_w_down", D_MODEL, False)]
    gw_out_conv = _grad_tn(ycn, dpre1, "grad_w_out_conv", D_MODEL, False)
    gw_out_attn = _grad_tn(yan, dpre1, "grad_w_out_attn", D_MODEL, False)
    gw_out = [jnp.concatenate([gc_, ga_], axis=0).reshape(N_CHIPS, D_MODEL // N_CHIPS, D_MODEL)
              for gc_, ga_ in zip(gw_out_conv, gw_out_attn)]
    dycn, dyan = _dmix(dpre1, w_out_f)
    dq, dk, dv, gattn_sums, recv_out, recv_up, recv_down = _attn_bwd(
        qkv, o, tot, dyan, g_attn, tri_earlier, gmat, [gw_out[1], gw_up[1], gw_down[1]])
    dbg, dy, conv_sums = _conv_bwd_gate(gates, dycn, taps, g_conv, gmat)
    dproj = _dproj_assemble(gates, dy, dbg, dq, dk, dv, taps)
    gw_in = _grad_tn(xs, dproj, "grad_w_in", IN_SHARD, True)
    grad_x, recv_in = _grad_x(dproj, w_in_f, dpre1, gw_in[1])

    halves = lambda g: g.reshape(N_CHIPS, 2, g.shape[1] // 2, g.shape[2])
    pieces = [_sum_partials(kc_idx, halves(g[0]), r, "sum_partials_" + nm)
              for g, r, nm in ((gw_in, recv_in, "w_in"), (gw_out, recv_out, "w_out"), (gw_up, recv_up, "w_up"),
                               (gw_down, recv_down, "w_down"))]
    conv_rows = jnp.transpose(conv_sums[0:3].reshape(3, N_CHIPS, LANES), (1, 0, 2)).reshape(3 * N_CHIPS, LANES)
    small = jnp.concatenate([
        loss_sum, _rows(conv_sums[3]), _rows(gattn_sums[0]), _rows(ln1_sums[0]), _rows(ln1_sums[1]),
        _rows(ln2_sums[0]), _rows(ln2_sums[1]), conv_rows,
        jnp.zeros((SMALL_ROWS - ROW_CONVW - 3 * N_CHIPS, LANES), F32)], axis=0)
    g_w_in, g_w_out, g_w_up, g_w_down, total = _finish_exchange(pieces, small)
    loss = total[ROW_LOSS, 0]
    g_conv_w = lax.dynamic_slice(total, (ROW_CONVW + 3 * k_idx, 0), (3, LANES))

    def pack(gc_, ga_, l1g, l1b, l2g, l2b, cw):
        return jnp.concatenate([_rows(gc_), _rows(ga_), _rows(l1g), _rows(l1b), _rows(l2g), _rows(l2b), cw[0],
                                jnp.zeros((PARAM_ROWS + SUBLANES - ROW_CONVW - 3, LANES), F32)], axis=0)

    small_w = pack(g_conv, g_attn, ln1_g, ln1_b, ln2_g, ln2_b, conv_w)
    small_m = pack(m_g_conv, m_g_attn, m_ln1_g, m_ln1_b, m_ln2_g, m_ln2_b, m_conv_w)
    small_v = pack(v_g_conv, v_g_attn, v_ln1_g, v_ln1_b, v_ln2_g, v_ln2_b, v_conv_w)
    small_g = jnp.concatenate([total[ROW_GCONV:ROW_CONVW], g_conv_w,
                               jnp.zeros((PARAM_ROWS + SUBLANES - ROW_CONVW - 3, LANES), F32)], axis=0)
    small_out = _adamw(small_w, small_g, small_m, small_v, "adamw_small")

    def unpack(p):
        off = ROW_GCONV
        vec = lambda a, b: p[a - off:b - off].reshape(1, -1)
        return {"g_conv": vec(ROW_GCONV, ROW_GATTN), "g_attn": vec(ROW_GATTN, ROW_LN1G),
                "ln1_g": vec(ROW_LN1G, ROW_LN1B), "ln1_b": vec(ROW_LN1B, ROW_LN2G),
                "ln2_g": vec(ROW_LN2G, ROW_LN2B), "ln2_b": vec(ROW_LN2B, ROW_CONVW),
                "conv_w": p[ROW_CONVW - off:ROW_CONVW - off + 3][None]}

    big_out = {
        "w_in": _adamw(w_in[0], g_w_in, m_w_in[0], v_w_in[0], "adamw_w_in"),
        "w_out": _adamw(w_out[0], g_w_out, m_w_out[0], v_w_out[0], "adamw_w_out"),
        "w_up": _adamw(w_up[0], g_w_up, m_w_up[0], v_w_up[0], "adamw_w_up"),
        "w_down": _adamw(w_down[0], g_w_down, m_w_down[0], v_w_down[0], "adamw_w_down"),
    }
    big_grads = {"w_in": g_w_in, "w_out": g_w_out, "w_up": g_w_up, "w_down": g_w_down}
    order = ["w_in", "conv_w", "g_conv", "g_attn", "w_out", "ln1_g", "ln1_b", "w_up", "w_down", "ln2_g", "ln2_b"]
    small_parts = [unpack(small_g)] + [unpack(p) for p in small_out]

    def leaf(kind, name):
        if name in big_out:
            return (big_grads[name] if kind == 0 else big_out[name][kind - 1])[None]
        return small_parts[kind][name]

    outs = [loss, grad_x[None]]
    for kind in range(4):
        outs.extend(leaf(kind, name) for name in order)
    return tuple(outs)
```

```python
import functools

import jax
import jax.numpy as jnp
from jax import lax
from jax.experimental import pallas as pl
from jax.experimental.pallas import tpu as pltpu

F32 = jnp.float32
BF16 = jnp.bfloat16
SDS = jax.ShapeDtypeStruct

D_MODEL = 1024
CONV_WIDTH = 512
ATTN_WIDTH = 512
GROUP = 64
GATE_COLS = 3 * CONV_WIDTH
QKV_COLS = 3 * ATTN_WIDTH
IN_COLS = GATE_COLS + QKV_COLS
D_FF = 4 * D_MODEL
N_CHIPS = 4
IN_SHARD = IN_COLS // N_CHIPS
FF_SHARD = D_FF // N_CHIPS
ALPHA = float(2.0 ** 0.25)
LN_EPS = 1e-5
RMS_EPS = 1e-6
ATTN_SCALE = GROUP ** -0.5
LOG2_E = 1.4426950408889634
ADAM_LR = 0.001
ADAM_B1 = 0.9
ADAM_B2 = 0.999
ADAM_EPS = 1e-08
ADAM_WD = 0.01
ADAM_STEP = 10

LANES = 128
SUBLANES = 8
KEY_BLOCK = 128
ATTN_Q_TILE = 512
ATTN_KEY_BLOCKS = 4
ATTN_DIAG_GROUPS = 2
VMEM_LIMIT = 56 * 1024 * 1024

MESH = pl.DeviceIdType.MESH
CHIP_FLIPS = ((1, 0), (0, 1), (1, 1))
DEVICE_FLIPS = tuple((fx, fy, fc) for fx in (0, 1) for fy in (0, 1) for fc in (0, 1))[1:]
NT_DIMS = (((1,), (1,)), ((), ()))
TN_DIMS = (((0,), (0,)), ((), ()))

ROW_LOSS = 0
ROW_GCONV = 8
ROW_GATTN = 12
ROW_LN1G = 16
ROW_LN1B = 24
ROW_LN2G = 32
ROW_LN2B = 40
ROW_CONVW = 48
SMALL_ROWS = 64
PARAM_ROWS = 48


def _params(sem=None):
    return pltpu.CompilerParams(dimension_semantics=sem, vmem_limit_bytes=VMEM_LIMIT)


def _flip(v, f):
    return 1 - v if f else v


def _position():
    return lax.axis_index("x"), lax.axis_index("y"), lax.axis_index("c")


def _hilo(v):
    hi = v.astype(BF16)
    lo = (v - hi.astype(F32)).astype(BF16)
    return jnp.concatenate([hi, lo], axis=1)


def _hilo_dot(v, mat):
    return jnp.dot(_hilo(v), mat, preferred_element_type=F32)


def _group_sum(v, gmat):
    parts = [_hilo_dot(v[:, LANES * j:LANES * (j + 1)], gmat) for j in range(v.shape[1] // LANES)]
    return parts[0] if len(parts) == 1 else jnp.concatenate(parts, axis=1)


def _softplus_terms(z):
    sp = jnp.log2(1.0 + jnp.exp2(-jnp.abs(z)))
    log_beta = jnp.minimum(z, 0.0) - sp
    return log_beta, log_beta - z


def _layer_norm_fwd(pre, g, b):
    mu = jnp.mean(pre, axis=-1, keepdims=True)
    d = pre - mu
    var = jnp.mean(d * d, axis=-1, keepdims=True)
    rstd = lax.rsqrt(var + LN_EPS)
    xhat = d * rstd
    return xhat * g + b, xhat, rstd


def _layer_norm_bwd(dy, xhat, rstd, g):
    dxh = dy * g
    m1 = jnp.mean(dxh, axis=-1, keepdims=True)
    m2 = jnp.mean(dxh * xhat, axis=-1, keepdims=True)
    return rstd * (dxh - m1 - xhat * m2)


def _row_tile(s, want):
    return min(s, want)


def _cast_into_slot(kc_idx, w, name):
    r, c = w.shape
    tr = _row_tile(r, 256)

    def body(kc_ref, w_ref, o_ref):
        o_ref[...] = w_ref[...].astype(BF16)

    grid_spec = pltpu.PrefetchScalarGridSpec(
        num_scalar_prefetch=1, grid=(r // tr,),
        in_specs=[pl.BlockSpec((tr, c), lambda i, kc: (i, 0))],
        out_specs=pl.BlockSpec((None, tr, c), lambda i, kc: (kc[0], i, 0)))
    return pl.pallas_call(
        body, name=name, grid_spec=grid_spec, out_shape=SDS((N_CHIPS, r, c), BF16),
        compiler_params=_params(("parallel",)))(kc_idx, w)


def _proj(x, w_in):
    s = x.shape[0]
    tm = _row_tile(s, 512)

    def body(x_ref, w_ref, gates_ref, qkv_ref):
        xb = x_ref[...].astype(BF16)
        for k in range(N_CHIPS):
            acc = jnp.dot(xb, w_ref[k], preferred_element_type=F32)
            if k < 2:
                gates_ref[:, IN_SHARD * k:IN_SHARD * (k + 1)] = acc
            else:
                qkv_ref[:, IN_SHARD * (k - 2):IN_SHARD * (k - 1)] = acc.astype(BF16)

    return pl.pallas_call(
        body, name="proj", grid=(s // tm,),
        in_specs=[pl.BlockSpec((tm, D_MODEL), lambda i: (i, 0)),
                  pl.BlockSpec((N_CHIPS, D_MODEL, IN_SHARD), lambda i: (0, 0, 0))],
        out_specs=[pl.BlockSpec((tm, GATE_COLS), lambda i: (i, 0)),
                   pl.BlockSpec((tm, QKV_COLS), lambda i: (i, 0))],
        out_shape=[SDS((s, GATE_COLS), F32), SDS((s, QKV_COLS), BF16)],
        compiler_params=_params(("parallel",)))(x, w_in)


def _conv_forward_values(g_ref, halo_ref, taps_ref, first_block):
    gates = g_ref[...]
    tr = gates.shape[0]
    bg = gates[:, :CONV_WIDTH]
    cg = gates[:, CONV_WIDTH:2 * CONV_WIDTH]
    h = gates[:, 2 * CONV_WIDTH:]
    u = cg * h

    def prev(r):
        v = halo_ref[r:r + 1, CONV_WIDTH:2 * CONV_WIDTH] * halo_ref[r:r + 1, 2 * CONV_WIDTH:GATE_COLS]
        return jnp.where(first_block, 0.0, v)

    row = lax.broadcasted_iota(jnp.int32, (tr, CONV_WIDTH), 0)
    u1 = jnp.where(row == 0, prev(7), pltpu.roll(u, 1, 0))
    u2 = jnp.where(row == 0, prev(6), jnp.where(row == 1, prev(7), pltpu.roll(u, 2, 0)))
    y = taps_ref[0:1, :] * u2 + taps_ref[1:2, :] * u1 + taps_ref[2:3, :] * u
    return bg, cg, h, u, u1, u2, y


def _conv_fwd(gates, taps, g_conv, gmat):
    s = gates.shape[0]
    tr = _row_tile(s, 512)
    hb = tr // SUBLANES

    def body(g_ref, halo_ref, taps_ref, gain_ref, gmat_ref, out_ref):
        i = pl.program_id(0)
        bg, _, _, _, _, _, y = _conv_forward_values(g_ref, halo_ref, taps_ref, i == 0)
        yc = bg * y
        ms = _group_sum(yc * yc, gmat_ref[...]) * (1.0 / GROUP)
        out_ref[...] = (yc * lax.rsqrt(ms + RMS_EPS) * gain_ref[...]).astype(BF16)

    return pl.pallas_call(
        body, name="conv_fwd", grid=(s // tr,),
        in_specs=[pl.BlockSpec((tr, GATE_COLS), lambda i: (i, 0)),
                  pl.BlockSpec((SUBLANES, GATE_COLS), lambda i: (jnp.maximum(i * hb - 1, 0), 0)),
                  pl.BlockSpec((SUBLANES, CONV_WIDTH), lambda i: (0, 0)),
                  pl.BlockSpec((1, CONV_WIDTH), lambda i: (0, 0)),
                  pl.BlockSpec((2 * LANES, LANES), lambda i: (0, 0))],
        out_specs=pl.BlockSpec((tr, CONV_WIDTH), lambda i: (i, 0)),
        out_shape=SDS((s, CONV_WIDTH), BF16),
        compiler_params=_params(("parallel",)))(gates, gates, taps, g_conv, gmat)


def _conv_bwd_gate(gates, dycn, taps, g_conv, gmat):
    s = gates.shape[0]
    tr = _row_tile(s, 512)
    hb = tr // SUBLANES

    def body(g_ref, halo_ref, dn_ref, taps_ref, gain_ref, gmat_ref, dbg_ref, dy_ref, sums_ref):
        i = pl.program_id(0)
        bg, _, _, u, u1, u2, y = _conv_forward_values(g_ref, halo_ref, taps_ref, i == 0)
        gmat_v = gmat_ref[...]
        yc = bg * y
        rstd = lax.rsqrt(_group_sum(yc * yc, gmat_v) * (1.0 / GROUP) + RMS_EPS)
        n = yc * rstd
        dout = dn_ref[...]
        dn = dout * gain_ref[...]
        dyc = rstd * (dn - n * (_group_sum(dn * n, gmat_v) * (1.0 / GROUP)))
        dbg_ref[...] = (dyc * y).astype(BF16)
        dy = dyc * bg
        dy_ref[...] = dy

        @pl.when(i == 0)
        def _():
            sums_ref[...] = jnp.zeros_like(sums_ref)

        sums_ref[0:1, :] += jnp.sum(dy * u2, axis=0, keepdims=True)
        sums_ref[1:2, :] += jnp.sum(dy * u1, axis=0, keepdims=True)
        sums_ref[2:3, :] += jnp.sum(dy * u, axis=0, keepdims=True)
        sums_ref[3:4, :] += jnp.sum(dout * n, axis=0, keepdims=True)

    return pl.pallas_call(
        body, name="conv_bwd_gate", grid=(s // tr,),
        in_specs=[pl.BlockSpec((tr, GATE_COLS), lambda i: (i, 0)),
                  pl.BlockSpec((SUBLANES, GATE_COLS), lambda i: (jnp.maximum(i * hb - 1, 0), 0)),
                  pl.BlockSpec((tr, CONV_WIDTH), lambda i: (i, 0)),
                  pl.BlockSpec((SUBLANES, CONV_WIDTH), lambda i: (0, 0)),
                  pl.BlockSpec((1, CONV_WIDTH), lambda i: (0, 0)),
                  pl.BlockSpec((2 * LANES, LANES), lambda i: (0, 0))],
        out_specs=[pl.BlockSpec((tr, CONV_WIDTH), lambda i: (i, 0)),
                   pl.BlockSpec((tr, CONV_WIDTH), lambda i: (i, 0)),
                   pl.BlockSpec((SUBLANES, CONV_WIDTH), lambda i: (0, 0))],
        out_shape=[SDS((s, CONV_WIDTH), BF16), SDS((s, CONV_WIDTH), F32), SDS((SUBLANES, CONV_WIDTH), F32)],
        compiler_params=_params(("arbitrary",)))(gates, gates, dycn, taps, g_conv, gmat)


def _dproj_assemble(gates, dy, dbg, dq, dk, dv, taps):
    s = gates.shape[0]
    tr = _row_tile(s, 512)
    hb = tr // SUBLANES
    last = s // SUBLANES - 1
    n_blocks = s // tr

    def body(g_ref, dy_ref, halo_ref, dbg_ref, dq_ref, dk_ref, dv_ref, taps_ref, out_ref):
        i = pl.program_id(0)
        gates_v = g_ref[...]
        cg = gates_v[:, CONV_WIDTH:2 * CONV_WIDTH]
        h = gates_v[:, 2 * CONV_WIDTH:]
        dy_v = dy_ref[...]
        last_block = i == n_blocks - 1
        nxt = lambda r: jnp.where(last_block, 0.0, halo_ref[r:r + 1, :])
        row = lax.broadcasted_iota(jnp.int32, (tr, CONV_WIDTH), 0)
        d1 = jnp.where(row == tr - 1, nxt(0), pltpu.roll(dy_v, tr - 1, 0))
        d2 = jnp.where(row == tr - 1, nxt(1), jnp.where(row == tr - 2, nxt(0), pltpu.roll(dy_v, tr - 2, 0)))
        du = taps_ref[2:3, :] * dy_v + taps_ref[1:2, :] * d1 + taps_ref[0:1, :] * d2
        out_ref[:, 0:CONV_WIDTH] = dbg_ref[...]
        out_ref[:, CONV_WIDTH:2 * CONV_WIDTH] = (du * h).astype(BF16)
        out_ref[:, 2 * CONV_WIDTH:GATE_COLS] = (du * cg).astype(BF16)
        out_ref[:, GATE_COLS:GATE_COLS + ATTN_WIDTH] = dq_ref[...]
        out_ref[:, GATE_COLS + ATTN_WIDTH:GATE_COLS + 2 * ATTN_WIDTH] = dk_ref[...].astype(BF16)
        out_ref[:, GATE_COLS + 2 * ATTN_WIDTH:] = dv_ref[...].astype(BF16)

    row_spec = lambda w: pl.BlockSpec((tr, w), lambda i: (i, 0))
    return pl.pallas_call(
        body, name="dproj_assemble", grid=(s // tr,),
        in_specs=[row_spec(GATE_COLS), row_spec(CONV_WIDTH),
                  pl.BlockSpec((SUBLANES, CONV_WIDTH), lambda i: (jnp.minimum((i + 1) * hb, last), 0)),
                  row_spec(CONV_WIDTH), row_spec(ATTN_WIDTH), row_spec(ATTN_WIDTH), row_spec(ATTN_WIDTH),
                  pl.BlockSpec((SUBLANES, CONV_WIDTH), lambda i: (0, 0))],
        out_specs=row_spec(IN_COLS),
        out_shape=SDS((s, IN_COLS), BF16),
        compiler_params=_params(("parallel",)))(gates, dy, dy, dbg, dq, dk, dv, taps)


def _stack_heads(rows, nb):
    lane = lax.broadcasted_iota(jnp.int32, (1, LANES), 1)
    zero = jnp.zeros((KEY_BLOCK, LANES), rows.dtype)
    parts = []
    for blk in range(nb):
        r = rows[blk * KEY_BLOCK:(blk + 1) * KEY_BLOCK]
        parts.append(jnp.where(lane < GROUP, r, zero))
        parts.append(jnp.where(lane < GROUP, zero, r))
    return jnp.concatenate(parts, axis=0)


def _stack_hilo(v, n_cols):
    return jnp.concatenate([_hilo(v[:, c * KEY_BLOCK:(c + 1) * KEY_BLOCK]) for c in range(n_cols)], axis=0)


def _causal_mask(tq, nb, diag_base):
    shape = (tq, 2 * nb * KEY_BLOCK)
    row = lax.broadcasted_iota(jnp.int32, shape, 0)
    col = lax.broadcasted_iota(jnp.int32, shape, 1)
    key = diag_base + (col // (2 * KEY_BLOCK)) * KEY_BLOCK + col % KEY_BLOCK
    return key < row


ANY = pl.BlockSpec(memory_space=pl.ANY)


def _remote_copy(src, dst, sems, idx, target):
    return pltpu.make_async_remote_copy(src_ref=src, dst_ref=dst, send_sem=sems[0].at[idx], recv_sem=sems[1].at[idx],
                                        device_id=target, device_id_type=MESH)


def _gather_copies(bufs, sems):
    x, y, c = _position()
    sends, arrivals = [], []
    for a in range(len(bufs)):
        mine = bufs[a].at[2 * x + y]
        for j, (fx, fy) in enumerate(CHIP_FLIPS):
            tx, ty = _flip(x, fx), _flip(y, fy)
            there = bufs[a].at[2 * tx + ty]
            sends.append(_remote_copy(mine, mine, sems, 3 * a + j, (tx, ty, c)))
            arrivals.append(_remote_copy(there, there, sems, 3 * a + j, (tx, ty, c)))
    return sends, arrivals


def _reduce_copies(ins, outs, sems):
    x, y, c = _position()
    sends, arrivals = [], []
    for a in range(len(ins)):
        h = ins[a].shape[1] // 2
        for f, (fx, fy, fc) in enumerate(DEVICE_FLIPS):
            tx, ty, tc = _flip(x, fx), _flip(y, fy), _flip(c, fc)
            src = ins[a].at[2 * tx + ty, pl.ds(pl.multiple_of(tc * h, h), h)]
            sends.append(_remote_copy(src, outs[a].at[f], sems, 7 * a + f, (tx, ty, tc)))
            arrivals.append(_remote_copy(outs[a].at[f], outs[a].at[f], sems, 7 * a + f, (tx, ty, tc)))
    return sends, arrivals


def _start_copies(make):
    sends, _ = make()
    for cp in sends:
        cp.start()


def _finish_copies(make):
    sends, arrivals = make()
    for cp in arrivals:
        cp.wait_recv()
    for cp in sends:
        cp.wait_send()


def _attn_fwd(qkv, g_attn, tri, gmat, shards):
    n_w = len(shards)
    s = qkv.shape[0]
    tq = _row_tile(s, ATTN_Q_TILE)
    tk = KEY_BLOCK
    nb = ATTN_KEY_BLOCKS
    width = nb * tk
    n_groups = ATTN_DIAG_GROUPS
    group = tq // n_groups
    pairs = ATTN_WIDTH // LANES

    def body(q_ref, k_ref, v_ref, gain_ref, tri_ref, gmat_ref, *rest):
        o_ref, yn_ref, tot_ref = rest[n_w:n_w + 3]
        w_bufs, sems = rest[n_w + 3:2 * n_w + 3], rest[2 * n_w + 3:]
        copies = functools.partial(_gather_copies, w_bufs, sems)
        p, i = pl.program_id(0), pl.program_id(1)
        pl.when((p == 0) & (i == 0))(functools.partial(_start_copies, copies))
        q2 = q_ref[...]
        tri_v = tri_ref[...]

        def trip(s0, n_blk, rows, carry, diag_base):
            r0, nr = rows
            run = [carry[0], carry[1]]
            oacc = carry[2]
            ksel = _stack_heads(k_ref[pl.ds(s0, n_blk * tk), :], n_blk)
            vsel = _stack_heads(v_ref[pl.ds(s0, n_blk * tk), :], n_blk)
            z = lax.dot_general(q2[r0:r0 + nr], ksel, NT_DIMS, preferred_element_type=F32) * (ATTN_SCALE * LOG2_E)
            log_beta, log_keep = _softplus_terms(z)
            if diag_base is not None:
                valid = _causal_mask(nr, n_blk, diag_base)
                log_keep = jnp.where(valid, log_keep, 0.0)
            ct = jnp.dot(_stack_hilo(log_keep, 2 * n_blk), tri_v, preferred_element_type=F32)
            a_parts = [None] * (2 * n_blk)
            for c in reversed(range(2 * n_blk)):
                h = c % 2
                ct_c = ct[c * nr:(c + 1) * nr]
                a_parts[c] = jnp.exp2(log_beta[:, c * tk:(c + 1) * tk] + ct_c[:, :tk] + run[h])
                run[h] = run[h] + ct_c[:, tk:]
            a = jnp.concatenate(a_parts, axis=1)
            if diag_base is not None:
                a = jnp.where(valid, a, 0.0)
            oacc = oacc + jnp.dot(a.astype(BF16), vsel, preferred_element_type=F32)
            return run[0], run[1], oacc

        groups = []
        for g in range(n_groups):
            zeros = (jnp.zeros((group, tk), F32), jnp.zeros((group, tk), F32), jnp.zeros((group, LANES), F32))
            groups.append(trip(pl.multiple_of(i * tq, tq), (g + 1) * group // tk, (g * group, group), zeros,
                               -g * group))
        carry = tuple(jnp.concatenate([grp[j] for grp in groups], axis=0) for j in range(3))
        n_full = i * (tq // width)
        carry = lax.fori_loop(
            0, n_full,
            lambda it, c: trip(pl.multiple_of((n_full - 1 - it) * width, width), nb, (0, tq), c, None), carry)
        run_a, run_b, oacc = carry
        lane = lax.broadcasted_iota(jnp.int32, (1, LANES), 1)
        o_ref[...] = oacc
        tot_ref[...] = jnp.where(lane < GROUP, run_a, run_b)
        ms = _group_sum(oacc * oacc, gmat_ref[...]) * (1.0 / GROUP)
        yn_ref[...] = (oacc * lax.rsqrt(ms + RMS_EPS) * gain_ref[...]).astype(BF16)
        pl.when((p == pairs - 1) & (i == pl.num_programs(1) - 1))(functools.partial(_finish_copies, copies))

    blk = lambda: pl.BlockSpec((tq, LANES), lambda p, i: (i, p))
    return pl.pallas_call(
        body, name="attn_fwd", grid=(pairs, s // tq),
        in_specs=[pl.BlockSpec((tq, LANES), lambda p, i: (i, p)),
                  pl.BlockSpec((s, LANES), lambda p, i: (0, pairs + p)),
                  pl.BlockSpec((s, LANES), lambda p, i: (0, 2 * pairs + p)),
                  pl.BlockSpec((1, LANES), lambda p, i: (0, p)),
                  pl.BlockSpec((2 * tk, 2 * tk), lambda p, i: (0, 0)),
                  pl.BlockSpec((2 * LANES, LANES), lambda p, i: (0, 0))] + [ANY] * n_w,
        out_specs=[blk(), blk(), blk()] + [ANY] * n_w,
        out_shape=[SDS((s, ATTN_WIDTH), F32), SDS((s, ATTN_WIDTH), BF16), SDS((s, ATTN_WIDTH), F32)]
        + [SDS(w.shape, w.dtype) for w in shards],
        input_output_aliases={6 + a: 3 + a for a in range(n_w)},
        scratch_shapes=[pltpu.SemaphoreType.DMA((3 * n_w,)), pltpu.SemaphoreType.DMA((3 * n_w,))],
        compiler_params=_params(("arbitrary", "arbitrary")))(qkv, qkv, qkv, g_attn, tri, gmat, *shards)


def _attn_bwd(qkv, o, tot, dyn, g_attn, tri, gmat, partials):
    n_g = len(partials)
    s = qkv.shape[0]
    tq = _row_tile(s, ATTN_Q_TILE)
    tk = KEY_BLOCK
    nb = ATTN_KEY_BLOCKS
    width = nb * tk
    n_groups = ATTN_DIAG_GROUPS
    group = tq // n_groups
    pairs = ATTN_WIDTH // LANES

    def body(q_ref, k_ref, v_ref, o_ref, tot_ref, dyn_ref, gain_ref, tri_ref, gmat_ref, *rest):
        g_ins, (dq_ref, dk_ref, dv_ref, dg_ref) = rest[:n_g], rest[n_g:n_g + 4]
        g_outs, sems = rest[n_g + 4:2 * n_g + 4], rest[2 * n_g + 4:]
        copies = functools.partial(_reduce_copies, g_ins, g_outs, sems)
        p, i = pl.program_id(0), pl.program_id(1)
        pl.when((p == 0) & (i == 0))(functools.partial(_start_copies, copies))

        @pl.when(i == 0)
        def _():
            dk_ref[...] = jnp.zeros_like(dk_ref)
            dv_ref[...] = jnp.zeros_like(dv_ref)
            dg_ref[...] = jnp.zeros_like(dg_ref)

        gmat_v = gmat_ref[...]
        o_v = o_ref[...]
        rstd = lax.rsqrt(_group_sum(o_v * o_v, gmat_v) * (1.0 / GROUP) + RMS_EPS)
        n = o_v * rstd
        dout = dyn_ref[...]
        dg_ref[0:1, :] += jnp.sum(dout * n, axis=0, keepdims=True)
        dn = dout * gain_ref[...]
        do2 = (rstd * (dn - n * (_group_sum(dn * n, gmat_v) * (1.0 / GROUP)))).astype(BF16)
        q2 = q_ref[...]
        tot_v = tot_ref[...]
        tots = (jnp.broadcast_to(tot_v[:, 0:1], (tq, tk)), jnp.broadcast_to(tot_v[:, GROUP:GROUP + 1], (tq, tk)))
        tri_v, tri_incl_v = tri_ref[0], tri_ref[1]
        lane = lax.broadcasted_iota(jnp.int32, (1, LANES), 1)

        def trip(s0, n_blk, rows, carry, diag_base):
            r0, nr = rows
            rest_l = [carry[0], carry[1]]
            pref_g = [carry[2], carry[3]]
            dq = carry[4]
            q_rows, do_rows = q2[r0:r0 + nr], do2[r0:r0 + nr]
            ksel = _stack_heads(k_ref[pl.ds(s0, n_blk * tk), :], n_blk)
            vsel = _stack_heads(v_ref[pl.ds(s0, n_blk * tk), :], n_blk)
            z = lax.dot_general(q_rows, ksel, NT_DIMS, preferred_element_type=F32) * (ATTN_SCALE * LOG2_E)
            log_beta, log_keep = _softplus_terms(z)
            if diag_base is not None:
                valid = _causal_mask(nr, n_blk, diag_base)
                log_keep = jnp.where(valid, log_keep, 0.0)
            ctl = jnp.dot(_stack_hilo(log_keep, 2 * n_blk), tri_incl_v, preferred_element_type=F32)
            da = lax.dot_general(do_rows, vsel, NT_DIMS, preferred_element_type=F32)
            a_parts = []
            for c in range(2 * n_blk):
                h = c % 2
                ct_c = ctl[c * nr:(c + 1) * nr]
                cols = slice(c * tk, (c + 1) * tk)
                a_parts.append(jnp.exp2(log_beta[:, cols] + (rest_l[h] - ct_c[:, :tk])))
                rest_l[h] = rest_l[h] - ct_c[:, tk:]
            a = jnp.concatenate(a_parts, axis=1)
            if diag_base is not None:
                a = jnp.where(valid, a, 0.0)
            g = a * da
            ctg = jnp.dot(_stack_hilo(g, 2 * n_blk), tri_v, preferred_element_type=F32)
            dz_parts = []
            for c in range(2 * n_blk):
                h = c % 2
                ct_c = ctg[c * nr:(c + 1) * nr]
                cols = slice(c * tk, (c + 1) * tk)
                prefix = pref_g[h] + ct_c[:, :tk]
                pref_g[h] = pref_g[h] + ct_c[:, tk:]
                g_c = g[:, cols]
                dz_parts.append(g_c - jnp.exp2(log_beta[:, cols]) * (g_c + prefix))
            dz = jnp.concatenate(dz_parts, axis=1) * ATTN_SCALE
            if diag_base is not None:
                dz = jnp.where(valid, dz, 0.0)
            dzb = dz.astype(BF16)
            dq = dq + jnp.dot(dzb, ksel, preferred_element_type=F32)
            dkt = lax.dot_general(dzb, q_rows, TN_DIMS, preferred_element_type=F32)
            dvt = lax.dot_general(a.astype(BF16), do_rows, TN_DIMS, preferred_element_type=F32)
            for blk in range(n_blk):
                ra, rb = slice(2 * blk * tk, (2 * blk + 1) * tk), slice((2 * blk + 1) * tk, (2 * blk + 2) * tk)
                keys = pl.ds(pl.multiple_of(s0 + blk * tk, tk), tk)
                dk_ref[keys, :] += jnp.where(lane < GROUP, dkt[ra], dkt[rb])
                dv_ref[keys, :] += jnp.where(lane < GROUP, dvt[ra], dvt[rb])
            return rest_l[0], rest_l[1], pref_g[0], pref_g[1], dq

        zeros_qk = jnp.zeros((tq, tk), F32)
        carry = (tots[0], tots[1], zeros_qk, zeros_qk, jnp.zeros((tq, LANES), F32))
        carry = lax.fori_loop(
            0, i * (tq // width), lambda t, c: trip(pl.multiple_of(t * width, width), nb, (0, tq), c, None), carry)
        dq_groups = []
        for g in range(n_groups):
            sub = tuple(x[g * group:(g + 1) * group] for x in carry)
            dq_groups.append(trip(pl.multiple_of(i * tq, tq), (g + 1) * group // tk, (g * group, group), sub,
                                  -g * group)[4])
        dq_ref[...] = jnp.concatenate(dq_groups, axis=0).astype(BF16)
        pl.when((p == pairs - 1) & (i == pl.num_programs(1) - 1))(functools.partial(_finish_copies, copies))

    blk = lambda: pl.BlockSpec((tq, LANES), lambda p, i: (i, p))
    col = lambda: pl.BlockSpec((s, LANES), lambda p, i: (0, p))
    n_peers = len(DEVICE_FLIPS)
    return pl.pallas_call(
        body, name="attn_bwd", grid=(pairs, s // tq),
        in_specs=[pl.BlockSpec((tq, LANES), lambda p, i: (i, p)),
                  pl.BlockSpec((s, LANES), lambda p, i: (0, pairs + p)),
                  pl.BlockSpec((s, LANES), lambda p, i: (0, 2 * pairs + p)),
                  blk(), blk(), blk(),
                  pl.BlockSpec((1, LANES), lambda p, i: (0, p)),
                  pl.BlockSpec((2, 2 * tk, 2 * tk), lambda p, i: (0, 0, 0)),
                  pl.BlockSpec((2 * LANES, LANES), lambda p, i: (0, 0))] + [ANY] * n_g,
        out_specs=[blk(), col(), col(), pl.BlockSpec((SUBLANES, LANES), lambda p, i: (0, p))] + [ANY] * n_g,
        out_shape=[SDS((s, ATTN_WIDTH), BF16), SDS((s, ATTN_WIDTH), F32), SDS((s, ATTN_WIDTH), F32),
                   SDS((SUBLANES, ATTN_WIDTH), F32)]
        + [SDS((n_peers, g.shape[1] // 2, g.shape[2]), g.dtype) for g in partials],
        scratch_shapes=[pltpu.SemaphoreType.DMA((n_peers * n_g,)), pltpu.SemaphoreType.DMA((n_peers * n_g,))],
        compiler_params=_params(("arbitrary", "arbitrary")))(qkv, qkv, qkv, o, tot, dyn, g_attn, tri, gmat, *partials)


def _mix_ln1(ycn, yan, w_out, x, g, b):
    s = x.shape[0]
    tm = _row_tile(s, 512)

    def body(yc_ref, ya_ref, w_ref, x_ref, g_ref, b_ref, x1_ref, xhat_ref, rstd_ref):
        mix = jnp.dot(yc_ref[...], w_ref[0:CONV_WIDTH, :], preferred_element_type=F32)
        mix = mix + jnp.dot(ya_ref[...], w_ref[CONV_WIDTH:, :], preferred_element_type=F32)
        x1, xhat, rstd = _layer_norm_fwd(ALPHA * x_ref[...] + mix, g_ref[...], b_ref[...])
        x1_ref[...] = x1
        xhat_ref[...] = xhat
        rstd_ref[...] = rstd

    row = lambda w: pl.BlockSpec((tm, w), lambda i: (i, 0))
    vec = lambda: pl.BlockSpec((1, D_MODEL), lambda i: (0, 0))
    return pl.pallas_call(
        body, name="mix_ln1", grid=(s // tm,),
        in_specs=[row(CONV_WIDTH), row(ATTN_WIDTH), pl.BlockSpec((D_MODEL, D_MODEL), lambda i: (0, 0)),
                  row(D_MODEL), vec(), vec()],
        out_specs=[row(D_MODEL), row(D_MODEL), row(1)],
        out_shape=[SDS((s, D_MODEL), F32), SDS((s, D_MODEL), F32), SDS((s, 1), F32)],
        compiler_params=_params(("parallel",)))(ycn, yan, w_out, x, g, b)


def _mlp_fwd_loss(x1, w_up, w_down, target, g, b):
    s = x1.shape[0]
    tm = _row_tile(s, 256)

    def body(x1_ref, wu_ref, wd_ref, t_ref, g_ref, b_ref, dpre_ref, sums_ref, loss_ref, r_ref, hid_ref):
        i = pl.program_id(0)
        x1_v = x1_ref[...]
        xb = x1_v.astype(BF16)
        ffn = jnp.zeros((tm, D_MODEL), F32)
        for k in range(N_CHIPS):
            r = jnp.maximum(jnp.dot(xb, wu_ref[k], preferred_element_type=F32), 0.0)
            hid = (r * r).astype(BF16)
            r_ref[:, FF_SHARD * k:FF_SHARD * (k + 1)] = r.astype(BF16)
            hid_ref[:, FF_SHARD * k:FF_SHARD * (k + 1)] = hid
            ffn = ffn + jnp.dot(hid, wd_ref[k], preferred_element_type=F32)
        g_v = g_ref[...]
        x2, xhat, rstd = _layer_norm_fwd(ALPHA * x1_v + ffn, g_v, b_ref[...])
        err = x2 - t_ref[...]
        dx2 = err * (1.0 / D_MODEL)
        dpre_ref[...] = _layer_norm_bwd(dx2, xhat, rstd, g_v)

        @pl.when(i == 0)
        def _():
            sums_ref[...] = jnp.zeros_like(sums_ref)
            loss_ref[...] = jnp.zeros_like(loss_ref)

        sums_ref[0:1, :] += jnp.sum(dx2 * xhat, axis=0, keepdims=True)
        sums_ref[1:2, :] += jnp.sum(dx2, axis=0, keepdims=True)
        loss_ref[...] += jnp.sum(jnp.sum(err * err, axis=1, keepdims=True), axis=0, keepdims=True) * (0.5 / D_MODEL)

    row = lambda: pl.BlockSpec((tm, D_MODEL), lambda i: (i, 0))
    wide = lambda: pl.BlockSpec((tm, D_FF), lambda i: (i, 0))
    vec = lambda: pl.BlockSpec((1, D_MODEL), lambda i: (0, 0))
    return pl.pallas_call(
        body, name="mlp_fwd_loss", grid=(s // tm,),
        in_specs=[row(), _resident_weight(), _resident_weight(), row(), vec(), vec()],
        out_specs=[row(), pl.BlockSpec((SUBLANES, D_MODEL), lambda i: (0, 0)),
                   pl.BlockSpec((SUBLANES, LANES), lambda i: (0, 0)), wide(), wide()],
        out_shape=[SDS((s, D_MODEL), F32), SDS((SUBLANES, D_MODEL), F32), SDS((SUBLANES, LANES), F32),
                   SDS((s, D_FF), BF16), SDS((s, D_FF), BF16)],
        compiler_params=_params(("arbitrary",)))(x1, w_up, w_down, target, g, b)


def _resident_weight():
    return pl.BlockSpec((N_CHIPS, D_MODEL, FF_SHARD), lambda i: (0, 0, 0), pipeline_mode=pl.Buffered(1))


def _mlp_bwd_ln1(relu_up, dpre2, w_up, w_down, xhat1, rstd1, g1):
    s = dpre2.shape[0]
    tm = _row_tile(s, 256)

    def body(r_ref, d2_ref, wu_ref, wd_ref, xh_ref, rs_ref, g_ref, dup_ref, dpre_ref, sums_ref):
        i = pl.program_id(0)
        d2 = d2_ref[...]
        d2b = d2.astype(BF16)
        dx1 = ALPHA * d2
        for k in range(N_CHIPS):
            r = r_ref[:, FF_SHARD * k:FF_SHARD * (k + 1)].astype(F32)
            dhid = lax.dot_general(d2b, wd_ref[k], NT_DIMS, preferred_element_type=F32)
            dupb = (dhid * (2.0 * r)).astype(BF16)
            dup_ref[:, FF_SHARD * k:FF_SHARD * (k + 1)] = dupb
            dx1 = dx1 + lax.dot_general(dupb, wu_ref[k], NT_DIMS, preferred_element_type=F32)
        xhat = xh_ref[...]
        dpre_ref[...] = _layer_norm_bwd(dx1, xhat, rs_ref[...], g_ref[...])

        @pl.when(i == 0)
        def _():
            sums_ref[...] = jnp.zeros_like(sums_ref)

        sums_ref[0:1, :] += jnp.sum(dx1 * xhat, axis=0, keepdims=True)
        sums_ref[1:2, :] += jnp.sum(dx1, axis=0, keepdims=True)

    row = lambda w: pl.BlockSpec((tm, w), lambda i: (i, 0))
    return pl.pallas_call(
        body, name="mlp_bwd_ln1", grid=(s // tm,),
        in_specs=[row(D_FF), row(D_MODEL), _resident_weight(), _resident_weight(), row(D_MODEL), row(1),
                  pl.BlockSpec((1, D_MODEL), lambda i: (0, 0))],
        out_specs=[row(D_FF), row(D_MODEL), pl.BlockSpec((SUBLANES, D_MODEL), lambda i: (0, 0))],
        out_shape=[SDS((s, D_FF), BF16), SDS((s, D_MODEL), F32), SDS((SUBLANES, D_MODEL), F32)],
        compiler_params=_params(("arbitrary",)))(relu_up, dpre2, w_up, w_down, xhat1, rstd1, g1)


def _grad_tn(a, b, name, out_cols, stacked):
    s, ka = a.shape
    n = b.shape[1]
    ts = _row_tile(s, 2048)
    n_steps = s // ts
    if stacked:
        tka, tn = ka, out_cols
        grid = (1, n // tn, n_steps)
        shape = (n // tn, ka, tn)
        out_spec = lambda: pl.BlockSpec((None, tka, tn), lambda r, c, t: (c, 0, 0))
    else:
        tka, tn = min(ka, 1024), n
        grid = (ka // tka, 1, n_steps)
        shape = (ka, n)
        out_spec = lambda: pl.BlockSpec((tka, tn), lambda r, c, t: (r, 0))

    def body(a_ref, b_ref, o_ref, ob_ref):
        t = pl.program_id(2)

        @pl.when(t == 0)
        def _():
            o_ref[...] = jnp.zeros_like(o_ref)

        o_ref[...] += lax.dot_general(a_ref[...].astype(BF16), b_ref[...].astype(BF16), TN_DIMS,
                                      preferred_element_type=F32)

        @pl.when(t == n_steps - 1)
        def _():
            ob_ref[...] = o_ref[...].astype(BF16)

    return pl.pallas_call(
        body, name=name, grid=grid,
        in_specs=[pl.BlockSpec((ts, tka), lambda r, c, t: (t, r)),
                  pl.BlockSpec((ts, tn), lambda r, c, t: (t, c))],
        out_specs=[out_spec(), out_spec()], out_shape=[SDS(shape, F32), SDS(shape, BF16)],
        compiler_params=_params(("parallel", "parallel", "arbitrary")))(a, b)


def _dmix(dpre1, w_out):
    s = dpre1.shape[0]
    tm = _row_tile(s, 512)

    def body(d_ref, w_ref, dc_ref, da_ref):
        db = d_ref[...].astype(BF16)
        dc_ref[...] = lax.dot_general(db, w_ref[0:CONV_WIDTH, :], NT_DIMS, preferred_element_type=F32)
        da_ref[...] = lax.dot_general(db, w_ref[CONV_WIDTH:, :], NT_DIMS, preferred_element_type=F32)

    return pl.pallas_call(
        body, name="dmix", grid=(s // tm,),
        in_specs=[pl.BlockSpec((tm, D_MODEL), lambda i: (i, 0)),
                  pl.BlockSpec((D_MODEL, D_MODEL), lambda i: (0, 0))],
        out_specs=[pl.BlockSpec((tm, CONV_WIDTH), lambda i: (i, 0)),
                   pl.BlockSpec((tm, ATTN_WIDTH), lambda i: (i, 0))],
        out_shape=[SDS((s, CONV_WIDTH), F32), SDS((s, ATTN_WIDTH), F32)],
        compiler_params=_params(("parallel",)))(dpre1, w_out)


def _sum_with_peers(own_ref, r_ref, o_ref):
    acc = own_ref[...]
    for f in range(r_ref.shape[0]):
        acc = acc + r_ref[f].astype(F32)
    o_ref[...] = acc


def _grad_x(kc_idx, dproj, w_in, dpre1, partial, earlier):
    s = dproj.shape[0]
    tm = _row_tile(s, 512)
    steps = s // tm
    n_peers = len(DEVICE_FLIPS)
    n_e = len(earlier)

    def body(kc_ref, dp_ref, w_ref, d1_ref, *rest):
        sum_ins, g_in = rest[:2 * n_e], rest[2 * n_e]
        o_ref, g_out = rest[2 * n_e + 1], rest[2 * n_e + 2]
        sum_outs, sems = rest[2 * n_e + 3:3 * n_e + 3], rest[3 * n_e + 3:]
        copies = functools.partial(_reduce_copies, [g_in], [g_out], sems)
        i = pl.program_id(0)
        pl.when(i == 0)(functools.partial(_start_copies, copies))
        acc = ALPHA * d1_ref[...]
        for k in range(N_CHIPS):
            acc = acc + lax.dot_general(dp_ref[:, IN_SHARD * k:IN_SHARD * (k + 1)], w_ref[k], NT_DIMS,
                                        preferred_element_type=F32)
        o_ref[...] = acc
        for a in range(n_e):
            _sum_with_peers(sum_ins[2 * a], sum_ins[2 * a + 1], sum_outs[a])
        pl.when(i == steps - 1)(functools.partial(_finish_copies, copies))

    in_specs = [pl.BlockSpec((tm, IN_COLS), lambda i, kc: (i, 0)),
                pl.BlockSpec((N_CHIPS, D_MODEL, IN_SHARD), lambda i, kc: (0, 0, 0)),
                pl.BlockSpec((tm, D_MODEL), lambda i, kc: (i, 0))]
    out_specs = [pl.BlockSpec((tm, D_MODEL), lambda i, kc: (i, 0)), ANY]
    out_shape = [SDS((s, D_MODEL), F32), SDS((n_peers, partial.shape[1] // 2, partial.shape[2]), partial.dtype)]
    operands = []
    for own, recv in earlier:
        _, _, h, cols = own.shape
        th = h // steps
        in_specs.append(pl.BlockSpec((None, None, th, cols), lambda i, kc: (kc[0], kc[1], i, 0)))
        in_specs.append(pl.BlockSpec((n_peers, th, cols), lambda i, kc: (0, i, 0)))
        out_specs.append(pl.BlockSpec((th, cols), lambda i, kc: (kc[1] * steps + i, 0)))
        out_shape.append(SDS((2 * h, cols), F32))
        operands += [own, recv]
    grid_spec = pltpu.PrefetchScalarGridSpec(
        num_scalar_prefetch=1, grid=(steps,), in_specs=in_specs + [ANY], out_specs=out_specs,
        scratch_shapes=[pltpu.SemaphoreType.DMA((n_peers,)), pltpu.SemaphoreType.DMA((n_peers,))])
    return pl.pallas_call(
        body, name="grad_x", grid_spec=grid_spec, out_shape=out_shape,
        compiler_params=_params(("arbitrary",)))(kc_idx, dproj, w_in, dpre1, *operands, partial)


def _adamw(w, g, m, v, name):
    r, c = w.shape
    tr = _row_tile(r, 256)

    def body(w_ref, g_ref, m_ref, v_ref, d_ref, nm_ref, nv_ref):
        g_v = g_ref[...]
        nm = ADAM_B1 * m_ref[...] + (1.0 - ADAM_B1) * g_v
        nv = ADAM_B2 * v_ref[...] + (1.0 - ADAM_B2) * (g_v * g_v)
        m_hat = nm / (1.0 - ADAM_B1 ** ADAM_STEP)
        v_hat = nv / (1.0 - ADAM_B2 ** ADAM_STEP)
        d_ref[...] = -ADAM_LR * (m_hat / (jnp.sqrt(v_hat) + ADAM_EPS) + ADAM_WD * w_ref[...])
        nm_ref[...] = nm
        nv_ref[...] = nv

    spec = lambda: pl.BlockSpec((tr, c), lambda i: (i, 0))
    return pl.pallas_call(
        body, name=name, grid=(r // tr,),
        in_specs=[spec(), spec(), spec(), spec()], out_specs=[spec(), spec(), spec()],
        out_shape=[SDS((r, c), F32)] * 3, compiler_params=_params(("parallel",)))(w, g, m, v)


def _sum_partials(kc_idx, grad, recv, name):
    _, _, h, cols = grad.shape
    th = _row_tile(h, 128)
    n_peers = recv.shape[0]

    def body(kc_ref, own_ref, r_ref, o_ref):
        _sum_with_peers(own_ref, r_ref, o_ref)

    grid_spec = pltpu.PrefetchScalarGridSpec(
        num_scalar_prefetch=1, grid=(h // th,),
        in_specs=[pl.BlockSpec((None, None, th, cols), lambda t, kc: (kc[0], kc[1], t, 0)),
                  pl.BlockSpec((n_peers, th, cols), lambda t, kc: (0, t, 0))],
        out_specs=pl.BlockSpec((th, cols), lambda t, kc: (kc[1] * (h // th) + t, 0)))
    return pl.pallas_call(
        body, name=name, grid_spec=grid_spec, out_shape=SDS((2 * h, cols), F32),
        compiler_params=_params(("parallel",)))(kc_idx, grad, recv)


def _gather_weights(shards, conv_shard):
    n = len(shards)

    def body(*refs):
        outs, conv_out = refs[n + 1:2 * n + 1], refs[2 * n + 1]
        send_sems, recv_sems = refs[2 * n + 2:]
        x, y, c = _position()
        k = 2 * x + y
        sibling = (x, y, 1 - c)
        chips = [(_flip(x, fx), _flip(y, fy)) for fx, fy in CHIP_FLIPS]

        def half(a, rows_of_core):
            h = shards[a].shape[1] // 2
            return pl.ds(pl.multiple_of(rows_of_core * h, h), h)

        def remote(src, dst, idx, target):
            return pltpu.make_async_remote_copy(src_ref=src, dst_ref=dst, send_sem=send_sems.at[idx],
                                                recv_sem=recv_sems.at[idx], device_id=target,
                                                device_id_type=MESH)

        started = []
        for a in range(n):
            mine = outs[a].at[k, half(a, c)]
            for j, (tx, ty) in enumerate(chips):
                cp = remote(mine, mine, 6 * a + j, (tx, ty, c))
                cp.start()
                started.append(cp)
        for j, (tx, ty) in enumerate(chips):
            cp = remote(conv_out.at[k], conv_out.at[k], 6 * n + j, (tx, ty, c))
            cp.start()
            started.append(cp)
        for a in range(n):
            for j, (tx, ty) in enumerate(chips):
                kj = 2 * tx + ty
                landed = outs[a].at[kj, half(a, c)]
                remote(landed, landed, 6 * a + j, sibling).wait_recv()
                cp = remote(landed, landed, 6 * a + 3 + j, sibling)
                cp.start()
                started.append(cp)
        for a in range(n):
            for j, (tx, ty) in enumerate(chips):
                kj = 2 * tx + ty
                other = outs[a].at[kj, half(a, 1 - c)]
                remote(other, other, 6 * a + 3 + j, sibling).wait_recv()
        for j, (tx, ty) in enumerate(chips):
            kj = 2 * tx + ty
            remote(conv_out.at[kj], conv_out.at[kj], 6 * n + j, sibling).wait_recv()
        for cp in started:
            cp.wait_send()

    out_shape = [SDS(w.shape, w.dtype) for w in shards] + [SDS(conv_shard.shape, conv_shard.dtype)]
    n_sems = 6 * n + 3
    return pl.pallas_call(
        body, name="gather_weights", in_specs=[ANY] * (n + 1), out_specs=[ANY] * (n + 1),
        out_shape=out_shape, input_output_aliases={a: a for a in range(n + 1)},
        scratch_shapes=[pltpu.SemaphoreType.DMA((n_sems,)), pltpu.SemaphoreType.DMA((n_sems,))])(
            *shards, conv_shard)


def _finish_exchange(pieces, vec):
    n = len(pieces)
    n_dev = 2 * N_CHIPS

    def body(*refs):
        v_ref = refs[n]
        outs, o_ref = refs[n + 1:2 * n + 1], refs[2 * n + 1]
        buf, send_sems, recv_sems = refs[2 * n + 2:]
        x, y, c = _position()
        sibling = (x, y, 1 - c)
        me = 4 * x + 2 * y + c
        buf[me] = v_ref[...]
        started = []
        for f, (fx, fy, fc) in enumerate(DEVICE_FLIPS):
            cp = pltpu.make_async_remote_copy(
                src_ref=v_ref, dst_ref=buf.at[me], send_sem=send_sems.at[n + f], recv_sem=recv_sems.at[n + f],
                device_id=(_flip(x, fx), _flip(y, fy), _flip(c, fc)), device_id_type=MESH)
            cp.start()
            started.append(cp)
        for a in range(n):
            h = pieces[a].shape[0] // 2
            mine = outs[a].at[pl.ds(pl.multiple_of(c * h, h), h)]
            cp = pltpu.make_async_remote_copy(
                src_ref=mine, dst_ref=mine, send_sem=send_sems.at[a], recv_sem=recv_sems.at[a],
                device_id=sibling, device_id_type=MESH)
            cp.start()
            started.append(cp)
        for a in range(n):
            h = pieces[a].shape[0] // 2
            theirs = outs[a].at[pl.ds(pl.multiple_of((1 - c) * h, h), h)]
            pltpu.make_async_remote_copy(
                src_ref=theirs, dst_ref=theirs, send_sem=send_sems.at[a], recv_sem=recv_sems.at[a],
                device_id=sibling, device_id_type=MESH).wait_recv()
        for f, (fx, fy, fc) in enumerate(DEVICE_FLIPS):
            src = 4 * _flip(x, fx) + 2 * _flip(y, fy) + _flip(c, fc)
            pltpu.make_async_remote_copy(
                src_ref=v_ref, dst_ref=buf.at[src], send_sem=send_sems.at[n + f], recv_sem=recv_sems.at[n + f],
                device_id=(x, y, c), device_id_type=MESH).wait_recv()
        for cp in started:
            cp.wait_send()
        acc = buf[0]
        for d in range(1, n_dev):
            acc = acc + buf[d]
        o_ref[...] = acc

    vmem = pl.BlockSpec(memory_space=pltpu.VMEM)
    out_shape = [SDS(p.shape, p.dtype) for p in pieces] + [SDS(vec.shape, vec.dtype)]
    n_sems = n + n_dev - 1
    return pl.pallas_call(
        body, name="finish_exchange", in_specs=[ANY] * n + [vmem], out_specs=[ANY] * n + [vmem],
        out_shape=out_shape, input_output_aliases={a: a for a in range(n)},
        scratch_shapes=[pltpu.VMEM((n_dev,) + vec.shape, vec.dtype), pltpu.SemaphoreType.DMA((n_sems,)),
                        pltpu.SemaphoreType.DMA((n_sems,))])(*pieces, vec)


def _constants():
    r = jnp.arange(2 * KEY_BLOCK)[:, None] % KEY_BLOCK
    c = jnp.arange(2 * KEY_BLOCK)[None, :]
    later = jnp.where(c < KEY_BLOCK, r > c, True).astype(BF16)
    earlier = jnp.where(c < KEY_BLOCK, r < c, True).astype(BF16)
    upto = jnp.where(c < KEY_BLOCK, r <= c, True).astype(BF16)
    gr = (jnp.arange(2 * LANES)[:, None] % LANES) // GROUP
    gc = jnp.arange(LANES)[None, :] // GROUP
    gmat = (gr == gc).astype(BF16)
    return later, jnp.stack([earlier, upto]), gmat


def _rows(v):
    return v.reshape(-1, LANES)


def kernel(x, w_in, conv_w, g_conv, g_attn, w_out, ln1_g, ln1_b, w_up, w_down, ln2_g, ln2_b, loss_target, m_w_in, m_conv_w, m_g_conv, m_g_attn, m_w_out, m_ln1_g, m_ln1_b, m_w_up, m_w_down, m_ln2_g, m_ln2_b, v_w_in, v_conv_w, v_g_conv, v_g_attn, v_w_out, v_ln1_g, v_ln1_b, v_w_up, v_w_down, v_ln2_g, v_ln2_b):
    xs, target = x[0], loss_target[0]
    mesh_x, mesh_y, mesh_c = _position()
    k_idx = 2 * mesh_x + mesh_y
    kc_idx = jnp.stack([k_idx, mesh_c]).astype(jnp.int32)
    tri_later, tri_earlier, gmat = _constants()

    w_in_b, w_out_b, w_up_b, w_down_b = [
        _cast_into_slot(kc_idx, w[0], "cast_" + nm)
        for w, nm in ((w_in, "w_in"), (w_out, "w_out"), (w_up, "w_up"), (w_down, "w_down"))]
    conv_slot = jnp.pad(conv_w, ((0, 0), (0, SUBLANES - conv_w.shape[1]), (0, 0)))
    conv_b = lax.dynamic_update_slice(jnp.zeros((N_CHIPS, SUBLANES, LANES), F32), conv_slot, (k_idx, 0, 0))
    w_in_f, conv_f = _gather_weights([w_in_b], conv_b)
    taps = jnp.transpose(conv_f, (1, 0, 2)).reshape(SUBLANES, CONV_WIDTH)

    gates, qkv = _proj(xs, w_in_f)
    ycn = _conv_fwd(gates, taps, g_conv, gmat)
    o, yan, tot, w_out_f, w_up_f, w_down_f = _attn_fwd(qkv, g_attn, tri_later, gmat, [w_out_b, w_up_b, w_down_b])
    w_out_f = w_out_f.reshape(D_MODEL, D_MODEL)
    x1, xhat1, rstd1 = _mix_ln1(ycn, yan, w_out_f, xs, ln1_g, ln1_b)
    dpre2, ln2_sums, loss_sum, relu_up, hid = _mlp_fwd_loss(x1, w_up_f, w_down_f, target, ln2_g, ln2_b)

    dup, dpre1, ln1_sums = _mlp_bwd_ln1(relu_up, dpre2, w_up_f, w_down_f, xhat1, rstd1, ln1_g)
    gw_up = _grad_tn(x1, dup, "grad_w_up", FF_SHARD, True)
    gw_down = [g.reshape(N_CHIPS, FF_SHARD, D_MODEL) for g in _grad_tn(hid, dpre2, "grad_w_down", D_MODEL, False)]
    gw_out_conv = _grad_tn(ycn, dpre1, "grad_w_out_conv", D_MODEL, False)
    gw_out_attn = _grad_tn(yan, dpre1, "grad_w_out_attn", D_MODEL, False)
    gw_out = [jnp.concatenate([gc_, ga_], axis=0).reshape(N_CHIPS, D_MODEL // N_CHIPS, D_MODEL)
              for gc_, ga_ in zip(gw_out_conv, gw_out_attn)]
    dycn, dyan = _dmix(dpre1, w_out_f)
    dq, dk, dv, gattn_sums, recv_out, recv_up, recv_down = _attn_bwd(
        qkv, o, tot, dyan, g_attn, tri_earlier, gmat, [gw_out[1], gw_up[1], gw_down[1]])
    dbg, dy, conv_sums = _conv_bwd_gate(gates, dycn, taps, g_conv, gmat)
    dproj = _dproj_assemble(gates, dy, dbg, dq, dk, dv, taps)
    gw_in = _grad_tn(xs, dproj, "grad_w_in", IN_SHARD, True)
    halves = lambda g: g.reshape(N_CHIPS, 2, g.shape[1] // 2, g.shape[2])
    grad_x, recv_in, p_out, p_up, p_down = _grad_x(
        kc_idx, dproj, w_in_f, dpre1, gw_in[1],
        [(halves(gw_out[0]), recv_out), (halves(gw_up[0]), recv_up), (halves(gw_down[0]), recv_down)])
    pieces = [_sum_partials(kc_idx, halves(gw_in[0]), recv_in, "sum_partials_w_in"), p_out, p_up, p_down]
    conv_rows = jnp.transpose(conv_sums[0:3].reshape(3, N_CHIPS, LANES), (1, 0, 2)).reshape(3 * N_CHIPS, LANES)
    small = jnp.concatenate([
        loss_sum, _rows(conv_sums[3]), _rows(gattn_sums[0]), _rows(ln1_sums[0]), _rows(ln1_sums[1]),
        _rows(ln2_sums[0]), _rows(ln2_sums[1]), conv_rows,
        jnp.zeros((SMALL_ROWS - ROW_CONVW - 3 * N_CHIPS, LANES), F32)], axis=0)
    g_w_in, g_w_out, g_w_up, g_w_down, total = _finish_exchange(pieces, small)
    loss = total[ROW_LOSS, 0]
    g_conv_w = lax.dynamic_slice(total, (ROW_CONVW + 3 * k_idx, 0), (3, LANES))

    def pack(gc_, ga_, l1g, l1b, l2g, l2b, cw):
        return jnp.concatenate([_rows(gc_), _rows(ga_), _rows(l1g), _rows(l1b), _rows(l2g), _rows(l2b), cw[0],
                                jnp.zeros((PARAM_ROWS + SUBLANES - ROW_CONVW - 3, LANES), F32)], axis=0)

    small_w = pack(g_conv, g_attn, ln1_g, ln1_b, ln2_g, ln2_b, conv_w)
    small_m = pack(m_g_conv, m_g_attn, m_ln1_g, m_ln1_b, m_ln2_g, m_ln2_b, m_conv_w)
    small_v = pack(v_g_conv, v_g_attn, v_ln1_g, v_ln1_b, v_ln2_g, v_ln2_b, v_conv_w)
    small_g = jnp.concatenate([total[ROW_GCONV:ROW_CONVW], g_conv_w,
                               jnp.zeros((PARAM_ROWS + SUBLANES - ROW_CONVW - 3, LANES), F32)], axis=0)
    small_out = _adamw(small_w, small_g, small_m, small_v, "adamw_small")

    def unpack(p):
        off = ROW_GCONV
        vec = lambda a, b: p[a - off:b - off].reshape(1, -1)
        return {"g_conv": vec(ROW_GCONV, ROW_GATTN), "g_attn": vec(ROW_GATTN, ROW_LN1G),
                "ln1_g": vec(ROW_LN1G, ROW_LN1B), "ln1_b": vec(ROW_LN1B, ROW_LN2G),
                "ln2_g": vec(ROW_LN2G, ROW_LN2B), "ln2_b": vec(ROW_LN2B, ROW_CONVW),
                "conv_w": p[ROW_CONVW - off:ROW_CONVW - off + 3][None]}

    big_out = {
        "w_in": _adamw(w_in[0], g_w_in, m_w_in[0], v_w_in[0], "adamw_w_in"),
        "w_out": _adamw(w_out[0], g_w_out, m_w_out[0], v_w_out[0], "adamw_w_out"),
        "w_up": _adamw(w_up[0], g_w_up, m_w_up[0], v_w_up[0], "adamw_w_up"),
        "w_down": _adamw(w_down[0], g_w_down, m_w_down[0], v_w_down[0], "adamw_w_down"),
    }
    big_grads = {"w_in": g_w_in, "w_out": g_w_out, "w_up": g_w_up, "w_down": g_w_down}
    order = ["w_in", "conv_w", "g_conv", "g_attn", "w_out", "ln1_g", "ln1_b", "w_up", "w_down", "ln2_g", "ln2_b"]
    small_parts = [unpack(small_g)] + [unpack(p) for p in small_out]

    def leaf(kind, name):
        if name in big_out:
            return (big_grads[name] if kind == 0 else big_out[name][kind - 1])[None]
        return small_parts[kind][name]

    outs = [loss, grad_x[None]]
    for kind in range(4):
        outs.extend(leaf(kind, name) for name in order)
    return tuple(outs)
```

```python
import functools

import jax
import jax.numpy as jnp
from jax import lax
from jax.experimental import pallas as pl
from jax.experimental.pallas import tpu as pltpu

F32 = jnp.float32
BF16 = jnp.bfloat16
SDS = jax.ShapeDtypeStruct

D_MODEL = 1024
CONV_WIDTH = 512
ATTN_WIDTH = 512
GROUP = 64
GATE_COLS = 3 * CONV_WIDTH
QKV_COLS = 3 * ATTN_WIDTH
IN_COLS = GATE_COLS + QKV_COLS
D_FF = 4 * D_MODEL
N_CHIPS = 4
IN_SHARD = IN_COLS // N_CHIPS
FF_SHARD = D_FF // N_CHIPS
ALPHA = float(2.0 ** 0.25)
LN_EPS = 1e-5
RMS_EPS = 1e-6
ATTN_SCALE = GROUP ** -0.5
LOG2_E = 1.4426950408889634
ADAM_LR = 0.001
ADAM_B1 = 0.9
ADAM_B2 = 0.999
ADAM_EPS = 1e-08
ADAM_WD = 0.01
ADAM_STEP = 10

LANES = 128
SUBLANES = 8
KEY_BLOCK = 128
ATTN_Q_TILE = 512
ATTN_KEY_BLOCKS = 4
ATTN_DIAG_GROUPS = 2
ATTN_DEAD_LOG2 = 200.0
VMEM_LIMIT = 56 * 1024 * 1024

MESH = pl.DeviceIdType.MESH
CHIP_FLIPS = ((1, 0), (0, 1), (1, 1))
DEVICE_FLIPS = tuple((fx, fy, fc) for fx in (0, 1) for fy in (0, 1) for fc in (0, 1))[1:]
NT_DIMS = (((1,), (1,)), ((), ()))
TN_DIMS = (((0,), (0,)), ((), ()))

ROW_LOSS = 0
ROW_GCONV = 8
ROW_GATTN = 12
ROW_LN1G = 16
ROW_LN1B = 24
ROW_LN2G = 32
ROW_LN2B = 40
ROW_CONVW = 48
SMALL_ROWS = 64
PARAM_ROWS = 48


def _params(sem=None):
    return pltpu.CompilerParams(dimension_semantics=sem, vmem_limit_bytes=VMEM_LIMIT)


def _flip(v, f):
    return 1 - v if f else v


def _position():
    return lax.axis_index("x"), lax.axis_index("y"), lax.axis_index("c")


def _hilo(v):
    hi = v.astype(BF16)
    lo = (v - hi.astype(F32)).astype(BF16)
    return jnp.concatenate([hi, lo], axis=1)


def _hilo_dot(v, mat):
    return jnp.dot(_hilo(v), mat, preferred_element_type=F32)


def _group_sum(v, gmat):
    parts = [_hilo_dot(v[:, LANES * j:LANES * (j + 1)], gmat) for j in range(v.shape[1] // LANES)]
    return parts[0] if len(parts) == 1 else jnp.concatenate(parts, axis=1)


def _softplus_terms(z):
    sp = jnp.log2(1.0 + jnp.exp2(-jnp.abs(z)))
    log_beta = jnp.minimum(z, 0.0) - sp
    return log_beta, log_beta - z


def _layer_norm_fwd(pre, g, b):
    mu = jnp.mean(pre, axis=-1, keepdims=True)
    d = pre - mu
    var = jnp.mean(d * d, axis=-1, keepdims=True)
    rstd = lax.rsqrt(var + LN_EPS)
    xhat = d * rstd
    return xhat * g + b, xhat, rstd


def _layer_norm_bwd(dy, xhat, rstd, g):
    dxh = dy * g
    m1 = jnp.mean(dxh, axis=-1, keepdims=True)
    m2 = jnp.mean(dxh * xhat, axis=-1, keepdims=True)
    return rstd * (dxh - m1 - xhat * m2)


def _row_tile(s, want):
    return min(s, want)


def _cast_into_slot(kc_idx, w, name):
    r, c = w.shape
    tr = _row_tile(r, 256)

    def body(kc_ref, w_ref, o_ref):
        o_ref[...] = w_ref[...].astype(BF16)

    grid_spec = pltpu.PrefetchScalarGridSpec(
        num_scalar_prefetch=1, grid=(r // tr,),
        in_specs=[pl.BlockSpec((tr, c), lambda i, kc: (i, 0))],
        out_specs=pl.BlockSpec((None, tr, c), lambda i, kc: (kc[0], i, 0)))
    return pl.pallas_call(
        body, name=name, grid_spec=grid_spec, out_shape=SDS((N_CHIPS, r, c), BF16),
        compiler_params=_params(("parallel",)))(kc_idx, w)


def _proj(x, w_in):
    s = x.shape[0]
    tm = _row_tile(s, 512)

    def body(x_ref, w_ref, gates_ref, qkv_ref):
        xb = x_ref[...].astype(BF16)
        for k in range(N_CHIPS):
            acc = jnp.dot(xb, w_ref[k], preferred_element_type=F32)
            if k < 2:
                gates_ref[:, IN_SHARD * k:IN_SHARD * (k + 1)] = acc
            else:
                qkv_ref[:, IN_SHARD * (k - 2):IN_SHARD * (k - 1)] = acc.astype(BF16)

    return pl.pallas_call(
        body, name="proj", grid=(s // tm,),
        in_specs=[pl.BlockSpec((tm, D_MODEL), lambda i: (i, 0)),
                  pl.BlockSpec((N_CHIPS, D_MODEL, IN_SHARD), lambda i: (0, 0, 0))],
        out_specs=[pl.BlockSpec((tm, GATE_COLS), lambda i: (i, 0)),
                   pl.BlockSpec((tm, QKV_COLS), lambda i: (i, 0))],
        out_shape=[SDS((s, GATE_COLS), F32), SDS((s, QKV_COLS), BF16)],
        compiler_params=_params(("parallel",)))(x, w_in)


def _conv_forward_values(g_ref, halo_ref, taps_ref, first_block):
    gates = g_ref[...]
    tr = gates.shape[0]
    bg = gates[:, :CONV_WIDTH]
    cg = gates[:, CONV_WIDTH:2 * CONV_WIDTH]
    h = gates[:, 2 * CONV_WIDTH:]
    u = cg * h

    def prev(r):
        v = halo_ref[r:r + 1, CONV_WIDTH:2 * CONV_WIDTH] * halo_ref[r:r + 1, 2 * CONV_WIDTH:GATE_COLS]
        return jnp.where(first_block, 0.0, v)

    row = lax.broadcasted_iota(jnp.int32, (tr, CONV_WIDTH), 0)
    u1 = jnp.where(row == 0, prev(7), pltpu.roll(u, 1, 0))
    u2 = jnp.where(row == 0, prev(6), jnp.where(row == 1, prev(7), pltpu.roll(u, 2, 0)))
    y = taps_ref[0:1, :] * u2 + taps_ref[1:2, :] * u1 + taps_ref[2:3, :] * u
    return bg, cg, h, u, u1, u2, y


def _conv_fwd(gates, taps, g_conv, gmat):
    s = gates.shape[0]
    tr = _row_tile(s, 512)
    hb = tr // SUBLANES

    def body(g_ref, halo_ref, taps_ref, gain_ref, gmat_ref, out_ref):
        i = pl.program_id(0)
        bg, _, _, _, _, _, y = _conv_forward_values(g_ref, halo_ref, taps_ref, i == 0)
        yc = bg * y
        ms = _group_sum(yc * yc, gmat_ref[...]) * (1.0 / GROUP)
        out_ref[...] = (yc * lax.rsqrt(ms + RMS_EPS) * gain_ref[...]).astype(BF16)

    return pl.pallas_call(
        body, name="conv_fwd", grid=(s // tr,),
        in_specs=[pl.BlockSpec((tr, GATE_COLS), lambda i: (i, 0)),
                  pl.BlockSpec((SUBLANES, GATE_COLS), lambda i: (jnp.maximum(i * hb - 1, 0), 0)),
                  pl.BlockSpec((SUBLANES, CONV_WIDTH), lambda i: (0, 0)),
                  pl.BlockSpec((1, CONV_WIDTH), lambda i: (0, 0)),
                  pl.BlockSpec((2 * LANES, LANES), lambda i: (0, 0))],
        out_specs=pl.BlockSpec((tr, CONV_WIDTH), lambda i: (i, 0)),
        out_shape=SDS((s, CONV_WIDTH), BF16),
        compiler_params=_params(("parallel",)))(gates, gates, taps, g_conv, gmat)


def _conv_bwd_gate(gates, dycn, taps, g_conv, gmat):
    s = gates.shape[0]
    tr = _row_tile(s, 512)
    hb = tr // SUBLANES

    def body(g_ref, halo_ref, dn_ref, taps_ref, gain_ref, gmat_ref, dbg_ref, dy_ref, sums_ref):
        i = pl.program_id(0)
        bg, _, _, u, u1, u2, y = _conv_forward_values(g_ref, halo_ref, taps_ref, i == 0)
        gmat_v = gmat_ref[...]
        yc = bg * y
        rstd = lax.rsqrt(_group_sum(yc * yc, gmat_v) * (1.0 / GROUP) + RMS_EPS)
        n = yc * rstd
        dout = dn_ref[...]
        dn = dout * gain_ref[...]
        dyc = rstd * (dn - n * (_group_sum(dn * n, gmat_v) * (1.0 / GROUP)))
        dbg_ref[...] = (dyc * y).astype(BF16)
        dy = dyc * bg
        dy_ref[...] = dy

        @pl.when(i == 0)
        def _():
            sums_ref[...] = jnp.zeros_like(sums_ref)

        sums_ref[0:1, :] += jnp.sum(dy * u2, axis=0, keepdims=True)
        sums_ref[1:2, :] += jnp.sum(dy * u1, axis=0, keepdims=True)
        sums_ref[2:3, :] += jnp.sum(dy * u, axis=0, keepdims=True)
        sums_ref[3:4, :] += jnp.sum(dout * n, axis=0, keepdims=True)

    return pl.pallas_call(
        body, name="conv_bwd_gate", grid=(s // tr,),
        in_specs=[pl.BlockSpec((tr, GATE_COLS), lambda i: (i, 0)),
                  pl.BlockSpec((SUBLANES, GATE_COLS), lambda i: (jnp.maximum(i * hb - 1, 0), 0)),
                  pl.BlockSpec((tr, CONV_WIDTH), lambda i: (i, 0)),
                  pl.BlockSpec((SUBLANES, CONV_WIDTH), lambda i: (0, 0)),
                  pl.BlockSpec((1, CONV_WIDTH), lambda i: (0, 0)),
                  pl.BlockSpec((2 * LANES, LANES), lambda i: (0, 0))],
        out_specs=[pl.BlockSpec((tr, CONV_WIDTH), lambda i: (i, 0)),
                   pl.BlockSpec((tr, CONV_WIDTH), lambda i: (i, 0)),
                   pl.BlockSpec((SUBLANES, CONV_WIDTH), lambda i: (0, 0))],
        out_shape=[SDS((s, CONV_WIDTH), BF16), SDS((s, CONV_WIDTH), F32), SDS((SUBLANES, CONV_WIDTH), F32)],
        compiler_params=_params(("arbitrary",)))(gates, gates, dycn, taps, g_conv, gmat)


def _dproj_assemble(gates, dy, dbg, dq, dk, dv, taps):
    s = gates.shape[0]
    tr = _row_tile(s, 512)
    hb = tr // SUBLANES
    last = s // SUBLANES - 1
    n_blocks = s // tr

    def body(g_ref, dy_ref, halo_ref, dbg_ref, dq_ref, dk_ref, dv_ref, taps_ref, out_ref):
        i = pl.program_id(0)
        gates_v = g_ref[...]
        cg = gates_v[:, CONV_WIDTH:2 * CONV_WIDTH]
        h = gates_v[:, 2 * CONV_WIDTH:]
        dy_v = dy_ref[...]
        last_block = i == n_blocks - 1
        nxt = lambda r: jnp.where(last_block, 0.0, halo_ref[r:r + 1, :])
        row = lax.broadcasted_iota(jnp.int32, (tr, CONV_WIDTH), 0)
        d1 = jnp.where(row == tr - 1, nxt(0), pltpu.roll(dy_v, tr - 1, 0))
        d2 = jnp.where(row == tr - 1, nxt(1), jnp.where(row == tr - 2, nxt(0), pltpu.roll(dy_v, tr - 2, 0)))
        du = taps_ref[2:3, :] * dy_v + taps_ref[1:2, :] * d1 + taps_ref[0:1, :] * d2
        out_ref[:, 0:CONV_WIDTH] = dbg_ref[...]
        out_ref[:, CONV_WIDTH:2 * CONV_WIDTH] = (du * h).astype(BF16)
        out_ref[:, 2 * CONV_WIDTH:GATE_COLS] = (du * cg).astype(BF16)
        out_ref[:, GATE_COLS:GATE_COLS + ATTN_WIDTH] = dq_ref[...]
        out_ref[:, GATE_COLS + ATTN_WIDTH:GATE_COLS + 2 * ATTN_WIDTH] = dk_ref[...].astype(BF16)
        out_ref[:, GATE_COLS + 2 * ATTN_WIDTH:] = dv_ref[...].astype(BF16)

    row_spec = lambda w: pl.BlockSpec((tr, w), lambda i: (i, 0))
    return pl.pallas_call(
        body, name="dproj_assemble", grid=(s // tr,),
        in_specs=[row_spec(GATE_COLS), row_spec(CONV_WIDTH),
                  pl.BlockSpec((SUBLANES, CONV_WIDTH), lambda i: (jnp.minimum((i + 1) * hb, last), 0)),
                  row_spec(CONV_WIDTH), row_spec(ATTN_WIDTH), row_spec(ATTN_WIDTH), row_spec(ATTN_WIDTH),
                  pl.BlockSpec((SUBLANES, CONV_WIDTH), lambda i: (0, 0))],
        out_specs=row_spec(IN_COLS),
        out_shape=SDS((s, IN_COLS), BF16),
        compiler_params=_params(("parallel",)))(gates, dy, dy, dbg, dq, dk, dv, taps)


def _stack_heads(rows, nb):
    lane = lax.broadcasted_iota(jnp.int32, (1, LANES), 1)
    zero = jnp.zeros((KEY_BLOCK, LANES), rows.dtype)
    parts = []
    for blk in range(nb):
        r = rows[blk * KEY_BLOCK:(blk + 1) * KEY_BLOCK]
        parts.append(jnp.where(lane < GROUP, r, zero))
        parts.append(jnp.where(lane < GROUP, zero, r))
    return jnp.concatenate(parts, axis=0)


def _stack_hilo(v, n_cols):
    return jnp.concatenate([_hilo(v[:, c * KEY_BLOCK:(c + 1) * KEY_BLOCK]) for c in range(n_cols)], axis=0)


def _causal_mask(tq, nb, diag_base):
    shape = (tq, 2 * nb * KEY_BLOCK)
    row = lax.broadcasted_iota(jnp.int32, shape, 0)
    col = lax.broadcasted_iota(jnp.int32, shape, 1)
    key = diag_base + (col // (2 * KEY_BLOCK)) * KEY_BLOCK + col % KEY_BLOCK
    return key < row


ANY = pl.BlockSpec(memory_space=pl.ANY)


def _remote_copy(src, dst, sems, idx, target):
    return pltpu.make_async_remote_copy(src_ref=src, dst_ref=dst, send_sem=sems[0].at[idx], recv_sem=sems[1].at[idx],
                                        device_id=target, device_id_type=MESH)


def _gather_copies(bufs, sems):
    x, y, c = _position()
    sends, arrivals = [], []
    for a in range(len(bufs)):
        mine = bufs[a].at[2 * x + y]
        for j, (fx, fy) in enumerate(CHIP_FLIPS):
            tx, ty = _flip(x, fx), _flip(y, fy)
            there = bufs[a].at[2 * tx + ty]
            sends.append(_remote_copy(mine, mine, sems, 3 * a + j, (tx, ty, c)))
            arrivals.append(_remote_copy(there, there, sems, 3 * a + j, (tx, ty, c)))
    return sends, arrivals


def _reduce_copies(ins, outs, sems):
    x, y, c = _position()
    sends, arrivals = [], []
    for a in range(len(ins)):
        h = ins[a].shape[1] // 2
        for f, (fx, fy, fc) in enumerate(DEVICE_FLIPS):
            tx, ty, tc = _flip(x, fx), _flip(y, fy), _flip(c, fc)
            src = ins[a].at[2 * tx + ty, pl.ds(pl.multiple_of(tc * h, h), h)]
            sends.append(_remote_copy(src, outs[a].at[f], sems, 7 * a + f, (tx, ty, tc)))
            arrivals.append(_remote_copy(outs[a].at[f], outs[a].at[f], sems, 7 * a + f, (tx, ty, tc)))
    return sends, arrivals


def _start_copies(make):
    sends, _ = make()
    for cp in sends:
        cp.start()


def _finish_copies(make):
    sends, arrivals = make()
    for cp in arrivals:
        cp.wait_recv()
    for cp in sends:
        cp.wait_send()


def _attn_fwd(qkv, g_attn, tri, gmat, shards):
    n_w = len(shards)
    s = qkv.shape[0]
    tq = _row_tile(s, ATTN_Q_TILE)
    tk = KEY_BLOCK
    nb = ATTN_KEY_BLOCKS
    width = nb * tk
    n_groups = ATTN_DIAG_GROUPS
    group = tq // n_groups
    pairs = ATTN_WIDTH // LANES

    def body(q_ref, k_ref, v_ref, gain_ref, tri_ref, gmat_ref, *rest):
        o_ref, yn_ref, tot_ref, cut_ref = rest[n_w:n_w + 4]
        w_bufs, sems = rest[n_w + 4:2 * n_w + 4], rest[2 * n_w + 4:]
        copies = functools.partial(_gather_copies, w_bufs, sems)
        p, i = pl.program_id(0), pl.program_id(1)
        pl.when((p == 0) & (i == 0))(functools.partial(_start_copies, copies))
        q2 = q_ref[...]
        tri_v = tri_ref[...]

        def trip(s0, n_blk, rows, carry, diag_base):
            r0, nr = rows
            run = [carry[0], carry[1]]
            oacc = carry[2]
            ksel = _stack_heads(k_ref[pl.ds(s0, n_blk * tk), :], n_blk)
            vsel = _stack_heads(v_ref[pl.ds(s0, n_blk * tk), :], n_blk)
            z = lax.dot_general(q2[r0:r0 + nr], ksel, NT_DIMS, preferred_element_type=F32) * (ATTN_SCALE * LOG2_E)
            log_beta, log_keep = _softplus_terms(z)
            if diag_base is not None:
                valid = _causal_mask(nr, n_blk, diag_base)
                log_keep = jnp.where(valid, log_keep, 0.0)
            ct = jnp.dot(_stack_hilo(log_keep, 2 * n_blk), tri_v, preferred_element_type=F32)
            a_parts = [None] * (2 * n_blk)
            for c in reversed(range(2 * n_blk)):
                h = c % 2
                ct_c = ct[c * nr:(c + 1) * nr]
                a_parts[c] = jnp.exp2(log_beta[:, c * tk:(c + 1) * tk] + ct_c[:, :tk] + run[h])
                run[h] = run[h] + ct_c[:, tk:]
            a = jnp.concatenate(a_parts, axis=1)
            if diag_base is not None:
                a = jnp.where(valid, a, 0.0)
            oacc = oacc + jnp.dot(a.astype(BF16), vsel, preferred_element_type=F32)
            return run[0], run[1], oacc

        groups = []
        for g in range(n_groups):
            zeros = (jnp.zeros((group, tk), F32), jnp.zeros((group, tk), F32), jnp.zeros((group, LANES), F32))
            groups.append(trip(pl.multiple_of(i * tq, tq), (g + 1) * group // tk, (g * group, group), zeros,
                               -g * group))
        carry = tuple(jnp.concatenate([grp[j] for grp in groups], axis=0) for j in range(3))
        n_full = i * (tq // width)

        def alive(run_a, run_b):
            return jnp.max(jnp.maximum(run_a, run_b)) > -ATTN_DEAD_LOG2

        def earlier_trip(c):
            done, _, run_a, run_b, oacc = c
            s0 = pl.multiple_of((n_full - 1 - done) * width, width)
            run_a, run_b, oacc = trip(s0, nb, (0, tq), (run_a, run_b, oacc), None)
            return done + 1, alive(run_a, run_b), run_a, run_b, oacc

        done, _, run_a, run_b, oacc = lax.while_loop(
            lambda c: (c[0] < n_full) & c[1], earlier_trip, (jnp.int32(0), alive(carry[0], carry[1])) + carry)
        cut_ref[p * pl.num_programs(1) + i] = (n_full - done).astype(F32)
        lane = lax.broadcasted_iota(jnp.int32, (1, LANES), 1)
        o_ref[...] = oacc
        tot_ref[...] = jnp.where(lane < GROUP, run_a, run_b)
        ms = _group_sum(oacc * oacc, gmat_ref[...]) * (1.0 / GROUP)
        yn_ref[...] = (oacc * lax.rsqrt(ms + RMS_EPS) * gain_ref[...]).astype(BF16)
        pl.when((p == pairs - 1) & (i == pl.num_programs(1) - 1))(functools.partial(_finish_copies, copies))

    blk = lambda: pl.BlockSpec((tq, LANES), lambda p, i: (i, p))
    return pl.pallas_call(
        body, name="attn_fwd", grid=(pairs, s // tq),
        in_specs=[pl.BlockSpec((tq, LANES), lambda p, i: (i, p)),
                  pl.BlockSpec((s, LANES), lambda p, i: (0, pairs + p)),
                  pl.BlockSpec((s, LANES), lambda p, i: (0, 2 * pairs + p)),
                  pl.BlockSpec((1, LANES), lambda p, i: (0, p)),
                  pl.BlockSpec((2 * tk, 2 * tk), lambda p, i: (0, 0)),
                  pl.BlockSpec((2 * LANES, LANES), lambda p, i: (0, 0))] + [ANY] * n_w,
        out_specs=[blk(), blk(), blk(), pl.BlockSpec(memory_space=pltpu.SMEM)] + [ANY] * n_w,
        out_shape=[SDS((s, ATTN_WIDTH), F32), SDS((s, ATTN_WIDTH), BF16), SDS((s, ATTN_WIDTH), F32),
                   SDS((pairs * (s // tq),), F32)]
        + [SDS(w.shape, w.dtype) for w in shards],
        input_output_aliases={6 + a: 4 + a for a in range(n_w)},
        scratch_shapes=[pltpu.SemaphoreType.DMA((3 * n_w,)), pltpu.SemaphoreType.DMA((3 * n_w,))],
        compiler_params=_params(("arbitrary", "arbitrary")))(qkv, qkv, qkv, g_attn, tri, gmat, *shards)


def _attn_bwd(qkv, o, tot, dyn, g_attn, tri, gmat, cut, partials):
    n_g = len(partials)
    s = qkv.shape[0]
    tq = _row_tile(s, ATTN_Q_TILE)
    tk = KEY_BLOCK
    nb = ATTN_KEY_BLOCKS
    width = nb * tk
    n_groups = ATTN_DIAG_GROUPS
    group = tq // n_groups
    pairs = ATTN_WIDTH // LANES

    def body(q_ref, k_ref, v_ref, o_ref, tot_ref, dyn_ref, gain_ref, tri_ref, gmat_ref, cut_ref, *rest):
        g_ins, (dq_ref, dk_ref, dv_ref, dg_ref) = rest[:n_g], rest[n_g:n_g + 4]
        g_outs, sems = rest[n_g + 4:2 * n_g + 4], rest[2 * n_g + 4:]
        copies = functools.partial(_reduce_copies, g_ins, g_outs, sems)
        p, i = pl.program_id(0), pl.program_id(1)
        pl.when((p == 0) & (i == 0))(functools.partial(_start_copies, copies))

        @pl.when(i == 0)
        def _():
            dk_ref[...] = jnp.zeros_like(dk_ref)
            dv_ref[...] = jnp.zeros_like(dv_ref)
            dg_ref[...] = jnp.zeros_like(dg_ref)

        gmat_v = gmat_ref[...]
        o_v = o_ref[...]
        rstd = lax.rsqrt(_group_sum(o_v * o_v, gmat_v) * (1.0 / GROUP) + RMS_EPS)
        n = o_v * rstd
        dout = dyn_ref[...]
        dg_ref[0:1, :] += jnp.sum(dout * n, axis=0, keepdims=True)
        dn = dout * gain_ref[...]
        do2 = (rstd * (dn - n * (_group_sum(dn * n, gmat_v) * (1.0 / GROUP)))).astype(BF16)
        q2 = q_ref[...]
        tot_v = tot_ref[...]
        tots = (jnp.broadcast_to(tot_v[:, 0:1], (tq, tk)), jnp.broadcast_to(tot_v[:, GROUP:GROUP + 1], (tq, tk)))
        tri_v, tri_incl_v = tri_ref[0], tri_ref[1]
        lane = lax.broadcasted_iota(jnp.int32, (1, LANES), 1)

        def trip(s0, n_blk, rows, carry, diag_base):
            r0, nr = rows
            rest_l = [carry[0], carry[1]]
            pref_g = [carry[2], carry[3]]
            dq = carry[4]
            q_rows, do_rows = q2[r0:r0 + nr], do2[r0:r0 + nr]
            ksel = _stack_heads(k_ref[pl.ds(s0, n_blk * tk), :], n_blk)
            vsel = _stack_heads(v_ref[pl.ds(s0, n_blk * tk), :], n_blk)
            z = lax.dot_general(q_rows, ksel, NT_DIMS, preferred_element_type=F32) * (ATTN_SCALE * LOG2_E)
            log_beta, log_keep = _softplus_terms(z)
            if diag_base is not None:
                valid = _causal_mask(nr, n_blk, diag_base)
                log_keep = jnp.where(valid, log_keep, 0.0)
            ctl = jnp.dot(_stack_hilo(log_keep, 2 * n_blk), tri_incl_v, preferred_element_type=F32)
            da = lax.dot_general(do_rows, vsel, NT_DIMS, preferred_element_type=F32)
            a_parts = []
            for c in range(2 * n_blk):
                h = c % 2
                ct_c = ctl[c * nr:(c + 1) * nr]
                cols = slice(c * tk, (c + 1) * tk)
                a_parts.append(jnp.exp2(log_beta[:, cols] + (rest_l[h] - ct_c[:, :tk])))
                rest_l[h] = rest_l[h] - ct_c[:, tk:]
            a = jnp.concatenate(a_parts, axis=1)
            if diag_base is not None:
                a = jnp.where(valid, a, 0.0)
            g = a * da
            ctg = jnp.dot(_stack_hilo(g, 2 * n_blk), tri_v, preferred_element_type=F32)
            dz_parts = []
            for c in range(2 * n_blk):
                h = c % 2
                ct_c = ctg[c * nr:(c + 1) * nr]
                cols = slice(c * tk, (c + 1) * tk)
                prefix = pref_g[h] + ct_c[:, :tk]
                pref_g[h] = pref_g[h] + ct_c[:, tk:]
                g_c = g[:, cols]
                dz_parts.append(g_c - jnp.exp2(log_beta[:, cols]) * (g_c + prefix))
            dz = jnp.concatenate(dz_parts, axis=1) * ATTN_SCALE
            if diag_base is not None:
                dz = jnp.where(valid, dz, 0.0)
            dzb = dz.astype(BF16)
            dq = dq + jnp.dot(dzb, ksel, preferred_element_type=F32)
            dkt = lax.dot_general(dzb, q_rows, TN_DIMS, preferred_element_type=F32)
            dvt = lax.dot_general(a.astype(BF16), do_rows, TN_DIMS, preferred_element_type=F32)
            for blk in range(n_blk):
                ra, rb = slice(2 * blk * tk, (2 * blk + 1) * tk), slice((2 * blk + 1) * tk, (2 * blk + 2) * tk)
                keys = pl.ds(pl.multiple_of(s0 + blk * tk, tk), tk)
                dk_ref[keys, :] += jnp.where(lane < GROUP, dkt[ra], dkt[rb])
                dv_ref[keys, :] += jnp.where(lane < GROUP, dvt[ra], dvt[rb])
            return rest_l[0], rest_l[1], pref_g[0], pref_g[1], dq

        zeros_qk = jnp.zeros((tq, tk), F32)
        carry = (tots[0], tots[1], zeros_qk, zeros_qk, jnp.zeros((tq, LANES), F32))
        n_full = i * (tq // width)
        first = jnp.clip(cut_ref[p * pl.num_programs(1) + i].astype(jnp.int32), 0, n_full)
        carry = lax.fori_loop(
            first, n_full, lambda t, c: trip(pl.multiple_of(t * width, width), nb, (0, tq), c, None), carry)
        dq_groups = []
        for g in range(n_groups):
            sub = tuple(x[g * group:(g + 1) * group] for x in carry)
            dq_groups.append(trip(pl.multiple_of(i * tq, tq), (g + 1) * group // tk, (g * group, group), sub,
                                  -g * group)[4])
        dq_ref[...] = jnp.concatenate(dq_groups, axis=0).astype(BF16)
        pl.when((p == pairs - 1) & (i == pl.num_programs(1) - 1))(functools.partial(_finish_copies, copies))

    blk = lambda: pl.BlockSpec((tq, LANES), lambda p, i: (i, p))
    col = lambda: pl.BlockSpec((s, LANES), lambda p, i: (0, p))
    n_peers = len(DEVICE_FLIPS)
    return pl.pallas_call(
        body, name="attn_bwd", grid=(pairs, s // tq),
        in_specs=[pl.BlockSpec((tq, LANES), lambda p, i: (i, p)),
                  pl.BlockSpec((s, LANES), lambda p, i: (0, pairs + p)),
                  pl.BlockSpec((s, LANES), lambda p, i: (0, 2 * pairs + p)),
                  blk(), blk(), blk(),
                  pl.BlockSpec((1, LANES), lambda p, i: (0, p)),
                  pl.BlockSpec((2, 2 * tk, 2 * tk), lambda p, i: (0, 0, 0)),
                  pl.BlockSpec((2 * LANES, LANES), lambda p, i: (0, 0)),
                  pl.BlockSpec(memory_space=pltpu.SMEM)] + [ANY] * n_g,
        out_specs=[blk(), col(), col(), pl.BlockSpec((SUBLANES, LANES), lambda p, i: (0, p))] + [ANY] * n_g,
        out_shape=[SDS((s, ATTN_WIDTH), BF16), SDS((s, ATTN_WIDTH), F32), SDS((s, ATTN_WIDTH), F32),
                   SDS((SUBLANES, ATTN_WIDTH), F32)]
        + [SDS((n_peers, g.shape[1] // 2, g.shape[2]), g.dtype) for g in partials],
        scratch_shapes=[pltpu.SemaphoreType.DMA((n_peers * n_g,)), pltpu.SemaphoreType.DMA((n_peers * n_g,))],
        compiler_params=_params(("arbitrary", "arbitrary")))(
            qkv, qkv, qkv, o, tot, dyn, g_attn, tri, gmat, cut, *partials)


def _mix_ln1(ycn, yan, w_out, x, g, b):
    s = x.shape[0]
    tm = _row_tile(s, 512)

    def body(yc_ref, ya_ref, w_ref, x_ref, g_ref, b_ref, x1_ref, xhat_ref, rstd_ref):
        mix = jnp.dot(yc_ref[...], w_ref[0:CONV_WIDTH, :], preferred_element_type=F32)
        mix = mix + jnp.dot(ya_ref[...], w_ref[CONV_WIDTH:, :], preferred_element_type=F32)
        x1, xhat, rstd = _layer_norm_fwd(ALPHA * x_ref[...] + mix, g_ref[...], b_ref[...])
        x1_ref[...] = x1
        xhat_ref[...] = xhat
        rstd_ref[...] = rstd

    row = lambda w: pl.BlockSpec((tm, w), lambda i: (i, 0))
    vec = lambda: pl.BlockSpec((1, D_MODEL), lambda i: (0, 0))
    return pl.pallas_call(
        body, name="mix_ln1", grid=(s // tm,),
        in_specs=[row(CONV_WIDTH), row(ATTN_WIDTH), pl.BlockSpec((D_MODEL, D_MODEL), lambda i: (0, 0)),
                  row(D_MODEL), vec(), vec()],
        out_specs=[row(D_MODEL), row(D_MODEL), row(1)],
        out_shape=[SDS((s, D_MODEL), F32), SDS((s, D_MODEL), F32), SDS((s, 1), F32)],
        compiler_params=_params(("parallel",)))(ycn, yan, w_out, x, g, b)


def _mlp_fwd_loss(x1, w_up, w_down, target, g, b):
    s = x1.shape[0]
    tm = _row_tile(s, 256)

    def body(x1_ref, wu_ref, wd_ref, t_ref, g_ref, b_ref, dpre_ref, sums_ref, loss_ref, r_ref, hid_ref):
        i = pl.program_id(0)
        x1_v = x1_ref[...]
        xb = x1_v.astype(BF16)
        ffn = jnp.zeros((tm, D_MODEL), F32)
        for k in range(N_CHIPS):
            r = jnp.maximum(jnp.dot(xb, wu_ref[k], preferred_element_type=F32), 0.0)
            hid = (r * r).astype(BF16)
            r_ref[:, FF_SHARD * k:FF_SHARD * (k + 1)] = r.astype(BF16)
            hid_ref[:, FF_SHARD * k:FF_SHARD * (k + 1)] = hid
            ffn = ffn + jnp.dot(hid, wd_ref[k], preferred_element_type=F32)
        g_v = g_ref[...]
        x2, xhat, rstd = _layer_norm_fwd(ALPHA * x1_v + ffn, g_v, b_ref[...])
        err = x2 - t_ref[...]
        dx2 = err * (1.0 / D_MODEL)
        dpre_ref[...] = _layer_norm_bwd(dx2, xhat, rstd, g_v)

        @pl.when(i == 0)
        def _():
            sums_ref[...] = jnp.zeros_like(sums_ref)
            loss_ref[...] = jnp.zeros_like(loss_ref)

        sums_ref[0:1, :] += jnp.sum(dx2 * xhat, axis=0, keepdims=True)
        sums_ref[1:2, :] += jnp.sum(dx2, axis=0, keepdims=True)
        loss_ref[...] += jnp.sum(jnp.sum(err * err, axis=1, keepdims=True), axis=0, keepdims=True) * (0.5 / D_MODEL)

    row = lambda: pl.BlockSpec((tm, D_MODEL), lambda i: (i, 0))
    wide = lambda: pl.BlockSpec((tm, D_FF), lambda i: (i, 0))
    vec = lambda: pl.BlockSpec((1, D_MODEL), lambda i: (0, 0))
    return pl.pallas_call(
        body, name="mlp_fwd_loss", grid=(s // tm,),
        in_specs=[row(), _resident_weight(), _resident_weight(), row(), vec(), vec()],
        out_specs=[row(), pl.BlockSpec((SUBLANES, D_MODEL), lambda i: (0, 0)),
                   pl.BlockSpec((SUBLANES, LANES), lambda i: (0, 0)), wide(), wide()],
        out_shape=[SDS((s, D_MODEL), F32), SDS((SUBLANES, D_MODEL), F32), SDS((SUBLANES, LANES), F32),
                   SDS((s, D_FF), BF16), SDS((s, D_FF), BF16)],
        compiler_params=_params(("arbitrary",)))(x1, w_up, w_down, target, g, b)


def _resident_weight():
    return pl.BlockSpec((N_CHIPS, D_MODEL, FF_SHARD), lambda i: (0, 0, 0), pipeline_mode=pl.Buffered(1))


def _mlp_bwd_ln1(relu_up, dpre2, w_up, w_down, xhat1, rstd1, g1):
    s = dpre2.shape[0]
    tm = _row_tile(s, 256)

    def body(r_ref, d2_ref, wu_ref, wd_ref, xh_ref, rs_ref, g_ref, dup_ref, dpre_ref, sums_ref):
        i = pl.program_id(0)
        d2 = d2_ref[...]
        d2b = d2.astype(BF16)
        dx1 = ALPHA * d2
        for k in range(N_CHIPS):
            r = r_ref[:, FF_SHARD * k:FF_SHARD * (k + 1)].astype(F32)
            dhid = lax.dot_general(d2b, wd_ref[k], NT_DIMS, preferred_element_type=F32)
            dupb = (dhid * (2.0 * r)).astype(BF16)
            dup_ref[:, FF_SHARD * k:FF_SHARD * (k + 1)] = dupb
            dx1 = dx1 + lax.dot_general(dupb, wu_ref[k], NT_DIMS, preferred_element_type=F32)
        xhat = xh_ref[...]
        dpre_ref[...] = _layer_norm_bwd(dx1, xhat, rs_ref[...], g_ref[...])

        @pl.when(i == 0)
        def _():
            sums_ref[...] = jnp.zeros_like(sums_ref)

        sums_ref[0:1, :] += jnp.sum(dx1 * xhat, axis=0, keepdims=True)
        sums_ref[1:2, :] += jnp.sum(dx1, axis=0, keepdims=True)

    row = lambda w: pl.BlockSpec((tm, w), lambda i: (i, 0))
    return pl.pallas_call(
        body, name="mlp_bwd_ln1", grid=(s // tm,),
        in_specs=[row(D_FF), row(D_MODEL), _resident_weight(), _resident_weight(), row(D_MODEL), row(1),
                  pl.BlockSpec((1, D_MODEL), lambda i: (0, 0))],
        out_specs=[row(D_FF), row(D_MODEL), pl.BlockSpec((SUBLANES, D_MODEL), lambda i: (0, 0))],
        out_shape=[SDS((s, D_FF), BF16), SDS((s, D_MODEL), F32), SDS((SUBLANES, D_MODEL), F32)],
        compiler_params=_params(("arbitrary",)))(relu_up, dpre2, w_up, w_down, xhat1, rstd1, g1)


def _grad_tn(a, b, name, out_cols, stacked):
    s, ka = a.shape
    n = b.shape[1]
    ts = _row_tile(s, 2048)
    n_steps = s // ts
    if stacked:
        tka, tn = ka, out_cols
        grid = (1, n // tn, n_steps)
        shape = (n // tn, ka, tn)
        out_spec = lambda: pl.BlockSpec((None, tka, tn), lambda r, c, t: (c, 0, 0))
    else:
        tka, tn = min(ka, 1024), n
        grid = (ka // tka, 1, n_steps)
        shape = (ka, n)
        out_spec = lambda: pl.BlockSpec((tka, tn), lambda r, c, t: (r, 0))

    def body(a_ref, b_ref, o_ref, ob_ref):
        t = pl.program_id(2)

        @pl.when(t == 0)
        def _():
            o_ref[...] = jnp.zeros_like(o_ref)

        o_ref[...] += lax.dot_general(a_ref[...].astype(BF16), b_ref[...].astype(BF16), TN_DIMS,
                                      preferred_element_type=F32)

        @pl.when(t == n_steps - 1)
        def _():
            ob_ref[...] = o_ref[...].astype(BF16)

    return pl.pallas_call(
        body, name=name, grid=grid,
        in_specs=[pl.BlockSpec((ts, tka), lambda r, c, t: (t, r)),
                  pl.BlockSpec((ts, tn), lambda r, c, t: (t, c))],
        out_specs=[out_spec(), out_spec()], out_shape=[SDS(shape, F32), SDS(shape, BF16)],
        compiler_params=_params(("parallel", "parallel", "arbitrary")))(a, b)


def _dmix(dpre1, w_out):
    s = dpre1.shape[0]
    tm = _row_tile(s, 512)

    def body(d_ref, w_ref, dc_ref, da_ref):
        db = d_ref[...].astype(BF16)
        dc_ref[...] = lax.dot_general(db, w_ref[0:CONV_WIDTH, :], NT_DIMS, preferred_element_type=F32)
        da_ref[...] = lax.dot_general(db, w_ref[CONV_WIDTH:, :], NT_DIMS, preferred_element_type=F32)

    return pl.pallas_call(
        body, name="dmix", grid=(s // tm,),
        in_specs=[pl.BlockSpec((tm, D_MODEL), lambda i: (i, 0)),
                  pl.BlockSpec((D_MODEL, D_MODEL), lambda i: (0, 0))],
        out_specs=[pl.BlockSpec((tm, CONV_WIDTH), lambda i: (i, 0)),
                   pl.BlockSpec((tm, ATTN_WIDTH), lambda i: (i, 0))],
        out_shape=[SDS((s, CONV_WIDTH), F32), SDS((s, ATTN_WIDTH), F32)],
        compiler_params=_params(("parallel",)))(dpre1, w_out)


def _sum_with_peers(own_ref, r_ref, o_ref):
    acc = own_ref[...]
    for f in range(r_ref.shape[0]):
        acc = acc + r_ref[f].astype(F32)
    o_ref[...] = acc


def _grad_x(kc_idx, dproj, w_in, dpre1, partial, earlier):
    s = dproj.shape[0]
    tm = _row_tile(s, 512)
    steps = s // tm
    n_peers = len(DEVICE_FLIPS)
    n_e = len(earlier)

    def body(kc_ref, dp_ref, w_ref, d1_ref, *rest):
        sum_ins, g_in = rest[:2 * n_e], rest[2 * n_e]
        o_ref, g_out = rest[2 * n_e + 1], rest[2 * n_e + 2]
        sum_outs, sems = rest[2 * n_e + 3:3 * n_e + 3], rest[3 * n_e + 3:]
        copies = functools.partial(_reduce_copies, [g_in], [g_out], sems)
        i = pl.program_id(0)
        pl.when(i == 0)(functools.partial(_start_copies, copies))
        acc = ALPHA * d1_ref[...]
        for k in range(N_CHIPS):
            acc = acc + lax.dot_general(dp_ref[:, IN_SHARD * k:IN_SHARD * (k + 1)], w_ref[k], NT_DIMS,
                                        preferred_element_type=F32)
        o_ref[...] = acc
        for a in range(n_e):
            _sum_with_peers(sum_ins[2 * a], sum_ins[2 * a + 1], sum_outs[a])
        pl.when(i == steps - 1)(functools.partial(_finish_copies, copies))

    in_specs = [pl.BlockSpec((tm, IN_COLS), lambda i, kc: (i, 0)),
                pl.BlockSpec((N_CHIPS, D_MODEL, IN_SHARD), lambda i, kc: (0, 0, 0)),
                pl.BlockSpec((tm, D_MODEL), lambda i, kc: (i, 0))]
    out_specs = [pl.BlockSpec((tm, D_MODEL), lambda i, kc: (i, 0)), ANY]
    out_shape = [SDS((s, D_MODEL), F32), SDS((n_peers, partial.shape[1] // 2, partial.shape[2]), partial.dtype)]
    operands = []
    for own, recv in earlier:
        _, _, h, cols = own.shape
        th = h // steps
        in_specs.append(pl.BlockSpec((None, None, th, cols), lambda i, kc: (kc[0], kc[1], i, 0)))
        in_specs.append(pl.BlockSpec((n_peers, th, cols), lambda i, kc: (0, i, 0)))
        out_specs.append(pl.BlockSpec((th, cols), lambda i, kc: (kc[1] * steps + i, 0)))
        out_shape.append(SDS((2 * h, cols), F32))
        operands += [own, recv]
    grid_spec = pltpu.PrefetchScalarGridSpec(
        num_scalar_prefetch=1, grid=(steps,), in_specs=in_specs + [ANY], out_specs=out_specs,
        scratch_shapes=[pltpu.SemaphoreType.DMA((n_peers,)), pltpu.SemaphoreType.DMA((n_peers,))])
    return pl.pallas_call(
        body, name="grad_x", grid_spec=grid_spec, out_shape=out_shape,
        compiler_params=_params(("arbitrary",)))(kc_idx, dproj, w_in, dpre1, *operands, partial)


def _adamw(w, g, m, v, name):
    r, c = w.shape
    tr = _row_tile(r, 256)

    def body(w_ref, g_ref, m_ref, v_ref, d_ref, nm_ref, nv_ref):
        g_v = g_ref[...]
        nm = ADAM_B1 * m_ref[...] + (1.0 - ADAM_B1) * g_v
        nv = ADAM_B2 * v_ref[...] + (1.0 - ADAM_B2) * (g_v * g_v)
        m_hat = nm / (1.0 - ADAM_B1 ** ADAM_STEP)
        v_hat = nv / (1.0 - ADAM_B2 ** ADAM_STEP)
        d_ref[...] = -ADAM_LR * (m_hat / (jnp.sqrt(v_hat) + ADAM_EPS) + ADAM_WD * w_ref[...])
        nm_ref[...] = nm
        nv_ref[...] = nv

    spec = lambda: pl.BlockSpec((tr, c), lambda i: (i, 0))
    return pl.pallas_call(
        body, name=name, grid=(r // tr,),
        in_specs=[spec(), spec(), spec(), spec()], out_specs=[spec(), spec(), spec()],
        out_shape=[SDS((r, c), F32)] * 3, compiler_params=_params(("parallel",)))(w, g, m, v)


def _sum_partials(kc_idx, grad, recv, name):
    _, _, h, cols = grad.shape
    th = _row_tile(h, 128)
    n_peers = recv.shape[0]

    def body(kc_ref, own_ref, r_ref, o_ref):
        _sum_with_peers(own_ref, r_ref, o_ref)

    grid_spec = pltpu.PrefetchScalarGridSpec(
        num_scalar_prefetch=1, grid=(h // th,),
        in_specs=[pl.BlockSpec((None, None, th, cols), lambda t, kc: (kc[0], kc[1], t, 0)),
                  pl.BlockSpec((n_peers, th, cols), lambda t, kc: (0, t, 0))],
        out_specs=pl.BlockSpec((th, cols), lambda t, kc: (kc[1] * (h // th) + t, 0)))
    return pl.pallas_call(
        body, name=name, grid_spec=grid_spec, out_shape=SDS((2 * h, cols), F32),
        compiler_params=_params(("parallel",)))(kc_idx, grad, recv)


def _gather_weights(shards, conv_shard):
    n = len(shards)

    def body(*refs):
        outs, conv_out = refs[n + 1:2 * n + 1], refs[2 * n + 1]
        send_sems, recv_sems = refs[2 * n + 2:]
        x, y, c = _position()
        k = 2 * x + y
        sibling = (x, y, 1 - c)
        chips = [(_flip(x, fx), _flip(y, fy)) for fx, fy in CHIP_FLIPS]

        def half(a, rows_of_core):
            h = shards[a].shape[1] // 2
            return pl.ds(pl.multiple_of(rows_of_core * h, h), h)

        def remote(src, dst, idx, target):
            return pltpu.make_async_remote_copy(src_ref=src, dst_ref=dst, send_sem=send_sems.at[idx],
                                                recv_sem=recv_sems.at[idx], device_id=target,
                                                device_id_type=MESH)

        started = []
        for a in range(n):
            mine = outs[a].at[k, half(a, c)]
            for j, (tx, ty) in enumerate(chips):
                cp = remote(mine, mine, 6 * a + j, (tx, ty, c))
                cp.start()
                started.append(cp)
        for j, (tx, ty) in enumerate(chips):
            cp = remote(conv_out.at[k], conv_out.at[k], 6 * n + j, (tx, ty, c))
            cp.start()
            started.append(cp)
        for a in range(n):
            for j, (tx, ty) in enumerate(chips):
                kj = 2 * tx + ty
                landed = outs[a].at[kj, half(a, c)]
                remote(landed, landed, 6 * a + j, sibling).wait_recv()
                cp = remote(landed, landed, 6 * a + 3 + j, sibling)
                cp.start()
                started.append(cp)
        for a in range(n):
            for j, (tx, ty) in enumerate(chips):
                kj = 2 * tx + ty
                other = outs[a].at[kj, half(a, 1 - c)]
                remote(other, other, 6 * a + 3 + j, sibling).wait_recv()
        for j, (tx, ty) in enumerate(chips):
            kj = 2 * tx + ty
            remote(conv_out.at[kj], conv_out.at[kj], 6 * n + j, sibling).wait_recv()
        for cp in started:
            cp.wait_send()

    out_shape = [SDS(w.shape, w.dtype) for w in shards] + [SDS(conv_shard.shape, conv_shard.dtype)]
    n_sems = 6 * n + 3
    return pl.pallas_call(
        body, name="gather_weights", in_specs=[ANY] * (n + 1), out_specs=[ANY] * (n + 1),
        out_shape=out_shape, input_output_aliases={a: a for a in range(n + 1)},
        scratch_shapes=[pltpu.SemaphoreType.DMA((n_sems,)), pltpu.SemaphoreType.DMA((n_sems,))])(
            *shards, conv_shard)


def _finish_exchange(pieces, vec):
    n = len(pieces)
    n_dev = 2 * N_CHIPS

    def body(*refs):
        v_ref = refs[n]
        outs, o_ref = refs[n + 1:2 * n + 1], refs[2 * n + 1]
        buf, send_sems, recv_sems = refs[2 * n + 2:]
        x, y, c = _position()
        sibling = (x, y, 1 - c)
        me = 4 * x + 2 * y + c
        buf[me] = v_ref[...]
        started = []
        for f, (fx, fy, fc) in enumerate(DEVICE_FLIPS):
            cp = pltpu.make_async_remote_copy(
                src_ref=v_ref, dst_ref=buf.at[me], send_sem=send_sems.at[n + f], recv_sem=recv_sems.at[n + f],
                device_id=(_flip(x, fx), _flip(y, fy), _flip(c, fc)), device_id_type=MESH)
            cp.start()
            started.append(cp)
        for a in range(n):
            h = pieces[a].shape[0] // 2
            mine = outs[a].at[pl.ds(pl.multiple_of(c * h, h), h)]
            cp = pltpu.make_async_remote_copy(
                src_ref=mine, dst_ref=mine, send_sem=send_sems.at[a], recv_sem=recv_sems.at[a],
                device_id=sibling, device_id_type=MESH)
            cp.start()
            started.append(cp)
        for a in range(n):
            h = pieces[a].shape[0] // 2
            theirs = outs[a].at[pl.ds(pl.multiple_of((1 - c) * h, h), h)]
            pltpu.make_async_remote_copy(
                src_ref=theirs, dst_ref=theirs, send_sem=send_sems.at[a], recv_sem=recv_sems.at[a],
                device_id=sibling, device_id_type=MESH).wait_recv()
        for f, (fx, fy, fc) in enumerate(DEVICE_FLIPS):
            src = 4 * _flip(x, fx) + 2 * _flip(y, fy) + _flip(c, fc)
            pltpu.make_async_remote_copy(
                src_ref=v_ref, dst_ref=buf.at[src], send_sem=send_sems.at[n + f], recv_sem=recv_sems.at[n + f],
                device_id=(x, y, c), device_id_type=MESH).wait_recv()
        for cp in started:
            cp.wait_send()
        acc = buf[0]
        for d in range(1, n_dev):
            acc = acc + buf[d]
        o_ref[...] = acc

    vmem = pl.BlockSpec(memory_space=pltpu.VMEM)
    out_shape = [SDS(p.shape, p.dtype) for p in pieces] + [SDS(vec.shape, vec.dtype)]
    n_sems = n + n_dev - 1
    return pl.pallas_call(
        body, name="finish_exchange", in_specs=[ANY] * n + [vmem], out_specs=[ANY] * n + [vmem],
        out_shape=out_shape, input_output_aliases={a: a for a in range(n)},
        scratch_shapes=[pltpu.VMEM((n_dev,) + vec.shape, vec.dtype), pltpu.SemaphoreType.DMA((n_sems,)),
                        pltpu.SemaphoreType.DMA((n_sems,))])(*pieces, vec)


def _constants():
    r = jnp.arange(2 * KEY_BLOCK)[:, None] % KEY_BLOCK
    c = jnp.arange(2 * KEY_BLOCK)[None, :]
    later = jnp.where(c < KEY_BLOCK, r > c, True).astype(BF16)
    earlier = jnp.where(c < KEY_BLOCK, r < c, True).astype(BF16)
    upto = jnp.where(c < KEY_BLOCK, r <= c, True).astype(BF16)
    gr = (jnp.arange(2 * LANES)[:, None] % LANES) // GROUP
    gc = jnp.arange(LANES)[None, :] // GROUP
    gmat = (gr == gc).astype(BF16)
    return later, jnp.stack([earlier, upto]), gmat


def _rows(v):
    return v.reshape(-1, LANES)


def kernel(x, w_in, conv_w, g_conv, g_attn, w_out, ln1_g, ln1_b, w_up, w_down, ln2_g, ln2_b, loss_target, m_w_in, m_conv_w, m_g_conv, m_g_attn, m_w_out, m_ln1_g, m_ln1_b, m_w_up, m_w_down, m_ln2_g, m_ln2_b, v_w_in, v_conv_w, v_g_conv, v_g_attn, v_w_out, v_ln1_g, v_ln1_b, v_w_up, v_w_down, v_ln2_g, v_ln2_b):
    xs, target = x[0], loss_target[0]
    mesh_x, mesh_y, mesh_c = _position()
    k_idx = 2 * mesh_x + mesh_y
    kc_idx = jnp.stack([k_idx, mesh_c]).astype(jnp.int32)
    tri_later, tri_earlier, gmat = _constants()

    w_in_b, w_out_b, w_up_b, w_down_b = [
        _cast_into_slot(kc_idx, w[0], "cast_" + nm)
        for w, nm in ((w_in, "w_in"), (w_out, "w_out"), (w_up, "w_up"), (w_down, "w_down"))]
    conv_slot = jnp.pad(conv_w, ((0, 0), (0, SUBLANES - conv_w.shape[1]), (0, 0)))
    conv_b = lax.dynamic_update_slice(jnp.zeros((N_CHIPS, SUBLANES, LANES), F32), conv_slot, (k_idx, 0, 0))
    w_in_f, conv_f = _gather_weights([w_in_b], conv_b)
    taps = jnp.transpose(conv_f, (1, 0, 2)).reshape(SUBLANES, CONV_WIDTH)

    gates, qkv = _proj(xs, w_in_f)
    ycn = _conv_fwd(gates, taps, g_conv, gmat)
    o, yan, tot, cut, w_out_f, w_up_f, w_down_f = _attn_fwd(
        qkv, g_attn, tri_later, gmat, [w_out_b, w_up_b, w_down_b])
    w_out_f = w_out_f.reshape(D_MODEL, D_MODEL)
    x1, xhat1, rstd1 = _mix_ln1(ycn, yan, w_out_f, xs, ln1_g, ln1_b)
    dpre2, ln2_sums, loss_sum, relu_up, hid = _mlp_fwd_loss(x1, w_up_f, w_down_f, target, ln2_g, ln2_b)

    dup, dpre1, ln1_sums = _mlp_bwd_ln1(relu_up, dpre2, w_up_f, w_down_f, xhat1, rstd1, ln1_g)
    gw_up = _grad_tn(x1, dup, "grad_w_up", FF_SHARD, True)
    gw_down = [g.reshape(N_CHIPS, FF_SHARD, D_MODEL) for g in _grad_tn(hid, dpre2, "grad_w_down", D_MODEL, False)]
    gw_out_conv = _grad_tn(ycn, dpre1, "grad_w_out_conv", D_MODEL, False)
    gw_out_attn = _grad_tn(yan, dpre1, "grad_w_out_attn", D_MODEL, False)
    gw_out = [jnp.concatenate([gc_, ga_], axis=0).reshape(N_CHIPS, D_MODEL // N_CHIPS, D_MODEL)
              for gc_, ga_ in zip(gw_out_conv, gw_out_attn)]
    dycn, dyan = _dmix(dpre1, w_out_f)
    dq, dk, dv, gattn_sums, recv_out, recv_up, recv_down = _attn_bwd(
        qkv, o, tot, dyan, g_attn, tri_earlier, gmat, cut, [gw_out[1], gw_up[1], gw_down[1]])
    dbg, dy, conv_sums = _conv_bwd_gate(gates, dycn, taps, g_conv, gmat)
    dproj = _dproj_assemble(gates, dy, dbg, dq, dk, dv, taps)
    gw_in = _grad_tn(xs, dproj, "grad_w_in", IN_SHARD, True)
    halves = lambda g: g.reshape(N_CHIPS, 2, g.shape[1] // 2, g.shape[2])
    grad_x, recv_in, p_out, p_up, p_down = _grad_x(
        kc_idx, dproj, w_in_f, dpre1, gw_in[1],
        [(halves(gw_out[0]), recv_out), (halves(gw_up[0]), recv_up), (halves(gw_down[0]), recv_down)])
    pieces = [_sum_partials(kc_idx, halves(gw_in[0]), recv_in, "sum_partials_w_in"), p_out, p_up, p_down]
    conv_rows = jnp.transpose(conv_sums[0:3].reshape(3, N_CHIPS, LANES), (1, 0, 2)).reshape(3 * N_CHIPS, LANES)
    small = jnp.concatenate([
        loss_sum, _rows(conv_sums[3]), _rows(gattn_sums[0]), _rows(ln1_sums[0]), _rows(ln1_sums[1]),
        _rows(ln2_sums[0]), _rows(ln2_sums[1]), conv_rows,
        jnp.zeros((SMALL_ROWS - ROW_CONVW - 3 * N_CHIPS, LANES), F32)], axis=0)
    g_w_in, g_w_out, g_w_up, g_w_down, total = _finish_exchange(pieces, small)
    loss = total[ROW_LOSS, 0]
    g_conv_w = lax.dynamic_slice(total, (ROW_CONVW + 3 * k_idx, 0), (3, LANES))

    def pack(gc_, ga_, l1g, l1b, l2g, l2b, cw):
        return jnp.concatenate([_rows(gc_), _rows(ga_), _rows(l1g), _rows(l1b), _rows(l2g), _rows(l2b), cw[0],
                                jnp.zeros((PARAM_ROWS + SUBLANES - ROW_CONVW - 3, LANES), F32)], axis=0)

    small_w = pack(g_conv, g_attn, ln1_g, ln1_b, ln2_g, ln2_b, conv_w)
    small_m = pack(m_g_conv, m_g_attn, m_ln1_g, m_ln1_b, m_ln2_g, m_ln2_b, m_conv_w)
    small_v = pack(v_g_conv, v_g_attn, v_ln1_g, v_ln1_b, v_ln2_g, v_ln2_b, v_conv_w)
    small_g = jnp.concatenate([total[ROW_GCONV:ROW_CONVW], g_conv_w,
                               jnp.zeros((PARAM_ROWS + SUBLANES - ROW_CONVW - 3, LANES), F32)], axis=0)
    small_out = _adamw(small_w, small_g, small_m, small_v, "adamw_small")

    def unpack(p):
        off = ROW_GCONV
        vec = lambda a, b: p[a - off:b - off].reshape(1, -1)
        return {"g_conv": vec(ROW_GCONV, ROW_GATTN), "g_attn": vec(ROW_GATTN, ROW_LN1G),
                "ln1_g": vec(ROW_LN1G, ROW_LN1B), "ln1_b": vec(ROW_LN1B, ROW_LN2G),
                "ln2_g": vec(ROW_LN2G, ROW_LN2B), "ln2_b": vec(ROW_LN2B, ROW_CONVW),
                "conv_w": p[ROW_CONVW - off:ROW_CONVW - off + 3][None]}

    big_out = {
        "w_in": _adamw(w_in[0], g_w_in, m_w_in[0], v_w_in[0], "adamw_w_in"),
        "w_out": _adamw(w_out[0], g_w_out, m_w_out[0], v_w_out[0], "adamw_w_out"),
        "w_up": _adamw(w_up[0], g_w_up, m_w_up[0], v_w_up[0], "adamw_w_up"),
        "w_down": _adamw(w_down[0], g_w_down, m_w_down[0], v_w_down[0], "adamw_w_down"),
    }
    big_grads = {"w_in": g_w_in, "w_out": g_w_out, "w_up": g_w_up, "w_down": g_w_down}
    order = ["w_in", "conv_w", "g_conv", "g_attn", "w_out", "ln1_g", "ln1_b", "w_up", "w_down", "ln2_g", "ln2_b"]
    small_parts = [unpack(small_g)] + [unpack(p) for p in small_out]

    def leaf(kind, name):
        if name in big_out:
            return (big_grads[name] if kind == 0 else big_out[name][kind - 1])[None]
        return small_parts[kind][name]

    outs = [loss, grad_x[None]]
    for kind in range(4):
        outs.extend(leaf(kind, name) for name in order)
    return tuple(outs)
```

```python
import functools

import jax
import jax.numpy as jnp
from jax import lax
from jax.experimental import pallas as pl
from jax.experimental.pallas import tpu as pltpu

F32 = jnp.float32
BF16 = jnp.bfloat16
SDS = jax.ShapeDtypeStruct

D_MODEL = 1024
CONV_WIDTH = 512
ATTN_WIDTH = 512
GROUP = 64
GATE_COLS = 3 * CONV_WIDTH
QKV_COLS = 3 * ATTN_WIDTH
IN_COLS = GATE_COLS + QKV_COLS
D_FF = 4 * D_MODEL
N_CHIPS = 4
IN_SHARD = IN_COLS // N_CHIPS
FF_SHARD = D_FF // N_CHIPS
ALPHA = float(2.0 ** 0.25)
LN_EPS = 1e-5
RMS_EPS = 1e-6
ATTN_SCALE = GROUP ** -0.5
LOG2_E = 1.4426950408889634
ADAM_LR = 0.001
ADAM_B1 = 0.9
ADAM_B2 = 0.999
ADAM_EPS = 1e-08
ADAM_WD = 0.01
ADAM_STEP = 10

LANES = 128
SUBLANES = 8
KEY_BLOCK = 128
ATTN_Q_TILE = 512
ATTN_KEY_BLOCKS = 2
ATTN_DIAG_GROUPS = 2
ATTN_DEAD_LOG2 = 200.0
VMEM_LIMIT = 56 * 1024 * 1024

MESH = pl.DeviceIdType.MESH
CHIP_FLIPS = ((1, 0), (0, 1), (1, 1))
DEVICE_FLIPS = tuple((fx, fy, fc) for fx in (0, 1) for fy in (0, 1) for fc in (0, 1))[1:]
NT_DIMS = (((1,), (1,)), ((), ()))
TN_DIMS = (((0,), (0,)), ((), ()))

ROW_LOSS = 0
ROW_GCONV = 8
ROW_GATTN = 12
ROW_LN1G = 16
ROW_LN1B = 24
ROW_LN2G = 32
ROW_LN2B = 40
ROW_CONVW = 48
SMALL_ROWS = 64
PARAM_ROWS = 48


def _params(sem=None):
    return pltpu.CompilerParams(dimension_semantics=sem, vmem_limit_bytes=VMEM_LIMIT)


def _flip(v, f):
    return 1 - v if f else v


def _position():
    return lax.axis_index("x"), lax.axis_index("y"), lax.axis_index("c")


def _hilo(v):
    hi = v.astype(BF16)
    lo = (v - hi.astype(F32)).astype(BF16)
    return jnp.concatenate([hi, lo], axis=1)


def _hilo_dot(v, mat):
    return jnp.dot(_hilo(v), mat, preferred_element_type=F32)


def _group_sum(v, gmat):
    parts = [_hilo_dot(v[:, LANES * j:LANES * (j + 1)], gmat) for j in range(v.shape[1] // LANES)]
    return parts[0] if len(parts) == 1 else jnp.concatenate(parts, axis=1)


def _softplus_terms(z):
    sp = jnp.log2(1.0 + jnp.exp2(-jnp.abs(z)))
    log_beta = jnp.minimum(z, 0.0) - sp
    return log_beta, log_beta - z


def _layer_norm_fwd(pre, g, b):
    mu = jnp.mean(pre, axis=-1, keepdims=True)
    d = pre - mu
    var = jnp.mean(d * d, axis=-1, keepdims=True)
    rstd = lax.rsqrt(var + LN_EPS)
    xhat = d * rstd
    return xhat * g + b, xhat, rstd


def _layer_norm_bwd(dy, xhat, rstd, g):
    dxh = dy * g
    m1 = jnp.mean(dxh, axis=-1, keepdims=True)
    m2 = jnp.mean(dxh * xhat, axis=-1, keepdims=True)
    return rstd * (dxh - m1 - xhat * m2)


def _row_tile(s, want):
    return min(s, want)


def _cast_into_slot(kc_idx, w, name):
    r, c = w.shape
    tr = _row_tile(r, 256)

    def body(kc_ref, w_ref, o_ref):
        o_ref[...] = w_ref[...].astype(BF16)

    grid_spec = pltpu.PrefetchScalarGridSpec(
        num_scalar_prefetch=1, grid=(r // tr,),
        in_specs=[pl.BlockSpec((tr, c), lambda i, kc: (i, 0))],
        out_specs=pl.BlockSpec((None, tr, c), lambda i, kc: (kc[0], i, 0)))
    return pl.pallas_call(
        body, name=name, grid_spec=grid_spec, out_shape=SDS((N_CHIPS, r, c), BF16),
        compiler_params=_params(("parallel",)))(kc_idx, w)


def _proj(x, w_in):
    s = x.shape[0]
    tm = _row_tile(s, 512)

    def body(x_ref, w_ref, gates_ref, qkv_ref):
        xb = x_ref[...].astype(BF16)
        for k in range(N_CHIPS):
            acc = jnp.dot(xb, w_ref[k], preferred_element_type=F32)
            if k < 2:
                gates_ref[:, IN_SHARD * k:IN_SHARD * (k + 1)] = acc
            else:
                qkv_ref[:, IN_SHARD * (k - 2):IN_SHARD * (k - 1)] = acc.astype(BF16)

    return pl.pallas_call(
        body, name="proj", grid=(s // tm,),
        in_specs=[pl.BlockSpec((tm, D_MODEL), lambda i: (i, 0)),
                  pl.BlockSpec((N_CHIPS, D_MODEL, IN_SHARD), lambda i: (0, 0, 0))],
        out_specs=[pl.BlockSpec((tm, GATE_COLS), lambda i: (i, 0)),
                   pl.BlockSpec((tm, QKV_COLS), lambda i: (i, 0))],
        out_shape=[SDS((s, GATE_COLS), F32), SDS((s, QKV_COLS), BF16)],
        compiler_params=_params(("parallel",)))(x, w_in)


def _conv_forward_values(g_ref, halo_ref, taps_ref, first_block):
    gates = g_ref[...]
    tr = gates.shape[0]
    bg = gates[:, :CONV_WIDTH]
    cg = gates[:, CONV_WIDTH:2 * CONV_WIDTH]
    h = gates[:, 2 * CONV_WIDTH:]
    u = cg * h

    def prev(r):
        v = halo_ref[r:r + 1, CONV_WIDTH:2 * CONV_WIDTH] * halo_ref[r:r + 1, 2 * CONV_WIDTH:GATE_COLS]
        return jnp.where(first_block, 0.0, v)

    row = lax.broadcasted_iota(jnp.int32, (tr, CONV_WIDTH), 0)
    u1 = jnp.where(row == 0, prev(7), pltpu.roll(u, 1, 0))
    u2 = jnp.where(row == 0, prev(6), jnp.where(row == 1, prev(7), pltpu.roll(u, 2, 0)))
    y = taps_ref[0:1, :] * u2 + taps_ref[1:2, :] * u1 + taps_ref[2:3, :] * u
    return bg, cg, h, u, u1, u2, y


def _conv_fwd(gates, taps, g_conv, gmat):
    s = gates.shape[0]
    tr = _row_tile(s, 512)
    hb = tr // SUBLANES

    def body(g_ref, halo_ref, taps_ref, gain_ref, gmat_ref, out_ref):
        i = pl.program_id(0)
        bg, _, _, _, _, _, y = _conv_forward_values(g_ref, halo_ref, taps_ref, i == 0)
        yc = bg * y
        ms = _group_sum(yc * yc, gmat_ref[...]) * (1.0 / GROUP)
        out_ref[...] = (yc * lax.rsqrt(ms + RMS_EPS) * gain_ref[...]).astype(BF16)

    return pl.pallas_call(
        body, name="conv_fwd", grid=(s // tr,),
        in_specs=[pl.BlockSpec((tr, GATE_COLS), lambda i: (i, 0)),
                  pl.BlockSpec((SUBLANES, GATE_COLS), lambda i: (jnp.maximum(i * hb - 1, 0), 0)),
                  pl.BlockSpec((SUBLANES, CONV_WIDTH), lambda i: (0, 0)),
                  pl.BlockSpec((1, CONV_WIDTH), lambda i: (0, 0)),
                  pl.BlockSpec((2 * LANES, LANES), lambda i: (0, 0))],
        out_specs=pl.BlockSpec((tr, CONV_WIDTH), lambda i: (i, 0)),
        out_shape=SDS((s, CONV_WIDTH), BF16),
        compiler_params=_params(("parallel",)))(gates, gates, taps, g_conv, gmat)


def _conv_bwd_gate(gates, dycn, taps, g_conv, gmat):
    s = gates.shape[0]
    tr = _row_tile(s, 512)
    hb = tr // SUBLANES

    def body(g_ref, halo_ref, dn_ref, taps_ref, gain_ref, gmat_ref, dbg_ref, dy_ref, sums_ref):
        i = pl.program_id(0)
        bg, _, _, u, u1, u2, y = _conv_forward_values(g_ref, halo_ref, taps_ref, i == 0)
        gmat_v = gmat_ref[...]
        yc = bg * y
        rstd = lax.rsqrt(_group_sum(yc * yc, gmat_v) * (1.0 / GROUP) + RMS_EPS)
        n = yc * rstd
        dout = dn_ref[...]
        dn = dout * gain_ref[...]
        dyc = rstd * (dn - n * (_group_sum(dn * n, gmat_v) * (1.0 / GROUP)))
        dbg_ref[...] = (dyc * y).astype(BF16)
        dy = dyc * bg
        dy_ref[...] = dy

        @pl.when(i == 0)
        def _():
            sums_ref[...] = jnp.zeros_like(sums_ref)

        sums_ref[0:1, :] += jnp.sum(dy * u2, axis=0, keepdims=True)
        sums_ref[1:2, :] += jnp.sum(dy * u1, axis=0, keepdims=True)
        sums_ref[2:3, :] += jnp.sum(dy * u, axis=0, keepdims=True)
        sums_ref[3:4, :] += jnp.sum(dout * n, axis=0, keepdims=True)

    return pl.pallas_call(
        body, name="conv_bwd_gate", grid=(s // tr,),
        in_specs=[pl.BlockSpec((tr, GATE_COLS), lambda i: (i, 0)),
                  pl.BlockSpec((SUBLANES, GATE_COLS), lambda i: (jnp.maximum(i * hb - 1, 0), 0)),
                  pl.BlockSpec((tr, CONV_WIDTH), lambda i: (i, 0)),
                  pl.BlockSpec((SUBLANES, CONV_WIDTH), lambda i: (0, 0)),
                  pl.BlockSpec((1, CONV_WIDTH), lambda i: (0, 0)),
                  pl.BlockSpec((2 * LANES, LANES), lambda i: (0, 0))],
        out_specs=[pl.BlockSpec((tr, CONV_WIDTH), lambda i: (i, 0)),
                   pl.BlockSpec((tr, CONV_WIDTH), lambda i: (i, 0)),
                   pl.BlockSpec((SUBLANES, CONV_WIDTH), lambda i: (0, 0))],
        out_shape=[SDS((s, CONV_WIDTH), BF16), SDS((s, CONV_WIDTH), F32), SDS((SUBLANES, CONV_WIDTH), F32)],
        compiler_params=_params(("arbitrary",)))(gates, gates, dycn, taps, g_conv, gmat)


def _dproj_assemble(gates, dy, dbg, dq, dk, dv, taps):
    s = gates.shape[0]
    tr = _row_tile(s, 512)
    hb = tr // SUBLANES
    last = s // SUBLANES - 1
    n_blocks = s // tr

    def body(g_ref, dy_ref, halo_ref, dbg_ref, dq_ref, dk_ref, dv_ref, taps_ref, out_ref):
        i = pl.program_id(0)
        gates_v = g_ref[...]
        cg = gates_v[:, CONV_WIDTH:2 * CONV_WIDTH]
        h = gates_v[:, 2 * CONV_WIDTH:]
        dy_v = dy_ref[...]
        last_block = i == n_blocks - 1
        nxt = lambda r: jnp.where(last_block, 0.0, halo_ref[r:r + 1, :])
        row = lax.broadcasted_iota(jnp.int32, (tr, CONV_WIDTH), 0)
        d1 = jnp.where(row == tr - 1, nxt(0), pltpu.roll(dy_v, tr - 1, 0))
        d2 = jnp.where(row == tr - 1, nxt(1), jnp.where(row == tr - 2, nxt(0), pltpu.roll(dy_v, tr - 2, 0)))
        du = taps_ref[2:3, :] * dy_v + taps_ref[1:2, :] * d1 + taps_ref[0:1, :] * d2
        out_ref[:, 0:CONV_WIDTH] = dbg_ref[...]
        out_ref[:, CONV_WIDTH:2 * CONV_WIDTH] = (du * h).astype(BF16)
        out_ref[:, 2 * CONV_WIDTH:GATE_COLS] = (du * cg).astype(BF16)
        out_ref[:, GATE_COLS:GATE_COLS + ATTN_WIDTH] = dq_ref[...]
        out_ref[:, GATE_COLS + ATTN_WIDTH:GATE_COLS + 2 * ATTN_WIDTH] = dk_ref[...].astype(BF16)
        out_ref[:, GATE_COLS + 2 * ATTN_WIDTH:] = dv_ref[...].astype(BF16)

    row_spec = lambda w: pl.BlockSpec((tr, w), lambda i: (i, 0))
    return pl.pallas_call(
        body, name="dproj_assemble", grid=(s // tr,),
        in_specs=[row_spec(GATE_COLS), row_spec(CONV_WIDTH),
                  pl.BlockSpec((SUBLANES, CONV_WIDTH), lambda i: (jnp.minimum((i + 1) * hb, last), 0)),
                  row_spec(CONV_WIDTH), row_spec(ATTN_WIDTH), row_spec(ATTN_WIDTH), row_spec(ATTN_WIDTH),
                  pl.BlockSpec((SUBLANES, CONV_WIDTH), lambda i: (0, 0))],
        out_specs=row_spec(IN_COLS),
        out_shape=SDS((s, IN_COLS), BF16),
        compiler_params=_params(("parallel",)))(gates, dy, dy, dbg, dq, dk, dv, taps)


def _stack_heads(rows, nb):
    lane = lax.broadcasted_iota(jnp.int32, (1, LANES), 1)
    zero = jnp.zeros((KEY_BLOCK, LANES), rows.dtype)
    parts = []
    for blk in range(nb):
        r = rows[blk * KEY_BLOCK:(blk + 1) * KEY_BLOCK]
        parts.append(jnp.where(lane < GROUP, r, zero))
        parts.append(jnp.where(lane < GROUP, zero, r))
    return jnp.concatenate(parts, axis=0)


def _stack_hilo(v, n_cols):
    return jnp.concatenate([_hilo(v[:, c * KEY_BLOCK:(c + 1) * KEY_BLOCK]) for c in range(n_cols)], axis=0)


def _causal_mask(tq, nb, diag_base):
    shape = (tq, 2 * nb * KEY_BLOCK)
    row = lax.broadcasted_iota(jnp.int32, shape, 0)
    col = lax.broadcasted_iota(jnp.int32, shape, 1)
    key = diag_base + (col // (2 * KEY_BLOCK)) * KEY_BLOCK + col % KEY_BLOCK
    return key < row


ANY = pl.BlockSpec(memory_space=pl.ANY)


def _remote_copy(src, dst, sems, idx, target):
    return pltpu.make_async_remote_copy(src_ref=src, dst_ref=dst, send_sem=sems[0].at[idx], recv_sem=sems[1].at[idx],
                                        device_id=target, device_id_type=MESH)


def _gather_chip_hop(bufs, sems):
    x, y, c = _position()
    sends, arrivals = [], []
    for a, buf in enumerate(bufs):
        h = buf.shape[1] // 2
        rows = pl.ds(pl.multiple_of(c * h, h), h)
        mine = buf.at[2 * x + y, rows]
        for j, (fx, fy) in enumerate(CHIP_FLIPS):
            tx, ty = _flip(x, fx), _flip(y, fy)
            there = buf.at[2 * tx + ty, rows]
            sends.append(_remote_copy(mine, mine, sems, 6 * a + j, (tx, ty, c)))
            arrivals.append(_remote_copy(there, there, sems, 6 * a + j, (tx, ty, c)))
    return sends, arrivals


def _gather_sibling_hop(bufs, sems):
    x, y, c = _position()
    sends, arrivals = [], []
    for a, buf in enumerate(bufs):
        h = buf.shape[1] // 2
        mine, theirs = pl.ds(pl.multiple_of(c * h, h), h), pl.ds(pl.multiple_of((1 - c) * h, h), h)
        for j, (fx, fy) in enumerate(CHIP_FLIPS):
            kj = 2 * _flip(x, fx) + _flip(y, fy)
            landed, other = buf.at[kj, mine], buf.at[kj, theirs]
            sends.append(_remote_copy(landed, landed, sems, 6 * a + 3 + j, (x, y, 1 - c)))
            arrivals.append(_remote_copy(other, other, sems, 6 * a + 3 + j, (x, y, 1 - c)))
    return sends, arrivals


def _reduce_copies(ins, outs, sems):
    x, y, c = _position()
    sends, arrivals = [], []
    for a in range(len(ins)):
        h = ins[a].shape[1] // 2
        for f, (fx, fy, fc) in enumerate(DEVICE_FLIPS):
            tx, ty, tc = _flip(x, fx), _flip(y, fy), _flip(c, fc)
            src = ins[a].at[2 * tx + ty, pl.ds(pl.multiple_of(tc * h, h), h)]
            sends.append(_remote_copy(src, outs[a].at[f], sems, 7 * a + f, (tx, ty, tc)))
            arrivals.append(_remote_copy(outs[a].at[f], outs[a].at[f], sems, 7 * a + f, (tx, ty, tc)))
    return sends, arrivals


def _start_copies(make):
    sends, _ = make()
    for cp in sends:
        cp.start()


def _finish_copies(make):
    sends, arrivals = make()
    for cp in arrivals:
        cp.wait_recv()
    for cp in sends:
        cp.wait_send()


def _attn_fwd(qkv, g_attn, tri, gmat, shards):
    n_w = len(shards)
    s = qkv.shape[0]
    tq = _row_tile(s, ATTN_Q_TILE)
    tk = KEY_BLOCK
    nb = ATTN_KEY_BLOCKS
    width = nb * tk
    n_groups = ATTN_DIAG_GROUPS
    group = tq // n_groups
    pairs = ATTN_WIDTH // LANES

    def body(q_ref, k_ref, v_ref, gain_ref, tri_ref, gmat_ref, *rest):
        o_ref, yn_ref, tot_ref, cut_ref = rest[n_w:n_w + 4]
        w_bufs, sems = rest[n_w + 4:2 * n_w + 4], rest[2 * n_w + 4:]
        chip_hop = functools.partial(_gather_chip_hop, w_bufs, sems)
        sibling_hop = functools.partial(_gather_sibling_hop, w_bufs, sems)
        p, i = pl.program_id(0), pl.program_id(1)
        pl.when((p == 0) & (i == 0))(functools.partial(_start_copies, chip_hop))

        @pl.when((p == pairs - 1) & (i == 0))
        def _():
            for cp in chip_hop()[1]:
                cp.wait_recv()
            _start_copies(sibling_hop)

        q2 = q_ref[...]
        tri_v = tri_ref[...]

        def trip(s0, n_blk, rows, carry, diag_base):
            r0, nr = rows
            run = [carry[0], carry[1]]
            oacc = carry[2]
            ksel = _stack_heads(k_ref[pl.ds(s0, n_blk * tk), :], n_blk)
            vsel = _stack_heads(v_ref[pl.ds(s0, n_blk * tk), :], n_blk)
            z = lax.dot_general(q2[r0:r0 + nr], ksel, NT_DIMS, preferred_element_type=F32) * (ATTN_SCALE * LOG2_E)
            log_beta, log_keep = _softplus_terms(z)
            if diag_base is not None:
                valid = _causal_mask(nr, n_blk, diag_base)
                log_keep = jnp.where(valid, log_keep, 0.0)
            ct = jnp.dot(_stack_hilo(log_keep, 2 * n_blk), tri_v, preferred_element_type=F32)
            a_parts = [None] * (2 * n_blk)
            for c in reversed(range(2 * n_blk)):
                h = c % 2
                ct_c = ct[c * nr:(c + 1) * nr]
                a_parts[c] = jnp.exp2(log_beta[:, c * tk:(c + 1) * tk] + ct_c[:, :tk] + run[h])
                run[h] = run[h] + ct_c[:, tk:]
            a = jnp.concatenate(a_parts, axis=1)
            if diag_base is not None:
                a = jnp.where(valid, a, 0.0)
            oacc = oacc + jnp.dot(a.astype(BF16), vsel, preferred_element_type=F32)
            return run[0], run[1], oacc

        groups = []
        for g in range(n_groups):
            zeros = (jnp.zeros((group, tk), F32), jnp.zeros((group, tk), F32), jnp.zeros((group, LANES), F32))
            groups.append(trip(pl.multiple_of(i * tq, tq), (g + 1) * group // tk, (g * group, group), zeros,
                               -g * group))
        carry = tuple(jnp.concatenate([grp[j] for grp in groups], axis=0) for j in range(3))
        n_full = i * (tq // width)

        def alive(run_a, run_b):
            return jnp.max(jnp.maximum(run_a, run_b)) > -ATTN_DEAD_LOG2

        def earlier_trip(c):
            done, _, run_a, run_b, oacc = c
            s0 = pl.multiple_of((n_full - 1 - done) * width, width)
            run_a, run_b, oacc = trip(s0, nb, (0, tq), (run_a, run_b, oacc), None)
            return done + 1, alive(run_a, run_b), run_a, run_b, oacc

        done, _, run_a, run_b, oacc = lax.while_loop(
            lambda c: (c[0] < n_full) & c[1], earlier_trip, (jnp.int32(0), alive(carry[0], carry[1])) + carry)
        cut_ref[p * pl.num_programs(1) + i] = (n_full - done).astype(F32)
        lane = lax.broadcasted_iota(jnp.int32, (1, LANES), 1)
        o_ref[...] = oacc
        tot_ref[...] = jnp.where(lane < GROUP, run_a, run_b)
        ms = _group_sum(oacc * oacc, gmat_ref[...]) * (1.0 / GROUP)
        yn_ref[...] = (oacc * lax.rsqrt(ms + RMS_EPS) * gain_ref[...]).astype(BF16)

        @pl.when((p == pairs - 1) & (i == pl.num_programs(1) - 1))
        def _():
            for cp in chip_hop()[0]:
                cp.wait_send()
            _finish_copies(sibling_hop)

    blk = lambda: pl.BlockSpec((tq, LANES), lambda p, i: (i, p))
    return pl.pallas_call(
        body, name="attn_fwd", grid=(pairs, s // tq),
        in_specs=[pl.BlockSpec((tq, LANES), lambda p, i: (i, p)),
                  pl.BlockSpec((s, LANES), lambda p, i: (0, pairs + p)),
                  pl.BlockSpec((s, LANES), lambda p, i: (0, 2 * pairs + p)),
                  pl.BlockSpec((1, LANES), lambda p, i: (0, p)),
                  pl.BlockSpec((2 * tk, 2 * tk), lambda p, i: (0, 0)),
                  pl.BlockSpec((2 * LANES, LANES), lambda p, i: (0, 0))] + [ANY] * n_w,
        out_specs=[blk(), blk(), blk(), pl.BlockSpec(memory_space=pltpu.SMEM)] + [ANY] * n_w,
        out_shape=[SDS((s, ATTN_WIDTH), F32), SDS((s, ATTN_WIDTH), BF16), SDS((s, ATTN_WIDTH), F32),
                   SDS((pairs * (s // tq),), F32)]
        + [SDS(w.shape, w.dtype) for w in shards],
        input_output_aliases={6 + a: 4 + a for a in range(n_w)},
        scratch_shapes=[pltpu.SemaphoreType.DMA((6 * n_w,)), pltpu.SemaphoreType.DMA((6 * n_w,))],
        compiler_params=_params(("arbitrary", "arbitrary")))(qkv, qkv, qkv, g_attn, tri, gmat, *shards)


def _attn_bwd(qkv, o, tot, dyn, g_attn, tri, gmat, cut, partials):
    n_g = len(partials)
    s = qkv.shape[0]
    tq = _row_tile(s, ATTN_Q_TILE)
    tk = KEY_BLOCK
    nb = ATTN_KEY_BLOCKS
    width = nb * tk
    n_groups = ATTN_DIAG_GROUPS
    group = tq // n_groups
    pairs = ATTN_WIDTH // LANES

    def body(q_ref, k_ref, v_ref, o_ref, tot_ref, dyn_ref, gain_ref, tri_ref, gmat_ref, cut_ref, *rest):
        g_ins, (dq_ref, dk_ref, dv_ref, dg_ref) = rest[:n_g], rest[n_g:n_g + 4]
        g_outs, sems = rest[n_g + 4:2 * n_g + 4], rest[2 * n_g + 4:]
        copies = functools.partial(_reduce_copies, g_ins, g_outs, sems)
        p, i = pl.program_id(0), pl.program_id(1)
        pl.when((p == 0) & (i == 0))(functools.partial(_start_copies, copies))

        @pl.when(i == 0)
        def _():
            dk_ref[...] = jnp.zeros_like(dk_ref)
            dv_ref[...] = jnp.zeros_like(dv_ref)
            dg_ref[...] = jnp.zeros_like(dg_ref)

        gmat_v = gmat_ref[...]
        o_v = o_ref[...]
        rstd = lax.rsqrt(_group_sum(o_v * o_v, gmat_v) * (1.0 / GROUP) + RMS_EPS)
        n = o_v * rstd
        dout = dyn_ref[...]
        dg_ref[0:1, :] += jnp.sum(dout * n, axis=0, keepdims=True)
        dn = dout * gain_ref[...]
        do2 = (rstd * (dn - n * (_group_sum(dn * n, gmat_v) * (1.0 / GROUP)))).astype(BF16)
        q2 = q_ref[...]
        tot_v = tot_ref[...]
        tots = (jnp.broadcast_to(tot_v[:, 0:1], (tq, tk)), jnp.broadcast_to(tot_v[:, GROUP:GROUP + 1], (tq, tk)))
        tri_v, tri_incl_v = tri_ref[0], tri_ref[1]
        lane = lax.broadcasted_iota(jnp.int32, (1, LANES), 1)

        def trip(s0, n_blk, rows, carry, diag_base):
            r0, nr = rows
            rest_l = [carry[0], carry[1]]
            pref_g = [carry[2], carry[3]]
            dq = carry[4]
            q_rows, do_rows = q2[r0:r0 + nr], do2[r0:r0 + nr]
            ksel = _stack_heads(k_ref[pl.ds(s0, n_blk * tk), :], n_blk)
            vsel = _stack_heads(v_ref[pl.ds(s0, n_blk * tk), :], n_blk)
            z = lax.dot_general(q_rows, ksel, NT_DIMS, preferred_element_type=F32) * (ATTN_SCALE * LOG2_E)
            log_beta, log_keep = _softplus_terms(z)
            if diag_base is not None:
                valid = _causal_mask(nr, n_blk, diag_base)
                log_keep = jnp.where(valid, log_keep, 0.0)
            ctl = jnp.dot(_stack_hilo(log_keep, 2 * n_blk), tri_incl_v, preferred_element_type=F32)
            da = lax.dot_general(do_rows, vsel, NT_DIMS, preferred_element_type=F32)
            a_parts = []
            for c in range(2 * n_blk):
                h = c % 2
                ct_c = ctl[c * nr:(c + 1) * nr]
                cols = slice(c * tk, (c + 1) * tk)
                a_parts.append(jnp.exp2(log_beta[:, cols] + (rest_l[h] - ct_c[:, :tk])))
                rest_l[h] = rest_l[h] - ct_c[:, tk:]
            a = jnp.concatenate(a_parts, axis=1)
            if diag_base is not None:
                a = jnp.where(valid, a, 0.0)
            g = a * da
            ctg = jnp.dot(_stack_hilo(g, 2 * n_blk), tri_v, preferred_element_type=F32)
            dz_parts = []
            for c in range(2 * n_blk):
                h = c % 2
                ct_c = ctg[c * nr:(c + 1) * nr]
                cols = slice(c * tk, (c + 1) * tk)
                prefix = pref_g[h] + ct_c[:, :tk]
                pref_g[h] = pref_g[h] + ct_c[:, tk:]
                g_c = g[:, cols]
                dz_parts.append(g_c - jnp.exp2(log_beta[:, cols]) * (g_c + prefix))
            dz = jnp.concatenate(dz_parts, axis=1) * ATTN_SCALE
            if diag_base is not None:
                dz = jnp.where(valid, dz, 0.0)
            dzb = dz.astype(BF16)
            dq = dq + jnp.dot(dzb, ksel, preferred_element_type=F32)
            dkt = lax.dot_general(dzb, q_rows, TN_DIMS, preferred_element_type=F32)
            dvt = lax.dot_general(a.astype(BF16), do_rows, TN_DIMS, preferred_element_type=F32)
            for blk in range(n_blk):
                ra, rb = slice(2 * blk * tk, (2 * blk + 1) * tk), slice((2 * blk + 1) * tk, (2 * blk + 2) * tk)
                keys = pl.ds(pl.multiple_of(s0 + blk * tk, tk), tk)
                dk_ref[keys, :] += jnp.where(lane < GROUP, dkt[ra], dkt[rb])
                dv_ref[keys, :] += jnp.where(lane < GROUP, dvt[ra], dvt[rb])
            return rest_l[0], rest_l[1], pref_g[0], pref_g[1], dq

        zeros_qk = jnp.zeros((tq, tk), F32)
        carry = (tots[0], tots[1], zeros_qk, zeros_qk, jnp.zeros((tq, LANES), F32))
        n_full = i * (tq // width)
        first = jnp.clip(cut_ref[p * pl.num_programs(1) + i].astype(jnp.int32), 0, n_full)
        carry = lax.fori_loop(
            first, n_full, lambda t, c: trip(pl.multiple_of(t * width, width), nb, (0, tq), c, None), carry)
        dq_groups = []
        for g in range(n_groups):
            sub = tuple(x[g * group:(g + 1) * group] for x in carry)
            dq_groups.append(trip(pl.multiple_of(i * tq, tq), (g + 1) * group // tk, (g * group, group), sub,
                                  -g * group)[4])
        dq_ref[...] = jnp.concatenate(dq_groups, axis=0).astype(BF16)
        pl.when((p == pairs - 1) & (i == pl.num_programs(1) - 1))(functools.partial(_finish_copies, copies))

    blk = lambda: pl.BlockSpec((tq, LANES), lambda p, i: (i, p))
    col = lambda: pl.BlockSpec((s, LANES), lambda p, i: (0, p))
    n_peers = len(DEVICE_FLIPS)
    return pl.pallas_call(
        body, name="attn_bwd", grid=(pairs, s // tq),
        in_specs=[pl.BlockSpec((tq, LANES), lambda p, i: (i, p)),
                  pl.BlockSpec((s, LANES), lambda p, i: (0, pairs + p)),
                  pl.BlockSpec((s, LANES), lambda p, i: (0, 2 * pairs + p)),
                  blk(), blk(), blk(),
                  pl.BlockSpec((1, LANES), lambda p, i: (0, p)),
                  pl.BlockSpec((2, 2 * tk, 2 * tk), lambda p, i: (0, 0, 0)),
                  pl.BlockSpec((2 * LANES, LANES), lambda p, i: (0, 0)),
                  pl.BlockSpec(memory_space=pltpu.SMEM)] + [ANY] * n_g,
        out_specs=[blk(), col(), col(), pl.BlockSpec((SUBLANES, LANES), lambda p, i: (0, p))] + [ANY] * n_g,
        out_shape=[SDS((s, ATTN_WIDTH), BF16), SDS((s, ATTN_WIDTH), F32), SDS((s, ATTN_WIDTH), F32),
                   SDS((SUBLANES, ATTN_WIDTH), F32)]
        + [SDS((n_peers, g.shape[1] // 2, g.shape[2]), g.dtype) for g in partials],
        scratch_shapes=[pltpu.SemaphoreType.DMA((n_peers * n_g,)), pltpu.SemaphoreType.DMA((n_peers * n_g,))],
        compiler_params=_params(("arbitrary", "arbitrary")))(
            qkv, qkv, qkv, o, tot, dyn, g_attn, tri, gmat, cut, *partials)


def _mix_ln1(ycn, yan, w_out, x, g, b):
    s = x.shape[0]
    tm = _row_tile(s, 512)

    def body(yc_ref, ya_ref, w_ref, x_ref, g_ref, b_ref, x1_ref, xhat_ref, rstd_ref):
        mix = jnp.dot(yc_ref[...], w_ref[0:CONV_WIDTH, :], preferred_element_type=F32)
        mix = mix + jnp.dot(ya_ref[...], w_ref[CONV_WIDTH:, :], preferred_element_type=F32)
        x1, xhat, rstd = _layer_norm_fwd(ALPHA * x_ref[...] + mix, g_ref[...], b_ref[...])
        x1_ref[...] = x1
        xhat_ref[...] = xhat
        rstd_ref[...] = rstd

    row = lambda w: pl.BlockSpec((tm, w), lambda i: (i, 0))
    vec = lambda: pl.BlockSpec((1, D_MODEL), lambda i: (0, 0))
    return pl.pallas_call(
        body, name="mix_ln1", grid=(s // tm,),
        in_specs=[row(CONV_WIDTH), row(ATTN_WIDTH), pl.BlockSpec((D_MODEL, D_MODEL), lambda i: (0, 0)),
                  row(D_MODEL), vec(), vec()],
        out_specs=[row(D_MODEL), row(D_MODEL), row(1)],
        out_shape=[SDS((s, D_MODEL), F32), SDS((s, D_MODEL), F32), SDS((s, 1), F32)],
        compiler_params=_params(("parallel",)))(ycn, yan, w_out, x, g, b)


def _mlp_fwd_loss(x1, w_up, w_down, target, g, b):
    s = x1.shape[0]
    tm = _row_tile(s, 256)

    def body(x1_ref, wu_ref, wd_ref, t_ref, g_ref, b_ref, dpre_ref, sums_ref, loss_ref, r_ref, hid_ref):
        i = pl.program_id(0)
        x1_v = x1_ref[...]
        xb = x1_v.astype(BF16)
        ffn = jnp.zeros((tm, D_MODEL), F32)
        for k in range(N_CHIPS):
            r = jnp.maximum(jnp.dot(xb, wu_ref[k], preferred_element_type=F32), 0.0)
            hid = (r * r).astype(BF16)
            r_ref[:, FF_SHARD * k:FF_SHARD * (k + 1)] = r.astype(BF16)
            hid_ref[:, FF_SHARD * k:FF_SHARD * (k + 1)] = hid
            ffn = ffn + jnp.dot(hid, wd_ref[k], preferred_element_type=F32)
        g_v = g_ref[...]
        x2, xhat, rstd = _layer_norm_fwd(ALPHA * x1_v + ffn, g_v, b_ref[...])
        err = x2 - t_ref[...]
        dx2 = err * (1.0 / D_MODEL)
        dpre_ref[...] = _layer_norm_bwd(dx2, xhat, rstd, g_v)

        @pl.when(i == 0)
        def _():
            sums_ref[...] = jnp.zeros_like(sums_ref)
            loss_ref[...] = jnp.zeros_like(loss_ref)

        sums_ref[0:1, :] += jnp.sum(dx2 * xhat, axis=0, keepdims=True)
        sums_ref[1:2, :] += jnp.sum(dx2, axis=0, keepdims=True)
        loss_ref[...] += jnp.sum(jnp.sum(err * err, axis=1, keepdims=True), axis=0, keepdims=True) * (0.5 / D_MODEL)

    row = lambda: pl.BlockSpec((tm, D_MODEL), lambda i: (i, 0))
    wide = lambda: pl.BlockSpec((tm, D_FF), lambda i: (i, 0))
    vec = lambda: pl.BlockSpec((1, D_MODEL), lambda i: (0, 0))
    return pl.pallas_call(
        body, name="mlp_fwd_loss", grid=(s // tm,),
        in_specs=[row(), _resident_weight(), _resident_weight(), row(), vec(), vec()],
        out_specs=[row(), pl.BlockSpec((SUBLANES, D_MODEL), lambda i: (0, 0)),
                   pl.BlockSpec((SUBLANES, LANES), lambda i: (0, 0)), wide(), wide()],
        out_shape=[SDS((s, D_MODEL), F32), SDS((SUBLANES, D_MODEL), F32), SDS((SUBLANES, LANES), F32),
                   SDS((s, D_FF), BF16), SDS((s, D_FF), BF16)],
        compiler_params=_params(("arbitrary",)))(x1, w_up, w_down, target, g, b)


def _resident_weight():
    return pl.BlockSpec((N_CHIPS, D_MODEL, FF_SHARD), lambda i: (0, 0, 0), pipeline_mode=pl.Buffered(1))


def _mlp_bwd_ln1(relu_up, dpre2, w_up, w_down, xhat1, rstd1, g1):
    s = dpre2.shape[0]
    tm = _row_tile(s, 256)

    def body(r_ref, d2_ref, wu_ref, wd_ref, xh_ref, rs_ref, g_ref, dup_ref, dpre_ref, sums_ref):
        i = pl.program_id(0)
        d2 = d2_ref[...]
        d2b = d2.astype(BF16)
        dx1 = ALPHA * d2
        for k in range(N_CHIPS):
            r = r_ref[:, FF_SHARD * k:FF_SHARD * (k + 1)].astype(F32)
            dhid = lax.dot_general(d2b, wd_ref[k], NT_DIMS, preferred_element_type=F32)
            dupb = (dhid * (2.0 * r)).astype(BF16)
            dup_ref[:, FF_SHARD * k:FF_SHARD * (k + 1)] = dupb
            dx1 = dx1 + lax.dot_general(dupb, wu_ref[k], NT_DIMS, preferred_element_type=F32)
        xhat = xh_ref[...]
        dpre_ref[...] = _layer_norm_bwd(dx1, xhat, rs_ref[...], g_ref[...])

        @pl.when(i == 0)
        def _():
            sums_ref[...] = jnp.zeros_like(sums_ref)

        sums_ref[0:1, :] += jnp.sum(dx1 * xhat, axis=0, keepdims=True)
        sums_ref[1:2, :] += jnp.sum(dx1, axis=0, keepdims=True)

    row = lambda w: pl.BlockSpec((tm, w), lambda i: (i, 0))
    return pl.pallas_call(
        body, name="mlp_bwd_ln1", grid=(s // tm,),
        in_specs=[row(D_FF), row(D_MODEL), _resident_weight(), _resident_weight(), row(D_MODEL), row(1),
                  pl.BlockSpec((1, D_MODEL), lambda i: (0, 0))],
        out_specs=[row(D_FF), row(D_MODEL), pl.BlockSpec((SUBLANES, D_MODEL), lambda i: (0, 0))],
        out_shape=[SDS((s, D_FF), BF16), SDS((s, D_MODEL), F32), SDS((SUBLANES, D_MODEL), F32)],
        compiler_params=_params(("arbitrary",)))(relu_up, dpre2, w_up, w_down, xhat1, rstd1, g1)


def _grad_tn(a, b, name, out_cols, stacked):
    s, ka = a.shape
    n = b.shape[1]
    ts = _row_tile(s, 2048)
    n_steps = s // ts
    if stacked:
        tka, tn = ka, out_cols
        grid = (1, n // tn, n_steps)
        shape = (n // tn, ka, tn)
        out_spec = lambda: pl.BlockSpec((None, tka, tn), lambda r, c, t: (c, 0, 0))
    else:
        tka, tn = min(ka, 1024), n
        grid = (ka // tka, 1, n_steps)
        shape = (ka, n)
        out_spec = lambda: pl.BlockSpec((tka, tn), lambda r, c, t: (r, 0))

    def body(a_ref, b_ref, o_ref, ob_ref):
        t = pl.program_id(2)

        @pl.when(t == 0)
        def _():
            o_ref[...] = jnp.zeros_like(o_ref)

        o_ref[...] += lax.dot_general(a_ref[...].astype(BF16), b_ref[...].astype(BF16), TN_DIMS,
                                      preferred_element_type=F32)

        @pl.when(t == n_steps - 1)
        def _():
            ob_ref[...] = o_ref[...].astype(BF16)

    return pl.pallas_call(
        body, name=name, grid=grid,
        in_specs=[pl.BlockSpec((ts, tka), lambda r, c, t: (t, r)),
                  pl.BlockSpec((ts, tn), lambda r, c, t: (t, c))],
        out_specs=[out_spec(), out_spec()], out_shape=[SDS(shape, F32), SDS(shape, BF16)],
        compiler_params=_params(("parallel", "parallel", "arbitrary")))(a, b)


def _dmix(dpre1, w_out):
    s = dpre1.shape[0]
    tm = _row_tile(s, 512)

    def body(d_ref, w_ref, dc_ref, da_ref):
        db = d_ref[...].astype(BF16)
        dc_ref[...] = lax.dot_general(db, w_ref[0:CONV_WIDTH, :], NT_DIMS, preferred_element_type=F32)
        da_ref[...] = lax.dot_general(db, w_ref[CONV_WIDTH:, :], NT_DIMS, preferred_element_type=F32)

    return pl.pallas_call(
        body, name="dmix", grid=(s // tm,),
        in_specs=[pl.BlockSpec((tm, D_MODEL), lambda i: (i, 0)),
                  pl.BlockSpec((D_MODEL, D_MODEL), lambda i: (0, 0))],
        out_specs=[pl.BlockSpec((tm, CONV_WIDTH), lambda i: (i, 0)),
                   pl.BlockSpec((tm, ATTN_WIDTH), lambda i: (i, 0))],
        out_shape=[SDS((s, CONV_WIDTH), F32), SDS((s, ATTN_WIDTH), F32)],
        compiler_params=_params(("parallel",)))(dpre1, w_out)


def _sum_with_peers(own_ref, r_ref, o_ref):
    acc = own_ref[...]
    for f in range(r_ref.shape[0]):
        acc = acc + r_ref[f].astype(F32)
    o_ref[...] = acc


def _grad_x(kc_idx, dproj, w_in, dpre1, partial, earlier):
    s = dproj.shape[0]
    tm = _row_tile(s, 512)
    steps = s // tm
    n_peers = len(DEVICE_FLIPS)
    n_e = len(earlier)

    def body(kc_ref, dp_ref, w_ref, d1_ref, *rest):
        sum_ins, g_in = rest[:2 * n_e], rest[2 * n_e]
        o_ref, g_out = rest[2 * n_e + 1], rest[2 * n_e + 2]
        sum_outs, sems = rest[2 * n_e + 3:3 * n_e + 3], rest[3 * n_e + 3:]
        copies = functools.partial(_reduce_copies, [g_in], [g_out], sems)
        i = pl.program_id(0)
        pl.when(i == 0)(functools.partial(_start_copies, copies))
        acc = ALPHA * d1_ref[...]
        for k in range(N_CHIPS):
            acc = acc + lax.dot_general(dp_ref[:, IN_SHARD * k:IN_SHARD * (k + 1)], w_ref[k], NT_DIMS,
                                        preferred_element_type=F32)
        o_ref[...] = acc
        for a in range(n_e):
            _sum_with_peers(sum_ins[2 * a], sum_ins[2 * a + 1], sum_outs[a])
        pl.when(i == steps - 1)(functools.partial(_finish_copies, copies))

    in_specs = [pl.BlockSpec((tm, IN_COLS), lambda i, kc: (i, 0)),
                pl.BlockSpec((N_CHIPS, D_MODEL, IN_SHARD), lambda i, kc: (0, 0, 0)),
                pl.BlockSpec((tm, D_MODEL), lambda i, kc: (i, 0))]
    out_specs = [pl.BlockSpec((tm, D_MODEL), lambda i, kc: (i, 0)), ANY]
    out_shape = [SDS((s, D_MODEL), F32), SDS((n_peers, partial.shape[1] // 2, partial.shape[2]), partial.dtype)]
    operands = []
    for own, recv in earlier:
        _, _, h, cols = own.shape
        th = h // steps
        in_specs.append(pl.BlockSpec((None, None, th, cols), lambda i, kc: (kc[0], kc[1], i, 0)))
        in_specs.append(pl.BlockSpec((n_peers, th, cols), lambda i, kc: (0, i, 0)))
        out_specs.append(pl.BlockSpec((th, cols), lambda i, kc: (kc[1] * steps + i, 0)))
        out_shape.append(SDS((2 * h, cols), F32))
        operands += [own, recv]
    grid_spec = pltpu.PrefetchScalarGridSpec(
        num_scalar_prefetch=1, grid=(steps,), in_specs=in_specs + [ANY], out_specs=out_specs,
        scratch_shapes=[pltpu.SemaphoreType.DMA((n_peers,)), pltpu.SemaphoreType.DMA((n_peers,))])
    return pl.pallas_call(
        body, name="grad_x", grid_spec=grid_spec, out_shape=out_shape,
        compiler_params=_params(("arbitrary",)))(kc_idx, dproj, w_in, dpre1, *operands, partial)


def _adamw(w, g, m, v, name):
    r, c = w.shape
    tr = _row_tile(r, 256)

    def body(w_ref, g_ref, m_ref, v_ref, go_ref, d_ref, nm_ref, nv_ref):
        g_v = g_ref[...]
        go_ref[...] = g_v
        nm = ADAM_B1 * m_ref[...] + (1.0 - ADAM_B1) * g_v
        nv = ADAM_B2 * v_ref[...] + (1.0 - ADAM_B2) * (g_v * g_v)
        m_hat = nm / (1.0 - ADAM_B1 ** ADAM_STEP)
        v_hat = nv / (1.0 - ADAM_B2 ** ADAM_STEP)
        d_ref[...] = -ADAM_LR * (m_hat / (jnp.sqrt(v_hat) + ADAM_EPS) + ADAM_WD * w_ref[...])
        nm_ref[...] = nm
        nv_ref[...] = nv

    spec = lambda: pl.BlockSpec((tr, c), lambda i: (i, 0))
    return pl.pallas_call(
        body, name=name, grid=(r // tr,),
        in_specs=[spec(), spec(), spec(), spec()], out_specs=[spec(), spec(), spec(), spec()],
        out_shape=[SDS((r, c), F32)] * 4, compiler_params=_params(("parallel",)))(w, g, m, v)


def _sum_partials(kc_idx, grad, recv, name):
    _, _, h, cols = grad.shape
    th = _row_tile(h, 128)
    n_peers = recv.shape[0]

    def body(kc_ref, own_ref, r_ref, o_ref):
        _sum_with_peers(own_ref, r_ref, o_ref)

    grid_spec = pltpu.PrefetchScalarGridSpec(
        num_scalar_prefetch=1, grid=(h // th,),
        in_specs=[pl.BlockSpec((None, None, th, cols), lambda t, kc: (kc[0], kc[1], t, 0)),
                  pl.BlockSpec((n_peers, th, cols), lambda t, kc: (0, t, 0))],
        out_specs=pl.BlockSpec((th, cols), lambda t, kc: (kc[1] * (h // th) + t, 0)))
    return pl.pallas_call(
        body, name=name, grid_spec=grid_spec, out_shape=SDS((2 * h, cols), F32),
        compiler_params=_params(("parallel",)))(kc_idx, grad, recv)


def _gather_weights(shards, conv_shard):
    n = len(shards)

    def body(*refs):
        outs, conv_out = refs[n + 1:2 * n + 1], refs[2 * n + 1]
        sems = refs[2 * n + 2:]
        x, y, c = _position()
        chip_sends, chip_arrivals = _gather_chip_hop(outs, sems)
        mine = conv_out.at[2 * x + y]
        conv_arrivals = []
        for j, (fx, fy) in enumerate(CHIP_FLIPS):
            tx, ty = _flip(x, fx), _flip(y, fy)
            there = conv_out.at[2 * tx + ty]
            chip_sends.append(_remote_copy(mine, mine, sems, 6 * n + j, (tx, ty, c)))
            conv_arrivals.append(_remote_copy(there, there, sems, 6 * n + j, (tx, ty, c)))
        for cp in chip_sends:
            cp.start()
        for cp in chip_arrivals:
            cp.wait_recv()
        sibling_sends, sibling_arrivals = _gather_sibling_hop(outs, sems)
        for cp in sibling_sends:
            cp.start()
        for cp in sibling_arrivals + conv_arrivals:
            cp.wait_recv()
        for cp in chip_sends + sibling_sends:
            cp.wait_send()

    out_shape = [SDS(w.shape, w.dtype) for w in shards] + [SDS(conv_shard.shape, conv_shard.dtype)]
    n_sems = 6 * n + 3
    return pl.pallas_call(
        body, name="gather_weights", in_specs=[ANY] * (n + 1), out_specs=[ANY] * (n + 1),
        out_shape=out_shape, input_output_aliases={a: a for a in range(n + 1)},
        scratch_shapes=[pltpu.SemaphoreType.DMA((n_sems,)), pltpu.SemaphoreType.DMA((n_sems,))])(
            *shards, conv_shard)


def _finish_exchange(pieces, vec):
    n = len(pieces)
    n_dev = 2 * N_CHIPS

    def body(*refs):
        v_ref = refs[n]
        outs, o_ref = refs[n + 1:2 * n + 1], refs[2 * n + 1]
        buf, send_sems, recv_sems = refs[2 * n + 2:]
        x, y, c = _position()
        sibling = (x, y, 1 - c)
        me = 4 * x + 2 * y + c
        buf[me] = v_ref[...]
        started = []
        for f, (fx, fy, fc) in enumerate(DEVICE_FLIPS):
            cp = pltpu.make_async_remote_copy(
                src_ref=v_ref, dst_ref=buf.at[me], send_sem=send_sems.at[n + f], recv_sem=recv_sems.at[n + f],
                device_id=(_flip(x, fx), _flip(y, fy), _flip(c, fc)), device_id_type=MESH)
            cp.start()
            started.append(cp)
        for a in range(n):
            h = pieces[a].shape[0] // 2
            mine = outs[a].at[pl.ds(pl.multiple_of(c * h, h), h)]
            cp = pltpu.make_async_remote_copy(
                src_ref=mine, dst_ref=mine, send_sem=send_sems.at[a], recv_sem=recv_sems.at[a],
                device_id=sibling, device_id_type=MESH)
            cp.start()
            started.append(cp)
        for a in range(n):
            h = pieces[a].shape[0] // 2
            theirs = outs[a].at[pl.ds(pl.multiple_of((1 - c) * h, h), h)]
            pltpu.make_async_remote_copy(
                src_ref=theirs, dst_ref=theirs, send_sem=send_sems.at[a], recv_sem=recv_sems.at[a],
                device_id=sibling, device_id_type=MESH).wait_recv()
        for f, (fx, fy, fc) in enumerate(DEVICE_FLIPS):
            src = 4 * _flip(x, fx) + 2 * _flip(y, fy) + _flip(c, fc)
            pltpu.make_async_remote_copy(
                src_ref=v_ref, dst_ref=buf.at[src], send_sem=send_sems.at[n + f], recv_sem=recv_sems.at[n + f],
                device_id=(x, y, c), device_id_type=MESH).wait_recv()
        for cp in started:
            cp.wait_send()
        acc = buf[0]
        for d in range(1, n_dev):
            acc = acc + buf[d]
        o_ref[...] = acc

    vmem = pl.BlockSpec(memory_space=pltpu.VMEM)
    out_shape = [SDS(p.shape, p.dtype) for p in pieces] + [SDS(vec.shape, vec.dtype)]
    n_sems = n + n_dev - 1
    return pl.pallas_call(
        body, name="finish_exchange", in_specs=[ANY] * n + [vmem], out_specs=[ANY] * n + [vmem],
        out_shape=out_shape, input_output_aliases={a: a for a in range(n)},
        scratch_shapes=[pltpu.VMEM((n_dev,) + vec.shape, vec.dtype), pltpu.SemaphoreType.DMA((n_sems,)),
                        pltpu.SemaphoreType.DMA((n_sems,))])(*pieces, vec)


def _constants():
    r = jnp.arange(2 * KEY_BLOCK)[:, None] % KEY_BLOCK
    c = jnp.arange(2 * KEY_BLOCK)[None, :]
    later = jnp.where(c < KEY_BLOCK, r > c, True).astype(BF16)
    earlier = jnp.where(c < KEY_BLOCK, r < c, True).astype(BF16)
    upto = jnp.where(c < KEY_BLOCK, r <= c, True).astype(BF16)
    gr = (jnp.arange(2 * LANES)[:, None] % LANES) // GROUP
    gc = jnp.arange(LANES)[None, :] // GROUP
    gmat = (gr == gc).astype(BF16)
    return later, jnp.stack([earlier, upto]), gmat


def _rows(v):
    return v.reshape(-1, LANES)


def kernel(x, w_in, conv_w, g_conv, g_attn, w_out, ln1_g, ln1_b, w_up, w_down, ln2_g, ln2_b, loss_target, m_w_in, m_conv_w, m_g_conv, m_g_attn, m_w_out, m_ln1_g, m_ln1_b, m_w_up, m_w_down, m_ln2_g, m_ln2_b, v_w_in, v_conv_w, v_g_conv, v_g_attn, v_w_out, v_ln1_g, v_ln1_b, v_w_up, v_w_down, v_ln2_g, v_ln2_b):
    xs, target = x[0], loss_target[0]
    mesh_x, mesh_y, mesh_c = _position()
    k_idx = 2 * mesh_x + mesh_y
    kc_idx = jnp.stack([k_idx, mesh_c]).astype(jnp.int32)
    tri_later, tri_earlier, gmat = _constants()

    w_in_b, w_out_b, w_up_b, w_down_b = [
        _cast_into_slot(kc_idx, w[0], "cast_" + nm)
        for w, nm in ((w_in, "w_in"), (w_out, "w_out"), (w_up, "w_up"), (w_down, "w_down"))]
    conv_slot = jnp.pad(conv_w, ((0, 0), (0, SUBLANES - conv_w.shape[1]), (0, 0)))
    conv_b = lax.dynamic_update_slice(jnp.zeros((N_CHIPS, SUBLANES, LANES), F32), conv_slot, (k_idx, 0, 0))
    w_in_f, conv_f = _gather_weights([w_in_b], conv_b)
    taps = jnp.transpose(conv_f, (1, 0, 2)).reshape(SUBLANES, CONV_WIDTH)

    gates, qkv = _proj(xs, w_in_f)
    ycn = _conv_fwd(gates, taps, g_conv, gmat)
    o, yan, tot, cut, w_out_f, w_up_f, w_down_f = _attn_fwd(
        qkv, g_attn, tri_later, gmat, [w_out_b, w_up_b, w_down_b])
    w_out_f = w_out_f.reshape(D_MODEL, D_MODEL)
    x1, xhat1, rstd1 = _mix_ln1(ycn, yan, w_out_f, xs, ln1_g, ln1_b)
    dpre2, ln2_sums, loss_sum, relu_up, hid = _mlp_fwd_loss(x1, w_up_f, w_down_f, target, ln2_g, ln2_b)

    dup, dpre1, ln1_sums = _mlp_bwd_ln1(relu_up, dpre2, w_up_f, w_down_f, xhat1, rstd1, ln1_g)
    gw_up = _grad_tn(x1, dup, "grad_w_up", FF_SHARD, True)
    gw_down = [g.reshape(N_CHIPS, FF_SHARD, D_MODEL) for g in _grad_tn(hid, dpre2, "grad_w_down", D_MODEL, False)]
    gw_out_conv = _grad_tn(ycn, dpre1, "grad_w_out_conv", D_MODEL, False)
    gw_out_attn = _grad_tn(yan, dpre1, "grad_w_out_attn", D_MODEL, False)
    gw_out = [jnp.concatenate([gc_, ga_], axis=0).reshape(N_CHIPS, D_MODEL // N_CHIPS, D_MODEL)
              for gc_, ga_ in zip(gw_out_conv, gw_out_attn)]
    dycn, dyan = _dmix(dpre1, w_out_f)
    dq, dk, dv, gattn_sums, recv_out, recv_up, recv_down = _attn_bwd(
        qkv, o, tot, dyan, g_attn, tri_earlier, gmat, cut, [gw_out[1], gw_up[1], gw_down[1]])
    dbg, dy, conv_sums = _conv_bwd_gate(gates, dycn, taps, g_conv, gmat)
    dproj = _dproj_assemble(gates, dy, dbg, dq, dk, dv, taps)
    gw_in = _grad_tn(xs, dproj, "grad_w_in", IN_SHARD, True)
    halves = lambda g: g.reshape(N_CHIPS, 2, g.shape[1] // 2, g.shape[2])
    grad_x, recv_in, p_out, p_up, p_down = _grad_x(
        kc_idx, dproj, w_in_f, dpre1, gw_in[1],
        [(halves(gw_out[0]), recv_out), (halves(gw_up[0]), recv_up), (halves(gw_down[0]), recv_down)])
    pieces = [_sum_partials(kc_idx, halves(gw_in[0]), recv_in, "sum_partials_w_in"), p_out, p_up, p_down]
    conv_rows = jnp.transpose(conv_sums[0:3].reshape(3, N_CHIPS, LANES), (1, 0, 2)).reshape(3 * N_CHIPS, LANES)
    small = jnp.concatenate([
        loss_sum, _rows(conv_sums[3]), _rows(gattn_sums[0]), _rows(ln1_sums[0]), _rows(ln1_sums[1]),
        _rows(ln2_sums[0]), _rows(ln2_sums[1]), conv_rows,
        jnp.zeros((SMALL_ROWS - ROW_CONVW - 3 * N_CHIPS, LANES), F32)], axis=0)
    g_w_in, g_w_out, g_w_up, g_w_down, total = _finish_exchange(pieces, small)
    loss = total[ROW_LOSS, 0]
    g_conv_w = lax.dynamic_slice(total, (ROW_CONVW + 3 * k_idx, 0), (3, LANES))

    def pack(gc_, ga_, l1g, l1b, l2g, l2b, cw):
        return jnp.concatenate([_rows(gc_), _rows(ga_), _rows(l1g), _rows(l1b), _rows(l2g), _rows(l2b), cw[0],
                                jnp.zeros((PARAM_ROWS + SUBLANES - ROW_CONVW - 3, LANES), F32)], axis=0)

    small_w = pack(g_conv, g_attn, ln1_g, ln1_b, ln2_g, ln2_b, conv_w)
    small_m = pack(m_g_conv, m_g_attn, m_ln1_g, m_ln1_b, m_ln2_g, m_ln2_b, m_conv_w)
    small_v = pack(v_g_conv, v_g_attn, v_ln1_g, v_ln1_b, v_ln2_g, v_ln2_b, v_conv_w)
    small_g = jnp.concatenate([total[ROW_GCONV:ROW_CONVW], g_conv_w,
                               jnp.zeros((PARAM_ROWS + SUBLANES - ROW_CONVW - 3, LANES), F32)], axis=0)
    small_out = _adamw(small_w, small_g, small_m, small_v, "adamw_small")

    def unpack(p):
        off = ROW_GCONV
        vec = lambda a, b: p[a - off:b - off].reshape(1, -1)
        return {"g_conv": vec(ROW_GCONV, ROW_GATTN), "g_attn": vec(ROW_GATTN, ROW_LN1G),
                "ln1_g": vec(ROW_LN1G, ROW_LN1B), "ln1_b": vec(ROW_LN1B, ROW_LN2G),
                "ln2_g": vec(ROW_LN2G, ROW_LN2B), "ln2_b": vec(ROW_LN2B, ROW_CONVW),
                "conv_w": p[ROW_CONVW - off:ROW_CONVW - off + 3][None]}

    big_out = {
        "w_in": _adamw(w_in[0], g_w_in, m_w_in[0], v_w_in[0], "adamw_w_in"),
        "w_out": _adamw(w_out[0], g_w_out, m_w_out[0], v_w_out[0], "adamw_w_out"),
        "w_up": _adamw(w_up[0], g_w_up, m_w_up[0], v_w_up[0], "adamw_w_up"),
        "w_down": _adamw(w_down[0], g_w_down, m_w_down[0], v_w_down[0], "adamw_w_down"),
    }
    order = ["w_in", "conv_w", "g_conv", "g_attn", "w_out", "ln1_g", "ln1_b", "w_up", "w_down", "ln2_g", "ln2_b"]
    small_parts = [unpack(p) for p in small_out]

    def leaf(kind, name):
        if name in big_out:
            return big_out[name][kind][None]
        return small_parts[kind][name]

    outs = [loss, grad_x[None]]
    for kind in range(4):
        outs.extend(leaf(kind, name) for name in order)
    return tuple(outs)
```

```python
import functools

import jax
import jax.numpy as jnp
from jax import lax
from jax.experimental import pallas as pl
from jax.experimental.pallas import tpu as pltpu

F32 = jnp.float32
BF16 = jnp.bfloat16
SDS = jax.ShapeDtypeStruct

D_MODEL = 1024
CONV_WIDTH = 512
ATTN_WIDTH = 512
GROUP = 64
GATE_COLS = 3 * CONV_WIDTH
QKV_COLS = 3 * ATTN_WIDTH
IN_COLS = GATE_COLS + QKV_COLS
D_FF = 4 * D_MODEL
N_CHIPS = 4
IN_SHARD = IN_COLS // N_CHIPS
FF_SHARD = D_FF // N_CHIPS
ALPHA = float(2.0 ** 0.25)
LN_EPS = 1e-5
RMS_EPS = 1e-6
ATTN_SCALE = GROUP ** -0.5
LOG2_E = 1.4426950408889634
ADAM_LR = 0.001
ADAM_B1 = 0.9
ADAM_B2 = 0.999
ADAM_EPS = 1e-08
ADAM_WD = 0.01
ADAM_STEP = 10

LANES = 128
SUBLANES = 8
KEY_BLOCK = 128
ATTN_Q_TILE = 512
ATTN_KEY_BLOCKS = 2
ATTN_DIAG_GROUPS = 2
ATTN_DEAD_LOG2 = 200.0
VMEM_LIMIT = 56 * 1024 * 1024

MESH = pl.DeviceIdType.MESH
CHIP_FLIPS = ((1, 0), (0, 1), (1, 1))
DEVICE_FLIPS = tuple((fx, fy, fc) for fx in (0, 1) for fy in (0, 1) for fc in (0, 1))[1:]
NT_DIMS = (((1,), (1,)), ((), ()))
TN_DIMS = (((0,), (0,)), ((), ()))

ROW_LOSS = 0
ROW_GCONV = 8
ROW_GATTN = 12
ROW_LN1G = 16
ROW_LN1B = 24
ROW_LN2G = 32
ROW_LN2B = 40
ROW_CONVW = 48
SMALL_ROWS = 64
PARAM_ROWS = 48


def _params(sem=None):
    return pltpu.CompilerParams(dimension_semantics=sem, vmem_limit_bytes=VMEM_LIMIT)


def _flip(v, f):
    return 1 - v if f else v


def _position():
    return lax.axis_index("x"), lax.axis_index("y"), lax.axis_index("c")


def _hilo(v):
    hi = v.astype(BF16)
    lo = (v - hi.astype(F32)).astype(BF16)
    return jnp.concatenate([hi, lo], axis=1)


def _hilo_dot(v, mat):
    return jnp.dot(_hilo(v), mat, preferred_element_type=F32)


def _group_sum(v, gmat):
    parts = [_hilo_dot(v[:, LANES * j:LANES * (j + 1)], gmat) for j in range(v.shape[1] // LANES)]
    return parts[0] if len(parts) == 1 else jnp.concatenate(parts, axis=1)


def _softplus_terms(z):
    sp = jnp.log2(1.0 + jnp.exp2(-jnp.abs(z)))
    log_beta = jnp.minimum(z, 0.0) - sp
    return log_beta, log_beta - z


def _layer_norm_fwd(pre, g, b):
    mu = jnp.mean(pre, axis=-1, keepdims=True)
    d = pre - mu
    var = jnp.mean(d * d, axis=-1, keepdims=True)
    rstd = lax.rsqrt(var + LN_EPS)
    xhat = d * rstd
    return xhat * g + b, xhat, rstd


def _layer_norm_bwd(dy, xhat, rstd, g):
    dxh = dy * g
    m1 = jnp.mean(dxh, axis=-1, keepdims=True)
    m2 = jnp.mean(dxh * xhat, axis=-1, keepdims=True)
    return rstd * (dxh - m1 - xhat * m2)


def _row_tile(s, want):
    return min(s, want)


def _cast_into_slot(kc_idx, w, name):
    r, c = w.shape
    tr = _row_tile(r, 256)

    def body(kc_ref, w_ref, o_ref):
        o_ref[...] = w_ref[...].astype(BF16)

    grid_spec = pltpu.PrefetchScalarGridSpec(
        num_scalar_prefetch=1, grid=(r // tr,),
        in_specs=[pl.BlockSpec((tr, c), lambda i, kc: (i, 0))],
        out_specs=pl.BlockSpec((None, tr, c), lambda i, kc: (kc[0], i, 0)))
    return pl.pallas_call(
        body, name=name, grid_spec=grid_spec, out_shape=SDS((N_CHIPS, r, c), BF16),
        compiler_params=_params(("parallel",)))(kc_idx, w)


def _proj(x, w_in):
    s = x.shape[0]
    tm = _row_tile(s, 512)

    def body(x_ref, w_ref, gates_ref, qkv_ref, xb_ref):
        xb = x_ref[...].astype(BF16)
        xb_ref[...] = xb
        for k in range(N_CHIPS):
            acc = jnp.dot(xb, w_ref[k], preferred_element_type=F32)
            if k < 2:
                gates_ref[:, IN_SHARD * k:IN_SHARD * (k + 1)] = acc
            else:
                qkv_ref[:, IN_SHARD * (k - 2):IN_SHARD * (k - 1)] = acc.astype(BF16)

    return pl.pallas_call(
        body, name="proj", grid=(s // tm,),
        in_specs=[pl.BlockSpec((tm, D_MODEL), lambda i: (i, 0)),
                  pl.BlockSpec((N_CHIPS, D_MODEL, IN_SHARD), lambda i: (0, 0, 0))],
        out_specs=[pl.BlockSpec((tm, GATE_COLS), lambda i: (i, 0)),
                   pl.BlockSpec((tm, QKV_COLS), lambda i: (i, 0)),
                   pl.BlockSpec((tm, D_MODEL), lambda i: (i, 0))],
        out_shape=[SDS((s, GATE_COLS), F32), SDS((s, QKV_COLS), BF16), SDS((s, D_MODEL), BF16)],
        compiler_params=_params(("parallel",)))(x, w_in)


def _conv_forward_values(g_ref, halo_ref, taps_ref, first_block):
    gates = g_ref[...]
    tr = gates.shape[0]
    bg = gates[:, :CONV_WIDTH]
    cg = gates[:, CONV_WIDTH:2 * CONV_WIDTH]
    h = gates[:, 2 * CONV_WIDTH:]
    u = cg * h

    def prev(r):
        v = halo_ref[r:r + 1, CONV_WIDTH:2 * CONV_WIDTH] * halo_ref[r:r + 1, 2 * CONV_WIDTH:GATE_COLS]
        return jnp.where(first_block, 0.0, v)

    row = lax.broadcasted_iota(jnp.int32, (tr, CONV_WIDTH), 0)
    u1 = jnp.where(row == 0, prev(7), pltpu.roll(u, 1, 0))
    u2 = jnp.where(row == 0, prev(6), jnp.where(row == 1, prev(7), pltpu.roll(u, 2, 0)))
    y = taps_ref[0:1, :] * u2 + taps_ref[1:2, :] * u1 + taps_ref[2:3, :] * u
    return bg, cg, h, u, u1, u2, y


def _conv_fwd(gates, taps, g_conv, gmat):
    s = gates.shape[0]
    tr = _row_tile(s, 512)
    hb = tr // SUBLANES

    def body(g_ref, halo_ref, taps_ref, gain_ref, gmat_ref, out_ref):
        i = pl.program_id(0)
        bg, _, _, _, _, _, y = _conv_forward_values(g_ref, halo_ref, taps_ref, i == 0)
        yc = bg * y
        ms = _group_sum(yc * yc, gmat_ref[...]) * (1.0 / GROUP)
        out_ref[...] = (yc * lax.rsqrt(ms + RMS_EPS) * gain_ref[...]).astype(BF16)

    return pl.pallas_call(
        body, name="conv_fwd", grid=(s // tr,),
        in_specs=[pl.BlockSpec((tr, GATE_COLS), lambda i: (i, 0)),
                  pl.BlockSpec((SUBLANES, GATE_COLS), lambda i: (jnp.maximum(i * hb - 1, 0), 0)),
                  pl.BlockSpec((SUBLANES, CONV_WIDTH), lambda i: (0, 0)),
                  pl.BlockSpec((1, CONV_WIDTH), lambda i: (0, 0)),
                  pl.BlockSpec((2 * LANES, LANES), lambda i: (0, 0))],
        out_specs=pl.BlockSpec((tr, CONV_WIDTH), lambda i: (i, 0)),
        out_shape=SDS((s, CONV_WIDTH), BF16),
        compiler_params=_params(("parallel",)))(gates, gates, taps, g_conv, gmat)


def _conv_bwd_gate(gates, dycn, taps, g_conv, gmat):
    s = gates.shape[0]
    tr = _row_tile(s, 512)
    hb = tr // SUBLANES

    def body(g_ref, halo_ref, dn_ref, taps_ref, gain_ref, gmat_ref, dbg_ref, dy_ref, sums_ref):
        i = pl.program_id(0)
        bg, _, _, u, u1, u2, y = _conv_forward_values(g_ref, halo_ref, taps_ref, i == 0)
        gmat_v = gmat_ref[...]
        yc = bg * y
        rstd = lax.rsqrt(_group_sum(yc * yc, gmat_v) * (1.0 / GROUP) + RMS_EPS)
        n = yc * rstd
        dout = dn_ref[...]
        dn = dout * gain_ref[...]
        dyc = rstd * (dn - n * (_group_sum(dn * n, gmat_v) * (1.0 / GROUP)))
        dbg_ref[...] = (dyc * y).astype(BF16)
        dy = dyc * bg
        dy_ref[...] = dy

        @pl.when(i == 0)
        def _():
            sums_ref[...] = jnp.zeros_like(sums_ref)

        sums_ref[0:1, :] += jnp.sum(dy * u2, axis=0, keepdims=True)
        sums_ref[1:2, :] += jnp.sum(dy * u1, axis=0, keepdims=True)
        sums_ref[2:3, :] += jnp.sum(dy * u, axis=0, keepdims=True)
        sums_ref[3:4, :] += jnp.sum(dout * n, axis=0, keepdims=True)

    return pl.pallas_call(
        body, name="conv_bwd_gate", grid=(s // tr,),
        in_specs=[pl.BlockSpec((tr, GATE_COLS), lambda i: (i, 0)),
                  pl.BlockSpec((SUBLANES, GATE_COLS), lambda i: (jnp.maximum(i * hb - 1, 0), 0)),
                  pl.BlockSpec((tr, CONV_WIDTH), lambda i: (i, 0)),
                  pl.BlockSpec((SUBLANES, CONV_WIDTH), lambda i: (0, 0)),
                  pl.BlockSpec((1, CONV_WIDTH), lambda i: (0, 0)),
                  pl.BlockSpec((2 * LANES, LANES), lambda i: (0, 0))],
        out_specs=[pl.BlockSpec((tr, CONV_WIDTH), lambda i: (i, 0)),
                   pl.BlockSpec((tr, CONV_WIDTH), lambda i: (i, 0)),
                   pl.BlockSpec((SUBLANES, CONV_WIDTH), lambda i: (0, 0))],
        out_shape=[SDS((s, CONV_WIDTH), BF16), SDS((s, CONV_WIDTH), F32), SDS((SUBLANES, CONV_WIDTH), F32)],
        compiler_params=_params(("arbitrary",)))(gates, gates, dycn, taps, g_conv, gmat)


def _dproj_assemble(gates, dy, dbg, dq, dk, dv, taps):
    s = gates.shape[0]
    tr = _row_tile(s, 512)
    hb = tr // SUBLANES
    last = s // SUBLANES - 1
    n_blocks = s // tr

    def body(g_ref, dy_ref, halo_ref, dbg_ref, dq_ref, dk_ref, dv_ref, taps_ref, out_ref):
        i = pl.program_id(0)
        gates_v = g_ref[...]
        cg = gates_v[:, CONV_WIDTH:2 * CONV_WIDTH]
        h = gates_v[:, 2 * CONV_WIDTH:]
        dy_v = dy_ref[...]
        last_block = i == n_blocks - 1
        nxt = lambda r: jnp.where(last_block, 0.0, halo_ref[r:r + 1, :])
        row = lax.broadcasted_iota(jnp.int32, (tr, CONV_WIDTH), 0)
        d1 = jnp.where(row == tr - 1, nxt(0), pltpu.roll(dy_v, tr - 1, 0))
        d2 = jnp.where(row == tr - 1, nxt(1), jnp.where(row == tr - 2, nxt(0), pltpu.roll(dy_v, tr - 2, 0)))
        du = taps_ref[2:3, :] * dy_v + taps_ref[1:2, :] * d1 + taps_ref[0:1, :] * d2
        out_ref[:, 0:CONV_WIDTH] = dbg_ref[...]
        out_ref[:, CONV_WIDTH:2 * CONV_WIDTH] = (du * h).astype(BF16)
        out_ref[:, 2 * CONV_WIDTH:GATE_COLS] = (du * cg).astype(BF16)
        out_ref[:, GATE_COLS:GATE_COLS + ATTN_WIDTH] = dq_ref[...]
        out_ref[:, GATE_COLS + ATTN_WIDTH:GATE_COLS + 2 * ATTN_WIDTH] = dk_ref[...].astype(BF16)
        out_ref[:, GATE_COLS + 2 * ATTN_WIDTH:] = dv_ref[...].astype(BF16)

    row_spec = lambda w: pl.BlockSpec((tr, w), lambda i: (i, 0))
    return pl.pallas_call(
        body, name="dproj_assemble", grid=(s // tr,),
        in_specs=[row_spec(GATE_COLS), row_spec(CONV_WIDTH),
                  pl.BlockSpec((SUBLANES, CONV_WIDTH), lambda i: (jnp.minimum((i + 1) * hb, last), 0)),
                  row_spec(CONV_WIDTH), row_spec(ATTN_WIDTH), row_spec(ATTN_WIDTH), row_spec(ATTN_WIDTH),
                  pl.BlockSpec((SUBLANES, CONV_WIDTH), lambda i: (0, 0))],
        out_specs=row_spec(IN_COLS),
        out_shape=SDS((s, IN_COLS), BF16),
        compiler_params=_params(("parallel",)))(gates, dy, dy, dbg, dq, dk, dv, taps)


def _stack_heads(rows, nb):
    lane = lax.broadcasted_iota(jnp.int32, (1, LANES), 1)
    zero = jnp.zeros((KEY_BLOCK, LANES), rows.dtype)
    parts = []
    for blk in range(nb):
        r = rows[blk * KEY_BLOCK:(blk + 1) * KEY_BLOCK]
        parts.append(jnp.where(lane < GROUP, r, zero))
        parts.append(jnp.where(lane < GROUP, zero, r))
    return jnp.concatenate(parts, axis=0)


def _stack_hilo(v, n_cols):
    return jnp.concatenate([_hilo(v[:, c * KEY_BLOCK:(c + 1) * KEY_BLOCK]) for c in range(n_cols)], axis=0)


def _causal_mask(tq, nb, diag_base):
    shape = (tq, 2 * nb * KEY_BLOCK)
    row = lax.broadcasted_iota(jnp.int32, shape, 0)
    col = lax.broadcasted_iota(jnp.int32, shape, 1)
    key = diag_base + (col // (2 * KEY_BLOCK)) * KEY_BLOCK + col % KEY_BLOCK
    return key < row


ANY = pl.BlockSpec(memory_space=pl.ANY)


def _remote_copy(src, dst, sems, idx, target):
    return pltpu.make_async_remote_copy(src_ref=src, dst_ref=dst, send_sem=sems[0].at[idx], recv_sem=sems[1].at[idx],
                                        device_id=target, device_id_type=MESH)


def _gather_chip_hop(bufs, sems):
    x, y, c = _position()
    sends, arrivals = [], []
    for a, buf in enumerate(bufs):
        h = buf.shape[1] // 2
        rows = pl.ds(pl.multiple_of(c * h, h), h)
        mine = buf.at[2 * x + y, rows]
        for j, (fx, fy) in enumerate(CHIP_FLIPS):
            tx, ty = _flip(x, fx), _flip(y, fy)
            there = buf.at[2 * tx + ty, rows]
            sends.append(_remote_copy(mine, mine, sems, 6 * a + j, (tx, ty, c)))
            arrivals.append(_remote_copy(there, there, sems, 6 * a + j, (tx, ty, c)))
    return sends, arrivals


def _gather_sibling_hop(bufs, sems):
    x, y, c = _position()
    sends, arrivals = [], []
    for a, buf in enumerate(bufs):
        h = buf.shape[1] // 2
        mine, theirs = pl.ds(pl.multiple_of(c * h, h), h), pl.ds(pl.multiple_of((1 - c) * h, h), h)
        for j, (fx, fy) in enumerate(CHIP_FLIPS):
            kj = 2 * _flip(x, fx) + _flip(y, fy)
            landed, other = buf.at[kj, mine], buf.at[kj, theirs]
            sends.append(_remote_copy(landed, landed, sems, 6 * a + 3 + j, (x, y, 1 - c)))
            arrivals.append(_remote_copy(other, other, sems, 6 * a + 3 + j, (x, y, 1 - c)))
    return sends, arrivals


def _reduce_copies(ins, outs, sems):
    x, y, c = _position()
    sends, arrivals = [], []
    for a in range(len(ins)):
        h = ins[a].shape[1] // 2
        for f, (fx, fy, fc) in enumerate(DEVICE_FLIPS):
            tx, ty, tc = _flip(x, fx), _flip(y, fy), _flip(c, fc)
            src = ins[a].at[2 * tx + ty, pl.ds(pl.multiple_of(tc * h, h), h)]
            sends.append(_remote_copy(src, outs[a].at[f], sems, 7 * a + f, (tx, ty, tc)))
            arrivals.append(_remote_copy(outs[a].at[f], outs[a].at[f], sems, 7 * a + f, (tx, ty, tc)))
    return sends, arrivals


def _start_copies(make):
    sends, _ = make()
    for cp in sends:
        cp.start()


def _finish_copies(make):
    sends, arrivals = make()
    for cp in arrivals:
        cp.wait_recv()
    for cp in sends:
        cp.wait_send()


def _attn_fwd(qkv, g_attn, tri, gmat, shards):
    n_w = len(shards)
    s = qkv.shape[0]
    tq = _row_tile(s, ATTN_Q_TILE)
    tk = KEY_BLOCK
    nb = ATTN_KEY_BLOCKS
    width = nb * tk
    n_groups = ATTN_DIAG_GROUPS
    group = tq // n_groups
    pairs = ATTN_WIDTH // LANES

    def body(q_ref, k_ref, v_ref, gain_ref, tri_ref, gmat_ref, *rest):
        o_ref, yn_ref, tot_ref, cut_ref = rest[n_w:n_w + 4]
        w_bufs, sems = rest[n_w + 4:2 * n_w + 4], rest[2 * n_w + 4:]
        chip_hop = functools.partial(_gather_chip_hop, w_bufs, sems)
        sibling_hop = functools.partial(_gather_sibling_hop, w_bufs, sems)
        p, i = pl.program_id(0), pl.program_id(1)
        pl.when((p == 0) & (i == 0))(functools.partial(_start_copies, chip_hop))

        @pl.when((p == pairs - 1) & (i == 0))
        def _():
            for cp in chip_hop()[1]:
                cp.wait_recv()
            _start_copies(sibling_hop)

        q2 = q_ref[...]
        tri_v = tri_ref[...]

        def trip(s0, n_blk, rows, carry, diag_base):
            r0, nr = rows
            run = [carry[0], carry[1]]
            oacc = carry[2]
            ksel = _stack_heads(k_ref[pl.ds(s0, n_blk * tk), :], n_blk)
            vsel = _stack_heads(v_ref[pl.ds(s0, n_blk * tk), :], n_blk)
            z = lax.dot_general(q2[r0:r0 + nr], ksel, NT_DIMS, preferred_element_type=F32) * (ATTN_SCALE * LOG2_E)
            log_beta, log_keep = _softplus_terms(z)
            if diag_base is not None:
                valid = _causal_mask(nr, n_blk, diag_base)
                log_keep = jnp.where(valid, log_keep, 0.0)
            ct = jnp.dot(_stack_hilo(log_keep, 2 * n_blk), tri_v, preferred_element_type=F32)
            a_parts = [None] * (2 * n_blk)
            for c in reversed(range(2 * n_blk)):
                h = c % 2
                ct_c = ct[c * nr:(c + 1) * nr]
                a_parts[c] = jnp.exp2(log_beta[:, c * tk:(c + 1) * tk] + ct_c[:, :tk] + run[h])
                run[h] = run[h] + ct_c[:, tk:]
            a = jnp.concatenate(a_parts, axis=1)
            if diag_base is not None:
                a = jnp.where(valid, a, 0.0)
            oacc = oacc + jnp.dot(a.astype(BF16), vsel, preferred_element_type=F32)
            return run[0], run[1], oacc

        groups = []
        for g in range(n_groups):
            zeros = (jnp.zeros((group, tk), F32), jnp.zeros((group, tk), F32), jnp.zeros((group, LANES), F32))
            groups.append(trip(pl.multiple_of(i * tq, tq), (g + 1) * group // tk, (g * group, group), zeros,
                               -g * group))
        carry = tuple(jnp.concatenate([grp[j] for grp in groups], axis=0) for j in range(3))
        n_full = i * (tq // width)

        def alive(run_a, run_b):
            return jnp.max(jnp.maximum(run_a, run_b)) > -ATTN_DEAD_LOG2

        def earlier_trip(c):
            done, _, run_a, run_b, oacc = c
            s0 = pl.multiple_of((n_full - 1 - done) * width, width)
            run_a, run_b, oacc = trip(s0, nb, (0, tq), (run_a, run_b, oacc), None)
            return done + 1, alive(run_a, run_b), run_a, run_b, oacc

        done, _, run_a, run_b, oacc = lax.while_loop(
            lambda c: (c[0] < n_full) & c[1], earlier_trip, (jnp.int32(0), alive(carry[0], carry[1])) + carry)
        cut_ref[p * pl.num_programs(1) + i] = (n_full - done).astype(F32)
        lane = lax.broadcasted_iota(jnp.int32, (1, LANES), 1)
        o_ref[...] = oacc
        tot_ref[...] = jnp.where(lane < GROUP, run_a, run_b)
        ms = _group_sum(oacc * oacc, gmat_ref[...]) * (1.0 / GROUP)
        yn_ref[...] = (oacc * lax.rsqrt(ms + RMS_EPS) * gain_ref[...]).astype(BF16)

        @pl.when((p == pairs - 1) & (i == pl.num_programs(1) - 1))
        def _():
            for cp in chip_hop()[0]:
                cp.wait_send()
            _finish_copies(sibling_hop)

    blk = lambda: pl.BlockSpec((tq, LANES), lambda p, i: (i, p))
    return pl.pallas_call(
        body, name="attn_fwd", grid=(pairs, s // tq),
        in_specs=[pl.BlockSpec((tq, LANES), lambda p, i: (i, p)),
                  pl.BlockSpec((s, LANES), lambda p, i: (0, pairs + p)),
                  pl.BlockSpec((s, LANES), lambda p, i: (0, 2 * pairs + p)),
                  pl.BlockSpec((1, LANES), lambda p, i: (0, p)),
                  pl.BlockSpec((2 * tk, 2 * tk), lambda p, i: (0, 0)),
                  pl.BlockSpec((2 * LANES, LANES), lambda p, i: (0, 0))] + [ANY] * n_w,
        out_specs=[blk(), blk(), blk(), pl.BlockSpec(memory_space=pltpu.SMEM)] + [ANY] * n_w,
        out_shape=[SDS((s, ATTN_WIDTH), F32), SDS((s, ATTN_WIDTH), BF16), SDS((s, ATTN_WIDTH), F32),
                   SDS((pairs * (s // tq),), F32)]
        + [SDS(w.shape, w.dtype) for w in shards],
        input_output_aliases={6 + a: 4 + a for a in range(n_w)},
        scratch_shapes=[pltpu.SemaphoreType.DMA((6 * n_w,)), pltpu.SemaphoreType.DMA((6 * n_w,))],
        compiler_params=_params(("arbitrary", "arbitrary")))(qkv, qkv, qkv, g_attn, tri, gmat, *shards)


def _attn_bwd(qkv, o, tot, dyn, g_attn, tri, gmat, cut, partials):
    n_g = len(partials)
    s = qkv.shape[0]
    tq = _row_tile(s, ATTN_Q_TILE)
    tk = KEY_BLOCK
    nb = ATTN_KEY_BLOCKS
    width = nb * tk
    n_groups = ATTN_DIAG_GROUPS
    group = tq // n_groups
    pairs = ATTN_WIDTH // LANES

    def body(q_ref, k_ref, v_ref, o_ref, tot_ref, dyn_ref, gain_ref, tri_ref, gmat_ref, cut_ref, *rest):
        g_ins, (dq_ref, dk_ref, dv_ref, dg_ref) = rest[:n_g], rest[n_g:n_g + 4]
        g_outs, sems = rest[n_g + 4:2 * n_g + 4], rest[2 * n_g + 4:]
        copies = functools.partial(_reduce_copies, g_ins, g_outs, sems)
        p, i = pl.program_id(0), pl.program_id(1)
        pl.when((p == 0) & (i == 0))(functools.partial(_start_copies, copies))

        @pl.when(i == 0)
        def _():
            dk_ref[...] = jnp.zeros_like(dk_ref)
            dv_ref[...] = jnp.zeros_like(dv_ref)
            dg_ref[...] = jnp.zeros_like(dg_ref)

        gmat_v = gmat_ref[...]
        o_v = o_ref[...]
        rstd = lax.rsqrt(_group_sum(o_v * o_v, gmat_v) * (1.0 / GROUP) + RMS_EPS)
        n = o_v * rstd
        dout = dyn_ref[...]
        dg_ref[0:1, :] += jnp.sum(dout * n, axis=0, keepdims=True)
        dn = dout * gain_ref[...]
        do2 = (rstd * (dn - n * (_group_sum(dn * n, gmat_v) * (1.0 / GROUP)))).astype(BF16)
        q2 = q_ref[...]
        tot_v = tot_ref[...]
        tots = (jnp.broadcast_to(tot_v[:, 0:1], (tq, tk)), jnp.broadcast_to(tot_v[:, GROUP:GROUP + 1], (tq, tk)))
        tri_v, tri_incl_v = tri_ref[0], tri_ref[1]
        lane = lax.broadcasted_iota(jnp.int32, (1, LANES), 1)

        def trip(s0, n_blk, rows, carry, diag_base):
            r0, nr = rows
            rest_l = [carry[0], carry[1]]
            pref_g = [carry[2], carry[3]]
            dq = carry[4]
            q_rows, do_rows = q2[r0:r0 + nr], do2[r0:r0 + nr]
            ksel = _stack_heads(k_ref[pl.ds(s0, n_blk * tk), :], n_blk)
            vsel = _stack_heads(v_ref[pl.ds(s0, n_blk * tk), :], n_blk)
            z = lax.dot_general(q_rows, ksel, NT_DIMS, preferred_element_type=F32) * (ATTN_SCALE * LOG2_E)
            log_beta, log_keep = _softplus_terms(z)
            if diag_base is not None:
                valid = _causal_mask(nr, n_blk, diag_base)
                log_keep = jnp.where(valid, log_keep, 0.0)
            ctl = jnp.dot(_stack_hilo(log_keep, 2 * n_blk), tri_incl_v, preferred_element_type=F32)
            da = lax.dot_general(do_rows, vsel, NT_DIMS, preferred_element_type=F32)
            a_parts = []
            for c in range(2 * n_blk):
                h = c % 2
                ct_c = ctl[c * nr:(c + 1) * nr]
                cols = slice(c * tk, (c + 1) * tk)
                a_parts.append(jnp.exp2(log_beta[:, cols] + (rest_l[h] - ct_c[:, :tk])))
                rest_l[h] = rest_l[h] - ct_c[:, tk:]
            a = jnp.concatenate(a_parts, axis=1)
            if diag_base is not None:
                a = jnp.where(valid, a, 0.0)
            g = a * da
            ctg = jnp.dot(_stack_hilo(g, 2 * n_blk), tri_v, preferred_element_type=F32)
            dz_parts = []
            for c in range(2 * n_blk):
                h = c % 2
                ct_c = ctg[c * nr:(c + 1) * nr]
                cols = slice(c * tk, (c + 1) * tk)
                prefix = pref_g[h] + ct_c[:, :tk]
                pref_g[h] = pref_g[h] + ct_c[:, tk:]
                g_c = g[:, cols]
                dz_parts.append(g_c - jnp.exp2(log_beta[:, cols]) * (g_c + prefix))
            dz = jnp.concatenate(dz_parts, axis=1) * ATTN_SCALE
            if diag_base is not None:
                dz = jnp.where(valid, dz, 0.0)
            dzb = dz.astype(BF16)
            dq = dq + jnp.dot(dzb, ksel, preferred_element_type=F32)
            dkt = lax.dot_general(dzb, q_rows, TN_DIMS, preferred_element_type=F32)
            dvt = lax.dot_general(a.astype(BF16), do_rows, TN_DIMS, preferred_element_type=F32)
            for blk in range(n_blk):
                ra, rb = slice(2 * blk * tk, (2 * blk + 1) * tk), slice((2 * blk + 1) * tk, (2 * blk + 2) * tk)
                keys = pl.ds(pl.multiple_of(s0 + blk * tk, tk), tk)
                dk_ref[keys, :] += jnp.where(lane < GROUP, dkt[ra], dkt[rb])
                dv_ref[keys, :] += jnp.where(lane < GROUP, dvt[ra], dvt[rb])
            return rest_l[0], rest_l[1], pref_g[0], pref_g[1], dq

        zeros_qk = jnp.zeros((tq, tk), F32)
        carry = (tots[0], tots[1], zeros_qk, zeros_qk, jnp.zeros((tq, LANES), F32))
        n_full = i * (tq // width)
        first = jnp.clip(cut_ref[p * pl.num_programs(1) + i].astype(jnp.int32), 0, n_full)
        carry = lax.fori_loop(
            first, n_full, lambda t, c: trip(pl.multiple_of(t * width, width), nb, (0, tq), c, None), carry)
        dq_groups = []
        for g in range(n_groups):
            sub = tuple(x[g * group:(g + 1) * group] for x in carry)
            dq_groups.append(trip(pl.multiple_of(i * tq, tq), (g + 1) * group // tk, (g * group, group), sub,
                                  -g * group)[4])
        dq_ref[...] = jnp.concatenate(dq_groups, axis=0).astype(BF16)
        pl.when((p == pairs - 1) & (i == pl.num_programs(1) - 1))(functools.partial(_finish_copies, copies))

    blk = lambda: pl.BlockSpec((tq, LANES), lambda p, i: (i, p))
    col = lambda: pl.BlockSpec((s, LANES), lambda p, i: (0, p))
    n_peers = len(DEVICE_FLIPS)
    return pl.pallas_call(
        body, name="attn_bwd", grid=(pairs, s // tq),
        in_specs=[pl.BlockSpec((tq, LANES), lambda p, i: (i, p)),
                  pl.BlockSpec((s, LANES), lambda p, i: (0, pairs + p)),
                  pl.BlockSpec((s, LANES), lambda p, i: (0, 2 * pairs + p)),
                  blk(), blk(), blk(),
                  pl.BlockSpec((1, LANES), lambda p, i: (0, p)),
                  pl.BlockSpec((2, 2 * tk, 2 * tk), lambda p, i: (0, 0, 0)),
                  pl.BlockSpec((2 * LANES, LANES), lambda p, i: (0, 0)),
                  pl.BlockSpec(memory_space=pltpu.SMEM)] + [ANY] * n_g,
        out_specs=[blk(), col(), col(), pl.BlockSpec((SUBLANES, LANES), lambda p, i: (0, p))] + [ANY] * n_g,
        out_shape=[SDS((s, ATTN_WIDTH), BF16), SDS((s, ATTN_WIDTH), F32), SDS((s, ATTN_WIDTH), F32),
                   SDS((SUBLANES, ATTN_WIDTH), F32)]
        + [SDS((n_peers, g.shape[1] // 2, g.shape[2]), g.dtype) for g in partials],
        scratch_shapes=[pltpu.SemaphoreType.DMA((n_peers * n_g,)), pltpu.SemaphoreType.DMA((n_peers * n_g,))],
        compiler_params=_params(("arbitrary", "arbitrary")))(
            qkv, qkv, qkv, o, tot, dyn, g_attn, tri, gmat, cut, *partials)


def _mix_ln1(ycn, yan, w_out, x, g, b):
    s = x.shape[0]
    tm = _row_tile(s, 512)

    def body(yc_ref, ya_ref, w_ref, x_ref, g_ref, b_ref, x1_ref, xhat_ref, rstd_ref, x1b_ref):
        mix = jnp.dot(yc_ref[...], w_ref[0:CONV_WIDTH, :], preferred_element_type=F32)
        mix = mix + jnp.dot(ya_ref[...], w_ref[CONV_WIDTH:, :], preferred_element_type=F32)
        x1, xhat, rstd = _layer_norm_fwd(ALPHA * x_ref[...] + mix, g_ref[...], b_ref[...])
        x1_ref[...] = x1
        xhat_ref[...] = xhat
        rstd_ref[...] = rstd
        x1b_ref[...] = x1.astype(BF16)

    row = lambda w: pl.BlockSpec((tm, w), lambda i: (i, 0))
    vec = lambda: pl.BlockSpec((1, D_MODEL), lambda i: (0, 0))
    return pl.pallas_call(
        body, name="mix_ln1", grid=(s // tm,),
        in_specs=[row(CONV_WIDTH), row(ATTN_WIDTH), pl.BlockSpec((D_MODEL, D_MODEL), lambda i: (0, 0)),
                  row(D_MODEL), vec(), vec()],
        out_specs=[row(D_MODEL), row(D_MODEL), row(1), row(D_MODEL)],
        out_shape=[SDS((s, D_MODEL), F32), SDS((s, D_MODEL), F32), SDS((s, 1), F32), SDS((s, D_MODEL), BF16)],
        compiler_params=_params(("parallel",)))(ycn, yan, w_out, x, g, b)


def _mlp_fwd_loss(x1, w_up, w_down, target, g, b):
    s = x1.shape[0]
    tm = _row_tile(s, 256)

    def body(x1_ref, wu_ref, wd_ref, t_ref, g_ref, b_ref, dpre_ref, sums_ref, loss_ref, r_ref, hid_ref, dpreb_ref):
        i = pl.program_id(0)
        x1_v = x1_ref[...]
        xb = x1_v.astype(BF16)
        ffn = jnp.zeros((tm, D_MODEL), F32)
        for k in range(N_CHIPS):
            r = jnp.maximum(jnp.dot(xb, wu_ref[k], preferred_element_type=F32), 0.0)
            hid = (r * r).astype(BF16)
            r_ref[:, FF_SHARD * k:FF_SHARD * (k + 1)] = r.astype(BF16)
            hid_ref[:, FF_SHARD * k:FF_SHARD * (k + 1)] = hid
            ffn = ffn + jnp.dot(hid, wd_ref[k], preferred_element_type=F32)
        g_v = g_ref[...]
        x2, xhat, rstd = _layer_norm_fwd(ALPHA * x1_v + ffn, g_v, b_ref[...])
        err = x2 - t_ref[...]
        dx2 = err * (1.0 / D_MODEL)
        dpre = _layer_norm_bwd(dx2, xhat, rstd, g_v)
        dpre_ref[...] = dpre
        dpreb_ref[...] = dpre.astype(BF16)

        @pl.when(i == 0)
        def _():
            sums_ref[...] = jnp.zeros_like(sums_ref)
            loss_ref[...] = jnp.zeros_like(loss_ref)

        sums_ref[0:1, :] += jnp.sum(dx2 * xhat, axis=0, keepdims=True)
        sums_ref[1:2, :] += jnp.sum(dx2, axis=0, keepdims=True)
        loss_ref[...] += jnp.sum(jnp.sum(err * err, axis=1, keepdims=True), axis=0, keepdims=True) * (0.5 / D_MODEL)

    row = lambda: pl.BlockSpec((tm, D_MODEL), lambda i: (i, 0))
    wide = lambda: pl.BlockSpec((tm, D_FF), lambda i: (i, 0))
    vec = lambda: pl.BlockSpec((1, D_MODEL), lambda i: (0, 0))
    return pl.pallas_call(
        body, name="mlp_fwd_loss", grid=(s // tm,),
        in_specs=[row(), _resident_weight(), _resident_weight(), row(), vec(), vec()],
        out_specs=[row(), pl.BlockSpec((SUBLANES, D_MODEL), lambda i: (0, 0)),
                   pl.BlockSpec((SUBLANES, LANES), lambda i: (0, 0)), wide(), wide(), row()],
        out_shape=[SDS((s, D_MODEL), F32), SDS((SUBLANES, D_MODEL), F32), SDS((SUBLANES, LANES), F32),
                   SDS((s, D_FF), BF16), SDS((s, D_FF), BF16), SDS((s, D_MODEL), BF16)],
        compiler_params=_params(("arbitrary",)))(x1, w_up, w_down, target, g, b)


def _resident_weight():
    return pl.BlockSpec((N_CHIPS, D_MODEL, FF_SHARD), lambda i: (0, 0, 0), pipeline_mode=pl.Buffered(1))


def _mlp_bwd_ln1(relu_up, dpre2, w_up, w_down, xhat1, rstd1, g1):
    s = dpre2.shape[0]
    tm = _row_tile(s, 256)

    def body(r_ref, d2_ref, wu_ref, wd_ref, xh_ref, rs_ref, g_ref, dup_ref, dpre_ref, sums_ref):
        i = pl.program_id(0)
        d2 = d2_ref[...]
        d2b = d2.astype(BF16)
        dx1 = ALPHA * d2
        for k in range(N_CHIPS):
            r = r_ref[:, FF_SHARD * k:FF_SHARD * (k + 1)].astype(F32)
            dhid = lax.dot_general(d2b, wd_ref[k], NT_DIMS, preferred_element_type=F32)
            dupb = (dhid * (2.0 * r)).astype(BF16)
            dup_ref[:, FF_SHARD * k:FF_SHARD * (k + 1)] = dupb
            dx1 = dx1 + lax.dot_general(dupb, wu_ref[k], NT_DIMS, preferred_element_type=F32)
        xhat = xh_ref[...]
        dpre_ref[...] = _layer_norm_bwd(dx1, xhat, rs_ref[...], g_ref[...])

        @pl.when(i == 0)
        def _():
            sums_ref[...] = jnp.zeros_like(sums_ref)

        sums_ref[0:1, :] += jnp.sum(dx1 * xhat, axis=0, keepdims=True)
        sums_ref[1:2, :] += jnp.sum(dx1, axis=0, keepdims=True)

    row = lambda w: pl.BlockSpec((tm, w), lambda i: (i, 0))
    return pl.pallas_call(
        body, name="mlp_bwd_ln1", grid=(s // tm,),
        in_specs=[row(D_FF), row(D_MODEL), _resident_weight(), _resident_weight(), row(D_MODEL), row(1),
                  pl.BlockSpec((1, D_MODEL), lambda i: (0, 0))],
        out_specs=[row(D_FF), row(D_MODEL), pl.BlockSpec((SUBLANES, D_MODEL), lambda i: (0, 0))],
        out_shape=[SDS((s, D_FF), BF16), SDS((s, D_MODEL), F32), SDS((SUBLANES, D_MODEL), F32)],
        compiler_params=_params(("arbitrary",)))(relu_up, dpre2, w_up, w_down, xhat1, rstd1, g1)


def _grad_tn(a, b, name, out_cols, stacked):
    s, ka = a.shape
    n = b.shape[1]
    ts = _row_tile(s, 2048)
    n_steps = s // ts
    if stacked:
        tka, tn = ka, out_cols
        grid = (1, n // tn, n_steps)
        shape = (n // tn, ka, tn)
        out_spec = lambda: pl.BlockSpec((None, tka, tn), lambda r, c, t: (c, 0, 0))
    else:
        tka, tn = min(ka, 1024), n
        grid = (ka // tka, 1, n_steps)
        shape = (ka, n)
        out_spec = lambda: pl.BlockSpec((tka, tn), lambda r, c, t: (r, 0))

    def body(a_ref, b_ref, o_ref, ob_ref):
        t = pl.program_id(2)

        @pl.when(t == 0)
        def _():
            o_ref[...] = jnp.zeros_like(o_ref)

        o_ref[...] += lax.dot_general(a_ref[...].astype(BF16), b_ref[...].astype(BF16), TN_DIMS,
                                      preferred_element_type=F32)

        @pl.when(t == n_steps - 1)
        def _():
            ob_ref[...] = o_ref[...].astype(BF16)

    return pl.pallas_call(
        body, name=name, grid=grid,
        in_specs=[pl.BlockSpec((ts, tka), lambda r, c, t: (t, r)),
                  pl.BlockSpec((ts, tn), lambda r, c, t: (t, c))],
        out_specs=[out_spec(), out_spec()], out_shape=[SDS(shape, F32), SDS(shape, BF16)],
        compiler_params=_params(("parallel", "parallel", "arbitrary")))(a, b)


def _dmix(dpre1, w_out):
    s = dpre1.shape[0]
    tm = _row_tile(s, 512)

    def body(d_ref, w_ref, dc_ref, da_ref):
        db = d_ref[...].astype(BF16)
        dc_ref[...] = lax.dot_general(db, w_ref[0:CONV_WIDTH, :], NT_DIMS, preferred_element_type=F32)
        da_ref[...] = lax.dot_general(db, w_ref[CONV_WIDTH:, :], NT_DIMS, preferred_element_type=F32)

    return pl.pallas_call(
        body, name="dmix", grid=(s // tm,),
        in_specs=[pl.BlockSpec((tm, D_MODEL), lambda i: (i, 0)),
                  pl.BlockSpec((D_MODEL, D_MODEL), lambda i: (0, 0))],
        out_specs=[pl.BlockSpec((tm, CONV_WIDTH), lambda i: (i, 0)),
                   pl.BlockSpec((tm, ATTN_WIDTH), lambda i: (i, 0))],
        out_shape=[SDS((s, CONV_WIDTH), F32), SDS((s, ATTN_WIDTH), F32)],
        compiler_params=_params(("parallel",)))(dpre1, w_out)


def _sum_with_peers(own_ref, r_ref, o_ref):
    acc = own_ref[...]
    for f in range(r_ref.shape[0]):
        acc = acc + r_ref[f].astype(F32)
    o_ref[...] = acc


def _grad_x(kc_idx, dproj, w_in, dpre1, partial, earlier):
    s = dproj.shape[0]
    tm = _row_tile(s, 512)
    steps = s // tm
    n_peers = len(DEVICE_FLIPS)
    n_e = len(earlier)

    def body(kc_ref, dp_ref, w_ref, d1_ref, *rest):
        sum_ins, g_in = rest[:2 * n_e], rest[2 * n_e]
        o_ref, g_out = rest[2 * n_e + 1], rest[2 * n_e + 2]
        sum_outs, sems = rest[2 * n_e + 3:3 * n_e + 3], rest[3 * n_e + 3:]
        copies = functools.partial(_reduce_copies, [g_in], [g_out], sems)
        i = pl.program_id(0)
        pl.when(i == 0)(functools.partial(_start_copies, copies))
        acc = ALPHA * d1_ref[...]
        for k in range(N_CHIPS):
            acc = acc + lax.dot_general(dp_ref[:, IN_SHARD * k:IN_SHARD * (k + 1)], w_ref[k], NT_DIMS,
                                        preferred_element_type=F32)
        o_ref[...] = acc
        for a in range(n_e):
            _sum_with_peers(sum_ins[2 * a], sum_ins[2 * a + 1], sum_outs[a])
        pl.when(i == steps - 1)(functools.partial(_finish_copies, copies))

    in_specs = [pl.BlockSpec((tm, IN_COLS), lambda i, kc: (i, 0)),
                pl.BlockSpec((N_CHIPS, D_MODEL, IN_SHARD), lambda i, kc: (0, 0, 0)),
                pl.BlockSpec((tm, D_MODEL), lambda i, kc: (i, 0))]
    out_specs = [pl.BlockSpec((tm, D_MODEL), lambda i, kc: (i, 0)), ANY]
    out_shape = [SDS((s, D_MODEL), F32), SDS((n_peers, partial.shape[1] // 2, partial.shape[2]), partial.dtype)]
    operands = []
    for own, recv in earlier:
        _, _, h, cols = own.shape
        th = h // steps
        in_specs.append(pl.BlockSpec((None, None, th, cols), lambda i, kc: (kc[0], kc[1], i, 0)))
        in_specs.append(pl.BlockSpec((n_peers, th, cols), lambda i, kc: (0, i, 0)))
        out_specs.append(pl.BlockSpec((th, cols), lambda i, kc: (kc[1] * steps + i, 0)))
        out_shape.append(SDS((2 * h, cols), F32))
        operands += [own, recv]
    grid_spec = pltpu.PrefetchScalarGridSpec(
        num_scalar_prefetch=1, grid=(steps,), in_specs=in_specs + [ANY], out_specs=out_specs,
        scratch_shapes=[pltpu.SemaphoreType.DMA((n_peers,)), pltpu.SemaphoreType.DMA((n_peers,))])
    return pl.pallas_call(
        body, name="grad_x", grid_spec=grid_spec, out_shape=out_shape,
        compiler_params=_params(("arbitrary",)))(kc_idx, dproj, w_in, dpre1, *operands, partial)


def _adamw(w, g, m, v, name):
    r, c = w.shape
    tr = _row_tile(r, 256)

    def body(w_ref, g_ref, m_ref, v_ref, go_ref, d_ref, nm_ref, nv_ref):
        g_v = g_ref[...]
        go_ref[...] = g_v
        nm = ADAM_B1 * m_ref[...] + (1.0 - ADAM_B1) * g_v
        nv = ADAM_B2 * v_ref[...] + (1.0 - ADAM_B2) * (g_v * g_v)
        m_hat = nm / (1.0 - ADAM_B1 ** ADAM_STEP)
        v_hat = nv / (1.0 - ADAM_B2 ** ADAM_STEP)
        d_ref[...] = -ADAM_LR * (m_hat / (jnp.sqrt(v_hat) + ADAM_EPS) + ADAM_WD * w_ref[...])
        nm_ref[...] = nm
        nv_ref[...] = nv

    spec = lambda: pl.BlockSpec((tr, c), lambda i: (i, 0))
    return pl.pallas_call(
        body, name=name, grid=(r // tr,),
        in_specs=[spec(), spec(), spec(), spec()], out_specs=[spec(), spec(), spec(), spec()],
        out_shape=[SDS((r, c), F32)] * 4, compiler_params=_params(("parallel",)))(w, g, m, v)


def _sum_partials(kc_idx, grad, recv, name):
    _, _, h, cols = grad.shape
    th = _row_tile(h, 128)
    n_peers = recv.shape[0]

    def body(kc_ref, own_ref, r_ref, o_ref):
        _sum_with_peers(own_ref, r_ref, o_ref)

    grid_spec = pltpu.PrefetchScalarGridSpec(
        num_scalar_prefetch=1, grid=(h // th,),
        in_specs=[pl.BlockSpec((None, None, th, cols), lambda t, kc: (kc[0], kc[1], t, 0)),
                  pl.BlockSpec((n_peers, th, cols), lambda t, kc: (0, t, 0))],
        out_specs=pl.BlockSpec((th, cols), lambda t, kc: (kc[1] * (h // th) + t, 0)))
    return pl.pallas_call(
        body, name=name, grid_spec=grid_spec, out_shape=SDS((2 * h, cols), F32),
        compiler_params=_params(("parallel",)))(kc_idx, grad, recv)


def _gather_weights(kc_idx, w_in_slots, conv_slots, later):
    n_l = len(later)
    steps = SUBLANES

    def body(kc_ref, *refs):
        cast_ins, cast_outs = refs[:n_l], refs[n_l + 2:2 * n_l + 2]
        w_buf, conv_buf = refs[2 * n_l + 2], refs[2 * n_l + 3]
        sems = refs[2 * n_l + 4:]
        i = pl.program_id(0)

        def first_hop():
            x, y, c = _position()
            sends, arrivals = _gather_chip_hop([w_buf], sems)
            mine = conv_buf.at[2 * x + y]
            for j, (fx, fy) in enumerate(CHIP_FLIPS):
                tx, ty = _flip(x, fx), _flip(y, fy)
                there = conv_buf.at[2 * tx + ty]
                sends.append(_remote_copy(mine, mine, sems, 6 + j, (tx, ty, c)))
                arrivals.append(_remote_copy(there, there, sems, 6 + j, (tx, ty, c)))
            return sends, arrivals

        pl.when(i == 0)(functools.partial(_start_copies, first_hop))
        for src, dst in zip(cast_ins, cast_outs):
            dst[...] = src[...].astype(BF16)

        @pl.when(i == steps - 1)
        def _():
            sends, arrivals = first_hop()
            for cp in arrivals:
                cp.wait_recv()
            _start_copies(functools.partial(_gather_sibling_hop, [w_buf], sems))
            _finish_copies(functools.partial(_gather_sibling_hop, [w_buf], sems))
            for cp in sends:
                cp.wait_send()

    in_specs, out_specs, out_shape = [], [], []
    for w in later:
        r, c = w.shape
        in_specs.append(pl.BlockSpec((r // steps, c), lambda i, kc: (i, 0)))
        out_specs.append(pl.BlockSpec((None, r // steps, c), lambda i, kc: (kc[0], i, 0)))
        out_shape.append(SDS((N_CHIPS, r, c), BF16))
    grid_spec = pltpu.PrefetchScalarGridSpec(
        num_scalar_prefetch=1, grid=(steps,), in_specs=in_specs + [ANY, ANY], out_specs=out_specs + [ANY, ANY],
        scratch_shapes=[pltpu.SemaphoreType.DMA((9,)), pltpu.SemaphoreType.DMA((9,))])
    return pl.pallas_call(
        body, name="gather_weights", grid_spec=grid_spec,
        out_shape=out_shape + [SDS(w_in_slots.shape, w_in_slots.dtype), SDS(conv_slots.shape, conv_slots.dtype)],
        input_output_aliases={n_l + 1: n_l, n_l + 2: n_l + 1},
        compiler_params=_params(("arbitrary",)))(kc_idx, *later, w_in_slots, conv_slots)


def _finish_exchange(pieces, vec):
    n = len(pieces)
    n_dev = 2 * N_CHIPS

    def body(*refs):
        v_ref = refs[n]
        outs, o_ref = refs[n + 1:2 * n + 1], refs[2 * n + 1]
        buf, send_sems, recv_sems = refs[2 * n + 2:]
        x, y, c = _position()
        sibling = (x, y, 1 - c)
        me = 4 * x + 2 * y + c
        buf[me] = v_ref[...]
        started = []
        for f, (fx, fy, fc) in enumerate(DEVICE_FLIPS):
            cp = pltpu.make_async_remote_copy(
                src_ref=v_ref, dst_ref=buf.at[me], send_sem=send_sems.at[n + f], recv_sem=recv_sems.at[n + f],
                device_id=(_flip(x, fx), _flip(y, fy), _flip(c, fc)), device_id_type=MESH)
            cp.start()
            started.append(cp)
        for a in range(n):
            h = pieces[a].shape[0] // 2
            mine = outs[a].at[pl.ds(pl.multiple_of(c * h, h), h)]
            cp = pltpu.make_async_remote_copy(
                src_ref=mine, dst_ref=mine, send_sem=send_sems.at[a], recv_sem=recv_sems.at[a],
                device_id=sibling, device_id_type=MESH)
            cp.start()
            started.append(cp)
        for a in range(n):
            h = pieces[a].shape[0] // 2
            theirs = outs[a].at[pl.ds(pl.multiple_of((1 - c) * h, h), h)]
            pltpu.make_async_remote_copy(
                src_ref=theirs, dst_ref=theirs, send_sem=send_sems.at[a], recv_sem=recv_sems.at[a],
                device_id=sibling, device_id_type=MESH).wait_recv()
        for f, (fx, fy, fc) in enumerate(DEVICE_FLIPS):
            src = 4 * _flip(x, fx) + 2 * _flip(y, fy) + _flip(c, fc)
            pltpu.make_async_remote_copy(
                src_ref=v_ref, dst_ref=buf.at[src], send_sem=send_sems.at[n + f], recv_sem=recv_sems.at[n + f],
                device_id=(x, y, c), device_id_type=MESH).wait_recv()
        for cp in started:
            cp.wait_send()
        acc = buf[0]
        for d in range(1, n_dev):
            acc = acc + buf[d]
        o_ref[...] = acc

    vmem = pl.BlockSpec(memory_space=pltpu.VMEM)
    out_shape = [SDS(p.shape, p.dtype) for p in pieces] + [SDS(vec.shape, vec.dtype)]
    n_sems = n + n_dev - 1
    return pl.pallas_call(
        body, name="finish_exchange", in_specs=[ANY] * n + [vmem], out_specs=[ANY] * n + [vmem],
        out_shape=out_shape, input_output_aliases={a: a for a in range(n)},
        scratch_shapes=[pltpu.VMEM((n_dev,) + vec.shape, vec.dtype), pltpu.SemaphoreType.DMA((n_sems,)),
                        pltpu.SemaphoreType.DMA((n_sems,))])(*pieces, vec)


def _constants():
    r = jnp.arange(2 * KEY_BLOCK)[:, None] % KEY_BLOCK
    c = jnp.arange(2 * KEY_BLOCK)[None, :]
    later = jnp.where(c < KEY_BLOCK, r > c, True).astype(BF16)
    earlier = jnp.where(c < KEY_BLOCK, r < c, True).astype(BF16)
    upto = jnp.where(c < KEY_BLOCK, r <= c, True).astype(BF16)
    gr = (jnp.arange(2 * LANES)[:, None] % LANES) // GROUP
    gc = jnp.arange(LANES)[None, :] // GROUP
    gmat = (gr == gc).astype(BF16)
    return later, jnp.stack([earlier, upto]), gmat


def _rows(v):
    return v.reshape(-1, LANES)


def kernel(x, w_in, conv_w, g_conv, g_attn, w_out, ln1_g, ln1_b, w_up, w_down, ln2_g, ln2_b, loss_target, m_w_in, m_conv_w, m_g_conv, m_g_attn, m_w_out, m_ln1_g, m_ln1_b, m_w_up, m_w_down, m_ln2_g, m_ln2_b, v_w_in, v_conv_w, v_g_conv, v_g_attn, v_w_out, v_ln1_g, v_ln1_b, v_w_up, v_w_down, v_ln2_g, v_ln2_b):
    xs, target = x[0], loss_target[0]
    mesh_x, mesh_y, mesh_c = _position()
    k_idx = 2 * mesh_x + mesh_y
    kc_idx = jnp.stack([k_idx, mesh_c]).astype(jnp.int32)
    tri_later, tri_earlier, gmat = _constants()

    w_in_b = _cast_into_slot(kc_idx, w_in[0], "cast_w_in")
    conv_slot = jnp.pad(conv_w, ((0, 0), (0, SUBLANES - conv_w.shape[1]), (0, 0)))
    conv_b = lax.dynamic_update_slice(jnp.zeros((N_CHIPS, SUBLANES, LANES), F32), conv_slot, (k_idx, 0, 0))
    w_out_b, w_up_b, w_down_b, w_in_f, conv_f = _gather_weights(
        kc_idx, w_in_b, conv_b, [w_out[0], w_up[0], w_down[0]])
    taps = jnp.transpose(conv_f, (1, 0, 2)).reshape(SUBLANES, CONV_WIDTH)

    gates, qkv, xs_b = _proj(xs, w_in_f)
    ycn = _conv_fwd(gates, taps, g_conv, gmat)
    o, yan, tot, cut, w_out_f, w_up_f, w_down_f = _attn_fwd(
        qkv, g_attn, tri_later, gmat, [w_out_b, w_up_b, w_down_b])
    w_out_f = w_out_f.reshape(D_MODEL, D_MODEL)
    x1, xhat1, rstd1, x1_b = _mix_ln1(ycn, yan, w_out_f, xs, ln1_g, ln1_b)
    dpre2, ln2_sums, loss_sum, relu_up, hid, dpre2_b = _mlp_fwd_loss(x1, w_up_f, w_down_f, target, ln2_g, ln2_b)

    dup, dpre1, ln1_sums = _mlp_bwd_ln1(relu_up, dpre2, w_up_f, w_down_f, xhat1, rstd1, ln1_g)
    gw_up = _grad_tn(x1_b, dup, "grad_w_up", FF_SHARD, True)
    gw_down = [g.reshape(N_CHIPS, FF_SHARD, D_MODEL) for g in _grad_tn(hid, dpre2_b, "grad_w_down", D_MODEL, False)]
    gw_out_conv = _grad_tn(ycn, dpre1, "grad_w_out_conv", D_MODEL, False)
    gw_out_attn = _grad_tn(yan, dpre1, "grad_w_out_attn", D_MODEL, False)
    gw_out = [jnp.concatenate([gc_, ga_], axis=0).reshape(N_CHIPS, D_MODEL // N_CHIPS, D_MODEL)
              for gc_, ga_ in zip(gw_out_conv, gw_out_attn)]
    dycn, dyan = _dmix(dpre1, w_out_f)
    dq, dk, dv, gattn_sums, recv_out, recv_up, recv_down = _attn_bwd(
        qkv, o, tot, dyan, g_attn, tri_earlier, gmat, cut, [gw_out[1], gw_up[1], gw_down[1]])
    dbg, dy, conv_sums = _conv_bwd_gate(gates, dycn, taps, g_conv, gmat)
    dproj = _dproj_assemble(gates, dy, dbg, dq, dk, dv, taps)
    gw_in = _grad_tn(xs_b, dproj, "grad_w_in", IN_SHARD, True)
    halves = lambda g: g.reshape(N_CHIPS, 2, g.shape[1] // 2, g.shape[2])
    grad_x, recv_in, p_out, p_up, p_down = _grad_x(
        kc_idx, dproj, w_in_f, dpre1, gw_in[1],
        [(halves(gw_out[0]), recv_out), (halves(gw_up[0]), recv_up), (halves(gw_down[0]), recv_down)])
    pieces = [_sum_partials(kc_idx, halves(gw_in[0]), recv_in, "sum_partials_w_in"), p_out, p_up, p_down]
    conv_rows = jnp.transpose(conv_sums[0:3].reshape(3, N_CHIPS, LANES), (1, 0, 2)).reshape(3 * N_CHIPS, LANES)
    small = jnp.concatenate([
        loss_sum, _rows(conv_sums[3]), _rows(gattn_sums[0]), _rows(ln1_sums[0]), _rows(ln1_sums[1]),
        _rows(ln2_sums[0]), _rows(ln2_sums[1]), conv_rows,
        jnp.zeros((SMALL_ROWS - ROW_CONVW - 3 * N_CHIPS, LANES), F32)], axis=0)
    g_w_in, g_w_out, g_w_up, g_w_down, total = _finish_exchange(pieces, small)
    loss = total[ROW_LOSS, 0]
    g_conv_w = lax.dynamic_slice(total, (ROW_CONVW + 3 * k_idx, 0), (3, LANES))

    def pack(gc_, ga_, l1g, l1b, l2g, l2b, cw):
        return jnp.concatenate([_rows(gc_), _rows(ga_), _rows(l1g), _rows(l1b), _rows(l2g), _rows(l2b), cw[0],
                                jnp.zeros((PARAM_ROWS + SUBLANES - ROW_CONVW - 3, LANES), F32)], axis=0)

    small_w = pack(g_conv, g_attn, ln1_g, ln1_b, ln2_g, ln2_b, conv_w)
    small_m = pack(m_g_conv, m_g_attn, m_ln1_g, m_ln1_b, m_ln2_g, m_ln2_b, m_conv_w)
    small_v = pack(v_g_conv, v_g_attn, v_ln1_g, v_ln1_b, v_ln2_g, v_ln2_b, v_conv_w)
    small_g = jnp.concatenate([total[ROW_GCONV:ROW_CONVW], g_conv_w,
                               jnp.zeros((PARAM_ROWS + SUBLANES - ROW_CONVW - 3, LANES), F32)], axis=0)
    small_out = _adamw(small_w, small_g, small_m, small_v, "adamw_small")

    def unpack(p):
        off = ROW_GCONV
        vec = lambda a, b: p[a - off:b - off].reshape(1, -1)
        return {"g_conv": vec(ROW_GCONV, ROW_GATTN), "g_attn": vec(ROW_GATTN, ROW_LN1G),
                "ln1_g": vec(ROW_LN1G, ROW_LN1B), "ln1_b": vec(ROW_LN1B, ROW_LN2G),
                "ln2_g": vec(ROW_LN2G, ROW_LN2B), "ln2_b": vec(ROW_LN2B, ROW_CONVW),
                "conv_w": p[ROW_CONVW - off:ROW_CONVW - off + 3][None]}

    big_out = {
        "w_in": _adamw(w_in[0], g_w_in, m_w_in[0], v_w_in[0], "adamw_w_in"),
        "w_out": _adamw(w_out[0], g_w_out, m_w_out[0], v_w_out[0], "adamw_w_out"),
        "w_up": _adamw(w_up[0], g_w_up, m_w_up[0], v_w_up[0], "adamw_w_up"),
        "w_down": _adamw(w_down[0], g_w_down, m_w_down[0], v_w_down[0], "adamw_w_down"),
    }
    order = ["w_in", "conv_w", "g_conv", "g_attn", "w_out", "ln1_g", "ln1_b", "w_up", "w_down", "ln2_g", "ln2_b"]
    small_parts = [unpack(p) for p in small_out]

    def leaf(kind, name):
        if name in big_out:
            return big_out[name][kind][None]
        return small_parts[kind][name]

    outs = [loss, grad_x[None]]
    for kind in range(4):
        outs.extend(leaf(kind, name) for name in order)
    return tuple(outs)
```

```python
import functools

import jax
import jax.numpy as jnp
from jax import lax
from jax.experimental import pallas as pl
from jax.experimental.pallas import tpu as pltpu

F32 = jnp.float32
BF16 = jnp.bfloat16
SDS = jax.ShapeDtypeStruct

D_MODEL = 1024
CONV_WIDTH = 512
ATTN_WIDTH = 512
GROUP = 64
GATE_COLS = 3 * CONV_WIDTH
QKV_COLS = 3 * ATTN_WIDTH
IN_COLS = GATE_COLS + QKV_COLS
D_FF = 4 * D_MODEL
N_CHIPS = 4
IN_SHARD = IN_COLS // N_CHIPS
FF_SHARD = D_FF // N_CHIPS
ALPHA = float(2.0 ** 0.25)
LN_EPS = 1e-5
RMS_EPS = 1e-6
ATTN_SCALE = GROUP ** -0.5
LOG2_E = 1.4426950408889634
ADAM_LR = 0.001
ADAM_B1 = 0.9
ADAM_B2 = 0.999
ADAM_EPS = 1e-08
ADAM_WD = 0.01
ADAM_STEP = 10

LANES = 128
SUBLANES = 8
KEY_BLOCK = 128
ATTN_Q_TILE = 512
ATTN_KEY_BLOCKS = 2
ATTN_DIAG_GROUPS = 2
ATTN_DEAD_LOG2 = 200.0
VMEM_LIMIT = 56 * 1024 * 1024

MESH = pl.DeviceIdType.MESH
CHIP_FLIPS = ((1, 0), (0, 1), (1, 1))
DEVICE_FLIPS = tuple((fx, fy, fc) for fx in (0, 1) for fy in (0, 1) for fc in (0, 1))[1:]
NT_DIMS = (((1,), (1,)), ((), ()))
TN_DIMS = (((0,), (0,)), ((), ()))

ROW_LOSS = 0
ROW_GCONV = 8
ROW_GATTN = 12
ROW_LN1G = 16
ROW_LN1B = 24
ROW_LN2G = 32
ROW_LN2B = 40
ROW_CONVW = 48
SMALL_ROWS = 64
PARAM_ROWS = 48


def _params(sem=None):
    return pltpu.CompilerParams(dimension_semantics=sem, vmem_limit_bytes=VMEM_LIMIT)


def _flip(v, f):
    return 1 - v if f else v


def _position():
    return lax.axis_index("x"), lax.axis_index("y"), lax.axis_index("c")


def _hilo(v):
    hi = v.astype(BF16)
    lo = (v - hi.astype(F32)).astype(BF16)
    return jnp.concatenate([hi, lo], axis=1)


def _hilo_dot(v, mat):
    return jnp.dot(_hilo(v), mat, preferred_element_type=F32)


def _group_sum(v, gmat):
    parts = [_hilo_dot(v[:, LANES * j:LANES * (j + 1)], gmat) for j in range(v.shape[1] // LANES)]
    return parts[0] if len(parts) == 1 else jnp.concatenate(parts, axis=1)


def _softplus_terms(z):
    sp = jnp.log2(1.0 + jnp.exp2(-jnp.abs(z)))
    log_beta = jnp.minimum(z, 0.0) - sp
    return log_beta, log_beta - z


def _layer_norm_fwd(pre, g, b):
    mu = jnp.mean(pre, axis=-1, keepdims=True)
    d = pre - mu
    var = jnp.mean(d * d, axis=-1, keepdims=True)
    rstd = lax.rsqrt(var + LN_EPS)
    xhat = d * rstd
    return xhat * g + b, xhat, rstd


def _layer_norm_bwd(dy, xhat, rstd, g):
    dxh = dy * g
    m1 = jnp.mean(dxh, axis=-1, keepdims=True)
    m2 = jnp.mean(dxh * xhat, axis=-1, keepdims=True)
    return rstd * (dxh - m1 - xhat * m2)


def _row_tile(s, want):
    return min(s, want)


def _cast_into_slot(kc_idx, w, name):
    r, c = w.shape
    tr = _row_tile(r, 256)

    def body(kc_ref, w_ref, o_ref):
        o_ref[...] = w_ref[...].astype(BF16)

    grid_spec = pltpu.PrefetchScalarGridSpec(
        num_scalar_prefetch=1, grid=(r // tr,),
        in_specs=[pl.BlockSpec((tr, c), lambda i, kc: (i, 0))],
        out_specs=pl.BlockSpec((None, tr, c), lambda i, kc: (kc[0], i, 0)))
    return pl.pallas_call(
        body, name=name, grid_spec=grid_spec, out_shape=SDS((N_CHIPS, r, c), BF16),
        compiler_params=_params(("parallel",)))(kc_idx, w)


def _proj(x, w_in):
    s = x.shape[0]
    tm = _row_tile(s, 512)

    def body(x_ref, w_ref, gates_ref, qkv_ref, xb_ref):
        xb = x_ref[...].astype(BF16)
        xb_ref[...] = xb
        for k in range(N_CHIPS):
            acc = jnp.dot(xb, w_ref[k], preferred_element_type=F32)
            if k < 2:
                gates_ref[:, IN_SHARD * k:IN_SHARD * (k + 1)] = acc
            else:
                qkv_ref[:, IN_SHARD * (k - 2):IN_SHARD * (k - 1)] = acc.astype(BF16)

    return pl.pallas_call(
        body, name="proj", grid=(s // tm,),
        in_specs=[pl.BlockSpec((tm, D_MODEL), lambda i: (i, 0)),
                  pl.BlockSpec((N_CHIPS, D_MODEL, IN_SHARD), lambda i: (0, 0, 0))],
        out_specs=[pl.BlockSpec((tm, GATE_COLS), lambda i: (i, 0)),
                   pl.BlockSpec((tm, QKV_COLS), lambda i: (i, 0)),
                   pl.BlockSpec((tm, D_MODEL), lambda i: (i, 0))],
        out_shape=[SDS((s, GATE_COLS), F32), SDS((s, QKV_COLS), BF16), SDS((s, D_MODEL), BF16)],
        compiler_params=_params(("parallel",)))(x, w_in)


def _conv_forward_values(g_ref, halo_ref, taps_ref, first_block):
    gates = g_ref[...]
    tr = gates.shape[0]
    bg = gates[:, :CONV_WIDTH]
    cg = gates[:, CONV_WIDTH:2 * CONV_WIDTH]
    h = gates[:, 2 * CONV_WIDTH:]
    u = cg * h

    def prev(r):
        v = halo_ref[r:r + 1, CONV_WIDTH:2 * CONV_WIDTH] * halo_ref[r:r + 1, 2 * CONV_WIDTH:GATE_COLS]
        return jnp.where(first_block, 0.0, v)

    row = lax.broadcasted_iota(jnp.int32, (tr, CONV_WIDTH), 0)
    u1 = jnp.where(row == 0, prev(7), pltpu.roll(u, 1, 0))
    u2 = jnp.where(row == 0, prev(6), jnp.where(row == 1, prev(7), pltpu.roll(u, 2, 0)))
    y = taps_ref[0:1, :] * u2 + taps_ref[1:2, :] * u1 + taps_ref[2:3, :] * u
    return bg, cg, h, u, u1, u2, y


def _conv_fwd(gates, taps, g_conv, gmat):
    s = gates.shape[0]
    tr = _row_tile(s, 512)
    hb = tr // SUBLANES

    def body(g_ref, halo_ref, taps_ref, gain_ref, gmat_ref, out_ref):
        i = pl.program_id(0)
        bg, _, _, _, _, _, y = _conv_forward_values(g_ref, halo_ref, taps_ref, i == 0)
        yc = bg * y
        ms = _group_sum(yc * yc, gmat_ref[...]) * (1.0 / GROUP)
        out_ref[...] = (yc * lax.rsqrt(ms + RMS_EPS) * gain_ref[...]).astype(BF16)

    return pl.pallas_call(
        body, name="conv_fwd", grid=(s // tr,),
        in_specs=[pl.BlockSpec((tr, GATE_COLS), lambda i: (i, 0)),
                  pl.BlockSpec((SUBLANES, GATE_COLS), lambda i: (jnp.maximum(i * hb - 1, 0), 0)),
                  pl.BlockSpec((SUBLANES, CONV_WIDTH), lambda i: (0, 0)),
                  pl.BlockSpec((1, CONV_WIDTH), lambda i: (0, 0)),
                  pl.BlockSpec((2 * LANES, LANES), lambda i: (0, 0))],
        out_specs=pl.BlockSpec((tr, CONV_WIDTH), lambda i: (i, 0)),
        out_shape=SDS((s, CONV_WIDTH), BF16),
        compiler_params=_params(("parallel",)))(gates, gates, taps, g_conv, gmat)


def _conv_bwd_gate(gates, dycn, taps, g_conv, gmat):
    s = gates.shape[0]
    tr = _row_tile(s, 512)
    hb = tr // SUBLANES

    def body(g_ref, halo_ref, dn_ref, taps_ref, gain_ref, gmat_ref, dbg_ref, dy_ref, sums_ref):
        i = pl.program_id(0)
        bg, _, _, u, u1, u2, y = _conv_forward_values(g_ref, halo_ref, taps_ref, i == 0)
        gmat_v = gmat_ref[...]
        yc = bg * y
        rstd = lax.rsqrt(_group_sum(yc * yc, gmat_v) * (1.0 / GROUP) + RMS_EPS)
        n = yc * rstd
        dout = dn_ref[...]
        dn = dout * gain_ref[...]
        dyc = rstd * (dn - n * (_group_sum(dn * n, gmat_v) * (1.0 / GROUP)))
        dbg_ref[...] = (dyc * y).astype(BF16)
        dy = dyc * bg
        dy_ref[...] = dy

        @pl.when(i == 0)
        def _():
            sums_ref[...] = jnp.zeros_like(sums_ref)

        sums_ref[0:1, :] += jnp.sum(dy * u2, axis=0, keepdims=True)
        sums_ref[1:2, :] += jnp.sum(dy * u1, axis=0, keepdims=True)
        sums_ref[2:3, :] += jnp.sum(dy * u, axis=0, keepdims=True)
        sums_ref[3:4, :] += jnp.sum(dout * n, axis=0, keepdims=True)

    return pl.pallas_call(
        body, name="conv_bwd_gate", grid=(s // tr,),
        in_specs=[pl.BlockSpec((tr, GATE_COLS), lambda i: (i, 0)),
                  pl.BlockSpec((SUBLANES, GATE_COLS), lambda i: (jnp.maximum(i * hb - 1, 0), 0)),
                  pl.BlockSpec((tr, CONV_WIDTH), lambda i: (i, 0)),
                  pl.BlockSpec((SUBLANES, CONV_WIDTH), lambda i: (0, 0)),
                  pl.BlockSpec((1, CONV_WIDTH), lambda i: (0, 0)),
                  pl.BlockSpec((2 * LANES, LANES), lambda i: (0, 0))],
        out_specs=[pl.BlockSpec((tr, CONV_WIDTH), lambda i: (i, 0)),
                   pl.BlockSpec((tr, CONV_WIDTH), lambda i: (i, 0)),
                   pl.BlockSpec((SUBLANES, CONV_WIDTH), lambda i: (0, 0))],
        out_shape=[SDS((s, CONV_WIDTH), BF16), SDS((s, CONV_WIDTH), F32), SDS((SUBLANES, CONV_WIDTH), F32)],
        compiler_params=_params(("arbitrary",)))(gates, gates, dycn, taps, g_conv, gmat)


def _dproj_assemble(gates, dy, dbg, dq, dk, dv, taps):
    s = gates.shape[0]
    tr = _row_tile(s, 512)
    hb = tr // SUBLANES
    last = s // SUBLANES - 1
    n_blocks = s // tr

    def body(g_ref, dy_ref, halo_ref, dbg_ref, dq_ref, dk_ref, dv_ref, taps_ref, out_ref):
        i = pl.program_id(0)
        gates_v = g_ref[...]
        cg = gates_v[:, CONV_WIDTH:2 * CONV_WIDTH]
        h = gates_v[:, 2 * CONV_WIDTH:]
        dy_v = dy_ref[...]
        last_block = i == n_blocks - 1
        nxt = lambda r: jnp.where(last_block, 0.0, halo_ref[r:r + 1, :])
        row = lax.broadcasted_iota(jnp.int32, (tr, CONV_WIDTH), 0)
        d1 = jnp.where(row == tr - 1, nxt(0), pltpu.roll(dy_v, tr - 1, 0))
        d2 = jnp.where(row == tr - 1, nxt(1), jnp.where(row == tr - 2, nxt(0), pltpu.roll(dy_v, tr - 2, 0)))
        du = taps_ref[2:3, :] * dy_v + taps_ref[1:2, :] * d1 + taps_ref[0:1, :] * d2
        out_ref[:, 0:CONV_WIDTH] = dbg_ref[...]
        out_ref[:, CONV_WIDTH:2 * CONV_WIDTH] = (du * h).astype(BF16)
        out_ref[:, 2 * CONV_WIDTH:GATE_COLS] = (du * cg).astype(BF16)
        out_ref[:, GATE_COLS:GATE_COLS + ATTN_WIDTH] = dq_ref[...]
        out_ref[:, GATE_COLS + ATTN_WIDTH:GATE_COLS + 2 * ATTN_WIDTH] = dk_ref[...].astype(BF16)
        out_ref[:, GATE_COLS + 2 * ATTN_WIDTH:] = dv_ref[...].astype(BF16)

    row_spec = lambda w: pl.BlockSpec((tr, w), lambda i: (i, 0))
    return pl.pallas_call(
        body, name="dproj_assemble", grid=(s // tr,),
        in_specs=[row_spec(GATE_COLS), row_spec(CONV_WIDTH),
                  pl.BlockSpec((SUBLANES, CONV_WIDTH), lambda i: (jnp.minimum((i + 1) * hb, last), 0)),
                  row_spec(CONV_WIDTH), row_spec(ATTN_WIDTH), row_spec(ATTN_WIDTH), row_spec(ATTN_WIDTH),
                  pl.BlockSpec((SUBLANES, CONV_WIDTH), lambda i: (0, 0))],
        out_specs=row_spec(IN_COLS),
        out_shape=SDS((s, IN_COLS), BF16),
        compiler_params=_params(("parallel",)))(gates, dy, dy, dbg, dq, dk, dv, taps)


def _stack_heads(rows, nb):
    lane = lax.broadcasted_iota(jnp.int32, (1, LANES), 1)
    zero = jnp.zeros((KEY_BLOCK, LANES), rows.dtype)
    parts = []
    for blk in range(nb):
        r = rows[blk * KEY_BLOCK:(blk + 1) * KEY_BLOCK]
        parts.append(jnp.where(lane < GROUP, r, zero))
        parts.append(jnp.where(lane < GROUP, zero, r))
    return jnp.concatenate(parts, axis=0)


def _stack_hilo(v, n_cols):
    return jnp.concatenate([_hilo(v[:, c * KEY_BLOCK:(c + 1) * KEY_BLOCK]) for c in range(n_cols)], axis=0)


def _causal_mask(tq, nb, diag_base):
    shape = (tq, 2 * nb * KEY_BLOCK)
    row = lax.broadcasted_iota(jnp.int32, shape, 0)
    col = lax.broadcasted_iota(jnp.int32, shape, 1)
    key = diag_base + (col // (2 * KEY_BLOCK)) * KEY_BLOCK + col % KEY_BLOCK
    return key < row


ANY = pl.BlockSpec(memory_space=pl.ANY)


def _remote_copy(src, dst, sems, idx, target):
    return pltpu.make_async_remote_copy(src_ref=src, dst_ref=dst, send_sem=sems[0].at[idx], recv_sem=sems[1].at[idx],
                                        device_id=target, device_id_type=MESH)


def _gather_chip_hop(bufs, sems):
    x, y, c = _position()
    sends, arrivals = [], []
    for a, buf in enumerate(bufs):
        h = buf.shape[1] // 2
        rows = pl.ds(pl.multiple_of(c * h, h), h)
        mine = buf.at[2 * x + y, rows]
        for j, (fx, fy) in enumerate(CHIP_FLIPS):
            tx, ty = _flip(x, fx), _flip(y, fy)
            there = buf.at[2 * tx + ty, rows]
            sends.append(_remote_copy(mine, mine, sems, 6 * a + j, (tx, ty, c)))
            arrivals.append(_remote_copy(there, there, sems, 6 * a + j, (tx, ty, c)))
    return sends, arrivals


def _gather_sibling_hop(bufs, sems):
    x, y, c = _position()
    sends, arrivals = [], []
    for a, buf in enumerate(bufs):
        h = buf.shape[1] // 2
        mine, theirs = pl.ds(pl.multiple_of(c * h, h), h), pl.ds(pl.multiple_of((1 - c) * h, h), h)
        for j, (fx, fy) in enumerate(CHIP_FLIPS):
            kj = 2 * _flip(x, fx) + _flip(y, fy)
            landed, other = buf.at[kj, mine], buf.at[kj, theirs]
            sends.append(_remote_copy(landed, landed, sems, 6 * a + 3 + j, (x, y, 1 - c)))
            arrivals.append(_remote_copy(other, other, sems, 6 * a + 3 + j, (x, y, 1 - c)))
    return sends, arrivals


def _reduce_copies(ins, outs, sems):
    x, y, c = _position()
    sends, arrivals = [], []
    for a in range(len(ins)):
        h = ins[a].shape[1] // 2
        for f, (fx, fy, fc) in enumerate(DEVICE_FLIPS):
            tx, ty, tc = _flip(x, fx), _flip(y, fy), _flip(c, fc)
            src = ins[a].at[2 * tx + ty, pl.ds(pl.multiple_of(tc * h, h), h)]
            sends.append(_remote_copy(src, outs[a].at[f], sems, 7 * a + f, (tx, ty, tc)))
            arrivals.append(_remote_copy(outs[a].at[f], outs[a].at[f], sems, 7 * a + f, (tx, ty, tc)))
    return sends, arrivals


def _chip_reduce_copies(src, dst, sems):
    x, y, c = _position()
    sends, arrivals = [], []
    for j, (fx, fy) in enumerate(CHIP_FLIPS):
        tx, ty = _flip(x, fx), _flip(y, fy)
        sends.append(_remote_copy(src.at[2 * tx + ty], dst.at[j], sems, j, (tx, ty, c)))
        arrivals.append(_remote_copy(dst.at[j], dst.at[j], sems, j, (tx, ty, c)))
    return sends, arrivals


def _start_copies(make):
    sends, _ = make()
    for cp in sends:
        cp.start()


def _finish_copies(make):
    sends, arrivals = make()
    for cp in arrivals:
        cp.wait_recv()
    for cp in sends:
        cp.wait_send()


def _attn_fwd(qkv, g_attn, tri, gmat, shards):
    n_w = len(shards)
    s = qkv.shape[0]
    tq = _row_tile(s, ATTN_Q_TILE)
    tk = KEY_BLOCK
    nb = ATTN_KEY_BLOCKS
    width = nb * tk
    n_groups = ATTN_DIAG_GROUPS
    group = tq // n_groups
    pairs = ATTN_WIDTH // LANES

    def body(q_ref, k_ref, v_ref, gain_ref, tri_ref, gmat_ref, *rest):
        o_ref, yn_ref, tot_ref, cut_ref = rest[n_w:n_w + 4]
        w_bufs, sems = rest[n_w + 4:2 * n_w + 4], rest[2 * n_w + 4:]
        chip_hop = functools.partial(_gather_chip_hop, w_bufs, sems)
        sibling_hop = functools.partial(_gather_sibling_hop, w_bufs, sems)
        p, i = pl.program_id(0), pl.program_id(1)
        pl.when((p == 0) & (i == 0))(functools.partial(_start_copies, chip_hop))

        @pl.when((p == pairs - 1) & (i == 0))
        def _():
            for cp in chip_hop()[1]:
                cp.wait_recv()
            _start_copies(sibling_hop)

        q2 = q_ref[...]
        tri_v = tri_ref[...]

        def trip(s0, n_blk, rows, carry, diag_base):
            r0, nr = rows
            run = [carry[0], carry[1]]
            oacc = carry[2]
            ksel = _stack_heads(k_ref[pl.ds(s0, n_blk * tk), :], n_blk)
            vsel = _stack_heads(v_ref[pl.ds(s0, n_blk * tk), :], n_blk)
            z = lax.dot_general(q2[r0:r0 + nr], ksel, NT_DIMS, preferred_element_type=F32) * (ATTN_SCALE * LOG2_E)
            log_beta, log_keep = _softplus_terms(z)
            if diag_base is not None:
                valid = _causal_mask(nr, n_blk, diag_base)
                log_keep = jnp.where(valid, log_keep, 0.0)
            ct = jnp.dot(_stack_hilo(log_keep, 2 * n_blk), tri_v, preferred_element_type=F32)
            a_parts = [None] * (2 * n_blk)
            for c in reversed(range(2 * n_blk)):
                h = c % 2
                ct_c = ct[c * nr:(c + 1) * nr]
                a_parts[c] = jnp.exp2(log_beta[:, c * tk:(c + 1) * tk] + ct_c[:, :tk] + run[h])
                run[h] = run[h] + ct_c[:, tk:]
            a = jnp.concatenate(a_parts, axis=1)
            if diag_base is not None:
                a = jnp.where(valid, a, 0.0)
            oacc = oacc + jnp.dot(a.astype(BF16), vsel, preferred_element_type=F32)
            return run[0], run[1], oacc

        groups = []
        for g in range(n_groups):
            zeros = (jnp.zeros((group, tk), F32), jnp.zeros((group, tk), F32), jnp.zeros((group, LANES), F32))
            groups.append(trip(pl.multiple_of(i * tq, tq), (g + 1) * group // tk, (g * group, group), zeros,
                               -g * group))
        carry = tuple(jnp.concatenate([grp[j] for grp in groups], axis=0) for j in range(3))
        n_full = i * (tq // width)

        def alive(run_a, run_b):
            return jnp.max(jnp.maximum(run_a, run_b)) > -ATTN_DEAD_LOG2

        def earlier_trip(c):
            done, _, run_a, run_b, oacc = c
            s0 = pl.multiple_of((n_full - 1 - done) * width, width)
            run_a, run_b, oacc = trip(s0, nb, (0, tq), (run_a, run_b, oacc), None)
            return done + 1, alive(run_a, run_b), run_a, run_b, oacc

        done, _, run_a, run_b, oacc = lax.while_loop(
            lambda c: (c[0] < n_full) & c[1], earlier_trip, (jnp.int32(0), alive(carry[0], carry[1])) + carry)
        cut_ref[p * pl.num_programs(1) + i] = (n_full - done).astype(F32)
        lane = lax.broadcasted_iota(jnp.int32, (1, LANES), 1)
        o_ref[...] = oacc
        tot_ref[...] = jnp.where(lane < GROUP, run_a, run_b)
        ms = _group_sum(oacc * oacc, gmat_ref[...]) * (1.0 / GROUP)
        yn_ref[...] = (oacc * lax.rsqrt(ms + RMS_EPS) * gain_ref[...]).astype(BF16)

        @pl.when((p == pairs - 1) & (i == pl.num_programs(1) - 1))
        def _():
            for cp in chip_hop()[0]:
                cp.wait_send()
            _finish_copies(sibling_hop)

    blk = lambda: pl.BlockSpec((tq, LANES), lambda p, i: (i, p))
    return pl.pallas_call(
        body, name="attn_fwd", grid=(pairs, s // tq),
        in_specs=[pl.BlockSpec((tq, LANES), lambda p, i: (i, p)),
                  pl.BlockSpec((s, LANES), lambda p, i: (0, pairs + p)),
                  pl.BlockSpec((s, LANES), lambda p, i: (0, 2 * pairs + p)),
                  pl.BlockSpec((1, LANES), lambda p, i: (0, p)),
                  pl.BlockSpec((2 * tk, 2 * tk), lambda p, i: (0, 0)),
                  pl.BlockSpec((2 * LANES, LANES), lambda p, i: (0, 0))] + [ANY] * n_w,
        out_specs=[blk(), blk(), blk(), pl.BlockSpec(memory_space=pltpu.SMEM)] + [ANY] * n_w,
        out_shape=[SDS((s, ATTN_WIDTH), F32), SDS((s, ATTN_WIDTH), BF16), SDS((s, ATTN_WIDTH), F32),
                   SDS((pairs * (s // tq),), F32)]
        + [SDS(w.shape, w.dtype) for w in shards],
        input_output_aliases={6 + a: 4 + a for a in range(n_w)},
        scratch_shapes=[pltpu.SemaphoreType.DMA((6 * n_w,)), pltpu.SemaphoreType.DMA((6 * n_w,))],
        compiler_params=_params(("arbitrary", "arbitrary")))(qkv, qkv, qkv, g_attn, tri, gmat, *shards)


def _attn_bwd(qkv, o, tot, dyn, g_attn, tri, gmat, cut, partials):
    n_g = len(partials)
    s = qkv.shape[0]
    tq = _row_tile(s, ATTN_Q_TILE)
    tk = KEY_BLOCK
    nb = ATTN_KEY_BLOCKS
    width = nb * tk
    n_groups = ATTN_DIAG_GROUPS
    group = tq // n_groups
    pairs = ATTN_WIDTH // LANES

    def body(q_ref, k_ref, v_ref, o_ref, tot_ref, dyn_ref, gain_ref, tri_ref, gmat_ref, cut_ref, *rest):
        g_ins, (dq_ref, dk_ref, dv_ref, dg_ref) = rest[:n_g], rest[n_g:n_g + 4]
        g_outs, sems = rest[n_g + 4:2 * n_g + 4], rest[2 * n_g + 4:]
        copies = functools.partial(_reduce_copies, g_ins, g_outs, sems)
        p, i = pl.program_id(0), pl.program_id(1)
        pl.when((p == 0) & (i == 0))(functools.partial(_start_copies, copies))

        @pl.when(i == 0)
        def _():
            dk_ref[...] = jnp.zeros_like(dk_ref)
            dv_ref[...] = jnp.zeros_like(dv_ref)
            dg_ref[...] = jnp.zeros_like(dg_ref)

        gmat_v = gmat_ref[...]
        o_v = o_ref[...]
        rstd = lax.rsqrt(_group_sum(o_v * o_v, gmat_v) * (1.0 / GROUP) + RMS_EPS)
        n = o_v * rstd
        dout = dyn_ref[...]
        dg_ref[0:1, :] += jnp.sum(dout * n, axis=0, keepdims=True)
        dn = dout * gain_ref[...]
        do2 = (rstd * (dn - n * (_group_sum(dn * n, gmat_v) * (1.0 / GROUP)))).astype(BF16)
        q2 = q_ref[...]
        tot_v = tot_ref[...]
        tots = (jnp.broadcast_to(tot_v[:, 0:1], (tq, tk)), jnp.broadcast_to(tot_v[:, GROUP:GROUP + 1], (tq, tk)))
        tri_v, tri_incl_v = tri_ref[0], tri_ref[1]
        lane = lax.broadcasted_iota(jnp.int32, (1, LANES), 1)

        def trip(s0, n_blk, rows, carry, diag_base):
            r0, nr = rows
            rest_l = [carry[0], carry[1]]
            pref_g = [carry[2], carry[3]]
            dq = carry[4]
            q_rows, do_rows = q2[r0:r0 + nr], do2[r0:r0 + nr]
            ksel = _stack_heads(k_ref[pl.ds(s0, n_blk * tk), :], n_blk)
            vsel = _stack_heads(v_ref[pl.ds(s0, n_blk * tk), :], n_blk)
            z = lax.dot_general(q_rows, ksel, NT_DIMS, preferred_element_type=F32) * (ATTN_SCALE * LOG2_E)
            log_beta, log_keep = _softplus_terms(z)
            if diag_base is not None:
                valid = _causal_mask(nr, n_blk, diag_base)
                log_keep = jnp.where(valid, log_keep, 0.0)
            ctl = jnp.dot(_stack_hilo(log_keep, 2 * n_blk), tri_incl_v, preferred_element_type=F32)
            da = lax.dot_general(do_rows, vsel, NT_DIMS, preferred_element_type=F32)
            a_parts = []
            for c in range(2 * n_blk):
                h = c % 2
                ct_c = ctl[c * nr:(c + 1) * nr]
                cols = slice(c * tk, (c + 1) * tk)
                a_parts.append(jnp.exp2(log_beta[:, cols] + (rest_l[h] - ct_c[:, :tk])))
                rest_l[h] = rest_l[h] - ct_c[:, tk:]
            a = jnp.concatenate(a_parts, axis=1)
            if diag_base is not None:
                a = jnp.where(valid, a, 0.0)
            g = a * da
            ctg = jnp.dot(_stack_hilo(g, 2 * n_blk), tri_v, preferred_element_type=F32)
            dz_parts = []
            for c in range(2 * n_blk):
                h = c % 2
                ct_c = ctg[c * nr:(c + 1) * nr]
                cols = slice(c * tk, (c + 1) * tk)
                prefix = pref_g[h] + ct_c[:, :tk]
                pref_g[h] = pref_g[h] + ct_c[:, tk:]
                g_c = g[:, cols]
                dz_parts.append(g_c - jnp.exp2(log_beta[:, cols]) * (g_c + prefix))
            dz = jnp.concatenate(dz_parts, axis=1) * ATTN_SCALE
            if diag_base is not None:
                dz = jnp.where(valid, dz, 0.0)
            dzb = dz.astype(BF16)
            dq = dq + jnp.dot(dzb, ksel, preferred_element_type=F32)
            dkt = lax.dot_general(dzb, q_rows, TN_DIMS, preferred_element_type=F32)
            dvt = lax.dot_general(a.astype(BF16), do_rows, TN_DIMS, preferred_element_type=F32)
            for blk in range(n_blk):
                ra, rb = slice(2 * blk * tk, (2 * blk + 1) * tk), slice((2 * blk + 1) * tk, (2 * blk + 2) * tk)
                keys = pl.ds(pl.multiple_of(s0 + blk * tk, tk), tk)
                dk_ref[keys, :] += jnp.where(lane < GROUP, dkt[ra], dkt[rb])
                dv_ref[keys, :] += jnp.where(lane < GROUP, dvt[ra], dvt[rb])
            return rest_l[0], rest_l[1], pref_g[0], pref_g[1], dq

        zeros_qk = jnp.zeros((tq, tk), F32)
        carry = (tots[0], tots[1], zeros_qk, zeros_qk, jnp.zeros((tq, LANES), F32))
        n_full = i * (tq // width)
        first = jnp.clip(cut_ref[p * pl.num_programs(1) + i].astype(jnp.int32), 0, n_full)
        carry = lax.fori_loop(
            first, n_full, lambda t, c: trip(pl.multiple_of(t * width, width), nb, (0, tq), c, None), carry)
        dq_groups = []
        for g in range(n_groups):
            sub = tuple(x[g * group:(g + 1) * group] for x in carry)
            dq_groups.append(trip(pl.multiple_of(i * tq, tq), (g + 1) * group // tk, (g * group, group), sub,
                                  -g * group)[4])
        dq_ref[...] = jnp.concatenate(dq_groups, axis=0).astype(BF16)
        pl.when((p == pairs - 1) & (i == pl.num_programs(1) - 1))(functools.partial(_finish_copies, copies))

    blk = lambda: pl.BlockSpec((tq, LANES), lambda p, i: (i, p))
    col = lambda: pl.BlockSpec((s, LANES), lambda p, i: (0, p))
    n_peers = len(DEVICE_FLIPS)
    return pl.pallas_call(
        body, name="attn_bwd", grid=(pairs, s // tq),
        in_specs=[pl.BlockSpec((tq, LANES), lambda p, i: (i, p)),
                  pl.BlockSpec((s, LANES), lambda p, i: (0, pairs + p)),
                  pl.BlockSpec((s, LANES), lambda p, i: (0, 2 * pairs + p)),
                  blk(), blk(), blk(),
                  pl.BlockSpec((1, LANES), lambda p, i: (0, p)),
                  pl.BlockSpec((2, 2 * tk, 2 * tk), lambda p, i: (0, 0, 0)),
                  pl.BlockSpec((2 * LANES, LANES), lambda p, i: (0, 0)),
                  pl.BlockSpec(memory_space=pltpu.SMEM)] + [ANY] * n_g,
        out_specs=[blk(), col(), col(), pl.BlockSpec((SUBLANES, LANES), lambda p, i: (0, p))] + [ANY] * n_g,
        out_shape=[SDS((s, ATTN_WIDTH), BF16), SDS((s, ATTN_WIDTH), F32), SDS((s, ATTN_WIDTH), F32),
                   SDS((SUBLANES, ATTN_WIDTH), F32)]
        + [SDS((n_peers, g.shape[1] // 2, g.shape[2]), g.dtype) for g in partials],
        scratch_shapes=[pltpu.SemaphoreType.DMA((n_peers * n_g,)), pltpu.SemaphoreType.DMA((n_peers * n_g,))],
        compiler_params=_params(("arbitrary", "arbitrary")))(
            qkv, qkv, qkv, o, tot, dyn, g_attn, tri, gmat, cut, *partials)


def _mix_ln1(ycn, yan, w_out, x, g, b):
    s = x.shape[0]
    tm = _row_tile(s, 512)

    def body(yc_ref, ya_ref, w_ref, x_ref, g_ref, b_ref, x1_ref, xhat_ref, rstd_ref, x1b_ref):
        mix = jnp.dot(yc_ref[...], w_ref[0:CONV_WIDTH, :], preferred_element_type=F32)
        mix = mix + jnp.dot(ya_ref[...], w_ref[CONV_WIDTH:, :], preferred_element_type=F32)
        x1, xhat, rstd = _layer_norm_fwd(ALPHA * x_ref[...] + mix, g_ref[...], b_ref[...])
        x1_ref[...] = x1
        xhat_ref[...] = xhat
        rstd_ref[...] = rstd
        x1b_ref[...] = x1.astype(BF16)

    row = lambda w: pl.BlockSpec((tm, w), lambda i: (i, 0))
    vec = lambda: pl.BlockSpec((1, D_MODEL), lambda i: (0, 0))
    return pl.pallas_call(
        body, name="mix_ln1", grid=(s // tm,),
        in_specs=[row(CONV_WIDTH), row(ATTN_WIDTH), pl.BlockSpec((D_MODEL, D_MODEL), lambda i: (0, 0)),
                  row(D_MODEL), vec(), vec()],
        out_specs=[row(D_MODEL), row(D_MODEL), row(1), row(D_MODEL)],
        out_shape=[SDS((s, D_MODEL), F32), SDS((s, D_MODEL), F32), SDS((s, 1), F32), SDS((s, D_MODEL), BF16)],
        compiler_params=_params(("parallel",)))(ycn, yan, w_out, x, g, b)


def _mlp_fwd_loss(x1, w_up, w_down, target, g, b):
    s = x1.shape[0]
    tm = _row_tile(s, 256)

    def body(x1_ref, wu_ref, wd_ref, t_ref, g_ref, b_ref, dpre_ref, sums_ref, loss_ref, r_ref, hid_ref, dpreb_ref):
        i = pl.program_id(0)
        x1_v = x1_ref[...]
        xb = x1_v.astype(BF16)
        ffn = jnp.zeros((tm, D_MODEL), F32)
        for k in range(N_CHIPS):
            r = jnp.maximum(jnp.dot(xb, wu_ref[k], preferred_element_type=F32), 0.0)
            hid = (r * r).astype(BF16)
            r_ref[:, FF_SHARD * k:FF_SHARD * (k + 1)] = r.astype(BF16)
            hid_ref[:, FF_SHARD * k:FF_SHARD * (k + 1)] = hid
            ffn = ffn + jnp.dot(hid, wd_ref[k], preferred_element_type=F32)
        g_v = g_ref[...]
        x2, xhat, rstd = _layer_norm_fwd(ALPHA * x1_v + ffn, g_v, b_ref[...])
        err = x2 - t_ref[...]
        dx2 = err * (1.0 / D_MODEL)
        dpre = _layer_norm_bwd(dx2, xhat, rstd, g_v)
        dpre_ref[...] = dpre
        dpreb_ref[...] = dpre.astype(BF16)

        @pl.when(i == 0)
        def _():
            sums_ref[...] = jnp.zeros_like(sums_ref)
            loss_ref[...] = jnp.zeros_like(loss_ref)

        sums_ref[0:1, :] += jnp.sum(dx2 * xhat, axis=0, keepdims=True)
        sums_ref[1:2, :] += jnp.sum(dx2, axis=0, keepdims=True)
        loss_ref[...] += jnp.sum(jnp.sum(err * err, axis=1, keepdims=True), axis=0, keepdims=True) * (0.5 / D_MODEL)

    row = lambda: pl.BlockSpec((tm, D_MODEL), lambda i: (i, 0))
    wide = lambda: pl.BlockSpec((tm, D_FF), lambda i: (i, 0))
    vec = lambda: pl.BlockSpec((1, D_MODEL), lambda i: (0, 0))
    return pl.pallas_call(
        body, name="mlp_fwd_loss", grid=(s // tm,),
        in_specs=[row(), _resident_weight(), _resident_weight(), row(), vec(), vec()],
        out_specs=[row(), pl.BlockSpec((SUBLANES, D_MODEL), lambda i: (0, 0)),
                   pl.BlockSpec((SUBLANES, LANES), lambda i: (0, 0)), wide(), wide(), row()],
        out_shape=[SDS((s, D_MODEL), F32), SDS((SUBLANES, D_MODEL), F32), SDS((SUBLANES, LANES), F32),
                   SDS((s, D_FF), BF16), SDS((s, D_FF), BF16), SDS((s, D_MODEL), BF16)],
        compiler_params=_params(("arbitrary",)))(x1, w_up, w_down, target, g, b)


def _resident_weight():
    return pl.BlockSpec((N_CHIPS, D_MODEL, FF_SHARD), lambda i: (0, 0, 0), pipeline_mode=pl.Buffered(1))


def _mlp_bwd_ln1(relu_up, dpre2, w_up, w_down, xhat1, rstd1, g1):
    s = dpre2.shape[0]
    tm = _row_tile(s, 256)

    def body(r_ref, d2_ref, wu_ref, wd_ref, xh_ref, rs_ref, g_ref, dup_ref, dpre_ref, sums_ref):
        i = pl.program_id(0)
        d2 = d2_ref[...]
        d2b = d2.astype(BF16)
        dx1 = ALPHA * d2
        for k in range(N_CHIPS):
            r = r_ref[:, FF_SHARD * k:FF_SHARD * (k + 1)].astype(F32)
            dhid = lax.dot_general(d2b, wd_ref[k], NT_DIMS, preferred_element_type=F32)
            dupb = (dhid * (2.0 * r)).astype(BF16)
            dup_ref[:, FF_SHARD * k:FF_SHARD * (k + 1)] = dupb
            dx1 = dx1 + lax.dot_general(dupb, wu_ref[k], NT_DIMS, preferred_element_type=F32)
        xhat = xh_ref[...]
        dpre_ref[...] = _layer_norm_bwd(dx1, xhat, rs_ref[...], g_ref[...])

        @pl.when(i == 0)
        def _():
            sums_ref[...] = jnp.zeros_like(sums_ref)

        sums_ref[0:1, :] += jnp.sum(dx1 * xhat, axis=0, keepdims=True)
        sums_ref[1:2, :] += jnp.sum(dx1, axis=0, keepdims=True)

    row = lambda w: pl.BlockSpec((tm, w), lambda i: (i, 0))
    return pl.pallas_call(
        body, name="mlp_bwd_ln1", grid=(s // tm,),
        in_specs=[row(D_FF), row(D_MODEL), _resident_weight(), _resident_weight(), row(D_MODEL), row(1),
                  pl.BlockSpec((1, D_MODEL), lambda i: (0, 0))],
        out_specs=[row(D_FF), row(D_MODEL), pl.BlockSpec((SUBLANES, D_MODEL), lambda i: (0, 0))],
        out_shape=[SDS((s, D_FF), BF16), SDS((s, D_MODEL), F32), SDS((SUBLANES, D_MODEL), F32)],
        compiler_params=_params(("arbitrary",)))(relu_up, dpre2, w_up, w_down, xhat1, rstd1, g1)


def _grad_tn(a, b, name, out_cols, stacked):
    s, ka = a.shape
    n = b.shape[1]
    ts = _row_tile(s, 2048)
    n_steps = s // ts
    if stacked:
        tka, tn = ka, out_cols
        grid = (1, n // tn, n_steps)
        shape = (n // tn, ka, tn)
        out_spec = lambda: pl.BlockSpec((None, tka, tn), lambda r, c, t: (c, 0, 0))
    else:
        tka, tn = min(ka, 1024), n
        grid = (ka // tka, 1, n_steps)
        shape = (ka, n)
        out_spec = lambda: pl.BlockSpec((tka, tn), lambda r, c, t: (r, 0))

    def body(a_ref, b_ref, o_ref, ob_ref):
        t = pl.program_id(2)

        @pl.when(t == 0)
        def _():
            o_ref[...] = jnp.zeros_like(o_ref)

        o_ref[...] += lax.dot_general(a_ref[...].astype(BF16), b_ref[...].astype(BF16), TN_DIMS,
                                      preferred_element_type=F32)

        @pl.when(t == n_steps - 1)
        def _():
            ob_ref[...] = o_ref[...].astype(BF16)

    return pl.pallas_call(
        body, name=name, grid=grid,
        in_specs=[pl.BlockSpec((ts, tka), lambda r, c, t: (t, r)),
                  pl.BlockSpec((ts, tn), lambda r, c, t: (t, c))],
        out_specs=[out_spec(), out_spec()], out_shape=[SDS(shape, F32), SDS(shape, BF16)],
        compiler_params=_params(("parallel", "parallel", "arbitrary")))(a, b)


def _dmix(dpre1, w_out):
    s = dpre1.shape[0]
    tm = _row_tile(s, 512)

    def body(d_ref, w_ref, dc_ref, da_ref):
        db = d_ref[...].astype(BF16)
        dc_ref[...] = lax.dot_general(db, w_ref[0:CONV_WIDTH, :], NT_DIMS, preferred_element_type=F32)
        da_ref[...] = lax.dot_general(db, w_ref[CONV_WIDTH:, :], NT_DIMS, preferred_element_type=F32)

    return pl.pallas_call(
        body, name="dmix", grid=(s // tm,),
        in_specs=[pl.BlockSpec((tm, D_MODEL), lambda i: (i, 0)),
                  pl.BlockSpec((D_MODEL, D_MODEL), lambda i: (0, 0))],
        out_specs=[pl.BlockSpec((tm, CONV_WIDTH), lambda i: (i, 0)),
                   pl.BlockSpec((tm, ATTN_WIDTH), lambda i: (i, 0))],
        out_shape=[SDS((s, CONV_WIDTH), F32), SDS((s, ATTN_WIDTH), F32)],
        compiler_params=_params(("parallel",)))(dpre1, w_out)


def _sum_with_peers(own_ref, r_ref, o_ref):
    acc = own_ref[...]
    for f in range(r_ref.shape[0]):
        acc = acc + r_ref[f].astype(F32)
    o_ref[...] = acc


def _grad_x(kc_idx, dproj, w_in, dpre1, chip_sums, earlier):
    s = dproj.shape[0]
    tm = _row_tile(s, 512)
    steps = s // tm
    n_peers = len(DEVICE_FLIPS)
    n_chips = len(CHIP_FLIPS)
    n_e = len(earlier)

    def body(kc_ref, dp_ref, w_ref, d1_ref, *rest):
        sum_ins, g_in = rest[:2 * n_e], rest[2 * n_e]
        o_ref, g_out = rest[2 * n_e + 1], rest[2 * n_e + 2]
        sum_outs, sems = rest[2 * n_e + 3:3 * n_e + 3], rest[3 * n_e + 3:]
        copies = functools.partial(_chip_reduce_copies, g_in, g_out, sems)
        i = pl.program_id(0)
        pl.when(i == 0)(functools.partial(_start_copies, copies))
        acc = ALPHA * d1_ref[...]
        for k in range(N_CHIPS):
            acc = acc + lax.dot_general(dp_ref[:, IN_SHARD * k:IN_SHARD * (k + 1)], w_ref[k], NT_DIMS,
                                        preferred_element_type=F32)
        o_ref[...] = acc
        for a in range(n_e):
            _sum_with_peers(sum_ins[2 * a], sum_ins[2 * a + 1], sum_outs[a])
        pl.when(i == steps - 1)(functools.partial(_finish_copies, copies))

    in_specs = [pl.BlockSpec((tm, IN_COLS), lambda i, kc: (i, 0)),
                pl.BlockSpec((N_CHIPS, D_MODEL, IN_SHARD), lambda i, kc: (0, 0, 0)),
                pl.BlockSpec((tm, D_MODEL), lambda i, kc: (i, 0))]
    out_specs = [pl.BlockSpec((tm, D_MODEL), lambda i, kc: (i, 0)), ANY]
    out_shape = [SDS((s, D_MODEL), F32), SDS((n_chips,) + chip_sums.shape[1:], chip_sums.dtype)]
    operands = []
    for own, recv in earlier:
        _, _, h, cols = own.shape
        th = h // steps
        in_specs.append(pl.BlockSpec((None, None, th, cols), lambda i, kc: (kc[0], kc[1], i, 0)))
        in_specs.append(pl.BlockSpec((n_peers, th, cols), lambda i, kc: (0, i, 0)))
        out_specs.append(pl.BlockSpec((th, cols), lambda i, kc: (kc[1] * steps + i, 0)))
        out_shape.append(SDS((2 * h, cols), F32))
        operands += [own, recv]
    grid_spec = pltpu.PrefetchScalarGridSpec(
        num_scalar_prefetch=1, grid=(steps,), in_specs=in_specs + [ANY], out_specs=out_specs,
        scratch_shapes=[pltpu.SemaphoreType.DMA((n_chips,)), pltpu.SemaphoreType.DMA((n_chips,))])
    return pl.pallas_call(
        body, name="grad_x", grid_spec=grid_spec, out_shape=out_shape,
        compiler_params=_params(("arbitrary",)))(kc_idx, dproj, w_in, dpre1, *operands, chip_sums)


def _adamw(w, g, m, v, name):
    r, c = w.shape
    tr = _row_tile(r, 256)

    def body(w_ref, g_ref, m_ref, v_ref, go_ref, d_ref, nm_ref, nv_ref):
        g_v = g_ref[...]
        go_ref[...] = g_v
        nm = ADAM_B1 * m_ref[...] + (1.0 - ADAM_B1) * g_v
        nv = ADAM_B2 * v_ref[...] + (1.0 - ADAM_B2) * (g_v * g_v)
        m_hat = nm / (1.0 - ADAM_B1 ** ADAM_STEP)
        v_hat = nv / (1.0 - ADAM_B2 ** ADAM_STEP)
        d_ref[...] = -ADAM_LR * (m_hat / (jnp.sqrt(v_hat) + ADAM_EPS) + ADAM_WD * w_ref[...])
        nm_ref[...] = nm
        nv_ref[...] = nv

    spec = lambda: pl.BlockSpec((tr, c), lambda i: (i, 0))
    return pl.pallas_call(
        body, name=name, grid=(r // tr,),
        in_specs=[spec(), spec(), spec(), spec()], out_specs=[spec(), spec(), spec(), spec()],
        out_shape=[SDS((r, c), F32)] * 4, compiler_params=_params(("parallel",)))(w, g, m, v)


def _sum_partials(kc_idx, own, recv, name):
    h, cols = own.shape
    th = _row_tile(h, 128)
    n_peers = recv.shape[0]

    def body(kc_ref, own_ref, r_ref, o_ref):
        _sum_with_peers(own_ref, r_ref, o_ref)

    grid_spec = pltpu.PrefetchScalarGridSpec(
        num_scalar_prefetch=1, grid=(h // th,),
        in_specs=[pl.BlockSpec((th, cols), lambda t, kc: (t, 0)),
                  pl.BlockSpec((n_peers, th, cols), lambda t, kc: (0, t, 0))],
        out_specs=pl.BlockSpec((th, cols), lambda t, kc: (kc[1] * (h // th) + t, 0)))
    return pl.pallas_call(
        body, name=name, grid_spec=grid_spec, out_shape=SDS((2 * h, cols), F32),
        compiler_params=_params(("parallel",)))(kc_idx, own, recv)


def _add_sibling(kc_idx, grad, recv):
    _, _, h, cols = grad.shape
    th = _row_tile(h, 128)

    def body(kc_ref, g_ref, r_ref, sums_ref, own_ref):
        total = g_ref[...] + r_ref[...].astype(F32)
        sums_ref[...] = total.astype(BF16)

        @pl.when(pl.program_id(1) == kc_ref[0])
        def _():
            own_ref[...] = total

    grid_spec = pltpu.PrefetchScalarGridSpec(
        num_scalar_prefetch=1, grid=(h // th, N_CHIPS),
        in_specs=[pl.BlockSpec((None, None, th, cols), lambda t, k, kc: (k, kc[1], t, 0)),
                  pl.BlockSpec((None, th, cols), lambda t, k, kc: (k, t, 0))],
        out_specs=[pl.BlockSpec((None, th, cols), lambda t, k, kc: (k, t, 0)),
                   pl.BlockSpec((th, cols), lambda t, k, kc: (t, 0))])
    return pl.pallas_call(
        body, name="add_sibling_w_in", grid_spec=grid_spec,
        out_shape=[SDS((N_CHIPS, h, cols), BF16), SDS((h, cols), F32)],
        compiler_params=_params(("parallel", "arbitrary")))(kc_idx, grad, recv)


def _gather_weights(kc_idx, w_in_slots, conv_slots, later):
    n_l = len(later)
    steps = SUBLANES

    def body(kc_ref, *refs):
        cast_ins, cast_outs = refs[:n_l], refs[n_l + 2:2 * n_l + 2]
        w_buf, conv_buf = refs[2 * n_l + 2], refs[2 * n_l + 3]
        sems = refs[2 * n_l + 4:]
        i = pl.program_id(0)

        def first_hop():
            x, y, c = _position()
            sends, arrivals = _gather_chip_hop([w_buf], sems)
            mine = conv_buf.at[2 * x + y]
            for j, (fx, fy) in enumerate(CHIP_FLIPS):
                tx, ty = _flip(x, fx), _flip(y, fy)
                there = conv_buf.at[2 * tx + ty]
                sends.append(_remote_copy(mine, mine, sems, 6 + j, (tx, ty, c)))
                arrivals.append(_remote_copy(there, there, sems, 6 + j, (tx, ty, c)))
            return sends, arrivals

        pl.when(i == 0)(functools.partial(_start_copies, first_hop))
        for src, dst in zip(cast_ins, cast_outs):
            dst[...] = src[...].astype(BF16)

        @pl.when(i == steps - 1)
        def _():
            sends, arrivals = first_hop()
            for cp in arrivals:
                cp.wait_recv()
            _start_copies(functools.partial(_gather_sibling_hop, [w_buf], sems))
            _finish_copies(functools.partial(_gather_sibling_hop, [w_buf], sems))
            for cp in sends:
                cp.wait_send()

    in_specs, out_specs, out_shape = [], [], []
    for w in later:
        r, c = w.shape
        in_specs.append(pl.BlockSpec((r // steps, c), lambda i, kc: (i, 0)))
        out_specs.append(pl.BlockSpec((None, r // steps, c), lambda i, kc: (kc[0], i, 0)))
        out_shape.append(SDS((N_CHIPS, r, c), BF16))
    grid_spec = pltpu.PrefetchScalarGridSpec(
        num_scalar_prefetch=1, grid=(steps,), in_specs=in_specs + [ANY, ANY], out_specs=out_specs + [ANY, ANY],
        scratch_shapes=[pltpu.SemaphoreType.DMA((9,)), pltpu.SemaphoreType.DMA((9,))])
    return pl.pallas_call(
        body, name="gather_weights", grid_spec=grid_spec,
        out_shape=out_shape + [SDS(w_in_slots.shape, w_in_slots.dtype), SDS(conv_slots.shape, conv_slots.dtype)],
        input_output_aliases={n_l + 1: n_l, n_l + 2: n_l + 1},
        compiler_params=_params(("arbitrary",)))(kc_idx, *later, w_in_slots, conv_slots)


def _exchange_with_sibling(partial):
    h = partial.shape[1] // 2

    def body(g_in, g_out, send_sems, recv_sems):
        x, y, c = _position()
        theirs = pl.ds(pl.multiple_of((1 - c) * h, h), h)
        copies = [_remote_copy(g_in.at[k, theirs], g_out.at[k], (send_sems, recv_sems), k, (x, y, 1 - c))
                  for k in range(N_CHIPS)]
        for cp in copies:
            cp.start()
        for cp in copies:
            cp.wait_recv()
        for cp in copies:
            cp.wait_send()

    return pl.pallas_call(
        body, name="exchange_with_sibling", in_specs=[ANY], out_specs=ANY,
        out_shape=SDS((N_CHIPS, h, partial.shape[2]), partial.dtype),
        scratch_shapes=[pltpu.SemaphoreType.DMA((N_CHIPS,)), pltpu.SemaphoreType.DMA((N_CHIPS,))])(partial)


def _finish_exchange(pieces, vec):
    n = len(pieces)
    n_dev = 2 * N_CHIPS

    def body(*refs):
        v_ref = refs[n]
        outs, o_ref = refs[n + 1:2 * n + 1], refs[2 * n + 1]
        buf, send_sems, recv_sems = refs[2 * n + 2:]
        x, y, c = _position()
        sibling = (x, y, 1 - c)
        me = 4 * x + 2 * y + c
        buf[me] = v_ref[...]
        started = []
        for f, (fx, fy, fc) in enumerate(DEVICE_FLIPS):
            cp = pltpu.make_async_remote_copy(
                src_ref=v_ref, dst_ref=buf.at[me], send_sem=send_sems.at[n + f], recv_sem=recv_sems.at[n + f],
                device_id=(_flip(x, fx), _flip(y, fy), _flip(c, fc)), device_id_type=MESH)
            cp.start()
            started.append(cp)
        for a in range(n):
            h = pieces[a].shape[0] // 2
            mine = outs[a].at[pl.ds(pl.multiple_of(c * h, h), h)]
            cp = pltpu.make_async_remote_copy(
                src_ref=mine, dst_ref=mine, send_sem=send_sems.at[a], recv_sem=recv_sems.at[a],
                device_id=sibling, device_id_type=MESH)
            cp.start()
            started.append(cp)
        for a in range(n):
            h = pieces[a].shape[0] // 2
            theirs = outs[a].at[pl.ds(pl.multiple_of((1 - c) * h, h), h)]
            pltpu.make_async_remote_copy(
                src_ref=theirs, dst_ref=theirs, send_sem=send_sems.at[a], recv_sem=recv_sems.at[a],
                device_id=sibling, device_id_type=MESH).wait_recv()
        for f, (fx, fy, fc) in enumerate(DEVICE_FLIPS):
            src = 4 * _flip(x, fx) + 2 * _flip(y, fy) + _flip(c, fc)
            pltpu.make_async_remote_copy(
                src_ref=v_ref, dst_ref=buf.at[src], send_sem=send_sems.at[n + f], recv_sem=recv_sems.at[n + f],
                device_id=(x, y, c), device_id_type=MESH).wait_recv()
        for cp in started:
            cp.wait_send()
        acc = buf[0]
        for d in range(1, n_dev):
            acc = acc + buf[d]
        o_ref[...] = acc

    vmem = pl.BlockSpec(memory_space=pltpu.VMEM)
    out_shape = [SDS(p.shape, p.dtype) for p in pieces] + [SDS(vec.shape, vec.dtype)]
    n_sems = n + n_dev - 1
    return pl.pallas_call(
        body, name="finish_exchange", in_specs=[ANY] * n + [vmem], out_specs=[ANY] * n + [vmem],
        out_shape=out_shape, input_output_aliases={a: a for a in range(n)},
        scratch_shapes=[pltpu.VMEM((n_dev,) + vec.shape, vec.dtype), pltpu.SemaphoreType.DMA((n_sems,)),
                        pltpu.SemaphoreType.DMA((n_sems,))])(*pieces, vec)


def _constants():
    r = jnp.arange(2 * KEY_BLOCK)[:, None] % KEY_BLOCK
    c = jnp.arange(2 * KEY_BLOCK)[None, :]
    later = jnp.where(c < KEY_BLOCK, r > c, True).astype(BF16)
    earlier = jnp.where(c < KEY_BLOCK, r < c, True).astype(BF16)
    upto = jnp.where(c < KEY_BLOCK, r <= c, True).astype(BF16)
    gr = (jnp.arange(2 * LANES)[:, None] % LANES) // GROUP
    gc = jnp.arange(LANES)[None, :] // GROUP
    gmat = (gr == gc).astype(BF16)
    return later, jnp.stack([earlier, upto]), gmat


def _rows(v):
    return v.reshape(-1, LANES)


def kernel(x, w_in, conv_w, g_conv, g_attn, w_out, ln1_g, ln1_b, w_up, w_down, ln2_g, ln2_b, loss_target, m_w_in, m_conv_w, m_g_conv, m_g_attn, m_w_out, m_ln1_g, m_ln1_b, m_w_up, m_w_down, m_ln2_g, m_ln2_b, v_w_in, v_conv_w, v_g_conv, v_g_attn, v_w_out, v_ln1_g, v_ln1_b, v_w_up, v_w_down, v_ln2_g, v_ln2_b):
    xs, target = x[0], loss_target[0]
    mesh_x, mesh_y, mesh_c = _position()
    k_idx = 2 * mesh_x + mesh_y
    kc_idx = jnp.stack([k_idx, mesh_c]).astype(jnp.int32)
    tri_later, tri_earlier, gmat = _constants()

    w_in_b = _cast_into_slot(kc_idx, w_in[0], "cast_w_in")
    conv_slot = jnp.pad(conv_w, ((0, 0), (0, SUBLANES - conv_w.shape[1]), (0, 0)))
    conv_b = lax.dynamic_update_slice(jnp.zeros((N_CHIPS, SUBLANES, LANES), F32), conv_slot, (k_idx, 0, 0))
    w_out_b, w_up_b, w_down_b, w_in_f, conv_f = _gather_weights(
        kc_idx, w_in_b, conv_b, [w_out[0], w_up[0], w_down[0]])
    taps = jnp.transpose(conv_f, (1, 0, 2)).reshape(SUBLANES, CONV_WIDTH)

    gates, qkv, xs_b = _proj(xs, w_in_f)
    ycn = _conv_fwd(gates, taps, g_conv, gmat)
    o, yan, tot, cut, w_out_f, w_up_f, w_down_f = _attn_fwd(
        qkv, g_attn, tri_later, gmat, [w_out_b, w_up_b, w_down_b])
    w_out_f = w_out_f.reshape(D_MODEL, D_MODEL)
    x1, xhat1, rstd1, x1_b = _mix_ln1(ycn, yan, w_out_f, xs, ln1_g, ln1_b)
    dpre2, ln2_sums, loss_sum, relu_up, hid, dpre2_b = _mlp_fwd_loss(x1, w_up_f, w_down_f, target, ln2_g, ln2_b)

    dup, dpre1, ln1_sums = _mlp_bwd_ln1(relu_up, dpre2, w_up_f, w_down_f, xhat1, rstd1, ln1_g)
    gw_up = _grad_tn(x1_b, dup, "grad_w_up", FF_SHARD, True)
    gw_down = [g.reshape(N_CHIPS, FF_SHARD, D_MODEL) for g in _grad_tn(hid, dpre2_b, "grad_w_down", D_MODEL, False)]
    gw_out_conv = _grad_tn(ycn, dpre1, "grad_w_out_conv", D_MODEL, False)
    gw_out_attn = _grad_tn(yan, dpre1, "grad_w_out_attn", D_MODEL, False)
    gw_out = [jnp.concatenate([gc_, ga_], axis=0).reshape(N_CHIPS, D_MODEL // N_CHIPS, D_MODEL)
              for gc_, ga_ in zip(gw_out_conv, gw_out_attn)]
    dycn, dyan = _dmix(dpre1, w_out_f)
    dq, dk, dv, gattn_sums, recv_out, recv_up, recv_down = _attn_bwd(
        qkv, o, tot, dyan, g_attn, tri_earlier, gmat, cut, [gw_out[1], gw_up[1], gw_down[1]])
    dbg, dy, conv_sums = _conv_bwd_gate(gates, dycn, taps, g_conv, gmat)
    dproj = _dproj_assemble(gates, dy, dbg, dq, dk, dv, taps)
    gw_in = _grad_tn(xs_b, dproj, "grad_w_in", IN_SHARD, True)
    halves = lambda g: g.reshape(N_CHIPS, 2, g.shape[1] // 2, g.shape[2])
    chip_sums, own_sum = _add_sibling(kc_idx, halves(gw_in[0]), _exchange_with_sibling(gw_in[1]))
    grad_x, recv_in, p_out, p_up, p_down = _grad_x(
        kc_idx, dproj, w_in_f, dpre1, chip_sums,
        [(halves(gw_out[0]), recv_out), (halves(gw_up[0]), recv_up), (halves(gw_down[0]), recv_down)])
    pieces = [_sum_partials(kc_idx, own_sum, recv_in, "sum_partials_w_in"), p_out, p_up, p_down]
    conv_rows = jnp.transpose(conv_sums[0:3].reshape(3, N_CHIPS, LANES), (1, 0, 2)).reshape(3 * N_CHIPS, LANES)
    small = jnp.concatenate([
        loss_sum, _rows(conv_sums[3]), _rows(gattn_sums[0]), _rows(ln1_sums[0]), _rows(ln1_sums[1]),
        _rows(ln2_sums[0]), _rows(ln2_sums[1]), conv_rows,
        jnp.zeros((SMALL_ROWS - ROW_CONVW - 3 * N_CHIPS, LANES), F32)], axis=0)
    g_w_in, g_w_out, g_w_up, g_w_down, total = _finish_exchange(pieces, small)
    loss = total[ROW_LOSS, 0]
    g_conv_w = lax.dynamic_slice(total, (ROW_CONVW + 3 * k_idx, 0), (3, LANES))

    def pack(gc_, ga_, l1g, l1b, l2g, l2b, cw):
        return jnp.concatenate([_rows(gc_), _rows(ga_), _rows(l1g), _rows(l1b), _rows(l2g), _rows(l2b), cw[0],
                                jnp.zeros((PARAM_ROWS + SUBLANES - ROW_CONVW - 3, LANES), F32)], axis=0)

    small_w = pack(g_conv, g_attn, ln1_g, ln1_b, ln2_g, ln2_b, conv_w)
    small_m = pack(m_g_conv, m_g_attn, m_ln1_g, m_ln1_b, m_ln2_g, m_ln2_b, m_conv_w)
    small_v = pack(v_g_conv, v_g_attn, v_ln1_g, v_ln1_b, v_ln2_g, v_ln2_b, v_conv_w)
    small_g = jnp.concatenate([total[ROW_GCONV:ROW_CONVW], g_conv_w,
                               jnp.zeros((PARAM_ROWS + SUBLANES - ROW_CONVW - 3, LANES), F32)], axis=0)
    small_out = _adamw(small_w, small_g, small_m, small_v, "adamw_small")

    def unpack(p):
        off = ROW_GCONV
        vec = lambda a, b: p[a - off:b - off].reshape(1, -1)
        return {"g_conv": vec(ROW_GCONV, ROW_GATTN), "g_attn": vec(ROW_GATTN, ROW_LN1G),
                "ln1_g": vec(ROW_LN1G, ROW_LN1B), "ln1_b": vec(ROW_LN1B, ROW_LN2G),
                "ln2_g": vec(ROW_LN2G, ROW_LN2B), "ln2_b": vec(ROW_LN2B, ROW_CONVW),
                "conv_w": p[ROW_CONVW - off:ROW_CONVW - off + 3][None]}

    big_out = {
        "w_in": _adamw(w_in[0], g_w_in, m_w_in[0], v_w_in[0], "adamw_w_in"),
        "w_out": _adamw(w_out[0], g_w_out, m_w_out[0], v_w_out[0], "adamw_w_out"),
        "w_up": _adamw(w_up[0], g_w_up, m_w_up[0], v_w_up[0], "adamw_w_up"),
        "w_down": _adamw(w_down[0], g_w_down, m_w_down[0], v_w_down[0], "adamw_w_down"),
    }
    order = ["w_in", "conv_w", "g_conv", "g_attn", "w_out", "ln1_g", "ln1_b", "w_up", "w_down", "ln2_g", "ln2_b"]
    small_parts = [unpack(p) for p in small_out]

    def leaf(kind, name):
        if name in big_out:
            return big_out[name][kind][None]
        return small_parts[kind][name]

    outs = [loss, grad_x[None]]
    for kind in range(4):
        outs.extend(leaf(kind, name) for name in order)
    return tuple(outs)
```

```python
import functools

import jax
import jax.numpy as jnp
from jax import lax
from jax.experimental import pallas as pl
from jax.experimental.pallas import tpu as pltpu

F32 = jnp.float32
BF16 = jnp.bfloat16
SDS = jax.ShapeDtypeStruct

D_MODEL = 1024
CONV_WIDTH = 512
ATTN_WIDTH = 512
GROUP = 64
GATE_COLS = 3 * CONV_WIDTH
QKV_COLS = 3 * ATTN_WIDTH
IN_COLS = GATE_COLS + QKV_COLS
D_FF = 4 * D_MODEL
N_CHIPS = 4
IN_SHARD = IN_COLS // N_CHIPS
FF_SHARD = D_FF // N_CHIPS
ALPHA = float(2.0 ** 0.25)
LN_EPS = 1e-5
RMS_EPS = 1e-6
ATTN_SCALE = GROUP ** -0.5
LOG2_E = 1.4426950408889634
ADAM_LR = 0.001
ADAM_B1 = 0.9
ADAM_B2 = 0.999
ADAM_EPS = 1e-08
ADAM_WD = 0.01
ADAM_STEP = 10

LANES = 128
SUBLANES = 8
KEY_BLOCK = 128
ATTN_Q_TILE = 512
ATTN_KEY_BLOCKS = 2
ATTN_DIAG_GROUPS = 2
ATTN_DEAD_LOG2 = 200.0
VMEM_LIMIT = 56 * 1024 * 1024

MESH = pl.DeviceIdType.MESH
CHIP_FLIPS = ((1, 0), (0, 1), (1, 1))
DEVICE_FLIPS = tuple((fx, fy, fc) for fx in (0, 1) for fy in (0, 1) for fc in (0, 1))[1:]
NT_DIMS = (((1,), (1,)), ((), ()))
TN_DIMS = (((0,), (0,)), ((), ()))

ROW_LOSS = 0
ROW_GCONV = 8
ROW_GATTN = 12
ROW_LN1G = 16
ROW_LN1B = 24
ROW_LN2G = 32
ROW_LN2B = 40
ROW_CONVW = 48
SMALL_ROWS = 64


def _params(sem=None):
    return pltpu.CompilerParams(dimension_semantics=sem, vmem_limit_bytes=VMEM_LIMIT)


def _flip(v, f):
    return 1 - v if f else v


def _position():
    return lax.axis_index("x"), lax.axis_index("y"), lax.axis_index("c")


def _hilo(v):
    hi = v.astype(BF16)
    lo = (v - hi.astype(F32)).astype(BF16)
    return jnp.concatenate([hi, lo], axis=1)


def _hilo_dot(v, mat):
    return jnp.dot(_hilo(v), mat, preferred_element_type=F32)


def _group_sum(v, gmat):
    parts = [_hilo_dot(v[:, LANES * j:LANES * (j + 1)], gmat) for j in range(v.shape[1] // LANES)]
    return parts[0] if len(parts) == 1 else jnp.concatenate(parts, axis=1)


def _softplus_terms(z):
    sp = jnp.log2(1.0 + jnp.exp2(-jnp.abs(z)))
    log_beta = jnp.minimum(z, 0.0) - sp
    return log_beta, log_beta - z


def _layer_norm_fwd(pre, g, b):
    mu = jnp.mean(pre, axis=-1, keepdims=True)
    d = pre - mu
    var = jnp.mean(d * d, axis=-1, keepdims=True)
    rstd = lax.rsqrt(var + LN_EPS)
    xhat = d * rstd
    return xhat * g + b, xhat, rstd


def _layer_norm_bwd(dy, xhat, rstd, g):
    dxh = dy * g
    m1 = jnp.mean(dxh, axis=-1, keepdims=True)
    m2 = jnp.mean(dxh * xhat, axis=-1, keepdims=True)
    return rstd * (dxh - m1 - xhat * m2)


def _row_tile(s, want):
    return min(s, want)


def _cast_into_slot(kc_idx, w, name):
    r, c = w.shape
    tr = _row_tile(r, 256)

    def body(kc_ref, w_ref, o_ref):
        o_ref[...] = w_ref[...].astype(BF16)

    grid_spec = pltpu.PrefetchScalarGridSpec(
        num_scalar_prefetch=1, grid=(r // tr,),
        in_specs=[pl.BlockSpec((tr, c), lambda i, kc: (i, 0))],
        out_specs=pl.BlockSpec((None, tr, c), lambda i, kc: (kc[0], i, 0)))
    return pl.pallas_call(
        body, name=name, grid_spec=grid_spec, out_shape=SDS((N_CHIPS, r, c), BF16),
        compiler_params=_params(("parallel",)))(kc_idx, w)


def _proj(x, w_in):
    s = x.shape[0]
    tm = _row_tile(s, 512)

    def body(x_ref, w_ref, gates_ref, qkv_ref, xb_ref):
        xb = x_ref[...].astype(BF16)
        xb_ref[...] = xb
        for k in range(N_CHIPS):
            acc = jnp.dot(xb, w_ref[k], preferred_element_type=F32)
            if k < 2:
                gates_ref[:, IN_SHARD * k:IN_SHARD * (k + 1)] = acc
            else:
                qkv_ref[:, IN_SHARD * (k - 2):IN_SHARD * (k - 1)] = acc.astype(BF16)

    return pl.pallas_call(
        body, name="proj", grid=(s // tm,),
        in_specs=[pl.BlockSpec((tm, D_MODEL), lambda i: (i, 0)),
                  pl.BlockSpec((N_CHIPS, D_MODEL, IN_SHARD), lambda i: (0, 0, 0))],
        out_specs=[pl.BlockSpec((tm, GATE_COLS), lambda i: (i, 0)),
                   pl.BlockSpec((tm, QKV_COLS), lambda i: (i, 0)),
                   pl.BlockSpec((tm, D_MODEL), lambda i: (i, 0))],
        out_shape=[SDS((s, GATE_COLS), F32), SDS((s, QKV_COLS), BF16), SDS((s, D_MODEL), BF16)],
        compiler_params=_params(("parallel",)))(x, w_in)


def _conv_forward_values(g_ref, halo_ref, taps_ref, first_block):
    gates = g_ref[...]
    tr = gates.shape[0]
    bg = gates[:, :CONV_WIDTH]
    cg = gates[:, CONV_WIDTH:2 * CONV_WIDTH]
    h = gates[:, 2 * CONV_WIDTH:]
    u = cg * h

    def prev(r):
        v = halo_ref[r:r + 1, CONV_WIDTH:2 * CONV_WIDTH] * halo_ref[r:r + 1, 2 * CONV_WIDTH:GATE_COLS]
        return jnp.where(first_block, 0.0, v)

    row = lax.broadcasted_iota(jnp.int32, (tr, CONV_WIDTH), 0)
    u1 = jnp.where(row == 0, prev(7), pltpu.roll(u, 1, 0))
    u2 = jnp.where(row == 0, prev(6), jnp.where(row == 1, prev(7), pltpu.roll(u, 2, 0)))
    y = taps_ref[0:1, :] * u2 + taps_ref[1:2, :] * u1 + taps_ref[2:3, :] * u
    return bg, cg, h, u, u1, u2, y


def _conv_fwd(gates, taps, g_conv, gmat):
    s = gates.shape[0]
    tr = _row_tile(s, 512)
    hb = tr // SUBLANES

    def body(g_ref, halo_ref, taps_ref, gain_ref, gmat_ref, out_ref):
        i = pl.program_id(0)
        bg, _, _, _, _, _, y = _conv_forward_values(g_ref, halo_ref, taps_ref, i == 0)
        yc = bg * y
        ms = _group_sum(yc * yc, gmat_ref[...]) * (1.0 / GROUP)
        out_ref[...] = (yc * lax.rsqrt(ms + RMS_EPS) * gain_ref[...]).astype(BF16)

    return pl.pallas_call(
        body, name="conv_fwd", grid=(s // tr,),
        in_specs=[pl.BlockSpec((tr, GATE_COLS), lambda i: (i, 0)),
                  pl.BlockSpec((SUBLANES, GATE_COLS), lambda i: (jnp.maximum(i * hb - 1, 0), 0)),
                  pl.BlockSpec((SUBLANES, CONV_WIDTH), lambda i: (0, 0)),
                  pl.BlockSpec((1, CONV_WIDTH), lambda i: (0, 0)),
                  pl.BlockSpec((2 * LANES, LANES), lambda i: (0, 0))],
        out_specs=pl.BlockSpec((tr, CONV_WIDTH), lambda i: (i, 0)),
        out_shape=SDS((s, CONV_WIDTH), BF16),
        compiler_params=_params(("parallel",)))(gates, gates, taps, g_conv, gmat)


def _conv_bwd_gate(gates, dycn, taps, g_conv, gmat):
    s = gates.shape[0]
    tr = _row_tile(s, 512)
    hb = tr // SUBLANES

    def body(g_ref, halo_ref, dn_ref, taps_ref, gain_ref, gmat_ref, dbg_ref, dy_ref, sums_ref):
        i = pl.program_id(0)
        bg, _, _, u, u1, u2, y = _conv_forward_values(g_ref, halo_ref, taps_ref, i == 0)
        gmat_v = gmat_ref[...]
        yc = bg * y
        rstd = lax.rsqrt(_group_sum(yc * yc, gmat_v) * (1.0 / GROUP) + RMS_EPS)
        n = yc * rstd
        dout = dn_ref[...]
        dn = dout * gain_ref[...]
        dyc = rstd * (dn - n * (_group_sum(dn * n, gmat_v) * (1.0 / GROUP)))
        dbg_ref[...] = (dyc * y).astype(BF16)
        dy = dyc * bg
        dy_ref[...] = dy

        @pl.when(i == 0)
        def _():
            sums_ref[...] = jnp.zeros_like(sums_ref)

        sums_ref[0:1, :] += jnp.sum(dy * u2, axis=0, keepdims=True)
        sums_ref[1:2, :] += jnp.sum(dy * u1, axis=0, keepdims=True)
        sums_ref[2:3, :] += jnp.sum(dy * u, axis=0, keepdims=True)
        sums_ref[3:4, :] += jnp.sum(dout * n, axis=0, keepdims=True)

    return pl.pallas_call(
        body, name="conv_bwd_gate", grid=(s // tr,),
        in_specs=[pl.BlockSpec((tr, GATE_COLS), lambda i: (i, 0)),
                  pl.BlockSpec((SUBLANES, GATE_COLS), lambda i: (jnp.maximum(i * hb - 1, 0), 0)),
                  pl.BlockSpec((tr, CONV_WIDTH), lambda i: (i, 0)),
                  pl.BlockSpec((SUBLANES, CONV_WIDTH), lambda i: (0, 0)),
                  pl.BlockSpec((1, CONV_WIDTH), lambda i: (0, 0)),
                  pl.BlockSpec((2 * LANES, LANES), lambda i: (0, 0))],
        out_specs=[pl.BlockSpec((tr, CONV_WIDTH), lambda i: (i, 0)),
                   pl.BlockSpec((tr, CONV_WIDTH), lambda i: (i, 0)),
                   pl.BlockSpec((SUBLANES, CONV_WIDTH), lambda i: (0, 0))],
        out_shape=[SDS((s, CONV_WIDTH), BF16), SDS((s, CONV_WIDTH), F32), SDS((SUBLANES, CONV_WIDTH), F32)],
        compiler_params=_params(("arbitrary",)))(gates, gates, dycn, taps, g_conv, gmat)


def _dproj_assemble(gates, dy, dbg, dq, dk, dv, taps):
    s = gates.shape[0]
    tr = _row_tile(s, 512)
    hb = tr // SUBLANES
    last = s // SUBLANES - 1
    n_blocks = s // tr

    def body(g_ref, dy_ref, halo_ref, dbg_ref, dq_ref, dk_ref, dv_ref, taps_ref, out_ref):
        i = pl.program_id(0)
        gates_v = g_ref[...]
        cg = gates_v[:, CONV_WIDTH:2 * CONV_WIDTH]
        h = gates_v[:, 2 * CONV_WIDTH:]
        dy_v = dy_ref[...]
        last_block = i == n_blocks - 1
        nxt = lambda r: jnp.where(last_block, 0.0, halo_ref[r:r + 1, :])
        row = lax.broadcasted_iota(jnp.int32, (tr, CONV_WIDTH), 0)
        d1 = jnp.where(row == tr - 1, nxt(0), pltpu.roll(dy_v, tr - 1, 0))
        d2 = jnp.where(row == tr - 1, nxt(1), jnp.where(row == tr - 2, nxt(0), pltpu.roll(dy_v, tr - 2, 0)))
        du = taps_ref[2:3, :] * dy_v + taps_ref[1:2, :] * d1 + taps_ref[0:1, :] * d2
        out_ref[:, 0:CONV_WIDTH] = dbg_ref[...]
        out_ref[:, CONV_WIDTH:2 * CONV_WIDTH] = (du * h).astype(BF16)
        out_ref[:, 2 * CONV_WIDTH:GATE_COLS] = (du * cg).astype(BF16)
        out_ref[:, GATE_COLS:GATE_COLS + ATTN_WIDTH] = dq_ref[...]
        out_ref[:, GATE_COLS + ATTN_WIDTH:GATE_COLS + 2 * ATTN_WIDTH] = dk_ref[...].astype(BF16)
        out_ref[:, GATE_COLS + 2 * ATTN_WIDTH:] = dv_ref[...].astype(BF16)

    row_spec = lambda w: pl.BlockSpec((tr, w), lambda i: (i, 0))
    return pl.pallas_call(
        body, name="dproj_assemble", grid=(s // tr,),
        in_specs=[row_spec(GATE_COLS), row_spec(CONV_WIDTH),
                  pl.BlockSpec((SUBLANES, CONV_WIDTH), lambda i: (jnp.minimum((i + 1) * hb, last), 0)),
                  row_spec(CONV_WIDTH), row_spec(ATTN_WIDTH), row_spec(ATTN_WIDTH), row_spec(ATTN_WIDTH),
                  pl.BlockSpec((SUBLANES, CONV_WIDTH), lambda i: (0, 0))],
        out_specs=row_spec(IN_COLS),
        out_shape=SDS((s, IN_COLS), BF16),
        compiler_params=_params(("parallel",)))(gates, dy, dy, dbg, dq, dk, dv, taps)


def _stack_heads(rows, nb):
    lane = lax.broadcasted_iota(jnp.int32, (1, LANES), 1)
    zero = jnp.zeros((KEY_BLOCK, LANES), rows.dtype)
    parts = []
    for blk in range(nb):
        r = rows[blk * KEY_BLOCK:(blk + 1) * KEY_BLOCK]
        parts.append(jnp.where(lane < GROUP, r, zero))
        parts.append(jnp.where(lane < GROUP, zero, r))
    return jnp.concatenate(parts, axis=0)


def _stack_hilo(v, n_cols):
    return jnp.concatenate([_hilo(v[:, c * KEY_BLOCK:(c + 1) * KEY_BLOCK]) for c in range(n_cols)], axis=0)


def _causal_mask(tq, nb, diag_base):
    shape = (tq, 2 * nb * KEY_BLOCK)
    row = lax.broadcasted_iota(jnp.int32, shape, 0)
    col = lax.broadcasted_iota(jnp.int32, shape, 1)
    key = diag_base + (col // (2 * KEY_BLOCK)) * KEY_BLOCK + col % KEY_BLOCK
    return key < row


ANY = pl.BlockSpec(memory_space=pl.ANY)


def _remote_copy(src, dst, sems, idx, target):
    return pltpu.make_async_remote_copy(src_ref=src, dst_ref=dst, send_sem=sems[0].at[idx], recv_sem=sems[1].at[idx],
                                        device_id=target, device_id_type=MESH)


def _gather_chip_hop(bufs, sems):
    x, y, c = _position()
    sends, arrivals = [], []
    for a, buf in enumerate(bufs):
        h = buf.shape[1] // 2
        rows = pl.ds(pl.multiple_of(c * h, h), h)
        mine = buf.at[2 * x + y, rows]
        for j, (fx, fy) in enumerate(CHIP_FLIPS):
            tx, ty = _flip(x, fx), _flip(y, fy)
            there = buf.at[2 * tx + ty, rows]
            sends.append(_remote_copy(mine, mine, sems, 6 * a + j, (tx, ty, c)))
            arrivals.append(_remote_copy(there, there, sems, 6 * a + j, (tx, ty, c)))
    return sends, arrivals


def _gather_sibling_hop(bufs, sems):
    x, y, c = _position()
    sends, arrivals = [], []
    for a, buf in enumerate(bufs):
        h = buf.shape[1] // 2
        mine, theirs = pl.ds(pl.multiple_of(c * h, h), h), pl.ds(pl.multiple_of((1 - c) * h, h), h)
        for j, (fx, fy) in enumerate(CHIP_FLIPS):
            kj = 2 * _flip(x, fx) + _flip(y, fy)
            landed, other = buf.at[kj, mine], buf.at[kj, theirs]
            sends.append(_remote_copy(landed, landed, sems, 6 * a + 3 + j, (x, y, 1 - c)))
            arrivals.append(_remote_copy(other, other, sems, 6 * a + 3 + j, (x, y, 1 - c)))
    return sends, arrivals


def _reduce_copies(ins, outs, sems):
    x, y, c = _position()
    sends, arrivals = [], []
    for a in range(len(ins)):
        h = ins[a].shape[1] // 2
        for f, (fx, fy, fc) in enumerate(DEVICE_FLIPS):
            tx, ty, tc = _flip(x, fx), _flip(y, fy), _flip(c, fc)
            src = ins[a].at[2 * tx + ty, pl.ds(pl.multiple_of(tc * h, h), h)]
            sends.append(_remote_copy(src, outs[a].at[f], sems, 7 * a + f, (tx, ty, tc)))
            arrivals.append(_remote_copy(outs[a].at[f], outs[a].at[f], sems, 7 * a + f, (tx, ty, tc)))
    return sends, arrivals


def _chip_reduce_copies(src, dst, sems):
    x, y, c = _position()
    sends, arrivals = [], []
    for j, (fx, fy) in enumerate(CHIP_FLIPS):
        tx, ty = _flip(x, fx), _flip(y, fy)
        sends.append(_remote_copy(src.at[2 * tx + ty], dst.at[j], sems, j, (tx, ty, c)))
        arrivals.append(_remote_copy(dst.at[j], dst.at[j], sems, j, (tx, ty, c)))
    return sends, arrivals


def _start_copies(make):
    sends, _ = make()
    for cp in sends:
        cp.start()


def _finish_copies(make):
    sends, arrivals = make()
    for cp in arrivals:
        cp.wait_recv()
    for cp in sends:
        cp.wait_send()


def _attn_fwd(qkv, g_attn, tri, gmat, shards):
    n_w = len(shards)
    s = qkv.shape[0]
    tq = _row_tile(s, ATTN_Q_TILE)
    tk = KEY_BLOCK
    nb = ATTN_KEY_BLOCKS
    width = nb * tk
    n_groups = ATTN_DIAG_GROUPS
    group = tq // n_groups
    pairs = ATTN_WIDTH // LANES

    def body(q_ref, k_ref, v_ref, gain_ref, tri_ref, gmat_ref, *rest):
        o_ref, yn_ref, tot_ref, cut_ref = rest[n_w:n_w + 4]
        w_bufs, sems = rest[n_w + 4:2 * n_w + 4], rest[2 * n_w + 4:]
        chip_hop = functools.partial(_gather_chip_hop, w_bufs, sems)
        sibling_hop = functools.partial(_gather_sibling_hop, w_bufs, sems)
        p, i = pl.program_id(0), pl.program_id(1)
        pl.when((p == 0) & (i == 0))(functools.partial(_start_copies, chip_hop))

        @pl.when((p == pairs - 1) & (i == 0))
        def _():
            for cp in chip_hop()[1]:
                cp.wait_recv()
            _start_copies(sibling_hop)

        q2 = q_ref[...]
        tri_v = tri_ref[...]

        def trip(s0, n_blk, rows, carry, diag_base):
            r0, nr = rows
            run = [carry[0], carry[1]]
            oacc = carry[2]
            ksel = _stack_heads(k_ref[pl.ds(s0, n_blk * tk), :], n_blk)
            vsel = _stack_heads(v_ref[pl.ds(s0, n_blk * tk), :], n_blk)
            z = lax.dot_general(q2[r0:r0 + nr], ksel, NT_DIMS, preferred_element_type=F32) * (ATTN_SCALE * LOG2_E)
            log_beta, log_keep = _softplus_terms(z)
            if diag_base is not None:
                valid = _causal_mask(nr, n_blk, diag_base)
                log_keep = jnp.where(valid, log_keep, 0.0)
            ct = jnp.dot(_stack_hilo(log_keep, 2 * n_blk), tri_v, preferred_element_type=F32)
            a_parts = [None] * (2 * n_blk)
            for c in reversed(range(2 * n_blk)):
                h = c % 2
                ct_c = ct[c * nr:(c + 1) * nr]
                a_parts[c] = jnp.exp2(log_beta[:, c * tk:(c + 1) * tk] + ct_c[:, :tk] + run[h])
                run[h] = run[h] + ct_c[:, tk:]
            a = jnp.concatenate(a_parts, axis=1)
            if diag_base is not None:
                a = jnp.where(valid, a, 0.0)
            oacc = oacc + jnp.dot(a.astype(BF16), vsel, preferred_element_type=F32)
            return run[0], run[1], oacc

        groups = []
        for g in range(n_groups):
            zeros = (jnp.zeros((group, tk), F32), jnp.zeros((group, tk), F32), jnp.zeros((group, LANES), F32))
            groups.append(trip(pl.multiple_of(i * tq, tq), (g + 1) * group // tk, (g * group, group), zeros,
                               -g * group))
        carry = tuple(jnp.concatenate([grp[j] for grp in groups], axis=0) for j in range(3))
        n_full = i * (tq // width)

        def alive(run_a, run_b):
            return jnp.max(jnp.maximum(run_a, run_b)) > -ATTN_DEAD_LOG2

        def earlier_trip(c):
            done, _, run_a, run_b, oacc = c
            s0 = pl.multiple_of((n_full - 1 - done) * width, width)
            run_a, run_b, oacc = trip(s0, nb, (0, tq), (run_a, run_b, oacc), None)
            return done + 1, alive(run_a, run_b), run_a, run_b, oacc

        done, _, run_a, run_b, oacc = lax.while_loop(
            lambda c: (c[0] < n_full) & c[1], earlier_trip, (jnp.int32(0), alive(carry[0], carry[1])) + carry)
        cut_ref[p * pl.num_programs(1) + i] = (n_full - done).astype(F32)
        lane = lax.broadcasted_iota(jnp.int32, (1, LANES), 1)
        o_ref[...] = oacc
        tot_ref[...] = jnp.where(lane < GROUP, run_a, run_b)
        ms = _group_sum(oacc * oacc, gmat_ref[...]) * (1.0 / GROUP)
        yn_ref[...] = (oacc * lax.rsqrt(ms + RMS_EPS) * gain_ref[...]).astype(BF16)

        @pl.when((p == pairs - 1) & (i == pl.num_programs(1) - 1))
        def _():
            for cp in chip_hop()[0]:
                cp.wait_send()
            _finish_copies(sibling_hop)

    blk = lambda: pl.BlockSpec((tq, LANES), lambda p, i: (i, p))
    return pl.pallas_call(
        body, name="attn_fwd", grid=(pairs, s // tq),
        in_specs=[pl.BlockSpec((tq, LANES), lambda p, i: (i, p)),
                  pl.BlockSpec((s, LANES), lambda p, i: (0, pairs + p)),
                  pl.BlockSpec((s, LANES), lambda p, i: (0, 2 * pairs + p)),
                  pl.BlockSpec((1, LANES), lambda p, i: (0, p)),
                  pl.BlockSpec((2 * tk, 2 * tk), lambda p, i: (0, 0)),
                  pl.BlockSpec((2 * LANES, LANES), lambda p, i: (0, 0))] + [ANY] * n_w,
        out_specs=[blk(), blk(), blk(), pl.BlockSpec(memory_space=pltpu.SMEM)] + [ANY] * n_w,
        out_shape=[SDS((s, ATTN_WIDTH), F32), SDS((s, ATTN_WIDTH), BF16), SDS((s, ATTN_WIDTH), F32),
                   SDS((pairs * (s // tq),), F32)]
        + [SDS(w.shape, w.dtype) for w in shards],
        input_output_aliases={6 + a: 4 + a for a in range(n_w)},
        scratch_shapes=[pltpu.SemaphoreType.DMA((6 * n_w,)), pltpu.SemaphoreType.DMA((6 * n_w,))],
        compiler_params=_params(("arbitrary", "arbitrary")))(qkv, qkv, qkv, g_attn, tri, gmat, *shards)


def _attn_bwd(qkv, o, tot, dyn, g_attn, tri, gmat, cut, partials):
    n_g = len(partials)
    s = qkv.shape[0]
    tq = _row_tile(s, ATTN_Q_TILE)
    tk = KEY_BLOCK
    nb = ATTN_KEY_BLOCKS
    width = nb * tk
    n_groups = ATTN_DIAG_GROUPS
    group = tq // n_groups
    pairs = ATTN_WIDTH // LANES

    def body(q_ref, k_ref, v_ref, o_ref, tot_ref, dyn_ref, gain_ref, tri_ref, gmat_ref, cut_ref, *rest):
        g_ins, (dq_ref, dk_ref, dv_ref, dg_ref) = rest[:n_g], rest[n_g:n_g + 4]
        g_outs, sems = rest[n_g + 4:2 * n_g + 4], rest[2 * n_g + 4:]
        copies = functools.partial(_reduce_copies, g_ins, g_outs, sems)
        p, i = pl.program_id(0), pl.program_id(1)
        pl.when((p == 0) & (i == 0))(functools.partial(_start_copies, copies))

        @pl.when(i == 0)
        def _():
            dk_ref[...] = jnp.zeros_like(dk_ref)
            dv_ref[...] = jnp.zeros_like(dv_ref)
            dg_ref[...] = jnp.zeros_like(dg_ref)

        gmat_v = gmat_ref[...]
        o_v = o_ref[...]
        rstd = lax.rsqrt(_group_sum(o_v * o_v, gmat_v) * (1.0 / GROUP) + RMS_EPS)
        n = o_v * rstd
        dout = dyn_ref[...]
        dg_ref[0:1, :] += jnp.sum(dout * n, axis=0, keepdims=True)
        dn = dout * gain_ref[...]
        do2 = (rstd * (dn - n * (_group_sum(dn * n, gmat_v) * (1.0 / GROUP)))).astype(BF16)
        q2 = q_ref[...]
        tot_v = tot_ref[...]
        tots = (jnp.broadcast_to(tot_v[:, 0:1], (tq, tk)), jnp.broadcast_to(tot_v[:, GROUP:GROUP + 1], (tq, tk)))
        tri_v, tri_incl_v = tri_ref[0], tri_ref[1]
        lane = lax.broadcasted_iota(jnp.int32, (1, LANES), 1)

        def trip(s0, n_blk, rows, carry, diag_base):
            r0, nr = rows
            rest_l = [carry[0], carry[1]]
            pref_g = [carry[2], carry[3]]
            dq = carry[4]
            q_rows, do_rows = q2[r0:r0 + nr], do2[r0:r0 + nr]
            ksel = _stack_heads(k_ref[pl.ds(s0, n_blk * tk), :], n_blk)
            vsel = _stack_heads(v_ref[pl.ds(s0, n_blk * tk), :], n_blk)
            z = lax.dot_general(q_rows, ksel, NT_DIMS, preferred_element_type=F32) * (ATTN_SCALE * LOG2_E)
            log_beta, log_keep = _softplus_terms(z)
            if diag_base is not None:
                valid = _causal_mask(nr, n_blk, diag_base)
                log_keep = jnp.where(valid, log_keep, 0.0)
            ctl = jnp.dot(_stack_hilo(log_keep, 2 * n_blk), tri_incl_v, preferred_element_type=F32)
            da = lax.dot_general(do_rows, vsel, NT_DIMS, preferred_element_type=F32)
            a_parts = []
            for c in range(2 * n_blk):
                h = c % 2
                ct_c = ctl[c * nr:(c + 1) * nr]
                cols = slice(c * tk, (c + 1) * tk)
                a_parts.append(jnp.exp2(log_beta[:, cols] + (rest_l[h] - ct_c[:, :tk])))
                rest_l[h] = rest_l[h] - ct_c[:, tk:]
            a = jnp.concatenate(a_parts, axis=1)
            if diag_base is not None:
                a = jnp.where(valid, a, 0.0)
            g = a * da
            ctg = jnp.dot(_stack_hilo(g, 2 * n_blk), tri_v, preferred_element_type=F32)
            dz_parts = []
            for c in range(2 * n_blk):
                h = c % 2
                ct_c = ctg[c * nr:(c + 1) * nr]
                cols = slice(c * tk, (c + 1) * tk)
                prefix = pref_g[h] + ct_c[:, :tk]
                pref_g[h] = pref_g[h] + ct_c[:, tk:]
                g_c = g[:, cols]
                dz_parts.append(g_c - jnp.exp2(log_beta[:, cols]) * (g_c + prefix))
            dz = jnp.concatenate(dz_parts, axis=1) * ATTN_SCALE
            if diag_base is not None:
                dz = jnp.where(valid, dz, 0.0)
            dzb = dz.astype(BF16)
            dq = dq + jnp.dot(dzb, ksel, preferred_element_type=F32)
            dkt = lax.dot_general(dzb, q_rows, TN_DIMS, preferred_element_type=F32)
            dvt = lax.dot_general(a.astype(BF16), do_rows, TN_DIMS, preferred_element_type=F32)
            for blk in range(n_blk):
                ra, rb = slice(2 * blk * tk, (2 * blk + 1) * tk), slice((2 * blk + 1) * tk, (2 * blk + 2) * tk)
                keys = pl.ds(pl.multiple_of(s0 + blk * tk, tk), tk)
                dk_ref[keys, :] += jnp.where(lane < GROUP, dkt[ra], dkt[rb])
                dv_ref[keys, :] += jnp.where(lane < GROUP, dvt[ra], dvt[rb])
            return rest_l[0], rest_l[1], pref_g[0], pref_g[1], dq

        zeros_qk = jnp.zeros((tq, tk), F32)
        carry = (tots[0], tots[1], zeros_qk, zeros_qk, jnp.zeros((tq, LANES), F32))
        n_full = i * (tq // width)
        first = jnp.clip(cut_ref[p * pl.num_programs(1) + i].astype(jnp.int32), 0, n_full)
        carry = lax.fori_loop(
            first, n_full, lambda t, c: trip(pl.multiple_of(t * width, width), nb, (0, tq), c, None), carry)
        dq_groups = []
        for g in range(n_groups):
            sub = tuple(x[g * group:(g + 1) * group] for x in carry)
            dq_groups.append(trip(pl.multiple_of(i * tq, tq), (g + 1) * group // tk, (g * group, group), sub,
                                  -g * group)[4])
        dq_ref[...] = jnp.concatenate(dq_groups, axis=0).astype(BF16)
        pl.when((p == pairs - 1) & (i == pl.num_programs(1) - 1))(functools.partial(_finish_copies, copies))

    blk = lambda: pl.BlockSpec((tq, LANES), lambda p, i: (i, p))
    col = lambda: pl.BlockSpec((s, LANES), lambda p, i: (0, p))
    n_peers = len(DEVICE_FLIPS)
    return pl.pallas_call(
        body, name="attn_bwd", grid=(pairs, s // tq),
        in_specs=[pl.BlockSpec((tq, LANES), lambda p, i: (i, p)),
                  pl.BlockSpec((s, LANES), lambda p, i: (0, pairs + p)),
                  pl.BlockSpec((s, LANES), lambda p, i: (0, 2 * pairs + p)),
                  blk(), blk(), blk(),
                  pl.BlockSpec((1, LANES), lambda p, i: (0, p)),
                  pl.BlockSpec((2, 2 * tk, 2 * tk), lambda p, i: (0, 0, 0)),
                  pl.BlockSpec((2 * LANES, LANES), lambda p, i: (0, 0)),
                  pl.BlockSpec(memory_space=pltpu.SMEM)] + [ANY] * n_g,
        out_specs=[blk(), col(), col(), pl.BlockSpec((SUBLANES, LANES), lambda p, i: (0, p))] + [ANY] * n_g,
        out_shape=[SDS((s, ATTN_WIDTH), BF16), SDS((s, ATTN_WIDTH), F32), SDS((s, ATTN_WIDTH), F32),
                   SDS((SUBLANES, ATTN_WIDTH), F32)]
        + [SDS((n_peers, g.shape[1] // 2, g.shape[2]), g.dtype) for g in partials],
        scratch_shapes=[pltpu.SemaphoreType.DMA((n_peers * n_g,)), pltpu.SemaphoreType.DMA((n_peers * n_g,))],
        compiler_params=_params(("arbitrary", "arbitrary")))(
            qkv, qkv, qkv, o, tot, dyn, g_attn, tri, gmat, cut, *partials)


def _mix_ln1(ycn, yan, w_out, x, g, b):
    s = x.shape[0]
    tm = _row_tile(s, 512)

    def body(yc_ref, ya_ref, w_ref, x_ref, g_ref, b_ref, x1_ref, xhat_ref, rstd_ref, x1b_ref):
        mix = jnp.dot(yc_ref[...], w_ref[0:CONV_WIDTH, :], preferred_element_type=F32)
        mix = mix + jnp.dot(ya_ref[...], w_ref[CONV_WIDTH:, :], preferred_element_type=F32)
        x1, xhat, rstd = _layer_norm_fwd(ALPHA * x_ref[...] + mix, g_ref[...], b_ref[...])
        x1_ref[...] = x1
        xhat_ref[...] = xhat
        rstd_ref[...] = rstd
        x1b_ref[...] = x1.astype(BF16)

    row = lambda w: pl.BlockSpec((tm, w), lambda i: (i, 0))
    vec = lambda: pl.BlockSpec((1, D_MODEL), lambda i: (0, 0))
    return pl.pallas_call(
        body, name="mix_ln1", grid=(s // tm,),
        in_specs=[row(CONV_WIDTH), row(ATTN_WIDTH), pl.BlockSpec((D_MODEL, D_MODEL), lambda i: (0, 0)),
                  row(D_MODEL), vec(), vec()],
        out_specs=[row(D_MODEL), row(D_MODEL), row(1), row(D_MODEL)],
        out_shape=[SDS((s, D_MODEL), F32), SDS((s, D_MODEL), F32), SDS((s, 1), F32), SDS((s, D_MODEL), BF16)],
        compiler_params=_params(("parallel",)))(ycn, yan, w_out, x, g, b)


def _mlp_fwd_loss(x1, w_up, w_down, target, g, b):
    s = x1.shape[0]
    tm = _row_tile(s, 256)

    def body(x1_ref, wu_ref, wd_ref, t_ref, g_ref, b_ref, dpre_ref, sums_ref, loss_ref, r_ref, hid_ref, dpreb_ref):
        i = pl.program_id(0)
        x1_v = x1_ref[...]
        xb = x1_v.astype(BF16)
        ffn = jnp.zeros((tm, D_MODEL), F32)
        for k in range(N_CHIPS):
            r = jnp.maximum(jnp.dot(xb, wu_ref[k], preferred_element_type=F32), 0.0)
            hid = (r * r).astype(BF16)
            r_ref[:, FF_SHARD * k:FF_SHARD * (k + 1)] = r.astype(BF16)
            hid_ref[:, FF_SHARD * k:FF_SHARD * (k + 1)] = hid
            ffn = ffn + jnp.dot(hid, wd_ref[k], preferred_element_type=F32)
        g_v = g_ref[...]
        x2, xhat, rstd = _layer_norm_fwd(ALPHA * x1_v + ffn, g_v, b_ref[...])
        err = x2 - t_ref[...]
        dx2 = err * (1.0 / D_MODEL)
        dpre = _layer_norm_bwd(dx2, xhat, rstd, g_v)
        dpre_ref[...] = dpre
        dpreb_ref[...] = dpre.astype(BF16)

        @pl.when(i == 0)
        def _():
            sums_ref[...] = jnp.zeros_like(sums_ref)
            loss_ref[...] = jnp.zeros_like(loss_ref)

        sums_ref[0:1, :] += jnp.sum(dx2 * xhat, axis=0, keepdims=True)
        sums_ref[1:2, :] += jnp.sum(dx2, axis=0, keepdims=True)
        loss_ref[...] += jnp.sum(jnp.sum(err * err, axis=1, keepdims=True), axis=0, keepdims=True) * (0.5 / D_MODEL)

    row = lambda: pl.BlockSpec((tm, D_MODEL), lambda i: (i, 0))
    wide = lambda: pl.BlockSpec((tm, D_FF), lambda i: (i, 0))
    vec = lambda: pl.BlockSpec((1, D_MODEL), lambda i: (0, 0))
    return pl.pallas_call(
        body, name="mlp_fwd_loss", grid=(s // tm,),
        in_specs=[row(), _resident_weight(), _resident_weight(), row(), vec(), vec()],
        out_specs=[row(), pl.BlockSpec((SUBLANES, D_MODEL), lambda i: (0, 0)),
                   pl.BlockSpec((SUBLANES, LANES), lambda i: (0, 0)), wide(), wide(), row()],
        out_shape=[SDS((s, D_MODEL), F32), SDS((SUBLANES, D_MODEL), F32), SDS((SUBLANES, LANES), F32),
                   SDS((s, D_FF), BF16), SDS((s, D_FF), BF16), SDS((s, D_MODEL), BF16)],
        compiler_params=_params(("arbitrary",)))(x1, w_up, w_down, target, g, b)


def _resident_weight():
    return pl.BlockSpec((N_CHIPS, D_MODEL, FF_SHARD), lambda i: (0, 0, 0), pipeline_mode=pl.Buffered(1))


def _mlp_bwd_ln1(relu_up, dpre2, w_up, w_down, xhat1, rstd1, g1):
    s = dpre2.shape[0]
    tm = _row_tile(s, 256)

    def body(r_ref, d2_ref, wu_ref, wd_ref, xh_ref, rs_ref, g_ref, dup_ref, dpre_ref, sums_ref):
        i = pl.program_id(0)
        d2 = d2_ref[...]
        d2b = d2.astype(BF16)
        dx1 = ALPHA * d2
        for k in range(N_CHIPS):
            r = r_ref[:, FF_SHARD * k:FF_SHARD * (k + 1)].astype(F32)
            dhid = lax.dot_general(d2b, wd_ref[k], NT_DIMS, preferred_element_type=F32)
            dupb = (dhid * (2.0 * r)).astype(BF16)
            dup_ref[:, FF_SHARD * k:FF_SHARD * (k + 1)] = dupb
            dx1 = dx1 + lax.dot_general(dupb, wu_ref[k], NT_DIMS, preferred_element_type=F32)
        xhat = xh_ref[...]
        dpre_ref[...] = _layer_norm_bwd(dx1, xhat, rs_ref[...], g_ref[...])

        @pl.when(i == 0)
        def _():
            sums_ref[...] = jnp.zeros_like(sums_ref)

        sums_ref[0:1, :] += jnp.sum(dx1 * xhat, axis=0, keepdims=True)
        sums_ref[1:2, :] += jnp.sum(dx1, axis=0, keepdims=True)

    row = lambda w: pl.BlockSpec((tm, w), lambda i: (i, 0))
    return pl.pallas_call(
        body, name="mlp_bwd_ln1", grid=(s // tm,),
        in_specs=[row(D_FF), row(D_MODEL), _resident_weight(), _resident_weight(), row(D_MODEL), row(1),
                  pl.BlockSpec((1, D_MODEL), lambda i: (0, 0))],
        out_specs=[row(D_FF), row(D_MODEL), pl.BlockSpec((SUBLANES, D_MODEL), lambda i: (0, 0))],
        out_shape=[SDS((s, D_FF), BF16), SDS((s, D_MODEL), F32), SDS((SUBLANES, D_MODEL), F32)],
        compiler_params=_params(("arbitrary",)))(relu_up, dpre2, w_up, w_down, xhat1, rstd1, g1)


def _grad_tn(a, b, name, out_cols, stacked):
    s, ka = a.shape
    n = b.shape[1]
    ts = _row_tile(s, 2048)
    n_steps = s // ts
    if stacked:
        tka, tn = ka, out_cols
        grid = (1, n // tn, n_steps)
        shape = (n // tn, ka, tn)
        out_spec = lambda: pl.BlockSpec((None, tka, tn), lambda r, c, t: (c, 0, 0))
    else:
        tka, tn = min(ka, 1024), n
        grid = (ka // tka, 1, n_steps)
        shape = (ka, n)
        out_spec = lambda: pl.BlockSpec((tka, tn), lambda r, c, t: (r, 0))

    def body(a_ref, b_ref, o_ref, ob_ref):
        t = pl.program_id(2)

        @pl.when(t == 0)
        def _():
            o_ref[...] = jnp.zeros_like(o_ref)

        o_ref[...] += lax.dot_general(a_ref[...].astype(BF16), b_ref[...].astype(BF16), TN_DIMS,
                                      preferred_element_type=F32)

        @pl.when(t == n_steps - 1)
        def _():
            ob_ref[...] = o_ref[...].astype(BF16)

    return pl.pallas_call(
        body, name=name, grid=grid,
        in_specs=[pl.BlockSpec((ts, tka), lambda r, c, t: (t, r)),
                  pl.BlockSpec((ts, tn), lambda r, c, t: (t, c))],
        out_specs=[out_spec(), out_spec()], out_shape=[SDS(shape, F32), SDS(shape, BF16)],
        compiler_params=_params(("parallel", "parallel", "arbitrary")))(a, b)


def _grad_w_out(ycn, yan, dpre1):
    s = dpre1.shape[0]
    ts = _row_tile(s, 2048)
    n_steps = s // ts

    def body(yc_ref, ya_ref, d_ref, o_ref, ob_ref):
        half, t = pl.program_id(0), pl.program_id(1)

        @pl.when(t == 0)
        def _():
            o_ref[...] = jnp.zeros_like(o_ref)

        db = d_ref[...].astype(BF16)

        @pl.when(half == 0)
        def _():
            o_ref[...] += lax.dot_general(yc_ref[...], db, TN_DIMS, preferred_element_type=F32)

        @pl.when(half == 1)
        def _():
            o_ref[...] += lax.dot_general(ya_ref[...], db, TN_DIMS, preferred_element_type=F32)

        @pl.when(t == n_steps - 1)
        def _():
            ob_ref[...] = o_ref[...].astype(BF16)

    out_spec = lambda: pl.BlockSpec((CONV_WIDTH, D_MODEL), lambda r, t: (r, 0))
    return pl.pallas_call(
        body, name="grad_w_out", grid=(2, n_steps),
        in_specs=[pl.BlockSpec((ts, CONV_WIDTH), lambda r, t: (t, 0)),
                  pl.BlockSpec((ts, ATTN_WIDTH), lambda r, t: (t, 0)),
                  pl.BlockSpec((ts, D_MODEL), lambda r, t: (t, 0))],
        out_specs=[out_spec(), out_spec()],
        out_shape=[SDS((D_MODEL, D_MODEL), F32), SDS((D_MODEL, D_MODEL), BF16)],
        compiler_params=_params(("parallel", "arbitrary")))(ycn, yan, dpre1)


def _dmix(dpre1, w_out):
    s = dpre1.shape[0]
    tm = _row_tile(s, 512)

    def body(d_ref, w_ref, dc_ref, da_ref):
        db = d_ref[...].astype(BF16)
        dc_ref[...] = lax.dot_general(db, w_ref[0:CONV_WIDTH, :], NT_DIMS, preferred_element_type=F32)
        da_ref[...] = lax.dot_general(db, w_ref[CONV_WIDTH:, :], NT_DIMS, preferred_element_type=F32)

    return pl.pallas_call(
        body, name="dmix", grid=(s // tm,),
        in_specs=[pl.BlockSpec((tm, D_MODEL), lambda i: (i, 0)),
                  pl.BlockSpec((D_MODEL, D_MODEL), lambda i: (0, 0))],
        out_specs=[pl.BlockSpec((tm, CONV_WIDTH), lambda i: (i, 0)),
                   pl.BlockSpec((tm, ATTN_WIDTH), lambda i: (i, 0))],
        out_shape=[SDS((s, CONV_WIDTH), F32), SDS((s, ATTN_WIDTH), F32)],
        compiler_params=_params(("parallel",)))(dpre1, w_out)


def _sum_with_peers(own_ref, r_ref, o_ref):
    acc = own_ref[...]
    for f in range(r_ref.shape[0]):
        acc = acc + r_ref[f].astype(F32)
    o_ref[...] = acc


def _grad_x(kc_idx, dproj, w_in, dpre1, chip_sums, earlier):
    s = dproj.shape[0]
    tm = _row_tile(s, 512)
    steps = s // tm
    n_peers = len(DEVICE_FLIPS)
    n_chips = len(CHIP_FLIPS)
    n_e = len(earlier)

    def body(kc_ref, dp_ref, w_ref, d1_ref, *rest):
        sum_ins, g_in = rest[:2 * n_e], rest[2 * n_e]
        o_ref, g_out = rest[2 * n_e + 1], rest[2 * n_e + 2]
        sum_outs, sems = rest[2 * n_e + 3:3 * n_e + 3], rest[3 * n_e + 3:]
        copies = functools.partial(_chip_reduce_copies, g_in, g_out, sems)
        i = pl.program_id(0)
        pl.when(i == 0)(functools.partial(_start_copies, copies))
        acc = ALPHA * d1_ref[...]
        for k in range(N_CHIPS):
            acc = acc + lax.dot_general(dp_ref[:, IN_SHARD * k:IN_SHARD * (k + 1)], w_ref[k], NT_DIMS,
                                        preferred_element_type=F32)
        o_ref[...] = acc
        for a in range(n_e):
            _sum_with_peers(sum_ins[2 * a], sum_ins[2 * a + 1], sum_outs[a])
        pl.when(i == steps - 1)(functools.partial(_finish_copies, copies))

    in_specs = [pl.BlockSpec((tm, IN_COLS), lambda i, kc: (i, 0)),
                pl.BlockSpec((N_CHIPS, D_MODEL, IN_SHARD), lambda i, kc: (0, 0, 0)),
                pl.BlockSpec((tm, D_MODEL), lambda i, kc: (i, 0))]
    out_specs = [pl.BlockSpec((tm, D_MODEL), lambda i, kc: (i, 0)), ANY]
    out_shape = [SDS((s, D_MODEL), F32), SDS((n_chips,) + chip_sums.shape[1:], chip_sums.dtype)]
    operands = []
    for own, recv in earlier:
        _, _, h, cols = own.shape
        th = h // steps
        in_specs.append(pl.BlockSpec((None, None, th, cols), lambda i, kc: (kc[0], kc[1], i, 0)))
        in_specs.append(pl.BlockSpec((n_peers, th, cols), lambda i, kc: (0, i, 0)))
        out_specs.append(pl.BlockSpec((th, cols), lambda i, kc: (kc[1] * steps + i, 0)))
        out_shape.append(SDS((2 * h, cols), F32))
        operands += [own, recv]
    grid_spec = pltpu.PrefetchScalarGridSpec(
        num_scalar_prefetch=1, grid=(steps,), in_specs=in_specs + [ANY], out_specs=out_specs,
        scratch_shapes=[pltpu.SemaphoreType.DMA((n_chips,)), pltpu.SemaphoreType.DMA((n_chips,))])
    return pl.pallas_call(
        body, name="grad_x", grid_spec=grid_spec, out_shape=out_shape,
        compiler_params=_params(("arbitrary",)))(kc_idx, dproj, w_in, dpre1, *operands, chip_sums)


def _adamw_step(w, g, m, v):
    nm = ADAM_B1 * m + (1.0 - ADAM_B1) * g
    nv = ADAM_B2 * v + (1.0 - ADAM_B2) * (g * g)
    m_hat = nm / (1.0 - ADAM_B1 ** ADAM_STEP)
    v_hat = nv / (1.0 - ADAM_B2 ** ADAM_STEP)
    return -ADAM_LR * (m_hat / (jnp.sqrt(v_hat) + ADAM_EPS) + ADAM_WD * w), nm, nv


def _adamw(w, g, m, v, name):
    r, c = w.shape
    tr = _row_tile(r, 256)

    def body(w_ref, g_ref, m_ref, v_ref, go_ref, d_ref, nm_ref, nv_ref):
        g_v = g_ref[...]
        go_ref[...] = g_v
        d_ref[...], nm_ref[...], nv_ref[...] = _adamw_step(w_ref[...], g_v, m_ref[...], v_ref[...])

    spec = lambda: pl.BlockSpec((tr, c), lambda i: (i, 0))
    return pl.pallas_call(
        body, name=name, grid=(r // tr,),
        in_specs=[spec(), spec(), spec(), spec()], out_specs=[spec(), spec(), spec(), spec()],
        out_shape=[SDS((r, c), F32)] * 4, compiler_params=_params(("parallel",)))(w, g, m, v)


def _adamw_small(total, conv_grad, weights, moments, variances):
    n = len(weights)
    starts = (ROW_GCONV, ROW_GATTN, ROW_LN1G, ROW_LN1B, ROW_LN2G, ROW_LN2B)

    def body(total_ref, cg_ref, *refs):
        w, m, v, outs = refs[:n], refs[n:2 * n], refs[2 * n:3 * n], refs[3 * n:]
        for p in range(n):
            rows = w[p].shape[0]
            g = cg_ref[...] if p == n - 1 else total_ref[starts[p]:starts[p] + rows, :]
            outs[4 * p][...] = g
            outs[4 * p + 1][...], outs[4 * p + 2][...], outs[4 * p + 3][...] = _adamw_step(
                w[p][...], g, m[p][...], v[p][...])

    vmem = pl.BlockSpec(memory_space=pltpu.VMEM)
    out_shape = [SDS(w.shape, F32) for w in weights for _ in range(4)]
    flat = pl.pallas_call(
        body, name="adamw_small", in_specs=[vmem] * (2 + 3 * n), out_specs=[vmem] * (4 * n),
        out_shape=out_shape)(total, conv_grad, *weights, *moments, *variances)
    return [flat[4 * p:4 * p + 4] for p in range(n)]


def _sum_partials(kc_idx, own, recv, name):
    h, cols = own.shape
    th = _row_tile(h, 128)
    n_peers = recv.shape[0]

    def body(kc_ref, own_ref, r_ref, o_ref):
        _sum_with_peers(own_ref, r_ref, o_ref)

    grid_spec = pltpu.PrefetchScalarGridSpec(
        num_scalar_prefetch=1, grid=(h // th,),
        in_specs=[pl.BlockSpec((th, cols), lambda t, kc: (t, 0)),
                  pl.BlockSpec((n_peers, th, cols), lambda t, kc: (0, t, 0))],
        out_specs=pl.BlockSpec((th, cols), lambda t, kc: (kc[1] * (h // th) + t, 0)))
    return pl.pallas_call(
        body, name=name, grid_spec=grid_spec, out_shape=SDS((2 * h, cols), F32),
        compiler_params=_params(("parallel",)))(kc_idx, own, recv)


def _add_sibling(kc_idx, grad, recv):
    _, _, h, cols = grad.shape
    th = _row_tile(h, 512)

    def body(kc_ref, g_ref, r_ref, sums_ref, own_ref):
        total = g_ref[...] + r_ref[...].astype(F32)
        sums_ref[...] = total.astype(BF16)

        @pl.when(pl.program_id(1) == kc_ref[0])
        def _():
            own_ref[...] = total

    grid_spec = pltpu.PrefetchScalarGridSpec(
        num_scalar_prefetch=1, grid=(h // th, N_CHIPS),
        in_specs=[pl.BlockSpec((None, None, th, cols), lambda t, k, kc: (k, kc[1], t, 0)),
                  pl.BlockSpec((None, th, cols), lambda t, k, kc: (k, t, 0))],
        out_specs=[pl.BlockSpec((None, th, cols), lambda t, k, kc: (k, t, 0)),
                   pl.BlockSpec((th, cols), lambda t, k, kc: (t, 0))])
    return pl.pallas_call(
        body, name="add_sibling_w_in", grid_spec=grid_spec,
        out_shape=[SDS((N_CHIPS, h, cols), BF16), SDS((h, cols), F32)],
        compiler_params=_params(("parallel", "arbitrary")))(kc_idx, grad, recv)


def _gather_weights(kc_idx, w_in_slots, conv_slots, later):
    n_l = len(later)
    steps = SUBLANES

    def body(kc_ref, *refs):
        cast_ins, cast_outs = refs[:n_l], refs[n_l + 2:2 * n_l + 2]
        w_buf, conv_buf = refs[2 * n_l + 2], refs[2 * n_l + 3]
        sems = refs[2 * n_l + 4:]
        i = pl.program_id(0)

        def first_hop():
            x, y, c = _position()
            sends, arrivals = _gather_chip_hop([w_buf], sems)
            mine = conv_buf.at[2 * x + y]
            for j, (fx, fy) in enumerate(CHIP_FLIPS):
                tx, ty = _flip(x, fx), _flip(y, fy)
                there = conv_buf.at[2 * tx + ty]
                sends.append(_remote_copy(mine, mine, sems, 6 + j, (tx, ty, c)))
                arrivals.append(_remote_copy(there, there, sems, 6 + j, (tx, ty, c)))
            return sends, arrivals

        pl.when(i == 0)(functools.partial(_start_copies, first_hop))
        for src, dst in zip(cast_ins, cast_outs):
            dst[...] = src[...].astype(BF16)

        @pl.when(i == steps - 1)
        def _():
            sends, arrivals = first_hop()
            for cp in arrivals:
                cp.wait_recv()
            _start_copies(functools.partial(_gather_sibling_hop, [w_buf], sems))
            _finish_copies(functools.partial(_gather_sibling_hop, [w_buf], sems))
            for cp in sends:
                cp.wait_send()

    in_specs, out_specs, out_shape = [], [], []
    for w in later:
        r, c = w.shape
        in_specs.append(pl.BlockSpec((r // steps, c), lambda i, kc: (i, 0)))
        out_specs.append(pl.BlockSpec((None, r // steps, c), lambda i, kc: (kc[0], i, 0)))
        out_shape.append(SDS((N_CHIPS, r, c), BF16))
    grid_spec = pltpu.PrefetchScalarGridSpec(
        num_scalar_prefetch=1, grid=(steps,), in_specs=in_specs + [ANY, ANY], out_specs=out_specs + [ANY, ANY],
        scratch_shapes=[pltpu.SemaphoreType.DMA((9,)), pltpu.SemaphoreType.DMA((9,))])
    return pl.pallas_call(
        body, name="gather_weights", grid_spec=grid_spec,
        out_shape=out_shape + [SDS(w_in_slots.shape, w_in_slots.dtype), SDS(conv_slots.shape, conv_slots.dtype)],
        input_output_aliases={n_l + 1: n_l, n_l + 2: n_l + 1},
        compiler_params=_params(("arbitrary",)))(kc_idx, *later, w_in_slots, conv_slots)


def _exchange_with_sibling(partial):
    h = partial.shape[1] // 2

    def body(g_in, g_out, send_sems, recv_sems):
        x, y, c = _position()
        theirs = pl.ds(pl.multiple_of((1 - c) * h, h), h)
        copies = [_remote_copy(g_in.at[k, theirs], g_out.at[k], (send_sems, recv_sems), k, (x, y, 1 - c))
                  for k in range(N_CHIPS)]
        for cp in copies:
            cp.start()
        for cp in copies:
            cp.wait_recv()
        for cp in copies:
            cp.wait_send()

    return pl.pallas_call(
        body, name="exchange_with_sibling", in_specs=[ANY], out_specs=ANY,
        out_shape=SDS((N_CHIPS, h, partial.shape[2]), partial.dtype),
        scratch_shapes=[pltpu.SemaphoreType.DMA((N_CHIPS,)), pltpu.SemaphoreType.DMA((N_CHIPS,))])(partial)


def _finish_exchange(pieces, vec):
    n = len(pieces)
    n_dev = 2 * N_CHIPS

    def body(*refs):
        v_ref = refs[n]
        outs, o_ref = refs[n + 1:2 * n + 1], refs[2 * n + 1]
        buf, send_sems, recv_sems = refs[2 * n + 2:]
        x, y, c = _position()
        sibling = (x, y, 1 - c)
        me = 4 * x + 2 * y + c
        buf[me] = v_ref[...]
        started = []
        for f, (fx, fy, fc) in enumerate(DEVICE_FLIPS):
            cp = pltpu.make_async_remote_copy(
                src_ref=v_ref, dst_ref=buf.at[me], send_sem=send_sems.at[n + f], recv_sem=recv_sems.at[n + f],
                device_id=(_flip(x, fx), _flip(y, fy), _flip(c, fc)), device_id_type=MESH)
            cp.start()
            started.append(cp)
        for a in range(n):
            h = pieces[a].shape[0] // 2
            mine = outs[a].at[pl.ds(pl.multiple_of(c * h, h), h)]
            cp = pltpu.make_async_remote_copy(
                src_ref=mine, dst_ref=mine, send_sem=send_sems.at[a], recv_sem=recv_sems.at[a],
                device_id=sibling, device_id_type=MESH)
            cp.start()
            started.append(cp)
        for a in range(n):
            h = pieces[a].shape[0] // 2
            theirs = outs[a].at[pl.ds(pl.multiple_of((1 - c) * h, h), h)]
            pltpu.make_async_remote_copy(
                src_ref=theirs, dst_ref=theirs, send_sem=send_sems.at[a], recv_sem=recv_sems.at[a],
                device_id=sibling, device_id_type=MESH).wait_recv()
        for f, (fx, fy, fc) in enumerate(DEVICE_FLIPS):
            src = 4 * _flip(x, fx) + 2 * _flip(y, fy) + _flip(c, fc)
            pltpu.make_async_remote_copy(
                src_ref=v_ref, dst_ref=buf.at[src], send_sem=send_sems.at[n + f], recv_sem=recv_sems.at[n + f],
                device_id=(x, y, c), device_id_type=MESH).wait_recv()
        for cp in started:
            cp.wait_send()
        acc = buf[0]
        for d in range(1, n_dev):
            acc = acc + buf[d]
        o_ref[...] = acc

    vmem = pl.BlockSpec(memory_space=pltpu.VMEM)
    out_shape = [SDS(p.shape, p.dtype) for p in pieces] + [SDS(vec.shape, vec.dtype)]
    n_sems = n + n_dev - 1
    return pl.pallas_call(
        body, name="finish_exchange", in_specs=[ANY] * n + [vmem], out_specs=[ANY] * n + [vmem],
        out_shape=out_shape, input_output_aliases={a: a for a in range(n)},
        scratch_shapes=[pltpu.VMEM((n_dev,) + vec.shape, vec.dtype), pltpu.SemaphoreType.DMA((n_sems,)),
                        pltpu.SemaphoreType.DMA((n_sems,))])(*pieces, vec)


def _constants():
    r = jnp.arange(2 * KEY_BLOCK)[:, None] % KEY_BLOCK
    c = jnp.arange(2 * KEY_BLOCK)[None, :]
    later = jnp.where(c < KEY_BLOCK, r > c, True).astype(BF16)
    earlier = jnp.where(c < KEY_BLOCK, r < c, True).astype(BF16)
    upto = jnp.where(c < KEY_BLOCK, r <= c, True).astype(BF16)
    gr = (jnp.arange(2 * LANES)[:, None] % LANES) // GROUP
    gc = jnp.arange(LANES)[None, :] // GROUP
    gmat = (gr == gc).astype(BF16)
    return later, jnp.stack([earlier, upto]), gmat


def _rows(v):
    return v.reshape(-1, LANES)


def kernel(x, w_in, conv_w, g_conv, g_attn, w_out, ln1_g, ln1_b, w_up, w_down, ln2_g, ln2_b, loss_target, m_w_in, m_conv_w, m_g_conv, m_g_attn, m_w_out, m_ln1_g, m_ln1_b, m_w_up, m_w_down, m_ln2_g, m_ln2_b, v_w_in, v_conv_w, v_g_conv, v_g_attn, v_w_out, v_ln1_g, v_ln1_b, v_w_up, v_w_down, v_ln2_g, v_ln2_b):
    xs, target = x[0], loss_target[0]
    mesh_x, mesh_y, mesh_c = _position()
    k_idx = 2 * mesh_x + mesh_y
    kc_idx = jnp.stack([k_idx, mesh_c]).astype(jnp.int32)
    tri_later, tri_earlier, gmat = _constants()

    w_in_b = _cast_into_slot(kc_idx, w_in[0], "cast_w_in")
    conv_slot = jnp.pad(conv_w, ((0, 0), (0, SUBLANES - conv_w.shape[1]), (0, 0)))
    conv_b = lax.dynamic_update_slice(jnp.zeros((N_CHIPS, SUBLANES, LANES), F32), conv_slot, (k_idx, 0, 0))
    w_out_b, w_up_b, w_down_b, w_in_f, conv_f = _gather_weights(
        kc_idx, w_in_b, conv_b, [w_out[0], w_up[0], w_down[0]])
    taps = jnp.transpose(conv_f, (1, 0, 2)).reshape(SUBLANES, CONV_WIDTH)

    gates, qkv, xs_b = _proj(xs, w_in_f)
    ycn = _conv_fwd(gates, taps, g_conv, gmat)
    o, yan, tot, cut, w_out_f, w_up_f, w_down_f = _attn_fwd(
        qkv, g_attn, tri_later, gmat, [w_out_b, w_up_b, w_down_b])
    w_out_f = w_out_f.reshape(D_MODEL, D_MODEL)
    x1, xhat1, rstd1, x1_b = _mix_ln1(ycn, yan, w_out_f, xs, ln1_g, ln1_b)
    dpre2, ln2_sums, loss_sum, relu_up, hid, dpre2_b = _mlp_fwd_loss(x1, w_up_f, w_down_f, target, ln2_g, ln2_b)

    dup, dpre1, ln1_sums = _mlp_bwd_ln1(relu_up, dpre2, w_up_f, w_down_f, xhat1, rstd1, ln1_g)
    gw_up = _grad_tn(x1_b, dup, "grad_w_up", FF_SHARD, True)
    gw_down = [g.reshape(N_CHIPS, FF_SHARD, D_MODEL) for g in _grad_tn(hid, dpre2_b, "grad_w_down", D_MODEL, False)]
    gw_out = [g.reshape(N_CHIPS, D_MODEL // N_CHIPS, D_MODEL) for g in _grad_w_out(ycn, yan, dpre1)]
    dycn, dyan = _dmix(dpre1, w_out_f)
    dq, dk, dv, gattn_sums, recv_out, recv_up, recv_down = _attn_bwd(
        qkv, o, tot, dyan, g_attn, tri_earlier, gmat, cut, [gw_out[1], gw_up[1], gw_down[1]])
    dbg, dy, conv_sums = _conv_bwd_gate(gates, dycn, taps, g_conv, gmat)
    dproj = _dproj_assemble(gates, dy, dbg, dq, dk, dv, taps)
    gw_in = _grad_tn(xs_b, dproj, "grad_w_in", IN_SHARD, True)
    halves = lambda g: g.reshape(N_CHIPS, 2, g.shape[1] // 2, g.shape[2])
    chip_sums, own_sum = _add_sibling(kc_idx, halves(gw_in[0]), _exchange_with_sibling(gw_in[1]))
    grad_x, recv_in, p_out, p_up, p_down = _grad_x(
        kc_idx, dproj, w_in_f, dpre1, chip_sums,
        [(halves(gw_out[0]), recv_out), (halves(gw_up[0]), recv_up), (halves(gw_down[0]), recv_down)])
    pieces = [_sum_partials(kc_idx, own_sum, recv_in, "sum_partials_w_in"), p_out, p_up, p_down]
    conv_rows = jnp.transpose(conv_sums[0:3].reshape(3, N_CHIPS, LANES), (1, 0, 2)).reshape(3 * N_CHIPS, LANES)
    small = jnp.concatenate([
        loss_sum, _rows(conv_sums[3]), _rows(gattn_sums[0]), _rows(ln1_sums[0]), _rows(ln1_sums[1]),
        _rows(ln2_sums[0]), _rows(ln2_sums[1]), conv_rows,
        jnp.zeros((SMALL_ROWS - ROW_CONVW - 3 * N_CHIPS, LANES), F32)], axis=0)
    g_w_in, g_w_out, g_w_up, g_w_down, total = _finish_exchange(pieces, small)
    loss = total[ROW_LOSS, 0]
    g_conv_w = lax.dynamic_slice(total, (ROW_CONVW + 3 * k_idx, 0), (3, LANES))

    small_names = ["g_conv", "g_attn", "ln1_g", "ln1_b", "ln2_g", "ln2_b", "conv_w"]
    small_w = [g_conv, g_attn, ln1_g, ln1_b, ln2_g, ln2_b, conv_w]
    small_m = [m_g_conv, m_g_attn, m_ln1_g, m_ln1_b, m_ln2_g, m_ln2_b, m_conv_w]
    small_v = [v_g_conv, v_g_attn, v_ln1_g, v_ln1_b, v_ln2_g, v_ln2_b, v_conv_w]
    small_out = dict(zip(small_names, _adamw_small(
        total, g_conv_w, [_rows(a) for a in small_w], [_rows(a) for a in small_m], [_rows(a) for a in small_v])))
    small_shape = dict(zip(small_names, (a.shape for a in small_w)))
    big_out = {
        "w_in": _adamw(w_in[0], g_w_in, m_w_in[0], v_w_in[0], "adamw_w_in"),
        "w_out": _adamw(w_out[0], g_w_out, m_w_out[0], v_w_out[0], "adamw_w_out"),
        "w_up": _adamw(w_up[0], g_w_up, m_w_up[0], v_w_up[0], "adamw_w_up"),
        "w_down": _adamw(w_down[0], g_w_down, m_w_down[0], v_w_down[0], "adamw_w_down"),
    }
    order = ["w_in", "conv_w", "g_conv", "g_attn", "w_out", "ln1_g", "ln1_b", "w_up", "w_down", "ln2_g", "ln2_b"]

    def leaf(kind, name):
        if name in big_out:
            return big_out[name][kind][None]
        return small_out[name][kind].reshape(small_shape[name])

    outs = [loss, grad_x[None]]
    for kind in range(4):
        outs.extend(leaf(kind, name) for name in order)
    return tuple(outs)
```

```python
import functools

import jax
import jax.numpy as jnp
from jax import lax
from jax.experimental import pallas as pl
from jax.experimental.pallas import tpu as pltpu

F32 = jnp.float32
BF16 = jnp.bfloat16
SDS = jax.ShapeDtypeStruct

D_MODEL = 1024
CONV_WIDTH = 512
ATTN_WIDTH = 512
GROUP = 64
GATE_COLS = 3 * CONV_WIDTH
QKV_COLS = 3 * ATTN_WIDTH
IN_COLS = GATE_COLS + QKV_COLS
D_FF = 4 * D_MODEL
N_CHIPS = 4
IN_SHARD = IN_COLS // N_CHIPS
FF_SHARD = D_FF // N_CHIPS
ALPHA = float(2.0 ** 0.25)
LN_EPS = 1e-5
RMS_EPS = 1e-6
ATTN_SCALE = GROUP ** -0.5
LOG2_E = 1.4426950408889634
ADAM_LR = 0.001
ADAM_B1 = 0.9
ADAM_B2 = 0.999
ADAM_EPS = 1e-08
ADAM_WD = 0.01
ADAM_STEP = 10

LANES = 128
SUBLANES = 8
KEY_BLOCK = 128
ATTN_Q_TILE = 512
ATTN_KEY_BLOCKS = 2
ATTN_DIAG_GROUPS = 2
ATTN_DEAD_LOG2 = 200.0
VMEM_LIMIT = 56 * 1024 * 1024

MESH = pl.DeviceIdType.MESH
CHIP_FLIPS = ((1, 0), (0, 1), (1, 1))
DEVICE_FLIPS = tuple((fx, fy, fc) for fx in (0, 1) for fy in (0, 1) for fc in (0, 1))[1:]
NT_DIMS = (((1,), (1,)), ((), ()))
TN_DIMS = (((0,), (0,)), ((), ()))

ROW_LOSS = 0
ROW_GCONV = 8
ROW_GATTN = 12
ROW_LN1G = 16
ROW_LN1B = 24
ROW_LN2G = 32
ROW_LN2B = 40
ROW_CONVW = 48
SMALL_ROWS = 64


def _params(sem=None):
    return pltpu.CompilerParams(dimension_semantics=sem, vmem_limit_bytes=VMEM_LIMIT)


def _flip(v, f):
    return 1 - v if f else v


def _position():
    return lax.axis_index("x"), lax.axis_index("y"), lax.axis_index("c")


def _hilo(v):
    hi = v.astype(BF16)
    lo = (v - hi.astype(F32)).astype(BF16)
    return jnp.concatenate([hi, lo], axis=1)


def _hilo_dot(v, mat):
    return jnp.dot(_hilo(v), mat, preferred_element_type=F32)


def _group_sum(v, gmat):
    parts = [_hilo_dot(v[:, LANES * j:LANES * (j + 1)], gmat) for j in range(v.shape[1] // LANES)]
    return parts[0] if len(parts) == 1 else jnp.concatenate(parts, axis=1)


def _softplus_terms(z):
    sp = jnp.log2(1.0 + jnp.exp2(-jnp.abs(z)))
    log_beta = jnp.minimum(z, 0.0) - sp
    return log_beta, log_beta - z


def _layer_norm_fwd(pre, g, b):
    mu = jnp.mean(pre, axis=-1, keepdims=True)
    d = pre - mu
    var = jnp.mean(d * d, axis=-1, keepdims=True)
    rstd = lax.rsqrt(var + LN_EPS)
    xhat = d * rstd
    return xhat * g + b, xhat, rstd


def _layer_norm_bwd(dy, xhat, rstd, g):
    dxh = dy * g
    m1 = jnp.mean(dxh, axis=-1, keepdims=True)
    m2 = jnp.mean(dxh * xhat, axis=-1, keepdims=True)
    return rstd * (dxh - m1 - xhat * m2)


def _row_tile(s, want):
    return min(s, want)


def _cast_into_slot(kc_idx, w, name):
    r, c = w.shape
    tr = _row_tile(r, 256)

    def body(kc_ref, w_ref, o_ref):
        o_ref[...] = w_ref[...].astype(BF16)

    grid_spec = pltpu.PrefetchScalarGridSpec(
        num_scalar_prefetch=1, grid=(r // tr,),
        in_specs=[pl.BlockSpec((tr, c), lambda i, kc: (i, 0))],
        out_specs=pl.BlockSpec((None, tr, c), lambda i, kc: (kc[0], i, 0)))
    return pl.pallas_call(
        body, name=name, grid_spec=grid_spec, out_shape=SDS((N_CHIPS, r, c), BF16),
        compiler_params=_params(("parallel",)))(kc_idx, w)


def _proj(x, w_in):
    s = x.shape[0]
    tm = _row_tile(s, 512)

    def body(x_ref, w_ref, gates_ref, qkv_ref, xb_ref):
        xb = x_ref[...].astype(BF16)
        xb_ref[...] = xb
        for k in range(N_CHIPS):
            acc = jnp.dot(xb, w_ref[k], preferred_element_type=F32)
            if k < 2:
                gates_ref[:, IN_SHARD * k:IN_SHARD * (k + 1)] = acc
            else:
                qkv_ref[:, IN_SHARD * (k - 2):IN_SHARD * (k - 1)] = acc.astype(BF16)

    return pl.pallas_call(
        body, name="proj", grid=(s // tm,),
        in_specs=[pl.BlockSpec((tm, D_MODEL), lambda i: (i, 0)),
                  pl.BlockSpec((N_CHIPS, D_MODEL, IN_SHARD), lambda i: (0, 0, 0))],
        out_specs=[pl.BlockSpec((tm, GATE_COLS), lambda i: (i, 0)),
                   pl.BlockSpec((tm, QKV_COLS), lambda i: (i, 0)),
                   pl.BlockSpec((tm, D_MODEL), lambda i: (i, 0))],
        out_shape=[SDS((s, GATE_COLS), F32), SDS((s, QKV_COLS), BF16), SDS((s, D_MODEL), BF16)],
        compiler_params=_params(("parallel",)))(x, w_in)


def _conv_forward_values(g_ref, halo_ref, taps_ref, first_block):
    gates = g_ref[...]
    tr = gates.shape[0]
    bg = gates[:, :CONV_WIDTH]
    cg = gates[:, CONV_WIDTH:2 * CONV_WIDTH]
    h = gates[:, 2 * CONV_WIDTH:]
    u = cg * h

    def prev(r):
        v = halo_ref[r:r + 1, CONV_WIDTH:2 * CONV_WIDTH] * halo_ref[r:r + 1, 2 * CONV_WIDTH:GATE_COLS]
        return jnp.where(first_block, 0.0, v)

    row = lax.broadcasted_iota(jnp.int32, (tr, CONV_WIDTH), 0)
    u1 = jnp.where(row == 0, prev(7), pltpu.roll(u, 1, 0))
    u2 = jnp.where(row == 0, prev(6), jnp.where(row == 1, prev(7), pltpu.roll(u, 2, 0)))
    y = taps_ref[0:1, :] * u2 + taps_ref[1:2, :] * u1 + taps_ref[2:3, :] * u
    return bg, cg, h, u, u1, u2, y


def _conv_fwd(gates, taps, g_conv, gmat):
    s = gates.shape[0]
    tr = _row_tile(s, 512)
    hb = tr // SUBLANES

    def body(g_ref, halo_ref, taps_ref, gain_ref, gmat_ref, out_ref):
        i = pl.program_id(0)
        bg, _, _, _, _, _, y = _conv_forward_values(g_ref, halo_ref, taps_ref, i == 0)
        yc = bg * y
        ms = _group_sum(yc * yc, gmat_ref[...]) * (1.0 / GROUP)
        out_ref[...] = (yc * lax.rsqrt(ms + RMS_EPS) * gain_ref[...]).astype(BF16)

    return pl.pallas_call(
        body, name="conv_fwd", grid=(s // tr,),
        in_specs=[pl.BlockSpec((tr, GATE_COLS), lambda i: (i, 0)),
                  pl.BlockSpec((SUBLANES, GATE_COLS), lambda i: (jnp.maximum(i * hb - 1, 0), 0)),
                  pl.BlockSpec((SUBLANES, CONV_WIDTH), lambda i: (0, 0)),
                  pl.BlockSpec((1, CONV_WIDTH), lambda i: (0, 0)),
                  pl.BlockSpec((2 * LANES, LANES), lambda i: (0, 0))],
        out_specs=pl.BlockSpec((tr, CONV_WIDTH), lambda i: (i, 0)),
        out_shape=SDS((s, CONV_WIDTH), BF16),
        compiler_params=_params(("parallel",)))(gates, gates, taps, g_conv, gmat)


def _conv_bwd_gate(gates, dycn, taps, g_conv, gmat):
    s = gates.shape[0]
    tr = _row_tile(s, 512)
    hb = tr // SUBLANES

    def body(g_ref, halo_ref, dn_ref, taps_ref, gain_ref, gmat_ref, dbg_ref, dy_ref, sums_ref):
        i = pl.program_id(0)
        bg, _, _, u, u1, u2, y = _conv_forward_values(g_ref, halo_ref, taps_ref, i == 0)
        gmat_v = gmat_ref[...]
        yc = bg * y
        rstd = lax.rsqrt(_group_sum(yc * yc, gmat_v) * (1.0 / GROUP) + RMS_EPS)
        n = yc * rstd
        dout = dn_ref[...]
        dn = dout * gain_ref[...]
        dyc = rstd * (dn - n * (_group_sum(dn * n, gmat_v) * (1.0 / GROUP)))
        dbg_ref[...] = (dyc * y).astype(BF16)
        dy = dyc * bg
        dy_ref[...] = dy

        @pl.when(i == 0)
        def _():
            sums_ref[...] = jnp.zeros_like(sums_ref)

        sums_ref[0:1, :] += jnp.sum(dy * u2, axis=0, keepdims=True)
        sums_ref[1:2, :] += jnp.sum(dy * u1, axis=0, keepdims=True)
        sums_ref[2:3, :] += jnp.sum(dy * u, axis=0, keepdims=True)
        sums_ref[3:4, :] += jnp.sum(dout * n, axis=0, keepdims=True)

    return pl.pallas_call(
        body, name="conv_bwd_gate", grid=(s // tr,),
        in_specs=[pl.BlockSpec((tr, GATE_COLS), lambda i: (i, 0)),
                  pl.BlockSpec((SUBLANES, GATE_COLS), lambda i: (jnp.maximum(i * hb - 1, 0), 0)),
                  pl.BlockSpec((tr, CONV_WIDTH), lambda i: (i, 0)),
                  pl.BlockSpec((SUBLANES, CONV_WIDTH), lambda i: (0, 0)),
                  pl.BlockSpec((1, CONV_WIDTH), lambda i: (0, 0)),
                  pl.BlockSpec((2 * LANES, LANES), lambda i: (0, 0))],
        out_specs=[pl.BlockSpec((tr, CONV_WIDTH), lambda i: (i, 0)),
                   pl.BlockSpec((tr, CONV_WIDTH), lambda i: (i, 0)),
                   pl.BlockSpec((SUBLANES, CONV_WIDTH), lambda i: (0, 0))],
        out_shape=[SDS((s, CONV_WIDTH), BF16), SDS((s, CONV_WIDTH), F32), SDS((SUBLANES, CONV_WIDTH), F32)],
        compiler_params=_params(("arbitrary",)))(gates, gates, dycn, taps, g_conv, gmat)


def _dproj_assemble(gates, dy, dbg, dq, dk, dv, taps):
    s = gates.shape[0]
    tr = _row_tile(s, 512)
    hb = tr // SUBLANES
    last = s // SUBLANES - 1
    n_blocks = s // tr

    def body(g_ref, dy_ref, halo_ref, dbg_ref, dq_ref, dk_ref, dv_ref, taps_ref, out_ref):
        i = pl.program_id(0)
        gates_v = g_ref[...]
        cg = gates_v[:, CONV_WIDTH:2 * CONV_WIDTH]
        h = gates_v[:, 2 * CONV_WIDTH:]
        dy_v = dy_ref[...]
        last_block = i == n_blocks - 1
        nxt = lambda r: jnp.where(last_block, 0.0, halo_ref[r:r + 1, :])
        row = lax.broadcasted_iota(jnp.int32, (tr, CONV_WIDTH), 0)
        d1 = jnp.where(row == tr - 1, nxt(0), pltpu.roll(dy_v, tr - 1, 0))
        d2 = jnp.where(row == tr - 1, nxt(1), jnp.where(row == tr - 2, nxt(0), pltpu.roll(dy_v, tr - 2, 0)))
        du = taps_ref[2:3, :] * dy_v + taps_ref[1:2, :] * d1 + taps_ref[0:1, :] * d2
        out_ref[:, 0:CONV_WIDTH] = dbg_ref[...]
        out_ref[:, CONV_WIDTH:2 * CONV_WIDTH] = (du * h).astype(BF16)
        out_ref[:, 2 * CONV_WIDTH:GATE_COLS] = (du * cg).astype(BF16)
        out_ref[:, GATE_COLS:GATE_COLS + ATTN_WIDTH] = dq_ref[...]
        out_ref[:, GATE_COLS + ATTN_WIDTH:GATE_COLS + 2 * ATTN_WIDTH] = dk_ref[...]
        out_ref[:, GATE_COLS + 2 * ATTN_WIDTH:] = dv_ref[...]

    row_spec = lambda w: pl.BlockSpec((tr, w), lambda i: (i, 0))
    return pl.pallas_call(
        body, name="dproj_assemble", grid=(s // tr,),
        in_specs=[row_spec(GATE_COLS), row_spec(CONV_WIDTH),
                  pl.BlockSpec((SUBLANES, CONV_WIDTH), lambda i: (jnp.minimum((i + 1) * hb, last), 0)),
                  row_spec(CONV_WIDTH), row_spec(ATTN_WIDTH), row_spec(ATTN_WIDTH), row_spec(ATTN_WIDTH),
                  pl.BlockSpec((SUBLANES, CONV_WIDTH), lambda i: (0, 0))],
        out_specs=row_spec(IN_COLS),
        out_shape=SDS((s, IN_COLS), BF16),
        compiler_params=_params(("parallel",)))(gates, dy, dy, dbg, dq, dk, dv, taps)


def _stack_heads(rows, nb):
    lane = lax.broadcasted_iota(jnp.int32, (1, LANES), 1)
    zero = jnp.zeros((KEY_BLOCK, LANES), rows.dtype)
    parts = []
    for blk in range(nb):
        r = rows[blk * KEY_BLOCK:(blk + 1) * KEY_BLOCK]
        parts.append(jnp.where(lane < GROUP, r, zero))
        parts.append(jnp.where(lane < GROUP, zero, r))
    return jnp.concatenate(parts, axis=0)


def _stack_hilo(v, n_cols):
    return jnp.concatenate([_hilo(v[:, c * KEY_BLOCK:(c + 1) * KEY_BLOCK]) for c in range(n_cols)], axis=0)


def _causal_mask(tq, nb, diag_base):
    shape = (tq, 2 * nb * KEY_BLOCK)
    row = lax.broadcasted_iota(jnp.int32, shape, 0)
    col = lax.broadcasted_iota(jnp.int32, shape, 1)
    key = diag_base + (col // (2 * KEY_BLOCK)) * KEY_BLOCK + col % KEY_BLOCK
    return key < row


ANY = pl.BlockSpec(memory_space=pl.ANY)


def _remote_copy(src, dst, sems, idx, target):
    return pltpu.make_async_remote_copy(src_ref=src, dst_ref=dst, send_sem=sems[0].at[idx], recv_sem=sems[1].at[idx],
                                        device_id=target, device_id_type=MESH)


def _gather_chip_hop(bufs, sems):
    x, y, c = _position()
    sends, arrivals = [], []
    for a, buf in enumerate(bufs):
        h = buf.shape[1] // 2
        rows = pl.ds(pl.multiple_of(c * h, h), h)
        mine = buf.at[2 * x + y, rows]
        for j, (fx, fy) in enumerate(CHIP_FLIPS):
            tx, ty = _flip(x, fx), _flip(y, fy)
            there = buf.at[2 * tx + ty, rows]
            sends.append(_remote_copy(mine, mine, sems, 6 * a + j, (tx, ty, c)))
            arrivals.append(_remote_copy(there, there, sems, 6 * a + j, (tx, ty, c)))
    return sends, arrivals


def _gather_sibling_hop(bufs, sems):
    x, y, c = _position()
    sends, arrivals = [], []
    for a, buf in enumerate(bufs):
        h = buf.shape[1] // 2
        mine, theirs = pl.ds(pl.multiple_of(c * h, h), h), pl.ds(pl.multiple_of((1 - c) * h, h), h)
        for j, (fx, fy) in enumerate(CHIP_FLIPS):
            kj = 2 * _flip(x, fx) + _flip(y, fy)
            landed, other = buf.at[kj, mine], buf.at[kj, theirs]
            sends.append(_remote_copy(landed, landed, sems, 6 * a + 3 + j, (x, y, 1 - c)))
            arrivals.append(_remote_copy(other, other, sems, 6 * a + 3 + j, (x, y, 1 - c)))
    return sends, arrivals


def _reduce_copies(ins, outs, sems):
    x, y, c = _position()
    sends, arrivals = [], []
    for a in range(len(ins)):
        h = ins[a].shape[1] // 2
        for f, (fx, fy, fc) in enumerate(DEVICE_FLIPS):
            tx, ty, tc = _flip(x, fx), _flip(y, fy), _flip(c, fc)
            src = ins[a].at[2 * tx + ty, pl.ds(pl.multiple_of(tc * h, h), h)]
            sends.append(_remote_copy(src, outs[a].at[f], sems, 7 * a + f, (tx, ty, tc)))
            arrivals.append(_remote_copy(outs[a].at[f], outs[a].at[f], sems, 7 * a + f, (tx, ty, tc)))
    return sends, arrivals


def _chip_reduce_copies(src, dst, sems):
    x, y, c = _position()
    sends, arrivals = [], []
    for j, (fx, fy) in enumerate(CHIP_FLIPS):
        tx, ty = _flip(x, fx), _flip(y, fy)
        sends.append(_remote_copy(src.at[2 * tx + ty], dst.at[j], sems, j, (tx, ty, c)))
        arrivals.append(_remote_copy(dst.at[j], dst.at[j], sems, j, (tx, ty, c)))
    return sends, arrivals


def _start_copies(make):
    sends, _ = make()
    for cp in sends:
        cp.start()


def _finish_copies(make):
    sends, arrivals = make()
    for cp in arrivals:
        cp.wait_recv()
    for cp in sends:
        cp.wait_send()


def _attn_fwd(qkv, g_attn, tri, gmat, shards):
    n_w = len(shards)
    s = qkv.shape[0]
    tq = _row_tile(s, ATTN_Q_TILE)
    tk = KEY_BLOCK
    nb = ATTN_KEY_BLOCKS
    width = nb * tk
    n_groups = ATTN_DIAG_GROUPS
    group = tq // n_groups
    pairs = ATTN_WIDTH // LANES

    def body(q_ref, k_ref, v_ref, gain_ref, tri_ref, gmat_ref, *rest):
        o_ref, yn_ref, tot_ref, cut_ref = rest[n_w:n_w + 4]
        w_bufs, sems = rest[n_w + 4:2 * n_w + 4], rest[2 * n_w + 4:]
        chip_hop = functools.partial(_gather_chip_hop, w_bufs, sems)
        sibling_hop = functools.partial(_gather_sibling_hop, w_bufs, sems)
        p, i = pl.program_id(0), pl.program_id(1)
        pl.when((p == 0) & (i == 0))(functools.partial(_start_copies, chip_hop))

        @pl.when((p == pairs - 1) & (i == 0))
        def _():
            for cp in chip_hop()[1]:
                cp.wait_recv()
            _start_copies(sibling_hop)

        q2 = q_ref[...]
        tri_v = tri_ref[...]

        def trip(s0, n_blk, rows, carry, diag_base):
            r0, nr = rows
            run = [carry[0], carry[1]]
            oacc = carry[2]
            ksel = _stack_heads(k_ref[pl.ds(s0, n_blk * tk), :], n_blk)
            vsel = _stack_heads(v_ref[pl.ds(s0, n_blk * tk), :], n_blk)
            z = lax.dot_general(q2[r0:r0 + nr], ksel, NT_DIMS, preferred_element_type=F32) * (ATTN_SCALE * LOG2_E)
            log_beta, log_keep = _softplus_terms(z)
            if diag_base is not None:
                valid = _causal_mask(nr, n_blk, diag_base)
                log_keep = jnp.where(valid, log_keep, 0.0)
            ct = jnp.dot(_stack_hilo(log_keep, 2 * n_blk), tri_v, preferred_element_type=F32)
            a_parts = [None] * (2 * n_blk)
            for c in reversed(range(2 * n_blk)):
                h = c % 2
                ct_c = ct[c * nr:(c + 1) * nr]
                a_parts[c] = jnp.exp2(log_beta[:, c * tk:(c + 1) * tk] + ct_c[:, :tk] + run[h])
                run[h] = run[h] + ct_c[:, tk:]
            a = jnp.concatenate(a_parts, axis=1)
            if diag_base is not None:
                a = jnp.where(valid, a, 0.0)
            oacc = oacc + jnp.dot(a.astype(BF16), vsel, preferred_element_type=F32)
            return run[0], run[1], oacc

        groups = []
        for g in range(n_groups):
            zeros = (jnp.zeros((group, tk), F32), jnp.zeros((group, tk), F32), jnp.zeros((group, LANES), F32))
            groups.append(trip(pl.multiple_of(i * tq, tq), (g + 1) * group // tk, (g * group, group), zeros,
                               -g * group))
        carry = tuple(jnp.concatenate([grp[j] for grp in groups], axis=0) for j in range(3))
        n_full = i * (tq // width)

        def alive(run_a, run_b):
            return jnp.max(jnp.maximum(run_a, run_b)) > -ATTN_DEAD_LOG2

        def earlier_trip(c):
            done, _, run_a, run_b, oacc = c
            s0 = pl.multiple_of((n_full - 1 - done) * width, width)
            run_a, run_b, oacc = trip(s0, nb, (0, tq), (run_a, run_b, oacc), None)
            return done + 1, alive(run_a, run_b), run_a, run_b, oacc

        done, _, run_a, run_b, oacc = lax.while_loop(
            lambda c: (c[0] < n_full) & c[1], earlier_trip, (jnp.int32(0), alive(carry[0], carry[1])) + carry)
        cut_ref[p * pl.num_programs(1) + i] = (n_full - done).astype(F32)
        lane = lax.broadcasted_iota(jnp.int32, (1, LANES), 1)
        o_ref[...] = oacc
        tot_ref[...] = jnp.where(lane < GROUP, run_a, run_b)
        ms = _group_sum(oacc * oacc, gmat_ref[...]) * (1.0 / GROUP)
        yn_ref[...] = (oacc * lax.rsqrt(ms + RMS_EPS) * gain_ref[...]).astype(BF16)

        @pl.when((p == pairs - 1) & (i == pl.num_programs(1) - 1))
        def _():
            for cp in chip_hop()[0]:
                cp.wait_send()
            _finish_copies(sibling_hop)

    blk = lambda: pl.BlockSpec((tq, LANES), lambda p, i: (i, p))
    return pl.pallas_call(
        body, name="attn_fwd", grid=(pairs, s // tq),
        in_specs=[pl.BlockSpec((tq, LANES), lambda p, i: (i, p)),
                  pl.BlockSpec((s, LANES), lambda p, i: (0, pairs + p)),
                  pl.BlockSpec((s, LANES), lambda p, i: (0, 2 * pairs + p)),
                  pl.BlockSpec((1, LANES), lambda p, i: (0, p)),
                  pl.BlockSpec((2 * tk, 2 * tk), lambda p, i: (0, 0)),
                  pl.BlockSpec((2 * LANES, LANES), lambda p, i: (0, 0))] + [ANY] * n_w,
        out_specs=[blk(), blk(), blk(), pl.BlockSpec(memory_space=pltpu.SMEM)] + [ANY] * n_w,
        out_shape=[SDS((s, ATTN_WIDTH), F32), SDS((s, ATTN_WIDTH), BF16), SDS((s, ATTN_WIDTH), F32),
                   SDS((pairs * (s // tq),), F32)]
        + [SDS(w.shape, w.dtype) for w in shards],
        input_output_aliases={6 + a: 4 + a for a in range(n_w)},
        scratch_shapes=[pltpu.SemaphoreType.DMA((6 * n_w,)), pltpu.SemaphoreType.DMA((6 * n_w,))],
        compiler_params=_params(("arbitrary", "arbitrary")))(qkv, qkv, qkv, g_attn, tri, gmat, *shards)


def _attn_bwd(qkv, o, tot, dyn, g_attn, tri, gmat, cut, partials):
    n_g = len(partials)
    s = qkv.shape[0]
    tq = _row_tile(s, ATTN_Q_TILE)
    tk = KEY_BLOCK
    nb = ATTN_KEY_BLOCKS
    width = nb * tk
    n_groups = ATTN_DIAG_GROUPS
    group = tq // n_groups
    pairs = ATTN_WIDTH // LANES

    def body(q_ref, k_ref, v_ref, o_ref, tot_ref, dyn_ref, gain_ref, tri_ref, gmat_ref, cut_ref, *rest):
        g_ins, (dq_ref, dk_out, dv_out, dg_ref) = rest[:n_g], rest[n_g:n_g + 4]
        g_outs, sems = rest[n_g + 4:2 * n_g + 4], rest[2 * n_g + 4:2 * n_g + 6]
        dk_ref, dv_ref = rest[2 * n_g + 6:]
        copies = functools.partial(_reduce_copies, g_ins, g_outs, sems)
        p, i = pl.program_id(0), pl.program_id(1)
        pl.when((p == 0) & (i == 0))(functools.partial(_start_copies, copies))

        @pl.when(i == 0)
        def _():
            dk_ref[...] = jnp.zeros_like(dk_ref)
            dv_ref[...] = jnp.zeros_like(dv_ref)
            dg_ref[...] = jnp.zeros_like(dg_ref)

        gmat_v = gmat_ref[...]
        o_v = o_ref[...]
        rstd = lax.rsqrt(_group_sum(o_v * o_v, gmat_v) * (1.0 / GROUP) + RMS_EPS)
        n = o_v * rstd
        dout = dyn_ref[...]
        dg_ref[0:1, :] += jnp.sum(dout * n, axis=0, keepdims=True)
        dn = dout * gain_ref[...]
        do2 = (rstd * (dn - n * (_group_sum(dn * n, gmat_v) * (1.0 / GROUP)))).astype(BF16)
        q2 = q_ref[...]
        tot_v = tot_ref[...]
        tots = (jnp.broadcast_to(tot_v[:, 0:1], (tq, tk)), jnp.broadcast_to(tot_v[:, GROUP:GROUP + 1], (tq, tk)))
        tri_v, tri_incl_v = tri_ref[0], tri_ref[1]
        lane = lax.broadcasted_iota(jnp.int32, (1, LANES), 1)

        def trip(s0, n_blk, rows, carry, diag_base):
            r0, nr = rows
            rest_l = [carry[0], carry[1]]
            pref_g = [carry[2], carry[3]]
            dq = carry[4]
            q_rows, do_rows = q2[r0:r0 + nr], do2[r0:r0 + nr]
            ksel = _stack_heads(k_ref[pl.ds(s0, n_blk * tk), :], n_blk)
            vsel = _stack_heads(v_ref[pl.ds(s0, n_blk * tk), :], n_blk)
            z = lax.dot_general(q_rows, ksel, NT_DIMS, preferred_element_type=F32) * (ATTN_SCALE * LOG2_E)
            log_beta, log_keep = _softplus_terms(z)
            if diag_base is not None:
                valid = _causal_mask(nr, n_blk, diag_base)
                log_keep = jnp.where(valid, log_keep, 0.0)
            ctl = jnp.dot(_stack_hilo(log_keep, 2 * n_blk), tri_incl_v, preferred_element_type=F32)
            da = lax.dot_general(do_rows, vsel, NT_DIMS, preferred_element_type=F32)
            a_parts = []
            for c in range(2 * n_blk):
                h = c % 2
                ct_c = ctl[c * nr:(c + 1) * nr]
                cols = slice(c * tk, (c + 1) * tk)
                a_parts.append(jnp.exp2(log_beta[:, cols] + (rest_l[h] - ct_c[:, :tk])))
                rest_l[h] = rest_l[h] - ct_c[:, tk:]
            a = jnp.concatenate(a_parts, axis=1)
            if diag_base is not None:
                a = jnp.where(valid, a, 0.0)
            g = a * da
            ctg = jnp.dot(_stack_hilo(g, 2 * n_blk), tri_v, preferred_element_type=F32)
            dz_parts = []
            for c in range(2 * n_blk):
                h = c % 2
                ct_c = ctg[c * nr:(c + 1) * nr]
                cols = slice(c * tk, (c + 1) * tk)
                prefix = pref_g[h] + ct_c[:, :tk]
                pref_g[h] = pref_g[h] + ct_c[:, tk:]
                g_c = g[:, cols]
                dz_parts.append(g_c - jnp.exp2(log_beta[:, cols]) * (g_c + prefix))
            dz = jnp.concatenate(dz_parts, axis=1) * ATTN_SCALE
            if diag_base is not None:
                dz = jnp.where(valid, dz, 0.0)
            dzb = dz.astype(BF16)
            dq = dq + jnp.dot(dzb, ksel, preferred_element_type=F32)
            dkt = lax.dot_general(dzb, q_rows, TN_DIMS, preferred_element_type=F32)
            dvt = lax.dot_general(a.astype(BF16), do_rows, TN_DIMS, preferred_element_type=F32)
            for blk in range(n_blk):
                ra, rb = slice(2 * blk * tk, (2 * blk + 1) * tk), slice((2 * blk + 1) * tk, (2 * blk + 2) * tk)
                keys = pl.ds(pl.multiple_of(s0 + blk * tk, tk), tk)
                dk_ref[keys, :] += jnp.where(lane < GROUP, dkt[ra], dkt[rb])
                dv_ref[keys, :] += jnp.where(lane < GROUP, dvt[ra], dvt[rb])
            return rest_l[0], rest_l[1], pref_g[0], pref_g[1], dq

        zeros_qk = jnp.zeros((tq, tk), F32)
        carry = (tots[0], tots[1], zeros_qk, zeros_qk, jnp.zeros((tq, LANES), F32))
        n_full = i * (tq // width)
        first = jnp.clip(cut_ref[p * pl.num_programs(1) + i].astype(jnp.int32), 0, n_full)
        carry = lax.fori_loop(
            first, n_full, lambda t, c: trip(pl.multiple_of(t * width, width), nb, (0, tq), c, None), carry)
        dq_groups = []
        for g in range(n_groups):
            sub = tuple(x[g * group:(g + 1) * group] for x in carry)
            dq_groups.append(trip(pl.multiple_of(i * tq, tq), (g + 1) * group // tk, (g * group, group), sub,
                                  -g * group)[4])
        dq_ref[...] = jnp.concatenate(dq_groups, axis=0).astype(BF16)

        @pl.when(i == pl.num_programs(1) - 1)
        def _():
            dk_out[...] = dk_ref[...].astype(BF16)
            dv_out[...] = dv_ref[...].astype(BF16)

        pl.when((p == pairs - 1) & (i == pl.num_programs(1) - 1))(functools.partial(_finish_copies, copies))

    blk = lambda: pl.BlockSpec((tq, LANES), lambda p, i: (i, p))
    col = lambda: pl.BlockSpec((s, LANES), lambda p, i: (0, p))
    n_peers = len(DEVICE_FLIPS)
    return pl.pallas_call(
        body, name="attn_bwd", grid=(pairs, s // tq),
        in_specs=[pl.BlockSpec((tq, LANES), lambda p, i: (i, p)),
                  pl.BlockSpec((s, LANES), lambda p, i: (0, pairs + p)),
                  pl.BlockSpec((s, LANES), lambda p, i: (0, 2 * pairs + p)),
                  blk(), blk(), blk(),
                  pl.BlockSpec((1, LANES), lambda p, i: (0, p)),
                  pl.BlockSpec((2, 2 * tk, 2 * tk), lambda p, i: (0, 0, 0)),
                  pl.BlockSpec((2 * LANES, LANES), lambda p, i: (0, 0)),
                  pl.BlockSpec(memory_space=pltpu.SMEM)] + [ANY] * n_g,
        out_specs=[blk(), col(), col(), pl.BlockSpec((SUBLANES, LANES), lambda p, i: (0, p))] + [ANY] * n_g,
        out_shape=[SDS((s, ATTN_WIDTH), BF16), SDS((s, ATTN_WIDTH), BF16), SDS((s, ATTN_WIDTH), BF16),
                   SDS((SUBLANES, ATTN_WIDTH), F32)]
        + [SDS((n_peers, g.shape[1] // 2, g.shape[2]), g.dtype) for g in partials],
        scratch_shapes=[pltpu.SemaphoreType.DMA((n_peers * n_g,)), pltpu.SemaphoreType.DMA((n_peers * n_g,)),
                        pltpu.VMEM((s, LANES), F32), pltpu.VMEM((s, LANES), F32)],
        compiler_params=_params(("arbitrary", "arbitrary")))(
            qkv, qkv, qkv, o, tot, dyn, g_attn, tri, gmat, cut, *partials)


def _mix_ln1(ycn, yan, w_out, x, g, b):
    s = x.shape[0]
    tm = _row_tile(s, 512)

    def body(yc_ref, ya_ref, w_ref, x_ref, g_ref, b_ref, x1_ref, xhat_ref, rstd_ref, x1b_ref):
        mix = jnp.dot(yc_ref[...], w_ref[0:CONV_WIDTH, :], preferred_element_type=F32)
        mix = mix + jnp.dot(ya_ref[...], w_ref[CONV_WIDTH:, :], preferred_element_type=F32)
        x1, xhat, rstd = _layer_norm_fwd(ALPHA * x_ref[...] + mix, g_ref[...], b_ref[...])
        x1_ref[...] = x1
        xhat_ref[...] = xhat
        rstd_ref[...] = rstd
        x1b_ref[...] = x1.astype(BF16)

    row = lambda w: pl.BlockSpec((tm, w), lambda i: (i, 0))
    vec = lambda: pl.BlockSpec((1, D_MODEL), lambda i: (0, 0))
    return pl.pallas_call(
        body, name="mix_ln1", grid=(s // tm,),
        in_specs=[row(CONV_WIDTH), row(ATTN_WIDTH), pl.BlockSpec((D_MODEL, D_MODEL), lambda i: (0, 0)),
                  row(D_MODEL), vec(), vec()],
        out_specs=[row(D_MODEL), row(D_MODEL), row(1), row(D_MODEL)],
        out_shape=[SDS((s, D_MODEL), F32), SDS((s, D_MODEL), F32), SDS((s, 1), F32), SDS((s, D_MODEL), BF16)],
        compiler_params=_params(("parallel",)))(ycn, yan, w_out, x, g, b)


def _mlp_fwd_loss(x1, w_up, w_down, target, g, b):
    s = x1.shape[0]
    tm = _row_tile(s, 256)

    def body(x1_ref, wu_ref, wd_ref, t_ref, g_ref, b_ref, dpre_ref, sums_ref, loss_ref, r_ref, hid_ref, dpreb_ref):
        i = pl.program_id(0)
        x1_v = x1_ref[...]
        xb = x1_v.astype(BF16)
        ffn = jnp.zeros((tm, D_MODEL), F32)
        for k in range(N_CHIPS):
            r = jnp.maximum(jnp.dot(xb, wu_ref[k], preferred_element_type=F32), 0.0)
            hid = (r * r).astype(BF16)
            r_ref[:, FF_SHARD * k:FF_SHARD * (k + 1)] = r.astype(BF16)
            hid_ref[:, FF_SHARD * k:FF_SHARD * (k + 1)] = hid
            ffn = ffn + jnp.dot(hid, wd_ref[k], preferred_element_type=F32)
        g_v = g_ref[...]
        x2, xhat, rstd = _layer_norm_fwd(ALPHA * x1_v + ffn, g_v, b_ref[...])
        err = x2 - t_ref[...]
        dx2 = err * (1.0 / D_MODEL)
        dpre = _layer_norm_bwd(dx2, xhat, rstd, g_v)
        dpre_ref[...] = dpre
        dpreb_ref[...] = dpre.astype(BF16)

        @pl.when(i == 0)
        def _():
            sums_ref[...] = jnp.zeros_like(sums_ref)
            loss_ref[...] = jnp.zeros_like(loss_ref)

        sums_ref[0:1, :] += jnp.sum(dx2 * xhat, axis=0, keepdims=True)
        sums_ref[1:2, :] += jnp.sum(dx2, axis=0, keepdims=True)
        loss_ref[...] += jnp.sum(jnp.sum(err * err, axis=1, keepdims=True), axis=0, keepdims=True) * (0.5 / D_MODEL)

    row = lambda: pl.BlockSpec((tm, D_MODEL), lambda i: (i, 0))
    wide = lambda: pl.BlockSpec((tm, D_FF), lambda i: (i, 0))
    vec = lambda: pl.BlockSpec((1, D_MODEL), lambda i: (0, 0))
    return pl.pallas_call(
        body, name="mlp_fwd_loss", grid=(s // tm,),
        in_specs=[row(), _resident_weight(), _resident_weight(), row(), vec(), vec()],
        out_specs=[row(), pl.BlockSpec((SUBLANES, D_MODEL), lambda i: (0, 0)),
                   pl.BlockSpec((SUBLANES, LANES), lambda i: (0, 0)), wide(), wide(), row()],
        out_shape=[SDS((s, D_MODEL), F32), SDS((SUBLANES, D_MODEL), F32), SDS((SUBLANES, LANES), F32),
                   SDS((s, D_FF), BF16), SDS((s, D_FF), BF16), SDS((s, D_MODEL), BF16)],
        compiler_params=_params(("arbitrary",)))(x1, w_up, w_down, target, g, b)


def _resident_weight():
    return pl.BlockSpec((N_CHIPS, D_MODEL, FF_SHARD), lambda i: (0, 0, 0), pipeline_mode=pl.Buffered(1))


def _mlp_bwd_ln1(relu_up, dpre2, w_up, w_down, xhat1, rstd1, g1, w_out):
    s = dpre2.shape[0]
    tm = _row_tile(s, 256)

    def body(r_ref, d2_ref, wu_ref, wd_ref, xh_ref, rs_ref, g_ref, wo_ref,
             dup_ref, dpre_ref, sums_ref, dpreb_ref, dyc_ref, dya_ref):
        i = pl.program_id(0)
        d2 = d2_ref[...]
        d2b = d2.astype(BF16)
        dx1 = ALPHA * d2
        for k in range(N_CHIPS):
            r = r_ref[:, FF_SHARD * k:FF_SHARD * (k + 1)].astype(F32)
            dhid = lax.dot_general(d2b, wd_ref[k], NT_DIMS, preferred_element_type=F32)
            dupb = (dhid * (2.0 * r)).astype(BF16)
            dup_ref[:, FF_SHARD * k:FF_SHARD * (k + 1)] = dupb
            dx1 = dx1 + lax.dot_general(dupb, wu_ref[k], NT_DIMS, preferred_element_type=F32)
        xhat = xh_ref[...]
        dpre = _layer_norm_bwd(dx1, xhat, rs_ref[...], g_ref[...])
        dpre_ref[...] = dpre
        dpb = dpre.astype(BF16)
        dpreb_ref[...] = dpb
        dyc_ref[...] = lax.dot_general(dpb, wo_ref[0:CONV_WIDTH, :], NT_DIMS, preferred_element_type=F32)
        dya_ref[...] = lax.dot_general(dpb, wo_ref[CONV_WIDTH:, :], NT_DIMS, preferred_element_type=F32)

        @pl.when(i == 0)
        def _():
            sums_ref[...] = jnp.zeros_like(sums_ref)

        sums_ref[0:1, :] += jnp.sum(dx1 * xhat, axis=0, keepdims=True)
        sums_ref[1:2, :] += jnp.sum(dx1, axis=0, keepdims=True)

    row = lambda w: pl.BlockSpec((tm, w), lambda i: (i, 0))
    return pl.pallas_call(
        body, name="mlp_bwd_ln1", grid=(s // tm,),
        in_specs=[row(D_FF), row(D_MODEL), _resident_weight(), _resident_weight(), row(D_MODEL), row(1),
                  pl.BlockSpec((1, D_MODEL), lambda i: (0, 0)),
                  pl.BlockSpec((D_MODEL, D_MODEL), lambda i: (0, 0), pipeline_mode=pl.Buffered(1))],
        out_specs=[row(D_FF), row(D_MODEL), pl.BlockSpec((SUBLANES, D_MODEL), lambda i: (0, 0)), row(D_MODEL),
                   row(CONV_WIDTH), row(ATTN_WIDTH)],
        out_shape=[SDS((s, D_FF), BF16), SDS((s, D_MODEL), F32), SDS((SUBLANES, D_MODEL), F32),
                   SDS((s, D_MODEL), BF16), SDS((s, CONV_WIDTH), F32), SDS((s, ATTN_WIDTH), F32)],
        compiler_params=_params(("arbitrary",)))(relu_up, dpre2, w_up, w_down, xhat1, rstd1, g1, w_out)


def _grad_tn(a, b, name, out_cols, stacked):
    s, ka = a.shape
    n = b.shape[1]
    ts = _row_tile(s, 2048)
    n_steps = s // ts
    if stacked:
        tka, tn = ka, out_cols
        grid = (1, n // tn, n_steps)
        shape = (n // tn, ka, tn)
        out_spec = lambda: pl.BlockSpec((None, tka, tn), lambda r, c, t: (c, 0, 0))
    else:
        tka, tn = min(ka, 1024), n
        grid = (ka // tka, 1, n_steps)
        shape = (ka, n)
        out_spec = lambda: pl.BlockSpec((tka, tn), lambda r, c, t: (r, 0))

    def body(a_ref, b_ref, o_ref, ob_ref):
        t = pl.program_id(2)

        @pl.when(t == 0)
        def _():
            o_ref[...] = jnp.zeros_like(o_ref)

        o_ref[...] += lax.dot_general(a_ref[...].astype(BF16), b_ref[...].astype(BF16), TN_DIMS,
                                      preferred_element_type=F32)

        @pl.when(t == n_steps - 1)
        def _():
            ob_ref[...] = o_ref[...].astype(BF16)

    return pl.pallas_call(
        body, name=name, grid=grid,
        in_specs=[pl.BlockSpec((ts, tka), lambda r, c, t: (t, r)),
                  pl.BlockSpec((ts, tn), lambda r, c, t: (t, c))],
        out_specs=[out_spec(), out_spec()], out_shape=[SDS(shape, F32), SDS(shape, BF16)],
        compiler_params=_params(("parallel", "parallel", "arbitrary")))(a, b)


def _grad_w_out(ycn, yan, dpre1):
    s = dpre1.shape[0]
    ts = _row_tile(s, 2048)
    n_steps = s // ts

    def body(yc_ref, ya_ref, d_ref, o_ref, ob_ref):
        half, t = pl.program_id(0), pl.program_id(1)

        @pl.when(t == 0)
        def _():
            o_ref[...] = jnp.zeros_like(o_ref)

        db = d_ref[...]

        @pl.when(half == 0)
        def _():
            o_ref[...] += lax.dot_general(yc_ref[...], db, TN_DIMS, preferred_element_type=F32)

        @pl.when(half == 1)
        def _():
            o_ref[...] += lax.dot_general(ya_ref[...], db, TN_DIMS, preferred_element_type=F32)

        @pl.when(t == n_steps - 1)
        def _():
            ob_ref[...] = o_ref[...].astype(BF16)

    out_spec = lambda: pl.BlockSpec((CONV_WIDTH, D_MODEL), lambda r, t: (r, 0))
    return pl.pallas_call(
        body, name="grad_w_out", grid=(2, n_steps),
        in_specs=[pl.BlockSpec((ts, CONV_WIDTH), lambda r, t: (t, 0)),
                  pl.BlockSpec((ts, ATTN_WIDTH), lambda r, t: (t, 0)),
                  pl.BlockSpec((ts, D_MODEL), lambda r, t: (t, 0))],
        out_specs=[out_spec(), out_spec()],
        out_shape=[SDS((D_MODEL, D_MODEL), F32), SDS((D_MODEL, D_MODEL), BF16)],
        compiler_params=_params(("parallel", "arbitrary")))(ycn, yan, dpre1)


def _sum_with_peers(own_ref, r_ref, o_ref):
    acc = own_ref[...]
    for f in range(r_ref.shape[0]):
        acc = acc + r_ref[f].astype(F32)
    o_ref[...] = acc


def _grad_x(kc_idx, dproj, w_in, dpre1, chip_sums, earlier):
    s = dproj.shape[0]
    tm = _row_tile(s, 512)
    steps = s // tm
    n_peers = len(DEVICE_FLIPS)
    n_chips = len(CHIP_FLIPS)
    n_e = len(earlier)

    def body(kc_ref, dp_ref, w_ref, d1_ref, *rest):
        sum_ins, g_in = rest[:2 * n_e], rest[2 * n_e]
        o_ref, g_out = rest[2 * n_e + 1], rest[2 * n_e + 2]
        sum_outs, sems = rest[2 * n_e + 3:3 * n_e + 3], rest[3 * n_e + 3:]
        copies = functools.partial(_chip_reduce_copies, g_in, g_out, sems)
        i = pl.program_id(0)
        pl.when(i == 0)(functools.partial(_start_copies, copies))
        acc = ALPHA * d1_ref[...]
        for k in range(N_CHIPS):
            acc = acc + lax.dot_general(dp_ref[:, IN_SHARD * k:IN_SHARD * (k + 1)], w_ref[k], NT_DIMS,
                                        preferred_element_type=F32)
        o_ref[...] = acc
        for a in range(n_e):
            _sum_with_peers(sum_ins[2 * a], sum_ins[2 * a + 1], sum_outs[a])
        pl.when(i == steps - 1)(functools.partial(_finish_copies, copies))

    in_specs = [pl.BlockSpec((tm, IN_COLS), lambda i, kc: (i, 0)),
                pl.BlockSpec((N_CHIPS, D_MODEL, IN_SHARD), lambda i, kc: (0, 0, 0)),
                pl.BlockSpec((tm, D_MODEL), lambda i, kc: (i, 0))]
    out_specs = [pl.BlockSpec((tm, D_MODEL), lambda i, kc: (i, 0)), ANY]
    out_shape = [SDS((s, D_MODEL), F32), SDS((n_chips,) + chip_sums.shape[1:], chip_sums.dtype)]
    operands = []
    for own, recv in earlier:
        _, _, h, cols = own.shape
        th = h // steps
        in_specs.append(pl.BlockSpec((None, None, th, cols), lambda i, kc: (kc[0], kc[1], i, 0)))
        in_specs.append(pl.BlockSpec((n_peers, th, cols), lambda i, kc: (0, i, 0)))
        out_specs.append(pl.BlockSpec((th, cols), lambda i, kc: (kc[1] * steps + i, 0)))
        out_shape.append(SDS((2 * h, cols), F32))
        operands += [own, recv]
    grid_spec = pltpu.PrefetchScalarGridSpec(
        num_scalar_prefetch=1, grid=(steps,), in_specs=in_specs + [ANY], out_specs=out_specs,
        scratch_shapes=[pltpu.SemaphoreType.DMA((n_chips,)), pltpu.SemaphoreType.DMA((n_chips,))])
    return pl.pallas_call(
        body, name="grad_x", grid_spec=grid_spec, out_shape=out_shape,
        compiler_params=_params(("arbitrary",)))(kc_idx, dproj, w_in, dpre1, *operands, chip_sums)


def _adamw_step(w, g, m, v):
    nm = ADAM_B1 * m + (1.0 - ADAM_B1) * g
    nv = ADAM_B2 * v + (1.0 - ADAM_B2) * (g * g)
    m_hat = nm / (1.0 - ADAM_B1 ** ADAM_STEP)
    v_hat = nv / (1.0 - ADAM_B2 ** ADAM_STEP)
    return -ADAM_LR * (m_hat / (jnp.sqrt(v_hat) + ADAM_EPS) + ADAM_WD * w), nm, nv


def _adamw(w, g, m, v, name):
    r, c = w.shape
    tr = _row_tile(r, 256)

    def body(w_ref, g_ref, m_ref, v_ref, go_ref, d_ref, nm_ref, nv_ref):
        g_v = g_ref[...]
        go_ref[...] = g_v
        d_ref[...], nm_ref[...], nv_ref[...] = _adamw_step(w_ref[...], g_v, m_ref[...], v_ref[...])

    spec = lambda: pl.BlockSpec((tr, c), lambda i: (i, 0))
    return pl.pallas_call(
        body, name=name, grid=(r // tr,),
        in_specs=[spec(), spec(), spec(), spec()], out_specs=[spec(), spec(), spec(), spec()],
        out_shape=[SDS((r, c), F32)] * 4, compiler_params=_params(("parallel",)))(w, g, m, v)


def _adamw_small(total, conv_grad, weights, moments, variances):
    n = len(weights)
    starts = (ROW_GCONV, ROW_GATTN, ROW_LN1G, ROW_LN1B, ROW_LN2G, ROW_LN2B)

    def body(total_ref, cg_ref, *refs):
        w, m, v, outs = refs[:n], refs[n:2 * n], refs[2 * n:3 * n], refs[3 * n:]
        for p in range(n):
            rows = w[p].shape[0]
            g = cg_ref[...] if p == n - 1 else total_ref[starts[p]:starts[p] + rows, :]
            outs[4 * p][...] = g
            outs[4 * p + 1][...], outs[4 * p + 2][...], outs[4 * p + 3][...] = _adamw_step(
                w[p][...], g, m[p][...], v[p][...])

    vmem = pl.BlockSpec(memory_space=pltpu.VMEM)
    out_shape = [SDS(w.shape, F32) for w in weights for _ in range(4)]
    flat = pl.pallas_call(
        body, name="adamw_small", in_specs=[vmem] * (2 + 3 * n), out_specs=[vmem] * (4 * n),
        out_shape=out_shape)(total, conv_grad, *weights, *moments, *variances)
    return [flat[4 * p:4 * p + 4] for p in range(n)]


def _sum_partials(kc_idx, own, recv, name):
    h, cols = own.shape
    th = _row_tile(h, 128)
    n_peers = recv.shape[0]

    def body(kc_ref, own_ref, r_ref, o_ref):
        _sum_with_peers(own_ref, r_ref, o_ref)

    grid_spec = pltpu.PrefetchScalarGridSpec(
        num_scalar_prefetch=1, grid=(h // th,),
        in_specs=[pl.BlockSpec((th, cols), lambda t, kc: (t, 0)),
                  pl.BlockSpec((n_peers, th, cols), lambda t, kc: (0, t, 0))],
        out_specs=pl.BlockSpec((th, cols), lambda t, kc: (kc[1] * (h // th) + t, 0)))
    return pl.pallas_call(
        body, name=name, grid_spec=grid_spec, out_shape=SDS((2 * h, cols), F32),
        compiler_params=_params(("parallel",)))(kc_idx, own, recv)


def _add_sibling(kc_idx, grad, recv):
    _, _, h, cols = grad.shape
    th = _row_tile(h, 512)

    def body(kc_ref, g_ref, r_ref, sums_ref, own_ref):
        total = g_ref[...] + r_ref[...].astype(F32)
        sums_ref[...] = total.astype(BF16)

        @pl.when(pl.program_id(1) == kc_ref[0])
        def _():
            own_ref[...] = total

    grid_spec = pltpu.PrefetchScalarGridSpec(
        num_scalar_prefetch=1, grid=(h // th, N_CHIPS),
        in_specs=[pl.BlockSpec((None, None, th, cols), lambda t, k, kc: (k, kc[1], t, 0)),
                  pl.BlockSpec((None, th, cols), lambda t, k, kc: (k, t, 0))],
        out_specs=[pl.BlockSpec((None, th, cols), lambda t, k, kc: (k, t, 0)),
                   pl.BlockSpec((th, cols), lambda t, k, kc: (t, 0))])
    return pl.pallas_call(
        body, name="add_sibling_w_in", grid_spec=grid_spec,
        out_shape=[SDS((N_CHIPS, h, cols), BF16), SDS((h, cols), F32)],
        compiler_params=_params(("parallel", "arbitrary")))(kc_idx, grad, recv)


def _gather_weights(kc_idx, w_in_slots, conv_slots, later):
    n_l = len(later)
    steps = SUBLANES

    def body(kc_ref, *refs):
        cast_ins, cast_outs = refs[:n_l], refs[n_l + 2:2 * n_l + 2]
        w_buf, conv_buf = refs[2 * n_l + 2], refs[2 * n_l + 3]
        sems = refs[2 * n_l + 4:]
        i = pl.program_id(0)

        def first_hop():
            x, y, c = _position()
            sends, arrivals = _gather_chip_hop([w_buf], sems)
            mine = conv_buf.at[2 * x + y]
            for j, (fx, fy) in enumerate(CHIP_FLIPS):
                tx, ty = _flip(x, fx), _flip(y, fy)
                there = conv_buf.at[2 * tx + ty]
                sends.append(_remote_copy(mine, mine, sems, 6 + j, (tx, ty, c)))
                arrivals.append(_remote_copy(there, there, sems, 6 + j, (tx, ty, c)))
            return sends, arrivals

        pl.when(i == 0)(functools.partial(_start_copies, first_hop))
        for src, dst in zip(cast_ins, cast_outs):
            dst[...] = src[...].astype(BF16)

        @pl.when(i == steps - 1)
        def _():
            sends, arrivals = first_hop()
            for cp in arrivals:
                cp.wait_recv()
            _start_copies(functools.partial(_gather_sibling_hop, [w_buf], sems))
            _finish_copies(functools.partial(_gather_sibling_hop, [w_buf], sems))
            for cp in sends:
                cp.wait_send()

    in_specs, out_specs, out_shape = [], [], []
    for w in later:
        r, c = w.shape
        in_specs.append(pl.BlockSpec((r // steps, c), lambda i, kc: (i, 0)))
        out_specs.append(pl.BlockSpec((None, r // steps, c), lambda i, kc: (kc[0], i, 0)))
        out_shape.append(SDS((N_CHIPS, r, c), BF16))
    grid_spec = pltpu.PrefetchScalarGridSpec(
        num_scalar_prefetch=1, grid=(steps,), in_specs=in_specs + [ANY, ANY], out_specs=out_specs + [ANY, ANY],
        scratch_shapes=[pltpu.SemaphoreType.DMA((9,)), pltpu.SemaphoreType.DMA((9,))])
    return pl.pallas_call(
        body, name="gather_weights", grid_spec=grid_spec,
        out_shape=out_shape + [SDS(w_in_slots.shape, w_in_slots.dtype), SDS(conv_slots.shape, conv_slots.dtype)],
        input_output_aliases={n_l + 1: n_l, n_l + 2: n_l + 1},
        compiler_params=_params(("arbitrary",)))(kc_idx, *later, w_in_slots, conv_slots)


def _exchange_with_sibling(partial):
    h = partial.shape[1] // 2

    def body(g_in, g_out, send_sems, recv_sems):
        x, y, c = _position()
        theirs = pl.ds(pl.multiple_of((1 - c) * h, h), h)
        copies = [_remote_copy(g_in.at[k, theirs], g_out.at[k], (send_sems, recv_sems), k, (x, y, 1 - c))
                  for k in range(N_CHIPS)]
        for cp in copies:
            cp.start()
        for cp in copies:
            cp.wait_recv()
        for cp in copies:
            cp.wait_send()

    return pl.pallas_call(
        body, name="exchange_with_sibling", in_specs=[ANY], out_specs=ANY,
        out_shape=SDS((N_CHIPS, h, partial.shape[2]), partial.dtype),
        scratch_shapes=[pltpu.SemaphoreType.DMA((N_CHIPS,)), pltpu.SemaphoreType.DMA((N_CHIPS,))])(partial)


def _finish_exchange(pieces, vec):
    n = len(pieces)
    n_dev = 2 * N_CHIPS

    def body(*refs):
        v_ref = refs[n]
        outs, o_ref = refs[n + 1:2 * n + 1], refs[2 * n + 1]
        buf, send_sems, recv_sems = refs[2 * n + 2:]
        x, y, c = _position()
        sibling = (x, y, 1 - c)
        me = 4 * x + 2 * y + c
        buf[me] = v_ref[...]
        started = []
        for f, (fx, fy, fc) in enumerate(DEVICE_FLIPS):
            cp = pltpu.make_async_remote_copy(
                src_ref=v_ref, dst_ref=buf.at[me], send_sem=send_sems.at[n + f], recv_sem=recv_sems.at[n + f],
                device_id=(_flip(x, fx), _flip(y, fy), _flip(c, fc)), device_id_type=MESH)
            cp.start()
            started.append(cp)
        for a in range(n):
            h = pieces[a].shape[0] // 2
            mine = outs[a].at[pl.ds(pl.multiple_of(c * h, h), h)]
            cp = pltpu.make_async_remote_copy(
                src_ref=mine, dst_ref=mine, send_sem=send_sems.at[a], recv_sem=recv_sems.at[a],
                device_id=sibling, device_id_type=MESH)
            cp.start()
            started.append(cp)
        for a in range(n):
            h = pieces[a].shape[0] // 2
            theirs = outs[a].at[pl.ds(pl.multiple_of((1 - c) * h, h), h)]
            pltpu.make_async_remote_copy(
                src_ref=theirs, dst_ref=theirs, send_sem=send_sems.at[a], recv_sem=recv_sems.at[a],
                device_id=sibling, device_id_type=MESH).wait_recv()
        for f, (fx, fy, fc) in enumerate(DEVICE_FLIPS):
            src = 4 * _flip(x, fx) + 2 * _flip(y, fy) + _flip(c, fc)
            pltpu.make_async_remote_copy(
                src_ref=v_ref, dst_ref=buf.at[src], send_sem=send_sems.at[n + f], recv_sem=recv_sems.at[n + f],
                device_id=(x, y, c), device_id_type=MESH).wait_recv()
        for cp in started:
            cp.wait_send()
        acc = buf[0]
        for d in range(1, n_dev):
            acc = acc + buf[d]
        o_ref[...] = acc

    vmem = pl.BlockSpec(memory_space=pltpu.VMEM)
    out_shape = [SDS(p.shape, p.dtype) for p in pieces] + [SDS(vec.shape, vec.dtype)]
    n_sems = n + n_dev - 1
    return pl.pallas_call(
        body, name="finish_exchange", in_specs=[ANY] * n + [vmem], out_specs=[ANY] * n + [vmem],
        out_shape=out_shape, input_output_aliases={a: a for a in range(n)},
        scratch_shapes=[pltpu.VMEM((n_dev,) + vec.shape, vec.dtype), pltpu.SemaphoreType.DMA((n_sems,)),
                        pltpu.SemaphoreType.DMA((n_sems,))])(*pieces, vec)


def _constants():
    r = jnp.arange(2 * KEY_BLOCK)[:, None] % KEY_BLOCK
    c = jnp.arange(2 * KEY_BLOCK)[None, :]
    later = jnp.where(c < KEY_BLOCK, r > c, True).astype(BF16)
    earlier = jnp.where(c < KEY_BLOCK, r < c, True).astype(BF16)
    upto = jnp.where(c < KEY_BLOCK, r <= c, True).astype(BF16)
    gr = (jnp.arange(2 * LANES)[:, None] % LANES) // GROUP
    gc = jnp.arange(LANES)[None, :] // GROUP
    gmat = (gr == gc).astype(BF16)
    return later, jnp.stack([earlier, upto]), gmat


def _rows(v):
    return v.reshape(-1, LANES)


def kernel(x, w_in, conv_w, g_conv, g_attn, w_out, ln1_g, ln1_b, w_up, w_down, ln2_g, ln2_b, loss_target, m_w_in, m_conv_w, m_g_conv, m_g_attn, m_w_out, m_ln1_g, m_ln1_b, m_w_up, m_w_down, m_ln2_g, m_ln2_b, v_w_in, v_conv_w, v_g_conv, v_g_attn, v_w_out, v_ln1_g, v_ln1_b, v_w_up, v_w_down, v_ln2_g, v_ln2_b):
    xs, target = x[0], loss_target[0]
    mesh_x, mesh_y, mesh_c = _position()
    k_idx = 2 * mesh_x + mesh_y
    kc_idx = jnp.stack([k_idx, mesh_c]).astype(jnp.int32)
    tri_later, tri_earlier, gmat = _constants()

    w_in_b = _cast_into_slot(kc_idx, w_in[0], "cast_w_in")
    conv_slot = jnp.pad(conv_w, ((0, 0), (0, SUBLANES - conv_w.shape[1]), (0, 0)))
    conv_b = lax.dynamic_update_slice(jnp.zeros((N_CHIPS, SUBLANES, LANES), F32), conv_slot, (k_idx, 0, 0))
    w_out_b, w_up_b, w_down_b, w_in_f, conv_f = _gather_weights(
        kc_idx, w_in_b, conv_b, [w_out[0], w_up[0], w_down[0]])
    taps = jnp.transpose(conv_f, (1, 0, 2)).reshape(SUBLANES, CONV_WIDTH)

    gates, qkv, xs_b = _proj(xs, w_in_f)
    ycn = _conv_fwd(gates, taps, g_conv, gmat)
    o, yan, tot, cut, w_out_f, w_up_f, w_down_f = _attn_fwd(
        qkv, g_attn, tri_later, gmat, [w_out_b, w_up_b, w_down_b])
    w_out_f = w_out_f.reshape(D_MODEL, D_MODEL)
    x1, xhat1, rstd1, x1_b = _mix_ln1(ycn, yan, w_out_f, xs, ln1_g, ln1_b)
    dpre2, ln2_sums, loss_sum, relu_up, hid, dpre2_b = _mlp_fwd_loss(x1, w_up_f, w_down_f, target, ln2_g, ln2_b)

    dup, dpre1, ln1_sums, dpre1_b, dycn, dyan = _mlp_bwd_ln1(
        relu_up, dpre2, w_up_f, w_down_f, xhat1, rstd1, ln1_g, w_out_f)
    gw_up = _grad_tn(x1_b, dup, "grad_w_up", FF_SHARD, True)
    gw_down = [g.reshape(N_CHIPS, FF_SHARD, D_MODEL) for g in _grad_tn(hid, dpre2_b, "grad_w_down", D_MODEL, False)]
    gw_out = [g.reshape(N_CHIPS, D_MODEL // N_CHIPS, D_MODEL) for g in _grad_w_out(ycn, yan, dpre1_b)]
    dq, dk, dv, gattn_sums, recv_out, recv_up, recv_down = _attn_bwd(
        qkv, o, tot, dyan, g_attn, tri_earlier, gmat, cut, [gw_out[1], gw_up[1], gw_down[1]])
    dbg, dy, conv_sums = _conv_bwd_gate(gates, dycn, taps, g_conv, gmat)
    dproj = _dproj_assemble(gates, dy, dbg, dq, dk, dv, taps)
    gw_in = _grad_tn(xs_b, dproj, "grad_w_in", IN_SHARD, True)
    halves = lambda g: g.reshape(N_CHIPS, 2, g.shape[1] // 2, g.shape[2])
    chip_sums, own_sum = _add_sibling(kc_idx, halves(gw_in[0]), _exchange_with_sibling(gw_in[1]))
    grad_x, recv_in, p_out, p_up, p_down = _grad_x(
        kc_idx, dproj, w_in_f, dpre1, chip_sums,
        [(halves(gw_out[0]), recv_out), (halves(gw_up[0]), recv_up), (halves(gw_down[0]), recv_down)])
    pieces = [_sum_partials(kc_idx, own_sum, recv_in, "sum_partials_w_in"), p_out, p_up, p_down]
    conv_rows = jnp.transpose(conv_sums[0:3].reshape(3, N_CHIPS, LANES), (1, 0, 2)).reshape(3 * N_CHIPS, LANES)
    small = jnp.concatenate([
        loss_sum, _rows(conv_sums[3]), _rows(gattn_sums[0]), _rows(ln1_sums[0]), _rows(ln1_sums[1]),
        _rows(ln2_sums[0]), _rows(ln2_sums[1]), conv_rows,
        jnp.zeros((SMALL_ROWS - ROW_CONVW - 3 * N_CHIPS, LANES), F32)], axis=0)
    g_w_in, g_w_out, g_w_up, g_w_down, total = _finish_exchange(pieces, small)
    loss = total[ROW_LOSS, 0]
    g_conv_w = lax.dynamic_slice(total, (ROW_CONVW + 3 * k_idx, 0), (3, LANES))

    small_names = ["g_conv", "g_attn", "ln1_g", "ln1_b", "ln2_g", "ln2_b", "conv_w"]
    small_w = [g_conv, g_attn, ln1_g, ln1_b, ln2_g, ln2_b, conv_w]
    small_m = [m_g_conv, m_g_attn, m_ln1_g, m_ln1_b, m_ln2_g, m_ln2_b, m_conv_w]
    small_v = [v_g_conv, v_g_attn, v_ln1_g, v_ln1_b, v_ln2_g, v_ln2_b, v_conv_w]
    small_out = dict(zip(small_names, _adamw_small(
        total, g_conv_w, [_rows(a) for a in small_w], [_rows(a) for a in small_m], [_rows(a) for a in small_v])))
    small_shape = dict(zip(small_names, (a.shape for a in small_w)))
    big_out = {
        "w_in": _adamw(w_in[0], g_w_in, m_w_in[0], v_w_in[0], "adamw_w_in"),
        "w_out": _adamw(w_out[0], g_w_out, m_w_out[0], v_w_out[0], "adamw_w_out"),
        "w_up": _adamw(w_up[0], g_w_up, m_w_up[0], v_w_up[0], "adamw_w_up"),
        "w_down": _adamw(w_down[0], g_w_down, m_w_down[0], v_w_down[0], "adamw_w_down"),
    }
    order = ["w_in", "conv_w", "g_conv", "g_attn", "w_out", "ln1_g", "ln1_b", "w_up", "w_down", "ln2_g", "ln2_b"]

    def leaf(kind, name):
        if name in big_out:
            return big_out[name][kind][None]
        return small_out[name][kind].reshape(small_shape[name])

    outs = [loss, grad_x[None]]
    for kind in range(4):
        outs.extend(leaf(kind, name) for name in order)
    return tuple(outs)
```

```python
import functools

import jax
import jax.numpy as jnp
from jax import lax
from jax.experimental import pallas as pl
from jax.experimental.pallas import tpu as pltpu

F32 = jnp.float32
BF16 = jnp.bfloat16
SDS = jax.ShapeDtypeStruct

D_MODEL = 1024
CONV_WIDTH = 512
ATTN_WIDTH = 512
GROUP = 64
GATE_COLS = 3 * CONV_WIDTH
QKV_COLS = 3 * ATTN_WIDTH
IN_COLS = GATE_COLS + QKV_COLS
D_FF = 4 * D_MODEL
N_CHIPS = 4
IN_SHARD = IN_COLS // N_CHIPS
FF_SHARD = D_FF // N_CHIPS
ALPHA = float(2.0 ** 0.25)
LN_EPS = 1e-5
RMS_EPS = 1e-6
ATTN_SCALE = GROUP ** -0.5
LOG2_E = 1.4426950408889634
ADAM_LR = 0.001
ADAM_B1 = 0.9
ADAM_B2 = 0.999
ADAM_EPS = 1e-08
ADAM_WD = 0.01
ADAM_STEP = 10

LANES = 128
SUBLANES = 8
KEY_BLOCK = 128
ATTN_Q_TILE = 512
ATTN_KEY_BLOCKS = 2
ATTN_DIAG_GROUPS = 2
ATTN_DEAD_LOG2 = 200.0
VMEM_LIMIT = 56 * 1024 * 1024

MESH = pl.DeviceIdType.MESH
CHIP_FLIPS = ((1, 0), (0, 1), (1, 1))
DEVICE_FLIPS = tuple((fx, fy, fc) for fx in (0, 1) for fy in (0, 1) for fc in (0, 1))[1:]
NT_DIMS = (((1,), (1,)), ((), ()))
TN_DIMS = (((0,), (0,)), ((), ()))

ROW_LOSS = 0
ROW_GCONV = 8
ROW_GATTN = 12
ROW_LN1G = 16
ROW_LN1B = 24
ROW_LN2G = 32
ROW_LN2B = 40
ROW_CONVW = 48
SMALL_ROWS = 64


def _params(sem=None):
    return pltpu.CompilerParams(dimension_semantics=sem, vmem_limit_bytes=VMEM_LIMIT)


def _flip(v, f):
    return 1 - v if f else v


def _position():
    return lax.axis_index("x"), lax.axis_index("y"), lax.axis_index("c")


def _hilo(v):
    hi = v.astype(BF16)
    lo = (v - hi.astype(F32)).astype(BF16)
    return jnp.concatenate([hi, lo], axis=1)


def _hilo_dot(v, mat):
    return jnp.dot(_hilo(v), mat, preferred_element_type=F32)


def _group_sum(v, gmat):
    parts = [_hilo_dot(v[:, LANES * j:LANES * (j + 1)], gmat) for j in range(v.shape[1] // LANES)]
    return parts[0] if len(parts) == 1 else jnp.concatenate(parts, axis=1)


def _softplus_terms(z):
    sp = jnp.log2(1.0 + jnp.exp2(-jnp.abs(z)))
    log_beta = jnp.minimum(z, 0.0) - sp
    return log_beta, log_beta - z


def _layer_norm_fwd(pre, g, b):
    mu = jnp.mean(pre, axis=-1, keepdims=True)
    d = pre - mu
    var = jnp.mean(d * d, axis=-1, keepdims=True)
    rstd = lax.rsqrt(var + LN_EPS)
    xhat = d * rstd
    return xhat * g + b, xhat, rstd


def _layer_norm_bwd(dy, xhat, rstd, g):
    dxh = dy * g
    m1 = jnp.mean(dxh, axis=-1, keepdims=True)
    m2 = jnp.mean(dxh * xhat, axis=-1, keepdims=True)
    return rstd * (dxh - m1 - xhat * m2)


def _row_tile(s, want):
    return min(s, want)


def _cast_into_slot(kc_idx, w, name):
    r, c = w.shape
    tr = _row_tile(r, 256)

    def body(kc_ref, w_ref, o_ref):
        o_ref[...] = w_ref[...].astype(BF16)

    grid_spec = pltpu.PrefetchScalarGridSpec(
        num_scalar_prefetch=1, grid=(r // tr,),
        in_specs=[pl.BlockSpec((tr, c), lambda i, kc: (i, 0))],
        out_specs=pl.BlockSpec((None, tr, c), lambda i, kc: (kc[0], i, 0)))
    return pl.pallas_call(
        body, name=name, grid_spec=grid_spec, out_shape=SDS((N_CHIPS, r, c), BF16),
        compiler_params=_params(("parallel",)))(kc_idx, w)


def _proj(x, w_in):
    s = x.shape[0]
    tm = _row_tile(s, 512)

    def body(x_ref, w_ref, gates_ref, qkv_ref, xb_ref):
        xb = x_ref[...].astype(BF16)
        xb_ref[...] = xb
        for k in range(N_CHIPS):
            acc = jnp.dot(xb, w_ref[k], preferred_element_type=F32)
            if k < 2:
                gates_ref[:, IN_SHARD * k:IN_SHARD * (k + 1)] = acc
            else:
                qkv_ref[:, IN_SHARD * (k - 2):IN_SHARD * (k - 1)] = acc.astype(BF16)

    return pl.pallas_call(
        body, name="proj", grid=(s // tm,),
        in_specs=[pl.BlockSpec((tm, D_MODEL), lambda i: (i, 0)),
                  pl.BlockSpec((N_CHIPS, D_MODEL, IN_SHARD), lambda i: (0, 0, 0))],
        out_specs=[pl.BlockSpec((tm, GATE_COLS), lambda i: (i, 0)),
                   pl.BlockSpec((tm, QKV_COLS), lambda i: (i, 0)),
                   pl.BlockSpec((tm, D_MODEL), lambda i: (i, 0))],
        out_shape=[SDS((s, GATE_COLS), F32), SDS((s, QKV_COLS), BF16), SDS((s, D_MODEL), BF16)],
        compiler_params=_params(("parallel",)))(x, w_in)


def _conv_forward_values(g_ref, halo_ref, taps_ref, first_block):
    gates = g_ref[...]
    tr = gates.shape[0]
    bg = gates[:, :CONV_WIDTH]
    cg = gates[:, CONV_WIDTH:2 * CONV_WIDTH]
    h = gates[:, 2 * CONV_WIDTH:]
    u = cg * h

    def prev(r):
        v = halo_ref[r:r + 1, CONV_WIDTH:2 * CONV_WIDTH] * halo_ref[r:r + 1, 2 * CONV_WIDTH:GATE_COLS]
        return jnp.where(first_block, 0.0, v)

    row = lax.broadcasted_iota(jnp.int32, (tr, CONV_WIDTH), 0)
    u1 = jnp.where(row == 0, prev(7), pltpu.roll(u, 1, 0))
    u2 = jnp.where(row == 0, prev(6), jnp.where(row == 1, prev(7), pltpu.roll(u, 2, 0)))
    y = taps_ref[0:1, :] * u2 + taps_ref[1:2, :] * u1 + taps_ref[2:3, :] * u
    return bg, cg, h, u, u1, u2, y


def _conv_fwd(gates, taps, g_conv, gmat):
    s = gates.shape[0]
    tr = _row_tile(s, 512)
    hb = tr // SUBLANES

    def body(g_ref, halo_ref, taps_ref, gain_ref, gmat_ref, out_ref):
        i = pl.program_id(0)
        bg, _, _, _, _, _, y = _conv_forward_values(g_ref, halo_ref, taps_ref, i == 0)
        yc = bg * y
        ms = _group_sum(yc * yc, gmat_ref[...]) * (1.0 / GROUP)
        out_ref[...] = (yc * lax.rsqrt(ms + RMS_EPS) * gain_ref[...]).astype(BF16)

    return pl.pallas_call(
        body, name="conv_fwd", grid=(s // tr,),
        in_specs=[pl.BlockSpec((tr, GATE_COLS), lambda i: (i, 0)),
                  pl.BlockSpec((SUBLANES, GATE_COLS), lambda i: (jnp.maximum(i * hb - 1, 0), 0)),
                  pl.BlockSpec((SUBLANES, CONV_WIDTH), lambda i: (0, 0)),
                  pl.BlockSpec((1, CONV_WIDTH), lambda i: (0, 0)),
                  pl.BlockSpec((2 * LANES, LANES), lambda i: (0, 0))],
        out_specs=pl.BlockSpec((tr, CONV_WIDTH), lambda i: (i, 0)),
        out_shape=SDS((s, CONV_WIDTH), BF16),
        compiler_params=_params(("parallel",)))(gates, gates, taps, g_conv, gmat)


def _conv_bwd_gate(gates, dycn, taps, g_conv, gmat):
    s = gates.shape[0]
    tr = _row_tile(s, 512)
    hb = tr // SUBLANES

    def body(g_ref, halo_ref, dn_ref, taps_ref, gain_ref, gmat_ref, dbg_ref, dy_ref, sums_ref):
        i = pl.program_id(0)
        bg, _, _, u, u1, u2, y = _conv_forward_values(g_ref, halo_ref, taps_ref, i == 0)
        gmat_v = gmat_ref[...]
        yc = bg * y
        rstd = lax.rsqrt(_group_sum(yc * yc, gmat_v) * (1.0 / GROUP) + RMS_EPS)
        n = yc * rstd
        dout = dn_ref[...]
        dn = dout * gain_ref[...]
        dyc = rstd * (dn - n * (_group_sum(dn * n, gmat_v) * (1.0 / GROUP)))
        dbg_ref[...] = (dyc * y).astype(BF16)
        dy = dyc * bg
        dy_ref[...] = dy

        @pl.when(i == 0)
        def _():
            sums_ref[...] = jnp.zeros_like(sums_ref)

        sums_ref[0:1, :] += jnp.sum(dy * u2, axis=0, keepdims=True)
        sums_ref[1:2, :] += jnp.sum(dy * u1, axis=0, keepdims=True)
        sums_ref[2:3, :] += jnp.sum(dy * u, axis=0, keepdims=True)
        sums_ref[3:4, :] += jnp.sum(dout * n, axis=0, keepdims=True)

    return pl.pallas_call(
        body, name="conv_bwd_gate", grid=(s // tr,),
        in_specs=[pl.BlockSpec((tr, GATE_COLS), lambda i: (i, 0)),
                  pl.BlockSpec((SUBLANES, GATE_COLS), lambda i: (jnp.maximum(i * hb - 1, 0), 0)),
                  pl.BlockSpec((tr, CONV_WIDTH), lambda i: (i, 0)),
                  pl.BlockSpec((SUBLANES, CONV_WIDTH), lambda i: (0, 0)),
                  pl.BlockSpec((1, CONV_WIDTH), lambda i: (0, 0)),
                  pl.BlockSpec((2 * LANES, LANES), lambda i: (0, 0))],
        out_specs=[pl.BlockSpec((tr, CONV_WIDTH), lambda i: (i, 0)),
                   pl.BlockSpec((tr, CONV_WIDTH), lambda i: (i, 0)),
                   pl.BlockSpec((SUBLANES, CONV_WIDTH), lambda i: (0, 0))],
        out_shape=[SDS((s, CONV_WIDTH), BF16), SDS((s, CONV_WIDTH), F32), SDS((SUBLANES, CONV_WIDTH), F32)],
        compiler_params=_params(("arbitrary",)))(gates, gates, dycn, taps, g_conv, gmat)


def _dproj_assemble(gates, dy, dbg, dq, dk, dv, taps):
    s = gates.shape[0]
    tr = _row_tile(s, 512)
    hb = tr // SUBLANES
    last = s // SUBLANES - 1
    n_blocks = s // tr

    def body(g_ref, dy_ref, halo_ref, dbg_ref, dq_ref, dk_ref, dv_ref, taps_ref, out_ref):
        i = pl.program_id(0)
        gates_v = g_ref[...]
        cg = gates_v[:, CONV_WIDTH:2 * CONV_WIDTH]
        h = gates_v[:, 2 * CONV_WIDTH:]
        dy_v = dy_ref[...]
        last_block = i == n_blocks - 1
        nxt = lambda r: jnp.where(last_block, 0.0, halo_ref[r:r + 1, :])
        row = lax.broadcasted_iota(jnp.int32, (tr, CONV_WIDTH), 0)
        d1 = jnp.where(row == tr - 1, nxt(0), pltpu.roll(dy_v, tr - 1, 0))
        d2 = jnp.where(row == tr - 1, nxt(1), jnp.where(row == tr - 2, nxt(0), pltpu.roll(dy_v, tr - 2, 0)))
        du = taps_ref[2:3, :] * dy_v + taps_ref[1:2, :] * d1 + taps_ref[0:1, :] * d2
        out_ref[:, 0:CONV_WIDTH] = dbg_ref[...]
        out_ref[:, CONV_WIDTH:2 * CONV_WIDTH] = (du * h).astype(BF16)
        out_ref[:, 2 * CONV_WIDTH:GATE_COLS] = (du * cg).astype(BF16)
        out_ref[:, GATE_COLS:GATE_COLS + ATTN_WIDTH] = dq_ref[...]
        out_ref[:, GATE_COLS + ATTN_WIDTH:GATE_COLS + 2 * ATTN_WIDTH] = dk_ref[...]
        out_ref[:, GATE_COLS + 2 * ATTN_WIDTH:] = dv_ref[...]

    row_spec = lambda w: pl.BlockSpec((tr, w), lambda i: (i, 0))
    return pl.pallas_call(
        body, name="dproj_assemble", grid=(s // tr,),
        in_specs=[row_spec(GATE_COLS), row_spec(CONV_WIDTH),
                  pl.BlockSpec((SUBLANES, CONV_WIDTH), lambda i: (jnp.minimum((i + 1) * hb, last), 0)),
                  row_spec(CONV_WIDTH), row_spec(ATTN_WIDTH), row_spec(ATTN_WIDTH), row_spec(ATTN_WIDTH),
                  pl.BlockSpec((SUBLANES, CONV_WIDTH), lambda i: (0, 0))],
        out_specs=row_spec(IN_COLS),
        out_shape=SDS((s, IN_COLS), BF16),
        compiler_params=_params(("parallel",)))(gates, dy, dy, dbg, dq, dk, dv, taps)


def _stack_heads(rows, nb):
    lane = lax.broadcasted_iota(jnp.int32, (1, LANES), 1)
    zero = jnp.zeros((KEY_BLOCK, LANES), rows.dtype)
    parts = []
    for blk in range(nb):
        r = rows[blk * KEY_BLOCK:(blk + 1) * KEY_BLOCK]
        parts.append(jnp.where(lane < GROUP, r, zero))
        parts.append(jnp.where(lane < GROUP, zero, r))
    return jnp.concatenate(parts, axis=0)


def _stack_hilo(v, n_cols):
    return jnp.concatenate([_hilo(v[:, c * KEY_BLOCK:(c + 1) * KEY_BLOCK]) for c in range(n_cols)], axis=0)


def _causal_mask(tq, nb, diag_base):
    shape = (tq, 2 * nb * KEY_BLOCK)
    row = lax.broadcasted_iota(jnp.int32, shape, 0)
    col = lax.broadcasted_iota(jnp.int32, shape, 1)
    key = diag_base + (col // (2 * KEY_BLOCK)) * KEY_BLOCK + col % KEY_BLOCK
    return key < row


ANY = pl.BlockSpec(memory_space=pl.ANY)


def _remote_copy(src, dst, sems, idx, target):
    return pltpu.make_async_remote_copy(src_ref=src, dst_ref=dst, send_sem=sems[0].at[idx], recv_sem=sems[1].at[idx],
                                        device_id=target, device_id_type=MESH)


def _gather_chip_hop(bufs, sems):
    x, y, c = _position()
    sends, arrivals = [], []
    for a, buf in enumerate(bufs):
        h = buf.shape[1] // 2
        rows = pl.ds(pl.multiple_of(c * h, h), h)
        mine = buf.at[2 * x + y, rows]
        for j, (fx, fy) in enumerate(CHIP_FLIPS):
            tx, ty = _flip(x, fx), _flip(y, fy)
            there = buf.at[2 * tx + ty, rows]
            sends.append(_remote_copy(mine, mine, sems, 6 * a + j, (tx, ty, c)))
            arrivals.append(_remote_copy(there, there, sems, 6 * a + j, (tx, ty, c)))
    return sends, arrivals


def _gather_sibling_hop(bufs, sems):
    x, y, c = _position()
    sends, arrivals = [], []
    for a, buf in enumerate(bufs):
        h = buf.shape[1] // 2
        mine, theirs = pl.ds(pl.multiple_of(c * h, h), h), pl.ds(pl.multiple_of((1 - c) * h, h), h)
        for j, (fx, fy) in enumerate(CHIP_FLIPS):
            kj = 2 * _flip(x, fx) + _flip(y, fy)
            landed, other = buf.at[kj, mine], buf.at[kj, theirs]
            sends.append(_remote_copy(landed, landed, sems, 6 * a + 3 + j, (x, y, 1 - c)))
            arrivals.append(_remote_copy(other, other, sems, 6 * a + 3 + j, (x, y, 1 - c)))
    return sends, arrivals


def _reduce_copies(ins, outs, sems):
    x, y, c = _position()
    sends, arrivals = [], []
    for a in range(len(ins)):
        h = ins[a].shape[1] // 2
        for f, (fx, fy, fc) in enumerate(DEVICE_FLIPS):
            tx, ty, tc = _flip(x, fx), _flip(y, fy), _flip(c, fc)
            src = ins[a].at[2 * tx + ty, pl.ds(pl.multiple_of(tc * h, h), h)]
            sends.append(_remote_copy(src, outs[a].at[f], sems, 7 * a + f, (tx, ty, tc)))
            arrivals.append(_remote_copy(outs[a].at[f], outs[a].at[f], sems, 7 * a + f, (tx, ty, tc)))
    return sends, arrivals


def _chip_reduce_copies(src, dst, sems):
    x, y, c = _position()
    sends, arrivals = [], []
    for j, (fx, fy) in enumerate(CHIP_FLIPS):
        tx, ty = _flip(x, fx), _flip(y, fy)
        sends.append(_remote_copy(src.at[2 * tx + ty], dst.at[j], sems, j, (tx, ty, c)))
        arrivals.append(_remote_copy(dst.at[j], dst.at[j], sems, j, (tx, ty, c)))
    return sends, arrivals


def _start_copies(make):
    sends, _ = make()
    for cp in sends:
        cp.start()


def _finish_copies(make):
    sends, arrivals = make()
    for cp in arrivals:
        cp.wait_recv()
    for cp in sends:
        cp.wait_send()


def _attn_fwd(qkv, g_attn, tri, gmat, shards):
    n_w = len(shards)
    s = qkv.shape[0]
    tq = _row_tile(s, ATTN_Q_TILE)
    tk = KEY_BLOCK
    nb = ATTN_KEY_BLOCKS
    width = nb * tk
    n_groups = ATTN_DIAG_GROUPS
    group = tq // n_groups
    pairs = ATTN_WIDTH // LANES

    def body(q_ref, k_ref, v_ref, gain_ref, tri_ref, gmat_ref, *rest):
        o_ref, yn_ref, tot_ref, cut_ref = rest[n_w:n_w + 4]
        w_bufs, sems = rest[n_w + 4:2 * n_w + 4], rest[2 * n_w + 4:]
        chip_hop = functools.partial(_gather_chip_hop, w_bufs, sems)
        sibling_hop = functools.partial(_gather_sibling_hop, w_bufs, sems)
        p, i = pl.program_id(0), pl.program_id(1)
        pl.when((p == 0) & (i == 0))(functools.partial(_start_copies, chip_hop))

        @pl.when((p == pairs - 1) & (i == 0))
        def _():
            for cp in chip_hop()[1]:
                cp.wait_recv()
            _start_copies(sibling_hop)

        q2 = q_ref[...]
        tri_v = tri_ref[...]

        def trip(s0, n_blk, rows, carry, diag_base):
            r0, nr = rows
            run = [carry[0], carry[1]]
            oacc = carry[2]
            ksel = _stack_heads(k_ref[pl.ds(s0, n_blk * tk), :], n_blk)
            vsel = _stack_heads(v_ref[pl.ds(s0, n_blk * tk), :], n_blk)
            z = lax.dot_general(q2[r0:r0 + nr], ksel, NT_DIMS, preferred_element_type=F32) * (ATTN_SCALE * LOG2_E)
            log_beta, log_keep = _softplus_terms(z)
            if diag_base is not None:
                valid = _causal_mask(nr, n_blk, diag_base)
                log_keep = jnp.where(valid, log_keep, 0.0)
            ct = jnp.dot(_stack_hilo(log_keep, 2 * n_blk), tri_v, preferred_element_type=F32)
            a_parts = [None] * (2 * n_blk)
            for c in reversed(range(2 * n_blk)):
                h = c % 2
                ct_c = ct[c * nr:(c + 1) * nr]
                a_parts[c] = jnp.exp2(log_beta[:, c * tk:(c + 1) * tk] + ct_c[:, :tk] + run[h])
                run[h] = run[h] + ct_c[:, tk:]
            a = jnp.concatenate(a_parts, axis=1)
            if diag_base is not None:
                a = jnp.where(valid, a, 0.0)
            oacc = oacc + jnp.dot(a.astype(BF16), vsel, preferred_element_type=F32)
            return run[0], run[1], oacc

        n_full = i * (tq // width)
        tile = p * pl.num_programs(1) + i

        def alive(run_a, run_b):
            return jnp.max(jnp.maximum(run_a, run_b)) > -ATTN_DEAD_LOG2

        groups = []
        for g in range(n_groups):
            rows = (g * group, group)
            zeros = (jnp.zeros((group, tk), F32), jnp.zeros((group, tk), F32), jnp.zeros((group, LANES), F32))
            state = trip(pl.multiple_of(i * tq, tq), (g + 1) * group // tk, rows, zeros, -g * group)

            def earlier_trip(c, rows=rows):
                done, _, run_a, run_b, oacc = c
                s0 = pl.multiple_of((n_full - 1 - done) * width, width)
                run_a, run_b, oacc = trip(s0, nb, rows, (run_a, run_b, oacc), None)
                return done + 1, alive(run_a, run_b), run_a, run_b, oacc

            swept = lax.while_loop(lambda c: (c[0] < n_full) & c[1], earlier_trip,
                                   (jnp.int32(0), alive(state[0], state[1])) + state)
            cut_ref[n_groups * tile + g] = (n_full - swept[0]).astype(F32)
            groups.append(swept[2:])
        run_a, run_b, oacc = (jnp.concatenate([grp[j] for grp in groups], axis=0) for j in range(3))
        lane = lax.broadcasted_iota(jnp.int32, (1, LANES), 1)
        o_ref[...] = oacc
        tot_ref[...] = jnp.where(lane < GROUP, run_a, run_b)
        ms = _group_sum(oacc * oacc, gmat_ref[...]) * (1.0 / GROUP)
        yn_ref[...] = (oacc * lax.rsqrt(ms + RMS_EPS) * gain_ref[...]).astype(BF16)

        @pl.when((p == pairs - 1) & (i == pl.num_programs(1) - 1))
        def _():
            for cp in chip_hop()[0]:
                cp.wait_send()
            _finish_copies(sibling_hop)

    blk = lambda: pl.BlockSpec((tq, LANES), lambda p, i: (i, p))
    return pl.pallas_call(
        body, name="attn_fwd", grid=(pairs, s // tq),
        in_specs=[pl.BlockSpec((tq, LANES), lambda p, i: (i, p)),
                  pl.BlockSpec((s, LANES), lambda p, i: (0, pairs + p)),
                  pl.BlockSpec((s, LANES), lambda p, i: (0, 2 * pairs + p)),
                  pl.BlockSpec((1, LANES), lambda p, i: (0, p)),
                  pl.BlockSpec((2 * tk, 2 * tk), lambda p, i: (0, 0)),
                  pl.BlockSpec((2 * LANES, LANES), lambda p, i: (0, 0))] + [ANY] * n_w,
        out_specs=[blk(), blk(), blk(), pl.BlockSpec(memory_space=pltpu.SMEM)] + [ANY] * n_w,
        out_shape=[SDS((s, ATTN_WIDTH), F32), SDS((s, ATTN_WIDTH), BF16), SDS((s, ATTN_WIDTH), F32),
                   SDS((n_groups * pairs * (s // tq),), F32)]
        + [SDS(w.shape, w.dtype) for w in shards],
        input_output_aliases={6 + a: 4 + a for a in range(n_w)},
        scratch_shapes=[pltpu.SemaphoreType.DMA((6 * n_w,)), pltpu.SemaphoreType.DMA((6 * n_w,))],
        compiler_params=_params(("arbitrary", "arbitrary")))(qkv, qkv, qkv, g_attn, tri, gmat, *shards)


def _attn_bwd(qkv, o, tot, dyn, g_attn, tri, gmat, cut, partials):
    n_g = len(partials)
    s = qkv.shape[0]
    tq = _row_tile(s, ATTN_Q_TILE)
    tk = KEY_BLOCK
    nb = ATTN_KEY_BLOCKS
    width = nb * tk
    n_groups = ATTN_DIAG_GROUPS
    group = tq // n_groups
    pairs = ATTN_WIDTH // LANES

    def body(q_ref, k_ref, v_ref, o_ref, tot_ref, dyn_ref, gain_ref, tri_ref, gmat_ref, cut_ref, *rest):
        g_ins, (dq_ref, dk_out, dv_out, dg_ref) = rest[:n_g], rest[n_g:n_g + 4]
        g_outs, sems = rest[n_g + 4:2 * n_g + 4], rest[2 * n_g + 4:2 * n_g + 6]
        dk_ref, dv_ref = rest[2 * n_g + 6:]
        copies = functools.partial(_reduce_copies, g_ins, g_outs, sems)
        p, i = pl.program_id(0), pl.program_id(1)
        pl.when((p == 0) & (i == 0))(functools.partial(_start_copies, copies))

        @pl.when(i == 0)
        def _():
            dk_ref[...] = jnp.zeros_like(dk_ref)
            dv_ref[...] = jnp.zeros_like(dv_ref)
            dg_ref[...] = jnp.zeros_like(dg_ref)

        gmat_v = gmat_ref[...]
        o_v = o_ref[...]
        rstd = lax.rsqrt(_group_sum(o_v * o_v, gmat_v) * (1.0 / GROUP) + RMS_EPS)
        n = o_v * rstd
        dout = dyn_ref[...]
        dg_ref[0:1, :] += jnp.sum(dout * n, axis=0, keepdims=True)
        dn = dout * gain_ref[...]
        do2 = (rstd * (dn - n * (_group_sum(dn * n, gmat_v) * (1.0 / GROUP)))).astype(BF16)
        q2 = q_ref[...]
        tot_v = tot_ref[...]
        tots = (jnp.broadcast_to(tot_v[:, 0:1], (tq, tk)), jnp.broadcast_to(tot_v[:, GROUP:GROUP + 1], (tq, tk)))
        tri_v, tri_incl_v = tri_ref[0], tri_ref[1]
        lane = lax.broadcasted_iota(jnp.int32, (1, LANES), 1)

        def trip(s0, n_blk, rows, carry, diag_base):
            r0, nr = rows
            rest_l = [carry[0], carry[1]]
            pref_g = [carry[2], carry[3]]
            dq = carry[4]
            q_rows, do_rows = q2[r0:r0 + nr], do2[r0:r0 + nr]
            ksel = _stack_heads(k_ref[pl.ds(s0, n_blk * tk), :], n_blk)
            vsel = _stack_heads(v_ref[pl.ds(s0, n_blk * tk), :], n_blk)
            z = lax.dot_general(q_rows, ksel, NT_DIMS, preferred_element_type=F32) * (ATTN_SCALE * LOG2_E)
            log_beta, log_keep = _softplus_terms(z)
            if diag_base is not None:
                valid = _causal_mask(nr, n_blk, diag_base)
                log_keep = jnp.where(valid, log_keep, 0.0)
            ctl = jnp.dot(_stack_hilo(log_keep, 2 * n_blk), tri_incl_v, preferred_element_type=F32)
            da = lax.dot_general(do_rows, vsel, NT_DIMS, preferred_element_type=F32)
            a_parts = []
            for c in range(2 * n_blk):
                h = c % 2
                ct_c = ctl[c * nr:(c + 1) * nr]
                cols = slice(c * tk, (c + 1) * tk)
                a_parts.append(jnp.exp2(log_beta[:, cols] + (rest_l[h] - ct_c[:, :tk])))
                rest_l[h] = rest_l[h] - ct_c[:, tk:]
            a = jnp.concatenate(a_parts, axis=1)
            if diag_base is not None:
                a = jnp.where(valid, a, 0.0)
            g = a * da
            ctg = jnp.dot(_stack_hilo(g, 2 * n_blk), tri_v, preferred_element_type=F32)
            dz_parts = []
            for c in range(2 * n_blk):
                h = c % 2
                ct_c = ctg[c * nr:(c + 1) * nr]
                cols = slice(c * tk, (c + 1) * tk)
                prefix = pref_g[h] + ct_c[:, :tk]
                pref_g[h] = pref_g[h] + ct_c[:, tk:]
                g_c = g[:, cols]
                dz_parts.append(g_c - jnp.exp2(log_beta[:, cols]) * (g_c + prefix))
            dz = jnp.concatenate(dz_parts, axis=1) * ATTN_SCALE
            if diag_base is not None:
                dz = jnp.where(valid, dz, 0.0)
            dzb = dz.astype(BF16)
            dq = dq + jnp.dot(dzb, ksel, preferred_element_type=F32)
            dkt = lax.dot_general(dzb, q_rows, TN_DIMS, preferred_element_type=F32)
            dvt = lax.dot_general(a.astype(BF16), do_rows, TN_DIMS, preferred_element_type=F32)
            for blk in range(n_blk):
                ra, rb = slice(2 * blk * tk, (2 * blk + 1) * tk), slice((2 * blk + 1) * tk, (2 * blk + 2) * tk)
                keys = pl.ds(pl.multiple_of(s0 + blk * tk, tk), tk)
                dk_ref[keys, :] += jnp.where(lane < GROUP, dkt[ra], dkt[rb])
                dv_ref[keys, :] += jnp.where(lane < GROUP, dvt[ra], dvt[rb])
            return rest_l[0], rest_l[1], pref_g[0], pref_g[1], dq

        n_full = i * (tq // width)
        tile = p * pl.num_programs(1) + i
        dq_groups = []
        for g in range(n_groups):
            rows = (g * group, group)
            first = jnp.clip(cut_ref[n_groups * tile + g].astype(jnp.int32), 0, n_full)
            zeros_qk = jnp.zeros((group, tk), F32)
            carry = (tots[0][g * group:(g + 1) * group], tots[1][g * group:(g + 1) * group], zeros_qk, zeros_qk,
                     jnp.zeros((group, LANES), F32))
            carry = lax.fori_loop(
                first, n_full,
                lambda t, c, rows=rows: trip(pl.multiple_of(t * width, width), nb, rows, c, None), carry)
            dq_groups.append(trip(pl.multiple_of(i * tq, tq), (g + 1) * group // tk, rows, carry, -g * group)[4])
        dq_ref[...] = jnp.concatenate(dq_groups, axis=0).astype(BF16)

        @pl.when(i == pl.num_programs(1) - 1)
        def _():
            dk_out[...] = dk_ref[...].astype(BF16)
            dv_out[...] = dv_ref[...].astype(BF16)

        pl.when((p == pairs - 1) & (i == pl.num_programs(1) - 1))(functools.partial(_finish_copies, copies))

    blk = lambda: pl.BlockSpec((tq, LANES), lambda p, i: (i, p))
    col = lambda: pl.BlockSpec((s, LANES), lambda p, i: (0, p))
    n_peers = len(DEVICE_FLIPS)
    return pl.pallas_call(
        body, name="attn_bwd", grid=(pairs, s // tq),
        in_specs=[pl.BlockSpec((tq, LANES), lambda p, i: (i, p)),
                  pl.BlockSpec((s, LANES), lambda p, i: (0, pairs + p)),
                  pl.BlockSpec((s, LANES), lambda p, i: (0, 2 * pairs + p)),
                  blk(), blk(), blk(),
                  pl.BlockSpec((1, LANES), lambda p, i: (0, p)),
                  pl.BlockSpec((2, 2 * tk, 2 * tk), lambda p, i: (0, 0, 0)),
                  pl.BlockSpec((2 * LANES, LANES), lambda p, i: (0, 0)),
                  pl.BlockSpec(memory_space=pltpu.SMEM)] + [ANY] * n_g,
        out_specs=[blk(), col(), col(), pl.BlockSpec((SUBLANES, LANES), lambda p, i: (0, p))] + [ANY] * n_g,
        out_shape=[SDS((s, ATTN_WIDTH), BF16), SDS((s, ATTN_WIDTH), BF16), SDS((s, ATTN_WIDTH), BF16),
                   SDS((SUBLANES, ATTN_WIDTH), F32)]
        + [SDS((n_peers, g.shape[1] // 2, g.shape[2]), g.dtype) for g in partials],
        scratch_shapes=[pltpu.SemaphoreType.DMA((n_peers * n_g,)), pltpu.SemaphoreType.DMA((n_peers * n_g,)),
                        pltpu.VMEM((s, LANES), F32), pltpu.VMEM((s, LANES), F32)],
        compiler_params=_params(("arbitrary", "arbitrary")))(
            qkv, qkv, qkv, o, tot, dyn, g_attn, tri, gmat, cut, *partials)


def _mix_ln1(ycn, yan, w_out, x, g, b):
    s = x.shape[0]
    tm = _row_tile(s, 512)

    def body(yc_ref, ya_ref, w_ref, x_ref, g_ref, b_ref, x1_ref, xhat_ref, rstd_ref, x1b_ref):
        mix = jnp.dot(yc_ref[...], w_ref[0:CONV_WIDTH, :], preferred_element_type=F32)
        mix = mix + jnp.dot(ya_ref[...], w_ref[CONV_WIDTH:, :], preferred_element_type=F32)
        x1, xhat, rstd = _layer_norm_fwd(ALPHA * x_ref[...] + mix, g_ref[...], b_ref[...])
        x1_ref[...] = x1
        xhat_ref[...] = xhat
        rstd_ref[...] = rstd
        x1b_ref[...] = x1.astype(BF16)

    row = lambda w: pl.BlockSpec((tm, w), lambda i: (i, 0))
    vec = lambda: pl.BlockSpec((1, D_MODEL), lambda i: (0, 0))
    return pl.pallas_call(
        body, name="mix_ln1", grid=(s // tm,),
        in_specs=[row(CONV_WIDTH), row(ATTN_WIDTH), pl.BlockSpec((D_MODEL, D_MODEL), lambda i: (0, 0)),
                  row(D_MODEL), vec(), vec()],
        out_specs=[row(D_MODEL), row(D_MODEL), row(1), row(D_MODEL)],
        out_shape=[SDS((s, D_MODEL), F32), SDS((s, D_MODEL), F32), SDS((s, 1), F32), SDS((s, D_MODEL), BF16)],
        compiler_params=_params(("parallel",)))(ycn, yan, w_out, x, g, b)


def _mlp_fwd_loss(x1, w_up, w_down, target, g, b):
    s = x1.shape[0]
    tm = _row_tile(s, 256)

    def body(x1_ref, wu_ref, wd_ref, t_ref, g_ref, b_ref, dpre_ref, sums_ref, loss_ref, r_ref, hid_ref, dpreb_ref):
        i = pl.program_id(0)
        x1_v = x1_ref[...]
        xb = x1_v.astype(BF16)
        ffn = jnp.zeros((tm, D_MODEL), F32)
        for k in range(N_CHIPS):
            r = jnp.maximum(jnp.dot(xb, wu_ref[k], preferred_element_type=F32), 0.0)
            hid = (r * r).astype(BF16)
            r_ref[:, FF_SHARD * k:FF_SHARD * (k + 1)] = r.astype(BF16)
            hid_ref[:, FF_SHARD * k:FF_SHARD * (k + 1)] = hid
            ffn = ffn + jnp.dot(hid, wd_ref[k], preferred_element_type=F32)
        g_v = g_ref[...]
        x2, xhat, rstd = _layer_norm_fwd(ALPHA * x1_v + ffn, g_v, b_ref[...])
        err = x2 - t_ref[...]
        dx2 = err * (1.0 / D_MODEL)
        dpre = _layer_norm_bwd(dx2, xhat, rstd, g_v)
        dpre_ref[...] = dpre
        dpreb_ref[...] = dpre.astype(BF16)

        @pl.when(i == 0)
        def _():
            sums_ref[...] = jnp.zeros_like(sums_ref)
            loss_ref[...] = jnp.zeros_like(loss_ref)

        sums_ref[0:1, :] += jnp.sum(dx2 * xhat, axis=0, keepdims=True)
        sums_ref[1:2, :] += jnp.sum(dx2, axis=0, keepdims=True)
        loss_ref[...] += jnp.sum(jnp.sum(err * err, axis=1, keepdims=True), axis=0, keepdims=True) * (0.5 / D_MODEL)

    row = lambda: pl.BlockSpec((tm, D_MODEL), lambda i: (i, 0))
    wide = lambda: pl.BlockSpec((tm, D_FF), lambda i: (i, 0))
    vec = lambda: pl.BlockSpec((1, D_MODEL), lambda i: (0, 0))
    return pl.pallas_call(
        body, name="mlp_fwd_loss", grid=(s // tm,),
        in_specs=[row(), _resident_weight(), _resident_weight(), row(), vec(), vec()],
        out_specs=[row(), pl.BlockSpec((SUBLANES, D_MODEL), lambda i: (0, 0)),
                   pl.BlockSpec((SUBLANES, LANES), lambda i: (0, 0)), wide(), wide(), row()],
        out_shape=[SDS((s, D_MODEL), F32), SDS((SUBLANES, D_MODEL), F32), SDS((SUBLANES, LANES), F32),
                   SDS((s, D_FF), BF16), SDS((s, D_FF), BF16), SDS((s, D_MODEL), BF16)],
        compiler_params=_params(("arbitrary",)))(x1, w_up, w_down, target, g, b)


def _resident_weight():
    return pl.BlockSpec((N_CHIPS, D_MODEL, FF_SHARD), lambda i: (0, 0, 0), pipeline_mode=pl.Buffered(1))


def _mlp_bwd_ln1(relu_up, dpre2, w_up, w_down, xhat1, rstd1, g1, w_out):
    s = dpre2.shape[0]
    tm = _row_tile(s, 256)

    def body(r_ref, d2_ref, wu_ref, wd_ref, xh_ref, rs_ref, g_ref, wo_ref,
             dup_ref, dpre_ref, sums_ref, dpreb_ref, dyc_ref, dya_ref):
        i = pl.program_id(0)
        d2 = d2_ref[...]
        d2b = d2.astype(BF16)
        dx1 = ALPHA * d2
        for k in range(N_CHIPS):
            r = r_ref[:, FF_SHARD * k:FF_SHARD * (k + 1)].astype(F32)
            dhid = lax.dot_general(d2b, wd_ref[k], NT_DIMS, preferred_element_type=F32)
            dupb = (dhid * (2.0 * r)).astype(BF16)
            dup_ref[:, FF_SHARD * k:FF_SHARD * (k + 1)] = dupb
            dx1 = dx1 + lax.dot_general(dupb, wu_ref[k], NT_DIMS, preferred_element_type=F32)
        xhat = xh_ref[...]
        dpre = _layer_norm_bwd(dx1, xhat, rs_ref[...], g_ref[...])
        dpre_ref[...] = dpre
        dpb = dpre.astype(BF16)
        dpreb_ref[...] = dpb
        dyc_ref[...] = lax.dot_general(dpb, wo_ref[0:CONV_WIDTH, :], NT_DIMS, preferred_element_type=F32)
        dya_ref[...] = lax.dot_general(dpb, wo_ref[CONV_WIDTH:, :], NT_DIMS, preferred_element_type=F32)

        @pl.when(i == 0)
        def _():
            sums_ref[...] = jnp.zeros_like(sums_ref)

        sums_ref[0:1, :] += jnp.sum(dx1 * xhat, axis=0, keepdims=True)
        sums_ref[1:2, :] += jnp.sum(dx1, axis=0, keepdims=True)

    row = lambda w: pl.BlockSpec((tm, w), lambda i: (i, 0))
    return pl.pallas_call(
        body, name="mlp_bwd_ln1", grid=(s // tm,),
        in_specs=[row(D_FF), row(D_MODEL), _resident_weight(), _resident_weight(), row(D_MODEL), row(1),
                  pl.BlockSpec((1, D_MODEL), lambda i: (0, 0)),
                  pl.BlockSpec((D_MODEL, D_MODEL), lambda i: (0, 0), pipeline_mode=pl.Buffered(1))],
        out_specs=[row(D_FF), row(D_MODEL), pl.BlockSpec((SUBLANES, D_MODEL), lambda i: (0, 0)), row(D_MODEL),
                   row(CONV_WIDTH), row(ATTN_WIDTH)],
        out_shape=[SDS((s, D_FF), BF16), SDS((s, D_MODEL), F32), SDS((SUBLANES, D_MODEL), F32),
                   SDS((s, D_MODEL), BF16), SDS((s, CONV_WIDTH), F32), SDS((s, ATTN_WIDTH), F32)],
        compiler_params=_params(("arbitrary",)))(relu_up, dpre2, w_up, w_down, xhat1, rstd1, g1, w_out)


def _grad_tn(a, b, name, out_cols, stacked):
    s, ka = a.shape
    n = b.shape[1]
    ts = _row_tile(s, 2048)
    n_steps = s // ts
    if stacked:
        tka, tn = ka, out_cols
        grid = (1, n // tn, n_steps)
        shape = (n // tn, ka, tn)
        out_spec = lambda: pl.BlockSpec((None, tka, tn), lambda r, c, t: (c, 0, 0))
    else:
        tka, tn = min(ka, 1024), n
        grid = (ka // tka, 1, n_steps)
        shape = (ka, n)
        out_spec = lambda: pl.BlockSpec((tka, tn), lambda r, c, t: (r, 0))

    def body(a_ref, b_ref, o_ref, ob_ref):
        t = pl.program_id(2)

        @pl.when(t == 0)
        def _():
            o_ref[...] = jnp.zeros_like(o_ref)

        o_ref[...] += lax.dot_general(a_ref[...].astype(BF16), b_ref[...].astype(BF16), TN_DIMS,
                                      preferred_element_type=F32)

        @pl.when(t == n_steps - 1)
        def _():
            ob_ref[...] = o_ref[...].astype(BF16)

    return pl.pallas_call(
        body, name=name, grid=grid,
        in_specs=[pl.BlockSpec((ts, tka), lambda r, c, t: (t, r)),
                  pl.BlockSpec((ts, tn), lambda r, c, t: (t, c))],
        out_specs=[out_spec(), out_spec()], out_shape=[SDS(shape, F32), SDS(shape, BF16)],
        compiler_params=_params(("parallel", "parallel", "arbitrary")))(a, b)


def _grad_w_out(ycn, yan, dpre1):
    s = dpre1.shape[0]
    ts = _row_tile(s, 2048)
    n_steps = s // ts

    def body(yc_ref, ya_ref, d_ref, o_ref, ob_ref):
        half, t = pl.program_id(0), pl.program_id(1)

        @pl.when(t == 0)
        def _():
            o_ref[...] = jnp.zeros_like(o_ref)

        db = d_ref[...]

        @pl.when(half == 0)
        def _():
            o_ref[...] += lax.dot_general(yc_ref[...], db, TN_DIMS, preferred_element_type=F32)

        @pl.when(half == 1)
        def _():
            o_ref[...] += lax.dot_general(ya_ref[...], db, TN_DIMS, preferred_element_type=F32)

        @pl.when(t == n_steps - 1)
        def _():
            ob_ref[...] = o_ref[...].astype(BF16)

    out_spec = lambda: pl.BlockSpec((CONV_WIDTH, D_MODEL), lambda r, t: (r, 0))
    return pl.pallas_call(
        body, name="grad_w_out", grid=(2, n_steps),
        in_specs=[pl.BlockSpec((ts, CONV_WIDTH), lambda r, t: (t, 0)),
                  pl.BlockSpec((ts, ATTN_WIDTH), lambda r, t: (t, 0)),
                  pl.BlockSpec((ts, D_MODEL), lambda r, t: (t, 0))],
        out_specs=[out_spec(), out_spec()],
        out_shape=[SDS((D_MODEL, D_MODEL), F32), SDS((D_MODEL, D_MODEL), BF16)],
        compiler_params=_params(("parallel", "arbitrary")))(ycn, yan, dpre1)


def _sum_with_peers(own_ref, r_ref, o_ref):
    acc = own_ref[...]
    for f in range(r_ref.shape[0]):
        acc = acc + r_ref[f].astype(F32)
    o_ref[...] = acc


def _grad_x(kc_idx, dproj, w_in, dpre1, chip_sums, earlier):
    s = dproj.shape[0]
    tm = _row_tile(s, 512)
    steps = s // tm
    n_peers = len(DEVICE_FLIPS)
    n_chips = len(CHIP_FLIPS)
    n_e = len(earlier)

    def body(kc_ref, dp_ref, w_ref, d1_ref, *rest):
        sum_ins, g_in = rest[:2 * n_e], rest[2 * n_e]
        o_ref, g_out = rest[2 * n_e + 1], rest[2 * n_e + 2]
        sum_outs, sems = rest[2 * n_e + 3:3 * n_e + 3], rest[3 * n_e + 3:]
        copies = functools.partial(_chip_reduce_copies, g_in, g_out, sems)
        i = pl.program_id(0)
        pl.when(i == 0)(functools.partial(_start_copies, copies))
        acc = ALPHA * d1_ref[...]
        for k in range(N_CHIPS):
            acc = acc + lax.dot_general(dp_ref[:, IN_SHARD * k:IN_SHARD * (k + 1)], w_ref[k], NT_DIMS,
                                        preferred_element_type=F32)
        o_ref[...] = acc
        for a in range(n_e):
            _sum_with_peers(sum_ins[2 * a], sum_ins[2 * a + 1], sum_outs[a])
        pl.when(i == steps - 1)(functools.partial(_finish_copies, copies))

    in_specs = [pl.BlockSpec((tm, IN_COLS), lambda i, kc: (i, 0)),
                pl.BlockSpec((N_CHIPS, D_MODEL, IN_SHARD), lambda i, kc: (0, 0, 0)),
                pl.BlockSpec((tm, D_MODEL), lambda i, kc: (i, 0))]
    out_specs = [pl.BlockSpec((tm, D_MODEL), lambda i, kc: (i, 0)), ANY]
    out_shape = [SDS((s, D_MODEL), F32), SDS((n_chips,) + chip_sums.shape[1:], chip_sums.dtype)]
    operands = []
    for own, recv in earlier:
        _, _, h, cols = own.shape
        th = h // steps
        in_specs.append(pl.BlockSpec((None, None, th, cols), lambda i, kc: (kc[0], kc[1], i, 0)))
        in_specs.append(pl.BlockSpec((n_peers, th, cols), lambda i, kc: (0, i, 0)))
        out_specs.append(pl.BlockSpec((th, cols), lambda i, kc: (kc[1] * steps + i, 0)))
        out_shape.append(SDS((2 * h, cols), F32))
        operands += [own, recv]
    grid_spec = pltpu.PrefetchScalarGridSpec(
        num_scalar_prefetch=1, grid=(steps,), in_specs=in_specs + [ANY], out_specs=out_specs,
        scratch_shapes=[pltpu.SemaphoreType.DMA((n_chips,)), pltpu.SemaphoreType.DMA((n_chips,))])
    return pl.pallas_call(
        body, name="grad_x", grid_spec=grid_spec, out_shape=out_shape,
        compiler_params=_params(("arbitrary",)))(kc_idx, dproj, w_in, dpre1, *operands, chip_sums)


def _adamw_step(w, g, m, v):
    nm = ADAM_B1 * m + (1.0 - ADAM_B1) * g
    nv = ADAM_B2 * v + (1.0 - ADAM_B2) * (g * g)
    m_hat = nm / (1.0 - ADAM_B1 ** ADAM_STEP)
    v_hat = nv / (1.0 - ADAM_B2 ** ADAM_STEP)
    return -ADAM_LR * (m_hat / (jnp.sqrt(v_hat) + ADAM_EPS) + ADAM_WD * w), nm, nv


def _adamw(w, g, m, v, name):
    r, c = w.shape
    tr = _row_tile(r, 256)

    def body(w_ref, g_ref, m_ref, v_ref, go_ref, d_ref, nm_ref, nv_ref):
        g_v = g_ref[...]
        go_ref[...] = g_v
        d_ref[...], nm_ref[...], nv_ref[...] = _adamw_step(w_ref[...], g_v, m_ref[...], v_ref[...])

    spec = lambda: pl.BlockSpec((tr, c), lambda i: (i, 0))
    return pl.pallas_call(
        body, name=name, grid=(r // tr,),
        in_specs=[spec(), spec(), spec(), spec()], out_specs=[spec(), spec(), spec(), spec()],
        out_shape=[SDS((r, c), F32)] * 4, compiler_params=_params(("parallel",)))(w, g, m, v)


def _adamw_small(total, conv_grad, weights, moments, variances):
    n = len(weights)
    starts = (ROW_GCONV, ROW_GATTN, ROW_LN1G, ROW_LN1B, ROW_LN2G, ROW_LN2B)

    def body(total_ref, cg_ref, *refs):
        w, m, v, outs = refs[:n], refs[n:2 * n], refs[2 * n:3 * n], refs[3 * n:]
        for p in range(n):
            rows = w[p].shape[0]
            g = cg_ref[...] if p == n - 1 else total_ref[starts[p]:starts[p] + rows, :]
            outs[4 * p][...] = g
            outs[4 * p + 1][...], outs[4 * p + 2][...], outs[4 * p + 3][...] = _adamw_step(
                w[p][...], g, m[p][...], v[p][...])

    vmem = pl.BlockSpec(memory_space=pltpu.VMEM)
    out_shape = [SDS(w.shape, F32) for w in weights for _ in range(4)]
    flat = pl.pallas_call(
        body, name="adamw_small", in_specs=[vmem] * (2 + 3 * n), out_specs=[vmem] * (4 * n),
        out_shape=out_shape)(total, conv_grad, *weights, *moments, *variances)
    return [flat[4 * p:4 * p + 4] for p in range(n)]


def _sum_partials(kc_idx, own, recv, name):
    h, cols = own.shape
    th = _row_tile(h, 128)
    n_peers = recv.shape[0]

    def body(kc_ref, own_ref, r_ref, o_ref):
        _sum_with_peers(own_ref, r_ref, o_ref)

    grid_spec = pltpu.PrefetchScalarGridSpec(
        num_scalar_prefetch=1, grid=(h // th,),
        in_specs=[pl.BlockSpec((th, cols), lambda t, kc: (t, 0)),
                  pl.BlockSpec((n_peers, th, cols), lambda t, kc: (0, t, 0))],
        out_specs=pl.BlockSpec((th, cols), lambda t, kc: (kc[1] * (h // th) + t, 0)))
    return pl.pallas_call(
        body, name=name, grid_spec=grid_spec, out_shape=SDS((2 * h, cols), F32),
        compiler_params=_params(("parallel",)))(kc_idx, own, recv)


def _add_sibling(kc_idx, grad, recv):
    _, _, h, cols = grad.shape
    th = _row_tile(h, 512)

    def body(kc_ref, g_ref, r_ref, sums_ref, own_ref):
        total = g_ref[...] + r_ref[...].astype(F32)
        sums_ref[...] = total.astype(BF16)

        @pl.when(pl.program_id(1) == kc_ref[0])
        def _():
            own_ref[...] = total

    grid_spec = pltpu.PrefetchScalarGridSpec(
        num_scalar_prefetch=1, grid=(h // th, N_CHIPS),
        in_specs=[pl.BlockSpec((None, None, th, cols), lambda t, k, kc: (k, kc[1], t, 0)),
                  pl.BlockSpec((None, th, cols), lambda t, k, kc: (k, t, 0))],
        out_specs=[pl.BlockSpec((None, th, cols), lambda t, k, kc: (k, t, 0)),
                   pl.BlockSpec((th, cols), lambda t, k, kc: (t, 0))])
    return pl.pallas_call(
        body, name="add_sibling_w_in", grid_spec=grid_spec,
        out_shape=[SDS((N_CHIPS, h, cols), BF16), SDS((h, cols), F32)],
        compiler_params=_params(("parallel", "arbitrary")))(kc_idx, grad, recv)


def _gather_weights(kc_idx, w_in_slots, conv_slots, later):
    n_l = len(later)
    steps = SUBLANES

    def body(kc_ref, *refs):
        cast_ins, cast_outs = refs[:n_l], refs[n_l + 2:2 * n_l + 2]
        w_buf, conv_buf = refs[2 * n_l + 2], refs[2 * n_l + 3]
        sems = refs[2 * n_l + 4:]
        i = pl.program_id(0)

        def first_hop():
            x, y, c = _position()
            sends, arrivals = _gather_chip_hop([w_buf], sems)
            mine = conv_buf.at[2 * x + y]
            for j, (fx, fy) in enumerate(CHIP_FLIPS):
                tx, ty = _flip(x, fx), _flip(y, fy)
                there = conv_buf.at[2 * tx + ty]
                sends.append(_remote_copy(mine, mine, sems, 6 + j, (tx, ty, c)))
                arrivals.append(_remote_copy(there, there, sems, 6 + j, (tx, ty, c)))
            return sends, arrivals

        pl.when(i == 0)(functools.partial(_start_copies, first_hop))
        for src, dst in zip(cast_ins, cast_outs):
            dst[...] = src[...].astype(BF16)

        @pl.when(i == steps - 1)
        def _():
            sends, arrivals = first_hop()
            for cp in arrivals:
                cp.wait_recv()
            _start_copies(functools.partial(_gather_sibling_hop, [w_buf], sems))
            _finish_copies(functools.partial(_gather_sibling_hop, [w_buf], sems))
            for cp in sends:
                cp.wait_send()

    in_specs, out_specs, out_shape = [], [], []
    for w in later:
        r, c = w.shape
        in_specs.append(pl.BlockSpec((r // steps, c), lambda i, kc: (i, 0)))
        out_specs.append(pl.BlockSpec((None, r // steps, c), lambda i, kc: (kc[0], i, 0)))
        out_shape.append(SDS((N_CHIPS, r, c), BF16))
    grid_spec = pltpu.PrefetchScalarGridSpec(
        num_scalar_prefetch=1, grid=(steps,), in_specs=in_specs + [ANY, ANY], out_specs=out_specs + [ANY, ANY],
        scratch_shapes=[pltpu.SemaphoreType.DMA((9,)), pltpu.SemaphoreType.DMA((9,))])
    return pl.pallas_call(
        body, name="gather_weights", grid_spec=grid_spec,
        out_shape=out_shape + [SDS(w_in_slots.shape, w_in_slots.dtype), SDS(conv_slots.shape, conv_slots.dtype)],
        input_output_aliases={n_l + 1: n_l, n_l + 2: n_l + 1},
        compiler_params=_params(("arbitrary",)))(kc_idx, *later, w_in_slots, conv_slots)


def _exchange_with_sibling(partial):
    h = partial.shape[1] // 2

    def body(g_in, g_out, send_sems, recv_sems):
        x, y, c = _position()
        theirs = pl.ds(pl.multiple_of((1 - c) * h, h), h)
        copies = [_remote_copy(g_in.at[k, theirs], g_out.at[k], (send_sems, recv_sems), k, (x, y, 1 - c))
                  for k in range(N_CHIPS)]
        for cp in copies:
            cp.start()
        for cp in copies:
            cp.wait_recv()
        for cp in copies:
            cp.wait_send()

    return pl.pallas_call(
        body, name="exchange_with_sibling", in_specs=[ANY], out_specs=ANY,
        out_shape=SDS((N_CHIPS, h, partial.shape[2]), partial.dtype),
        scratch_shapes=[pltpu.SemaphoreType.DMA((N_CHIPS,)), pltpu.SemaphoreType.DMA((N_CHIPS,))])(partial)


def _finish_exchange(pieces, vec):
    n = len(pieces)
    n_dev = 2 * N_CHIPS

    def body(*refs):
        v_ref = refs[n]
        outs, o_ref = refs[n + 1:2 * n + 1], refs[2 * n + 1]
        buf, send_sems, recv_sems = refs[2 * n + 2:]
        x, y, c = _position()
        sibling = (x, y, 1 - c)
        me = 4 * x + 2 * y + c
        buf[me] = v_ref[...]
        started = []
        for f, (fx, fy, fc) in enumerate(DEVICE_FLIPS):
            cp = pltpu.make_async_remote_copy(
                src_ref=v_ref, dst_ref=buf.at[me], send_sem=send_sems.at[n + f], recv_sem=recv_sems.at[n + f],
                device_id=(_flip(x, fx), _flip(y, fy), _flip(c, fc)), device_id_type=MESH)
            cp.start()
            started.append(cp)
        for a in range(n):
            h = pieces[a].shape[0] // 2
            mine = outs[a].at[pl.ds(pl.multiple_of(c * h, h), h)]
            cp = pltpu.make_async_remote_copy(
                src_ref=mine, dst_ref=mine, send_sem=send_sems.at[a], recv_sem=recv_sems.at[a],
                device_id=sibling, device_id_type=MESH)
            cp.start()
            started.append(cp)
        for a in range(n):
            h = pieces[a].shape[0] // 2
            theirs = outs[a].at[pl.ds(pl.multiple_of((1 - c) * h, h), h)]
            pltpu.make_async_remote_copy(
                src_ref=theirs, dst_ref=theirs, send_sem=send_sems.at[a], recv_sem=recv_sems.at[a],
                device_id=sibling, device_id_type=MESH).wait_recv()
        for f, (fx, fy, fc) in enumerate(DEVICE_FLIPS):
            src = 4 * _flip(x, fx) + 2 * _flip(y, fy) + _flip(c, fc)
            pltpu.make_async_remote_copy(
                src_ref=v_ref, dst_ref=buf.at[src], send_sem=send_sems.at[n + f], recv_sem=recv_sems.at[n + f],
                device_id=(x, y, c), device_id_type=MESH).wait_recv()
        for cp in started:
            cp.wait_send()
        acc = buf[0]
        for d in range(1, n_dev):
            acc = acc + buf[d]
        o_ref[...] = acc

    vmem = pl.BlockSpec(memory_space=pltpu.VMEM)
    out_shape = [SDS(p.shape, p.dtype) for p in pieces] + [SDS(vec.shape, vec.dtype)]
    n_sems = n + n_dev - 1
    return pl.pallas_call(
        body, name="finish_exchange", in_specs=[ANY] * n + [vmem], out_specs=[ANY] * n + [vmem],
        out_shape=out_shape, input_output_aliases={a: a for a in range(n)},
        scratch_shapes=[pltpu.VMEM((n_dev,) + vec.shape, vec.dtype), pltpu.SemaphoreType.DMA((n_sems,)),
                        pltpu.SemaphoreType.DMA((n_sems,))])(*pieces, vec)


def _constants():
    r = jnp.arange(2 * KEY_BLOCK)[:, None] % KEY_BLOCK
    c = jnp.arange(2 * KEY_BLOCK)[None, :]
    later = jnp.where(c < KEY_BLOCK, r > c, True).astype(BF16)
    earlier = jnp.where(c < KEY_BLOCK, r < c, True).astype(BF16)
    upto = jnp.where(c < KEY_BLOCK, r <= c, True).astype(BF16)
    gr = (jnp.arange(2 * LANES)[:, None] % LANES) // GROUP
    gc = jnp.arange(LANES)[None, :] // GROUP
    gmat = (gr == gc).astype(BF16)
    return later, jnp.stack([earlier, upto]), gmat


def _rows(v):
    return v.reshape(-1, LANES)


def kernel(x, w_in, conv_w, g_conv, g_attn, w_out, ln1_g, ln1_b, w_up, w_down, ln2_g, ln2_b, loss_target, m_w_in, m_conv_w, m_g_conv, m_g_attn, m_w_out, m_ln1_g, m_ln1_b, m_w_up, m_w_down, m_ln2_g, m_ln2_b, v_w_in, v_conv_w, v_g_conv, v_g_attn, v_w_out, v_ln1_g, v_ln1_b, v_w_up, v_w_down, v_ln2_g, v_ln2_b):
    xs, target = x[0], loss_target[0]
    mesh_x, mesh_y, mesh_c = _position()
    k_idx = 2 * mesh_x + mesh_y
    kc_idx = jnp.stack([k_idx, mesh_c]).astype(jnp.int32)
    tri_later, tri_earlier, gmat = _constants()

    w_in_b = _cast_into_slot(kc_idx, w_in[0], "cast_w_in")
    conv_slot = jnp.pad(conv_w, ((0, 0), (0, SUBLANES - conv_w.shape[1]), (0, 0)))
    conv_b = lax.dynamic_update_slice(jnp.zeros((N_CHIPS, SUBLANES, LANES), F32), conv_slot, (k_idx, 0, 0))
    w_out_b, w_up_b, w_down_b, w_in_f, conv_f = _gather_weights(
        kc_idx, w_in_b, conv_b, [w_out[0], w_up[0], w_down[0]])
    taps = jnp.transpose(conv_f, (1, 0, 2)).reshape(SUBLANES, CONV_WIDTH)

    gates, qkv, xs_b = _proj(xs, w_in_f)
    ycn = _conv_fwd(gates, taps, g_conv, gmat)
    o, yan, tot, cut, w_out_f, w_up_f, w_down_f = _attn_fwd(
        qkv, g_attn, tri_later, gmat, [w_out_b, w_up_b, w_down_b])
    w_out_f = w_out_f.reshape(D_MODEL, D_MODEL)
    x1, xhat1, rstd1, x1_b = _mix_ln1(ycn, yan, w_out_f, xs, ln1_g, ln1_b)
    dpre2, ln2_sums, loss_sum, relu_up, hid, dpre2_b = _mlp_fwd_loss(x1, w_up_f, w_down_f, target, ln2_g, ln2_b)

    dup, dpre1, ln1_sums, dpre1_b, dycn, dyan = _mlp_bwd_ln1(
        relu_up, dpre2, w_up_f, w_down_f, xhat1, rstd1, ln1_g, w_out_f)
    gw_up = _grad_tn(x1_b, dup, "grad_w_up", FF_SHARD, True)
    gw_down = [g.reshape(N_CHIPS, FF_SHARD, D_MODEL) for g in _grad_tn(hid, dpre2_b, "grad_w_down", D_MODEL, False)]
    gw_out = [g.reshape(N_CHIPS, D_MODEL // N_CHIPS, D_MODEL) for g in _grad_w_out(ycn, yan, dpre1_b)]
    dq, dk, dv, gattn_sums, recv_out, recv_up, recv_down = _attn_bwd(
        qkv, o, tot, dyan, g_attn, tri_earlier, gmat, cut, [gw_out[1], gw_up[1], gw_down[1]])
    dbg, dy, conv_sums = _conv_bwd_gate(gates, dycn, taps, g_conv, gmat)
    dproj = _dproj_assemble(gates, dy, dbg, dq, dk, dv, taps)
    gw_in = _grad_tn(xs_b, dproj, "grad_w_in", IN_SHARD, True)
    halves = lambda g: g.reshape(N_CHIPS, 2, g.shape[1] // 2, g.shape[2])
    chip_sums, own_sum = _add_sibling(kc_idx, halves(gw_in[0]), _exchange_with_sibling(gw_in[1]))
    grad_x, recv_in, p_out, p_up, p_down = _grad_x(
        kc_idx, dproj, w_in_f, dpre1, chip_sums,
        [(halves(gw_out[0]), recv_out), (halves(gw_up[0]), recv_up), (halves(gw_down[0]), recv_down)])
    pieces = [_sum_partials(kc_idx, own_sum, recv_in, "sum_partials_w_in"), p_out, p_up, p_down]
    conv_rows = jnp.transpose(conv_sums[0:3].reshape(3, N_CHIPS, LANES), (1, 0, 2)).reshape(3 * N_CHIPS, LANES)
    small = jnp.concatenate([
        loss_sum, _rows(conv_sums[3]), _rows(gattn_sums[0]), _rows(ln1_sums[0]), _rows(ln1_sums[1]),
        _rows(ln2_sums[0]), _rows(ln2_sums[1]), conv_rows,
        jnp.zeros((SMALL_ROWS - ROW_CONVW - 3 * N_CHIPS, LANES), F32)], axis=0)
    g_w_in, g_w_out, g_w_up, g_w_down, total = _finish_exchange(pieces, small)
    loss = total[ROW_LOSS, 0]
    g_conv_w = lax.dynamic_slice(total, (ROW_CONVW + 3 * k_idx, 0), (3, LANES))

    small_names = ["g_conv", "g_attn", "ln1_g", "ln1_b", "ln2_g", "ln2_b", "conv_w"]
    small_w = [g_conv, g_attn, ln1_g, ln1_b, ln2_g, ln2_b, conv_w]
    small_m = [m_g_conv, m_g_attn, m_ln1_g, m_ln1_b, m_ln2_g, m_ln2_b, m_conv_w]
    small_v = [v_g_conv, v_g_attn, v_ln1_g, v_ln1_b, v_ln2_g, v_ln2_b, v_conv_w]
    small_out = dict(zip(small_names, _adamw_small(
        total, g_conv_w, [_rows(a) for a in small_w], [_rows(a) for a in small_m], [_rows(a) for a in small_v])))
    small_shape = dict(zip(small_names, (a.shape for a in small_w)))
    big_out = {
        "w_in": _adamw(w_in[0], g_w_in, m_w_in[0], v_w_in[0], "adamw_w_in"),
        "w_out": _adamw(w_out[0], g_w_out, m_w_out[0], v_w_out[0], "adamw_w_out"),
        "w_up": _adamw(w_up[0], g_w_up, m_w_up[0], v_w_up[0], "adamw_w_up"),
        "w_down": _adamw(w_down[0], g_w_down, m_w_down[0], v_w_down[0], "adamw_w_down"),
    }
    order = ["w_in", "conv_w", "g_conv", "g_attn", "w_out", "ln1_g", "ln1_b", "w_up", "w_down", "ln2_g", "ln2_b"]

    def leaf(kind, name):
        if name in big_out:
            return big_out[name][kind][None]
        return small_out[name][kind].reshape(small_shape[name])

    outs = [loss, grad_x[None]]
    for kind in range(4):
        outs.extend(leaf(kind, name) for name in order)
    return tuple(outs)
```

```python
import functools

import jax
import jax.numpy as jnp
from jax import lax
from jax.experimental import pallas as pl
from jax.experimental.pallas import tpu as pltpu

F32 = jnp.float32
BF16 = jnp.bfloat16
SDS = jax.ShapeDtypeStruct

D_MODEL = 1024
CONV_WIDTH = 512
ATTN_WIDTH = 512
GROUP = 64
GATE_COLS = 3 * CONV_WIDTH
QKV_COLS = 3 * ATTN_WIDTH
IN_COLS = GATE_COLS + QKV_COLS
D_FF = 4 * D_MODEL
N_CHIPS = 4
IN_SHARD = IN_COLS // N_CHIPS
FF_SHARD = D_FF // N_CHIPS
ALPHA = float(2.0 ** 0.25)
LN_EPS = 1e-5
RMS_EPS = 1e-6
ATTN_SCALE = GROUP ** -0.5
LOG2_E = 1.4426950408889634
ADAM_LR = 0.001
ADAM_B1 = 0.9
ADAM_B2 = 0.999
ADAM_EPS = 1e-08
ADAM_WD = 0.01
ADAM_STEP = 10

LANES = 128
SUBLANES = 8
KEY_BLOCK = 128
ATTN_Q_TILE = 512
ATTN_KEY_BLOCKS = 2
ATTN_DIAG_GROUPS = 2
ATTN_DEAD_LOG2 = 200.0
VMEM_LIMIT = 56 * 1024 * 1024

MESH = pl.DeviceIdType.MESH
CHIP_FLIPS = ((1, 0), (0, 1), (1, 1))
DEVICE_FLIPS = tuple((fx, fy, fc) for fx in (0, 1) for fy in (0, 1) for fc in (0, 1))[1:]
NT_DIMS = (((1,), (1,)), ((), ()))
TN_DIMS = (((0,), (0,)), ((), ()))

ROW_LOSS = 0
ROW_GCONV = 8
ROW_GATTN = 12
ROW_LN1G = 16
ROW_LN1B = 24
ROW_LN2G = 32
ROW_LN2B = 40
ROW_CONVW = 48
SMALL_ROWS = 64


def _params(sem=None):
    return pltpu.CompilerParams(dimension_semantics=sem, vmem_limit_bytes=VMEM_LIMIT)


def _flip(v, f):
    return 1 - v if f else v


def _position():
    return lax.axis_index("x"), lax.axis_index("y"), lax.axis_index("c")


def _hilo(v):
    hi = v.astype(BF16)
    lo = (v - hi.astype(F32)).astype(BF16)
    return jnp.concatenate([hi, lo], axis=1)


def _hilo_dot(v, mat):
    return jnp.dot(_hilo(v), mat, preferred_element_type=F32)


def _group_sum(v, gmat):
    parts = [_hilo_dot(v[:, LANES * j:LANES * (j + 1)], gmat) for j in range(v.shape[1] // LANES)]
    return parts[0] if len(parts) == 1 else jnp.concatenate(parts, axis=1)


def _softplus_terms(z):
    sp = jnp.log2(1.0 + jnp.exp2(-jnp.abs(z)))
    log_beta = jnp.minimum(z, 0.0) - sp
    return log_beta, log_beta - z


def _layer_norm_fwd(pre, g, b):
    mu = jnp.mean(pre, axis=-1, keepdims=True)
    d = pre - mu
    var = jnp.mean(d * d, axis=-1, keepdims=True)
    rstd = lax.rsqrt(var + LN_EPS)
    xhat = d * rstd
    return xhat * g + b, xhat, rstd


def _layer_norm_bwd(dy, xhat, rstd, g):
    dxh = dy * g
    m1 = jnp.mean(dxh, axis=-1, keepdims=True)
    m2 = jnp.mean(dxh * xhat, axis=-1, keepdims=True)
    return rstd * (dxh - m1 - xhat * m2)


def _row_tile(s, want):
    return min(s, want)


def _cast_into_slot(kc_idx, w, name):
    r, c = w.shape
    tr = _row_tile(r, 256)

    def body(kc_ref, w_ref, o_ref):
        o_ref[...] = w_ref[...].astype(BF16)

    grid_spec = pltpu.PrefetchScalarGridSpec(
        num_scalar_prefetch=1, grid=(r // tr,),
        in_specs=[pl.BlockSpec((tr, c), lambda i, kc: (i, 0))],
        out_specs=pl.BlockSpec((None, tr, c), lambda i, kc: (kc[0], i, 0)))
    return pl.pallas_call(
        body, name=name, grid_spec=grid_spec, out_shape=SDS((N_CHIPS, r, c), BF16),
        compiler_params=_params(("parallel",)))(kc_idx, w)


def _proj(x, w_in):
    s = x.shape[0]
    tm = _row_tile(s, 512)

    def body(x_ref, w_ref, gates_ref, qkv_ref, xb_ref):
        xb = x_ref[...].astype(BF16)
        xb_ref[...] = xb
        for k in range(N_CHIPS):
            acc = jnp.dot(xb, w_ref[k], preferred_element_type=F32)
            if k < 2:
                gates_ref[:, IN_SHARD * k:IN_SHARD * (k + 1)] = acc
            else:
                qkv_ref[:, IN_SHARD * (k - 2):IN_SHARD * (k - 1)] = acc.astype(BF16)

    return pl.pallas_call(
        body, name="proj", grid=(s // tm,),
        in_specs=[pl.BlockSpec((tm, D_MODEL), lambda i: (i, 0)),
                  pl.BlockSpec((N_CHIPS, D_MODEL, IN_SHARD), lambda i: (0, 0, 0))],
        out_specs=[pl.BlockSpec((tm, GATE_COLS), lambda i: (i, 0)),
                   pl.BlockSpec((tm, QKV_COLS), lambda i: (i, 0)),
                   pl.BlockSpec((tm, D_MODEL), lambda i: (i, 0))],
        out_shape=[SDS((s, GATE_COLS), F32), SDS((s, QKV_COLS), BF16), SDS((s, D_MODEL), BF16)],
        compiler_params=_params(("parallel",)))(x, w_in)


def _conv_forward_values(g_ref, halo_ref, taps_ref, first_block):
    gates = g_ref[...]
    tr = gates.shape[0]
    bg = gates[:, :CONV_WIDTH]
    cg = gates[:, CONV_WIDTH:2 * CONV_WIDTH]
    h = gates[:, 2 * CONV_WIDTH:]
    u = cg * h

    def prev(r):
        v = halo_ref[r:r + 1, CONV_WIDTH:2 * CONV_WIDTH] * halo_ref[r:r + 1, 2 * CONV_WIDTH:GATE_COLS]
        return jnp.where(first_block, 0.0, v)

    row = lax.broadcasted_iota(jnp.int32, (tr, CONV_WIDTH), 0)
    u1 = jnp.where(row == 0, prev(7), pltpu.roll(u, 1, 0))
    u2 = jnp.where(row == 0, prev(6), jnp.where(row == 1, prev(7), pltpu.roll(u, 2, 0)))
    y = taps_ref[0:1, :] * u2 + taps_ref[1:2, :] * u1 + taps_ref[2:3, :] * u
    return bg, cg, h, u, u1, u2, y


def _conv_fwd(gates, taps, g_conv, gmat):
    s = gates.shape[0]
    tr = _row_tile(s, 512)
    hb = tr // SUBLANES

    def body(g_ref, halo_ref, taps_ref, gain_ref, gmat_ref, out_ref):
        i = pl.program_id(0)
        bg, _, _, _, _, _, y = _conv_forward_values(g_ref, halo_ref, taps_ref, i == 0)
        yc = bg * y
        ms = _group_sum(yc * yc, gmat_ref[...]) * (1.0 / GROUP)
        out_ref[...] = (yc * lax.rsqrt(ms + RMS_EPS) * gain_ref[...]).astype(BF16)

    return pl.pallas_call(
        body, name="conv_fwd", grid=(s // tr,),
        in_specs=[pl.BlockSpec((tr, GATE_COLS), lambda i: (i, 0)),
                  pl.BlockSpec((SUBLANES, GATE_COLS), lambda i: (jnp.maximum(i * hb - 1, 0), 0)),
                  pl.BlockSpec((SUBLANES, CONV_WIDTH), lambda i: (0, 0)),
                  pl.BlockSpec((1, CONV_WIDTH), lambda i: (0, 0)),
                  pl.BlockSpec((2 * LANES, LANES), lambda i: (0, 0))],
        out_specs=pl.BlockSpec((tr, CONV_WIDTH), lambda i: (i, 0)),
        out_shape=SDS((s, CONV_WIDTH), BF16),
        compiler_params=_params(("parallel",)))(gates, gates, taps, g_conv, gmat)


def _conv_bwd_gate_step(first_block, g_ref, halo_ref, dn_ref, taps_ref, gain_ref, gmat_v, dbg_ref, dy_ref, sums_ref):
    bg, _, _, u, u1, u2, y = _conv_forward_values(g_ref, halo_ref, taps_ref, first_block)
    yc = bg * y
    rstd = lax.rsqrt(_group_sum(yc * yc, gmat_v) * (1.0 / GROUP) + RMS_EPS)
    n = yc * rstd
    dout = dn_ref[...]
    dn = dout * gain_ref[...]
    dyc = rstd * (dn - n * (_group_sum(dn * n, gmat_v) * (1.0 / GROUP)))
    dbg_ref[...] = (dyc * y).astype(BF16)
    dy = dyc * bg
    dy_ref[...] = dy

    @pl.when(first_block)
    def _():
        sums_ref[...] = jnp.zeros_like(sums_ref)

    sums_ref[0:1, :] += jnp.sum(dy * u2, axis=0, keepdims=True)
    sums_ref[1:2, :] += jnp.sum(dy * u1, axis=0, keepdims=True)
    sums_ref[2:3, :] += jnp.sum(dy * u, axis=0, keepdims=True)
    sums_ref[3:4, :] += jnp.sum(dout * n, axis=0, keepdims=True)


def _dproj_assemble(gates, dy, dbg, dq, dk, dv, taps):
    s = gates.shape[0]
    tr = _row_tile(s, 512)
    hb = tr // SUBLANES
    last = s // SUBLANES - 1
    n_blocks = s // tr

    def body(g_ref, dy_ref, halo_ref, dbg_ref, dq_ref, dk_ref, dv_ref, taps_ref, out_ref):
        i = pl.program_id(0)
        gates_v = g_ref[...]
        cg = gates_v[:, CONV_WIDTH:2 * CONV_WIDTH]
        h = gates_v[:, 2 * CONV_WIDTH:]
        dy_v = dy_ref[...]
        last_block = i == n_blocks - 1
        nxt = lambda r: jnp.where(last_block, 0.0, halo_ref[r:r + 1, :])
        row = lax.broadcasted_iota(jnp.int32, (tr, CONV_WIDTH), 0)
        d1 = jnp.where(row == tr - 1, nxt(0), pltpu.roll(dy_v, tr - 1, 0))
        d2 = jnp.where(row == tr - 1, nxt(1), jnp.where(row == tr - 2, nxt(0), pltpu.roll(dy_v, tr - 2, 0)))
        du = taps_ref[2:3, :] * dy_v + taps_ref[1:2, :] * d1 + taps_ref[0:1, :] * d2
        out_ref[:, 0:CONV_WIDTH] = dbg_ref[...]
        out_ref[:, CONV_WIDTH:2 * CONV_WIDTH] = (du * h).astype(BF16)
        out_ref[:, 2 * CONV_WIDTH:GATE_COLS] = (du * cg).astype(BF16)
        out_ref[:, GATE_COLS:GATE_COLS + ATTN_WIDTH] = dq_ref[...]
        out_ref[:, GATE_COLS + ATTN_WIDTH:GATE_COLS + 2 * ATTN_WIDTH] = dk_ref[...]
        out_ref[:, GATE_COLS + 2 * ATTN_WIDTH:] = dv_ref[...]

    row_spec = lambda w: pl.BlockSpec((tr, w), lambda i: (i, 0))
    return pl.pallas_call(
        body, name="dproj_assemble", grid=(s // tr,),
        in_specs=[row_spec(GATE_COLS), row_spec(CONV_WIDTH),
                  pl.BlockSpec((SUBLANES, CONV_WIDTH), lambda i: (jnp.minimum((i + 1) * hb, last), 0)),
                  row_spec(CONV_WIDTH), row_spec(ATTN_WIDTH), row_spec(ATTN_WIDTH), row_spec(ATTN_WIDTH),
                  pl.BlockSpec((SUBLANES, CONV_WIDTH), lambda i: (0, 0))],
        out_specs=row_spec(IN_COLS),
        out_shape=SDS((s, IN_COLS), BF16),
        compiler_params=_params(("parallel",)))(gates, dy, dy, dbg, dq, dk, dv, taps)


def _stack_heads(rows, nb):
    lane = lax.broadcasted_iota(jnp.int32, (1, LANES), 1)
    zero = jnp.zeros((KEY_BLOCK, LANES), rows.dtype)
    parts = []
    for blk in range(nb):
        r = rows[blk * KEY_BLOCK:(blk + 1) * KEY_BLOCK]
        parts.append(jnp.where(lane < GROUP, r, zero))
        parts.append(jnp.where(lane < GROUP, zero, r))
    return jnp.concatenate(parts, axis=0)


def _stack_hilo(v, n_cols):
    return jnp.concatenate([_hilo(v[:, c * KEY_BLOCK:(c + 1) * KEY_BLOCK]) for c in range(n_cols)], axis=0)


def _causal_mask(tq, nb, diag_base):
    shape = (tq, 2 * nb * KEY_BLOCK)
    row = lax.broadcasted_iota(jnp.int32, shape, 0)
    col = lax.broadcasted_iota(jnp.int32, shape, 1)
    key = diag_base + (col // (2 * KEY_BLOCK)) * KEY_BLOCK + col % KEY_BLOCK
    return key < row


ANY = pl.BlockSpec(memory_space=pl.ANY)


def _remote_copy(src, dst, sems, idx, target):
    return pltpu.make_async_remote_copy(src_ref=src, dst_ref=dst, send_sem=sems[0].at[idx], recv_sem=sems[1].at[idx],
                                        device_id=target, device_id_type=MESH)


def _gather_chip_hop(bufs, sems):
    x, y, c = _position()
    sends, arrivals = [], []
    for a, buf in enumerate(bufs):
        h = buf.shape[1] // 2
        rows = pl.ds(pl.multiple_of(c * h, h), h)
        mine = buf.at[2 * x + y, rows]
        for j, (fx, fy) in enumerate(CHIP_FLIPS):
            tx, ty = _flip(x, fx), _flip(y, fy)
            there = buf.at[2 * tx + ty, rows]
            sends.append(_remote_copy(mine, mine, sems, 6 * a + j, (tx, ty, c)))
            arrivals.append(_remote_copy(there, there, sems, 6 * a + j, (tx, ty, c)))
    return sends, arrivals


def _gather_sibling_hop(bufs, sems):
    x, y, c = _position()
    sends, arrivals = [], []
    for a, buf in enumerate(bufs):
        h = buf.shape[1] // 2
        mine, theirs = pl.ds(pl.multiple_of(c * h, h), h), pl.ds(pl.multiple_of((1 - c) * h, h), h)
        for j, (fx, fy) in enumerate(CHIP_FLIPS):
            kj = 2 * _flip(x, fx) + _flip(y, fy)
            landed, other = buf.at[kj, mine], buf.at[kj, theirs]
            sends.append(_remote_copy(landed, landed, sems, 6 * a + 3 + j, (x, y, 1 - c)))
            arrivals.append(_remote_copy(other, other, sems, 6 * a + 3 + j, (x, y, 1 - c)))
    return sends, arrivals


def _reduce_copies(ins, outs, sems):
    x, y, c = _position()
    sends, arrivals = [], []
    for a in range(len(ins)):
        h = ins[a].shape[1] // 2
        for f, (fx, fy, fc) in enumerate(DEVICE_FLIPS):
            tx, ty, tc = _flip(x, fx), _flip(y, fy), _flip(c, fc)
            src = ins[a].at[2 * tx + ty, pl.ds(pl.multiple_of(tc * h, h), h)]
            sends.append(_remote_copy(src, outs[a].at[f], sems, 7 * a + f, (tx, ty, tc)))
            arrivals.append(_remote_copy(outs[a].at[f], outs[a].at[f], sems, 7 * a + f, (tx, ty, tc)))
    return sends, arrivals


def _chip_reduce_copies(src, dst, sems):
    x, y, c = _position()
    sends, arrivals = [], []
    for j, (fx, fy) in enumerate(CHIP_FLIPS):
        tx, ty = _flip(x, fx), _flip(y, fy)
        sends.append(_remote_copy(src.at[2 * tx + ty], dst.at[j], sems, j, (tx, ty, c)))
        arrivals.append(_remote_copy(dst.at[j], dst.at[j], sems, j, (tx, ty, c)))
    return sends, arrivals


def _start_copies(make):
    sends, _ = make()
    for cp in sends:
        cp.start()


def _finish_copies(make):
    sends, arrivals = make()
    for cp in arrivals:
        cp.wait_recv()
    for cp in sends:
        cp.wait_send()


def _attn_fwd(qkv, g_attn, tri, gmat, shards):
    n_w = len(shards)
    s = qkv.shape[0]
    tq = _row_tile(s, ATTN_Q_TILE)
    tk = KEY_BLOCK
    nb = ATTN_KEY_BLOCKS
    width = nb * tk
    n_groups = ATTN_DIAG_GROUPS
    group = tq // n_groups
    pairs = ATTN_WIDTH // LANES

    def body(q_ref, k_ref, v_ref, gain_ref, tri_ref, gmat_ref, *rest):
        o_ref, yn_ref, tot_ref, cut_ref = rest[n_w:n_w + 4]
        w_bufs, sems = rest[n_w + 4:2 * n_w + 4], rest[2 * n_w + 4:]
        chip_hop = functools.partial(_gather_chip_hop, w_bufs, sems)
        sibling_hop = functools.partial(_gather_sibling_hop, w_bufs, sems)
        p, i = pl.program_id(0), pl.program_id(1)
        pl.when((p == 0) & (i == 0))(functools.partial(_start_copies, chip_hop))

        @pl.when((p == pairs - 1) & (i == 0))
        def _():
            for cp in chip_hop()[1]:
                cp.wait_recv()
            _start_copies(sibling_hop)

        q2 = q_ref[...]
        tri_v = tri_ref[...]

        def trip(s0, n_blk, rows, carry, diag_base):
            r0, nr = rows
            run = [carry[0], carry[1]]
            oacc = carry[2]
            ksel = _stack_heads(k_ref[pl.ds(s0, n_blk * tk), :], n_blk)
            vsel = _stack_heads(v_ref[pl.ds(s0, n_blk * tk), :], n_blk)
            z = lax.dot_general(q2[r0:r0 + nr], ksel, NT_DIMS, preferred_element_type=F32) * (ATTN_SCALE * LOG2_E)
            log_beta, log_keep = _softplus_terms(z)
            if diag_base is not None:
                valid = _causal_mask(nr, n_blk, diag_base)
                log_keep = jnp.where(valid, log_keep, 0.0)
            ct = jnp.dot(_stack_hilo(log_keep, 2 * n_blk), tri_v, preferred_element_type=F32)
            a_parts = [None] * (2 * n_blk)
            for c in reversed(range(2 * n_blk)):
                h = c % 2
                ct_c = ct[c * nr:(c + 1) * nr]
                a_parts[c] = jnp.exp2(log_beta[:, c * tk:(c + 1) * tk] + ct_c[:, :tk] + run[h])
                run[h] = run[h] + ct_c[:, tk:]
            a = jnp.concatenate(a_parts, axis=1)
            if diag_base is not None:
                a = jnp.where(valid, a, 0.0)
            oacc = oacc + jnp.dot(a.astype(BF16), vsel, preferred_element_type=F32)
            return run[0], run[1], oacc

        n_full = i * (tq // width)
        tile = p * pl.num_programs(1) + i

        def alive(run_a, run_b):
            return jnp.max(jnp.maximum(run_a, run_b)) > -ATTN_DEAD_LOG2

        groups = []
        for g in range(n_groups):
            rows = (g * group, group)
            zeros = (jnp.zeros((group, tk), F32), jnp.zeros((group, tk), F32), jnp.zeros((group, LANES), F32))
            state = trip(pl.multiple_of(i * tq, tq), (g + 1) * group // tk, rows, zeros, -g * group)

            def earlier_trip(c, rows=rows):
                done, _, run_a, run_b, oacc = c
                s0 = pl.multiple_of((n_full - 1 - done) * width, width)
                run_a, run_b, oacc = trip(s0, nb, rows, (run_a, run_b, oacc), None)
                return done + 1, alive(run_a, run_b), run_a, run_b, oacc

            swept = lax.while_loop(lambda c: (c[0] < n_full) & c[1], earlier_trip,
                                   (jnp.int32(0), alive(state[0], state[1])) + state)
            cut_ref[n_groups * tile + g] = (n_full - swept[0]).astype(F32)
            groups.append(swept[2:])
        run_a, run_b, oacc = (jnp.concatenate([grp[j] for grp in groups], axis=0) for j in range(3))
        lane = lax.broadcasted_iota(jnp.int32, (1, LANES), 1)
        o_ref[...] = oacc
        tot_ref[...] = jnp.where(lane < GROUP, run_a, run_b)
        ms = _group_sum(oacc * oacc, gmat_ref[...]) * (1.0 / GROUP)
        yn_ref[...] = (oacc * lax.rsqrt(ms + RMS_EPS) * gain_ref[...]).astype(BF16)

        @pl.when((p == pairs - 1) & (i == pl.num_programs(1) - 1))
        def _():
            for cp in chip_hop()[0]:
                cp.wait_send()
            _finish_copies(sibling_hop)

    blk = lambda: pl.BlockSpec((tq, LANES), lambda p, i: (i, p))
    return pl.pallas_call(
        body, name="attn_fwd", grid=(pairs, s // tq),
        in_specs=[pl.BlockSpec((tq, LANES), lambda p, i: (i, p)),
                  pl.BlockSpec((s, LANES), lambda p, i: (0, pairs + p)),
                  pl.BlockSpec((s, LANES), lambda p, i: (0, 2 * pairs + p)),
                  pl.BlockSpec((1, LANES), lambda p, i: (0, p)),
                  pl.BlockSpec((2 * tk, 2 * tk), lambda p, i: (0, 0)),
                  pl.BlockSpec((2 * LANES, LANES), lambda p, i: (0, 0))] + [ANY] * n_w,
        out_specs=[blk(), blk(), blk(), pl.BlockSpec(memory_space=pltpu.SMEM)] + [ANY] * n_w,
        out_shape=[SDS((s, ATTN_WIDTH), F32), SDS((s, ATTN_WIDTH), BF16), SDS((s, ATTN_WIDTH), F32),
                   SDS((n_groups * pairs * (s // tq),), F32)]
        + [SDS(w.shape, w.dtype) for w in shards],
        input_output_aliases={6 + a: 4 + a for a in range(n_w)},
        scratch_shapes=[pltpu.SemaphoreType.DMA((6 * n_w,)), pltpu.SemaphoreType.DMA((6 * n_w,))],
        compiler_params=_params(("arbitrary", "arbitrary")))(qkv, qkv, qkv, g_attn, tri, gmat, *shards)


def _attn_bwd(qkv, o, tot, dyn, g_attn, tri, gmat, cut, gates, dycn, taps, g_conv, partials):
    n_g = len(partials)
    s = qkv.shape[0]
    tq = _row_tile(s, ATTN_Q_TILE)
    tk = KEY_BLOCK
    nb = ATTN_KEY_BLOCKS
    width = nb * tk
    n_groups = ATTN_DIAG_GROUPS
    group = tq // n_groups
    pairs = ATTN_WIDTH // LANES

    def body(q_ref, k_ref, v_ref, o_ref, tot_ref, dyn_ref, gain_ref, tri_ref, gmat_ref, cut_ref,
             gates_ref, halo_ref, dycn_ref, taps_ref, gconv_ref, *rest):
        g_ins, (dq_ref, dk_out, dv_out, dg_ref) = rest[:n_g], rest[n_g:n_g + 4]
        dbg_ref, dy_ref, conv_sums_ref = rest[n_g + 4:n_g + 7]
        g_outs, sems = rest[n_g + 7:2 * n_g + 7], rest[2 * n_g + 7:2 * n_g + 9]
        dk_ref, dv_ref = rest[2 * n_g + 9:]
        copies = functools.partial(_reduce_copies, g_ins, g_outs, sems)
        p, i = pl.program_id(0), pl.program_id(1)
        pl.when((p == 0) & (i == 0))(functools.partial(_start_copies, copies))

        @pl.when(i == 0)
        def _():
            dk_ref[...] = jnp.zeros_like(dk_ref)
            dv_ref[...] = jnp.zeros_like(dv_ref)
            dg_ref[...] = jnp.zeros_like(dg_ref)

        gmat_v = gmat_ref[...]
        _conv_bwd_gate_step((p == 0) & (i == 0), gates_ref, halo_ref, dycn_ref, taps_ref, gconv_ref, gmat_v,
                            dbg_ref, dy_ref, conv_sums_ref)
        o_v = o_ref[...]
        rstd = lax.rsqrt(_group_sum(o_v * o_v, gmat_v) * (1.0 / GROUP) + RMS_EPS)
        n = o_v * rstd
        dout = dyn_ref[...]
        dg_ref[0:1, :] += jnp.sum(dout * n, axis=0, keepdims=True)
        dn = dout * gain_ref[...]
        do2 = (rstd * (dn - n * (_group_sum(dn * n, gmat_v) * (1.0 / GROUP)))).astype(BF16)
        q2 = q_ref[...]
        tot_v = tot_ref[...]
        tots = (jnp.broadcast_to(tot_v[:, 0:1], (tq, tk)), jnp.broadcast_to(tot_v[:, GROUP:GROUP + 1], (tq, tk)))
        tri_v, tri_incl_v = tri_ref[0], tri_ref[1]
        lane = lax.broadcasted_iota(jnp.int32, (1, LANES), 1)

        def trip(s0, n_blk, rows, carry, diag_base):
            r0, nr = rows
            rest_l = [carry[0], carry[1]]
            pref_g = [carry[2], carry[3]]
            dq = carry[4]
            q_rows, do_rows = q2[r0:r0 + nr], do2[r0:r0 + nr]
            ksel = _stack_heads(k_ref[pl.ds(s0, n_blk * tk), :], n_blk)
            vsel = _stack_heads(v_ref[pl.ds(s0, n_blk * tk), :], n_blk)
            z = lax.dot_general(q_rows, ksel, NT_DIMS, preferred_element_type=F32) * (ATTN_SCALE * LOG2_E)
            log_beta, log_keep = _softplus_terms(z)
            if diag_base is not None:
                valid = _causal_mask(nr, n_blk, diag_base)
                log_keep = jnp.where(valid, log_keep, 0.0)
            ctl = jnp.dot(_stack_hilo(log_keep, 2 * n_blk), tri_incl_v, preferred_element_type=F32)
            da = lax.dot_general(do_rows, vsel, NT_DIMS, preferred_element_type=F32)
            a_parts = []
            for c in range(2 * n_blk):
                h = c % 2
                ct_c = ctl[c * nr:(c + 1) * nr]
                cols = slice(c * tk, (c + 1) * tk)
                a_parts.append(jnp.exp2(log_beta[:, cols] + (rest_l[h] - ct_c[:, :tk])))
                rest_l[h] = rest_l[h] - ct_c[:, tk:]
            a = jnp.concatenate(a_parts, axis=1)
            if diag_base is not None:
                a = jnp.where(valid, a, 0.0)
            g = a * da
            ctg = jnp.dot(_stack_hilo(g, 2 * n_blk), tri_v, preferred_element_type=F32)
            dz_parts = []
            for c in range(2 * n_blk):
                h = c % 2
                ct_c = ctg[c * nr:(c + 1) * nr]
                cols = slice(c * tk, (c + 1) * tk)
                prefix = pref_g[h] + ct_c[:, :tk]
                pref_g[h] = pref_g[h] + ct_c[:, tk:]
                g_c = g[:, cols]
                dz_parts.append(g_c - jnp.exp2(log_beta[:, cols]) * (g_c + prefix))
            dz = jnp.concatenate(dz_parts, axis=1) * ATTN_SCALE
            if diag_base is not None:
                dz = jnp.where(valid, dz, 0.0)
            dzb = dz.astype(BF16)
            dq = dq + jnp.dot(dzb, ksel, preferred_element_type=F32)
            dkt = lax.dot_general(dzb, q_rows, TN_DIMS, preferred_element_type=F32)
            dvt = lax.dot_general(a.astype(BF16), do_rows, TN_DIMS, preferred_element_type=F32)
            for blk in range(n_blk):
                ra, rb = slice(2 * blk * tk, (2 * blk + 1) * tk), slice((2 * blk + 1) * tk, (2 * blk + 2) * tk)
                keys = pl.ds(pl.multiple_of(s0 + blk * tk, tk), tk)
                dk_ref[keys, :] += jnp.where(lane < GROUP, dkt[ra], dkt[rb])
                dv_ref[keys, :] += jnp.where(lane < GROUP, dvt[ra], dvt[rb])
            return rest_l[0], rest_l[1], pref_g[0], pref_g[1], dq

        n_full = i * (tq // width)
        tile = p * pl.num_programs(1) + i
        dq_groups = []
        for g in range(n_groups):
            rows = (g * group, group)
            first = jnp.clip(cut_ref[n_groups * tile + g].astype(jnp.int32), 0, n_full)
            zeros_qk = jnp.zeros((group, tk), F32)
            carry = (tots[0][g * group:(g + 1) * group], tots[1][g * group:(g + 1) * group], zeros_qk, zeros_qk,
                     jnp.zeros((group, LANES), F32))
            carry = lax.fori_loop(
                first, n_full,
                lambda t, c, rows=rows: trip(pl.multiple_of(t * width, width), nb, rows, c, None), carry)
            dq_groups.append(trip(pl.multiple_of(i * tq, tq), (g + 1) * group // tk, rows, carry, -g * group)[4])
        dq_ref[...] = jnp.concatenate(dq_groups, axis=0).astype(BF16)

        @pl.when(i == pl.num_programs(1) - 1)
        def _():
            dk_out[...] = dk_ref[...].astype(BF16)
            dv_out[...] = dv_ref[...].astype(BF16)

        pl.when((p == pairs - 1) & (i == pl.num_programs(1) - 1))(functools.partial(_finish_copies, copies))

    blk = lambda: pl.BlockSpec((tq, LANES), lambda p, i: (i, p))
    col = lambda: pl.BlockSpec((s, LANES), lambda p, i: (0, p))
    n_peers = len(DEVICE_FLIPS)
    nq = s // tq
    conv_rows = s // (pairs * nq)
    conv_blk = lambda w: pl.BlockSpec((conv_rows, w), lambda p, i: (p * nq + i, 0))
    halo_blk = pl.BlockSpec((SUBLANES, GATE_COLS),
                            lambda p, i: (jnp.maximum((p * nq + i) * (conv_rows // SUBLANES) - 1, 0), 0))
    return pl.pallas_call(
        body, name="attn_bwd", grid=(pairs, nq),
        in_specs=[pl.BlockSpec((tq, LANES), lambda p, i: (i, p)),
                  pl.BlockSpec((s, LANES), lambda p, i: (0, pairs + p)),
                  pl.BlockSpec((s, LANES), lambda p, i: (0, 2 * pairs + p)),
                  blk(), blk(), blk(),
                  pl.BlockSpec((1, LANES), lambda p, i: (0, p)),
                  pl.BlockSpec((2, 2 * tk, 2 * tk), lambda p, i: (0, 0, 0)),
                  pl.BlockSpec((2 * LANES, LANES), lambda p, i: (0, 0)),
                  pl.BlockSpec(memory_space=pltpu.SMEM),
                  conv_blk(GATE_COLS), halo_blk, conv_blk(CONV_WIDTH),
                  pl.BlockSpec((SUBLANES, CONV_WIDTH), lambda p, i: (0, 0)),
                  pl.BlockSpec((1, CONV_WIDTH), lambda p, i: (0, 0))] + [ANY] * n_g,
        out_specs=[blk(), col(), col(), pl.BlockSpec((SUBLANES, LANES), lambda p, i: (0, p)),
                   conv_blk(CONV_WIDTH), conv_blk(CONV_WIDTH),
                   pl.BlockSpec((SUBLANES, CONV_WIDTH), lambda p, i: (0, 0))] + [ANY] * n_g,
        out_shape=[SDS((s, ATTN_WIDTH), BF16), SDS((s, ATTN_WIDTH), BF16), SDS((s, ATTN_WIDTH), BF16),
                   SDS((SUBLANES, ATTN_WIDTH), F32),
                   SDS((s, CONV_WIDTH), BF16), SDS((s, CONV_WIDTH), F32), SDS((SUBLANES, CONV_WIDTH), F32)]
        + [SDS((n_peers, g.shape[1] // 2, g.shape[2]), g.dtype) for g in partials],
        scratch_shapes=[pltpu.SemaphoreType.DMA((n_peers * n_g,)), pltpu.SemaphoreType.DMA((n_peers * n_g,)),
                        pltpu.VMEM((s, LANES), F32), pltpu.VMEM((s, LANES), F32)],
        compiler_params=_params(("arbitrary", "arbitrary")))(
            qkv, qkv, qkv, o, tot, dyn, g_attn, tri, gmat, cut, gates, gates, dycn, taps, g_conv, *partials)


def _mix_ln1(ycn, yan, w_out, x, g, b):
    s = x.shape[0]
    tm = _row_tile(s, 512)

    def body(yc_ref, ya_ref, w_ref, x_ref, g_ref, b_ref, x1_ref, xhat_ref, rstd_ref, x1b_ref):
        mix = jnp.dot(yc_ref[...], w_ref[0:CONV_WIDTH, :], preferred_element_type=F32)
        mix = mix + jnp.dot(ya_ref[...], w_ref[CONV_WIDTH:, :], preferred_element_type=F32)
        x1, xhat, rstd = _layer_norm_fwd(ALPHA * x_ref[...] + mix, g_ref[...], b_ref[...])
        x1_ref[...] = x1
        xhat_ref[...] = xhat
        rstd_ref[...] = rstd
        x1b_ref[...] = x1.astype(BF16)

    row = lambda w: pl.BlockSpec((tm, w), lambda i: (i, 0))
    vec = lambda: pl.BlockSpec((1, D_MODEL), lambda i: (0, 0))
    return pl.pallas_call(
        body, name="mix_ln1", grid=(s // tm,),
        in_specs=[row(CONV_WIDTH), row(ATTN_WIDTH), pl.BlockSpec((D_MODEL, D_MODEL), lambda i: (0, 0)),
                  row(D_MODEL), vec(), vec()],
        out_specs=[row(D_MODEL), row(D_MODEL), row(1), row(D_MODEL)],
        out_shape=[SDS((s, D_MODEL), F32), SDS((s, D_MODEL), F32), SDS((s, 1), F32), SDS((s, D_MODEL), BF16)],
        compiler_params=_params(("parallel",)))(ycn, yan, w_out, x, g, b)


def _mlp_fwd_loss(x1, w_up, w_down, target, g, b):
    s = x1.shape[0]
    tm = _row_tile(s, 256)

    def body(x1_ref, wu_ref, wd_ref, t_ref, g_ref, b_ref, dpre_ref, sums_ref, loss_ref, r_ref, hid_ref, dpreb_ref):
        i = pl.program_id(0)
        x1_v = x1_ref[...]
        xb = x1_v.astype(BF16)
        ffn = jnp.zeros((tm, D_MODEL), F32)
        for k in range(N_CHIPS):
            r = jnp.maximum(jnp.dot(xb, wu_ref[k], preferred_element_type=F32), 0.0)
            hid = (r * r).astype(BF16)
            r_ref[:, FF_SHARD * k:FF_SHARD * (k + 1)] = r.astype(BF16)
            hid_ref[:, FF_SHARD * k:FF_SHARD * (k + 1)] = hid
            ffn = ffn + jnp.dot(hid, wd_ref[k], preferred_element_type=F32)
        g_v = g_ref[...]
        x2, xhat, rstd = _layer_norm_fwd(ALPHA * x1_v + ffn, g_v, b_ref[...])
        err = x2 - t_ref[...]
        dx2 = err * (1.0 / D_MODEL)
        dpre = _layer_norm_bwd(dx2, xhat, rstd, g_v)
        dpre_ref[...] = dpre
        dpreb_ref[...] = dpre.astype(BF16)

        @pl.when(i == 0)
        def _():
            sums_ref[...] = jnp.zeros_like(sums_ref)
            loss_ref[...] = jnp.zeros_like(loss_ref)

        sums_ref[0:1, :] += jnp.sum(dx2 * xhat, axis=0, keepdims=True)
        sums_ref[1:2, :] += jnp.sum(dx2, axis=0, keepdims=True)
        loss_ref[...] += jnp.sum(jnp.sum(err * err, axis=1, keepdims=True), axis=0, keepdims=True) * (0.5 / D_MODEL)

    row = lambda: pl.BlockSpec((tm, D_MODEL), lambda i: (i, 0))
    wide = lambda: pl.BlockSpec((tm, D_FF), lambda i: (i, 0))
    vec = lambda: pl.BlockSpec((1, D_MODEL), lambda i: (0, 0))
    return pl.pallas_call(
        body, name="mlp_fwd_loss", grid=(s // tm,),
        in_specs=[row(), _resident_weight(), _resident_weight(), row(), vec(), vec()],
        out_specs=[row(), pl.BlockSpec((SUBLANES, D_MODEL), lambda i: (0, 0)),
                   pl.BlockSpec((SUBLANES, LANES), lambda i: (0, 0)), wide(), wide(), row()],
        out_shape=[SDS((s, D_MODEL), F32), SDS((SUBLANES, D_MODEL), F32), SDS((SUBLANES, LANES), F32),
                   SDS((s, D_FF), BF16), SDS((s, D_FF), BF16), SDS((s, D_MODEL), BF16)],
        compiler_params=_params(("arbitrary",)))(x1, w_up, w_down, target, g, b)


def _resident_weight():
    return pl.BlockSpec((N_CHIPS, D_MODEL, FF_SHARD), lambda i: (0, 0, 0), pipeline_mode=pl.Buffered(1))


def _mlp_bwd_ln1(relu_up, dpre2, w_up, w_down, xhat1, rstd1, g1, w_out):
    s = dpre2.shape[0]
    tm = _row_tile(s, 256)

    def body(r_ref, d2_ref, wu_ref, wd_ref, xh_ref, rs_ref, g_ref, wo_ref,
             dup_ref, dpre_ref, sums_ref, dpreb_ref, dyc_ref, dya_ref):
        i = pl.program_id(0)
        d2 = d2_ref[...]
        d2b = d2.astype(BF16)
        dx1 = ALPHA * d2
        for k in range(N_CHIPS):
            r = r_ref[:, FF_SHARD * k:FF_SHARD * (k + 1)].astype(F32)
            dhid = lax.dot_general(d2b, wd_ref[k], NT_DIMS, preferred_element_type=F32)
            dupb = (dhid * (2.0 * r)).astype(BF16)
            dup_ref[:, FF_SHARD * k:FF_SHARD * (k + 1)] = dupb
            dx1 = dx1 + lax.dot_general(dupb, wu_ref[k], NT_DIMS, preferred_element_type=F32)
        xhat = xh_ref[...]
        dpre = _layer_norm_bwd(dx1, xhat, rs_ref[...], g_ref[...])
        dpre_ref[...] = dpre
        dpb = dpre.astype(BF16)
        dpreb_ref[...] = dpb
        dyc_ref[...] = lax.dot_general(dpb, wo_ref[0:CONV_WIDTH, :], NT_DIMS, preferred_element_type=F32)
        dya_ref[...] = lax.dot_general(dpb, wo_ref[CONV_WIDTH:, :], NT_DIMS, preferred_element_type=F32)

        @pl.when(i == 0)
        def _():
            sums_ref[...] = jnp.zeros_like(sums_ref)

        sums_ref[0:1, :] += jnp.sum(dx1 * xhat, axis=0, keepdims=True)
        sums_ref[1:2, :] += jnp.sum(dx1, axis=0, keepdims=True)

    row = lambda w: pl.BlockSpec((tm, w), lambda i: (i, 0))
    return pl.pallas_call(
        body, name="mlp_bwd_ln1", grid=(s // tm,),
        in_specs=[row(D_FF), row(D_MODEL), _resident_weight(), _resident_weight(), row(D_MODEL), row(1),
                  pl.BlockSpec((1, D_MODEL), lambda i: (0, 0)),
                  pl.BlockSpec((D_MODEL, D_MODEL), lambda i: (0, 0), pipeline_mode=pl.Buffered(1))],
        out_specs=[row(D_FF), row(D_MODEL), pl.BlockSpec((SUBLANES, D_MODEL), lambda i: (0, 0)), row(D_MODEL),
                   row(CONV_WIDTH), row(ATTN_WIDTH)],
        out_shape=[SDS((s, D_FF), BF16), SDS((s, D_MODEL), F32), SDS((SUBLANES, D_MODEL), F32),
                   SDS((s, D_MODEL), BF16), SDS((s, CONV_WIDTH), F32), SDS((s, ATTN_WIDTH), F32)],
        compiler_params=_params(("arbitrary",)))(relu_up, dpre2, w_up, w_down, xhat1, rstd1, g1, w_out)


def _grad_tn(a, b, name, out_cols, stacked):
    s, ka = a.shape
    n = b.shape[1]
    ts = _row_tile(s, 2048)
    n_steps = s // ts
    if stacked:
        tka, tn = ka, out_cols
        grid = (1, n // tn, n_steps)
        shape = (n // tn, ka, tn)
        out_spec = lambda: pl.BlockSpec((None, tka, tn), lambda r, c, t: (c, 0, 0))
    else:
        tka, tn = min(ka, 1024), n
        grid = (ka // tka, 1, n_steps)
        shape = (ka, n)
        out_spec = lambda: pl.BlockSpec((tka, tn), lambda r, c, t: (r, 0))

    def body(a_ref, b_ref, o_ref, ob_ref):
        t = pl.program_id(2)

        @pl.when(t == 0)
        def _():
            o_ref[...] = jnp.zeros_like(o_ref)

        o_ref[...] += lax.dot_general(a_ref[...].astype(BF16), b_ref[...].astype(BF16), TN_DIMS,
                                      preferred_element_type=F32)

        @pl.when(t == n_steps - 1)
        def _():
            ob_ref[...] = o_ref[...].astype(BF16)

    return pl.pallas_call(
        body, name=name, grid=grid,
        in_specs=[pl.BlockSpec((ts, tka), lambda r, c, t: (t, r)),
                  pl.BlockSpec((ts, tn), lambda r, c, t: (t, c))],
        out_specs=[out_spec(), out_spec()], out_shape=[SDS(shape, F32), SDS(shape, BF16)],
        compiler_params=_params(("parallel", "parallel", "arbitrary")))(a, b)


def _grad_w_out(ycn, yan, dpre1):
    s = dpre1.shape[0]
    ts = _row_tile(s, 2048)
    n_steps = s // ts

    def body(yc_ref, ya_ref, d_ref, o_ref, ob_ref):
        half, t = pl.program_id(0), pl.program_id(1)

        @pl.when(t == 0)
        def _():
            o_ref[...] = jnp.zeros_like(o_ref)

        db = d_ref[...]

        @pl.when(half == 0)
        def _():
            o_ref[...] += lax.dot_general(yc_ref[...], db, TN_DIMS, preferred_element_type=F32)

        @pl.when(half == 1)
        def _():
            o_ref[...] += lax.dot_general(ya_ref[...], db, TN_DIMS, preferred_element_type=F32)

        @pl.when(t == n_steps - 1)
        def _():
            ob_ref[...] = o_ref[...].astype(BF16)

    out_spec = lambda: pl.BlockSpec((CONV_WIDTH, D_MODEL), lambda r, t: (r, 0))
    return pl.pallas_call(
        body, name="grad_w_out", grid=(2, n_steps),
        in_specs=[pl.BlockSpec((ts, CONV_WIDTH), lambda r, t: (t, 0)),
                  pl.BlockSpec((ts, ATTN_WIDTH), lambda r, t: (t, 0)),
                  pl.BlockSpec((ts, D_MODEL), lambda r, t: (t, 0))],
        out_specs=[out_spec(), out_spec()],
        out_shape=[SDS((D_MODEL, D_MODEL), F32), SDS((D_MODEL, D_MODEL), BF16)],
        compiler_params=_params(("parallel", "arbitrary")))(ycn, yan, dpre1)


def _sum_with_peers(own_ref, r_ref, o_ref):
    acc = own_ref[...]
    for f in range(r_ref.shape[0]):
        acc = acc + r_ref[f].astype(F32)
    o_ref[...] = acc


def _grad_x(kc_idx, dproj, w_in, dpre1, chip_sums, earlier):
    s = dproj.shape[0]
    tm = _row_tile(s, 512)
    steps = s // tm
    n_peers = len(DEVICE_FLIPS)
    n_chips = len(CHIP_FLIPS)
    n_e = len(earlier)

    def body(kc_ref, dp_ref, w_ref, d1_ref, *rest):
        sum_ins, g_in = rest[:2 * n_e], rest[2 * n_e]
        o_ref, g_out = rest[2 * n_e + 1], rest[2 * n_e + 2]
        sum_outs, sems = rest[2 * n_e + 3:3 * n_e + 3], rest[3 * n_e + 3:]
        copies = functools.partial(_chip_reduce_copies, g_in, g_out, sems)
        i = pl.program_id(0)
        pl.when(i == 0)(functools.partial(_start_copies, copies))
        acc = ALPHA * d1_ref[...]
        for k in range(N_CHIPS):
            acc = acc + lax.dot_general(dp_ref[:, IN_SHARD * k:IN_SHARD * (k + 1)], w_ref[k], NT_DIMS,
                                        preferred_element_type=F32)
        o_ref[...] = acc
        for a in range(n_e):
            _sum_with_peers(sum_ins[2 * a], sum_ins[2 * a + 1], sum_outs[a])
        pl.when(i == steps - 1)(functools.partial(_finish_copies, copies))

    in_specs = [pl.BlockSpec((tm, IN_COLS), lambda i, kc: (i, 0)),
                pl.BlockSpec((N_CHIPS, D_MODEL, IN_SHARD), lambda i, kc: (0, 0, 0)),
                pl.BlockSpec((tm, D_MODEL), lambda i, kc: (i, 0))]
    out_specs = [pl.BlockSpec((tm, D_MODEL), lambda i, kc: (i, 0)), ANY]
    out_shape = [SDS((s, D_MODEL), F32), SDS((n_chips,) + chip_sums.shape[1:], chip_sums.dtype)]
    operands = []
    for own, recv in earlier:
        _, _, h, cols = own.shape
        th = h // steps
        in_specs.append(pl.BlockSpec((None, None, th, cols), lambda i, kc: (kc[0], kc[1], i, 0)))
        in_specs.append(pl.BlockSpec((n_peers, th, cols), lambda i, kc: (0, i, 0)))
        out_specs.append(pl.BlockSpec((th, cols), lambda i, kc: (kc[1] * steps + i, 0)))
        out_shape.append(SDS((2 * h, cols), F32))
        operands += [own, recv]
    grid_spec = pltpu.PrefetchScalarGridSpec(
        num_scalar_prefetch=1, grid=(steps,), in_specs=in_specs + [ANY], out_specs=out_specs,
        scratch_shapes=[pltpu.SemaphoreType.DMA((n_chips,)), pltpu.SemaphoreType.DMA((n_chips,))])
    return pl.pallas_call(
        body, name="grad_x", grid_spec=grid_spec, out_shape=out_shape,
        compiler_params=_params(("arbitrary",)))(kc_idx, dproj, w_in, dpre1, *operands, chip_sums)


def _adamw_step(w, g, m, v):
    nm = ADAM_B1 * m + (1.0 - ADAM_B1) * g
    nv = ADAM_B2 * v + (1.0 - ADAM_B2) * (g * g)
    m_hat = nm / (1.0 - ADAM_B1 ** ADAM_STEP)
    v_hat = nv / (1.0 - ADAM_B2 ** ADAM_STEP)
    return -ADAM_LR * (m_hat / (jnp.sqrt(v_hat) + ADAM_EPS) + ADAM_WD * w), nm, nv


def _adamw(w, g, m, v, name):
    r, c = w.shape
    tr = _row_tile(r, 256)

    def body(w_ref, g_ref, m_ref, v_ref, go_ref, d_ref, nm_ref, nv_ref):
        g_v = g_ref[...]
        go_ref[...] = g_v
        d_ref[...], nm_ref[...], nv_ref[...] = _adamw_step(w_ref[...], g_v, m_ref[...], v_ref[...])

    spec = lambda: pl.BlockSpec((tr, c), lambda i: (i, 0))
    return pl.pallas_call(
        body, name=name, grid=(r // tr,),
        in_specs=[spec(), spec(), spec(), spec()], out_specs=[spec(), spec(), spec(), spec()],
        out_shape=[SDS((r, c), F32)] * 4, compiler_params=_params(("parallel",)))(w, g, m, v)


def _adamw_small(total, conv_grad, weights, moments, variances):
    n = len(weights)
    starts = (ROW_GCONV, ROW_GATTN, ROW_LN1G, ROW_LN1B, ROW_LN2G, ROW_LN2B)

    def body(total_ref, cg_ref, *refs):
        w, m, v, outs = refs[:n], refs[n:2 * n], refs[2 * n:3 * n], refs[3 * n:]
        for p in range(n):
            rows = w[p].shape[0]
            g = cg_ref[...] if p == n - 1 else total_ref[starts[p]:starts[p] + rows, :]
            outs[4 * p][...] = g
            outs[4 * p + 1][...], outs[4 * p + 2][...], outs[4 * p + 3][...] = _adamw_step(
                w[p][...], g, m[p][...], v[p][...])

    vmem = pl.BlockSpec(memory_space=pltpu.VMEM)
    out_shape = [SDS(w.shape, F32) for w in weights for _ in range(4)]
    flat = pl.pallas_call(
        body, name="adamw_small", in_specs=[vmem] * (2 + 3 * n), out_specs=[vmem] * (4 * n),
        out_shape=out_shape)(total, conv_grad, *weights, *moments, *variances)
    return [flat[4 * p:4 * p + 4] for p in range(n)]


def _sum_partials(kc_idx, own, recv, name):
    h, cols = own.shape
    th = _row_tile(h, 128)
    n_peers = recv.shape[0]

    def body(kc_ref, own_ref, r_ref, o_ref):
        _sum_with_peers(own_ref, r_ref, o_ref)

    grid_spec = pltpu.PrefetchScalarGridSpec(
        num_scalar_prefetch=1, grid=(h // th,),
        in_specs=[pl.BlockSpec((th, cols), lambda t, kc: (t, 0)),
                  pl.BlockSpec((n_peers, th, cols), lambda t, kc: (0, t, 0))],
        out_specs=pl.BlockSpec((th, cols), lambda t, kc: (kc[1] * (h // th) + t, 0)))
    return pl.pallas_call(
        body, name=name, grid_spec=grid_spec, out_shape=SDS((2 * h, cols), F32),
        compiler_params=_params(("parallel",)))(kc_idx, own, recv)


def _add_sibling(kc_idx, grad, recv):
    _, _, h, cols = grad.shape
    th = _row_tile(h, 512)

    def body(kc_ref, g_ref, r_ref, sums_ref, own_ref):
        total = g_ref[...] + r_ref[...].astype(F32)
        sums_ref[...] = total.astype(BF16)

        @pl.when(pl.program_id(1) == kc_ref[0])
        def _():
            own_ref[...] = total

    grid_spec = pltpu.PrefetchScalarGridSpec(
        num_scalar_prefetch=1, grid=(h // th, N_CHIPS),
        in_specs=[pl.BlockSpec((None, None, th, cols), lambda t, k, kc: (k, kc[1], t, 0)),
                  pl.BlockSpec((None, th, cols), lambda t, k, kc: (k, t, 0))],
        out_specs=[pl.BlockSpec((None, th, cols), lambda t, k, kc: (k, t, 0)),
                   pl.BlockSpec((th, cols), lambda t, k, kc: (t, 0))])
    return pl.pallas_call(
        body, name="add_sibling_w_in", grid_spec=grid_spec,
        out_shape=[SDS((N_CHIPS, h, cols), BF16), SDS((h, cols), F32)],
        compiler_params=_params(("parallel", "arbitrary")))(kc_idx, grad, recv)


def _gather_weights(kc_idx, w_in_slots, conv_slots, later):
    n_l = len(later)
    steps = SUBLANES

    def body(kc_ref, *refs):
        cast_ins, cast_outs = refs[:n_l], refs[n_l + 2:2 * n_l + 2]
        w_buf, conv_buf = refs[2 * n_l + 2], refs[2 * n_l + 3]
        sems = refs[2 * n_l + 4:]
        i = pl.program_id(0)

        def first_hop():
            x, y, c = _position()
            sends, arrivals = _gather_chip_hop([w_buf], sems)
            mine = conv_buf.at[2 * x + y]
            for j, (fx, fy) in enumerate(CHIP_FLIPS):
                tx, ty = _flip(x, fx), _flip(y, fy)
                there = conv_buf.at[2 * tx + ty]
                sends.append(_remote_copy(mine, mine, sems, 6 + j, (tx, ty, c)))
                arrivals.append(_remote_copy(there, there, sems, 6 + j, (tx, ty, c)))
            return sends, arrivals

        pl.when(i == 0)(functools.partial(_start_copies, first_hop))
        for src, dst in zip(cast_ins, cast_outs):
            dst[...] = src[...].astype(BF16)

        @pl.when(i == steps - 1)
        def _():
            sends, arrivals = first_hop()
            for cp in arrivals:
                cp.wait_recv()
            _start_copies(functools.partial(_gather_sibling_hop, [w_buf], sems))
            _finish_copies(functools.partial(_gather_sibling_hop, [w_buf], sems))
            for cp in sends:
                cp.wait_send()

    in_specs, out_specs, out_shape = [], [], []
    for w in later:
        r, c = w.shape
        in_specs.append(pl.BlockSpec((r // steps, c), lambda i, kc: (i, 0)))
        out_specs.append(pl.BlockSpec((None, r // steps, c), lambda i, kc: (kc[0], i, 0)))
        out_shape.append(SDS((N_CHIPS, r, c), BF16))
    grid_spec = pltpu.PrefetchScalarGridSpec(
        num_scalar_prefetch=1, grid=(steps,), in_specs=in_specs + [ANY, ANY], out_specs=out_specs + [ANY, ANY],
        scratch_shapes=[pltpu.SemaphoreType.DMA((9,)), pltpu.SemaphoreType.DMA((9,))])
    return pl.pallas_call(
        body, name="gather_weights", grid_spec=grid_spec,
        out_shape=out_shape + [SDS(w_in_slots.shape, w_in_slots.dtype), SDS(conv_slots.shape, conv_slots.dtype)],
        input_output_aliases={n_l + 1: n_l, n_l + 2: n_l + 1},
        compiler_params=_params(("arbitrary",)))(kc_idx, *later, w_in_slots, conv_slots)


def _exchange_with_sibling(partial):
    h = partial.shape[1] // 2

    def body(g_in, g_out, send_sems, recv_sems):
        x, y, c = _position()
        theirs = pl.ds(pl.multiple_of((1 - c) * h, h), h)
        copies = [_remote_copy(g_in.at[k, theirs], g_out.at[k], (send_sems, recv_sems), k, (x, y, 1 - c))
                  for k in range(N_CHIPS)]
        for cp in copies:
            cp.start()
        for cp in copies:
            cp.wait_recv()
        for cp in copies:
            cp.wait_send()

    return pl.pallas_call(
        body, name="exchange_with_sibling", in_specs=[ANY], out_specs=ANY,
        out_shape=SDS((N_CHIPS, h, partial.shape[2]), partial.dtype),
        scratch_shapes=[pltpu.SemaphoreType.DMA((N_CHIPS,)), pltpu.SemaphoreType.DMA((N_CHIPS,))])(partial)


def _finish_exchange(pieces, vec):
    n = len(pieces)
    n_dev = 2 * N_CHIPS

    def body(*refs):
        v_ref = refs[n]
        outs, o_ref = refs[n + 1:2 * n + 1], refs[2 * n + 1]
        buf, send_sems, recv_sems = refs[2 * n + 2:]
        x, y, c = _position()
        sibling = (x, y, 1 - c)
        me = 4 * x + 2 * y + c
        buf[me] = v_ref[...]
        started = []
        for f, (fx, fy, fc) in enumerate(DEVICE_FLIPS):
            cp = pltpu.make_async_remote_copy(
                src_ref=v_ref, dst_ref=buf.at[me], send_sem=send_sems.at[n + f], recv_sem=recv_sems.at[n + f],
                device_id=(_flip(x, fx), _flip(y, fy), _flip(c, fc)), device_id_type=MESH)
            cp.start()
            started.append(cp)
        for a in range(n):
            h = pieces[a].shape[0] // 2
            mine = outs[a].at[pl.ds(pl.multiple_of(c * h, h), h)]
            cp = pltpu.make_async_remote_copy(
                src_ref=mine, dst_ref=mine, send_sem=send_sems.at[a], recv_sem=recv_sems.at[a],
                device_id=sibling, device_id_type=MESH)
            cp.start()
            started.append(cp)
        for a in range(n):
            h = pieces[a].shape[0] // 2
            theirs = outs[a].at[pl.ds(pl.multiple_of((1 - c) * h, h), h)]
            pltpu.make_async_remote_copy(
                src_ref=theirs, dst_ref=theirs, send_sem=send_sems.at[a], recv_sem=recv_sems.at[a],
                device_id=sibling, device_id_type=MESH).wait_recv()
        for f, (fx, fy, fc) in enumerate(DEVICE_FLIPS):
            src = 4 * _flip(x, fx) + 2 * _flip(y, fy) + _flip(c, fc)
            pltpu.make_async_remote_copy(
                src_ref=v_ref, dst_ref=buf.at[src], send_sem=send_sems.at[n + f], recv_sem=recv_sems.at[n + f],
                device_id=(x, y, c), device_id_type=MESH).wait_recv()
        for cp in started:
            cp.wait_send()
        acc = buf[0]
        for d in range(1, n_dev):
            acc = acc + buf[d]
        o_ref[...] = acc

    vmem = pl.BlockSpec(memory_space=pltpu.VMEM)
    out_shape = [SDS(p.shape, p.dtype) for p in pieces] + [SDS(vec.shape, vec.dtype)]
    n_sems = n + n_dev - 1
    return pl.pallas_call(
        body, name="finish_exchange", in_specs=[ANY] * n + [vmem], out_specs=[ANY] * n + [vmem],
        out_shape=out_shape, input_output_aliases={a: a for a in range(n)},
        scratch_shapes=[pltpu.VMEM((n_dev,) + vec.shape, vec.dtype), pltpu.SemaphoreType.DMA((n_sems,)),
                        pltpu.SemaphoreType.DMA((n_sems,))])(*pieces, vec)


def _constants():
    r = jnp.arange(2 * KEY_BLOCK)[:, None] % KEY_BLOCK
    c = jnp.arange(2 * KEY_BLOCK)[None, :]
    later = jnp.where(c < KEY_BLOCK, r > c, True).astype(BF16)
    earlier = jnp.where(c < KEY_BLOCK, r < c, True).astype(BF16)
    upto = jnp.where(c < KEY_BLOCK, r <= c, True).astype(BF16)
    gr = (jnp.arange(2 * LANES)[:, None] % LANES) // GROUP
    gc = jnp.arange(LANES)[None, :] // GROUP
    gmat = (gr == gc).astype(BF16)
    return later, jnp.stack([earlier, upto]), gmat


def _rows(v):
    return v.reshape(-1, LANES)


def kernel(x, w_in, conv_w, g_conv, g_attn, w_out, ln1_g, ln1_b, w_up, w_down, ln2_g, ln2_b, loss_target, m_w_in, m_conv_w, m_g_conv, m_g_attn, m_w_out, m_ln1_g, m_ln1_b, m_w_up, m_w_down, m_ln2_g, m_ln2_b, v_w_in, v_conv_w, v_g_conv, v_g_attn, v_w_out, v_ln1_g, v_ln1_b, v_w_up, v_w_down, v_ln2_g, v_ln2_b):
    xs, target = x[0], loss_target[0]
    mesh_x, mesh_y, mesh_c = _position()
    k_idx = 2 * mesh_x + mesh_y
    kc_idx = jnp.stack([k_idx, mesh_c]).astype(jnp.int32)
    tri_later, tri_earlier, gmat = _constants()

    w_in_b = _cast_into_slot(kc_idx, w_in[0], "cast_w_in")
    conv_slot = jnp.pad(conv_w, ((0, 0), (0, SUBLANES - conv_w.shape[1]), (0, 0)))
    conv_b = lax.dynamic_update_slice(jnp.zeros((N_CHIPS, SUBLANES, LANES), F32), conv_slot, (k_idx, 0, 0))
    w_out_b, w_up_b, w_down_b, w_in_f, conv_f = _gather_weights(
        kc_idx, w_in_b, conv_b, [w_out[0], w_up[0], w_down[0]])
    taps = jnp.transpose(conv_f, (1, 0, 2)).reshape(SUBLANES, CONV_WIDTH)

    gates, qkv, xs_b = _proj(xs, w_in_f)
    ycn = _conv_fwd(gates, taps, g_conv, gmat)
    o, yan, tot, cut, w_out_f, w_up_f, w_down_f = _attn_fwd(
        qkv, g_attn, tri_later, gmat, [w_out_b, w_up_b, w_down_b])
    w_out_f = w_out_f.reshape(D_MODEL, D_MODEL)
    x1, xhat1, rstd1, x1_b = _mix_ln1(ycn, yan, w_out_f, xs, ln1_g, ln1_b)
    dpre2, ln2_sums, loss_sum, relu_up, hid, dpre2_b = _mlp_fwd_loss(x1, w_up_f, w_down_f, target, ln2_g, ln2_b)

    dup, dpre1, ln1_sums, dpre1_b, dycn, dyan = _mlp_bwd_ln1(
        relu_up, dpre2, w_up_f, w_down_f, xhat1, rstd1, ln1_g, w_out_f)
    gw_up = _grad_tn(x1_b, dup, "grad_w_up", FF_SHARD, True)
    gw_down = [g.reshape(N_CHIPS, FF_SHARD, D_MODEL) for g in _grad_tn(hid, dpre2_b, "grad_w_down", D_MODEL, False)]
    gw_out = [g.reshape(N_CHIPS, D_MODEL // N_CHIPS, D_MODEL) for g in _grad_w_out(ycn, yan, dpre1_b)]
    dq, dk, dv, gattn_sums, dbg, dy, conv_sums, recv_out, recv_up, recv_down = _attn_bwd(
        qkv, o, tot, dyan, g_attn, tri_earlier, gmat, cut, gates, dycn, taps, g_conv,
        [gw_out[1], gw_up[1], gw_down[1]])
    dproj = _dproj_assemble(gates, dy, dbg, dq, dk, dv, taps)
    gw_in = _grad_tn(xs_b, dproj, "grad_w_in", IN_SHARD, True)
    halves = lambda g: g.reshape(N_CHIPS, 2, g.shape[1] // 2, g.shape[2])
    chip_sums, own_sum = _add_sibling(kc_idx, halves(gw_in[0]), _exchange_with_sibling(gw_in[1]))
    grad_x, recv_in, p_out, p_up, p_down = _grad_x(
        kc_idx, dproj, w_in_f, dpre1, chip_sums,
        [(halves(gw_out[0]), recv_out), (halves(gw_up[0]), recv_up), (halves(gw_down[0]), recv_down)])
    pieces = [_sum_partials(kc_idx, own_sum, recv_in, "sum_partials_w_in"), p_out, p_up, p_down]
    conv_rows = jnp.transpose(conv_sums[0:3].reshape(3, N_CHIPS, LANES), (1, 0, 2)).reshape(3 * N_CHIPS, LANES)
    small = jnp.concatenate([
        loss_sum, _rows(conv_sums[3]), _rows(gattn_sums[0]), _rows(ln1_sums[0]), _rows(ln1_sums[1]),
        _rows(ln2_sums[0]), _rows(ln2_sums[1]), conv_rows,
        jnp.zeros((SMALL_ROWS - ROW_CONVW - 3 * N_CHIPS, LANES), F32)], axis=0)
    g_w_in, g_w_out, g_w_up, g_w_down, total = _finish_exchange(pieces, small)
    loss = total[ROW_LOSS, 0]
    g_conv_w = lax.dynamic_slice(total, (ROW_CONVW + 3 * k_idx, 0), (3, LANES))

    small_names = ["g_conv", "g_attn", "ln1_g", "ln1_b", "ln2_g", "ln2_b", "conv_w"]
    small_w = [g_conv, g_attn, ln1_g, ln1_b, ln2_g, ln2_b, conv_w]
    small_m = [m_g_conv, m_g_attn, m_ln1_g, m_ln1_b, m_ln2_g, m_ln2_b, m_conv_w]
    small_v = [v_g_conv, v_g_attn, v_ln1_g, v_ln1_b, v_ln2_g, v_ln2_b, v_conv_w]
    small_out = dict(zip(small_names, _adamw_small(
        total, g_conv_w, [_rows(a) for a in small_w], [_rows(a) for a in small_m], [_rows(a) for a in small_v])))
    small_shape = dict(zip(small_names, (a.shape for a in small_w)))
    big_out = {
        "w_in": _adamw(w_in[0], g_w_in, m_w_in[0], v_w_in[0], "adamw_w_in"),
        "w_out": _adamw(w_out[0], g_w_out, m_w_out[0], v_w_out[0], "adamw_w_out"),
        "w_up": _adamw(w_up[0], g_w_up, m_w_up[0], v_w_up[0], "adamw_w_up"),
        "w_down": _adamw(w_down[0], g_w_down, m_w_down[0], v_w_down[0], "adamw_w_down"),
    }
    order = ["w_in", "conv_w", "g_conv", "g_attn", "w_out", "ln1_g", "ln1_b", "w_up", "w_down", "ln2_g", "ln2_b"]

    def leaf(kind, name):
        if name in big_out:
            return big_out[name][kind][None]
        return small_out[name][kind].reshape(small_shape[name])

    outs = [loss, grad_x[None]]
    for kind in range(4):
        outs.extend(leaf(kind, name) for name in order)
    return tuple(outs)
```

```python
import functools

import jax
import jax.numpy as jnp
from jax import lax
from jax.experimental import pallas as pl
from jax.experimental.pallas import tpu as pltpu

F32 = jnp.float32
BF16 = jnp.bfloat16
SDS = jax.ShapeDtypeStruct

D_MODEL = 1024
CONV_WIDTH = 512
ATTN_WIDTH = 512
GROUP = 64
GATE_COLS = 3 * CONV_WIDTH
QKV_COLS = 3 * ATTN_WIDTH
IN_COLS = GATE_COLS + QKV_COLS
D_FF = 4 * D_MODEL
N_CHIPS = 4
IN_SHARD = IN_COLS // N_CHIPS
FF_SHARD = D_FF // N_CHIPS
ALPHA = float(2.0 ** 0.25)
LN_EPS = 1e-5
RMS_EPS = 1e-6
ATTN_SCALE = GROUP ** -0.5
LOG2_E = 1.4426950408889634
ADAM_LR = 0.001
ADAM_B1 = 0.9
ADAM_B2 = 0.999
ADAM_EPS = 1e-08
ADAM_WD = 0.01
ADAM_STEP = 10

LANES = 128
SUBLANES = 8
KEY_BLOCK = 128
ATTN_Q_TILE = 512
ATTN_KEY_BLOCKS = 2
ATTN_DIAG_GROUPS = 2
ATTN_DEAD_LOG2 = 200.0
VMEM_LIMIT = 56 * 1024 * 1024
ROW_TILE = 512
MLP_ROW_TILE = 256
GRAD_SEQ_TILE = 2048
SHARD_ROW_TILE = 256
SUM_ROW_TILE = 128
CAST_STEPS = 8

MESH = pl.DeviceIdType.MESH
CHIP_FLIPS = ((1, 0), (0, 1), (1, 1))
DEVICE_FLIPS = tuple((fx, fy, fc) for fx in (0, 1) for fy in (0, 1) for fc in (0, 1))[1:]
NT_DIMS = (((1,), (1,)), ((), ()))
TN_DIMS = (((0,), (0,)), ((), ()))

ROW_LOSS = 0
ROW_GCONV = 8
ROW_GATTN = 12
ROW_LN1G = 16
ROW_LN1B = 24
ROW_LN2G = 32
ROW_LN2B = 40
ROW_CONVW = 48
SMALL_ROWS = 64


def _params(sem=None):
    return pltpu.CompilerParams(dimension_semantics=sem, vmem_limit_bytes=VMEM_LIMIT)


def _flip(v, f):
    return 1 - v if f else v


def _position():
    return lax.axis_index("x"), lax.axis_index("y"), lax.axis_index("c")


def _hilo(v):
    hi = v.astype(BF16)
    lo = (v - hi.astype(F32)).astype(BF16)
    return jnp.concatenate([hi, lo], axis=1)


def _hilo_dot(v, mat):
    return jnp.dot(_hilo(v), mat, preferred_element_type=F32)


def _group_sum(v, gmat):
    parts = [_hilo_dot(v[:, LANES * j:LANES * (j + 1)], gmat) for j in range(v.shape[1] // LANES)]
    return parts[0] if len(parts) == 1 else jnp.concatenate(parts, axis=1)


def _softplus_terms(z):
    sp = jnp.log2(1.0 + jnp.exp2(-jnp.abs(z)))
    log_beta = jnp.minimum(z, 0.0) - sp
    return log_beta, log_beta - z


def _layer_norm_fwd(pre, g, b):
    mu = jnp.mean(pre, axis=-1, keepdims=True)
    d = pre - mu
    var = jnp.mean(d * d, axis=-1, keepdims=True)
    rstd = lax.rsqrt(var + LN_EPS)
    xhat = d * rstd
    return xhat * g + b, xhat, rstd


def _layer_norm_bwd(dy, xhat, rstd, g):
    dxh = dy * g
    m1 = jnp.mean(dxh, axis=-1, keepdims=True)
    m2 = jnp.mean(dxh * xhat, axis=-1, keepdims=True)
    return rstd * (dxh - m1 - xhat * m2)


def _row_tile(s, want):
    return min(s, want)


def _cast_into_slot(kc_idx, w, name):
    r, c = w.shape
    tr = _row_tile(r, SHARD_ROW_TILE)

    def body(kc_ref, w_ref, o_ref):
        o_ref[...] = w_ref[...].astype(BF16)

    grid_spec = pltpu.PrefetchScalarGridSpec(
        num_scalar_prefetch=1, grid=(r // tr,),
        in_specs=[pl.BlockSpec((tr, c), lambda i, kc: (i, 0))],
        out_specs=pl.BlockSpec((None, tr, c), lambda i, kc: (kc[0], i, 0)))
    return pl.pallas_call(
        body, name=name, grid_spec=grid_spec, out_shape=SDS((N_CHIPS, r, c), BF16),
        compiler_params=_params(("parallel",)))(kc_idx, w)


def _proj(x, w_in, taps, g_conv, gmat):
    s = x.shape[0]
    tm = _row_tile(s, ROW_TILE)

    def body(x_ref, w_ref, taps_ref, gain_ref, gmat_ref, gates_ref, qkv_ref, xb_ref, ycn_ref, halo_ref):
        i = pl.program_id(0)
        xb = x_ref[...].astype(BF16)
        xb_ref[...] = xb
        for k in range(N_CHIPS):
            acc = jnp.dot(xb, w_ref[k], preferred_element_type=F32)
            if k < 2:
                gates_ref[:, IN_SHARD * k:IN_SHARD * (k + 1)] = acc
            else:
                qkv_ref[:, IN_SHARD * (k - 2):IN_SHARD * (k - 1)] = acc.astype(BF16)
        bg, _, _, _, _, _, y = _conv_forward_values(gates_ref, halo_ref, taps_ref, i == 0)
        yc = bg * y
        ms = _group_sum(yc * yc, gmat_ref[...]) * (1.0 / GROUP)
        ycn_ref[...] = (yc * lax.rsqrt(ms + RMS_EPS) * gain_ref[...]).astype(BF16)
        halo_ref[...] = gates_ref[tm - SUBLANES:tm, :]

    return pl.pallas_call(
        body, name="proj", grid=(s // tm,),
        in_specs=[pl.BlockSpec((tm, D_MODEL), lambda i: (i, 0)),
                  pl.BlockSpec((N_CHIPS, D_MODEL, IN_SHARD), lambda i: (0, 0, 0)),
                  pl.BlockSpec((SUBLANES, CONV_WIDTH), lambda i: (0, 0)),
                  pl.BlockSpec((1, CONV_WIDTH), lambda i: (0, 0)),
                  pl.BlockSpec((2 * LANES, LANES), lambda i: (0, 0))],
        out_specs=[pl.BlockSpec((tm, GATE_COLS), lambda i: (i, 0)),
                   pl.BlockSpec((tm, QKV_COLS), lambda i: (i, 0)),
                   pl.BlockSpec((tm, D_MODEL), lambda i: (i, 0)),
                   pl.BlockSpec((tm, CONV_WIDTH), lambda i: (i, 0))],
        out_shape=[SDS((s, GATE_COLS), F32), SDS((s, QKV_COLS), BF16), SDS((s, D_MODEL), BF16),
                   SDS((s, CONV_WIDTH), BF16)],
        scratch_shapes=[pltpu.VMEM((SUBLANES, GATE_COLS), F32)],
        compiler_params=_params(("arbitrary",)))(x, w_in, taps, g_conv, gmat)


def _conv_forward_values(g_ref, halo_ref, taps_ref, first_block):
    gates = g_ref[...]
    tr = gates.shape[0]
    bg = gates[:, :CONV_WIDTH]
    cg = gates[:, CONV_WIDTH:2 * CONV_WIDTH]
    h = gates[:, 2 * CONV_WIDTH:]
    u = cg * h

    def prev(r):
        v = halo_ref[r:r + 1, CONV_WIDTH:2 * CONV_WIDTH] * halo_ref[r:r + 1, 2 * CONV_WIDTH:GATE_COLS]
        return jnp.where(first_block, 0.0, v)

    row = lax.broadcasted_iota(jnp.int32, (tr, CONV_WIDTH), 0)
    u1 = jnp.where(row == 0, prev(7), pltpu.roll(u, 1, 0))
    u2 = jnp.where(row == 0, prev(6), jnp.where(row == 1, prev(7), pltpu.roll(u, 2, 0)))
    y = taps_ref[0:1, :] * u2 + taps_ref[1:2, :] * u1 + taps_ref[2:3, :] * u
    return bg, cg, h, u, u1, u2, y


def _conv_bwd_gate_step(first_block, g_ref, halo_ref, dn_ref, taps_ref, gain_ref, gmat_v, dbg_ref, dy_ref, sums_ref):
    bg, _, _, u, u1, u2, y = _conv_forward_values(g_ref, halo_ref, taps_ref, first_block)
    yc = bg * y
    rstd = lax.rsqrt(_group_sum(yc * yc, gmat_v) * (1.0 / GROUP) + RMS_EPS)
    n = yc * rstd
    dout = dn_ref[...]
    dn = dout * gain_ref[...]
    dyc = rstd * (dn - n * (_group_sum(dn * n, gmat_v) * (1.0 / GROUP)))
    dbg_ref[...] = (dyc * y).astype(BF16)
    dy = dyc * bg
    dy_ref[...] = dy

    @pl.when(first_block)
    def _():
        sums_ref[...] = jnp.zeros_like(sums_ref)

    sums_ref[0:1, :] += jnp.sum(dy * u2, axis=0, keepdims=True)
    sums_ref[1:2, :] += jnp.sum(dy * u1, axis=0, keepdims=True)
    sums_ref[2:3, :] += jnp.sum(dy * u, axis=0, keepdims=True)
    sums_ref[3:4, :] += jnp.sum(dout * n, axis=0, keepdims=True)


def _dproj_assemble(gates, dy, dbg, dq, dk, dv, taps):
    s = gates.shape[0]
    tr = _row_tile(s, ROW_TILE)
    hb = tr // SUBLANES
    last = s // SUBLANES - 1
    n_blocks = s // tr

    def body(g_ref, dy_ref, halo_ref, dbg_ref, dq_ref, dk_ref, dv_ref, taps_ref, out_ref):
        i = pl.program_id(0)
        gates_v = g_ref[...]
        cg = gates_v[:, CONV_WIDTH:2 * CONV_WIDTH]
        h = gates_v[:, 2 * CONV_WIDTH:]
        dy_v = dy_ref[...]
        last_block = i == n_blocks - 1
        nxt = lambda r: jnp.where(last_block, 0.0, halo_ref[r:r + 1, :])
        row = lax.broadcasted_iota(jnp.int32, (tr, CONV_WIDTH), 0)
        d1 = jnp.where(row == tr - 1, nxt(0), pltpu.roll(dy_v, tr - 1, 0))
        d2 = jnp.where(row == tr - 1, nxt(1), jnp.where(row == tr - 2, nxt(0), pltpu.roll(dy_v, tr - 2, 0)))
        du = taps_ref[2:3, :] * dy_v + taps_ref[1:2, :] * d1 + taps_ref[0:1, :] * d2
        out_ref[:, 0:CONV_WIDTH] = dbg_ref[...]
        out_ref[:, CONV_WIDTH:2 * CONV_WIDTH] = (du * h).astype(BF16)
        out_ref[:, 2 * CONV_WIDTH:GATE_COLS] = (du * cg).astype(BF16)
        out_ref[:, GATE_COLS:GATE_COLS + ATTN_WIDTH] = dq_ref[...]
        out_ref[:, GATE_COLS + ATTN_WIDTH:GATE_COLS + 2 * ATTN_WIDTH] = dk_ref[...]
        out_ref[:, GATE_COLS + 2 * ATTN_WIDTH:] = dv_ref[...]

    row_spec = lambda w: pl.BlockSpec((tr, w), lambda i: (i, 0))
    return pl.pallas_call(
        body, name="dproj_assemble", grid=(s // tr,),
        in_specs=[row_spec(GATE_COLS), row_spec(CONV_WIDTH),
                  pl.BlockSpec((SUBLANES, CONV_WIDTH), lambda i: (jnp.minimum((i + 1) * hb, last), 0)),
                  row_spec(CONV_WIDTH), row_spec(ATTN_WIDTH), row_spec(ATTN_WIDTH), row_spec(ATTN_WIDTH),
                  pl.BlockSpec((SUBLANES, CONV_WIDTH), lambda i: (0, 0))],
        out_specs=row_spec(IN_COLS),
        out_shape=SDS((s, IN_COLS), BF16),
        compiler_params=_params(("parallel",)))(gates, dy, dy, dbg, dq, dk, dv, taps)


def _stack_heads(rows, nb):
    lane = lax.broadcasted_iota(jnp.int32, (1, LANES), 1)
    zero = jnp.zeros((KEY_BLOCK, LANES), rows.dtype)
    parts = []
    for blk in range(nb):
        r = rows[blk * KEY_BLOCK:(blk + 1) * KEY_BLOCK]
        parts.append(jnp.where(lane < GROUP, r, zero))
        parts.append(jnp.where(lane < GROUP, zero, r))
    return jnp.concatenate(parts, axis=0)


def _stack_hilo(v, n_cols):
    return jnp.concatenate([_hilo(v[:, c * KEY_BLOCK:(c + 1) * KEY_BLOCK]) for c in range(n_cols)], axis=0)


def _causal_mask(tq, nb, diag_base):
    shape = (tq, 2 * nb * KEY_BLOCK)
    row = lax.broadcasted_iota(jnp.int32, shape, 0)
    col = lax.broadcasted_iota(jnp.int32, shape, 1)
    key = diag_base + (col // (2 * KEY_BLOCK)) * KEY_BLOCK + col % KEY_BLOCK
    return key < row


ANY = pl.BlockSpec(memory_space=pl.ANY)


def _remote_copy(src, dst, sems, idx, target):
    return pltpu.make_async_remote_copy(src_ref=src, dst_ref=dst, send_sem=sems[0].at[idx], recv_sem=sems[1].at[idx],
                                        device_id=target, device_id_type=MESH)


def _gather_chip_hop(bufs, sems):
    x, y, c = _position()
    sends, arrivals = [], []
    for a, buf in enumerate(bufs):
        h = buf.shape[1] // 2
        rows = pl.ds(pl.multiple_of(c * h, h), h)
        mine = buf.at[2 * x + y, rows]
        for j, (fx, fy) in enumerate(CHIP_FLIPS):
            tx, ty = _flip(x, fx), _flip(y, fy)
            there = buf.at[2 * tx + ty, rows]
            sends.append(_remote_copy(mine, mine, sems, 6 * a + j, (tx, ty, c)))
            arrivals.append(_remote_copy(there, there, sems, 6 * a + j, (tx, ty, c)))
    return sends, arrivals


def _gather_sibling_hop(bufs, sems):
    x, y, c = _position()
    sends, arrivals = [], []
    for a, buf in enumerate(bufs):
        h = buf.shape[1] // 2
        mine, theirs = pl.ds(pl.multiple_of(c * h, h), h), pl.ds(pl.multiple_of((1 - c) * h, h), h)
        for j, (fx, fy) in enumerate(CHIP_FLIPS):
            kj = 2 * _flip(x, fx) + _flip(y, fy)
            landed, other = buf.at[kj, mine], buf.at[kj, theirs]
            sends.append(_remote_copy(landed, landed, sems, 6 * a + 3 + j, (x, y, 1 - c)))
            arrivals.append(_remote_copy(other, other, sems, 6 * a + 3 + j, (x, y, 1 - c)))
    return sends, arrivals


def _reduce_copies(ins, outs, sems):
    x, y, c = _position()
    sends, arrivals = [], []
    for a in range(len(ins)):
        h = ins[a].shape[1] // 2
        for f, (fx, fy, fc) in enumerate(DEVICE_FLIPS):
            tx, ty, tc = _flip(x, fx), _flip(y, fy), _flip(c, fc)
            src = ins[a].at[2 * tx + ty, pl.ds(pl.multiple_of(tc * h, h), h)]
            sends.append(_remote_copy(src, outs[a].at[f], sems, 7 * a + f, (tx, ty, tc)))
            arrivals.append(_remote_copy(outs[a].at[f], outs[a].at[f], sems, 7 * a + f, (tx, ty, tc)))
    return sends, arrivals


def _chip_reduce_copies(src, dst, sems):
    x, y, c = _position()
    sends, arrivals = [], []
    for j, (fx, fy) in enumerate(CHIP_FLIPS):
        tx, ty = _flip(x, fx), _flip(y, fy)
        sends.append(_remote_copy(src.at[2 * tx + ty], dst.at[j], sems, j, (tx, ty, c)))
        arrivals.append(_remote_copy(dst.at[j], dst.at[j], sems, j, (tx, ty, c)))
    return sends, arrivals


def _start_copies(make):
    sends, _ = make()
    for cp in sends:
        cp.start()


def _finish_copies(make):
    sends, arrivals = make()
    for cp in arrivals:
        cp.wait_recv()
    for cp in sends:
        cp.wait_send()


def _attn_fwd(qkv, g_attn, tri, gmat, shards):
    n_w = len(shards)
    s = qkv.shape[0]
    tq = _row_tile(s, ATTN_Q_TILE)
    tk = KEY_BLOCK
    nb = ATTN_KEY_BLOCKS
    width = nb * tk
    n_groups = ATTN_DIAG_GROUPS
    group = tq // n_groups
    pairs = ATTN_WIDTH // LANES

    def body(q_ref, k_ref, v_ref, gain_ref, tri_ref, gmat_ref, *rest):
        o_ref, yn_ref, tot_ref, cut_ref = rest[n_w:n_w + 4]
        w_bufs, sems = rest[n_w + 4:2 * n_w + 4], rest[2 * n_w + 4:]
        chip_hop = functools.partial(_gather_chip_hop, w_bufs, sems)
        sibling_hop = functools.partial(_gather_sibling_hop, w_bufs, sems)
        p, i = pl.program_id(0), pl.program_id(1)
        pl.when((p == 0) & (i == 0))(functools.partial(_start_copies, chip_hop))

        @pl.when((p == pairs - 1) & (i == 0))
        def _():
            for cp in chip_hop()[1]:
                cp.wait_recv()
            _start_copies(sibling_hop)

        q2 = q_ref[...]
        tri_v = tri_ref[...]

        def trip(s0, n_blk, rows, carry, diag_base):
            r0, nr = rows
            run = [carry[0], carry[1]]
            oacc = carry[2]
            ksel = _stack_heads(k_ref[pl.ds(s0, n_blk * tk), :], n_blk)
            vsel = _stack_heads(v_ref[pl.ds(s0, n_blk * tk), :], n_blk)
            z = lax.dot_general(q2[r0:r0 + nr], ksel, NT_DIMS, preferred_element_type=F32) * (ATTN_SCALE * LOG2_E)
            log_beta, log_keep = _softplus_terms(z)
            if diag_base is not None:
                valid = _causal_mask(nr, n_blk, diag_base)
                log_keep = jnp.where(valid, log_keep, 0.0)
            ct = jnp.dot(_stack_hilo(log_keep, 2 * n_blk), tri_v, preferred_element_type=F32)
            a_parts = [None] * (2 * n_blk)
            for c in reversed(range(2 * n_blk)):
                h = c % 2
                ct_c = ct[c * nr:(c + 1) * nr]
                a_parts[c] = jnp.exp2(log_beta[:, c * tk:(c + 1) * tk] + ct_c[:, :tk] + run[h])
                run[h] = run[h] + ct_c[:, tk:]
            a = jnp.concatenate(a_parts, axis=1)
            if diag_base is not None:
                a = jnp.where(valid, a, 0.0)
            oacc = oacc + jnp.dot(a.astype(BF16), vsel, preferred_element_type=F32)
            return run[0], run[1], oacc

        n_full = i * (tq // width)
        tile = p * pl.num_programs(1) + i

        def alive(run_a, run_b):
            return jnp.max(jnp.maximum(run_a, run_b)) > -ATTN_DEAD_LOG2

        groups = []
        for g in range(n_groups):
            rows = (g * group, group)
            zeros = (jnp.zeros((group, tk), F32), jnp.zeros((group, tk), F32), jnp.zeros((group, LANES), F32))
            state = trip(pl.multiple_of(i * tq, tq), (g + 1) * group // tk, rows, zeros, -g * group)

            def earlier_trip(c, rows=rows):
                done, _, run_a, run_b, oacc = c
                s0 = pl.multiple_of((n_full - 1 - done) * width, width)
                run_a, run_b, oacc = trip(s0, nb, rows, (run_a, run_b, oacc), None)
                return done + 1, alive(run_a, run_b), run_a, run_b, oacc

            swept = lax.while_loop(lambda c: (c[0] < n_full) & c[1], earlier_trip,
                                   (jnp.int32(0), alive(state[0], state[1])) + state)
            cut_ref[n_groups * tile + g] = (n_full - swept[0]).astype(F32)
            groups.append(swept[2:])
        run_a, run_b, oacc = (jnp.concatenate([grp[j] for grp in groups], axis=0) for j in range(3))
        lane = lax.broadcasted_iota(jnp.int32, (1, LANES), 1)
        o_ref[...] = oacc
        tot_ref[...] = jnp.where(lane < GROUP, run_a, run_b)
        ms = _group_sum(oacc * oacc, gmat_ref[...]) * (1.0 / GROUP)
        yn_ref[...] = (oacc * lax.rsqrt(ms + RMS_EPS) * gain_ref[...]).astype(BF16)

        @pl.when((p == pairs - 1) & (i == pl.num_programs(1) - 1))
        def _():
            for cp in chip_hop()[0]:
                cp.wait_send()
            _finish_copies(sibling_hop)

    blk = lambda: pl.BlockSpec((tq, LANES), lambda p, i: (i, p))
    return pl.pallas_call(
        body, name="attn_fwd", grid=(pairs, s // tq),
        in_specs=[pl.BlockSpec((tq, LANES), lambda p, i: (i, p)),
                  pl.BlockSpec((s, LANES), lambda p, i: (0, pairs + p)),
                  pl.BlockSpec((s, LANES), lambda p, i: (0, 2 * pairs + p)),
                  pl.BlockSpec((1, LANES), lambda p, i: (0, p)),
                  pl.BlockSpec((2 * tk, 2 * tk), lambda p, i: (0, 0)),
                  pl.BlockSpec((2 * LANES, LANES), lambda p, i: (0, 0))] + [ANY] * n_w,
        out_specs=[blk(), blk(), blk(), pl.BlockSpec(memory_space=pltpu.SMEM)] + [ANY] * n_w,
        out_shape=[SDS((s, ATTN_WIDTH), F32), SDS((s, ATTN_WIDTH), BF16), SDS((s, ATTN_WIDTH), F32),
                   SDS((n_groups * pairs * (s // tq),), F32)]
        + [SDS(w.shape, w.dtype) for w in shards],
        input_output_aliases={6 + a: 4 + a for a in range(n_w)},
        scratch_shapes=[pltpu.SemaphoreType.DMA((6 * n_w,)), pltpu.SemaphoreType.DMA((6 * n_w,))],
        compiler_params=_params(("arbitrary", "arbitrary")))(qkv, qkv, qkv, g_attn, tri, gmat, *shards)


def _attn_bwd(qkv, o, tot, dyn, g_attn, tri, gmat, cut, gates, dycn, taps, g_conv, partials):
    n_g = len(partials)
    s = qkv.shape[0]
    tq = _row_tile(s, ATTN_Q_TILE)
    tk = KEY_BLOCK
    nb = ATTN_KEY_BLOCKS
    width = nb * tk
    n_groups = ATTN_DIAG_GROUPS
    group = tq // n_groups
    pairs = ATTN_WIDTH // LANES

    def body(q_ref, k_ref, v_ref, o_ref, tot_ref, dyn_ref, gain_ref, tri_ref, gmat_ref, cut_ref,
             gates_ref, halo_ref, dycn_ref, taps_ref, gconv_ref, *rest):
        g_ins, (dq_ref, dk_out, dv_out, dg_ref) = rest[:n_g], rest[n_g:n_g + 4]
        dbg_ref, dy_ref, conv_sums_ref = rest[n_g + 4:n_g + 7]
        g_outs, sems = rest[n_g + 7:2 * n_g + 7], rest[2 * n_g + 7:2 * n_g + 9]
        dk_ref, dv_ref = rest[2 * n_g + 9:]
        copies = functools.partial(_reduce_copies, g_ins, g_outs, sems)
        p, i = pl.program_id(0), pl.program_id(1)
        pl.when((p == 0) & (i == 0))(functools.partial(_start_copies, copies))

        @pl.when(i == 0)
        def _():
            dk_ref[...] = jnp.zeros_like(dk_ref)
            dv_ref[...] = jnp.zeros_like(dv_ref)
            dg_ref[...] = jnp.zeros_like(dg_ref)

        gmat_v = gmat_ref[...]
        _conv_bwd_gate_step((p == 0) & (i == 0), gates_ref, halo_ref, dycn_ref, taps_ref, gconv_ref, gmat_v,
                            dbg_ref, dy_ref, conv_sums_ref)
        o_v = o_ref[...]
        rstd = lax.rsqrt(_group_sum(o_v * o_v, gmat_v) * (1.0 / GROUP) + RMS_EPS)
        n = o_v * rstd
        dout = dyn_ref[...]
        dg_ref[0:1, :] += jnp.sum(dout * n, axis=0, keepdims=True)
        dn = dout * gain_ref[...]
        do2 = (rstd * (dn - n * (_group_sum(dn * n, gmat_v) * (1.0 / GROUP)))).astype(BF16)
        q2 = q_ref[...]
        tot_v = tot_ref[...]
        tots = (jnp.broadcast_to(tot_v[:, 0:1], (tq, tk)), jnp.broadcast_to(tot_v[:, GROUP:GROUP + 1], (tq, tk)))
        tri_v, tri_incl_v = tri_ref[0], tri_ref[1]
        lane = lax.broadcasted_iota(jnp.int32, (1, LANES), 1)

        def trip(s0, n_blk, rows, carry, diag_base):
            r0, nr = rows
            rest_l = [carry[0], carry[1]]
            pref_g = [carry[2], carry[3]]
            dq = carry[4]
            q_rows, do_rows = q2[r0:r0 + nr], do2[r0:r0 + nr]
            ksel = _stack_heads(k_ref[pl.ds(s0, n_blk * tk), :], n_blk)
            vsel = _stack_heads(v_ref[pl.ds(s0, n_blk * tk), :], n_blk)
            z = lax.dot_general(q_rows, ksel, NT_DIMS, preferred_element_type=F32) * (ATTN_SCALE * LOG2_E)
            log_beta, log_keep = _softplus_terms(z)
            if diag_base is not None:
                valid = _causal_mask(nr, n_blk, diag_base)
                log_keep = jnp.where(valid, log_keep, 0.0)
            ctl = jnp.dot(_stack_hilo(log_keep, 2 * n_blk), tri_incl_v, preferred_element_type=F32)
            da = lax.dot_general(do_rows, vsel, NT_DIMS, preferred_element_type=F32)
            a_parts = []
            for c in range(2 * n_blk):
                h = c % 2
                ct_c = ctl[c * nr:(c + 1) * nr]
                cols = slice(c * tk, (c + 1) * tk)
                a_parts.append(jnp.exp2(log_beta[:, cols] + (rest_l[h] - ct_c[:, :tk])))
                rest_l[h] = rest_l[h] - ct_c[:, tk:]
            a = jnp.concatenate(a_parts, axis=1)
            if diag_base is not None:
                a = jnp.where(valid, a, 0.0)
            g = a * da
            ctg = jnp.dot(_stack_hilo(g, 2 * n_blk), tri_v, preferred_element_type=F32)
            dz_parts = []
            for c in range(2 * n_blk):
                h = c % 2
                ct_c = ctg[c * nr:(c + 1) * nr]
                cols = slice(c * tk, (c + 1) * tk)
                prefix = pref_g[h] + ct_c[:, :tk]
                pref_g[h] = pref_g[h] + ct_c[:, tk:]
                g_c = g[:, cols]
                dz_parts.append(g_c - jnp.exp2(log_beta[:, cols]) * (g_c + prefix))
            dz = jnp.concatenate(dz_parts, axis=1) * ATTN_SCALE
            if diag_base is not None:
                dz = jnp.where(valid, dz, 0.0)
            dzb = dz.astype(BF16)
            dq = dq + jnp.dot(dzb, ksel, preferred_element_type=F32)
            dkt = lax.dot_general(dzb, q_rows, TN_DIMS, preferred_element_type=F32)
            dvt = lax.dot_general(a.astype(BF16), do_rows, TN_DIMS, preferred_element_type=F32)
            for blk in range(n_blk):
                ra, rb = slice(2 * blk * tk, (2 * blk + 1) * tk), slice((2 * blk + 1) * tk, (2 * blk + 2) * tk)
                keys = pl.ds(pl.multiple_of(s0 + blk * tk, tk), tk)
                dk_ref[keys, :] += jnp.where(lane < GROUP, dkt[ra], dkt[rb])
                dv_ref[keys, :] += jnp.where(lane < GROUP, dvt[ra], dvt[rb])
            return rest_l[0], rest_l[1], pref_g[0], pref_g[1], dq

        n_full = i * (tq // width)
        tile = p * pl.num_programs(1) + i
        dq_groups = []
        for g in range(n_groups):
            rows = (g * group, group)
            first = jnp.clip(cut_ref[n_groups * tile + g].astype(jnp.int32), 0, n_full)
            zeros_qk = jnp.zeros((group, tk), F32)
            carry = (tots[0][g * group:(g + 1) * group], tots[1][g * group:(g + 1) * group], zeros_qk, zeros_qk,
                     jnp.zeros((group, LANES), F32))
            carry = lax.fori_loop(
                first, n_full,
                lambda t, c, rows=rows: trip(pl.multiple_of(t * width, width), nb, rows, c, None), carry)
            dq_groups.append(trip(pl.multiple_of(i * tq, tq), (g + 1) * group // tk, rows, carry, -g * group)[4])
        dq_ref[...] = jnp.concatenate(dq_groups, axis=0).astype(BF16)

        @pl.when(i == pl.num_programs(1) - 1)
        def _():
            dk_out[...] = dk_ref[...].astype(BF16)
            dv_out[...] = dv_ref[...].astype(BF16)

        pl.when((p == pairs - 1) & (i == pl.num_programs(1) - 1))(functools.partial(_finish_copies, copies))

    blk = lambda: pl.BlockSpec((tq, LANES), lambda p, i: (i, p))
    col = lambda: pl.BlockSpec((s, LANES), lambda p, i: (0, p))
    n_peers = len(DEVICE_FLIPS)
    nq = s // tq
    conv_rows = s // (pairs * nq)
    conv_blk = lambda w: pl.BlockSpec((conv_rows, w), lambda p, i: (p * nq + i, 0))
    halo_blk = pl.BlockSpec((SUBLANES, GATE_COLS),
                            lambda p, i: (jnp.maximum((p * nq + i) * (conv_rows // SUBLANES) - 1, 0), 0))
    return pl.pallas_call(
        body, name="attn_bwd", grid=(pairs, nq),
        in_specs=[pl.BlockSpec((tq, LANES), lambda p, i: (i, p)),
                  pl.BlockSpec((s, LANES), lambda p, i: (0, pairs + p)),
                  pl.BlockSpec((s, LANES), lambda p, i: (0, 2 * pairs + p)),
                  blk(), blk(), blk(),
                  pl.BlockSpec((1, LANES), lambda p, i: (0, p)),
                  pl.BlockSpec((2, 2 * tk, 2 * tk), lambda p, i: (0, 0, 0)),
                  pl.BlockSpec((2 * LANES, LANES), lambda p, i: (0, 0)),
                  pl.BlockSpec(memory_space=pltpu.SMEM),
                  conv_blk(GATE_COLS), halo_blk, conv_blk(CONV_WIDTH),
                  pl.BlockSpec((SUBLANES, CONV_WIDTH), lambda p, i: (0, 0)),
                  pl.BlockSpec((1, CONV_WIDTH), lambda p, i: (0, 0))] + [ANY] * n_g,
        out_specs=[blk(), col(), col(), pl.BlockSpec((SUBLANES, LANES), lambda p, i: (0, p)),
                   conv_blk(CONV_WIDTH), conv_blk(CONV_WIDTH),
                   pl.BlockSpec((SUBLANES, CONV_WIDTH), lambda p, i: (0, 0))] + [ANY] * n_g,
        out_shape=[SDS((s, ATTN_WIDTH), BF16), SDS((s, ATTN_WIDTH), BF16), SDS((s, ATTN_WIDTH), BF16),
                   SDS((SUBLANES, ATTN_WIDTH), F32),
                   SDS((s, CONV_WIDTH), BF16), SDS((s, CONV_WIDTH), F32), SDS((SUBLANES, CONV_WIDTH), F32)]
        + [SDS((n_peers, g.shape[1] // 2, g.shape[2]), g.dtype) for g in partials],
        scratch_shapes=[pltpu.SemaphoreType.DMA((n_peers * n_g,)), pltpu.SemaphoreType.DMA((n_peers * n_g,)),
                        pltpu.VMEM((s, LANES), F32), pltpu.VMEM((s, LANES), F32)],
        compiler_params=_params(("arbitrary", "arbitrary")))(
            qkv, qkv, qkv, o, tot, dyn, g_attn, tri, gmat, cut, gates, gates, dycn, taps, g_conv, *partials)


def _mix_ln1(ycn, yan, w_out, x, g, b):
    s = x.shape[0]
    tm = _row_tile(s, ROW_TILE)

    def body(yc_ref, ya_ref, w_ref, x_ref, g_ref, b_ref, x1_ref, xhat_ref, rstd_ref, x1b_ref):
        mix = jnp.dot(yc_ref[...], w_ref[0:CONV_WIDTH, :], preferred_element_type=F32)
        mix = mix + jnp.dot(ya_ref[...], w_ref[CONV_WIDTH:, :], preferred_element_type=F32)
        x1, xhat, rstd = _layer_norm_fwd(ALPHA * x_ref[...] + mix, g_ref[...], b_ref[...])
        x1_ref[...] = x1
        xhat_ref[...] = xhat
        rstd_ref[...] = rstd
        x1b_ref[...] = x1.astype(BF16)

    row = lambda w: pl.BlockSpec((tm, w), lambda i: (i, 0))
    vec = lambda: pl.BlockSpec((1, D_MODEL), lambda i: (0, 0))
    return pl.pallas_call(
        body, name="mix_ln1", grid=(s // tm,),
        in_specs=[row(CONV_WIDTH), row(ATTN_WIDTH), pl.BlockSpec((D_MODEL, D_MODEL), lambda i: (0, 0)),
                  row(D_MODEL), vec(), vec()],
        out_specs=[row(D_MODEL), row(D_MODEL), row(1), row(D_MODEL)],
        out_shape=[SDS((s, D_MODEL), F32), SDS((s, D_MODEL), F32), SDS((s, 1), F32), SDS((s, D_MODEL), BF16)],
        compiler_params=_params(("parallel",)))(ycn, yan, w_out, x, g, b)


def _mlp_fwd_loss(x1, w_up, w_down, target, g, b):
    s = x1.shape[0]
    tm = _row_tile(s, MLP_ROW_TILE)

    def body(x1_ref, wu_ref, wd_ref, t_ref, g_ref, b_ref, dpre_ref, sums_ref, loss_ref, r_ref, hid_ref, dpreb_ref):
        i = pl.program_id(0)
        x1_v = x1_ref[...]
        xb = x1_v.astype(BF16)
        ffn = jnp.zeros((tm, D_MODEL), F32)
        for k in range(N_CHIPS):
            r = jnp.maximum(jnp.dot(xb, wu_ref[k], preferred_element_type=F32), 0.0)
            hid = (r * r).astype(BF16)
            r_ref[:, FF_SHARD * k:FF_SHARD * (k + 1)] = r.astype(BF16)
            hid_ref[:, FF_SHARD * k:FF_SHARD * (k + 1)] = hid
            ffn = ffn + jnp.dot(hid, wd_ref[k], preferred_element_type=F32)
        g_v = g_ref[...]
        x2, xhat, rstd = _layer_norm_fwd(ALPHA * x1_v + ffn, g_v, b_ref[...])
        err = x2 - t_ref[...]
        dx2 = err * (1.0 / D_MODEL)
        dpre = _layer_norm_bwd(dx2, xhat, rstd, g_v)
        dpre_ref[...] = dpre
        dpreb_ref[...] = dpre.astype(BF16)

        @pl.when(i == 0)
        def _():
            sums_ref[...] = jnp.zeros_like(sums_ref)
            loss_ref[...] = jnp.zeros_like(loss_ref)

        sums_ref[0:1, :] += jnp.sum(dx2 * xhat, axis=0, keepdims=True)
        sums_ref[1:2, :] += jnp.sum(dx2, axis=0, keepdims=True)
        loss_ref[...] += jnp.sum(jnp.sum(err * err, axis=1, keepdims=True), axis=0, keepdims=True) * (0.5 / D_MODEL)

    row = lambda: pl.BlockSpec((tm, D_MODEL), lambda i: (i, 0))
    wide = lambda: pl.BlockSpec((tm, D_FF), lambda i: (i, 0))
    vec = lambda: pl.BlockSpec((1, D_MODEL), lambda i: (0, 0))
    return pl.pallas_call(
        body, name="mlp_fwd_loss", grid=(s // tm,),
        in_specs=[row(), _resident_weight(), _resident_weight(), row(), vec(), vec()],
        out_specs=[row(), pl.BlockSpec((SUBLANES, D_MODEL), lambda i: (0, 0)),
                   pl.BlockSpec((SUBLANES, LANES), lambda i: (0, 0)), wide(), wide(), row()],
        out_shape=[SDS((s, D_MODEL), F32), SDS((SUBLANES, D_MODEL), F32), SDS((SUBLANES, LANES), F32),
                   SDS((s, D_FF), BF16), SDS((s, D_FF), BF16), SDS((s, D_MODEL), BF16)],
        compiler_params=_params(("arbitrary",)))(x1, w_up, w_down, target, g, b)


def _resident_weight():
    return pl.BlockSpec((N_CHIPS, D_MODEL, FF_SHARD), lambda i: (0, 0, 0), pipeline_mode=pl.Buffered(1))


def _mlp_bwd_ln1(relu_up, dpre2, w_up, w_down, xhat1, rstd1, g1, w_out):
    s = dpre2.shape[0]
    tm = _row_tile(s, MLP_ROW_TILE)

    def body(r_ref, d2_ref, wu_ref, wd_ref, xh_ref, rs_ref, g_ref, wo_ref,
             dup_ref, dpre_ref, sums_ref, dpreb_ref, dyc_ref, dya_ref):
        i = pl.program_id(0)
        d2 = d2_ref[...]
        d2b = d2.astype(BF16)
        dx1 = ALPHA * d2
        for k in range(N_CHIPS):
            r = r_ref[:, FF_SHARD * k:FF_SHARD * (k + 1)].astype(F32)
            dhid = lax.dot_general(d2b, wd_ref[k], NT_DIMS, preferred_element_type=F32)
            dupb = (dhid * (2.0 * r)).astype(BF16)
            dup_ref[:, FF_SHARD * k:FF_SHARD * (k + 1)] = dupb
            dx1 = dx1 + lax.dot_general(dupb, wu_ref[k], NT_DIMS, preferred_element_type=F32)
        xhat = xh_ref[...]
        dpre = _layer_norm_bwd(dx1, xhat, rs_ref[...], g_ref[...])
        dpre_ref[...] = dpre
        dpb = dpre.astype(BF16)
        dpreb_ref[...] = dpb
        dyc_ref[...] = lax.dot_general(dpb, wo_ref[0:CONV_WIDTH, :], NT_DIMS, preferred_element_type=F32)
        dya_ref[...] = lax.dot_general(dpb, wo_ref[CONV_WIDTH:, :], NT_DIMS, preferred_element_type=F32)

        @pl.when(i == 0)
        def _():
            sums_ref[...] = jnp.zeros_like(sums_ref)

        sums_ref[0:1, :] += jnp.sum(dx1 * xhat, axis=0, keepdims=True)
        sums_ref[1:2, :] += jnp.sum(dx1, axis=0, keepdims=True)

    row = lambda w: pl.BlockSpec((tm, w), lambda i: (i, 0))
    return pl.pallas_call(
        body, name="mlp_bwd_ln1", grid=(s // tm,),
        in_specs=[row(D_FF), row(D_MODEL), _resident_weight(), _resident_weight(), row(D_MODEL), row(1),
                  pl.BlockSpec((1, D_MODEL), lambda i: (0, 0)),
                  pl.BlockSpec((D_MODEL, D_MODEL), lambda i: (0, 0), pipeline_mode=pl.Buffered(1))],
        out_specs=[row(D_FF), row(D_MODEL), pl.BlockSpec((SUBLANES, D_MODEL), lambda i: (0, 0)), row(D_MODEL),
                   row(CONV_WIDTH), row(ATTN_WIDTH)],
        out_shape=[SDS((s, D_FF), BF16), SDS((s, D_MODEL), F32), SDS((SUBLANES, D_MODEL), F32),
                   SDS((s, D_MODEL), BF16), SDS((s, CONV_WIDTH), F32), SDS((s, ATTN_WIDTH), F32)],
        compiler_params=_params(("arbitrary",)))(relu_up, dpre2, w_up, w_down, xhat1, rstd1, g1, w_out)


def _grad_tn(a, b, name, out_cols, stacked):
    s, ka = a.shape
    n = b.shape[1]
    ts = _row_tile(s, GRAD_SEQ_TILE)
    n_steps = s // ts
    if stacked:
        tka, tn = ka, out_cols
        grid = (1, n // tn, n_steps)
        shape = (n // tn, ka, tn)
        out_spec = lambda: pl.BlockSpec((None, tka, tn), lambda r, c, t: (c, 0, 0))
    else:
        tka, tn = min(ka, 1024), n
        grid = (ka // tka, 1, n_steps)
        shape = (ka, n)
        out_spec = lambda: pl.BlockSpec((tka, tn), lambda r, c, t: (r, 0))

    def body(a_ref, b_ref, o_ref, ob_ref):
        t = pl.program_id(2)

        @pl.when(t == 0)
        def _():
            o_ref[...] = jnp.zeros_like(o_ref)

        o_ref[...] += lax.dot_general(a_ref[...].astype(BF16), b_ref[...].astype(BF16), TN_DIMS,
                                      preferred_element_type=F32)

        @pl.when(t == n_steps - 1)
        def _():
            ob_ref[...] = o_ref[...].astype(BF16)

    return pl.pallas_call(
        body, name=name, grid=grid,
        in_specs=[pl.BlockSpec((ts, tka), lambda r, c, t: (t, r)),
                  pl.BlockSpec((ts, tn), lambda r, c, t: (t, c))],
        out_specs=[out_spec(), out_spec()], out_shape=[SDS(shape, F32), SDS(shape, BF16)],
        compiler_params=_params(("parallel", "parallel", "arbitrary")))(a, b)


def _grad_w_out(ycn, yan, dpre1):
    s = dpre1.shape[0]
    ts = _row_tile(s, GRAD_SEQ_TILE)
    n_steps = s // ts

    def body(yc_ref, ya_ref, d_ref, o_ref, ob_ref):
        half, t = pl.program_id(0), pl.program_id(1)

        @pl.when(t == 0)
        def _():
            o_ref[...] = jnp.zeros_like(o_ref)

        db = d_ref[...]

        @pl.when(half == 0)
        def _():
            o_ref[...] += lax.dot_general(yc_ref[...], db, TN_DIMS, preferred_element_type=F32)

        @pl.when(half == 1)
        def _():
            o_ref[...] += lax.dot_general(ya_ref[...], db, TN_DIMS, preferred_element_type=F32)

        @pl.when(t == n_steps - 1)
        def _():
            ob_ref[...] = o_ref[...].astype(BF16)

    out_spec = lambda: pl.BlockSpec((CONV_WIDTH, D_MODEL), lambda r, t: (r, 0))
    return pl.pallas_call(
        body, name="grad_w_out", grid=(2, n_steps),
        in_specs=[pl.BlockSpec((ts, CONV_WIDTH), lambda r, t: (t, 0)),
                  pl.BlockSpec((ts, ATTN_WIDTH), lambda r, t: (t, 0)),
                  pl.BlockSpec((ts, D_MODEL), lambda r, t: (t, 0))],
        out_specs=[out_spec(), out_spec()],
        out_shape=[SDS((D_MODEL, D_MODEL), F32), SDS((D_MODEL, D_MODEL), BF16)],
        compiler_params=_params(("parallel", "arbitrary")))(ycn, yan, dpre1)


def _sum_with_peers(own_ref, r_ref, o_ref):
    acc = own_ref[...]
    for f in range(r_ref.shape[0]):
        acc = acc + r_ref[f].astype(F32)
    o_ref[...] = acc


def _grad_x(kc_idx, dproj, w_in, dpre1, chip_sums, earlier):
    s = dproj.shape[0]
    tm = _row_tile(s, ROW_TILE)
    steps = s // tm
    n_peers = len(DEVICE_FLIPS)
    n_chips = len(CHIP_FLIPS)
    n_e = len(earlier)

    def body(kc_ref, dp_ref, w_ref, d1_ref, *rest):
        sum_ins, g_in = rest[:2 * n_e], rest[2 * n_e]
        o_ref, g_out = rest[2 * n_e + 1], rest[2 * n_e + 2]
        sum_outs, sems = rest[2 * n_e + 3:3 * n_e + 3], rest[3 * n_e + 3:]
        copies = functools.partial(_chip_reduce_copies, g_in, g_out, sems)
        i = pl.program_id(0)
        pl.when(i == 0)(functools.partial(_start_copies, copies))
        acc = ALPHA * d1_ref[...]
        for k in range(N_CHIPS):
            acc = acc + lax.dot_general(dp_ref[:, IN_SHARD * k:IN_SHARD * (k + 1)], w_ref[k], NT_DIMS,
                                        preferred_element_type=F32)
        o_ref[...] = acc
        for a in range(n_e):
            _sum_with_peers(sum_ins[2 * a], sum_ins[2 * a + 1], sum_outs[a])
        pl.when(i == steps - 1)(functools.partial(_finish_copies, copies))

    in_specs = [pl.BlockSpec((tm, IN_COLS), lambda i, kc: (i, 0)),
                pl.BlockSpec((N_CHIPS, D_MODEL, IN_SHARD), lambda i, kc: (0, 0, 0)),
                pl.BlockSpec((tm, D_MODEL), lambda i, kc: (i, 0))]
    out_specs = [pl.BlockSpec((tm, D_MODEL), lambda i, kc: (i, 0)), ANY]
    out_shape = [SDS((s, D_MODEL), F32), SDS((n_chips,) + chip_sums.shape[1:], chip_sums.dtype)]
    operands = []
    for own, recv in earlier:
        _, _, h, cols = own.shape
        th = h // steps
        in_specs.append(pl.BlockSpec((None, None, th, cols), lambda i, kc: (kc[0], kc[1], i, 0)))
        in_specs.append(pl.BlockSpec((n_peers, th, cols), lambda i, kc: (0, i, 0)))
        out_specs.append(pl.BlockSpec((th, cols), lambda i, kc: (kc[1] * steps + i, 0)))
        out_shape.append(SDS((2 * h, cols), F32))
        operands += [own, recv]
    grid_spec = pltpu.PrefetchScalarGridSpec(
        num_scalar_prefetch=1, grid=(steps,), in_specs=in_specs + [ANY], out_specs=out_specs,
        scratch_shapes=[pltpu.SemaphoreType.DMA((n_chips,)), pltpu.SemaphoreType.DMA((n_chips,))])
    return pl.pallas_call(
        body, name="grad_x", grid_spec=grid_spec, out_shape=out_shape,
        compiler_params=_params(("arbitrary",)))(kc_idx, dproj, w_in, dpre1, *operands, chip_sums)


def _adamw_step(w, g, m, v):
    nm = ADAM_B1 * m + (1.0 - ADAM_B1) * g
    nv = ADAM_B2 * v + (1.0 - ADAM_B2) * (g * g)
    m_hat = nm / (1.0 - ADAM_B1 ** ADAM_STEP)
    v_hat = nv / (1.0 - ADAM_B2 ** ADAM_STEP)
    return -ADAM_LR * (m_hat / (jnp.sqrt(v_hat) + ADAM_EPS) + ADAM_WD * w), nm, nv


def _adamw(w, g, m, v, name):
    r, c = w.shape
    tr = _row_tile(r, SHARD_ROW_TILE)

    def body(w_ref, g_ref, m_ref, v_ref, go_ref, d_ref, nm_ref, nv_ref):
        g_v = g_ref[...]
        go_ref[...] = g_v
        d_ref[...], nm_ref[...], nv_ref[...] = _adamw_step(w_ref[...], g_v, m_ref[...], v_ref[...])

    spec = lambda: pl.BlockSpec((tr, c), lambda i: (i, 0))
    return pl.pallas_call(
        body, name=name, grid=(r // tr,),
        in_specs=[spec(), spec(), spec(), spec()], out_specs=[spec(), spec(), spec(), spec()],
        out_shape=[SDS((r, c), F32)] * 4, compiler_params=_params(("parallel",)))(w, g, m, v)


def _adamw_small(total, conv_grad, weights, moments, variances):
    n = len(weights)
    starts = (ROW_GCONV, ROW_GATTN, ROW_LN1G, ROW_LN1B, ROW_LN2G, ROW_LN2B)

    def body(total_ref, cg_ref, *refs):
        w, m, v, outs = refs[:n], refs[n:2 * n], refs[2 * n:3 * n], refs[3 * n:]
        for p in range(n):
            rows = w[p].shape[0]
            g = cg_ref[...] if p == n - 1 else total_ref[starts[p]:starts[p] + rows, :]
            outs[4 * p][...] = g
            outs[4 * p + 1][...], outs[4 * p + 2][...], outs[4 * p + 3][...] = _adamw_step(
                w[p][...], g, m[p][...], v[p][...])

    vmem = pl.BlockSpec(memory_space=pltpu.VMEM)
    out_shape = [SDS(w.shape, F32) for w in weights for _ in range(4)]
    flat = pl.pallas_call(
        body, name="adamw_small", in_specs=[vmem] * (2 + 3 * n), out_specs=[vmem] * (4 * n),
        out_shape=out_shape)(total, conv_grad, *weights, *moments, *variances)
    return [flat[4 * p:4 * p + 4] for p in range(n)]


def _sum_partials(kc_idx, own, recv, name):
    h, cols = own.shape
    th = _row_tile(h, SUM_ROW_TILE)
    n_peers = recv.shape[0]

    def body(kc_ref, own_ref, r_ref, o_ref):
        _sum_with_peers(own_ref, r_ref, o_ref)

    grid_spec = pltpu.PrefetchScalarGridSpec(
        num_scalar_prefetch=1, grid=(h // th,),
        in_specs=[pl.BlockSpec((th, cols), lambda t, kc: (t, 0)),
                  pl.BlockSpec((n_peers, th, cols), lambda t, kc: (0, t, 0))],
        out_specs=pl.BlockSpec((th, cols), lambda t, kc: (kc[1] * (h // th) + t, 0)))
    return pl.pallas_call(
        body, name=name, grid_spec=grid_spec, out_shape=SDS((2 * h, cols), F32),
        compiler_params=_params(("parallel",)))(kc_idx, own, recv)


def _add_sibling(kc_idx, grad, recv):
    _, _, h, cols = grad.shape
    th = _row_tile(h, ROW_TILE)

    def body(kc_ref, g_ref, r_ref, sums_ref, own_ref):
        total = g_ref[...] + r_ref[...].astype(F32)
        sums_ref[...] = total.astype(BF16)

        @pl.when(pl.program_id(1) == kc_ref[0])
        def _():
            own_ref[...] = total

    grid_spec = pltpu.PrefetchScalarGridSpec(
        num_scalar_prefetch=1, grid=(h // th, N_CHIPS),
        in_specs=[pl.BlockSpec((None, None, th, cols), lambda t, k, kc: (k, kc[1], t, 0)),
                  pl.BlockSpec((None, th, cols), lambda t, k, kc: (k, t, 0))],
        out_specs=[pl.BlockSpec((None, th, cols), lambda t, k, kc: (k, t, 0)),
                   pl.BlockSpec((th, cols), lambda t, k, kc: (t, 0))])
    return pl.pallas_call(
        body, name="add_sibling_w_in", grid_spec=grid_spec,
        out_shape=[SDS((N_CHIPS, h, cols), BF16), SDS((h, cols), F32)],
        compiler_params=_params(("parallel", "arbitrary")))(kc_idx, grad, recv)


def _gather_weights(kc_idx, w_in_slots, conv_slots, later):
    n_l = len(later)
    steps = CAST_STEPS
    n_sems = 6 + len(CHIP_FLIPS)

    def body(kc_ref, *refs):
        cast_ins, cast_outs = refs[:n_l], refs[n_l + 2:2 * n_l + 2]
        w_buf, conv_buf = refs[2 * n_l + 2], refs[2 * n_l + 3]
        sems = refs[2 * n_l + 4:]
        i = pl.program_id(0)

        def first_hop():
            x, y, c = _position()
            sends, arrivals = _gather_chip_hop([w_buf], sems)
            mine = conv_buf.at[2 * x + y]
            for j, (fx, fy) in enumerate(CHIP_FLIPS):
                tx, ty = _flip(x, fx), _flip(y, fy)
                there = conv_buf.at[2 * tx + ty]
                sends.append(_remote_copy(mine, mine, sems, 6 + j, (tx, ty, c)))
                arrivals.append(_remote_copy(there, there, sems, 6 + j, (tx, ty, c)))
            return sends, arrivals

        pl.when(i == 0)(functools.partial(_start_copies, first_hop))
        for src, dst in zip(cast_ins, cast_outs):
            dst[...] = src[...].astype(BF16)

        @pl.when(i == steps - 1)
        def _():
            sends, arrivals = first_hop()
            for cp in arrivals:
                cp.wait_recv()
            _start_copies(functools.partial(_gather_sibling_hop, [w_buf], sems))
            _finish_copies(functools.partial(_gather_sibling_hop, [w_buf], sems))
            for cp in sends:
                cp.wait_send()

    in_specs, out_specs, out_shape = [], [], []
    for w in later:
        r, c = w.shape
        in_specs.append(pl.BlockSpec((r // steps, c), lambda i, kc: (i, 0)))
        out_specs.append(pl.BlockSpec((None, r // steps, c), lambda i, kc: (kc[0], i, 0)))
        out_shape.append(SDS((N_CHIPS, r, c), BF16))
    grid_spec = pltpu.PrefetchScalarGridSpec(
        num_scalar_prefetch=1, grid=(steps,), in_specs=in_specs + [ANY, ANY], out_specs=out_specs + [ANY, ANY],
        scratch_shapes=[pltpu.SemaphoreType.DMA((n_sems,)), pltpu.SemaphoreType.DMA((n_sems,))])
    return pl.pallas_call(
        body, name="gather_weights", grid_spec=grid_spec,
        out_shape=out_shape + [SDS(w_in_slots.shape, w_in_slots.dtype), SDS(conv_slots.shape, conv_slots.dtype)],
        input_output_aliases={n_l + 1: n_l, n_l + 2: n_l + 1},
        compiler_params=_params(("arbitrary",)))(kc_idx, *later, w_in_slots, conv_slots)


def _exchange_with_sibling(partial):
    h = partial.shape[1] // 2

    def body(g_in, g_out, send_sems, recv_sems):
        x, y, c = _position()
        theirs = pl.ds(pl.multiple_of((1 - c) * h, h), h)
        copies = [_remote_copy(g_in.at[k, theirs], g_out.at[k], (send_sems, recv_sems), k, (x, y, 1 - c))
                  for k in range(N_CHIPS)]
        for cp in copies:
            cp.start()
        for cp in copies:
            cp.wait_recv()
        for cp in copies:
            cp.wait_send()

    return pl.pallas_call(
        body, name="exchange_with_sibling", in_specs=[ANY], out_specs=ANY,
        out_shape=SDS((N_CHIPS, h, partial.shape[2]), partial.dtype),
        scratch_shapes=[pltpu.SemaphoreType.DMA((N_CHIPS,)), pltpu.SemaphoreType.DMA((N_CHIPS,))])(partial)


def _finish_exchange(pieces, vec):
    n = len(pieces)
    n_dev = 2 * N_CHIPS

    def body(*refs):
        v_ref = refs[n]
        outs, o_ref = refs[n + 1:2 * n + 1], refs[2 * n + 1]
        buf, send_sems, recv_sems = refs[2 * n + 2:]
        x, y, c = _position()
        sibling = (x, y, 1 - c)
        me = 4 * x + 2 * y + c
        buf[me] = v_ref[...]
        started = []
        for f, (fx, fy, fc) in enumerate(DEVICE_FLIPS):
            cp = pltpu.make_async_remote_copy(
                src_ref=v_ref, dst_ref=buf.at[me], send_sem=send_sems.at[n + f], recv_sem=recv_sems.at[n + f],
                device_id=(_flip(x, fx), _flip(y, fy), _flip(c, fc)), device_id_type=MESH)
            cp.start()
            started.append(cp)
        for a in range(n):
            h = pieces[a].shape[0] // 2
            mine = outs[a].at[pl.ds(pl.multiple_of(c * h, h), h)]
            cp = pltpu.make_async_remote_copy(
                src_ref=mine, dst_ref=mine, send_sem=send_sems.at[a], recv_sem=recv_sems.at[a],
                device_id=sibling, device_id_type=MESH)
            cp.start()
            started.append(cp)
        for a in range(n):
            h = pieces[a].shape[0] // 2
            theirs = outs[a].at[pl.ds(pl.multiple_of((1 - c) * h, h), h)]
            pltpu.make_async_remote_copy(
                src_ref=theirs, dst_ref=theirs, send_sem=send_sems.at[a], recv_sem=recv_sems.at[a],
                device_id=sibling, device_id_type=MESH).wait_recv()
        for f, (fx, fy, fc) in enumerate(DEVICE_FLIPS):
            src = 4 * _flip(x, fx) + 2 * _flip(y, fy) + _flip(c, fc)
            pltpu.make_async_remote_copy(
                src_ref=v_ref, dst_ref=buf.at[src], send_sem=send_sems.at[n + f], recv_sem=recv_sems.at[n + f],
                device_id=(x, y, c), device_id_type=MESH).wait_recv()
        for cp in started:
            cp.wait_send()
        acc = buf[0]
        for d in range(1, n_dev):
            acc = acc + buf[d]
        o_ref[...] = acc

    vmem = pl.BlockSpec(memory_space=pltpu.VMEM)
    out_shape = [SDS(p.shape, p.dtype) for p in pieces] + [SDS(vec.shape, vec.dtype)]
    n_sems = n + n_dev - 1
    return pl.pallas_call(
        body, name="finish_exchange", in_specs=[ANY] * n + [vmem], out_specs=[ANY] * n + [vmem],
        out_shape=out_shape, input_output_aliases={a: a for a in range(n)},
        scratch_shapes=[pltpu.VMEM((n_dev,) + vec.shape, vec.dtype), pltpu.SemaphoreType.DMA((n_sems,)),
                        pltpu.SemaphoreType.DMA((n_sems,))])(*pieces, vec)


def _constants():
    r = jnp.arange(2 * KEY_BLOCK)[:, None] % KEY_BLOCK
    c = jnp.arange(2 * KEY_BLOCK)[None, :]
    later = jnp.where(c < KEY_BLOCK, r > c, True).astype(BF16)
    earlier = jnp.where(c < KEY_BLOCK, r < c, True).astype(BF16)
    upto = jnp.where(c < KEY_BLOCK, r <= c, True).astype(BF16)
    gr = (jnp.arange(2 * LANES)[:, None] % LANES) // GROUP
    gc = jnp.arange(LANES)[None, :] // GROUP
    gmat = (gr == gc).astype(BF16)
    return later, jnp.stack([earlier, upto]), gmat


def _rows(v):
    return v.reshape(-1, LANES)


def kernel(x, w_in, conv_w, g_conv, g_attn, w_out, ln1_g, ln1_b, w_up, w_down, ln2_g, ln2_b, loss_target, m_w_in, m_conv_w, m_g_conv, m_g_attn, m_w_out, m_ln1_g, m_ln1_b, m_w_up, m_w_down, m_ln2_g, m_ln2_b, v_w_in, v_conv_w, v_g_conv, v_g_attn, v_w_out, v_ln1_g, v_ln1_b, v_w_up, v_w_down, v_ln2_g, v_ln2_b):
    xs, target = x[0], loss_target[0]
    mesh_x, mesh_y, mesh_c = _position()
    k_idx = 2 * mesh_x + mesh_y
    kc_idx = jnp.stack([k_idx, mesh_c]).astype(jnp.int32)
    tri_later, tri_earlier, gmat = _constants()

    w_in_b = _cast_into_slot(kc_idx, w_in[0], "cast_w_in")
    conv_slot = jnp.pad(conv_w, ((0, 0), (0, SUBLANES - conv_w.shape[1]), (0, 0)))
    conv_b = lax.dynamic_update_slice(jnp.zeros((N_CHIPS, SUBLANES, LANES), F32), conv_slot, (k_idx, 0, 0))
    w_out_b, w_up_b, w_down_b, w_in_f, conv_f = _gather_weights(
        kc_idx, w_in_b, conv_b, [w_out[0], w_up[0], w_down[0]])
    taps = jnp.transpose(conv_f, (1, 0, 2)).reshape(SUBLANES, CONV_WIDTH)

    gates, qkv, xs_b, ycn = _proj(xs, w_in_f, taps, g_conv, gmat)
    o, yan, tot, cut, w_out_f, w_up_f, w_down_f = _attn_fwd(
        qkv, g_attn, tri_later, gmat, [w_out_b, w_up_b, w_down_b])
    w_out_f = w_out_f.reshape(D_MODEL, D_MODEL)
    x1, xhat1, rstd1, x1_b = _mix_ln1(ycn, yan, w_out_f, xs, ln1_g, ln1_b)
    dpre2, ln2_sums, loss_sum, relu_up, hid, dpre2_b = _mlp_fwd_loss(x1, w_up_f, w_down_f, target, ln2_g, ln2_b)

    dup, dpre1, ln1_sums, dpre1_b, dycn, dyan = _mlp_bwd_ln1(
        relu_up, dpre2, w_up_f, w_down_f, xhat1, rstd1, ln1_g, w_out_f)
    gw_up = _grad_tn(x1_b, dup, "grad_w_up", FF_SHARD, True)
    gw_down = [g.reshape(N_CHIPS, FF_SHARD, D_MODEL) for g in _grad_tn(hid, dpre2_b, "grad_w_down", D_MODEL, False)]
    gw_out = [g.reshape(N_CHIPS, D_MODEL // N_CHIPS, D_MODEL) for g in _grad_w_out(ycn, yan, dpre1_b)]
    dq, dk, dv, gattn_sums, dbg, dy, conv_sums, recv_out, recv_up, recv_down = _attn_bwd(
        qkv, o, tot, dyan, g_attn, tri_earlier, gmat, cut, gates, dycn, taps, g_conv,
        [gw_out[1], gw_up[1], gw_down[1]])
    dproj = _dproj_assemble(gates, dy, dbg, dq, dk, dv, taps)
    gw_in = _grad_tn(xs_b, dproj, "grad_w_in", IN_SHARD, True)
    halves = lambda g: g.reshape(N_CHIPS, 2, g.shape[1] // 2, g.shape[2])
    chip_sums, own_sum = _add_sibling(kc_idx, halves(gw_in[0]), _exchange_with_sibling(gw_in[1]))
    grad_x, recv_in, p_out, p_up, p_down = _grad_x(
        kc_idx, dproj, w_in_f, dpre1, chip_sums,
        [(halves(gw_out[0]), recv_out), (halves(gw_up[0]), recv_up), (halves(gw_down[0]), recv_down)])
    pieces = [_sum_partials(kc_idx, own_sum, recv_in, "sum_partials_w_in"), p_out, p_up, p_down]
    conv_rows = jnp.transpose(conv_sums[0:3].reshape(3, N_CHIPS, LANES), (1, 0, 2)).reshape(3 * N_CHIPS, LANES)
    small = jnp.concatenate([
        loss_sum, _rows(conv_sums[3]), _rows(gattn_sums[0]), _rows(ln1_sums[0]), _rows(ln1_sums[1]),
        _rows(ln2_sums[0]), _rows(ln2_sums[1]), conv_rows,
        jnp.zeros((SMALL_ROWS - ROW_CONVW - 3 * N_CHIPS, LANES), F32)], axis=0)
    g_w_in, g_w_out, g_w_up, g_w_down, total = _finish_exchange(pieces, small)
    loss = total[ROW_LOSS, 0]
    g_conv_w = lax.dynamic_slice(total, (ROW_CONVW + 3 * k_idx, 0), (3, LANES))

    small_names = ["g_conv", "g_attn", "ln1_g", "ln1_b", "ln2_g", "ln2_b", "conv_w"]
    small_w = [g_conv, g_attn, ln1_g, ln1_b, ln2_g, ln2_b, conv_w]
    small_m = [m_g_conv, m_g_attn, m_ln1_g, m_ln1_b, m_ln2_g, m_ln2_b, m_conv_w]
    small_v = [v_g_conv, v_g_attn, v_ln1_g, v_ln1_b, v_ln2_g, v_ln2_b, v_conv_w]
    small_out = dict(zip(small_names, _adamw_small(
        total, g_conv_w, [_rows(a) for a in small_w], [_rows(a) for a in small_m], [_rows(a) for a in small_v])))
    small_shape = dict(zip(small_names, (a.shape for a in small_w)))
    big_out = {
        "w_in": _adamw(w_in[0], g_w_in, m_w_in[0], v_w_in[0], "adamw_w_in"),
        "w_out": _adamw(w_out[0], g_w_out, m_w_out[0], v_w_out[0], "adamw_w_out"),
        "w_up": _adamw(w_up[0], g_w_up, m_w_up[0], v_w_up[0], "adamw_w_up"),
        "w_down": _adamw(w_down[0], g_w_down, m_w_down[0], v_w_down[0], "adamw_w_down"),
    }
    order = ["w_in", "conv_w", "g_conv", "g_attn", "w_out", "ln1_g", "ln1_b", "w_up", "w_down", "ln2_g", "ln2_b"]

    def leaf(kind, name):
        if name in big_out:
            return big_out[name][kind][None]
        return small_out[name][kind].reshape(small_shape[name])

    outs = [loss, grad_x[None]]
    for kind in range(4):
        outs.extend(leaf(kind, name) for name in order)
    return tuple(outs)
```

```python
import functools

import jax
import jax.numpy as jnp
from jax import lax
from jax.experimental import pallas as pl
from jax.experimental.pallas import tpu as pltpu

F32 = jnp.float32
BF16 = jnp.bfloat16
SDS = jax.ShapeDtypeStruct

D_MODEL = 1024
CONV_WIDTH = 512
ATTN_WIDTH = 512
GROUP = 64
GATE_COLS = 3 * CONV_WIDTH
QKV_COLS = 3 * ATTN_WIDTH
IN_COLS = GATE_COLS + QKV_COLS
D_FF = 4 * D_MODEL
N_CHIPS = 4
IN_SHARD = IN_COLS // N_CHIPS
FF_SHARD = D_FF // N_CHIPS
ALPHA = float(2.0 ** 0.25)
LN_EPS = 1e-5
RMS_EPS = 1e-6
ATTN_SCALE = GROUP ** -0.5
LOG2_E = 1.4426950408889634
ADAM_LR = 0.001
ADAM_B1 = 0.9
ADAM_B2 = 0.999
ADAM_EPS = 1e-08
ADAM_WD = 0.01
ADAM_STEP = 10

LANES = 128
SUBLANES = 8
KEY_BLOCK = 128
ATTN_Q_TILE = 512
ATTN_KEY_BLOCKS = 2
ATTN_DIAG_GROUPS = 2
ATTN_DEAD_LOG2 = 200.0
VMEM_LIMIT = 56 * 1024 * 1024
ROW_TILE = 512
MLP_ROW_TILE = 256
GRAD_SEQ_TILE = 2048
SHARD_ROW_TILE = 256
SUM_ROW_TILE = 128
SHARD_STEPS = 8

MESH = pl.DeviceIdType.MESH
CHIP_FLIPS = ((1, 0), (0, 1), (1, 1))
DEVICE_FLIPS = tuple((fx, fy, fc) for fx in (0, 1) for fy in (0, 1) for fc in (0, 1))[1:]
NT_DIMS = (((1,), (1,)), ((), ()))
TN_DIMS = (((0,), (0,)), ((), ()))

ROW_LOSS = 0
ROW_GCONV = 8
ROW_GATTN = 12
ROW_LN1G = 16
ROW_LN1B = 24
ROW_LN2G = 32
ROW_LN2B = 40
ROW_CONVW = 48
SMALL_ROWS = 64


def _params(sem=None):
    return pltpu.CompilerParams(dimension_semantics=sem, vmem_limit_bytes=VMEM_LIMIT)


def _flip(v, f):
    return 1 - v if f else v


def _position():
    return lax.axis_index("x"), lax.axis_index("y"), lax.axis_index("c")


def _hilo(v):
    hi = v.astype(BF16)
    lo = (v - hi.astype(F32)).astype(BF16)
    return jnp.concatenate([hi, lo], axis=1)


def _hilo_dot(v, mat):
    return jnp.dot(_hilo(v), mat, preferred_element_type=F32)


def _group_sum(v, gmat):
    parts = [_hilo_dot(v[:, LANES * j:LANES * (j + 1)], gmat) for j in range(v.shape[1] // LANES)]
    return parts[0] if len(parts) == 1 else jnp.concatenate(parts, axis=1)


def _softplus_terms(z):
    sp = jnp.log2(1.0 + jnp.exp2(-jnp.abs(z)))
    log_beta = jnp.minimum(z, 0.0) - sp
    return log_beta, log_beta - z


def _layer_norm_fwd(pre, g, b):
    mu = jnp.mean(pre, axis=-1, keepdims=True)
    d = pre - mu
    var = jnp.mean(d * d, axis=-1, keepdims=True)
    rstd = lax.rsqrt(var + LN_EPS)
    xhat = d * rstd
    return xhat * g + b, xhat, rstd


def _layer_norm_bwd(dy, xhat, rstd, g):
    dxh = dy * g
    m1 = jnp.mean(dxh, axis=-1, keepdims=True)
    m2 = jnp.mean(dxh * xhat, axis=-1, keepdims=True)
    return rstd * (dxh - m1 - xhat * m2)


def _row_tile(s, want):
    return min(s, want)


def _cast_into_slot(kc_idx, w, name):
    r, c = w.shape
    tr = _row_tile(r, SHARD_ROW_TILE)

    def body(kc_ref, w_ref, o_ref):
        o_ref[...] = w_ref[...].astype(BF16)

    grid_spec = pltpu.PrefetchScalarGridSpec(
        num_scalar_prefetch=1, grid=(r // tr,),
        in_specs=[pl.BlockSpec((tr, c), lambda i, kc: (i, 0))],
        out_specs=pl.BlockSpec((None, tr, c), lambda i, kc: (kc[0], i, 0)))
    return pl.pallas_call(
        body, name=name, grid_spec=grid_spec, out_shape=SDS((N_CHIPS, r, c), BF16),
        compiler_params=_params(("parallel",)))(kc_idx, w)


def _proj(x, w_in, taps, g_conv, gmat):
    s = x.shape[0]
    tm = _row_tile(s, ROW_TILE)

    def body(x_ref, w_ref, taps_ref, gain_ref, gmat_ref, gates_ref, qkv_ref, xb_ref, ycn_ref, halo_ref):
        i = pl.program_id(0)
        xb = x_ref[...].astype(BF16)
        xb_ref[...] = xb
        for k in range(N_CHIPS):
            acc = jnp.dot(xb, w_ref[k], preferred_element_type=F32)
            if k < 2:
                gates_ref[:, IN_SHARD * k:IN_SHARD * (k + 1)] = acc
            else:
                qkv_ref[:, IN_SHARD * (k - 2):IN_SHARD * (k - 1)] = acc.astype(BF16)
        bg, _, _, _, _, _, y = _conv_forward_values(gates_ref, halo_ref, taps_ref, i == 0)
        yc = bg * y
        ms = _group_sum(yc * yc, gmat_ref[...]) * (1.0 / GROUP)
        ycn_ref[...] = (yc * lax.rsqrt(ms + RMS_EPS) * gain_ref[...]).astype(BF16)
        halo_ref[...] = gates_ref[tm - SUBLANES:tm, :]

    return pl.pallas_call(
        body, name="proj", grid=(s // tm,),
        in_specs=[pl.BlockSpec((tm, D_MODEL), lambda i: (i, 0)),
                  pl.BlockSpec((N_CHIPS, D_MODEL, IN_SHARD), lambda i: (0, 0, 0)),
                  pl.BlockSpec((SUBLANES, CONV_WIDTH), lambda i: (0, 0)),
                  pl.BlockSpec((1, CONV_WIDTH), lambda i: (0, 0)),
                  pl.BlockSpec((2 * LANES, LANES), lambda i: (0, 0))],
        out_specs=[pl.BlockSpec((tm, GATE_COLS), lambda i: (i, 0)),
                   pl.BlockSpec((tm, QKV_COLS), lambda i: (i, 0)),
                   pl.BlockSpec((tm, D_MODEL), lambda i: (i, 0)),
                   pl.BlockSpec((tm, CONV_WIDTH), lambda i: (i, 0))],
        out_shape=[SDS((s, GATE_COLS), F32), SDS((s, QKV_COLS), BF16), SDS((s, D_MODEL), BF16),
                   SDS((s, CONV_WIDTH), BF16)],
        scratch_shapes=[pltpu.VMEM((SUBLANES, GATE_COLS), F32)],
        compiler_params=_params(("arbitrary",)))(x, w_in, taps, g_conv, gmat)


def _conv_forward_values(g_ref, halo_ref, taps_ref, first_block):
    gates = g_ref[...]
    tr = gates.shape[0]
    bg = gates[:, :CONV_WIDTH]
    cg = gates[:, CONV_WIDTH:2 * CONV_WIDTH]
    h = gates[:, 2 * CONV_WIDTH:]
    u = cg * h

    def prev(r):
        v = halo_ref[r:r + 1, CONV_WIDTH:2 * CONV_WIDTH] * halo_ref[r:r + 1, 2 * CONV_WIDTH:GATE_COLS]
        return jnp.where(first_block, 0.0, v)

    row = lax.broadcasted_iota(jnp.int32, (tr, CONV_WIDTH), 0)
    u1 = jnp.where(row == 0, prev(7), pltpu.roll(u, 1, 0))
    u2 = jnp.where(row == 0, prev(6), jnp.where(row == 1, prev(7), pltpu.roll(u, 2, 0)))
    y = taps_ref[0:1, :] * u2 + taps_ref[1:2, :] * u1 + taps_ref[2:3, :] * u
    return bg, cg, h, u, u1, u2, y


def _conv_bwd_gate_step(first_block, g_ref, halo_ref, dn_ref, taps_ref, gain_ref, gmat_v, dbg_ref, dy_ref, sums_ref):
    bg, _, _, u, u1, u2, y = _conv_forward_values(g_ref, halo_ref, taps_ref, first_block)
    yc = bg * y
    rstd = lax.rsqrt(_group_sum(yc * yc, gmat_v) * (1.0 / GROUP) + RMS_EPS)
    n = yc * rstd
    dout = dn_ref[...]
    dn = dout * gain_ref[...]
    dyc = rstd * (dn - n * (_group_sum(dn * n, gmat_v) * (1.0 / GROUP)))
    dbg_ref[...] = (dyc * y).astype(BF16)
    dy = dyc * bg
    dy_ref[...] = dy

    @pl.when(first_block)
    def _():
        sums_ref[...] = jnp.zeros_like(sums_ref)

    sums_ref[0:1, :] += jnp.sum(dy * u2, axis=0, keepdims=True)
    sums_ref[1:2, :] += jnp.sum(dy * u1, axis=0, keepdims=True)
    sums_ref[2:3, :] += jnp.sum(dy * u, axis=0, keepdims=True)
    sums_ref[3:4, :] += jnp.sum(dout * n, axis=0, keepdims=True)


def _dproj_assemble(gates, dy, dbg, dq, dk, dv, taps):
    s = gates.shape[0]
    tr = _row_tile(s, ROW_TILE)
    hb = tr // SUBLANES
    last = s // SUBLANES - 1
    n_blocks = s // tr

    def body(g_ref, dy_ref, halo_ref, dbg_ref, dq_ref, dk_ref, dv_ref, taps_ref, out_ref):
        i = pl.program_id(0)
        gates_v = g_ref[...]
        cg = gates_v[:, CONV_WIDTH:2 * CONV_WIDTH]
        h = gates_v[:, 2 * CONV_WIDTH:]
        dy_v = dy_ref[...]
        last_block = i == n_blocks - 1
        nxt = lambda r: jnp.where(last_block, 0.0, halo_ref[r:r + 1, :])
        row = lax.broadcasted_iota(jnp.int32, (tr, CONV_WIDTH), 0)
        d1 = jnp.where(row == tr - 1, nxt(0), pltpu.roll(dy_v, tr - 1, 0))
        d2 = jnp.where(row == tr - 1, nxt(1), jnp.where(row == tr - 2, nxt(0), pltpu.roll(dy_v, tr - 2, 0)))
        du = taps_ref[2:3, :] * dy_v + taps_ref[1:2, :] * d1 + taps_ref[0:1, :] * d2
        out_ref[:, 0:CONV_WIDTH] = dbg_ref[...]
        out_ref[:, CONV_WIDTH:2 * CONV_WIDTH] = (du * h).astype(BF16)
        out_ref[:, 2 * CONV_WIDTH:GATE_COLS] = (du * cg).astype(BF16)
        out_ref[:, GATE_COLS:GATE_COLS + ATTN_WIDTH] = dq_ref[...]
        out_ref[:, GATE_COLS + ATTN_WIDTH:GATE_COLS + 2 * ATTN_WIDTH] = dk_ref[...]
        out_ref[:, GATE_COLS + 2 * ATTN_WIDTH:] = dv_ref[...]

    row_spec = lambda w: pl.BlockSpec((tr, w), lambda i: (i, 0))
    return pl.pallas_call(
        body, name="dproj_assemble", grid=(s // tr,),
        in_specs=[row_spec(GATE_COLS), row_spec(CONV_WIDTH),
                  pl.BlockSpec((SUBLANES, CONV_WIDTH), lambda i: (jnp.minimum((i + 1) * hb, last), 0)),
                  row_spec(CONV_WIDTH), row_spec(ATTN_WIDTH), row_spec(ATTN_WIDTH), row_spec(ATTN_WIDTH),
                  pl.BlockSpec((SUBLANES, CONV_WIDTH), lambda i: (0, 0))],
        out_specs=row_spec(IN_COLS),
        out_shape=SDS((s, IN_COLS), BF16),
        compiler_params=_params(("parallel",)))(gates, dy, dy, dbg, dq, dk, dv, taps)


def _stack_heads(rows, nb):
    lane = lax.broadcasted_iota(jnp.int32, (1, LANES), 1)
    zero = jnp.zeros((KEY_BLOCK, LANES), rows.dtype)
    parts = []
    for blk in range(nb):
        r = rows[blk * KEY_BLOCK:(blk + 1) * KEY_BLOCK]
        parts.append(jnp.where(lane < GROUP, r, zero))
        parts.append(jnp.where(lane < GROUP, zero, r))
    return jnp.concatenate(parts, axis=0)


def _stack_hilo(v, n_cols):
    return jnp.concatenate([_hilo(v[:, c * KEY_BLOCK:(c + 1) * KEY_BLOCK]) for c in range(n_cols)], axis=0)


def _causal_mask(tq, nb, diag_base):
    shape = (tq, 2 * nb * KEY_BLOCK)
    row = lax.broadcasted_iota(jnp.int32, shape, 0)
    col = lax.broadcasted_iota(jnp.int32, shape, 1)
    key = diag_base + (col // (2 * KEY_BLOCK)) * KEY_BLOCK + col % KEY_BLOCK
    return key < row


ANY = pl.BlockSpec(memory_space=pl.ANY)


def _remote_copy(src, dst, sems, idx, target):
    return pltpu.make_async_remote_copy(src_ref=src, dst_ref=dst, send_sem=sems[0].at[idx], recv_sem=sems[1].at[idx],
                                        device_id=target, device_id_type=MESH)


def _gather_chip_hop(bufs, sems):
    x, y, c = _position()
    sends, arrivals = [], []
    for a, buf in enumerate(bufs):
        h = buf.shape[1] // 2
        rows = pl.ds(pl.multiple_of(c * h, h), h)
        mine = buf.at[2 * x + y, rows]
        for j, (fx, fy) in enumerate(CHIP_FLIPS):
            tx, ty = _flip(x, fx), _flip(y, fy)
            there = buf.at[2 * tx + ty, rows]
            sends.append(_remote_copy(mine, mine, sems, 6 * a + j, (tx, ty, c)))
            arrivals.append(_remote_copy(there, there, sems, 6 * a + j, (tx, ty, c)))
    return sends, arrivals


def _gather_sibling_hop(bufs, sems):
    x, y, c = _position()
    sends, arrivals = [], []
    for a, buf in enumerate(bufs):
        h = buf.shape[1] // 2
        mine, theirs = pl.ds(pl.multiple_of(c * h, h), h), pl.ds(pl.multiple_of((1 - c) * h, h), h)
        for j, (fx, fy) in enumerate(CHIP_FLIPS):
            kj = 2 * _flip(x, fx) + _flip(y, fy)
            landed, other = buf.at[kj, mine], buf.at[kj, theirs]
            sends.append(_remote_copy(landed, landed, sems, 6 * a + 3 + j, (x, y, 1 - c)))
            arrivals.append(_remote_copy(other, other, sems, 6 * a + 3 + j, (x, y, 1 - c)))
    return sends, arrivals


def _reduce_copies(ins, outs, sems):
    x, y, c = _position()
    sends, arrivals = [], []
    for a in range(len(ins)):
        h = ins[a].shape[1] // 2
        for f, (fx, fy, fc) in enumerate(DEVICE_FLIPS):
            tx, ty, tc = _flip(x, fx), _flip(y, fy), _flip(c, fc)
            src = ins[a].at[2 * tx + ty, pl.ds(pl.multiple_of(tc * h, h), h)]
            sends.append(_remote_copy(src, outs[a].at[f], sems, 7 * a + f, (tx, ty, tc)))
            arrivals.append(_remote_copy(outs[a].at[f], outs[a].at[f], sems, 7 * a + f, (tx, ty, tc)))
    return sends, arrivals


def _chip_reduce_copies(src, dst, sems):
    x, y, c = _position()
    sends, arrivals = [], []
    for j, (fx, fy) in enumerate(CHIP_FLIPS):
        tx, ty = _flip(x, fx), _flip(y, fy)
        sends.append(_remote_copy(src.at[2 * tx + ty], dst.at[j], sems, j, (tx, ty, c)))
        arrivals.append(_remote_copy(dst.at[j], dst.at[j], sems, j, (tx, ty, c)))
    return sends, arrivals


def _start_copies(make):
    sends, _ = make()
    for cp in sends:
        cp.start()


def _finish_copies(make):
    sends, arrivals = make()
    for cp in arrivals:
        cp.wait_recv()
    for cp in sends:
        cp.wait_send()


def _attn_fwd(qkv, g_attn, tri, gmat, shards):
    n_w = len(shards)
    s = qkv.shape[0]
    tq = _row_tile(s, ATTN_Q_TILE)
    tk = KEY_BLOCK
    nb = ATTN_KEY_BLOCKS
    width = nb * tk
    n_groups = ATTN_DIAG_GROUPS
    group = tq // n_groups
    pairs = ATTN_WIDTH // LANES

    def body(q_ref, k_ref, v_ref, gain_ref, tri_ref, gmat_ref, *rest):
        o_ref, yn_ref, tot_ref, cut_ref = rest[n_w:n_w + 4]
        w_bufs, sems = rest[n_w + 4:2 * n_w + 4], rest[2 * n_w + 4:]
        chip_hop = functools.partial(_gather_chip_hop, w_bufs, sems)
        sibling_hop = functools.partial(_gather_sibling_hop, w_bufs, sems)
        p, i = pl.program_id(0), pl.program_id(1)
        pl.when((p == 0) & (i == 0))(functools.partial(_start_copies, chip_hop))

        @pl.when((p == pairs - 1) & (i == 0))
        def _():
            for cp in chip_hop()[1]:
                cp.wait_recv()
            _start_copies(sibling_hop)

        q2 = q_ref[...]
        tri_v = tri_ref[...]

        def trip(s0, n_blk, rows, carry, diag_base):
            r0, nr = rows
            run = [carry[0], carry[1]]
            oacc = carry[2]
            ksel = _stack_heads(k_ref[pl.ds(s0, n_blk * tk), :], n_blk)
            vsel = _stack_heads(v_ref[pl.ds(s0, n_blk * tk), :], n_blk)
            z = lax.dot_general(q2[r0:r0 + nr], ksel, NT_DIMS, preferred_element_type=F32) * (ATTN_SCALE * LOG2_E)
            log_beta, log_keep = _softplus_terms(z)
            if diag_base is not None:
                valid = _causal_mask(nr, n_blk, diag_base)
                log_keep = jnp.where(valid, log_keep, 0.0)
            ct = jnp.dot(_stack_hilo(log_keep, 2 * n_blk), tri_v, preferred_element_type=F32)
            a_parts = [None] * (2 * n_blk)
            for c in reversed(range(2 * n_blk)):
                h = c % 2
                ct_c = ct[c * nr:(c + 1) * nr]
                a_parts[c] = jnp.exp2(log_beta[:, c * tk:(c + 1) * tk] + ct_c[:, :tk] + run[h])
                run[h] = run[h] + ct_c[:, tk:]
            a = jnp.concatenate(a_parts, axis=1)
            if diag_base is not None:
                a = jnp.where(valid, a, 0.0)
            oacc = oacc + jnp.dot(a.astype(BF16), vsel, preferred_element_type=F32)
            return run[0], run[1], oacc

        n_full = i * (tq // width)
        tile = p * pl.num_programs(1) + i

        def alive(run_a, run_b):
            return jnp.max(jnp.maximum(run_a, run_b)) > -ATTN_DEAD_LOG2

        groups = []
        for g in range(n_groups):
            rows = (g * group, group)
            zeros = (jnp.zeros((group, tk), F32), jnp.zeros((group, tk), F32), jnp.zeros((group, LANES), F32))
            state = trip(pl.multiple_of(i * tq, tq), (g + 1) * group // tk, rows, zeros, -g * group)

            def earlier_trip(c, rows=rows):
                done, _, run_a, run_b, oacc = c
                s0 = pl.multiple_of((n_full - 1 - done) * width, width)
                run_a, run_b, oacc = trip(s0, nb, rows, (run_a, run_b, oacc), None)
                return done + 1, alive(run_a, run_b), run_a, run_b, oacc

            swept = lax.while_loop(lambda c: (c[0] < n_full) & c[1], earlier_trip,
                                   (jnp.int32(0), alive(state[0], state[1])) + state)
            cut_ref[n_groups * tile + g] = (n_full - swept[0]).astype(F32)
            groups.append(swept[2:])
        run_a, run_b, oacc = (jnp.concatenate([grp[j] for grp in groups], axis=0) for j in range(3))
        lane = lax.broadcasted_iota(jnp.int32, (1, LANES), 1)
        o_ref[...] = oacc
        tot_ref[...] = jnp.where(lane < GROUP, run_a, run_b)
        ms = _group_sum(oacc * oacc, gmat_ref[...]) * (1.0 / GROUP)
        yn_ref[...] = (oacc * lax.rsqrt(ms + RMS_EPS) * gain_ref[...]).astype(BF16)

        @pl.when((p == pairs - 1) & (i == pl.num_programs(1) - 1))
        def _():
            for cp in chip_hop()[0]:
                cp.wait_send()
            _finish_copies(sibling_hop)

    blk = lambda: pl.BlockSpec((tq, LANES), lambda p, i: (i, p))
    return pl.pallas_call(
        body, name="attn_fwd", grid=(pairs, s // tq),
        in_specs=[pl.BlockSpec((tq, LANES), lambda p, i: (i, p)),
                  pl.BlockSpec((s, LANES), lambda p, i: (0, pairs + p)),
                  pl.BlockSpec((s, LANES), lambda p, i: (0, 2 * pairs + p)),
                  pl.BlockSpec((1, LANES), lambda p, i: (0, p)),
                  pl.BlockSpec((2 * tk, 2 * tk), lambda p, i: (0, 0)),
                  pl.BlockSpec((2 * LANES, LANES), lambda p, i: (0, 0))] + [ANY] * n_w,
        out_specs=[blk(), blk(), blk(), pl.BlockSpec(memory_space=pltpu.SMEM)] + [ANY] * n_w,
        out_shape=[SDS((s, ATTN_WIDTH), F32), SDS((s, ATTN_WIDTH), BF16), SDS((s, ATTN_WIDTH), F32),
                   SDS((n_groups * pairs * (s // tq),), F32)]
        + [SDS(w.shape, w.dtype) for w in shards],
        input_output_aliases={6 + a: 4 + a for a in range(n_w)},
        scratch_shapes=[pltpu.SemaphoreType.DMA((6 * n_w,)), pltpu.SemaphoreType.DMA((6 * n_w,))],
        compiler_params=_params(("arbitrary", "arbitrary")))(qkv, qkv, qkv, g_attn, tri, gmat, *shards)


def _attn_bwd(qkv, o, tot, dyn, g_attn, tri, gmat, cut, gates, dycn, taps, g_conv, partials):
    n_g = len(partials)
    s = qkv.shape[0]
    tq = _row_tile(s, ATTN_Q_TILE)
    tk = KEY_BLOCK
    nb = ATTN_KEY_BLOCKS
    width = nb * tk
    n_groups = ATTN_DIAG_GROUPS
    group = tq // n_groups
    pairs = ATTN_WIDTH // LANES

    def body(q_ref, k_ref, v_ref, o_ref, tot_ref, dyn_ref, gain_ref, tri_ref, gmat_ref, cut_ref,
             gates_ref, halo_ref, dycn_ref, taps_ref, gconv_ref, *rest):
        g_ins, (dq_ref, dk_out, dv_out, dg_ref) = rest[:n_g], rest[n_g:n_g + 4]
        dbg_ref, dy_ref, conv_sums_ref = rest[n_g + 4:n_g + 7]
        g_outs, sems = rest[n_g + 7:2 * n_g + 7], rest[2 * n_g + 7:2 * n_g + 9]
        dk_ref, dv_ref = rest[2 * n_g + 9:]
        copies = functools.partial(_reduce_copies, g_ins, g_outs, sems)
        p, i = pl.program_id(0), pl.program_id(1)
        pl.when((p == 0) & (i == 0))(functools.partial(_start_copies, copies))

        @pl.when(i == 0)
        def _():
            dk_ref[...] = jnp.zeros_like(dk_ref)
            dv_ref[...] = jnp.zeros_like(dv_ref)
            dg_ref[...] = jnp.zeros_like(dg_ref)

        gmat_v = gmat_ref[...]
        _conv_bwd_gate_step((p == 0) & (i == 0), gates_ref, halo_ref, dycn_ref, taps_ref, gconv_ref, gmat_v,
                            dbg_ref, dy_ref, conv_sums_ref)
        o_v = o_ref[...]
        rstd = lax.rsqrt(_group_sum(o_v * o_v, gmat_v) * (1.0 / GROUP) + RMS_EPS)
        n = o_v * rstd
        dout = dyn_ref[...]
        dg_ref[0:1, :] += jnp.sum(dout * n, axis=0, keepdims=True)
        dn = dout * gain_ref[...]
        do2 = (rstd * (dn - n * (_group_sum(dn * n, gmat_v) * (1.0 / GROUP)))).astype(BF16)
        q2 = q_ref[...]
        tot_v = tot_ref[...]
        tots = (jnp.broadcast_to(tot_v[:, 0:1], (tq, tk)), jnp.broadcast_to(tot_v[:, GROUP:GROUP + 1], (tq, tk)))
        tri_v, tri_incl_v = tri_ref[0], tri_ref[1]
        lane = lax.broadcasted_iota(jnp.int32, (1, LANES), 1)

        def trip(s0, n_blk, rows, carry, diag_base):
            r0, nr = rows
            rest_l = [carry[0], carry[1]]
            pref_g = [carry[2], carry[3]]
            dq = carry[4]
            q_rows, do_rows = q2[r0:r0 + nr], do2[r0:r0 + nr]
            ksel = _stack_heads(k_ref[pl.ds(s0, n_blk * tk), :], n_blk)
            vsel = _stack_heads(v_ref[pl.ds(s0, n_blk * tk), :], n_blk)
            z = lax.dot_general(q_rows, ksel, NT_DIMS, preferred_element_type=F32) * (ATTN_SCALE * LOG2_E)
            log_beta, log_keep = _softplus_terms(z)
            if diag_base is not None:
                valid = _causal_mask(nr, n_blk, diag_base)
                log_keep = jnp.where(valid, log_keep, 0.0)
            ctl = jnp.dot(_stack_hilo(log_keep, 2 * n_blk), tri_incl_v, preferred_element_type=F32)
            da = lax.dot_general(do_rows, vsel, NT_DIMS, preferred_element_type=F32)
            a_parts = []
            for c in range(2 * n_blk):
                h = c % 2
                ct_c = ctl[c * nr:(c + 1) * nr]
                cols = slice(c * tk, (c + 1) * tk)
                a_parts.append(jnp.exp2(log_beta[:, cols] + (rest_l[h] - ct_c[:, :tk])))
                rest_l[h] = rest_l[h] - ct_c[:, tk:]
            a = jnp.concatenate(a_parts, axis=1)
            if diag_base is not None:
                a = jnp.where(valid, a, 0.0)
            g = a * da
            ctg = jnp.dot(_stack_hilo(g, 2 * n_blk), tri_v, preferred_element_type=F32)
            dz_parts = []
            for c in range(2 * n_blk):
                h = c % 2
                ct_c = ctg[c * nr:(c + 1) * nr]
                cols = slice(c * tk, (c + 1) * tk)
                prefix = pref_g[h] + ct_c[:, :tk]
                pref_g[h] = pref_g[h] + ct_c[:, tk:]
                g_c = g[:, cols]
                dz_parts.append(g_c - jnp.exp2(log_beta[:, cols]) * (g_c + prefix))
            dz = jnp.concatenate(dz_parts, axis=1) * ATTN_SCALE
            if diag_base is not None:
                dz = jnp.where(valid, dz, 0.0)
            dzb = dz.astype(BF16)
            dq = dq + jnp.dot(dzb, ksel, preferred_element_type=F32)
            dkt = lax.dot_general(dzb, q_rows, TN_DIMS, preferred_element_type=F32)
            dvt = lax.dot_general(a.astype(BF16), do_rows, TN_DIMS, preferred_element_type=F32)
            for blk in range(n_blk):
                ra, rb = slice(2 * blk * tk, (2 * blk + 1) * tk), slice((2 * blk + 1) * tk, (2 * blk + 2) * tk)
                keys = pl.ds(pl.multiple_of(s0 + blk * tk, tk), tk)
                dk_ref[keys, :] += jnp.where(lane < GROUP, dkt[ra], dkt[rb])
                dv_ref[keys, :] += jnp.where(lane < GROUP, dvt[ra], dvt[rb])
            return rest_l[0], rest_l[1], pref_g[0], pref_g[1], dq

        n_full = i * (tq // width)
        tile = p * pl.num_programs(1) + i
        dq_groups = []
        for g in range(n_groups):
            rows = (g * group, group)
            first = jnp.clip(cut_ref[n_groups * tile + g].astype(jnp.int32), 0, n_full)
            zeros_qk = jnp.zeros((group, tk), F32)
            carry = (tots[0][g * group:(g + 1) * group], tots[1][g * group:(g + 1) * group], zeros_qk, zeros_qk,
                     jnp.zeros((group, LANES), F32))
            carry = lax.fori_loop(
                first, n_full,
                lambda t, c, rows=rows: trip(pl.multiple_of(t * width, width), nb, rows, c, None), carry)
            dq_groups.append(trip(pl.multiple_of(i * tq, tq), (g + 1) * group // tk, rows, carry, -g * group)[4])
        dq_ref[...] = jnp.concatenate(dq_groups, axis=0).astype(BF16)

        @pl.when(i == pl.num_programs(1) - 1)
        def _():
            dk_out[...] = dk_ref[...].astype(BF16)
            dv_out[...] = dv_ref[...].astype(BF16)

        pl.when((p == pairs - 1) & (i == pl.num_programs(1) - 1))(functools.partial(_finish_copies, copies))

    blk = lambda: pl.BlockSpec((tq, LANES), lambda p, i: (i, p))
    col = lambda: pl.BlockSpec((s, LANES), lambda p, i: (0, p))
    n_peers = len(DEVICE_FLIPS)
    nq = s // tq
    conv_rows = s // (pairs * nq)
    conv_blk = lambda w: pl.BlockSpec((conv_rows, w), lambda p, i: (p * nq + i, 0))
    halo_blk = pl.BlockSpec((SUBLANES, GATE_COLS),
                            lambda p, i: (jnp.maximum((p * nq + i) * (conv_rows // SUBLANES) - 1, 0), 0))
    return pl.pallas_call(
        body, name="attn_bwd", grid=(pairs, nq),
        in_specs=[pl.BlockSpec((tq, LANES), lambda p, i: (i, p)),
                  pl.BlockSpec((s, LANES), lambda p, i: (0, pairs + p)),
                  pl.BlockSpec((s, LANES), lambda p, i: (0, 2 * pairs + p)),
                  blk(), blk(), blk(),
                  pl.BlockSpec((1, LANES), lambda p, i: (0, p)),
                  pl.BlockSpec((2, 2 * tk, 2 * tk), lambda p, i: (0, 0, 0)),
                  pl.BlockSpec((2 * LANES, LANES), lambda p, i: (0, 0)),
                  pl.BlockSpec(memory_space=pltpu.SMEM),
                  conv_blk(GATE_COLS), halo_blk, conv_blk(CONV_WIDTH),
                  pl.BlockSpec((SUBLANES, CONV_WIDTH), lambda p, i: (0, 0)),
                  pl.BlockSpec((1, CONV_WIDTH), lambda p, i: (0, 0))] + [ANY] * n_g,
        out_specs=[blk(), col(), col(), pl.BlockSpec((SUBLANES, LANES), lambda p, i: (0, p)),
                   conv_blk(CONV_WIDTH), conv_blk(CONV_WIDTH),
                   pl.BlockSpec((SUBLANES, CONV_WIDTH), lambda p, i: (0, 0))] + [ANY] * n_g,
        out_shape=[SDS((s, ATTN_WIDTH), BF16), SDS((s, ATTN_WIDTH), BF16), SDS((s, ATTN_WIDTH), BF16),
                   SDS((SUBLANES, ATTN_WIDTH), F32),
                   SDS((s, CONV_WIDTH), BF16), SDS((s, CONV_WIDTH), F32), SDS((SUBLANES, CONV_WIDTH), F32)]
        + [SDS((n_peers, g.shape[1] // 2, g.shape[2]), g.dtype) for g in partials],
        scratch_shapes=[pltpu.SemaphoreType.DMA((n_peers * n_g,)), pltpu.SemaphoreType.DMA((n_peers * n_g,)),
                        pltpu.VMEM((s, LANES), F32), pltpu.VMEM((s, LANES), F32)],
        compiler_params=_params(("arbitrary", "arbitrary")))(
            qkv, qkv, qkv, o, tot, dyn, g_attn, tri, gmat, cut, gates, gates, dycn, taps, g_conv, *partials)


def _mix_ln1(ycn, yan, w_out, x, g, b):
    s = x.shape[0]
    tm = _row_tile(s, ROW_TILE)

    def body(yc_ref, ya_ref, w_ref, x_ref, g_ref, b_ref, xhat_ref, rstd_ref, x1b_ref):
        mix = jnp.dot(yc_ref[...], w_ref[0:CONV_WIDTH, :], preferred_element_type=F32)
        mix = mix + jnp.dot(ya_ref[...], w_ref[CONV_WIDTH:, :], preferred_element_type=F32)
        x1, xhat, rstd = _layer_norm_fwd(ALPHA * x_ref[...] + mix, g_ref[...], b_ref[...])
        xhat_ref[...] = xhat
        rstd_ref[...] = rstd
        x1b_ref[...] = x1.astype(BF16)

    row = lambda w: pl.BlockSpec((tm, w), lambda i: (i, 0))
    vec = lambda: pl.BlockSpec((1, D_MODEL), lambda i: (0, 0))
    return pl.pallas_call(
        body, name="mix_ln1", grid=(s // tm,),
        in_specs=[row(CONV_WIDTH), row(ATTN_WIDTH), pl.BlockSpec((D_MODEL, D_MODEL), lambda i: (0, 0)),
                  row(D_MODEL), vec(), vec()],
        out_specs=[row(D_MODEL), row(1), row(D_MODEL)],
        out_shape=[SDS((s, D_MODEL), F32), SDS((s, 1), F32), SDS((s, D_MODEL), BF16)],
        compiler_params=_params(("parallel",)))(ycn, yan, w_out, x, g, b)


def _mlp_fwd_loss(xhat1, g1, b1, w_up, w_down, target, g, b):
    s = xhat1.shape[0]
    tm = _row_tile(s, MLP_ROW_TILE)

    def body(xh_ref, g1_ref, b1_ref, wu_ref, wd_ref, t_ref, g_ref, b_ref,
             dpre_ref, sums_ref, loss_ref, r_ref, hid_ref, dpreb_ref):
        i = pl.program_id(0)
        x1_v = xh_ref[...] * g1_ref[...] + b1_ref[...]
        xb = x1_v.astype(BF16)
        ffn = jnp.zeros((tm, D_MODEL), F32)
        for k in range(N_CHIPS):
            r = jnp.maximum(jnp.dot(xb, wu_ref[k], preferred_element_type=F32), 0.0)
            hid = (r * r).astype(BF16)
            r_ref[:, FF_SHARD * k:FF_SHARD * (k + 1)] = r.astype(BF16)
            hid_ref[:, FF_SHARD * k:FF_SHARD * (k + 1)] = hid
            ffn = ffn + jnp.dot(hid, wd_ref[k], preferred_element_type=F32)
        g_v = g_ref[...]
        x2, xhat, rstd = _layer_norm_fwd(ALPHA * x1_v + ffn, g_v, b_ref[...])
        err = x2 - t_ref[...]
        dx2 = err * (1.0 / D_MODEL)
        dpre = _layer_norm_bwd(dx2, xhat, rstd, g_v)
        dpre_ref[...] = dpre
        dpreb_ref[...] = dpre.astype(BF16)

        @pl.when(i == 0)
        def _():
            sums_ref[...] = jnp.zeros_like(sums_ref)
            loss_ref[...] = jnp.zeros_like(loss_ref)

        sums_ref[0:1, :] += jnp.sum(dx2 * xhat, axis=0, keepdims=True)
        sums_ref[1:2, :] += jnp.sum(dx2, axis=0, keepdims=True)
        loss_ref[...] += jnp.sum(jnp.sum(err * err, axis=1, keepdims=True), axis=0, keepdims=True) * (0.5 / D_MODEL)

    row = lambda: pl.BlockSpec((tm, D_MODEL), lambda i: (i, 0))
    wide = lambda: pl.BlockSpec((tm, D_FF), lambda i: (i, 0))
    vec = lambda: pl.BlockSpec((1, D_MODEL), lambda i: (0, 0))
    return pl.pallas_call(
        body, name="mlp_fwd_loss", grid=(s // tm,),
        in_specs=[row(), vec(), vec(), _resident_weight(), _resident_weight(), row(), vec(), vec()],
        out_specs=[row(), pl.BlockSpec((SUBLANES, D_MODEL), lambda i: (0, 0)),
                   pl.BlockSpec((SUBLANES, LANES), lambda i: (0, 0)), wide(), wide(), row()],
        out_shape=[SDS((s, D_MODEL), F32), SDS((SUBLANES, D_MODEL), F32), SDS((SUBLANES, LANES), F32),
                   SDS((s, D_FF), BF16), SDS((s, D_FF), BF16), SDS((s, D_MODEL), BF16)],
        compiler_params=_params(("arbitrary",)))(xhat1, g1, b1, w_up, w_down, target, g, b)


def _resident_weight():
    return pl.BlockSpec((N_CHIPS, D_MODEL, FF_SHARD), lambda i: (0, 0, 0), pipeline_mode=pl.Buffered(1))


def _mlp_bwd_ln1(relu_up, dpre2, w_up, w_down, xhat1, rstd1, g1, w_out):
    s = dpre2.shape[0]
    tm = _row_tile(s, MLP_ROW_TILE)

    def body(r_ref, d2_ref, wu_ref, wd_ref, xh_ref, rs_ref, g_ref, wo_ref,
             dup_ref, dpre_ref, sums_ref, dpreb_ref, dyc_ref, dya_ref):
        i = pl.program_id(0)
        d2 = d2_ref[...]
        d2b = d2.astype(BF16)
        dx1 = ALPHA * d2
        for k in range(N_CHIPS):
            r = r_ref[:, FF_SHARD * k:FF_SHARD * (k + 1)].astype(F32)
            dhid = lax.dot_general(d2b, wd_ref[k], NT_DIMS, preferred_element_type=F32)
            dupb = (dhid * (2.0 * r)).astype(BF16)
            dup_ref[:, FF_SHARD * k:FF_SHARD * (k + 1)] = dupb
            dx1 = dx1 + lax.dot_general(dupb, wu_ref[k], NT_DIMS, preferred_element_type=F32)
        xhat = xh_ref[...]
        dpre = _layer_norm_bwd(dx1, xhat, rs_ref[...], g_ref[...])
        dpre_ref[...] = dpre
        dpb = dpre.astype(BF16)
        dpreb_ref[...] = dpb
        dyc_ref[...] = lax.dot_general(dpb, wo_ref[0:CONV_WIDTH, :], NT_DIMS, preferred_element_type=F32)
        dya_ref[...] = lax.dot_general(dpb, wo_ref[CONV_WIDTH:, :], NT_DIMS, preferred_element_type=F32)

        @pl.when(i == 0)
        def _():
            sums_ref[...] = jnp.zeros_like(sums_ref)

        sums_ref[0:1, :] += jnp.sum(dx1 * xhat, axis=0, keepdims=True)
        sums_ref[1:2, :] += jnp.sum(dx1, axis=0, keepdims=True)

    row = lambda w: pl.BlockSpec((tm, w), lambda i: (i, 0))
    return pl.pallas_call(
        body, name="mlp_bwd_ln1", grid=(s // tm,),
        in_specs=[row(D_FF), row(D_MODEL), _resident_weight(), _resident_weight(), row(D_MODEL), row(1),
                  pl.BlockSpec((1, D_MODEL), lambda i: (0, 0)),
                  pl.BlockSpec((D_MODEL, D_MODEL), lambda i: (0, 0), pipeline_mode=pl.Buffered(1))],
        out_specs=[row(D_FF), row(D_MODEL), pl.BlockSpec((SUBLANES, D_MODEL), lambda i: (0, 0)), row(D_MODEL),
                   row(CONV_WIDTH), row(ATTN_WIDTH)],
        out_shape=[SDS((s, D_FF), BF16), SDS((s, D_MODEL), F32), SDS((SUBLANES, D_MODEL), F32),
                   SDS((s, D_MODEL), BF16), SDS((s, CONV_WIDTH), F32), SDS((s, ATTN_WIDTH), F32)],
        compiler_params=_params(("arbitrary",)))(relu_up, dpre2, w_up, w_down, xhat1, rstd1, g1, w_out)


def _grad_tn(a, b, name, out_cols, stacked):
    s, ka = a.shape
    n = b.shape[1]
    ts = _row_tile(s, GRAD_SEQ_TILE)
    n_steps = s // ts
    if stacked:
        tka, tn = ka, out_cols
        grid = (1, n // tn, n_steps)
        shape = (n // tn, ka, tn)
        out_spec = lambda: pl.BlockSpec((None, tka, tn), lambda r, c, t: (c, 0, 0))
    else:
        tka, tn = min(ka, 1024), n
        grid = (ka // tka, 1, n_steps)
        shape = (ka, n)
        out_spec = lambda: pl.BlockSpec((tka, tn), lambda r, c, t: (r, 0))

    def body(a_ref, b_ref, o_ref, ob_ref):
        t = pl.program_id(2)

        @pl.when(t == 0)
        def _():
            o_ref[...] = jnp.zeros_like(o_ref)

        o_ref[...] += lax.dot_general(a_ref[...].astype(BF16), b_ref[...].astype(BF16), TN_DIMS,
                                      preferred_element_type=F32)

        @pl.when(t == n_steps - 1)
        def _():
            ob_ref[...] = o_ref[...].astype(BF16)

    return pl.pallas_call(
        body, name=name, grid=grid,
        in_specs=[pl.BlockSpec((ts, tka), lambda r, c, t: (t, r)),
                  pl.BlockSpec((ts, tn), lambda r, c, t: (t, c))],
        out_specs=[out_spec(), out_spec()], out_shape=[SDS(shape, F32), SDS(shape, BF16)],
        compiler_params=_params(("parallel", "parallel", "arbitrary")))(a, b)


def _grad_w_out(ycn, yan, dpre1):
    s = dpre1.shape[0]
    ts = _row_tile(s, GRAD_SEQ_TILE)
    n_steps = s // ts

    def body(yc_ref, ya_ref, d_ref, o_ref, ob_ref):
        half, t = pl.program_id(0), pl.program_id(1)

        @pl.when(t == 0)
        def _():
            o_ref[...] = jnp.zeros_like(o_ref)

        db = d_ref[...]

        @pl.when(half == 0)
        def _():
            o_ref[...] += lax.dot_general(yc_ref[...], db, TN_DIMS, preferred_element_type=F32)

        @pl.when(half == 1)
        def _():
            o_ref[...] += lax.dot_general(ya_ref[...], db, TN_DIMS, preferred_element_type=F32)

        @pl.when(t == n_steps - 1)
        def _():
            ob_ref[...] = o_ref[...].astype(BF16)

    out_spec = lambda: pl.BlockSpec((CONV_WIDTH, D_MODEL), lambda r, t: (r, 0))
    return pl.pallas_call(
        body, name="grad_w_out", grid=(2, n_steps),
        in_specs=[pl.BlockSpec((ts, CONV_WIDTH), lambda r, t: (t, 0)),
                  pl.BlockSpec((ts, ATTN_WIDTH), lambda r, t: (t, 0)),
                  pl.BlockSpec((ts, D_MODEL), lambda r, t: (t, 0))],
        out_specs=[out_spec(), out_spec()],
        out_shape=[SDS((D_MODEL, D_MODEL), F32), SDS((D_MODEL, D_MODEL), BF16)],
        compiler_params=_params(("parallel", "arbitrary")))(ycn, yan, dpre1)


def _sum_with_peers(own_ref, r_ref, o_ref):
    acc = own_ref[...]
    for f in range(r_ref.shape[0]):
        acc = acc + r_ref[f].astype(F32)
    o_ref[...] = acc


def _grad_x(kc_idx, dproj, w_in, dpre1, chip_sums, earlier):
    s = dproj.shape[0]
    tm = _row_tile(s, ROW_TILE)
    steps = s // tm
    n_peers = len(DEVICE_FLIPS)
    n_chips = len(CHIP_FLIPS)
    n_e = len(earlier)

    def body(kc_ref, dp_ref, w_ref, d1_ref, *rest):
        sum_ins, g_in = rest[:2 * n_e], rest[2 * n_e]
        o_ref, g_out = rest[2 * n_e + 1], rest[2 * n_e + 2]
        sum_outs, sems = rest[2 * n_e + 3:3 * n_e + 3], rest[3 * n_e + 3:]
        copies = functools.partial(_chip_reduce_copies, g_in, g_out, sems)
        i = pl.program_id(0)
        pl.when(i == 0)(functools.partial(_start_copies, copies))
        acc = ALPHA * d1_ref[...]
        for k in range(N_CHIPS):
            acc = acc + lax.dot_general(dp_ref[:, IN_SHARD * k:IN_SHARD * (k + 1)], w_ref[k], NT_DIMS,
                                        preferred_element_type=F32)
        o_ref[...] = acc
        for a in range(n_e):
            _sum_with_peers(sum_ins[2 * a], sum_ins[2 * a + 1], sum_outs[a])
        pl.when(i == steps - 1)(functools.partial(_finish_copies, copies))

    in_specs = [pl.BlockSpec((tm, IN_COLS), lambda i, kc: (i, 0)),
                pl.BlockSpec((N_CHIPS, D_MODEL, IN_SHARD), lambda i, kc: (0, 0, 0)),
                pl.BlockSpec((tm, D_MODEL), lambda i, kc: (i, 0))]
    out_specs = [pl.BlockSpec((tm, D_MODEL), lambda i, kc: (i, 0)), ANY]
    out_shape = [SDS((s, D_MODEL), F32), SDS((n_chips,) + chip_sums.shape[1:], chip_sums.dtype)]
    operands = []
    for own, recv in earlier:
        _, _, h, cols = own.shape
        th = h // steps
        in_specs.append(pl.BlockSpec((None, None, th, cols), lambda i, kc: (kc[0], kc[1], i, 0)))
        in_specs.append(pl.BlockSpec((n_peers, th, cols), lambda i, kc: (0, i, 0)))
        out_specs.append(pl.BlockSpec((th, cols), lambda i, kc: (kc[1] * steps + i, 0)))
        out_shape.append(SDS((2 * h, cols), F32))
        operands += [own, recv]
    grid_spec = pltpu.PrefetchScalarGridSpec(
        num_scalar_prefetch=1, grid=(steps,), in_specs=in_specs + [ANY], out_specs=out_specs,
        scratch_shapes=[pltpu.SemaphoreType.DMA((n_chips,)), pltpu.SemaphoreType.DMA((n_chips,))])
    return pl.pallas_call(
        body, name="grad_x", grid_spec=grid_spec, out_shape=out_shape,
        compiler_params=_params(("arbitrary",)))(kc_idx, dproj, w_in, dpre1, *operands, chip_sums)


def _adamw_step(w, g, m, v):
    nm = ADAM_B1 * m + (1.0 - ADAM_B1) * g
    nv = ADAM_B2 * v + (1.0 - ADAM_B2) * (g * g)
    m_hat = nm / (1.0 - ADAM_B1 ** ADAM_STEP)
    v_hat = nv / (1.0 - ADAM_B2 ** ADAM_STEP)
    return -ADAM_LR * (m_hat / (jnp.sqrt(v_hat) + ADAM_EPS) + ADAM_WD * w), nm, nv


def _adamw_shards(weights, grads, moments, variances):
    n = len(weights)
    steps = SHARD_STEPS

    def body(*refs):
        w, g, m, v, outs = refs[:n], refs[n:2 * n], refs[2 * n:3 * n], refs[3 * n:4 * n], refs[4 * n:]
        for a in range(n):
            g_v = g[a][...]
            outs[4 * a][...] = g_v
            outs[4 * a + 1][...], outs[4 * a + 2][...], outs[4 * a + 3][...] = _adamw_step(
                w[a][...], g_v, m[a][...], v[a][...])

    spec = lambda arr: pl.BlockSpec((arr.shape[0] // steps, arr.shape[1]), lambda i: (i, 0))
    flat = pl.pallas_call(
        body, name="adamw_shards", grid=(steps,),
        in_specs=[spec(a) for a in weights] * 4, out_specs=[spec(a) for a in weights for _ in range(4)],
        out_shape=[SDS(a.shape, F32) for a in weights for _ in range(4)],
        compiler_params=_params(("parallel",)))(*weights, *grads, *moments, *variances)
    return [flat[4 * a:4 * a + 4] for a in range(n)]


def _adamw_small(total, conv_grad, weights, moments, variances):
    n = len(weights)
    starts = (ROW_GCONV, ROW_GATTN, ROW_LN1G, ROW_LN1B, ROW_LN2G, ROW_LN2B)

    def body(total_ref, cg_ref, *refs):
        w, m, v, outs = refs[:n], refs[n:2 * n], refs[2 * n:3 * n], refs[3 * n:]
        for p in range(n):
            rows = w[p].shape[0]
            g = cg_ref[...] if p == n - 1 else total_ref[starts[p]:starts[p] + rows, :]
            outs[4 * p][...] = g
            outs[4 * p + 1][...], outs[4 * p + 2][...], outs[4 * p + 3][...] = _adamw_step(
                w[p][...], g, m[p][...], v[p][...])

    vmem = pl.BlockSpec(memory_space=pltpu.VMEM)
    out_shape = [SDS(w.shape, F32) for w in weights for _ in range(4)]
    flat = pl.pallas_call(
        body, name="adamw_small", in_specs=[vmem] * (2 + 3 * n), out_specs=[vmem] * (4 * n),
        out_shape=out_shape)(total, conv_grad, *weights, *moments, *variances)
    return [flat[4 * p:4 * p + 4] for p in range(n)]


def _sum_partials(kc_idx, own, recv, name):
    h, cols = own.shape
    th = _row_tile(h, SUM_ROW_TILE)
    n_peers = recv.shape[0]

    def body(kc_ref, own_ref, r_ref, o_ref):
        _sum_with_peers(own_ref, r_ref, o_ref)

    grid_spec = pltpu.PrefetchScalarGridSpec(
        num_scalar_prefetch=1, grid=(h // th,),
        in_specs=[pl.BlockSpec((th, cols), lambda t, kc: (t, 0)),
                  pl.BlockSpec((n_peers, th, cols), lambda t, kc: (0, t, 0))],
        out_specs=pl.BlockSpec((th, cols), lambda t, kc: (kc[1] * (h // th) + t, 0)))
    return pl.pallas_call(
        body, name=name, grid_spec=grid_spec, out_shape=SDS((2 * h, cols), F32),
        compiler_params=_params(("parallel",)))(kc_idx, own, recv)


def _add_sibling(kc_idx, grad, recv):
    _, _, h, cols = grad.shape
    th = _row_tile(h, ROW_TILE)

    def body(kc_ref, g_ref, r_ref, sums_ref, own_ref):
        total = g_ref[...] + r_ref[...].astype(F32)
        sums_ref[...] = total.astype(BF16)

        @pl.when(pl.program_id(1) == kc_ref[0])
        def _():
            own_ref[...] = total

    grid_spec = pltpu.PrefetchScalarGridSpec(
        num_scalar_prefetch=1, grid=(h // th, N_CHIPS),
        in_specs=[pl.BlockSpec((None, None, th, cols), lambda t, k, kc: (k, kc[1], t, 0)),
                  pl.BlockSpec((None, th, cols), lambda t, k, kc: (k, t, 0))],
        out_specs=[pl.BlockSpec((None, th, cols), lambda t, k, kc: (k, t, 0)),
                   pl.BlockSpec((th, cols), lambda t, k, kc: (t, 0))])
    return pl.pallas_call(
        body, name="add_sibling_w_in", grid_spec=grid_spec,
        out_shape=[SDS((N_CHIPS, h, cols), BF16), SDS((h, cols), F32)],
        compiler_params=_params(("parallel", "arbitrary")))(kc_idx, grad, recv)


def _gather_weights(kc_idx, w_in_slots, conv_slots, later):
    n_l = len(later)
    steps = SHARD_STEPS
    n_sems = 6 + len(CHIP_FLIPS)

    def body(kc_ref, *refs):
        cast_ins, cast_outs = refs[:n_l], refs[n_l + 2:2 * n_l + 2]
        w_buf, conv_buf = refs[2 * n_l + 2], refs[2 * n_l + 3]
        sems = refs[2 * n_l + 4:]
        i = pl.program_id(0)

        def first_hop():
            x, y, c = _position()
            sends, arrivals = _gather_chip_hop([w_buf], sems)
            mine = conv_buf.at[2 * x + y]
            for j, (fx, fy) in enumerate(CHIP_FLIPS):
                tx, ty = _flip(x, fx), _flip(y, fy)
                there = conv_buf.at[2 * tx + ty]
                sends.append(_remote_copy(mine, mine, sems, 6 + j, (tx, ty, c)))
                arrivals.append(_remote_copy(there, there, sems, 6 + j, (tx, ty, c)))
            return sends, arrivals

        pl.when(i == 0)(functools.partial(_start_copies, first_hop))
        for src, dst in zip(cast_ins, cast_outs):
            dst[...] = src[...].astype(BF16)

        @pl.when(i == steps - 1)
        def _():
            sends, arrivals = first_hop()
            for cp in arrivals:
                cp.wait_recv()
            _start_copies(functools.partial(_gather_sibling_hop, [w_buf], sems))
            _finish_copies(functools.partial(_gather_sibling_hop, [w_buf], sems))
            for cp in sends:
                cp.wait_send()

    in_specs, out_specs, out_shape = [], [], []
    for w in later:
        r, c = w.shape
        in_specs.append(pl.BlockSpec((r // steps, c), lambda i, kc: (i, 0)))
        out_specs.append(pl.BlockSpec((None, r // steps, c), lambda i, kc: (kc[0], i, 0)))
        out_shape.append(SDS((N_CHIPS, r, c), BF16))
    grid_spec = pltpu.PrefetchScalarGridSpec(
        num_scalar_prefetch=1, grid=(steps,), in_specs=in_specs + [ANY, ANY], out_specs=out_specs + [ANY, ANY],
        scratch_shapes=[pltpu.SemaphoreType.DMA((n_sems,)), pltpu.SemaphoreType.DMA((n_sems,))])
    return pl.pallas_call(
        body, name="gather_weights", grid_spec=grid_spec,
        out_shape=out_shape + [SDS(w_in_slots.shape, w_in_slots.dtype), SDS(conv_slots.shape, conv_slots.dtype)],
        input_output_aliases={n_l + 1: n_l, n_l + 2: n_l + 1},
        compiler_params=_params(("arbitrary",)))(kc_idx, *later, w_in_slots, conv_slots)


def _exchange_with_sibling(partial):
    h = partial.shape[1] // 2

    def body(g_in, g_out, send_sems, recv_sems):
        x, y, c = _position()
        theirs = pl.ds(pl.multiple_of((1 - c) * h, h), h)
        copies = [_remote_copy(g_in.at[k, theirs], g_out.at[k], (send_sems, recv_sems), k, (x, y, 1 - c))
                  for k in range(N_CHIPS)]
        for cp in copies:
            cp.start()
        for cp in copies:
            cp.wait_recv()
        for cp in copies:
            cp.wait_send()

    return pl.pallas_call(
        body, name="exchange_with_sibling", in_specs=[ANY], out_specs=ANY,
        out_shape=SDS((N_CHIPS, h, partial.shape[2]), partial.dtype),
        scratch_shapes=[pltpu.SemaphoreType.DMA((N_CHIPS,)), pltpu.SemaphoreType.DMA((N_CHIPS,))])(partial)


def _finish_exchange(pieces, vec):
    n = len(pieces)
    n_dev = 2 * N_CHIPS

    def body(*refs):
        v_ref = refs[n]
        outs, o_ref = refs[n + 1:2 * n + 1], refs[2 * n + 1]
        buf, send_sems, recv_sems = refs[2 * n + 2:]
        x, y, c = _position()
        sibling = (x, y, 1 - c)
        me = 4 * x + 2 * y + c
        buf[me] = v_ref[...]
        started = []
        for f, (fx, fy, fc) in enumerate(DEVICE_FLIPS):
            cp = pltpu.make_async_remote_copy(
                src_ref=v_ref, dst_ref=buf.at[me], send_sem=send_sems.at[n + f], recv_sem=recv_sems.at[n + f],
                device_id=(_flip(x, fx), _flip(y, fy), _flip(c, fc)), device_id_type=MESH)
            cp.start()
            started.append(cp)
        for a in range(n):
            h = pieces[a].shape[0] // 2
            mine = outs[a].at[pl.ds(pl.multiple_of(c * h, h), h)]
            cp = pltpu.make_async_remote_copy(
                src_ref=mine, dst_ref=mine, send_sem=send_sems.at[a], recv_sem=recv_sems.at[a],
                device_id=sibling, device_id_type=MESH)
            cp.start()
            started.append(cp)
        for a in range(n):
            h = pieces[a].shape[0] // 2
            theirs = outs[a].at[pl.ds(pl.multiple_of((1 - c) * h, h), h)]
            pltpu.make_async_remote_copy(
                src_ref=theirs, dst_ref=theirs, send_sem=send_sems.at[a], recv_sem=recv_sems.at[a],
                device_id=sibling, device_id_type=MESH).wait_recv()
        for f, (fx, fy, fc) in enumerate(DEVICE_FLIPS):
            src = 4 * _flip(x, fx) + 2 * _flip(y, fy) + _flip(c, fc)
            pltpu.make_async_remote_copy(
                src_ref=v_ref, dst_ref=buf.at[src], send_sem=send_sems.at[n + f], recv_sem=recv_sems.at[n + f],
                device_id=(x, y, c), device_id_type=MESH).wait_recv()
        for cp in started:
            cp.wait_send()
        acc = buf[0]
        for d in range(1, n_dev):
            acc = acc + buf[d]
        o_ref[...] = acc

    vmem = pl.BlockSpec(memory_space=pltpu.VMEM)
    out_shape = [SDS(p.shape, p.dtype) for p in pieces] + [SDS(vec.shape, vec.dtype)]
    n_sems = n + n_dev - 1
    return pl.pallas_call(
        body, name="finish_exchange", in_specs=[ANY] * n + [vmem], out_specs=[ANY] * n + [vmem],
        out_shape=out_shape, input_output_aliases={a: a for a in range(n)},
        scratch_shapes=[pltpu.VMEM((n_dev,) + vec.shape, vec.dtype), pltpu.SemaphoreType.DMA((n_sems,)),
                        pltpu.SemaphoreType.DMA((n_sems,))])(*pieces, vec)


def _constants():
    r = jnp.arange(2 * KEY_BLOCK)[:, None] % KEY_BLOCK
    c = jnp.arange(2 * KEY_BLOCK)[None, :]
    later = jnp.where(c < KEY_BLOCK, r > c, True).astype(BF16)
    earlier = jnp.where(c < KEY_BLOCK, r < c, True).astype(BF16)
    upto = jnp.where(c < KEY_BLOCK, r <= c, True).astype(BF16)
    gr = (jnp.arange(2 * LANES)[:, None] % LANES) // GROUP
    gc = jnp.arange(LANES)[None, :] // GROUP
    gmat = (gr == gc).astype(BF16)
    return later, jnp.stack([earlier, upto]), gmat


def _rows(v):
    return v.reshape(-1, LANES)


def kernel(x, w_in, conv_w, g_conv, g_attn, w_out, ln1_g, ln1_b, w_up, w_down, ln2_g, ln2_b, loss_target, m_w_in, m_conv_w, m_g_conv, m_g_attn, m_w_out, m_ln1_g, m_ln1_b, m_w_up, m_w_down, m_ln2_g, m_ln2_b, v_w_in, v_conv_w, v_g_conv, v_g_attn, v_w_out, v_ln1_g, v_ln1_b, v_w_up, v_w_down, v_ln2_g, v_ln2_b):
    xs, target = x[0], loss_target[0]
    mesh_x, mesh_y, mesh_c = _position()
    k_idx = 2 * mesh_x + mesh_y
    kc_idx = jnp.stack([k_idx, mesh_c]).astype(jnp.int32)
    tri_later, tri_earlier, gmat = _constants()

    w_in_b = _cast_into_slot(kc_idx, w_in[0], "cast_w_in")
    conv_slot = jnp.pad(conv_w, ((0, 0), (0, SUBLANES - conv_w.shape[1]), (0, 0)))
    conv_b = lax.dynamic_update_slice(jnp.zeros((N_CHIPS, SUBLANES, LANES), F32), conv_slot, (k_idx, 0, 0))
    w_out_b, w_up_b, w_down_b, w_in_f, conv_f = _gather_weights(
        kc_idx, w_in_b, conv_b, [w_out[0], w_up[0], w_down[0]])
    taps = jnp.transpose(conv_f, (1, 0, 2)).reshape(SUBLANES, CONV_WIDTH)

    gates, qkv, xs_b, ycn = _proj(xs, w_in_f, taps, g_conv, gmat)
    o, yan, tot, cut, w_out_f, w_up_f, w_down_f = _attn_fwd(
        qkv, g_attn, tri_later, gmat, [w_out_b, w_up_b, w_down_b])
    w_out_f = w_out_f.reshape(D_MODEL, D_MODEL)
    xhat1, rstd1, x1_b = _mix_ln1(ycn, yan, w_out_f, xs, ln1_g, ln1_b)
    dpre2, ln2_sums, loss_sum, relu_up, hid, dpre2_b = _mlp_fwd_loss(
        xhat1, ln1_g, ln1_b, w_up_f, w_down_f, target, ln2_g, ln2_b)

    dup, dpre1, ln1_sums, dpre1_b, dycn, dyan = _mlp_bwd_ln1(
        relu_up, dpre2, w_up_f, w_down_f, xhat1, rstd1, ln1_g, w_out_f)
    gw_up = _grad_tn(x1_b, dup, "grad_w_up", FF_SHARD, True)
    gw_down = [g.reshape(N_CHIPS, FF_SHARD, D_MODEL) for g in _grad_tn(hid, dpre2_b, "grad_w_down", D_MODEL, False)]
    gw_out = [g.reshape(N_CHIPS, D_MODEL // N_CHIPS, D_MODEL) for g in _grad_w_out(ycn, yan, dpre1_b)]
    dq, dk, dv, gattn_sums, dbg, dy, conv_sums, recv_out, recv_up, recv_down = _attn_bwd(
        qkv, o, tot, dyan, g_attn, tri_earlier, gmat, cut, gates, dycn, taps, g_conv,
        [gw_out[1], gw_up[1], gw_down[1]])
    dproj = _dproj_assemble(gates, dy, dbg, dq, dk, dv, taps)
    gw_in = _grad_tn(xs_b, dproj, "grad_w_in", IN_SHARD, True)
    halves = lambda g: g.reshape(N_CHIPS, 2, g.shape[1] // 2, g.shape[2])
    chip_sums, own_sum = _add_sibling(kc_idx, halves(gw_in[0]), _exchange_with_sibling(gw_in[1]))
    grad_x, recv_in, p_out, p_up, p_down = _grad_x(
        kc_idx, dproj, w_in_f, dpre1, chip_sums,
        [(halves(gw_out[0]), recv_out), (halves(gw_up[0]), recv_up), (halves(gw_down[0]), recv_down)])
    pieces = [_sum_partials(kc_idx, own_sum, recv_in, "sum_partials_w_in"), p_out, p_up, p_down]
    conv_rows = jnp.transpose(conv_sums[0:3].reshape(3, N_CHIPS, LANES), (1, 0, 2)).reshape(3 * N_CHIPS, LANES)
    small = jnp.concatenate([
        loss_sum, _rows(conv_sums[3]), _rows(gattn_sums[0]), _rows(ln1_sums[0]), _rows(ln1_sums[1]),
        _rows(ln2_sums[0]), _rows(ln2_sums[1]), conv_rows,
        jnp.zeros((SMALL_ROWS - ROW_CONVW - 3 * N_CHIPS, LANES), F32)], axis=0)
    g_w_in, g_w_out, g_w_up, g_w_down, total = _finish_exchange(pieces, small)
    loss = total[ROW_LOSS, 0]
    g_conv_w = lax.dynamic_slice(total, (ROW_CONVW + 3 * k_idx, 0), (3, LANES))

    small_names = ["g_conv", "g_attn", "ln1_g", "ln1_b", "ln2_g", "ln2_b", "conv_w"]
    small_w = [g_conv, g_attn, ln1_g, ln1_b, ln2_g, ln2_b, conv_w]
    small_m = [m_g_conv, m_g_attn, m_ln1_g, m_ln1_b, m_ln2_g, m_ln2_b, m_conv_w]
    small_v = [v_g_conv, v_g_attn, v_ln1_g, v_ln1_b, v_ln2_g, v_ln2_b, v_conv_w]
    small_out = dict(zip(small_names, _adamw_small(
        total, g_conv_w, [_rows(a) for a in small_w], [_rows(a) for a in small_m], [_rows(a) for a in small_v])))
    small_shape = dict(zip(small_names, (a.shape for a in small_w)))
    big_out = dict(zip(["w_in", "w_out", "w_up", "w_down"], _adamw_shards(
        [w_in[0], w_out[0], w_up[0], w_down[0]], [g_w_in, g_w_out, g_w_up, g_w_down],
        [m_w_in[0], m_w_out[0], m_w_up[0], m_w_down[0]], [v_w_in[0], v_w_out[0], v_w_up[0], v_w_down[0]])))
    order = ["w_in", "conv_w", "g_conv", "g_attn", "w_out", "ln1_g", "ln1_b", "w_up", "w_down", "ln2_g", "ln2_b"]

    def leaf(kind, name):
        if name in big_out:
            return big_out[name][kind][None]
        return small_out[name][kind].reshape(small_shape[name])

    outs = [loss, grad_x[None]]
    for kind in range(4):
        outs.extend(leaf(kind, name) for name in order)
    return tuple(outs)
```

```python
import functools

import jax
import jax.numpy as jnp
from jax import lax
from jax.experimental import pallas as pl
from jax.experimental.pallas import tpu as pltpu

F32 = jnp.float32
BF16 = jnp.bfloat16
SDS = jax.ShapeDtypeStruct

D_MODEL = 1024
CONV_WIDTH = 512
ATTN_WIDTH = 512
GROUP = 64
GATE_COLS = 3 * CONV_WIDTH
QKV_COLS = 3 * ATTN_WIDTH
IN_COLS = GATE_COLS + QKV_COLS
D_FF = 4 * D_MODEL
N_CHIPS = 4
IN_SHARD = IN_COLS // N_CHIPS
FF_SHARD = D_FF // N_CHIPS
ALPHA = float(2.0 ** 0.25)
LN_EPS = 1e-5
RMS_EPS = 1e-6
ATTN_SCALE = GROUP ** -0.5
LOG2_E = 1.4426950408889634
ADAM_LR = 0.001
ADAM_B1 = 0.9
ADAM_B2 = 0.999
ADAM_EPS = 1e-08
ADAM_WD = 0.01
ADAM_STEP = 10

LANES = 128
SUBLANES = 8
KEY_BLOCK = 128
ATTN_Q_TILE = 512
ATTN_KEY_BLOCKS = 2
ATTN_DIAG_GROUPS = 2
ATTN_DEAD_LOG2 = 200.0
VMEM_LIMIT = 56 * 1024 * 1024
ROW_TILE = 512
MLP_ROW_TILE = 256
GRAD_SEQ_TILE = 4096
SHARD_ROW_TILE = 256
SUM_ROW_TILE = 128
SHARD_STEPS = 8

MESH = pl.DeviceIdType.MESH
CHIP_FLIPS = ((1, 0), (0, 1), (1, 1))
DEVICE_FLIPS = tuple((fx, fy, fc) for fx in (0, 1) for fy in (0, 1) for fc in (0, 1))[1:]
NT_DIMS = (((1,), (1,)), ((), ()))
TN_DIMS = (((0,), (0,)), ((), ()))

ROW_LOSS = 0
ROW_GCONV = 8
ROW_GATTN = 12
ROW_LN1G = 16
ROW_LN1B = 24
ROW_LN2G = 32
ROW_LN2B = 40
ROW_CONVW = 48
SMALL_ROWS = 64


def _params(sem=None):
    return pltpu.CompilerParams(dimension_semantics=sem, vmem_limit_bytes=VMEM_LIMIT)


def _flip(v, f):
    return 1 - v if f else v


def _position():
    return lax.axis_index("x"), lax.axis_index("y"), lax.axis_index("c")


def _hilo(v):
    hi = v.astype(BF16)
    lo = (v - hi.astype(F32)).astype(BF16)
    return jnp.concatenate([hi, lo], axis=1)


def _hilo_dot(v, mat):
    return jnp.dot(_hilo(v), mat, preferred_element_type=F32)


def _group_sum(v, gmat):
    parts = [_hilo_dot(v[:, LANES * j:LANES * (j + 1)], gmat) for j in range(v.shape[1] // LANES)]
    return parts[0] if len(parts) == 1 else jnp.concatenate(parts, axis=1)


def _softplus_terms(z):
    sp = jnp.log2(1.0 + jnp.exp2(-jnp.abs(z)))
    log_beta = jnp.minimum(z, 0.0) - sp
    return log_beta, log_beta - z


def _layer_norm_fwd(pre, g, b):
    mu = jnp.mean(pre, axis=-1, keepdims=True)
    d = pre - mu
    var = jnp.mean(d * d, axis=-1, keepdims=True)
    rstd = lax.rsqrt(var + LN_EPS)
    xhat = d * rstd
    return xhat * g + b, xhat, rstd


def _layer_norm_bwd(dy, xhat, rstd, g):
    dxh = dy * g
    m1 = jnp.mean(dxh, axis=-1, keepdims=True)
    m2 = jnp.mean(dxh * xhat, axis=-1, keepdims=True)
    return rstd * (dxh - m1 - xhat * m2)


def _row_tile(s, want):
    return min(s, want)


def _cast_into_slot(kc_idx, w, name):
    r, c = w.shape
    tr = _row_tile(r, SHARD_ROW_TILE)

    def body(kc_ref, w_ref, o_ref):
        o_ref[...] = w_ref[...].astype(BF16)

    grid_spec = pltpu.PrefetchScalarGridSpec(
        num_scalar_prefetch=1, grid=(r // tr,),
        in_specs=[pl.BlockSpec((tr, c), lambda i, kc: (i, 0))],
        out_specs=pl.BlockSpec((None, tr, c), lambda i, kc: (kc[0], i, 0)))
    return pl.pallas_call(
        body, name=name, grid_spec=grid_spec, out_shape=SDS((N_CHIPS, r, c), BF16),
        compiler_params=_params(("parallel",)))(kc_idx, w)


def _proj(x, w_in, taps, g_conv, gmat):
    s = x.shape[0]
    tm = _row_tile(s, ROW_TILE)

    def body(x_ref, w_ref, taps_ref, gain_ref, gmat_ref, gates_ref, qkv_ref, xb_ref, ycn_ref, halo_ref):
        i = pl.program_id(0)
        xb = x_ref[...].astype(BF16)
        xb_ref[...] = xb
        for k in range(N_CHIPS):
            acc = jnp.dot(xb, w_ref[k], preferred_element_type=F32)
            if k < 2:
                gates_ref[:, IN_SHARD * k:IN_SHARD * (k + 1)] = acc
            else:
                qkv_ref[:, IN_SHARD * (k - 2):IN_SHARD * (k - 1)] = acc.astype(BF16)
        bg, _, _, _, _, _, y = _conv_forward_values(gates_ref, halo_ref, taps_ref, i == 0)
        yc = bg * y
        ms = _group_sum(yc * yc, gmat_ref[...]) * (1.0 / GROUP)
        ycn_ref[...] = (yc * lax.rsqrt(ms + RMS_EPS) * gain_ref[...]).astype(BF16)
        halo_ref[...] = gates_ref[tm - SUBLANES:tm, :]

    return pl.pallas_call(
        body, name="proj", grid=(s // tm,),
        in_specs=[pl.BlockSpec((tm, D_MODEL), lambda i: (i, 0)),
                  pl.BlockSpec((N_CHIPS, D_MODEL, IN_SHARD), lambda i: (0, 0, 0)),
                  pl.BlockSpec((SUBLANES, CONV_WIDTH), lambda i: (0, 0)),
                  pl.BlockSpec((1, CONV_WIDTH), lambda i: (0, 0)),
                  pl.BlockSpec((2 * LANES, LANES), lambda i: (0, 0))],
        out_specs=[pl.BlockSpec((tm, GATE_COLS), lambda i: (i, 0)),
                   pl.BlockSpec((tm, QKV_COLS), lambda i: (i, 0)),
                   pl.BlockSpec((tm, D_MODEL), lambda i: (i, 0)),
                   pl.BlockSpec((tm, CONV_WIDTH), lambda i: (i, 0))],
        out_shape=[SDS((s, GATE_COLS), F32), SDS((s, QKV_COLS), BF16), SDS((s, D_MODEL), BF16),
                   SDS((s, CONV_WIDTH), BF16)],
        scratch_shapes=[pltpu.VMEM((SUBLANES, GATE_COLS), F32)],
        compiler_params=_params(("arbitrary",)))(x, w_in, taps, g_conv, gmat)


def _conv_forward_values(g_ref, halo_ref, taps_ref, first_block):
    gates = g_ref[...]
    tr = gates.shape[0]
    bg = gates[:, :CONV_WIDTH]
    cg = gates[:, CONV_WIDTH:2 * CONV_WIDTH]
    h = gates[:, 2 * CONV_WIDTH:]
    u = cg * h

    def prev(r):
        v = halo_ref[r:r + 1, CONV_WIDTH:2 * CONV_WIDTH] * halo_ref[r:r + 1, 2 * CONV_WIDTH:GATE_COLS]
        return jnp.where(first_block, 0.0, v)

    row = lax.broadcasted_iota(jnp.int32, (tr, CONV_WIDTH), 0)
    u1 = jnp.where(row == 0, prev(7), pltpu.roll(u, 1, 0))
    u2 = jnp.where(row == 0, prev(6), jnp.where(row == 1, prev(7), pltpu.roll(u, 2, 0)))
    y = taps_ref[0:1, :] * u2 + taps_ref[1:2, :] * u1 + taps_ref[2:3, :] * u
    return bg, cg, h, u, u1, u2, y


def _conv_bwd_gate_step(first_block, g_ref, halo_ref, dn_ref, taps_ref, gain_ref, gmat_v, dbg_ref, dy_ref, sums_ref):
    bg, _, _, u, u1, u2, y = _conv_forward_values(g_ref, halo_ref, taps_ref, first_block)
    yc = bg * y
    rstd = lax.rsqrt(_group_sum(yc * yc, gmat_v) * (1.0 / GROUP) + RMS_EPS)
    n = yc * rstd
    dout = dn_ref[...]
    dn = dout * gain_ref[...]
    dyc = rstd * (dn - n * (_group_sum(dn * n, gmat_v) * (1.0 / GROUP)))
    dbg_ref[...] = (dyc * y).astype(BF16)
    dy = dyc * bg
    dy_ref[...] = dy

    @pl.when(first_block)
    def _():
        sums_ref[...] = jnp.zeros_like(sums_ref)

    sums_ref[0:1, :] += jnp.sum(dy * u2, axis=0, keepdims=True)
    sums_ref[1:2, :] += jnp.sum(dy * u1, axis=0, keepdims=True)
    sums_ref[2:3, :] += jnp.sum(dy * u, axis=0, keepdims=True)
    sums_ref[3:4, :] += jnp.sum(dout * n, axis=0, keepdims=True)


def _dproj_assemble(gates, dy, dbg, dq, dk, dv, taps):
    s = gates.shape[0]
    tr = _row_tile(s, ROW_TILE)
    hb = tr // SUBLANES
    last = s // SUBLANES - 1
    n_blocks = s // tr

    def body(g_ref, dy_ref, halo_ref, dbg_ref, dq_ref, dk_ref, dv_ref, taps_ref, out_ref):
        i = pl.program_id(0)
        gates_v = g_ref[...]
        cg = gates_v[:, CONV_WIDTH:2 * CONV_WIDTH]
        h = gates_v[:, 2 * CONV_WIDTH:]
        dy_v = dy_ref[...]
        last_block = i == n_blocks - 1
        nxt = lambda r: jnp.where(last_block, 0.0, halo_ref[r:r + 1, :])
        row = lax.broadcasted_iota(jnp.int32, (tr, CONV_WIDTH), 0)
        d1 = jnp.where(row == tr - 1, nxt(0), pltpu.roll(dy_v, tr - 1, 0))
        d2 = jnp.where(row == tr - 1, nxt(1), jnp.where(row == tr - 2, nxt(0), pltpu.roll(dy_v, tr - 2, 0)))
        du = taps_ref[2:3, :] * dy_v + taps_ref[1:2, :] * d1 + taps_ref[0:1, :] * d2
        out_ref[:, 0:CONV_WIDTH] = dbg_ref[...]
        out_ref[:, CONV_WIDTH:2 * CONV_WIDTH] = (du * h).astype(BF16)
        out_ref[:, 2 * CONV_WIDTH:GATE_COLS] = (du * cg).astype(BF16)
        out_ref[:, GATE_COLS:GATE_COLS + ATTN_WIDTH] = dq_ref[...]
        out_ref[:, GATE_COLS + ATTN_WIDTH:GATE_COLS + 2 * ATTN_WIDTH] = dk_ref[...]
        out_ref[:, GATE_COLS + 2 * ATTN_WIDTH:] = dv_ref[...]

    row_spec = lambda w: pl.BlockSpec((tr, w), lambda i: (i, 0))
    return pl.pallas_call(
        body, name="dproj_assemble", grid=(s // tr,),
        in_specs=[row_spec(GATE_COLS), row_spec(CONV_WIDTH),
                  pl.BlockSpec((SUBLANES, CONV_WIDTH), lambda i: (jnp.minimum((i + 1) * hb, last), 0)),
                  row_spec(CONV_WIDTH), row_spec(ATTN_WIDTH), row_spec(ATTN_WIDTH), row_spec(ATTN_WIDTH),
                  pl.BlockSpec((SUBLANES, CONV_WIDTH), lambda i: (0, 0))],
        out_specs=row_spec(IN_COLS),
        out_shape=SDS((s, IN_COLS), BF16),
        compiler_params=_params(("parallel",)))(gates, dy, dy, dbg, dq, dk, dv, taps)


def _stack_heads(rows, nb):
    lane = lax.broadcasted_iota(jnp.int32, (1, LANES), 1)
    zero = jnp.zeros((KEY_BLOCK, LANES), rows.dtype)
    parts = []
    for blk in range(nb):
        r = rows[blk * KEY_BLOCK:(blk + 1) * KEY_BLOCK]
        parts.append(jnp.where(lane < GROUP, r, zero))
        parts.append(jnp.where(lane < GROUP, zero, r))
    return jnp.concatenate(parts, axis=0)


def _stack_hilo(v, n_cols):
    return jnp.concatenate([_hilo(v[:, c * KEY_BLOCK:(c + 1) * KEY_BLOCK]) for c in range(n_cols)], axis=0)


def _causal_mask(tq, nb, diag_base):
    shape = (tq, 2 * nb * KEY_BLOCK)
    row = lax.broadcasted_iota(jnp.int32, shape, 0)
    col = lax.broadcasted_iota(jnp.int32, shape, 1)
    key = diag_base + (col // (2 * KEY_BLOCK)) * KEY_BLOCK + col % KEY_BLOCK
    return key < row


ANY = pl.BlockSpec(memory_space=pl.ANY)


def _remote_copy(src, dst, sems, idx, target):
    return pltpu.make_async_remote_copy(src_ref=src, dst_ref=dst, send_sem=sems[0].at[idx], recv_sem=sems[1].at[idx],
                                        device_id=target, device_id_type=MESH)


def _gather_chip_hop(bufs, sems):
    x, y, c = _position()
    sends, arrivals = [], []
    for a, buf in enumerate(bufs):
        h = buf.shape[1] // 2
        rows = pl.ds(pl.multiple_of(c * h, h), h)
        mine = buf.at[2 * x + y, rows]
        for j, (fx, fy) in enumerate(CHIP_FLIPS):
            tx, ty = _flip(x, fx), _flip(y, fy)
            there = buf.at[2 * tx + ty, rows]
            sends.append(_remote_copy(mine, mine, sems, 6 * a + j, (tx, ty, c)))
            arrivals.append(_remote_copy(there, there, sems, 6 * a + j, (tx, ty, c)))
    return sends, arrivals


def _gather_sibling_hop(bufs, sems):
    x, y, c = _position()
    sends, arrivals = [], []
    for a, buf in enumerate(bufs):
        h = buf.shape[1] // 2
        mine, theirs = pl.ds(pl.multiple_of(c * h, h), h), pl.ds(pl.multiple_of((1 - c) * h, h), h)
        for j, (fx, fy) in enumerate(CHIP_FLIPS):
            kj = 2 * _flip(x, fx) + _flip(y, fy)
            landed, other = buf.at[kj, mine], buf.at[kj, theirs]
            sends.append(_remote_copy(landed, landed, sems, 6 * a + 3 + j, (x, y, 1 - c)))
            arrivals.append(_remote_copy(other, other, sems, 6 * a + 3 + j, (x, y, 1 - c)))
    return sends, arrivals


def _reduce_copies(ins, outs, sems):
    x, y, c = _position()
    sends, arrivals = [], []
    for a in range(len(ins)):
        h = ins[a].shape[1] // 2
        for f, (fx, fy, fc) in enumerate(DEVICE_FLIPS):
            tx, ty, tc = _flip(x, fx), _flip(y, fy), _flip(c, fc)
            src = ins[a].at[2 * tx + ty, pl.ds(pl.multiple_of(tc * h, h), h)]
            sends.append(_remote_copy(src, outs[a].at[f], sems, 7 * a + f, (tx, ty, tc)))
            arrivals.append(_remote_copy(outs[a].at[f], outs[a].at[f], sems, 7 * a + f, (tx, ty, tc)))
    return sends, arrivals


def _chip_reduce_copies(src, dst, sems):
    x, y, c = _position()
    sends, arrivals = [], []
    for j, (fx, fy) in enumerate(CHIP_FLIPS):
        tx, ty = _flip(x, fx), _flip(y, fy)
        sends.append(_remote_copy(src.at[2 * tx + ty], dst.at[j], sems, j, (tx, ty, c)))
        arrivals.append(_remote_copy(dst.at[j], dst.at[j], sems, j, (tx, ty, c)))
    return sends, arrivals


def _start_copies(make):
    sends, _ = make()
    for cp in sends:
        cp.start()


def _finish_copies(make):
    sends, arrivals = make()
    for cp in arrivals:
        cp.wait_recv()
    for cp in sends:
        cp.wait_send()


def _attn_fwd(qkv, g_attn, tri, gmat, shards):
    n_w = len(shards)
    s = qkv.shape[0]
    tq = _row_tile(s, ATTN_Q_TILE)
    tk = KEY_BLOCK
    nb = ATTN_KEY_BLOCKS
    width = nb * tk
    n_groups = ATTN_DIAG_GROUPS
    group = tq // n_groups
    pairs = ATTN_WIDTH // LANES

    def body(q_ref, k_ref, v_ref, gain_ref, tri_ref, gmat_ref, *rest):
        o_ref, yn_ref, tot_ref, cut_ref = rest[n_w:n_w + 4]
        w_bufs, sems = rest[n_w + 4:2 * n_w + 4], rest[2 * n_w + 4:]
        chip_hop = functools.partial(_gather_chip_hop, w_bufs, sems)
        sibling_hop = functools.partial(_gather_sibling_hop, w_bufs, sems)
        p, i = pl.program_id(0), pl.program_id(1)
        pl.when((p == 0) & (i == 0))(functools.partial(_start_copies, chip_hop))

        @pl.when((p == pairs - 1) & (i == 0))
        def _():
            for cp in chip_hop()[1]:
                cp.wait_recv()
            _start_copies(sibling_hop)

        q2 = q_ref[...]
        tri_v = tri_ref[...]

        def trip(s0, n_blk, rows, carry, diag_base):
            r0, nr = rows
            run = [carry[0], carry[1]]
            oacc = carry[2]
            ksel = _stack_heads(k_ref[pl.ds(s0, n_blk * tk), :], n_blk)
            vsel = _stack_heads(v_ref[pl.ds(s0, n_blk * tk), :], n_blk)
            z = lax.dot_general(q2[r0:r0 + nr], ksel, NT_DIMS, preferred_element_type=F32) * (ATTN_SCALE * LOG2_E)
            log_beta, log_keep = _softplus_terms(z)
            if diag_base is not None:
                valid = _causal_mask(nr, n_blk, diag_base)
                log_keep = jnp.where(valid, log_keep, 0.0)
            ct = jnp.dot(_stack_hilo(log_keep, 2 * n_blk), tri_v, preferred_element_type=F32)
            a_parts = [None] * (2 * n_blk)
            for c in reversed(range(2 * n_blk)):
                h = c % 2
                ct_c = ct[c * nr:(c + 1) * nr]
                a_parts[c] = jnp.exp2(log_beta[:, c * tk:(c + 1) * tk] + ct_c[:, :tk] + run[h])
                run[h] = run[h] + ct_c[:, tk:]
            a = jnp.concatenate(a_parts, axis=1)
            if diag_base is not None:
                a = jnp.where(valid, a, 0.0)
            oacc = oacc + jnp.dot(a.astype(BF16), vsel, preferred_element_type=F32)
            return run[0], run[1], oacc

        n_full = i * (tq // width)
        tile = p * pl.num_programs(1) + i

        def alive(run_a, run_b):
            return jnp.max(jnp.maximum(run_a, run_b)) > -ATTN_DEAD_LOG2

        groups = []
        for g in range(n_groups):
            rows = (g * group, group)
            zeros = (jnp.zeros((group, tk), F32), jnp.zeros((group, tk), F32), jnp.zeros((group, LANES), F32))
            state = trip(pl.multiple_of(i * tq, tq), (g + 1) * group // tk, rows, zeros, -g * group)

            def earlier_trip(c, rows=rows):
                done, _, run_a, run_b, oacc = c
                s0 = pl.multiple_of((n_full - 1 - done) * width, width)
                run_a, run_b, oacc = trip(s0, nb, rows, (run_a, run_b, oacc), None)
                return done + 1, alive(run_a, run_b), run_a, run_b, oacc

            swept = lax.while_loop(lambda c: (c[0] < n_full) & c[1], earlier_trip,
                                   (jnp.int32(0), alive(state[0], state[1])) + state)
            cut_ref[n_groups * tile + g] = (n_full - swept[0]).astype(F32)
            groups.append(swept[2:])
        run_a, run_b, oacc = (jnp.concatenate([grp[j] for grp in groups], axis=0) for j in range(3))
        lane = lax.broadcasted_iota(jnp.int32, (1, LANES), 1)
        o_ref[...] = oacc
        tot_ref[...] = jnp.where(lane < GROUP, run_a, run_b)
        ms = _group_sum(oacc * oacc, gmat_ref[...]) * (1.0 / GROUP)
        yn_ref[...] = (oacc * lax.rsqrt(ms + RMS_EPS) * gain_ref[...]).astype(BF16)

        @pl.when((p == pairs - 1) & (i == pl.num_programs(1) - 1))
        def _():
            for cp in chip_hop()[0]:
                cp.wait_send()
            _finish_copies(sibling_hop)

    blk = lambda: pl.BlockSpec((tq, LANES), lambda p, i: (i, p))
    return pl.pallas_call(
        body, name="attn_fwd", grid=(pairs, s // tq),
        in_specs=[pl.BlockSpec((tq, LANES), lambda p, i: (i, p)),
                  pl.BlockSpec((s, LANES), lambda p, i: (0, pairs + p)),
                  pl.BlockSpec((s, LANES), lambda p, i: (0, 2 * pairs + p)),
                  pl.BlockSpec((1, LANES), lambda p, i: (0, p)),
                  pl.BlockSpec((2 * tk, 2 * tk), lambda p, i: (0, 0)),
                  pl.BlockSpec((2 * LANES, LANES), lambda p, i: (0, 0))] + [ANY] * n_w,
        out_specs=[blk(), blk(), blk(), pl.BlockSpec(memory_space=pltpu.SMEM)] + [ANY] * n_w,
        out_shape=[SDS((s, ATTN_WIDTH), F32), SDS((s, ATTN_WIDTH), BF16), SDS((s, ATTN_WIDTH), F32),
                   SDS((n_groups * pairs * (s // tq),), F32)]
        + [SDS(w.shape, w.dtype) for w in shards],
        input_output_aliases={6 + a: 4 + a for a in range(n_w)},
        scratch_shapes=[pltpu.SemaphoreType.DMA((6 * n_w,)), pltpu.SemaphoreType.DMA((6 * n_w,))],
        compiler_params=_params(("arbitrary", "arbitrary")))(qkv, qkv, qkv, g_attn, tri, gmat, *shards)


def _attn_bwd(qkv, o, tot, dyn, g_attn, tri, gmat, cut, gates, dycn, taps, g_conv, partials):
    n_g = len(partials)
    s = qkv.shape[0]
    tq = _row_tile(s, ATTN_Q_TILE)
    tk = KEY_BLOCK
    nb = ATTN_KEY_BLOCKS
    width = nb * tk
    n_groups = ATTN_DIAG_GROUPS
    group = tq // n_groups
    pairs = ATTN_WIDTH // LANES

    def body(q_ref, k_ref, v_ref, o_ref, tot_ref, dyn_ref, gain_ref, tri_ref, gmat_ref, cut_ref,
             gates_ref, halo_ref, dycn_ref, taps_ref, gconv_ref, *rest):
        g_ins, (dq_ref, dk_out, dv_out, dg_ref) = rest[:n_g], rest[n_g:n_g + 4]
        dbg_ref, dy_ref, conv_sums_ref = rest[n_g + 4:n_g + 7]
        g_outs, sems = rest[n_g + 7:2 * n_g + 7], rest[2 * n_g + 7:2 * n_g + 9]
        dk_ref, dv_ref = rest[2 * n_g + 9:]
        copies = functools.partial(_reduce_copies, g_ins, g_outs, sems)
        p, i = pl.program_id(0), pl.program_id(1)
        pl.when((p == 0) & (i == 0))(functools.partial(_start_copies, copies))

        @pl.when(i == 0)
        def _():
            dk_ref[...] = jnp.zeros_like(dk_ref)
            dv_ref[...] = jnp.zeros_like(dv_ref)
            dg_ref[...] = jnp.zeros_like(dg_ref)

        gmat_v = gmat_ref[...]
        _conv_bwd_gate_step((p == 0) & (i == 0), gates_ref, halo_ref, dycn_ref, taps_ref, gconv_ref, gmat_v,
                            dbg_ref, dy_ref, conv_sums_ref)
        o_v = o_ref[...]
        rstd = lax.rsqrt(_group_sum(o_v * o_v, gmat_v) * (1.0 / GROUP) + RMS_EPS)
        n = o_v * rstd
        dout = dyn_ref[...]
        dg_ref[0:1, :] += jnp.sum(dout * n, axis=0, keepdims=True)
        dn = dout * gain_ref[...]
        do2 = (rstd * (dn - n * (_group_sum(dn * n, gmat_v) * (1.0 / GROUP)))).astype(BF16)
        q2 = q_ref[...]
        tot_v = tot_ref[...]
        tots = (jnp.broadcast_to(tot_v[:, 0:1], (tq, tk)), jnp.broadcast_to(tot_v[:, GROUP:GROUP + 1], (tq, tk)))
        tri_v, tri_incl_v = tri_ref[0], tri_ref[1]
        lane = lax.broadcasted_iota(jnp.int32, (1, LANES), 1)

        def trip(s0, n_blk, rows, carry, diag_base):
            r0, nr = rows
            rest_l = [carry[0], carry[1]]
            pref_g = [carry[2], carry[3]]
            dq = carry[4]
            q_rows, do_rows = q2[r0:r0 + nr], do2[r0:r0 + nr]
            ksel = _stack_heads(k_ref[pl.ds(s0, n_blk * tk), :], n_blk)
            vsel = _stack_heads(v_ref[pl.ds(s0, n_blk * tk), :], n_blk)
            z = lax.dot_general(q_rows, ksel, NT_DIMS, preferred_element_type=F32) * (ATTN_SCALE * LOG2_E)
            log_beta, log_keep = _softplus_terms(z)
            if diag_base is not None:
                valid = _causal_mask(nr, n_blk, diag_base)
                log_keep = jnp.where(valid, log_keep, 0.0)
            ctl = jnp.dot(_stack_hilo(log_keep, 2 * n_blk), tri_incl_v, preferred_element_type=F32)
            da = lax.dot_general(do_rows, vsel, NT_DIMS, preferred_element_type=F32)
            a_parts = []
            for c in range(2 * n_blk):
                h = c % 2
                ct_c = ctl[c * nr:(c + 1) * nr]
                cols = slice(c * tk, (c + 1) * tk)
                a_parts.append(jnp.exp2(log_beta[:, cols] + (rest_l[h] - ct_c[:, :tk])))
                rest_l[h] = rest_l[h] - ct_c[:, tk:]
            a = jnp.concatenate(a_parts, axis=1)
            if diag_base is not None:
                a = jnp.where(valid, a, 0.0)
            g = a * da
            ctg = jnp.dot(_stack_hilo(g, 2 * n_blk), tri_v, preferred_element_type=F32)
            dz_parts = []
            for c in range(2 * n_blk):
                h = c % 2
                ct_c = ctg[c * nr:(c + 1) * nr]
                cols = slice(c * tk, (c + 1) * tk)
                prefix = pref_g[h] + ct_c[:, :tk]
                pref_g[h] = pref_g[h] + ct_c[:, tk:]
                g_c = g[:, cols]
                dz_parts.append(g_c - jnp.exp2(log_beta[:, cols]) * (g_c + prefix))
            dz = jnp.concatenate(dz_parts, axis=1) * ATTN_SCALE
            if diag_base is not None:
                dz = jnp.where(valid, dz, 0.0)
            dzb = dz.astype(BF16)
            dq = dq + jnp.dot(dzb, ksel, preferred_element_type=F32)
            dkt = lax.dot_general(dzb, q_rows, TN_DIMS, preferred_element_type=F32)
            dvt = lax.dot_general(a.astype(BF16), do_rows, TN_DIMS, preferred_element_type=F32)
            for blk in range(n_blk):
                ra, rb = slice(2 * blk * tk, (2 * blk + 1) * tk), slice((2 * blk + 1) * tk, (2 * blk + 2) * tk)
                keys = pl.ds(pl.multiple_of(s0 + blk * tk, tk), tk)
                dk_ref[keys, :] += jnp.where(lane < GROUP, dkt[ra], dkt[rb])
                dv_ref[keys, :] += jnp.where(lane < GROUP, dvt[ra], dvt[rb])
            return rest_l[0], rest_l[1], pref_g[0], pref_g[1], dq

        n_full = i * (tq // width)
        tile = p * pl.num_programs(1) + i
        dq_groups = []
        for g in range(n_groups):
            rows = (g * group, group)
            first = jnp.clip(cut_ref[n_groups * tile + g].astype(jnp.int32), 0, n_full)
            zeros_qk = jnp.zeros((group, tk), F32)
            carry = (tots[0][g * group:(g + 1) * group], tots[1][g * group:(g + 1) * group], zeros_qk, zeros_qk,
                     jnp.zeros((group, LANES), F32))
            carry = lax.fori_loop(
                first, n_full,
                lambda t, c, rows=rows: trip(pl.multiple_of(t * width, width), nb, rows, c, None), carry)
            dq_groups.append(trip(pl.multiple_of(i * tq, tq), (g + 1) * group // tk, rows, carry, -g * group)[4])
        dq_ref[...] = jnp.concatenate(dq_groups, axis=0).astype(BF16)

        @pl.when(i == pl.num_programs(1) - 1)
        def _():
            dk_out[...] = dk_ref[...].astype(BF16)
            dv_out[...] = dv_ref[...].astype(BF16)

        pl.when((p == pairs - 1) & (i == pl.num_programs(1) - 1))(functools.partial(_finish_copies, copies))

    blk = lambda: pl.BlockSpec((tq, LANES), lambda p, i: (i, p))
    col = lambda: pl.BlockSpec((s, LANES), lambda p, i: (0, p))
    n_peers = len(DEVICE_FLIPS)
    nq = s // tq
    conv_rows = s // (pairs * nq)
    conv_blk = lambda w: pl.BlockSpec((conv_rows, w), lambda p, i: (p * nq + i, 0))
    halo_blk = pl.BlockSpec((SUBLANES, GATE_COLS),
                            lambda p, i: (jnp.maximum((p * nq + i) * (conv_rows // SUBLANES) - 1, 0), 0))
    return pl.pallas_call(
        body, name="attn_bwd", grid=(pairs, nq),
        in_specs=[pl.BlockSpec((tq, LANES), lambda p, i: (i, p)),
                  pl.BlockSpec((s, LANES), lambda p, i: (0, pairs + p)),
                  pl.BlockSpec((s, LANES), lambda p, i: (0, 2 * pairs + p)),
                  blk(), blk(), blk(),
                  pl.BlockSpec((1, LANES), lambda p, i: (0, p)),
                  pl.BlockSpec((2, 2 * tk, 2 * tk), lambda p, i: (0, 0, 0)),
                  pl.BlockSpec((2 * LANES, LANES), lambda p, i: (0, 0)),
                  pl.BlockSpec(memory_space=pltpu.SMEM),
                  conv_blk(GATE_COLS), halo_blk, conv_blk(CONV_WIDTH),
                  pl.BlockSpec((SUBLANES, CONV_WIDTH), lambda p, i: (0, 0)),
                  pl.BlockSpec((1, CONV_WIDTH), lambda p, i: (0, 0))] + [ANY] * n_g,
        out_specs=[blk(), col(), col(), pl.BlockSpec((SUBLANES, LANES), lambda p, i: (0, p)),
                   conv_blk(CONV_WIDTH), conv_blk(CONV_WIDTH),
                   pl.BlockSpec((SUBLANES, CONV_WIDTH), lambda p, i: (0, 0))] + [ANY] * n_g,
        out_shape=[SDS((s, ATTN_WIDTH), BF16), SDS((s, ATTN_WIDTH), BF16), SDS((s, ATTN_WIDTH), BF16),
                   SDS((SUBLANES, ATTN_WIDTH), F32),
                   SDS((s, CONV_WIDTH), BF16), SDS((s, CONV_WIDTH), F32), SDS((SUBLANES, CONV_WIDTH), F32)]
        + [SDS((n_peers, g.shape[1] // 2, g.shape[2]), g.dtype) for g in partials],
        scratch_shapes=[pltpu.SemaphoreType.DMA((n_peers * n_g,)), pltpu.SemaphoreType.DMA((n_peers * n_g,)),
                        pltpu.VMEM((s, LANES), F32), pltpu.VMEM((s, LANES), F32)],
        compiler_params=_params(("arbitrary", "arbitrary")))(
            qkv, qkv, qkv, o, tot, dyn, g_attn, tri, gmat, cut, gates, gates, dycn, taps, g_conv, *partials)


def _mix_ln1(ycn, yan, w_out, x, g, b):
    s = x.shape[0]
    tm = _row_tile(s, ROW_TILE)

    def body(yc_ref, ya_ref, w_ref, x_ref, g_ref, b_ref, xhat_ref, rstd_ref, x1b_ref):
        mix = jnp.dot(yc_ref[...], w_ref[0:CONV_WIDTH, :], preferred_element_type=F32)
        mix = mix + jnp.dot(ya_ref[...], w_ref[CONV_WIDTH:, :], preferred_element_type=F32)
        x1, xhat, rstd = _layer_norm_fwd(ALPHA * x_ref[...] + mix, g_ref[...], b_ref[...])
        xhat_ref[...] = xhat
        rstd_ref[...] = rstd
        x1b_ref[...] = x1.astype(BF16)

    row = lambda w: pl.BlockSpec((tm, w), lambda i: (i, 0))
    vec = lambda: pl.BlockSpec((1, D_MODEL), lambda i: (0, 0))
    return pl.pallas_call(
        body, name="mix_ln1", grid=(s // tm,),
        in_specs=[row(CONV_WIDTH), row(ATTN_WIDTH), pl.BlockSpec((D_MODEL, D_MODEL), lambda i: (0, 0)),
                  row(D_MODEL), vec(), vec()],
        out_specs=[row(D_MODEL), row(1), row(D_MODEL)],
        out_shape=[SDS((s, D_MODEL), F32), SDS((s, 1), F32), SDS((s, D_MODEL), BF16)],
        compiler_params=_params(("parallel",)))(ycn, yan, w_out, x, g, b)


def _mlp_fwd_loss(xhat1, g1, b1, w_up, w_down, target, g, b):
    s = xhat1.shape[0]
    tm = _row_tile(s, MLP_ROW_TILE)

    def body(xh_ref, g1_ref, b1_ref, wu_ref, wd_ref, t_ref, g_ref, b_ref,
             dpre_ref, sums_ref, loss_ref, r_ref, hid_ref, dpreb_ref):
        i = pl.program_id(0)
        x1_v = xh_ref[...] * g1_ref[...] + b1_ref[...]
        xb = x1_v.astype(BF16)
        ffn = jnp.zeros((tm, D_MODEL), F32)
        for k in range(N_CHIPS):
            r = jnp.maximum(jnp.dot(xb, wu_ref[k], preferred_element_type=F32), 0.0)
            hid = (r * r).astype(BF16)
            r_ref[:, FF_SHARD * k:FF_SHARD * (k + 1)] = r.astype(BF16)
            hid_ref[:, FF_SHARD * k:FF_SHARD * (k + 1)] = hid
            ffn = ffn + jnp.dot(hid, wd_ref[k], preferred_element_type=F32)
        g_v = g_ref[...]
        x2, xhat, rstd = _layer_norm_fwd(ALPHA * x1_v + ffn, g_v, b_ref[...])
        err = x2 - t_ref[...]
        dx2 = err * (1.0 / D_MODEL)
        dpre = _layer_norm_bwd(dx2, xhat, rstd, g_v)
        dpre_ref[...] = dpre
        dpreb_ref[...] = dpre.astype(BF16)

        @pl.when(i == 0)
        def _():
            sums_ref[...] = jnp.zeros_like(sums_ref)
            loss_ref[...] = jnp.zeros_like(loss_ref)

        sums_ref[0:1, :] += jnp.sum(dx2 * xhat, axis=0, keepdims=True)
        sums_ref[1:2, :] += jnp.sum(dx2, axis=0, keepdims=True)
        loss_ref[...] += jnp.sum(jnp.sum(err * err, axis=1, keepdims=True), axis=0, keepdims=True) * (0.5 / D_MODEL)

    row = lambda: pl.BlockSpec((tm, D_MODEL), lambda i: (i, 0))
    wide = lambda: pl.BlockSpec((tm, D_FF), lambda i: (i, 0))
    vec = lambda: pl.BlockSpec((1, D_MODEL), lambda i: (0, 0))
    return pl.pallas_call(
        body, name="mlp_fwd_loss", grid=(s // tm,),
        in_specs=[row(), vec(), vec(), _resident_weight(), _resident_weight(), row(), vec(), vec()],
        out_specs=[row(), pl.BlockSpec((SUBLANES, D_MODEL), lambda i: (0, 0)),
                   pl.BlockSpec((SUBLANES, LANES), lambda i: (0, 0)), wide(), wide(), row()],
        out_shape=[SDS((s, D_MODEL), F32), SDS((SUBLANES, D_MODEL), F32), SDS((SUBLANES, LANES), F32),
                   SDS((s, D_FF), BF16), SDS((s, D_FF), BF16), SDS((s, D_MODEL), BF16)],
        compiler_params=_params(("arbitrary",)))(xhat1, g1, b1, w_up, w_down, target, g, b)


def _resident_weight():
    return pl.BlockSpec((N_CHIPS, D_MODEL, FF_SHARD), lambda i: (0, 0, 0), pipeline_mode=pl.Buffered(1))


def _mlp_bwd_ln1(relu_up, dpre2, w_up, w_down, xhat1, rstd1, g1, w_out):
    s = dpre2.shape[0]
    tm = _row_tile(s, MLP_ROW_TILE)

    def body(r_ref, d2_ref, wu_ref, wd_ref, xh_ref, rs_ref, g_ref, wo_ref,
             dup_ref, dpre_ref, sums_ref, dpreb_ref, dyc_ref, dya_ref):
        i = pl.program_id(0)
        d2 = d2_ref[...]
        d2b = d2.astype(BF16)
        dx1 = ALPHA * d2
        for k in range(N_CHIPS):
            r = r_ref[:, FF_SHARD * k:FF_SHARD * (k + 1)].astype(F32)
            dhid = lax.dot_general(d2b, wd_ref[k], NT_DIMS, preferred_element_type=F32)
            dupb = (dhid * (2.0 * r)).astype(BF16)
            dup_ref[:, FF_SHARD * k:FF_SHARD * (k + 1)] = dupb
            dx1 = dx1 + lax.dot_general(dupb, wu_ref[k], NT_DIMS, preferred_element_type=F32)
        xhat = xh_ref[...]
        dpre = _layer_norm_bwd(dx1, xhat, rs_ref[...], g_ref[...])
        dpre_ref[...] = dpre
        dpb = dpre.astype(BF16)
        dpreb_ref[...] = dpb
        dyc_ref[...] = lax.dot_general(dpb, wo_ref[0:CONV_WIDTH, :], NT_DIMS, preferred_element_type=F32)
        dya_ref[...] = lax.dot_general(dpb, wo_ref[CONV_WIDTH:, :], NT_DIMS, preferred_element_type=F32)

        @pl.when(i == 0)
        def _():
            sums_ref[...] = jnp.zeros_like(sums_ref)

        sums_ref[0:1, :] += jnp.sum(dx1 * xhat, axis=0, keepdims=True)
        sums_ref[1:2, :] += jnp.sum(dx1, axis=0, keepdims=True)

    row = lambda w: pl.BlockSpec((tm, w), lambda i: (i, 0))
    return pl.pallas_call(
        body, name="mlp_bwd_ln1", grid=(s // tm,),
        in_specs=[row(D_FF), row(D_MODEL), _resident_weight(), _resident_weight(), row(D_MODEL), row(1),
                  pl.BlockSpec((1, D_MODEL), lambda i: (0, 0)),
                  pl.BlockSpec((D_MODEL, D_MODEL), lambda i: (0, 0), pipeline_mode=pl.Buffered(1))],
        out_specs=[row(D_FF), row(D_MODEL), pl.BlockSpec((SUBLANES, D_MODEL), lambda i: (0, 0)), row(D_MODEL),
                   row(CONV_WIDTH), row(ATTN_WIDTH)],
        out_shape=[SDS((s, D_FF), BF16), SDS((s, D_MODEL), F32), SDS((SUBLANES, D_MODEL), F32),
                   SDS((s, D_MODEL), BF16), SDS((s, CONV_WIDTH), F32), SDS((s, ATTN_WIDTH), F32)],
        compiler_params=_params(("arbitrary",)))(relu_up, dpre2, w_up, w_down, xhat1, rstd1, g1, w_out)


def _grad_tn(a, b, name, out_cols, stacked):
    s, ka = a.shape
    n = b.shape[1]
    ts = _row_tile(s, GRAD_SEQ_TILE)
    n_steps = s // ts
    if stacked:
        tka, tn = ka, out_cols
        grid = (1, n // tn, n_steps)
        shape = (n // tn, ka, tn)
        out_spec = lambda: pl.BlockSpec((None, tka, tn), lambda r, c, t: (c, 0, 0))
    else:
        tka, tn = min(ka, 1024), n
        grid = (ka // tka, 1, n_steps)
        shape = (ka, n)
        out_spec = lambda: pl.BlockSpec((tka, tn), lambda r, c, t: (r, 0))

    def body(a_ref, b_ref, o_ref, ob_ref):
        t = pl.program_id(2)

        @pl.when(t == 0)
        def _():
            o_ref[...] = jnp.zeros_like(o_ref)

        o_ref[...] += lax.dot_general(a_ref[...].astype(BF16), b_ref[...].astype(BF16), TN_DIMS,
                                      preferred_element_type=F32)

        @pl.when(t == n_steps - 1)
        def _():
            ob_ref[...] = o_ref[...].astype(BF16)

    return pl.pallas_call(
        body, name=name, grid=grid,
        in_specs=[pl.BlockSpec((ts, tka), lambda r, c, t: (t, r)),
                  pl.BlockSpec((ts, tn), lambda r, c, t: (t, c))],
        out_specs=[out_spec(), out_spec()], out_shape=[SDS(shape, F32), SDS(shape, BF16)],
        compiler_params=_params(("parallel", "parallel", "arbitrary")))(a, b)


def _grad_w_out(ycn, yan, dpre1):
    s = dpre1.shape[0]
    ts = _row_tile(s, GRAD_SEQ_TILE)
    n_steps = s // ts

    def body(yc_ref, ya_ref, d_ref, o_ref, ob_ref):
        half, t = pl.program_id(0), pl.program_id(1)

        @pl.when(t == 0)
        def _():
            o_ref[...] = jnp.zeros_like(o_ref)

        db = d_ref[...]

        @pl.when(half == 0)
        def _():
            o_ref[...] += lax.dot_general(yc_ref[...], db, TN_DIMS, preferred_element_type=F32)

        @pl.when(half == 1)
        def _():
            o_ref[...] += lax.dot_general(ya_ref[...], db, TN_DIMS, preferred_element_type=F32)

        @pl.when(t == n_steps - 1)
        def _():
            ob_ref[...] = o_ref[...].astype(BF16)

    out_spec = lambda: pl.BlockSpec((CONV_WIDTH, D_MODEL), lambda r, t: (r, 0))
    return pl.pallas_call(
        body, name="grad_w_out", grid=(2, n_steps),
        in_specs=[pl.BlockSpec((ts, CONV_WIDTH), lambda r, t: (t, 0)),
                  pl.BlockSpec((ts, ATTN_WIDTH), lambda r, t: (t, 0)),
                  pl.BlockSpec((ts, D_MODEL), lambda r, t: (t, 0))],
        out_specs=[out_spec(), out_spec()],
        out_shape=[SDS((D_MODEL, D_MODEL), F32), SDS((D_MODEL, D_MODEL), BF16)],
        compiler_params=_params(("parallel", "arbitrary")))(ycn, yan, dpre1)


def _sum_with_peers(own_ref, r_ref, o_ref):
    acc = own_ref[...]
    for f in range(r_ref.shape[0]):
        acc = acc + r_ref[f].astype(F32)
    o_ref[...] = acc


def _grad_x(kc_idx, dproj, w_in, dpre1, chip_sums, earlier):
    s = dproj.shape[0]
    tm = _row_tile(s, ROW_TILE)
    steps = s // tm
    n_peers = len(DEVICE_FLIPS)
    n_chips = len(CHIP_FLIPS)
    n_e = len(earlier)

    def body(kc_ref, dp_ref, w_ref, d1_ref, *rest):
        sum_ins, g_in = rest[:2 * n_e], rest[2 * n_e]
        o_ref, g_out = rest[2 * n_e + 1], rest[2 * n_e + 2]
        sum_outs, sems = rest[2 * n_e + 3:3 * n_e + 3], rest[3 * n_e + 3:]
        copies = functools.partial(_chip_reduce_copies, g_in, g_out, sems)
        i = pl.program_id(0)
        pl.when(i == 0)(functools.partial(_start_copies, copies))
        acc = ALPHA * d1_ref[...]
        for k in range(N_CHIPS):
            acc = acc + lax.dot_general(dp_ref[:, IN_SHARD * k:IN_SHARD * (k + 1)], w_ref[k], NT_DIMS,
                                        preferred_element_type=F32)
        o_ref[...] = acc
        for a in range(n_e):
            _sum_with_peers(sum_ins[2 * a], sum_ins[2 * a + 1], sum_outs[a])
        pl.when(i == steps - 1)(functools.partial(_finish_copies, copies))

    in_specs = [pl.BlockSpec((tm, IN_COLS), lambda i, kc: (i, 0)),
                pl.BlockSpec((N_CHIPS, D_MODEL, IN_SHARD), lambda i, kc: (0, 0, 0)),
                pl.BlockSpec((tm, D_MODEL), lambda i, kc: (i, 0))]
    out_specs = [pl.BlockSpec((tm, D_MODEL), lambda i, kc: (i, 0)), ANY]
    out_shape = [SDS((s, D_MODEL), F32), SDS((n_chips,) + chip_sums.shape[1:], chip_sums.dtype)]
    operands = []
    for own, recv in earlier:
        _, _, h, cols = own.shape
        th = h // steps
        in_specs.append(pl.BlockSpec((None, None, th, cols), lambda i, kc: (kc[0], kc[1], i, 0)))
        in_specs.append(pl.BlockSpec((n_peers, th, cols), lambda i, kc: (0, i, 0)))
        out_specs.append(pl.BlockSpec((th, cols), lambda i, kc: (kc[1] * steps + i, 0)))
        out_shape.append(SDS((2 * h, cols), F32))
        operands += [own, recv]
    grid_spec = pltpu.PrefetchScalarGridSpec(
        num_scalar_prefetch=1, grid=(steps,), in_specs=in_specs + [ANY], out_specs=out_specs,
        scratch_shapes=[pltpu.SemaphoreType.DMA((n_chips,)), pltpu.SemaphoreType.DMA((n_chips,))])
    return pl.pallas_call(
        body, name="grad_x", grid_spec=grid_spec, out_shape=out_shape,
        compiler_params=_params(("arbitrary",)))(kc_idx, dproj, w_in, dpre1, *operands, chip_sums)


def _adamw_step(w, g, m, v):
    nm = ADAM_B1 * m + (1.0 - ADAM_B1) * g
    nv = ADAM_B2 * v + (1.0 - ADAM_B2) * (g * g)
    m_hat = nm / (1.0 - ADAM_B1 ** ADAM_STEP)
    v_hat = nv / (1.0 - ADAM_B2 ** ADAM_STEP)
    return -ADAM_LR * (m_hat / (jnp.sqrt(v_hat) + ADAM_EPS) + ADAM_WD * w), nm, nv


def _adamw_shards(weights, grads, moments, variances):
    n = len(weights)
    steps = SHARD_STEPS

    def body(*refs):
        w, g, m, v, outs = refs[:n], refs[n:2 * n], refs[2 * n:3 * n], refs[3 * n:4 * n], refs[4 * n:]
        for a in range(n):
            g_v = g[a][...]
            outs[4 * a][...] = g_v
            outs[4 * a + 1][...], outs[4 * a + 2][...], outs[4 * a + 3][...] = _adamw_step(
                w[a][...], g_v, m[a][...], v[a][...])

    spec = lambda arr: pl.BlockSpec((arr.shape[0] // steps, arr.shape[1]), lambda i: (i, 0))
    flat = pl.pallas_call(
        body, name="adamw_shards", grid=(steps,),
        in_specs=[spec(a) for a in weights] * 4, out_specs=[spec(a) for a in weights for _ in range(4)],
        out_shape=[SDS(a.shape, F32) for a in weights for _ in range(4)],
        compiler_params=_params(("parallel",)))(*weights, *grads, *moments, *variances)
    return [flat[4 * a:4 * a + 4] for a in range(n)]


def _adamw_small(total, conv_grad, weights, moments, variances):
    n = len(weights)
    starts = (ROW_GCONV, ROW_GATTN, ROW_LN1G, ROW_LN1B, ROW_LN2G, ROW_LN2B)

    def body(total_ref, cg_ref, *refs):
        w, m, v, outs = refs[:n], refs[n:2 * n], refs[2 * n:3 * n], refs[3 * n:]
        for p in range(n):
            rows = w[p].shape[0]
            g = cg_ref[...] if p == n - 1 else total_ref[starts[p]:starts[p] + rows, :]
            outs[4 * p][...] = g
            outs[4 * p + 1][...], outs[4 * p + 2][...], outs[4 * p + 3][...] = _adamw_step(
                w[p][...], g, m[p][...], v[p][...])

    vmem = pl.BlockSpec(memory_space=pltpu.VMEM)
    out_shape = [SDS(w.shape, F32) for w in weights for _ in range(4)]
    flat = pl.pallas_call(
        body, name="adamw_small", in_specs=[vmem] * (2 + 3 * n), out_specs=[vmem] * (4 * n),
        out_shape=out_shape)(total, conv_grad, *weights, *moments, *variances)
    return [flat[4 * p:4 * p + 4] for p in range(n)]


def _sum_partials(kc_idx, own, recv, name):
    h, cols = own.shape
    th = _row_tile(h, SUM_ROW_TILE)
    n_peers = recv.shape[0]

    def body(kc_ref, own_ref, r_ref, o_ref):
        _sum_with_peers(own_ref, r_ref, o_ref)

    grid_spec = pltpu.PrefetchScalarGridSpec(
        num_scalar_prefetch=1, grid=(h // th,),
        in_specs=[pl.BlockSpec((th, cols), lambda t, kc: (t, 0)),
                  pl.BlockSpec((n_peers, th, cols), lambda t, kc: (0, t, 0))],
        out_specs=pl.BlockSpec((th, cols), lambda t, kc: (kc[1] * (h // th) + t, 0)))
    return pl.pallas_call(
        body, name=name, grid_spec=grid_spec, out_shape=SDS((2 * h, cols), F32),
        compiler_params=_params(("parallel",)))(kc_idx, own, recv)


def _add_sibling(kc_idx, grad, recv):
    _, _, h, cols = grad.shape
    th = _row_tile(h, ROW_TILE)

    def body(kc_ref, g_ref, r_ref, sums_ref, own_ref):
        total = g_ref[...] + r_ref[...].astype(F32)
        sums_ref[...] = total.astype(BF16)

        @pl.when(pl.program_id(1) == kc_ref[0])
        def _():
            own_ref[...] = total

    grid_spec = pltpu.PrefetchScalarGridSpec(
        num_scalar_prefetch=1, grid=(h // th, N_CHIPS),
        in_specs=[pl.BlockSpec((None, None, th, cols), lambda t, k, kc: (k, kc[1], t, 0)),
                  pl.BlockSpec((None, th, cols), lambda t, k, kc: (k, t, 0))],
        out_specs=[pl.BlockSpec((None, th, cols), lambda t, k, kc: (k, t, 0)),
                   pl.BlockSpec((th, cols), lambda t, k, kc: (t, 0))])
    return pl.pallas_call(
        body, name="add_sibling_w_in", grid_spec=grid_spec,
        out_shape=[SDS((N_CHIPS, h, cols), BF16), SDS((h, cols), F32)],
        compiler_params=_params(("parallel", "arbitrary")))(kc_idx, grad, recv)


def _gather_weights(kc_idx, w_in_slots, conv_slots, later):
    n_l = len(later)
    steps = SHARD_STEPS
    n_sems = 6 + len(CHIP_FLIPS)

    def body(kc_ref, *refs):
        cast_ins, cast_outs = refs[:n_l], refs[n_l + 2:2 * n_l + 2]
        w_buf, conv_buf = refs[2 * n_l + 2], refs[2 * n_l + 3]
        sems = refs[2 * n_l + 4:]
        i = pl.program_id(0)

        def first_hop():
            x, y, c = _position()
            sends, arrivals = _gather_chip_hop([w_buf], sems)
            mine = conv_buf.at[2 * x + y]
            for j, (fx, fy) in enumerate(CHIP_FLIPS):
                tx, ty = _flip(x, fx), _flip(y, fy)
                there = conv_buf.at[2 * tx + ty]
                sends.append(_remote_copy(mine, mine, sems, 6 + j, (tx, ty, c)))
                arrivals.append(_remote_copy(there, there, sems, 6 + j, (tx, ty, c)))
            return sends, arrivals

        pl.when(i == 0)(functools.partial(_start_copies, first_hop))
        for src, dst in zip(cast_ins, cast_outs):
            dst[...] = src[...].astype(BF16)

        @pl.when(i == steps - 1)
        def _():
            sends, arrivals = first_hop()
            for cp in arrivals:
                cp.wait_recv()
            _start_copies(functools.partial(_gather_sibling_hop, [w_buf], sems))
            _finish_copies(functools.partial(_gather_sibling_hop, [w_buf], sems))
            for cp in sends:
                cp.wait_send()

    in_specs, out_specs, out_shape = [], [], []
    for w in later:
        r, c = w.shape
        in_specs.append(pl.BlockSpec((r // steps, c), lambda i, kc: (i, 0)))
        out_specs.append(pl.BlockSpec((None, r // steps, c), lambda i, kc: (kc[0], i, 0)))
        out_shape.append(SDS((N_CHIPS, r, c), BF16))
    grid_spec = pltpu.PrefetchScalarGridSpec(
        num_scalar_prefetch=1, grid=(steps,), in_specs=in_specs + [ANY, ANY], out_specs=out_specs + [ANY, ANY],
        scratch_shapes=[pltpu.SemaphoreType.DMA((n_sems,)), pltpu.SemaphoreType.DMA((n_sems,))])
    return pl.pallas_call(
        body, name="gather_weights", grid_spec=grid_spec,
        out_shape=out_shape + [SDS(w_in_slots.shape, w_in_slots.dtype), SDS(conv_slots.shape, conv_slots.dtype)],
        input_output_aliases={n_l + 1: n_l, n_l + 2: n_l + 1},
        compiler_params=_params(("arbitrary",)))(kc_idx, *later, w_in_slots, conv_slots)


def _exchange_with_sibling(partial):
    h = partial.shape[1] // 2

    def body(g_in, g_out, send_sems, recv_sems):
        x, y, c = _position()
        theirs = pl.ds(pl.multiple_of((1 - c) * h, h), h)
        copies = [_remote_copy(g_in.at[k, theirs], g_out.at[k], (send_sems, recv_sems), k, (x, y, 1 - c))
                  for k in range(N_CHIPS)]
        for cp in copies:
            cp.start()
        for cp in copies:
            cp.wait_recv()
        for cp in copies:
            cp.wait_send()

    return pl.pallas_call(
        body, name="exchange_with_sibling", in_specs=[ANY], out_specs=ANY,
        out_shape=SDS((N_CHIPS, h, partial.shape[2]), partial.dtype),
        scratch_shapes=[pltpu.SemaphoreType.DMA((N_CHIPS,)), pltpu.SemaphoreType.DMA((N_CHIPS,))])(partial)


def _finish_exchange(pieces, vec):
    n = len(pieces)
    n_dev = 2 * N_CHIPS

    def body(*refs):
        v_ref = refs[n]
        outs, o_ref = refs[n + 1:2 * n + 1], refs[2 * n + 1]
        buf, send_sems, recv_sems = refs[2 * n + 2:]
        x, y, c = _position()
        sibling = (x, y, 1 - c)
        me = 4 * x + 2 * y + c
        buf[me] = v_ref[...]
        started = []
        for f, (fx, fy, fc) in enumerate(DEVICE_FLIPS):
            cp = pltpu.make_async_remote_copy(
                src_ref=v_ref, dst_ref=buf.at[me], send_sem=send_sems.at[n + f], recv_sem=recv_sems.at[n + f],
                device_id=(_flip(x, fx), _flip(y, fy), _flip(c, fc)), device_id_type=MESH)
            cp.start()
            started.append(cp)
        for a in range(n):
            h = pieces[a].shape[0] // 2
            mine = outs[a].at[pl.ds(pl.multiple_of(c * h, h), h)]
            cp = pltpu.make_async_remote_copy(
                src_ref=mine, dst_ref=mine, send_sem=send_sems.at[a], recv_sem=recv_sems.at[a],
                device_id=sibling, device_id_type=MESH)
            cp.start()
            started.append(cp)
        for a in range(n):
            h = pieces[a].shape[0] // 2
            theirs = outs[a].at[pl.ds(pl.multiple_of((1 - c) * h, h), h)]
            pltpu.make_async_remote_copy(
                src_ref=theirs, dst_ref=theirs, send_sem=send_sems.at[a], recv_sem=recv_sems.at[a],
                device_id=sibling, device_id_type=MESH).wait_recv()
        for f, (fx, fy, fc) in enumerate(DEVICE_FLIPS):
            src = 4 * _flip(x, fx) + 2 * _flip(y, fy) + _flip(c, fc)
            pltpu.make_async_remote_copy(
                src_ref=v_ref, dst_ref=buf.at[src], send_sem=send_sems.at[n + f], recv_sem=recv_sems.at[n + f],
                device_id=(x, y, c), device_id_type=MESH).wait_recv()
        for cp in started:
            cp.wait_send()
        acc = buf[0]
        for d in range(1, n_dev):
            acc = acc + buf[d]
        o_ref[...] = acc

    vmem = pl.BlockSpec(memory_space=pltpu.VMEM)
    out_shape = [SDS(p.shape, p.dtype) for p in pieces] + [SDS(vec.shape, vec.dtype)]
    n_sems = n + n_dev - 1
    return pl.pallas_call(
        body, name="finish_exchange", in_specs=[ANY] * n + [vmem], out_specs=[ANY] * n + [vmem],
        out_shape=out_shape, input_output_aliases={a: a for a in range(n)},
        scratch_shapes=[pltpu.VMEM((n_dev,) + vec.shape, vec.dtype), pltpu.SemaphoreType.DMA((n_sems,)),
                        pltpu.SemaphoreType.DMA((n_sems,))])(*pieces, vec)


def _constants():
    r = jnp.arange(2 * KEY_BLOCK)[:, None] % KEY_BLOCK
    c = jnp.arange(2 * KEY_BLOCK)[None, :]
    later = jnp.where(c < KEY_BLOCK, r > c, True).astype(BF16)
    earlier = jnp.where(c < KEY_BLOCK, r < c, True).astype(BF16)
    upto = jnp.where(c < KEY_BLOCK, r <= c, True).astype(BF16)
    gr = (jnp.arange(2 * LANES)[:, None] % LANES) // GROUP
    gc = jnp.arange(LANES)[None, :] // GROUP
    gmat = (gr == gc).astype(BF16)
    return later, jnp.stack([earlier, upto]), gmat


def _rows(v):
    return v.reshape(-1, LANES)


def kernel(x, w_in, conv_w, g_conv, g_attn, w_out, ln1_g, ln1_b, w_up, w_down, ln2_g, ln2_b, loss_target, m_w_in, m_conv_w, m_g_conv, m_g_attn, m_w_out, m_ln1_g, m_ln1_b, m_w_up, m_w_down, m_ln2_g, m_ln2_b, v_w_in, v_conv_w, v_g_conv, v_g_attn, v_w_out, v_ln1_g, v_ln1_b, v_w_up, v_w_down, v_ln2_g, v_ln2_b):
    xs, target = x[0], loss_target[0]
    mesh_x, mesh_y, mesh_c = _position()
    k_idx = 2 * mesh_x + mesh_y
    kc_idx = jnp.stack([k_idx, mesh_c]).astype(jnp.int32)
    tri_later, tri_earlier, gmat = _constants()

    w_in_b = _cast_into_slot(kc_idx, w_in[0], "cast_w_in")
    conv_slot = jnp.pad(conv_w, ((0, 0), (0, SUBLANES - conv_w.shape[1]), (0, 0)))
    conv_b = lax.dynamic_update_slice(jnp.zeros((N_CHIPS, SUBLANES, LANES), F32), conv_slot, (k_idx, 0, 0))
    w_out_b, w_up_b, w_down_b, w_in_f, conv_f = _gather_weights(
        kc_idx, w_in_b, conv_b, [w_out[0], w_up[0], w_down[0]])
    taps = jnp.transpose(conv_f, (1, 0, 2)).reshape(SUBLANES, CONV_WIDTH)

    gates, qkv, xs_b, ycn = _proj(xs, w_in_f, taps, g_conv, gmat)
    o, yan, tot, cut, w_out_f, w_up_f, w_down_f = _attn_fwd(
        qkv, g_attn, tri_later, gmat, [w_out_b, w_up_b, w_down_b])
    w_out_f = w_out_f.reshape(D_MODEL, D_MODEL)
    xhat1, rstd1, x1_b = _mix_ln1(ycn, yan, w_out_f, xs, ln1_g, ln1_b)
    dpre2, ln2_sums, loss_sum, relu_up, hid, dpre2_b = _mlp_fwd_loss(
        xhat1, ln1_g, ln1_b, w_up_f, w_down_f, target, ln2_g, ln2_b)

    dup, dpre1, ln1_sums, dpre1_b, dycn, dyan = _mlp_bwd_ln1(
        relu_up, dpre2, w_up_f, w_down_f, xhat1, rstd1, ln1_g, w_out_f)
    gw_up = _grad_tn(x1_b, dup, "grad_w_up", FF_SHARD, True)
    gw_down = [g.reshape(N_CHIPS, FF_SHARD, D_MODEL) for g in _grad_tn(hid, dpre2_b, "grad_w_down", D_MODEL, False)]
    gw_out = [g.reshape(N_CHIPS, D_MODEL // N_CHIPS, D_MODEL) for g in _grad_w_out(ycn, yan, dpre1_b)]
    dq, dk, dv, gattn_sums, dbg, dy, conv_sums, recv_out, recv_up, recv_down = _attn_bwd(
        qkv, o, tot, dyan, g_attn, tri_earlier, gmat, cut, gates, dycn, taps, g_conv,
        [gw_out[1], gw_up[1], gw_down[1]])
    dproj = _dproj_assemble(gates, dy, dbg, dq, dk, dv, taps)
    gw_in = _grad_tn(xs_b, dproj, "grad_w_in", IN_SHARD, True)
    halves = lambda g: g.reshape(N_CHIPS, 2, g.shape[1] // 2, g.shape[2])
    chip_sums, own_sum = _add_sibling(kc_idx, halves(gw_in[0]), _exchange_with_sibling(gw_in[1]))
    grad_x, recv_in, p_out, p_up, p_down = _grad_x(
        kc_idx, dproj, w_in_f, dpre1, chip_sums,
        [(halves(gw_out[0]), recv_out), (halves(gw_up[0]), recv_up), (halves(gw_down[0]), recv_down)])
    pieces = [_sum_partials(kc_idx, own_sum, recv_in, "sum_partials_w_in"), p_out, p_up, p_down]
    conv_rows = jnp.transpose(conv_sums[0:3].reshape(3, N_CHIPS, LANES), (1, 0, 2)).reshape(3 * N_CHIPS, LANES)
    small = jnp.concatenate([
        loss_sum, _rows(conv_sums[3]), _rows(gattn_sums[0]), _rows(ln1_sums[0]), _rows(ln1_sums[1]),
        _rows(ln2_sums[0]), _rows(ln2_sums[1]), conv_rows,
        jnp.zeros((SMALL_ROWS - ROW_CONVW - 3 * N_CHIPS, LANES), F32)], axis=0)
    g_w_in, g_w_out, g_w_up, g_w_down, total = _finish_exchange(pieces, small)
    loss = total[ROW_LOSS, 0]
    g_conv_w = lax.dynamic_slice(total, (ROW_CONVW + 3 * k_idx, 0), (3, LANES))

    small_names = ["g_conv", "g_attn", "ln1_g", "ln1_b", "ln2_g", "ln2_b", "conv_w"]
    small_w = [g_conv, g_attn, ln1_g, ln1_b, ln2_g, ln2_b, conv_w]
    small_m = [m_g_conv, m_g_attn, m_ln1_g, m_ln1_b, m_ln2_g, m_ln2_b, m_conv_w]
    small_v = [v_g_conv, v_g_attn, v_ln1_g, v_ln1_b, v_ln2_g, v_ln2_b, v_conv_w]
    small_out = dict(zip(small_names, _adamw_small(
        total, g_conv_w, [_rows(a) for a in small_w], [_rows(a) for a in small_m], [_rows(a) for a in small_v])))
    small_shape = dict(zip(small_names, (a.shape for a in small_w)))
    big_out = dict(zip(["w_in", "w_out", "w_up", "w_down"], _adamw_shards(
        [w_in[0], w_out[0], w_up[0], w_down[0]], [g_w_in, g_w_out, g_w_up, g_w_down],
        [m_w_in[0], m_w_out[0], m_w_up[0], m_w_down[0]], [v_w_in[0], v_w_out[0], v_w_up[0], v_w_down[0]])))
    order = ["w_in", "conv_w", "g_conv", "g_attn", "w_out", "ln1_g", "ln1_b", "w_up", "w_down", "ln2_g", "ln2_b"]

    def leaf(kind, name):
        if name in big_out:
            return big_out[name][kind][None]
        return small_out[name][kind].reshape(small_shape[name])

    outs = [loss, grad_x[None]]
    for kind in range(4):
        outs.extend(leaf(kind, name) for name in order)
    return tuple(outs)
```

```python
import functools

import jax
import jax.numpy as jnp
from jax import lax
from jax.experimental import pallas as pl
from jax.experimental.pallas import tpu as pltpu

F32 = jnp.float32
BF16 = jnp.bfloat16
SDS = jax.ShapeDtypeStruct

D_MODEL = 1024
CONV_WIDTH = 512
ATTN_WIDTH = 512
GROUP = 64
GATE_COLS = 3 * CONV_WIDTH
QKV_COLS = 3 * ATTN_WIDTH
IN_COLS = GATE_COLS + QKV_COLS
D_FF = 4 * D_MODEL
N_CHIPS = 4
IN_SHARD = IN_COLS // N_CHIPS
FF_SHARD = D_FF // N_CHIPS
ALPHA = float(2.0 ** 0.25)
LN_EPS = 1e-5
RMS_EPS = 1e-6
ATTN_SCALE = GROUP ** -0.5
LOG2_E = 1.4426950408889634
ADAM_LR = 0.001
ADAM_B1 = 0.9
ADAM_B2 = 0.999
ADAM_EPS = 1e-08
ADAM_WD = 0.01
ADAM_STEP = 10

LANES = 128
SUBLANES = 8
KEY_BLOCK = 128
ATTN_Q_TILE = 512
ATTN_KEY_BLOCKS = 2
ATTN_DIAG_GROUPS = 2
ATTN_DEAD_LOG2 = 200.0
VMEM_LIMIT = 56 * 1024 * 1024
ROW_TILE = 512
MLP_ROW_TILE = 256
GRAD_SEQ_TILE = 4096
SHARD_ROW_TILE = 256
SUM_ROW_TILE = 128
SHARD_STEPS = 8

MESH = pl.DeviceIdType.MESH
CHIP_FLIPS = ((1, 0), (0, 1), (1, 1))
DEVICE_FLIPS = tuple((fx, fy, fc) for fx in (0, 1) for fy in (0, 1) for fc in (0, 1))[1:]
NT_DIMS = (((1,), (1,)), ((), ()))
TN_DIMS = (((0,), (0,)), ((), ()))

ROW_LOSS = 0
ROW_GCONV = 8
ROW_GATTN = 12
ROW_LN1G = 16
ROW_LN1B = 24
ROW_LN2G = 32
ROW_LN2B = 40
ROW_CONVW = 48
SMALL_ROWS = 64


def _params(sem=None):
    return pltpu.CompilerParams(dimension_semantics=sem, vmem_limit_bytes=VMEM_LIMIT)


def _flip(v, f):
    return 1 - v if f else v


def _position():
    return lax.axis_index("x"), lax.axis_index("y"), lax.axis_index("c")


def _hilo(v):
    hi = v.astype(BF16)
    lo = (v - hi.astype(F32)).astype(BF16)
    return jnp.concatenate([hi, lo], axis=1)


def _hilo_dot(v, mat):
    return jnp.dot(_hilo(v), mat, preferred_element_type=F32)


def _group_sum(v, gmat):
    parts = [_hilo_dot(v[:, LANES * j:LANES * (j + 1)], gmat) for j in range(v.shape[1] // LANES)]
    return parts[0] if len(parts) == 1 else jnp.concatenate(parts, axis=1)


def _softplus_terms(z):
    sp = jnp.log2(1.0 + jnp.exp2(-jnp.abs(z)))
    log_beta = jnp.minimum(z, 0.0) - sp
    return log_beta, log_beta - z


def _layer_norm_fwd(pre, g, b):
    mu = jnp.mean(pre, axis=-1, keepdims=True)
    d = pre - mu
    var = jnp.mean(d * d, axis=-1, keepdims=True)
    rstd = lax.rsqrt(var + LN_EPS)
    xhat = d * rstd
    return xhat * g + b, xhat, rstd


def _layer_norm_bwd(dy, xhat, rstd, g):
    dxh = dy * g
    m1 = jnp.mean(dxh, axis=-1, keepdims=True)
    m2 = jnp.mean(dxh * xhat, axis=-1, keepdims=True)
    return rstd * (dxh - m1 - xhat * m2)


def _row_tile(s, want):
    return min(s, want)


def _cast_into_slot(kc_idx, w, name):
    r, c = w.shape
    tr = _row_tile(r, SHARD_ROW_TILE)

    def body(kc_ref, w_ref, o_ref):
        o_ref[...] = w_ref[...].astype(BF16)

    grid_spec = pltpu.PrefetchScalarGridSpec(
        num_scalar_prefetch=1, grid=(r // tr,),
        in_specs=[pl.BlockSpec((tr, c), lambda i, kc: (i, 0))],
        out_specs=pl.BlockSpec((None, tr, c), lambda i, kc: (kc[0], i, 0)))
    return pl.pallas_call(
        body, name=name, grid_spec=grid_spec, out_shape=SDS((N_CHIPS, r, c), BF16),
        compiler_params=_params(("parallel",)))(kc_idx, w)


def _proj(x, w_in, taps, g_conv, gmat):
    s = x.shape[0]
    tm = _row_tile(s, ROW_TILE)

    def body(x_ref, w_ref, taps_ref, gain_ref, gmat_ref, gates_ref, qkv_ref, xb_ref, ycn_ref, halo_ref):
        i = pl.program_id(0)
        xb = x_ref[...].astype(BF16)
        xb_ref[...] = xb
        for k in range(N_CHIPS):
            acc = jnp.dot(xb, w_ref[k], preferred_element_type=F32)
            if k < 2:
                gates_ref[:, IN_SHARD * k:IN_SHARD * (k + 1)] = acc
            else:
                qkv_ref[:, IN_SHARD * (k - 2):IN_SHARD * (k - 1)] = acc.astype(BF16)
        bg, _, _, _, _, _, y = _conv_forward_values(gates_ref, halo_ref, taps_ref, i == 0)
        yc = bg * y
        ms = _group_sum(yc * yc, gmat_ref[...]) * (1.0 / GROUP)
        ycn_ref[...] = (yc * lax.rsqrt(ms + RMS_EPS) * gain_ref[...]).astype(BF16)
        halo_ref[...] = gates_ref[tm - SUBLANES:tm, :]

    return pl.pallas_call(
        body, name="proj", grid=(s // tm,),
        in_specs=[pl.BlockSpec((tm, D_MODEL), lambda i: (i, 0)),
                  pl.BlockSpec((N_CHIPS, D_MODEL, IN_SHARD), lambda i: (0, 0, 0)),
                  pl.BlockSpec((SUBLANES, CONV_WIDTH), lambda i: (0, 0)),
                  pl.BlockSpec((1, CONV_WIDTH), lambda i: (0, 0)),
                  pl.BlockSpec((2 * LANES, LANES), lambda i: (0, 0))],
        out_specs=[pl.BlockSpec((tm, GATE_COLS), lambda i: (i, 0)),
                   pl.BlockSpec((tm, QKV_COLS), lambda i: (i, 0)),
                   pl.BlockSpec((tm, D_MODEL), lambda i: (i, 0)),
                   pl.BlockSpec((tm, CONV_WIDTH), lambda i: (i, 0))],
        out_shape=[SDS((s, GATE_COLS), F32), SDS((s, QKV_COLS), BF16), SDS((s, D_MODEL), BF16),
                   SDS((s, CONV_WIDTH), BF16)],
        scratch_shapes=[pltpu.VMEM((SUBLANES, GATE_COLS), F32)],
        compiler_params=_params(("arbitrary",)))(x, w_in, taps, g_conv, gmat)


def _conv_forward_values(g_ref, halo_ref, taps_ref, first_block):
    gates = g_ref[...]
    tr = gates.shape[0]
    bg = gates[:, :CONV_WIDTH]
    cg = gates[:, CONV_WIDTH:2 * CONV_WIDTH]
    h = gates[:, 2 * CONV_WIDTH:]
    u = cg * h

    def prev(r):
        v = halo_ref[r:r + 1, CONV_WIDTH:2 * CONV_WIDTH] * halo_ref[r:r + 1, 2 * CONV_WIDTH:GATE_COLS]
        return jnp.where(first_block, 0.0, v)

    row = lax.broadcasted_iota(jnp.int32, (tr, CONV_WIDTH), 0)
    u1 = jnp.where(row == 0, prev(7), pltpu.roll(u, 1, 0))
    u2 = jnp.where(row == 0, prev(6), jnp.where(row == 1, prev(7), pltpu.roll(u, 2, 0)))
    y = taps_ref[0:1, :] * u2 + taps_ref[1:2, :] * u1 + taps_ref[2:3, :] * u
    return bg, cg, h, u, u1, u2, y


def _conv_bwd_gate_step(first_block, g_ref, halo_ref, dn_ref, taps_ref, gain_ref, gmat_v, dbg_ref, dy_ref, sums_ref):
    bg, _, _, u, u1, u2, y = _conv_forward_values(g_ref, halo_ref, taps_ref, first_block)
    yc = bg * y
    rstd = lax.rsqrt(_group_sum(yc * yc, gmat_v) * (1.0 / GROUP) + RMS_EPS)
    n = yc * rstd
    dout = dn_ref[...]
    dn = dout * gain_ref[...]
    dyc = rstd * (dn - n * (_group_sum(dn * n, gmat_v) * (1.0 / GROUP)))
    dbg_ref[...] = (dyc * y).astype(BF16)
    dy = dyc * bg
    dy_ref[...] = dy

    @pl.when(first_block)
    def _():
        sums_ref[...] = jnp.zeros_like(sums_ref)

    sums_ref[0:1, :] += jnp.sum(dy * u2, axis=0, keepdims=True)
    sums_ref[1:2, :] += jnp.sum(dy * u1, axis=0, keepdims=True)
    sums_ref[2:3, :] += jnp.sum(dy * u, axis=0, keepdims=True)
    sums_ref[3:4, :] += jnp.sum(dout * n, axis=0, keepdims=True)


def _dproj_assemble(gates, dy, dbg, dq, dk, dv, taps):
    s = gates.shape[0]
    tr = _row_tile(s, ROW_TILE)
    hb = tr // SUBLANES
    last = s // SUBLANES - 1
    n_blocks = s // tr

    def body(g_ref, dy_ref, halo_ref, dbg_ref, dq_ref, dk_ref, dv_ref, taps_ref, out_ref):
        i = pl.program_id(0)
        gates_v = g_ref[...]
        cg = gates_v[:, CONV_WIDTH:2 * CONV_WIDTH]
        h = gates_v[:, 2 * CONV_WIDTH:]
        dy_v = dy_ref[...]
        last_block = i == n_blocks - 1
        nxt = lambda r: jnp.where(last_block, 0.0, halo_ref[r:r + 1, :])
        row = lax.broadcasted_iota(jnp.int32, (tr, CONV_WIDTH), 0)
        d1 = jnp.where(row == tr - 1, nxt(0), pltpu.roll(dy_v, tr - 1, 0))
        d2 = jnp.where(row == tr - 1, nxt(1), jnp.where(row == tr - 2, nxt(0), pltpu.roll(dy_v, tr - 2, 0)))
        du = taps_ref[2:3, :] * dy_v + taps_ref[1:2, :] * d1 + taps_ref[0:1, :] * d2
        out_ref[:, 0:CONV_WIDTH] = dbg_ref[...]
        out_ref[:, CONV_WIDTH:2 * CONV_WIDTH] = (du * h).astype(BF16)
        out_ref[:, 2 * CONV_WIDTH:GATE_COLS] = (du * cg).astype(BF16)
        out_ref[:, GATE_COLS:GATE_COLS + ATTN_WIDTH] = dq_ref[...]
        out_ref[:, GATE_COLS + ATTN_WIDTH:GATE_COLS + 2 * ATTN_WIDTH] = dk_ref[...]
        out_ref[:, GATE_COLS + 2 * ATTN_WIDTH:] = dv_ref[...]

    row_spec = lambda w: pl.BlockSpec((tr, w), lambda i: (i, 0))
    return pl.pallas_call(
        body, name="dproj_assemble", grid=(s // tr,),
        in_specs=[row_spec(GATE_COLS), row_spec(CONV_WIDTH),
                  pl.BlockSpec((SUBLANES, CONV_WIDTH), lambda i: (jnp.minimum((i + 1) * hb, last), 0)),
                  row_spec(CONV_WIDTH), row_spec(ATTN_WIDTH), row_spec(ATTN_WIDTH), row_spec(ATTN_WIDTH),
                  pl.BlockSpec((SUBLANES, CONV_WIDTH), lambda i: (0, 0))],
        out_specs=row_spec(IN_COLS),
        out_shape=SDS((s, IN_COLS), BF16),
        compiler_params=_params(("parallel",)))(gates, dy, dy, dbg, dq, dk, dv, taps)


def _stack_heads(rows, nb):
    lane = lax.broadcasted_iota(jnp.int32, (1, LANES), 1)
    zero = jnp.zeros((KEY_BLOCK, LANES), rows.dtype)
    parts = []
    for blk in range(nb):
        r = rows[blk * KEY_BLOCK:(blk + 1) * KEY_BLOCK]
        parts.append(jnp.where(lane < GROUP, r, zero))
        parts.append(jnp.where(lane < GROUP, zero, r))
    return jnp.concatenate(parts, axis=0)


def _stack_hilo(v, n_cols):
    return jnp.concatenate([_hilo(v[:, c * KEY_BLOCK:(c + 1) * KEY_BLOCK]) for c in range(n_cols)], axis=0)


def _causal_mask(tq, nb, diag_base):
    shape = (tq, 2 * nb * KEY_BLOCK)
    row = lax.broadcasted_iota(jnp.int32, shape, 0)
    col = lax.broadcasted_iota(jnp.int32, shape, 1)
    key = diag_base + (col // (2 * KEY_BLOCK)) * KEY_BLOCK + col % KEY_BLOCK
    return key < row


ANY = pl.BlockSpec(memory_space=pl.ANY)


def _remote_copy(src, dst, sems, idx, target):
    return pltpu.make_async_remote_copy(src_ref=src, dst_ref=dst, send_sem=sems[0].at[idx], recv_sem=sems[1].at[idx],
                                        device_id=target, device_id_type=MESH)


def _gather_chip_hop(bufs, sems):
    x, y, c = _position()
    sends, arrivals = [], []
    for a, buf in enumerate(bufs):
        h = buf.shape[1] // 2
        rows = pl.ds(pl.multiple_of(c * h, h), h)
        mine = buf.at[2 * x + y, rows]
        for j, (fx, fy) in enumerate(CHIP_FLIPS):
            tx, ty = _flip(x, fx), _flip(y, fy)
            there = buf.at[2 * tx + ty, rows]
            sends.append(_remote_copy(mine, mine, sems, 6 * a + j, (tx, ty, c)))
            arrivals.append(_remote_copy(there, there, sems, 6 * a + j, (tx, ty, c)))
    return sends, arrivals


def _gather_sibling_hop(bufs, sems):
    x, y, c = _position()
    sends, arrivals = [], []
    for a, buf in enumerate(bufs):
        h = buf.shape[1] // 2
        mine, theirs = pl.ds(pl.multiple_of(c * h, h), h), pl.ds(pl.multiple_of((1 - c) * h, h), h)
        for j, (fx, fy) in enumerate(CHIP_FLIPS):
            kj = 2 * _flip(x, fx) + _flip(y, fy)
            landed, other = buf.at[kj, mine], buf.at[kj, theirs]
            sends.append(_remote_copy(landed, landed, sems, 6 * a + 3 + j, (x, y, 1 - c)))
            arrivals.append(_remote_copy(other, other, sems, 6 * a + 3 + j, (x, y, 1 - c)))
    return sends, arrivals


def _reduce_copies(ins, outs, sems):
    x, y, c = _position()
    sends, arrivals = [], []
    for a in range(len(ins)):
        h = ins[a].shape[1] // 2
        for f, (fx, fy, fc) in enumerate(DEVICE_FLIPS):
            tx, ty, tc = _flip(x, fx), _flip(y, fy), _flip(c, fc)
            src = ins[a].at[2 * tx + ty, pl.ds(pl.multiple_of(tc * h, h), h)]
            sends.append(_remote_copy(src, outs[a].at[f], sems, 7 * a + f, (tx, ty, tc)))
            arrivals.append(_remote_copy(outs[a].at[f], outs[a].at[f], sems, 7 * a + f, (tx, ty, tc)))
    return sends, arrivals


def _chip_reduce_copies(src, dst, sems):
    x, y, c = _position()
    sends, arrivals = [], []
    for j, (fx, fy) in enumerate(CHIP_FLIPS):
        tx, ty = _flip(x, fx), _flip(y, fy)
        sends.append(_remote_copy(src.at[2 * tx + ty], dst.at[j], sems, j, (tx, ty, c)))
        arrivals.append(_remote_copy(dst.at[j], dst.at[j], sems, j, (tx, ty, c)))
    return sends, arrivals


def _start_copies(make):
    sends, _ = make()
    for cp in sends:
        cp.start()


def _finish_copies(make):
    sends, arrivals = make()
    for cp in arrivals:
        cp.wait_recv()
    for cp in sends:
        cp.wait_send()


def _attn_fwd(qkv, g_attn, tri, gmat, shards):
    n_w = len(shards)
    s = qkv.shape[0]
    tq = _row_tile(s, ATTN_Q_TILE)
    tk = KEY_BLOCK
    nb = ATTN_KEY_BLOCKS
    width = nb * tk
    n_groups = ATTN_DIAG_GROUPS
    group = tq // n_groups
    pairs = ATTN_WIDTH // LANES

    def body(q_ref, k_ref, v_ref, gain_ref, tri_ref, gmat_ref, *rest):
        o_ref, yn_ref, tot_ref, cut_ref = rest[n_w:n_w + 4]
        w_bufs, sems = rest[n_w + 4:2 * n_w + 4], rest[2 * n_w + 4:]
        chip_hop = functools.partial(_gather_chip_hop, w_bufs, sems)
        sibling_hop = functools.partial(_gather_sibling_hop, w_bufs, sems)
        p, i = pl.program_id(0), pl.program_id(1)
        pl.when((p == 0) & (i == 0))(functools.partial(_start_copies, chip_hop))

        @pl.when((p == pairs - 1) & (i == 0))
        def _():
            for cp in chip_hop()[1]:
                cp.wait_recv()
            _start_copies(sibling_hop)

        q2 = q_ref[...]
        tri_v = tri_ref[...]

        def trip(s0, n_blk, rows, carry, diag_base):
            r0, nr = rows
            run = [carry[0], carry[1]]
            oacc = carry[2]
            ksel = _stack_heads(k_ref[pl.ds(s0, n_blk * tk), :], n_blk)
            vsel = _stack_heads(v_ref[pl.ds(s0, n_blk * tk), :], n_blk)
            z = lax.dot_general(q2[r0:r0 + nr], ksel, NT_DIMS, preferred_element_type=F32) * (ATTN_SCALE * LOG2_E)
            log_beta, log_keep = _softplus_terms(z)
            if diag_base is not None:
                valid = _causal_mask(nr, n_blk, diag_base)
                log_keep = jnp.where(valid, log_keep, 0.0)
            ct = jnp.dot(_stack_hilo(log_keep, 2 * n_blk), tri_v, preferred_element_type=F32)
            a_parts = [None] * (2 * n_blk)
            for c in reversed(range(2 * n_blk)):
                h = c % 2
                ct_c = ct[c * nr:(c + 1) * nr]
                a_parts[c] = jnp.exp2(log_beta[:, c * tk:(c + 1) * tk] + ct_c[:, :tk] + run[h])
                run[h] = run[h] + ct_c[:, tk:]
            a = jnp.concatenate(a_parts, axis=1)
            if diag_base is not None:
                a = jnp.where(valid, a, 0.0)
            oacc = oacc + jnp.dot(a.astype(BF16), vsel, preferred_element_type=F32)
            return run[0], run[1], oacc

        n_full = i * (tq // width)
        tile = p * pl.num_programs(1) + i

        def alive(run_a, run_b):
            return jnp.max(jnp.maximum(run_a, run_b)) > -ATTN_DEAD_LOG2

        groups = []
        for g in range(n_groups):
            rows = (g * group, group)
            zeros = (jnp.zeros((group, tk), F32), jnp.zeros((group, tk), F32), jnp.zeros((group, LANES), F32))
            state = trip(pl.multiple_of(i * tq, tq), (g + 1) * group // tk, rows, zeros, -g * group)

            def earlier_trip(c, rows=rows):
                done, _, run_a, run_b, oacc = c
                s0 = pl.multiple_of((n_full - 1 - done) * width, width)
                run_a, run_b, oacc = trip(s0, nb, rows, (run_a, run_b, oacc), None)
                return done + 1, alive(run_a, run_b), run_a, run_b, oacc

            swept = lax.while_loop(lambda c: (c[0] < n_full) & c[1], earlier_trip,
                                   (jnp.int32(0), alive(state[0], state[1])) + state)
            cut_ref[n_groups * tile + g] = (n_full - swept[0]).astype(F32)
            groups.append(swept[2:])
        run_a, run_b, oacc = (jnp.concatenate([grp[j] for grp in groups], axis=0) for j in range(3))
        lane = lax.broadcasted_iota(jnp.int32, (1, LANES), 1)
        o_ref[...] = oacc
        tot_ref[...] = jnp.where(lane < GROUP, run_a, run_b)
        ms = _group_sum(oacc * oacc, gmat_ref[...]) * (1.0 / GROUP)
        yn_ref[...] = (oacc * lax.rsqrt(ms + RMS_EPS) * gain_ref[...]).astype(BF16)

        @pl.when((p == pairs - 1) & (i == pl.num_programs(1) - 1))
        def _():
            for cp in chip_hop()[0]:
                cp.wait_send()
            _finish_copies(sibling_hop)

    blk = lambda: pl.BlockSpec((tq, LANES), lambda p, i: (i, p))
    return pl.pallas_call(
        body, name="attn_fwd", grid=(pairs, s // tq),
        in_specs=[pl.BlockSpec((tq, LANES), lambda p, i: (i, p)),
                  pl.BlockSpec((s, LANES), lambda p, i: (0, pairs + p)),
                  pl.BlockSpec((s, LANES), lambda p, i: (0, 2 * pairs + p)),
                  pl.BlockSpec((1, LANES), lambda p, i: (0, p)),
                  pl.BlockSpec((2 * tk, 2 * tk), lambda p, i: (0, 0)),
                  pl.BlockSpec((2 * LANES, LANES), lambda p, i: (0, 0))] + [ANY] * n_w,
        out_specs=[blk(), blk(), blk(), pl.BlockSpec(memory_space=pltpu.SMEM)] + [ANY] * n_w,
        out_shape=[SDS((s, ATTN_WIDTH), F32), SDS((s, ATTN_WIDTH), BF16), SDS((s, ATTN_WIDTH), F32),
                   SDS((n_groups * pairs * (s // tq),), F32)]
        + [SDS(w.shape, w.dtype) for w in shards],
        input_output_aliases={6 + a: 4 + a for a in range(n_w)},
        scratch_shapes=[pltpu.SemaphoreType.DMA((6 * n_w,)), pltpu.SemaphoreType.DMA((6 * n_w,))],
        compiler_params=_params(("arbitrary", "arbitrary")))(qkv, qkv, qkv, g_attn, tri, gmat, *shards)


def _attn_bwd(qkv, o, tot, dyn, g_attn, tri, gmat, cut, gates, dycn, taps, g_conv, partials):
    n_g = len(partials)
    s = qkv.shape[0]
    tq = _row_tile(s, ATTN_Q_TILE)
    tk = KEY_BLOCK
    nb = ATTN_KEY_BLOCKS
    width = nb * tk
    n_groups = ATTN_DIAG_GROUPS
    group = tq // n_groups
    pairs = ATTN_WIDTH // LANES

    def body(q_ref, k_ref, v_ref, o_ref, tot_ref, dyn_ref, gain_ref, tri_ref, gmat_ref, cut_ref,
             gates_ref, halo_ref, dycn_ref, taps_ref, gconv_ref, *rest):
        g_ins, (dq_ref, dk_out, dv_out, dg_ref) = rest[:n_g], rest[n_g:n_g + 4]
        dbg_ref, dy_ref, conv_sums_ref = rest[n_g + 4:n_g + 7]
        g_outs, sems = rest[n_g + 7:2 * n_g + 7], rest[2 * n_g + 7:2 * n_g + 9]
        dk_ref, dv_ref = rest[2 * n_g + 9:]
        copies = functools.partial(_reduce_copies, g_ins, g_outs, sems)
        p, i = pl.program_id(0), pl.program_id(1)
        pl.when((p == 0) & (i == 0))(functools.partial(_start_copies, copies))

        @pl.when(i == 0)
        def _():
            dk_ref[...] = jnp.zeros_like(dk_ref)
            dv_ref[...] = jnp.zeros_like(dv_ref)
            dg_ref[...] = jnp.zeros_like(dg_ref)

        gmat_v = gmat_ref[...]
        _conv_bwd_gate_step((p == 0) & (i == 0), gates_ref, halo_ref, dycn_ref, taps_ref, gconv_ref, gmat_v,
                            dbg_ref, dy_ref, conv_sums_ref)
        o_v = o_ref[...]
        rstd = lax.rsqrt(_group_sum(o_v * o_v, gmat_v) * (1.0 / GROUP) + RMS_EPS)
        n = o_v * rstd
        dout = dyn_ref[...]
        dg_ref[0:1, :] += jnp.sum(dout * n, axis=0, keepdims=True)
        dn = dout * gain_ref[...]
        do2 = (rstd * (dn - n * (_group_sum(dn * n, gmat_v) * (1.0 / GROUP)))).astype(BF16)
        q2 = q_ref[...]
        tot_v = tot_ref[...]
        tots = (jnp.broadcast_to(tot_v[:, 0:1], (tq, tk)), jnp.broadcast_to(tot_v[:, GROUP:GROUP + 1], (tq, tk)))
        tri_v, tri_incl_v = tri_ref[0], tri_ref[1]
        lane = lax.broadcasted_iota(jnp.int32, (1, LANES), 1)

        def trip(s0, n_blk, rows, carry, diag_base):
            r0, nr = rows
            rest_l = [carry[0], carry[1]]
            pref_g = [carry[2], carry[3]]
            dq = carry[4]
            q_rows, do_rows = q2[r0:r0 + nr], do2[r0:r0 + nr]
            ksel = _stack_heads(k_ref[pl.ds(s0, n_blk * tk), :], n_blk)
            vsel = _stack_heads(v_ref[pl.ds(s0, n_blk * tk), :], n_blk)
            z = lax.dot_general(q_rows, ksel, NT_DIMS, preferred_element_type=F32) * (ATTN_SCALE * LOG2_E)
            log_beta, log_keep = _softplus_terms(z)
            if diag_base is not None:
                valid = _causal_mask(nr, n_blk, diag_base)
                log_keep = jnp.where(valid, log_keep, 0.0)
            ctl = jnp.dot(_stack_hilo(log_keep, 2 * n_blk), tri_incl_v, preferred_element_type=F32)
            da = lax.dot_general(do_rows, vsel, NT_DIMS, preferred_element_type=F32)
            a_parts = []
            for c in range(2 * n_blk):
                h = c % 2
                ct_c = ctl[c * nr:(c + 1) * nr]
                cols = slice(c * tk, (c + 1) * tk)
                a_parts.append(jnp.exp2(log_beta[:, cols] + (rest_l[h] - ct_c[:, :tk])))
                rest_l[h] = rest_l[h] - ct_c[:, tk:]
            a = jnp.concatenate(a_parts, axis=1)
            if diag_base is not None:
                a = jnp.where(valid, a, 0.0)
            g = a * da
            ctg = jnp.dot(_stack_hilo(g, 2 * n_blk), tri_v, preferred_element_type=F32)
            dz_parts = []
            for c in range(2 * n_blk):
                h = c % 2
                ct_c = ctg[c * nr:(c + 1) * nr]
                cols = slice(c * tk, (c + 1) * tk)
                prefix = pref_g[h] + ct_c[:, :tk]
                pref_g[h] = pref_g[h] + ct_c[:, tk:]
                g_c = g[:, cols]
                dz_parts.append(g_c - jnp.exp2(log_beta[:, cols]) * (g_c + prefix))
            dz = jnp.concatenate(dz_parts, axis=1) * ATTN_SCALE
            if diag_base is not None:
                dz = jnp.where(valid, dz, 0.0)
            dzb = dz.astype(BF16)
            dq = dq + jnp.dot(dzb, ksel, preferred_element_type=F32)
            dkt = lax.dot_general(dzb, q_rows, TN_DIMS, preferred_element_type=F32)
            dvt = lax.dot_general(a.astype(BF16), do_rows, TN_DIMS, preferred_element_type=F32)
            for blk in range(n_blk):
                ra, rb = slice(2 * blk * tk, (2 * blk + 1) * tk), slice((2 * blk + 1) * tk, (2 * blk + 2) * tk)
                keys = pl.ds(pl.multiple_of(s0 + blk * tk, tk), tk)
                dk_ref[keys, :] += jnp.where(lane < GROUP, dkt[ra], dkt[rb])
                dv_ref[keys, :] += jnp.where(lane < GROUP, dvt[ra], dvt[rb])
            return rest_l[0], rest_l[1], pref_g[0], pref_g[1], dq

        n_full = i * (tq // width)
        tile = p * pl.num_programs(1) + i
        dq_groups = []
        for g in range(n_groups):
            rows = (g * group, group)
            first = jnp.clip(cut_ref[n_groups * tile + g].astype(jnp.int32), 0, n_full)
            zeros_qk = jnp.zeros((group, tk), F32)
            carry = (tots[0][g * group:(g + 1) * group], tots[1][g * group:(g + 1) * group], zeros_qk, zeros_qk,
                     jnp.zeros((group, LANES), F32))
            carry = lax.fori_loop(
                first, n_full,
                lambda t, c, rows=rows: trip(pl.multiple_of(t * width, width), nb, rows, c, None), carry)
            dq_groups.append(trip(pl.multiple_of(i * tq, tq), (g + 1) * group // tk, rows, carry, -g * group)[4])
        dq_ref[...] = jnp.concatenate(dq_groups, axis=0).astype(BF16)

        @pl.when(i == pl.num_programs(1) - 1)
        def _():
            dk_out[...] = dk_ref[...].astype(BF16)
            dv_out[...] = dv_ref[...].astype(BF16)

        pl.when((p == pairs - 1) & (i == pl.num_programs(1) - 1))(functools.partial(_finish_copies, copies))

    blk = lambda: pl.BlockSpec((tq, LANES), lambda p, i: (i, p))
    col = lambda: pl.BlockSpec((s, LANES), lambda p, i: (0, p))
    n_peers = len(DEVICE_FLIPS)
    nq = s // tq
    conv_rows = s // (pairs * nq)
    conv_blk = lambda w: pl.BlockSpec((conv_rows, w), lambda p, i: (p * nq + i, 0))
    halo_blk = pl.BlockSpec((SUBLANES, GATE_COLS),
                            lambda p, i: (jnp.maximum((p * nq + i) * (conv_rows // SUBLANES) - 1, 0), 0))
    return pl.pallas_call(
        body, name="attn_bwd", grid=(pairs, nq),
        in_specs=[pl.BlockSpec((tq, LANES), lambda p, i: (i, p)),
                  pl.BlockSpec((s, LANES), lambda p, i: (0, pairs + p)),
                  pl.BlockSpec((s, LANES), lambda p, i: (0, 2 * pairs + p)),
                  blk(), blk(), blk(),
                  pl.BlockSpec((1, LANES), lambda p, i: (0, p)),
                  pl.BlockSpec((2, 2 * tk, 2 * tk), lambda p, i: (0, 0, 0)),
                  pl.BlockSpec((2 * LANES, LANES), lambda p, i: (0, 0)),
                  pl.BlockSpec(memory_space=pltpu.SMEM),
                  conv_blk(GATE_COLS), halo_blk, conv_blk(CONV_WIDTH),
                  pl.BlockSpec((SUBLANES, CONV_WIDTH), lambda p, i: (0, 0)),
                  pl.BlockSpec((1, CONV_WIDTH), lambda p, i: (0, 0))] + [ANY] * n_g,
        out_specs=[blk(), col(), col(), pl.BlockSpec((SUBLANES, LANES), lambda p, i: (0, p)),
                   conv_blk(CONV_WIDTH), conv_blk(CONV_WIDTH),
                   pl.BlockSpec((SUBLANES, CONV_WIDTH), lambda p, i: (0, 0))] + [ANY] * n_g,
        out_shape=[SDS((s, ATTN_WIDTH), BF16), SDS((s, ATTN_WIDTH), BF16), SDS((s, ATTN_WIDTH), BF16),
                   SDS((SUBLANES, ATTN_WIDTH), F32),
                   SDS((s, CONV_WIDTH), BF16), SDS((s, CONV_WIDTH), F32), SDS((SUBLANES, CONV_WIDTH), F32)]
        + [SDS((n_peers, g.shape[1] // 2, g.shape[2]), g.dtype) for g in partials],
        scratch_shapes=[pltpu.SemaphoreType.DMA((n_peers * n_g,)), pltpu.SemaphoreType.DMA((n_peers * n_g,)),
                        pltpu.VMEM((s, LANES), F32), pltpu.VMEM((s, LANES), F32)],
        compiler_params=_params(("arbitrary", "arbitrary")))(
            qkv, qkv, qkv, o, tot, dyn, g_attn, tri, gmat, cut, gates, gates, dycn, taps, g_conv, *partials)


def _mix_ln1(ycn, yan, w_out, x, g, b):
    s = x.shape[0]
    tm = _row_tile(s, ROW_TILE)

    def body(yc_ref, ya_ref, w_ref, x_ref, g_ref, b_ref, xhat_ref, rstd_ref, x1b_ref):
        mix = jnp.dot(yc_ref[...], w_ref[0:CONV_WIDTH, :], preferred_element_type=F32)
        mix = mix + jnp.dot(ya_ref[...], w_ref[CONV_WIDTH:, :], preferred_element_type=F32)
        x1, xhat, rstd = _layer_norm_fwd(ALPHA * x_ref[...] + mix, g_ref[...], b_ref[...])
        xhat_ref[...] = xhat
        rstd_ref[...] = rstd
        x1b_ref[...] = x1.astype(BF16)

    row = lambda w: pl.BlockSpec((tm, w), lambda i: (i, 0))
    vec = lambda: pl.BlockSpec((1, D_MODEL), lambda i: (0, 0))
    return pl.pallas_call(
        body, name="mix_ln1", grid=(s // tm,),
        in_specs=[row(CONV_WIDTH), row(ATTN_WIDTH), pl.BlockSpec((D_MODEL, D_MODEL), lambda i: (0, 0)),
                  row(D_MODEL), vec(), vec()],
        out_specs=[row(D_MODEL), row(1), row(D_MODEL)],
        out_shape=[SDS((s, D_MODEL), F32), SDS((s, 1), F32), SDS((s, D_MODEL), BF16)],
        compiler_params=_params(("parallel",)))(ycn, yan, w_out, x, g, b)


def _mlp_fwd_loss(xhat1, g1, b1, w_up, w_down, target, g, b):
    s = xhat1.shape[0]
    tm = _row_tile(s, MLP_ROW_TILE)

    def body(xh_ref, g1_ref, b1_ref, wu_ref, wd_ref, t_ref, g_ref, b_ref,
             dpre_ref, sums_ref, loss_ref, r_ref, hid_ref, dpreb_ref):
        i = pl.program_id(0)
        x1_v = xh_ref[...] * g1_ref[...] + b1_ref[...]
        xb = x1_v.astype(BF16)
        ffn = jnp.zeros((tm, D_MODEL), F32)
        for k in range(N_CHIPS):
            r = jnp.maximum(jnp.dot(xb, wu_ref[k], preferred_element_type=F32), 0.0)
            hid = (r * r).astype(BF16)
            r_ref[:, FF_SHARD * k:FF_SHARD * (k + 1)] = r.astype(BF16)
            hid_ref[:, FF_SHARD * k:FF_SHARD * (k + 1)] = hid
            ffn = ffn + jnp.dot(hid, wd_ref[k], preferred_element_type=F32)
        g_v = g_ref[...]
        x2, xhat, rstd = _layer_norm_fwd(ALPHA * x1_v + ffn, g_v, b_ref[...])
        err = x2 - t_ref[...]
        dx2 = err * (1.0 / D_MODEL)
        dpre = _layer_norm_bwd(dx2, xhat, rstd, g_v)
        dpre_ref[...] = dpre
        dpreb_ref[...] = dpre.astype(BF16)

        @pl.when(i == 0)
        def _():
            sums_ref[...] = jnp.zeros_like(sums_ref)
            loss_ref[...] = jnp.zeros_like(loss_ref)

        sums_ref[0:1, :] += jnp.sum(dx2 * xhat, axis=0, keepdims=True)
        sums_ref[1:2, :] += jnp.sum(dx2, axis=0, keepdims=True)
        loss_ref[...] += jnp.sum(jnp.sum(err * err, axis=1, keepdims=True), axis=0, keepdims=True) * (0.5 / D_MODEL)

    row = lambda: pl.BlockSpec((tm, D_MODEL), lambda i: (i, 0))
    wide = lambda: pl.BlockSpec((tm, D_FF), lambda i: (i, 0))
    vec = lambda: pl.BlockSpec((1, D_MODEL), lambda i: (0, 0))
    return pl.pallas_call(
        body, name="mlp_fwd_loss", grid=(s // tm,),
        in_specs=[row(), vec(), vec(), _resident_weight(), _resident_weight(), row(), vec(), vec()],
        out_specs=[row(), pl.BlockSpec((SUBLANES, D_MODEL), lambda i: (0, 0)),
                   pl.BlockSpec((SUBLANES, LANES), lambda i: (0, 0)), wide(), wide(), row()],
        out_shape=[SDS((s, D_MODEL), F32), SDS((SUBLANES, D_MODEL), F32), SDS((SUBLANES, LANES), F32),
                   SDS((s, D_FF), BF16), SDS((s, D_FF), BF16), SDS((s, D_MODEL), BF16)],
        compiler_params=_params(("arbitrary",)))(xhat1, g1, b1, w_up, w_down, target, g, b)


def _resident_weight():
    return pl.BlockSpec((N_CHIPS, D_MODEL, FF_SHARD), lambda i: (0, 0, 0), pipeline_mode=pl.Buffered(1))


def _mlp_bwd_ln1(relu_up, dpre2, w_up, w_down, xhat1, rstd1, g1, w_out):
    s = dpre2.shape[0]
    tm = _row_tile(s, MLP_ROW_TILE)

    def body(r_ref, d2_ref, wu_ref, wd_ref, xh_ref, rs_ref, g_ref, wo_ref,
             dup_ref, dpre_ref, sums_ref, dpreb_ref, dyc_ref, dya_ref):
        i = pl.program_id(0)
        d2 = d2_ref[...]
        d2b = d2.astype(BF16)
        dx1 = ALPHA * d2
        for k in range(N_CHIPS):
            r = r_ref[:, FF_SHARD * k:FF_SHARD * (k + 1)].astype(F32)
            dhid = lax.dot_general(d2b, wd_ref[k], NT_DIMS, preferred_element_type=F32)
            dupb = (dhid * (2.0 * r)).astype(BF16)
            dup_ref[:, FF_SHARD * k:FF_SHARD * (k + 1)] = dupb
            dx1 = dx1 + lax.dot_general(dupb, wu_ref[k], NT_DIMS, preferred_element_type=F32)
        xhat = xh_ref[...]
        dpre = _layer_norm_bwd(dx1, xhat, rs_ref[...], g_ref[...])
        dpre_ref[...] = dpre
        dpb = dpre.astype(BF16)
        dpreb_ref[...] = dpb
        dyc_ref[...] = lax.dot_general(dpb, wo_ref[0:CONV_WIDTH, :], NT_DIMS, preferred_element_type=F32)
        dya_ref[...] = lax.dot_general(dpb, wo_ref[CONV_WIDTH:, :], NT_DIMS, preferred_element_type=F32)

        @pl.when(i == 0)
        def _():
            sums_ref[...] = jnp.zeros_like(sums_ref)

        sums_ref[0:1, :] += jnp.sum(dx1 * xhat, axis=0, keepdims=True)
        sums_ref[1:2, :] += jnp.sum(dx1, axis=0, keepdims=True)

    row = lambda w: pl.BlockSpec((tm, w), lambda i: (i, 0))
    return pl.pallas_call(
        body, name="mlp_bwd_ln1", grid=(s // tm,),
        in_specs=[row(D_FF), row(D_MODEL), _resident_weight(), _resident_weight(), row(D_MODEL), row(1),
                  pl.BlockSpec((1, D_MODEL), lambda i: (0, 0)),
                  pl.BlockSpec((D_MODEL, D_MODEL), lambda i: (0, 0), pipeline_mode=pl.Buffered(1))],
        out_specs=[row(D_FF), row(D_MODEL), pl.BlockSpec((SUBLANES, D_MODEL), lambda i: (0, 0)), row(D_MODEL),
                   row(CONV_WIDTH), row(ATTN_WIDTH)],
        out_shape=[SDS((s, D_FF), BF16), SDS((s, D_MODEL), F32), SDS((SUBLANES, D_MODEL), F32),
                   SDS((s, D_MODEL), BF16), SDS((s, CONV_WIDTH), F32), SDS((s, ATTN_WIDTH), F32)],
        compiler_params=_params(("arbitrary",)))(relu_up, dpre2, w_up, w_down, xhat1, rstd1, g1, w_out)


def _grad_tn(a, b, name, out_cols, stacked):
    s, ka = a.shape
    n = b.shape[1]
    ts = _row_tile(s, GRAD_SEQ_TILE)
    n_steps = s // ts
    if stacked:
        tka, tn = ka, out_cols
        grid = (1, n // tn, n_steps)
        shape = (n // tn, ka, tn)
        out_spec = lambda: pl.BlockSpec((None, tka, tn), lambda r, c, t: (c, 0, 0))
    else:
        tka, tn = min(ka, 1024), n
        grid = (ka // tka, 1, n_steps)
        shape = (ka, n)
        out_spec = lambda: pl.BlockSpec((tka, tn), lambda r, c, t: (r, 0))

    def body(a_ref, b_ref, o_ref, ob_ref):
        t = pl.program_id(2)

        @pl.when(t == 0)
        def _():
            o_ref[...] = jnp.zeros_like(o_ref)

        o_ref[...] += lax.dot_general(a_ref[...].astype(BF16), b_ref[...].astype(BF16), TN_DIMS,
                                      preferred_element_type=F32)

        @pl.when(t == n_steps - 1)
        def _():
            ob_ref[...] = o_ref[...].astype(BF16)

    return pl.pallas_call(
        body, name=name, grid=grid,
        in_specs=[pl.BlockSpec((ts, tka), lambda r, c, t: (t, r)),
                  pl.BlockSpec((ts, tn), lambda r, c, t: (t, c))],
        out_specs=[out_spec(), out_spec()], out_shape=[SDS(shape, F32), SDS(shape, BF16)],
        compiler_params=_params(("parallel", "parallel", "arbitrary")))(a, b)


def _grad_w_out(ycn, yan, dpre1):
    s = dpre1.shape[0]
    ts = _row_tile(s, GRAD_SEQ_TILE)
    n_steps = s // ts

    def body(yc_ref, ya_ref, d_ref, o_ref, ob_ref):
        half, t = pl.program_id(0), pl.program_id(1)

        @pl.when(t == 0)
        def _():
            o_ref[...] = jnp.zeros_like(o_ref)

        db = d_ref[...]

        @pl.when(half == 0)
        def _():
            o_ref[...] += lax.dot_general(yc_ref[...], db, TN_DIMS, preferred_element_type=F32)

        @pl.when(half == 1)
        def _():
            o_ref[...] += lax.dot_general(ya_ref[...], db, TN_DIMS, preferred_element_type=F32)

        @pl.when(t == n_steps - 1)
        def _():
            ob_ref[...] = o_ref[...].astype(BF16)

    out_spec = lambda: pl.BlockSpec((CONV_WIDTH, D_MODEL), lambda r, t: (r, 0))
    return pl.pallas_call(
        body, name="grad_w_out", grid=(2, n_steps),
        in_specs=[pl.BlockSpec((ts, CONV_WIDTH), lambda r, t: (t, 0)),
                  pl.BlockSpec((ts, ATTN_WIDTH), lambda r, t: (t, 0)),
                  pl.BlockSpec((ts, D_MODEL), lambda r, t: (t, 0))],
        out_specs=[out_spec(), out_spec()],
        out_shape=[SDS((D_MODEL, D_MODEL), F32), SDS((D_MODEL, D_MODEL), BF16)],
        compiler_params=_params(("parallel", "arbitrary")))(ycn, yan, dpre1)


def _sum_with_peers(own_ref, r_ref, o_ref):
    acc = own_ref[...]
    for f in range(r_ref.shape[0]):
        acc = acc + r_ref[f].astype(F32)
    o_ref[...] = acc


def _grad_x(kc_idx, dproj, w_in, dpre1, chip_sums, earlier):
    s = dproj.shape[0]
    tm = _row_tile(s, ROW_TILE)
    steps = s // tm
    n_peers = len(DEVICE_FLIPS)
    n_chips = len(CHIP_FLIPS)
    n_e = len(earlier)

    def body(kc_ref, dp_ref, w_ref, d1_ref, *rest):
        sum_ins, g_in = rest[:2 * n_e], rest[2 * n_e]
        o_ref, g_out = rest[2 * n_e + 1], rest[2 * n_e + 2]
        sum_outs, sems = rest[2 * n_e + 3:3 * n_e + 3], rest[3 * n_e + 3:]
        copies = functools.partial(_chip_reduce_copies, g_in, g_out, sems)
        i = pl.program_id(0)
        pl.when(i == 0)(functools.partial(_start_copies, copies))
        acc = ALPHA * d1_ref[...]
        for k in range(N_CHIPS):
            acc = acc + lax.dot_general(dp_ref[:, IN_SHARD * k:IN_SHARD * (k + 1)], w_ref[k], NT_DIMS,
                                        preferred_element_type=F32)
        o_ref[...] = acc
        for a in range(n_e):
            _sum_with_peers(sum_ins[2 * a], sum_ins[2 * a + 1], sum_outs[a])
        pl.when(i == steps - 1)(functools.partial(_finish_copies, copies))

    in_specs = [pl.BlockSpec((tm, IN_COLS), lambda i, kc: (i, 0)),
                pl.BlockSpec((N_CHIPS, D_MODEL, IN_SHARD), lambda i, kc: (0, 0, 0)),
                pl.BlockSpec((tm, D_MODEL), lambda i, kc: (i, 0))]
    out_specs = [pl.BlockSpec((tm, D_MODEL), lambda i, kc: (i, 0)), ANY]
    out_shape = [SDS((s, D_MODEL), F32), SDS((n_chips,) + chip_sums.shape[1:], chip_sums.dtype)]
    operands = []
    for own, recv in earlier:
        _, _, h, cols = own.shape
        th = h // steps
        in_specs.append(pl.BlockSpec((None, None, th, cols), lambda i, kc: (kc[0], kc[1], i, 0)))
        in_specs.append(pl.BlockSpec((n_peers, th, cols), lambda i, kc: (0, i, 0)))
        out_specs.append(pl.BlockSpec((th, cols), lambda i, kc: (kc[1] * steps + i, 0)))
        out_shape.append(SDS((2 * h, cols), F32))
        operands += [own, recv]
    grid_spec = pltpu.PrefetchScalarGridSpec(
        num_scalar_prefetch=1, grid=(steps,), in_specs=in_specs + [ANY], out_specs=out_specs,
        scratch_shapes=[pltpu.SemaphoreType.DMA((n_chips,)), pltpu.SemaphoreType.DMA((n_chips,))])
    return pl.pallas_call(
        body, name="grad_x", grid_spec=grid_spec, out_shape=out_shape,
        compiler_params=_params(("arbitrary",)))(kc_idx, dproj, w_in, dpre1, *operands, chip_sums)


def _adamw_step(w, g, m, v):
    nm = ADAM_B1 * m + (1.0 - ADAM_B1) * g
    nv = ADAM_B2 * v + (1.0 - ADAM_B2) * (g * g)
    m_hat = nm / (1.0 - ADAM_B1 ** ADAM_STEP)
    v_hat = nv / (1.0 - ADAM_B2 ** ADAM_STEP)
    return -ADAM_LR * (m_hat / (jnp.sqrt(v_hat) + ADAM_EPS) + ADAM_WD * w), nm, nv


def _adamw_shards(weights, grads, moments, variances):
    n = len(weights)
    steps = SHARD_STEPS

    def body(*refs):
        w, g, m, v, outs = refs[:n], refs[n:2 * n], refs[2 * n:3 * n], refs[3 * n:4 * n], refs[4 * n:]
        for a in range(n):
            g_v = g[a][...]
            outs[4 * a][...] = g_v
            outs[4 * a + 1][...], outs[4 * a + 2][...], outs[4 * a + 3][...] = _adamw_step(
                w[a][...], g_v, m[a][...], v[a][...])

    spec = lambda arr: pl.BlockSpec((arr.shape[0] // steps, arr.shape[1]), lambda i: (i, 0))
    flat = pl.pallas_call(
        body, name="adamw_shards", grid=(steps,),
        in_specs=[spec(a) for a in weights] * 4, out_specs=[spec(a) for a in weights for _ in range(4)],
        out_shape=[SDS(a.shape, F32) for a in weights for _ in range(4)],
        compiler_params=_params(("parallel",)))(*weights, *grads, *moments, *variances)
    return [flat[4 * a:4 * a + 4] for a in range(n)]


def _adamw_small(total, conv_grad, weights, moments, variances):
    n = len(weights)
    starts = (ROW_GCONV, ROW_GATTN, ROW_LN1G, ROW_LN1B, ROW_LN2G, ROW_LN2B)

    def body(total_ref, cg_ref, *refs):
        w, m, v, outs = refs[:n], refs[n:2 * n], refs[2 * n:3 * n], refs[3 * n:]
        for p in range(n):
            rows = w[p].shape[0]
            g = cg_ref[...] if p == n - 1 else total_ref[starts[p]:starts[p] + rows, :]
            outs[4 * p][...] = g
            outs[4 * p + 1][...], outs[4 * p + 2][...], outs[4 * p + 3][...] = _adamw_step(
                w[p][...], g, m[p][...], v[p][...])

    vmem = pl.BlockSpec(memory_space=pltpu.VMEM)
    out_shape = [SDS(w.shape, F32) for w in weights for _ in range(4)]
    flat = pl.pallas_call(
        body, name="adamw_small", in_specs=[vmem] * (2 + 3 * n), out_specs=[vmem] * (4 * n),
        out_shape=out_shape)(total, conv_grad, *weights, *moments, *variances)
    return [flat[4 * p:4 * p + 4] for p in range(n)]


def _sum_partials(kc_idx, own, recv, name):
    h, cols = own.shape
    th = _row_tile(h, SUM_ROW_TILE)
    n_peers = recv.shape[0]

    def body(kc_ref, own_ref, r_ref, o_ref):
        _sum_with_peers(own_ref, r_ref, o_ref)

    grid_spec = pltpu.PrefetchScalarGridSpec(
        num_scalar_prefetch=1, grid=(h // th,),
        in_specs=[pl.BlockSpec((th, cols), lambda t, kc: (t, 0)),
                  pl.BlockSpec((n_peers, th, cols), lambda t, kc: (0, t, 0))],
        out_specs=pl.BlockSpec((th, cols), lambda t, kc: (kc[1] * (h // th) + t, 0)))
    return pl.pallas_call(
        body, name=name, grid_spec=grid_spec, out_shape=SDS((2 * h, cols), F32),
        compiler_params=_params(("parallel",)))(kc_idx, own, recv)


def _gather_weights(kc_idx, w_in_slots, conv_slots, later):
    n_l = len(later)
    steps = SHARD_STEPS
    n_sems = 6 + len(CHIP_FLIPS)

    def body(kc_ref, *refs):
        cast_ins, cast_outs = refs[:n_l], refs[n_l + 2:2 * n_l + 2]
        w_buf, conv_buf = refs[2 * n_l + 2], refs[2 * n_l + 3]
        sems = refs[2 * n_l + 4:]
        i = pl.program_id(0)

        def first_hop():
            x, y, c = _position()
            sends, arrivals = _gather_chip_hop([w_buf], sems)
            mine = conv_buf.at[2 * x + y]
            for j, (fx, fy) in enumerate(CHIP_FLIPS):
                tx, ty = _flip(x, fx), _flip(y, fy)
                there = conv_buf.at[2 * tx + ty]
                sends.append(_remote_copy(mine, mine, sems, 6 + j, (tx, ty, c)))
                arrivals.append(_remote_copy(there, there, sems, 6 + j, (tx, ty, c)))
            return sends, arrivals

        pl.when(i == 0)(functools.partial(_start_copies, first_hop))
        for src, dst in zip(cast_ins, cast_outs):
            dst[...] = src[...].astype(BF16)

        @pl.when(i == steps - 1)
        def _():
            sends, arrivals = first_hop()
            for cp in arrivals:
                cp.wait_recv()
            _start_copies(functools.partial(_gather_sibling_hop, [w_buf], sems))
            _finish_copies(functools.partial(_gather_sibling_hop, [w_buf], sems))
            for cp in sends:
                cp.wait_send()

    in_specs, out_specs, out_shape = [], [], []
    for w in later:
        r, c = w.shape
        in_specs.append(pl.BlockSpec((r // steps, c), lambda i, kc: (i, 0)))
        out_specs.append(pl.BlockSpec((None, r // steps, c), lambda i, kc: (kc[0], i, 0)))
        out_shape.append(SDS((N_CHIPS, r, c), BF16))
    grid_spec = pltpu.PrefetchScalarGridSpec(
        num_scalar_prefetch=1, grid=(steps,), in_specs=in_specs + [ANY, ANY], out_specs=out_specs + [ANY, ANY],
        scratch_shapes=[pltpu.SemaphoreType.DMA((n_sems,)), pltpu.SemaphoreType.DMA((n_sems,))])
    return pl.pallas_call(
        body, name="gather_weights", grid_spec=grid_spec,
        out_shape=out_shape + [SDS(w_in_slots.shape, w_in_slots.dtype), SDS(conv_slots.shape, conv_slots.dtype)],
        input_output_aliases={n_l + 1: n_l, n_l + 2: n_l + 1},
        compiler_params=_params(("arbitrary",)))(kc_idx, *later, w_in_slots, conv_slots)


def _sibling_sums(grad, partial):
    h, cols = grad.shape[1] // 2, grad.shape[2]

    def body(g_ref, p_ref, recv_ref, sums_ref, own_ref, g_buf, r_buf, send_sems, recv_sems, load_sems):
        x, y, c = _position()
        theirs = pl.ds(pl.multiple_of((1 - c) * h, h), h)
        mine = pl.ds(pl.multiple_of(c * h, h), h)
        sends = [_remote_copy(p_ref.at[k, theirs], recv_ref.at[k], (send_sems, recv_sems), k, (x, y, 1 - c))
                 for k in range(N_CHIPS)]
        own_loads = [pltpu.make_async_copy(g_ref.at[k, mine], g_buf.at[k], load_sems.at[k])
                     for k in range(N_CHIPS)]
        recv_loads = [pltpu.make_async_copy(recv_ref.at[k], r_buf.at[k], load_sems.at[N_CHIPS + k])
                      for k in range(N_CHIPS)]
        for cp in sends + own_loads:
            cp.start()
        for k in range(N_CHIPS):
            sends[k].wait_recv()
            recv_loads[k].start()
            own_loads[k].wait()
            recv_loads[k].wait()
            total = g_buf[k] + r_buf[k].astype(F32)
            sums_ref[k] = total.astype(BF16)

            @pl.when(2 * x + y == k)
            def _():
                own_ref[...] = total

        for cp in sends:
            cp.wait_send()

    vmem = lambda: pl.BlockSpec(memory_space=pltpu.VMEM)
    return pl.pallas_call(
        body, name="sibling_sums_w_in", in_specs=[ANY, ANY], out_specs=[ANY, vmem(), vmem()],
        out_shape=[SDS((N_CHIPS, h, cols), BF16), SDS((N_CHIPS, h, cols), BF16), SDS((h, cols), F32)],
        scratch_shapes=[pltpu.VMEM((N_CHIPS, h, cols), F32), pltpu.VMEM((N_CHIPS, h, cols), BF16),
                        pltpu.SemaphoreType.DMA((N_CHIPS,)), pltpu.SemaphoreType.DMA((N_CHIPS,)),
                        pltpu.SemaphoreType.DMA((2 * N_CHIPS,))],
        compiler_params=_params())(grad, partial)


def _finish_exchange(pieces, vec):
    n = len(pieces)
    n_dev = 2 * N_CHIPS

    def body(*refs):
        v_ref = refs[n]
        outs, o_ref = refs[n + 1:2 * n + 1], refs[2 * n + 1]
        buf, send_sems, recv_sems = refs[2 * n + 2:]
        x, y, c = _position()
        sibling = (x, y, 1 - c)
        me = 4 * x + 2 * y + c
        buf[me] = v_ref[...]
        started = []
        for f, (fx, fy, fc) in enumerate(DEVICE_FLIPS):
            cp = pltpu.make_async_remote_copy(
                src_ref=v_ref, dst_ref=buf.at[me], send_sem=send_sems.at[n + f], recv_sem=recv_sems.at[n + f],
                device_id=(_flip(x, fx), _flip(y, fy), _flip(c, fc)), device_id_type=MESH)
            cp.start()
            started.append(cp)
        for a in range(n):
            h = pieces[a].shape[0] // 2
            mine = outs[a].at[pl.ds(pl.multiple_of(c * h, h), h)]
            cp = pltpu.make_async_remote_copy(
                src_ref=mine, dst_ref=mine, send_sem=send_sems.at[a], recv_sem=recv_sems.at[a],
                device_id=sibling, device_id_type=MESH)
            cp.start()
            started.append(cp)
        for a in range(n):
            h = pieces[a].shape[0] // 2
            theirs = outs[a].at[pl.ds(pl.multiple_of((1 - c) * h, h), h)]
            pltpu.make_async_remote_copy(
                src_ref=theirs, dst_ref=theirs, send_sem=send_sems.at[a], recv_sem=recv_sems.at[a],
                device_id=sibling, device_id_type=MESH).wait_recv()
        for f, (fx, fy, fc) in enumerate(DEVICE_FLIPS):
            src = 4 * _flip(x, fx) + 2 * _flip(y, fy) + _flip(c, fc)
            pltpu.make_async_remote_copy(
                src_ref=v_ref, dst_ref=buf.at[src], send_sem=send_sems.at[n + f], recv_sem=recv_sems.at[n + f],
                device_id=(x, y, c), device_id_type=MESH).wait_recv()
        for cp in started:
            cp.wait_send()
        acc = buf[0]
        for d in range(1, n_dev):
            acc = acc + buf[d]
        o_ref[...] = acc

    vmem = pl.BlockSpec(memory_space=pltpu.VMEM)
    out_shape = [SDS(p.shape, p.dtype) for p in pieces] + [SDS(vec.shape, vec.dtype)]
    n_sems = n + n_dev - 1
    return pl.pallas_call(
        body, name="finish_exchange", in_specs=[ANY] * n + [vmem], out_specs=[ANY] * n + [vmem],
        out_shape=out_shape, input_output_aliases={a: a for a in range(n)},
        scratch_shapes=[pltpu.VMEM((n_dev,) + vec.shape, vec.dtype), pltpu.SemaphoreType.DMA((n_sems,)),
                        pltpu.SemaphoreType.DMA((n_sems,))])(*pieces, vec)


def _constants():
    r = jnp.arange(2 * KEY_BLOCK)[:, None] % KEY_BLOCK
    c = jnp.arange(2 * KEY_BLOCK)[None, :]
    later = jnp.where(c < KEY_BLOCK, r > c, True).astype(BF16)
    earlier = jnp.where(c < KEY_BLOCK, r < c, True).astype(BF16)
    upto = jnp.where(c < KEY_BLOCK, r <= c, True).astype(BF16)
    gr = (jnp.arange(2 * LANES)[:, None] % LANES) // GROUP
    gc = jnp.arange(LANES)[None, :] // GROUP
    gmat = (gr == gc).astype(BF16)
    return later, jnp.stack([earlier, upto]), gmat


def _rows(v):
    return v.reshape(-1, LANES)


def kernel(x, w_in, conv_w, g_conv, g_attn, w_out, ln1_g, ln1_b, w_up, w_down, ln2_g, ln2_b, loss_target, m_w_in, m_conv_w, m_g_conv, m_g_attn, m_w_out, m_ln1_g, m_ln1_b, m_w_up, m_w_down, m_ln2_g, m_ln2_b, v_w_in, v_conv_w, v_g_conv, v_g_attn, v_w_out, v_ln1_g, v_ln1_b, v_w_up, v_w_down, v_ln2_g, v_ln2_b):
    xs, target = x[0], loss_target[0]
    mesh_x, mesh_y, mesh_c = _position()
    k_idx = 2 * mesh_x + mesh_y
    kc_idx = jnp.stack([k_idx, mesh_c]).astype(jnp.int32)
    tri_later, tri_earlier, gmat = _constants()

    w_in_b = _cast_into_slot(kc_idx, w_in[0], "cast_w_in")
    conv_slot = jnp.pad(conv_w, ((0, 0), (0, SUBLANES - conv_w.shape[1]), (0, 0)))
    conv_b = lax.dynamic_update_slice(jnp.zeros((N_CHIPS, SUBLANES, LANES), F32), conv_slot, (k_idx, 0, 0))
    w_out_b, w_up_b, w_down_b, w_in_f, conv_f = _gather_weights(
        kc_idx, w_in_b, conv_b, [w_out[0], w_up[0], w_down[0]])
    taps = jnp.transpose(conv_f, (1, 0, 2)).reshape(SUBLANES, CONV_WIDTH)

    gates, qkv, xs_b, ycn = _proj(xs, w_in_f, taps, g_conv, gmat)
    o, yan, tot, cut, w_out_f, w_up_f, w_down_f = _attn_fwd(
        qkv, g_attn, tri_later, gmat, [w_out_b, w_up_b, w_down_b])
    w_out_f = w_out_f.reshape(D_MODEL, D_MODEL)
    xhat1, rstd1, x1_b = _mix_ln1(ycn, yan, w_out_f, xs, ln1_g, ln1_b)
    dpre2, ln2_sums, loss_sum, relu_up, hid, dpre2_b = _mlp_fwd_loss(
        xhat1, ln1_g, ln1_b, w_up_f, w_down_f, target, ln2_g, ln2_b)

    dup, dpre1, ln1_sums, dpre1_b, dycn, dyan = _mlp_bwd_ln1(
        relu_up, dpre2, w_up_f, w_down_f, xhat1, rstd1, ln1_g, w_out_f)
    gw_up = _grad_tn(x1_b, dup, "grad_w_up", FF_SHARD, True)
    gw_down = [g.reshape(N_CHIPS, FF_SHARD, D_MODEL) for g in _grad_tn(hid, dpre2_b, "grad_w_down", D_MODEL, False)]
    gw_out = [g.reshape(N_CHIPS, D_MODEL // N_CHIPS, D_MODEL) for g in _grad_w_out(ycn, yan, dpre1_b)]
    dq, dk, dv, gattn_sums, dbg, dy, conv_sums, recv_out, recv_up, recv_down = _attn_bwd(
        qkv, o, tot, dyan, g_attn, tri_earlier, gmat, cut, gates, dycn, taps, g_conv,
        [gw_out[1], gw_up[1], gw_down[1]])
    dproj = _dproj_assemble(gates, dy, dbg, dq, dk, dv, taps)
    gw_in = _grad_tn(xs_b, dproj, "grad_w_in", IN_SHARD, True)
    halves = lambda g: g.reshape(N_CHIPS, 2, g.shape[1] // 2, g.shape[2])
    _, chip_sums, own_sum = _sibling_sums(gw_in[0], gw_in[1])
    grad_x, recv_in, p_out, p_up, p_down = _grad_x(
        kc_idx, dproj, w_in_f, dpre1, chip_sums,
        [(halves(gw_out[0]), recv_out), (halves(gw_up[0]), recv_up), (halves(gw_down[0]), recv_down)])
    pieces = [_sum_partials(kc_idx, own_sum, recv_in, "sum_partials_w_in"), p_out, p_up, p_down]
    conv_rows = jnp.transpose(conv_sums[0:3].reshape(3, N_CHIPS, LANES), (1, 0, 2)).reshape(3 * N_CHIPS, LANES)
    small = jnp.concatenate([
        loss_sum, _rows(conv_sums[3]), _rows(gattn_sums[0]), _rows(ln1_sums[0]), _rows(ln1_sums[1]),
        _rows(ln2_sums[0]), _rows(ln2_sums[1]), conv_rows,
        jnp.zeros((SMALL_ROWS - ROW_CONVW - 3 * N_CHIPS, LANES), F32)], axis=0)
    g_w_in, g_w_out, g_w_up, g_w_down, total = _finish_exchange(pieces, small)
    loss = total[ROW_LOSS, 0]
    g_conv_w = lax.dynamic_slice(total, (ROW_CONVW + 3 * k_idx, 0), (3, LANES))

    small_names = ["g_conv", "g_attn", "ln1_g", "ln1_b", "ln2_g", "ln2_b", "conv_w"]
    small_w = [g_conv, g_attn, ln1_g, ln1_b, ln2_g, ln2_b, conv_w]
    small_m = [m_g_conv, m_g_attn, m_ln1_g, m_ln1_b, m_ln2_g, m_ln2_b, m_conv_w]
    small_v = [v_g_conv, v_g_attn, v_ln1_g, v_ln1_b, v_ln2_g, v_ln2_b, v_conv_w]
    small_out = dict(zip(small_names, _adamw_small(
        total, g_conv_w, [_rows(a) for a in small_w], [_rows(a) for a in small_m], [_rows(a) for a in small_v])))
    small_shape = dict(zip(small_names, (a.shape for a in small_w)))
    big_out = dict(zip(["w_in", "w_out", "w_up", "w_down"], _adamw_shards(
        [w_in[0], w_out[0], w_up[0], w_down[0]], [g_w_in, g_w_out, g_w_up, g_w_down],
        [m_w_in[0], m_w_out[0], m_w_up[0], m_w_down[0]], [v_w_in[0], v_w_out[0], v_w_up[0], v_w_down[0]])))
    order = ["w_in", "conv_w", "g_conv", "g_attn", "w_out", "ln1_g", "ln1_b", "w_up", "w_down", "ln2_g", "ln2_b"]

    def leaf(kind, name):
        if name in big_out:
            return big_out[name][kind][None]
        return small_out[name][kind].reshape(small_shape[name])

    outs = [loss, grad_x[None]]
    for kind in range(4):
        outs.extend(leaf(kind, name) for name in order)
    return tuple(outs)
```
